```python
import math
import jax, jax.numpy as jnp
from jax import lax
import numpy as np

D_MODEL = 2048
BATCH = 4
SEQ = 2048
DEPTH = 1
DEC_BATCH = 8
DEC_SEQ = 16
PAST_LEN = 2048

CHUNK = 64
D_CONV = D_MODEL
CONV_A_WIDTH = 3
D_INNER = 2 * D_MODEL
SSM_HEAD_DIM = 64
N_SSM_HEADS = D_INNER // SSM_HEAD_DIM
N_SSM_GROUPS = 8
HEADS_PER_GROUP = N_SSM_HEADS // N_SSM_GROUPS
D_STATE = 128
CONV_M_WIDTH = 4
CONV_M_DIM = D_INNER + 2 * N_SSM_GROUPS * D_STATE
D_PROJ = 3 * D_CONV + D_INNER + CONV_M_DIM + N_SSM_HEADS + 2 * D_MODEL
N_MEM = 256
N_XHEADS = 4
XHEAD_DIM = D_MODEL // N_XHEADS
N_EXPERT_GROUPS = 4
EXPERTS_PER_GROUP = 8
N_EXPERTS = N_EXPERT_GROUPS * EXPERTS_PER_GROUP
TOP_K = 2
D_EXPERT = D_MODEL // 2
EPS = 1e-6

kernel_name = 'hybrid_shortconv_ssd_hmoe_stream_step'


def rmsnorm(x, g):
    xf = x.astype(jnp.float32)
    xf = xf * lax.rsqrt(jnp.mean(xf * xf, axis=-1, keepdims=True) + EPS)
    return (xf * g.astype(jnp.float32)).astype(x.dtype)


def causal_conv(u, prev, w):
    K = w.shape[0]
    L = u.shape[1]
    full = jnp.concatenate([prev.astype(u.dtype), u], axis=1)
    out = full[:, 0:L] * w[0]
    for k in range(1, K):
        out = out + full[:, k:k + L] * w[k]
    return out, full[:, L:]


def ssd_scan(x, dt, a, bm, cm, h0):
    f32 = jnp.float32
    b, L = x.shape[0], x.shape[1]
    Q = min(CHUNK, L)
    nc = L // Q
    xc = x.astype(f32).reshape(b, nc, Q, N_SSM_GROUPS, HEADS_PER_GROUP, SSM_HEAD_DIM)
    dtc = dt.astype(f32).reshape(b, nc, Q, N_SSM_GROUPS, HEADS_PER_GROUP)
    bc = bm.astype(f32).reshape(b, nc, Q, N_SSM_GROUPS, D_STATE)
    cc = cm.astype(f32).reshape(b, nc, Q, N_SSM_GROUPS, D_STATE)
    a_cum = jnp.cumsum(dtc * a, axis=2)
    xdt = xc * dtc[..., None]
    causal = jnp.tril(jnp.ones((Q, Q), dtype=bool))
    seg = a_cum[:, :, :, None] - a_cum[:, :, None, :]
    decay_in = jnp.exp(jnp.where(causal[:, :, None, None], seg, -jnp.inf))
    cb = jnp.einsum('bcign,bcjgn->bcijg', cc, bc)
    y_diag = jnp.einsum('bcijgr,bcjgrp->bcigrp', cb[..., None] * decay_in, xdt)
    decay_end = jnp.exp(a_cum[:, :, -1:] - a_cum)
    chunk_states = jnp.einsum('bcjgn,bcjgrp->bcgrpn', bc, xdt * decay_end[..., None])
    chunk_decay = jnp.exp(a_cum[:, :, -1])

    def step(h, inp):
        dec, st = inp
        return h * dec[..., None, None] + st, h

    h_last, h_in = lax.scan(step, h0.astype(f32),
                            (jnp.moveaxis(chunk_decay, 1, 0), jnp.moveaxis(chunk_states, 1, 0)))
    h_in = jnp.moveaxis(h_in, 0, 1)
    y_off = jnp.einsum('bcign,bcgrpn->bcigrp', cc, h_in) * jnp.exp(a_cum)[..., None]
    y = (y_diag + y_off).reshape(b, L, N_SSM_GROUPS, HEADS_PER_GROUP, SSM_HEAD_DIM)
    return y, h_last


def gated_rmsnorm(y, z, g):
    b, L, _ = y.shape
    h = (y.astype(jnp.float32) * jax.nn.silu(z.astype(jnp.float32)))
    h = h.reshape(b, L, N_SSM_GROUPS, D_INNER // N_SSM_GROUPS)
    h = h * lax.rsqrt(jnp.mean(h * h, axis=-1, keepdims=True) + EPS)
    return (h.reshape(b, L, D_INNER) * g.astype(jnp.float32)).astype(y.dtype)


def token_mixer(xn, conv_a_prev, conv_m_prev, ssm_prev, w_in, conv_a_w, w_a_out, conv_m_w, conv_m_b,
                dt_bias, a_log, d_skip, ssm_norm, w_m_out, w_o):
    b, L, _ = xn.shape
    sizes = (D_CONV, D_CONV, D_CONV, D_INNER, CONV_M_DIM, N_SSM_HEADS, D_MODEL, D_MODEL)
    points = np.cumsum(sizes)[:-1].tolist()
    a_b, a_c, a_h, z, xbc, dt_raw, g_a, g_m = jnp.split(xn @ w_in, points, axis=-1)
    v, conv_a_new = causal_conv(a_c * a_h, conv_a_prev, conv_a_w)
    out_a = (a_b * v) @ w_a_out
    xbc_c, conv_m_new = causal_conv(xbc, conv_m_prev, conv_m_w)
    xbc_c = jax.nn.silu(xbc_c + conv_m_b)
    xs, bm, cm = jnp.split(xbc_c, [D_INNER, D_INNER + N_SSM_GROUPS * D_STATE], axis=-1)
    xs = xs.reshape(b, L, N_SSM_GROUPS, HEADS_PER_GROUP, SSM_HEAD_DIM)
    bm = bm.reshape(b, L, N_SSM_GROUPS, D_STATE)
    cm = cm.reshape(b, L, N_SSM_GROUPS, D_STATE)
    dt = jax.nn.softplus(dt_raw.astype(jnp.float32) + dt_bias.astype(jnp.float32))
    dt = dt.reshape(b, L, N_SSM_GROUPS, HEADS_PER_GROUP)
    a = -jnp.exp(a_log.astype(jnp.float32)).reshape(N_SSM_GROUPS, HEADS_PER_GROUP)
    h0 = ssm_prev.reshape(b, N_SSM_GROUPS, HEADS_PER_GROUP, SSM_HEAD_DIM, D_STATE)
    y, h_last = ssd_scan(xs, dt, a, bm, cm, h0)
    y = y + d_skip.astype(jnp.float32).reshape(N_SSM_GROUPS, HEADS_PER_GROUP)[:, :, None] * xs.astype(jnp.float32)
    y = gated_rmsnorm(y.reshape(b, L, D_INNER).astype(xn.dtype), z, ssm_norm)
    out_m = y @ w_m_out
    merged = jax.nn.sigmoid(g_a) * out_a + jax.nn.sigmoid(g_m) * out_m
    ssm_new = h_last.reshape(b, N_SSM_HEADS, SSM_HEAD_DIM, D_STATE).astype(xn.dtype)
    return merged @ w_o, conv_a_new, conv_m_new, ssm_new


def memory_kv(mem, g_mem, w_k, w_v):
    b, M, _ = mem.shape
    mn = rmsnorm(mem, g_mem)
    k = (mn @ w_k).reshape(b, M, N_XHEADS, XHEAD_DIM)
    v = (mn @ w_v).reshape(b, M, N_XHEADS, XHEAD_DIM)
    return k, v


def cross_attend(hn, k, v, w_q, w_co):
    b, L, _ = hn.shape
    q = (hn @ w_q).reshape(b, L, N_XHEADS, XHEAD_DIM)
    s = jnp.einsum('blhd,bmhd->bhlm', q, k.astype(q.dtype)).astype(jnp.float32) * (XHEAD_DIM ** -0.5)
    p = jax.nn.softmax(s, axis=-1).astype(q.dtype)
    o = jnp.einsum('bhlm,bmhd->blhd', p, v.astype(q.dtype)).reshape(b, L, D_MODEL)
    return o @ w_co


def hier_moe(xn, w_rg, b_rg, w_re, b_re, w_gate, w_up, w_down):
    b, L, D = xn.shape
    xt = xn.reshape(b * L, D)
    g_logits = (xt @ w_rg).astype(jnp.float32) + b_rg.astype(jnp.float32)
    g_prob = jax.nn.softmax(g_logits, axis=-1)
    g_sel = jnp.argmax(g_logits, axis=-1)
    p_g = jnp.max(g_prob, axis=-1)
    e_logits = ((xt @ w_re).astype(jnp.float32) + b_re.astype(jnp.float32))
    e_logits = e_logits.reshape(-1, N_EXPERT_GROUPS, EXPERTS_PER_GROUP)
    e_logits = jnp.einsum('tge,tg->te', e_logits, jax.nn.one_hot(g_sel, N_EXPERT_GROUPS, dtype=jnp.float32))
    top_p, top_i = lax.top_k(jax.nn.softmax(e_logits, axis=-1), TOP_K)
    top_w = p_g[:, None] * top_p / jnp.sum(top_p, axis=-1, keepdims=True)
    expert_id = g_sel[:, None] * EXPERTS_PER_GROUP + top_i
    combine = jnp.sum(jax.nn.one_hot(expert_id, N_EXPERTS, dtype=jnp.float32) * top_w[..., None], axis=1)
    combine = combine.astype(xt.dtype)
    out = jnp.zeros_like(xt)
    for e in range(N_EXPERTS):
        hidden = jax.nn.silu(xt @ w_gate[e]) * (xt @ w_up[e])
        out = out + combine[:, e:e + 1] * (hidden @ w_down[e])
    return out.reshape(b, L, D)


def setup_inputs(seed: int = 0) -> dict:
    key = jax.random.key(seed)
    ks = iter(jax.random.split(key, 48))
    f32 = jnp.float32

    def nrm(shape, scale):
        return scale * jax.random.normal(next(ks), shape, f32)

    def gain(shape):
        return 1.0 + 0.01 * jax.random.normal(next(ks), shape, f32)

    x_prompt = nrm((BATCH, SEQ, D_MODEL), 1.0)
    x_sample = nrm((DEC_BATCH, DEC_SEQ, D_MODEL), 1.0)
    cache_mem_k = nrm((DEPTH, DEC_BATCH, N_MEM, N_XHEADS, XHEAD_DIM), 1.0)
    cache_mem_v = nrm((DEPTH, DEC_BATCH, N_MEM, N_XHEADS, XHEAD_DIM), 1.0)
    state_conv_a = nrm((DEPTH, DEC_BATCH, CONV_A_WIDTH - 1, D_CONV), 1.0)
    state_conv_m = nrm((DEPTH, DEC_BATCH, CONV_M_WIDTH - 1, CONV_M_DIM), 1.0)
    state_ssm = nrm((DEPTH, DEC_BATCH, N_SSM_HEADS, SSM_HEAD_DIM, D_STATE), 0.5)
    mem_prompt = nrm((BATCH, N_MEM, D_MODEL), 1.0)
    norm_mix = gain((DEPTH, D_MODEL))
    w_in = nrm((DEPTH, D_MODEL, D_PROJ), D_MODEL ** -0.5)
    conv_a_w = nrm((DEPTH, CONV_A_WIDTH, D_CONV), CONV_A_WIDTH ** -0.5)
    w_a_out = nrm((DEPTH, D_CONV, D_MODEL), D_CONV ** -0.5)
    conv_m_w = nrm((DEPTH, CONV_M_WIDTH, CONV_M_DIM), CONV_M_WIDTH ** -0.5)
    conv_m_b = nrm((DEPTH, CONV_M_DIM), 0.01)
    dt0 = jnp.exp(jax.random.uniform(next(ks), (DEPTH, N_SSM_HEADS), f32, math.log(1e-3), math.log(1e-1)))
    dt_bias = dt0 + jnp.log(-jnp.expm1(-dt0))
    a_log = jnp.log(jax.random.uniform(next(ks), (DEPTH, N_SSM_HEADS), f32, 1.0, 16.0))
    d_skip = gain((DEPTH, N_SSM_HEADS))
    ssm_norm = gain((DEPTH, D_INNER))
    w_m_out = nrm((DEPTH, D_INNER, D_MODEL), D_INNER ** -0.5)
    w_o = nrm((DEPTH, D_MODEL, D_MODEL), D_MODEL ** -0.5)
    norm_cross = gain((DEPTH, D_MODEL))
    norm_mem = gain((DEPTH, D_MODEL))
    w_q = nrm((DEPTH, D_MODEL, D_MODEL), D_MODEL ** -0.5)
    w_k = nrm((DEPTH, D_MODEL, D_MODEL), D_MODEL ** -0.5)
    w_v = nrm((DEPTH, D_MODEL, D_MODEL), D_MODEL ** -0.5)
    w_co = nrm((DEPTH, D_MODEL, D_MODEL), D_MODEL ** -0.5)
    norm_ffn = gain((DEPTH, D_MODEL))
    w_rg = nrm((DEPTH, D_MODEL, N_EXPERT_GROUPS), D_MODEL ** -0.5)
    b_rg = nrm((DEPTH, N_EXPERT_GROUPS), 0.01)
    w_re = nrm((DEPTH, D_MODEL, N_EXPERTS), D_MODEL ** -0.5)
    b_re = nrm((DEPTH, N_EXPERTS), 0.01)
    w_gate = nrm((DEPTH, N_EXPERTS, D_MODEL, D_EXPERT), D_MODEL ** -0.5)
    w_up = nrm((DEPTH, N_EXPERTS, D_MODEL, D_EXPERT), D_MODEL ** -0.5)
    w_down = nrm((DEPTH, N_EXPERTS, D_EXPERT, D_MODEL), D_EXPERT ** -0.5)
    norm_final = gain((D_MODEL,))
    return {'x_prompt': x_prompt, 'x_sample': x_sample, 'cache_mem_k': cache_mem_k, 'cache_mem_v': cache_mem_v,
            'state_conv_a': state_conv_a, 'state_conv_m': state_conv_m, 'state_ssm': state_ssm,
            'mem_prompt': mem_prompt, 'norm_mix': norm_mix, 'w_in': w_in, 'conv_a_w': conv_a_w,
            'w_a_out': w_a_out, 'conv_m_w': conv_m_w, 'conv_m_b': conv_m_b, 'dt_bias': dt_bias,
            'a_log': a_log, 'd_skip': d_skip, 'ssm_norm': ssm_norm, 'w_m_out': w_m_out, 'w_o': w_o,
            'norm_cross': norm_cross, 'norm_mem': norm_mem, 'w_q': w_q, 'w_k': w_k, 'w_v': w_v, 'w_co': w_co,
            'norm_ffn': norm_ffn, 'w_rg': w_rg, 'b_rg': b_rg, 'w_re': w_re, 'b_re': b_re,
            'w_gate': w_gate, 'w_up': w_up, 'w_down': w_down, 'norm_final': norm_final}


def reference(x_prompt, x_sample, cache_mem_k, cache_mem_v, state_conv_a, state_conv_m, state_ssm, mem_prompt,
              norm_mix, w_in, conv_a_w, w_a_out, conv_m_w, conv_m_b, dt_bias, a_log, d_skip, ssm_norm,
              w_m_out, w_o, norm_cross, norm_mem, w_q, w_k, w_v, w_co, norm_ffn, w_rg, b_rg, w_re, b_re,
              w_gate, w_up, w_down, norm_final):

    def run_layer(l, x, mem_k, mem_v, conv_a_prev, conv_m_prev, ssm_prev):
        mix, conv_a_new, conv_m_new, ssm_new = token_mixer(
            rmsnorm(x, norm_mix[l]), conv_a_prev, conv_m_prev, ssm_prev, w_in[l], conv_a_w[l], w_a_out[l],
            conv_m_w[l], conv_m_b[l], dt_bias[l], a_log[l], d_skip[l], ssm_norm[l], w_m_out[l], w_o[l])
        h = x + mix
        h = h + cross_attend(rmsnorm(h, norm_cross[l]), mem_k, mem_v, w_q[l], w_co[l])
        h = h + hier_moe(rmsnorm(h, norm_ffn[l]), w_rg[l], b_rg[l], w_re[l], b_re[l],
                         w_gate[l], w_up[l], w_down[l])
        return h, conv_a_new, conv_m_new, ssm_new

    dtype = x_prompt.dtype
    xp = x_prompt
    mk_p, mv_p, ca_p, cm_p, ss_p = [], [], [], [], []
    for l in range(DEPTH):
        mk, mv = memory_kv(mem_prompt, norm_mem[l], w_k[l], w_v[l])
        ca0 = jnp.zeros((BATCH, CONV_A_WIDTH - 1, D_CONV), dtype)
        cm0 = jnp.zeros((BATCH, CONV_M_WIDTH - 1, CONV_M_DIM), dtype)
        ss0 = jnp.zeros((BATCH, N_SSM_HEADS, SSM_HEAD_DIM, D_STATE), dtype)
        xp, ca, cm, ss = run_layer(l, xp, mk, mv, ca0, cm0, ss0)
        mk_p.append(mk)
        mv_p.append(mv)
        ca_p.append(ca)
        cm_p.append(cm)
        ss_p.append(ss)
    y_prompt = rmsnorm(xp, norm_final)

    xs = x_sample
    ca_s, cm_s, ss_s = [], [], []
    for l in range(DEPTH):
        xs, ca, cm, ss = run_layer(l, xs, cache_mem_k[l], cache_mem_v[l],
                                   state_conv_a[l], state_conv_m[l], state_ssm[l])
        ca_s.append(ca)
        cm_s.append(cm)
        ss_s.append(ss)
    y_sample = rmsnorm(xs, norm_final)

    return (y_prompt, y_sample, jnp.stack(mk_p), jnp.stack(mv_p), jnp.stack(ca_p), jnp.stack(cm_p),
            jnp.stack(ss_p), jnp.stack(ca_s), jnp.stack(cm_s), jnp.stack(ss_s))
```

```python
import functools

import jax
import jax.numpy as jnp
from jax import lax
from jax.experimental import pallas as pl
from jax.experimental.pallas import tpu as pltpu

F32 = jnp.float32
BF16 = jnp.bfloat16
EPS = 1e-6

V7X_VMEM_BYTES = 64 * 1024 * 1024
VMEM_LIMIT = V7X_VMEM_BYTES - 8 * 1024 * 1024
LANES = 128
SUBLANES = 8

N_GROUPS = 8
HEADS_PER_GROUP = 8
HEAD_DIM = 64
D_STATE = 128
GROUP_W = HEADS_PER_GROUP * HEAD_DIM
N_XHEADS = 4
N_EXPERTS = 32
N_EXPERT_GROUPS = 4
EXPERTS_PER_GROUP = 8
CONV_A_K = 3
CONV_M_K = 4

NT_DIMS = (((1,), (1,)), ((), ()))
TN_DIMS = (((0,), (0,)), ((), ()))


def _params(sem):
    return pltpu.CompilerParams(dimension_semantics=sem, vmem_limit_bytes=VMEM_LIMIT)


def _pick(n, cands):
    for c in cands:
        if n % c == 0:
            return c
    raise ValueError(f"no tile for {n} in {cands}")


def _dot(a, b):
    return jnp.dot(a, b, preferred_element_type=F32)


def _rms(x, g):
    return x * lax.rsqrt(jnp.mean(x * x, axis=-1, keepdims=True) + EPS) * g


def _split3(x):
    hi = x.astype(BF16)
    r = x - hi.astype(F32)
    mid = r.astype(BF16)
    lo = (r - mid.astype(F32)).astype(BF16)
    return hi, mid, lo


def _softplus(x):
    return jnp.maximum(x, 0.0) + jnp.log1p(jnp.exp(-jnp.abs(x)))


def _norm_dt_kernel(xp_ref, xs_ref, g_ref, wdt_ref, xnp_ref, xns_ref, dtp_ref, dts_ref):
    wdt = wdt_ref[...].astype(BF16)

    def one(x_ref, xn_ref, dt_ref):
        xn = _rms(x_ref[...], g_ref[...]).astype(BF16)
        xn_ref[...] = xn
        dt_ref[...] = _dot(xn, wdt)

    one(xp_ref, xnp_ref, dtp_ref)

    @pl.when(pl.program_id(0) == 0)
    def _():
        one(xs_ref, xns_ref, dts_ref)


def _norm_dt(xp, xs, g, wdt):
    tp, d = xp.shape
    ts = xs.shape[0]
    nh = wdt.shape[1]
    tm = _pick(tp, (512, 256, 128))
    return pl.pallas_call(
        _norm_dt_kernel,
        grid=(tp // tm,),
        in_specs=[
            pl.BlockSpec((tm, d), lambda i: (i, 0)),
            pl.BlockSpec((ts, d), lambda i: (0, 0)),
            pl.BlockSpec((1, d), lambda i: (0, 0)),
            pl.BlockSpec((d, nh), lambda i: (0, 0)),
        ],
        out_specs=[
            pl.BlockSpec((tm, d), lambda i: (i, 0)),
            pl.BlockSpec((ts, d), lambda i: (0, 0)),
            pl.BlockSpec((tm, nh), lambda i: (i, 0)),
            pl.BlockSpec((ts, nh), lambda i: (0, 0)),
        ],
        out_shape=[
            jax.ShapeDtypeStruct((tp, d), BF16),
            jax.ShapeDtypeStruct((ts, d), BF16),
            jax.ShapeDtypeStruct((tp, nh), F32),
            jax.ShapeDtypeStruct((ts, nh), F32),
        ],
        compiler_params=_params(("arbitrary",)),
        name="norm_dt",
    )(xp, xs, g, wdt)


def _proj_a_kernel(xp_ref, xs_ref, wb_ref, wc_ref, wh_ref, cw_ref, st_ref,
                   op_ref, os_ref, cap_ref, cas_ref,
                   wbf, ubuf, sbuf, s1buf, s2buf, *, tiles_per_batch, nb_s, l_s):
    i = pl.program_id(1)
    tm = xp_ref.shape[0]
    ts = xs_ref.shape[0]
    cw = cw_ref[...]

    @pl.when(i == 0)
    def _():
        wbf[0] = wb_ref[...].astype(BF16)
        wbf[1] = wc_ref[...].astype(BF16)
        wbf[2] = wh_ref[...].astype(BF16)
        x = xs_ref[...]
        u = _dot(x, wbf[1]) * _dot(x, wbf[2])
        sbuf[pl.ds(0, SUBLANES), :] = jnp.zeros((SUBLANES, u.shape[1]), F32)
        sbuf[pl.ds(SUBLANES, ts), :] = u
        s1buf[...] = jnp.zeros_like(s1buf)
        s2buf[...] = jnp.zeros_like(s2buf)
        for b in range(nb_s):
            s1buf[pl.ds(b * l_s, 1), :] = st_ref[b, pl.ds(1, 1), :]
            s2buf[pl.ds(b * l_s, 1), :] = st_ref[b, pl.ds(0, 1), :]
            s2buf[pl.ds(b * l_s + 1, 1), :] = st_ref[b, pl.ds(1, 1), :]
        rmod = lax.broadcasted_iota(jnp.int32, (ts, 1), 0) % l_s
        prev1 = jnp.where(rmod == 0, s1buf[...], sbuf[pl.ds(SUBLANES - 1, ts), :])
        prev2 = jnp.where(rmod < 2, s2buf[...], sbuf[pl.ds(SUBLANES - 2, ts), :])
        v = prev2 * cw[0:1, :] + prev1 * cw[1:2, :] + u * cw[2:3, :]
        os_ref[...] = (_dot(x, wbf[0]) * v).astype(BF16)
        for b in range(nb_s):
            cas_ref[b] = sbuf[pl.ds(SUBLANES + (b + 1) * l_s - 2, 2), :]

    @pl.when(i % tiles_per_batch == 0)
    def _():
        ubuf[pl.ds(0, SUBLANES), :] = jnp.zeros((SUBLANES, ubuf.shape[1]), F32)

    x = xp_ref[...]
    u = _dot(x, wbf[1]) * _dot(x, wbf[2])
    ubuf[pl.ds(SUBLANES, tm), :] = u
    v = (ubuf[pl.ds(SUBLANES - 2, tm), :] * cw[0:1, :]
         + ubuf[pl.ds(SUBLANES - 1, tm), :] * cw[1:2, :] + u * cw[2:3, :])
    op_ref[...] = (_dot(x, wbf[0]) * v).astype(BF16)
    ubuf[pl.ds(0, SUBLANES), :] = ubuf[pl.ds(tm, SUBLANES), :]

    @pl.when(i % tiles_per_batch == tiles_per_batch - 1)
    def _():
        cap_ref[0] = ubuf[pl.ds(SUBLANES + tm - 2, 2), :]


def _proj_a(xnp, xns, w_in, conv_w, state_s, n_batch_p, seq_p, d_conv):
    tp, d = xnp.shape
    ts = xns.shape[0]
    nb_s = state_s.shape[0]
    l_s = ts // nb_s
    tn = 512
    tm = _pick(seq_p, (1024, 512, 256, 128))
    tpb = seq_p // tm
    ncol = d_conv // tn
    kern = functools.partial(_proj_a_kernel, tiles_per_batch=tpb, nb_s=nb_s, l_s=l_s)
    return pl.pallas_call(
        kern,
        grid=(ncol, tp // tm),
        in_specs=[
            pl.BlockSpec((tm, d), lambda j, i: (i, 0)),
            pl.BlockSpec((ts, d), lambda j, i: (0, 0)),
            pl.BlockSpec((d, tn), lambda j, i: (0, j)),
            pl.BlockSpec((d, tn), lambda j, i: (0, j + ncol)),
            pl.BlockSpec((d, tn), lambda j, i: (0, j + 2 * ncol)),
            pl.BlockSpec((CONV_A_K, tn), lambda j, i: (0, j)),
            pl.BlockSpec((nb_s, CONV_A_K - 1, tn), lambda j, i: (0, 0, j)),
        ],
        out_specs=[
            pl.BlockSpec((tm, tn), lambda j, i: (i, j)),
            pl.BlockSpec((ts, tn), lambda j, i: (0, j)),
            pl.BlockSpec((1, CONV_A_K - 1, tn), lambda j, i: (i // tpb, 0, j)),
            pl.BlockSpec((nb_s, CONV_A_K - 1, tn), lambda j, i: (0, 0, j)),
        ],
        out_shape=[
            jax.ShapeDtypeStruct((tp, d_conv), BF16),
            jax.ShapeDtypeStruct((ts, d_conv), BF16),
            jax.ShapeDtypeStruct((n_batch_p, CONV_A_K - 1, d_conv), F32),
            jax.ShapeDtypeStruct((nb_s, CONV_A_K - 1, d_conv), F32),
        ],
        scratch_shapes=[
            pltpu.VMEM((3, d, tn), BF16),
            pltpu.VMEM((SUBLANES + tm, tn), F32),
            pltpu.VMEM((SUBLANES + ts, tn), F32),
            pltpu.VMEM((ts, tn), F32),
            pltpu.VMEM((ts, tn), F32),
        ],
        compiler_params=_params(("arbitrary", "arbitrary")),
        name="proj_a",
    )(xnp, xns, w_in, w_in, w_in, conv_w, state_s)


def _proj_raw_kernel(xp_ref, xs_ref, w_ref, op_ref, os_ref, wbf, *, nsplit, width):
    def emit(x_ref, o_ref):
        acc = _dot(x_ref[...], wbf[...])
        if nsplit == 0:
            o_ref[...] = acc
        else:
            for s in range(nsplit):
                o_ref[s] = acc[:, s * width:(s + 1) * width]

    @pl.when(pl.program_id(1) == 0)
    def _():
        wbf[...] = w_ref[...].astype(BF16)
        emit(xs_ref, os_ref)

    emit(xp_ref, op_ref)


def _proj_raw(xnp, xns, w, col0, ncols, width):
    tp, d = xnp.shape
    ts = xns.shape[0]
    tn = 1024
    tm = _pick(tp, (1024, 512, 256, 128))
    assert col0 % tn == 0 and ncols % tn == 0
    jb = col0 // tn
    if width == 0:
        nsplit = 0
        out_specs = [pl.BlockSpec((tm, tn), lambda j, i: (i, j)),
                     pl.BlockSpec((ts, tn), lambda j, i: (0, j))]
        out_shape = [jax.ShapeDtypeStruct((tp, ncols), F32), jax.ShapeDtypeStruct((ts, ncols), F32)]
    else:
        nsplit = tn // width
        out_specs = [pl.BlockSpec((nsplit, tm, width), lambda j, i: (j, i, 0)),
                     pl.BlockSpec((nsplit, ts, width), lambda j, i: (j, 0, 0))]
        out_shape = [jax.ShapeDtypeStruct((ncols // width, tp, width), F32),
                     jax.ShapeDtypeStruct((ncols // width, ts, width), F32)]
    kern = functools.partial(_proj_raw_kernel, nsplit=nsplit, width=width)
    return pl.pallas_call(
        kern,
        grid=(ncols // tn, tp // tm),
        in_specs=[
            pl.BlockSpec((tm, d), lambda j, i: (i, 0)),
            pl.BlockSpec((ts, d), lambda j, i: (0, 0)),
            pl.BlockSpec((d, tn), lambda j, i: (0, j + jb)),
        ],
        out_specs=out_specs,
        out_shape=out_shape,
        scratch_shapes=[pltpu.VMEM((d, tn), BF16)],
        compiler_params=_params(("arbitrary", "arbitrary")),
        name=f"proj_raw_{col0}",
    )(xnp, xns, w)


def _ssd_kernel(*refs, q, has_state, nchunks):
    (z_ref, xs_ref, b_ref, c_ref, dt_ref, cwx, cwb, cwc, cbx, cbb, cbc,
     dtb_ref, alog_ref, dsk_ref, gn_ref, *rest) = refs
    if has_state:
        cpx, cpb, cpc, hprev, *rest = rest
    (y_ref, ocx, ocb, occ, oh, h_s, ccx, ccb, ccc, xbuf, bbuf, cbuf, acg, dtg, rowt) = rest
    c = pl.program_id(1)
    nheads = N_GROUPS * HEADS_PER_GROUP
    tail = CONV_M_K - 1

    @pl.when(c == 0)
    def _init():
        ccx[...] = jnp.zeros_like(ccx)
        ccb[...] = jnp.zeros_like(ccb)
        ccc[...] = jnp.zeros_like(ccc)
        if has_state:
            for g in range(N_GROUPS):
                ccx[g, pl.ds(SUBLANES - tail, tail), :] = cpx[g]
                ccb[g, pl.ds(SUBLANES - tail, tail), :] = cpb[g]
                ccc[g, pl.ds(SUBLANES - tail, tail), :] = cpc[g]
            h_s[...] = hprev[...]
        else:
            h_s[...] = jnp.zeros_like(h_s)

    dt = _softplus(dt_ref[...] + dtb_ref[...])
    da = dt * (-jnp.exp(alog_ref[...]))
    ri = lax.broadcasted_iota(jnp.int32, (q, q), 0)
    ci = lax.broadcasted_iota(jnp.int32, (q, q), 1)
    causal = ri >= ci
    tril = jnp.where(causal, 1.0, 0.0).astype(BF16)
    acum = sum(_dot(tril, p) for p in _split3(da))
    eye = jnp.where(lax.broadcasted_iota(jnp.int32, (nheads, nheads), 0)
                    == lax.broadcasted_iota(jnp.int32, (nheads, nheads), 1), 1.0, 0.0).astype(BF16)
    rowt[0] = sum(lax.dot_general(eye, p, NT_DIMS, preferred_element_type=F32) for p in _split3(acum))
    rowt[1] = sum(lax.dot_general(eye, p, NT_DIMS, preferred_element_type=F32) for p in _split3(dt))
    for g in range(N_GROUPS):
        acg[g] = acum[:, g * HEADS_PER_GROUP:(g + 1) * HEADS_PER_GROUP]
        dtg[g] = dt[:, g * HEADS_PER_GROUP:(g + 1) * HEADS_PER_GROUP]

    lane = lax.broadcasted_iota(jnp.int32, (1, LANES), 1)
    rowi = lax.broadcasted_iota(jnp.int32, (LANES, 1), 0)
    half_w = LANES // 2

    def conv(raw_ref, cc, buf, cw, cb, ost, g):
        buf[pl.ds(0, SUBLANES), :] = cc[g]
        raw = raw_ref[g]
        buf[pl.ds(SUBLANES, q), :] = raw
        w = cw[g]
        acc = raw * w[tail:tail + 1, :]
        for k in range(tail):
            acc = acc + buf[pl.ds(SUBLANES - tail + k, q), :] * w[k:k + 1, :]
        cc[g] = buf[pl.ds(q, SUBLANES), :]

        @pl.when(c == nchunks - 1)
        def _():
            ost[g] = buf[pl.ds(SUBLANES + q - tail, tail), :]

        return jax.nn.silu(acc + cb[g])

    def group_body(g, carry):
        xs = conv(xs_ref, ccx, xbuf, cwx, cbx, ocx, g)
        bc = conv(b_ref, ccb, bbuf, cwb, cbb, ocb, g)
        cc_ = conv(c_ref, ccc, cbuf, cwc, cbc, occ, g)
        bb = bc.astype(BF16)
        cb_ = lax.dot_general(cc_.astype(BF16), bb, NT_DIMS, preferred_element_type=F32)
        ac8 = acg[g]
        dt8 = dtg[g]
        dsk = dsk_ref[g]
        ys = []
        for pair in range(HEADS_PER_GROUP // 2):
            sl = slice(pair * LANES, (pair + 1) * LANES)
            xp = xs[:, sl]
            hp = h_s[g, pl.ds(pair * LANES, LANES), :]
            ypair = dsk[:, sl] * xp
            wpair = None
            dpair = None
            for half in range(2):
                rr = 2 * pair + half
                acol = jnp.broadcast_to(ac8[:, rr:rr + 1], (q, LANES))
                dcol = jnp.broadcast_to(dt8[:, rr:rr + 1], (q, LANES))
                arow = rowt[0, pl.ds(g * HEADS_PER_GROUP + rr, 1), :]
                drow = rowt[1, pl.ds(g * HEADS_PER_GROUP + rr, 1), :]
                decay = jnp.exp(jnp.where(causal, acol[:, :q] - arow, -jnp.inf))
                m = (cb_ * decay * drow).astype(BF16)
                in_half = (lane < half_w) if half == 0 else (lane >= half_w)
                ypair = ypair + _dot(m, jnp.where(in_half, xp, 0.0).astype(BF16))
                ec = (jnp.exp(acol) * cc_).astype(BF16)
                row_half = (rowi < half_w) if half == 0 else (rowi >= half_w)
                hm = jnp.where(row_half, hp, 0.0).astype(BF16)
                ypair = ypair + lax.dot_general(ec, hm, NT_DIMS, preferred_element_type=F32)
                alast = acol[q - 1:q, :]
                wcol = jnp.exp(alast - acol) * dcol
                dlast = jnp.broadcast_to(jnp.exp(alast), (LANES, LANES))
                if half == 0:
                    wpair, dpair = wcol, dlast
                else:
                    wpair = jnp.where(lane < half_w, wpair, wcol)
                    dpair = jnp.where(rowi < half_w, dpair, dlast)
            st = lax.dot_general((xp * wpair).astype(BF16), bb, TN_DIMS, preferred_element_type=F32)
            h_s[g, pl.ds(pair * LANES, LANES), :] = hp * dpair + st
            ys.append(ypair)
        yg = jnp.concatenate(ys, axis=1)
        hh = yg * jax.nn.silu(z_ref[g])
        ms = jnp.mean(hh * hh, axis=-1, keepdims=True)
        y_ref[g] = (hh * lax.rsqrt(ms + EPS) * gn_ref[g]).astype(BF16)
        return carry

    lax.fori_loop(0, N_GROUPS, group_body, 0)

    @pl.when(c == nchunks - 1)
    def _():
        oh[...] = h_s[...]


def _ssd(zx, bc, dt_raw, prm, n_batch, seq, q, state=None):
    nchunks = seq // q
    nheads = N_GROUPS * HEADS_PER_GROUP
    has_state = state is not None
    g8 = N_GROUPS

    def tok(first):
        return lambda b, c: (first, b * nchunks + c, 0)

    def const3(b, c):
        return (0, 0, 0)

    in_specs = [
        pl.BlockSpec((g8, q, GROUP_W), tok(0)),
        pl.BlockSpec((g8, q, GROUP_W), tok(1)),
        pl.BlockSpec((g8, q, D_STATE), tok(0)),
        pl.BlockSpec((g8, q, D_STATE), tok(1)),
        pl.BlockSpec((q, nheads), lambda b, c: (b * nchunks + c, 0)),
        pl.BlockSpec((g8, CONV_M_K, GROUP_W), const3),
        pl.BlockSpec((g8, CONV_M_K, D_STATE), const3),
        pl.BlockSpec((g8, CONV_M_K, D_STATE), const3),
        pl.BlockSpec((g8, 1, GROUP_W), const3),
        pl.BlockSpec((g8, 1, D_STATE), const3),
        pl.BlockSpec((g8, 1, D_STATE), const3),
        pl.BlockSpec((1, nheads), lambda b, c: (0, 0)),
        pl.BlockSpec((1, nheads), lambda b, c: (0, 0)),
        pl.BlockSpec((g8, 1, GROUP_W), const3),
        pl.BlockSpec((g8, 1, GROUP_W), const3),
    ]
    args = [zx, zx, bc, bc, dt_raw, prm["cwx"], prm["cwb"], prm["cwc"], prm["cbx"], prm["cbb"], prm["cbc"],
            prm["dtb"], prm["alog"], prm["dsk"], prm["gn"]]
    tail = CONV_M_K - 1

    def per_batch(shape):
        return pl.BlockSpec((None,) + shape, lambda b, c: (b,) + (0,) * len(shape))

    if has_state:
        in_specs += [per_batch((g8, tail, GROUP_W)), per_batch((g8, tail, D_STATE)),
                     per_batch((g8, tail, D_STATE)), per_batch((g8, GROUP_W, D_STATE))]
        args += list(state)
    t = n_batch * seq
    out_specs = [
        pl.BlockSpec((g8, q, GROUP_W), lambda b, c: (0, b * nchunks + c, 0)),
        per_batch((g8, tail, GROUP_W)), per_batch((g8, tail, D_STATE)), per_batch((g8, tail, D_STATE)),
        per_batch((g8, GROUP_W, D_STATE)),
    ]
    out_shape = [
        jax.ShapeDtypeStruct((g8, t, GROUP_W), BF16),
        jax.ShapeDtypeStruct((n_batch, g8, tail, GROUP_W), F32),
        jax.ShapeDtypeStruct((n_batch, g8, tail, D_STATE), F32),
        jax.ShapeDtypeStruct((n_batch, g8, tail, D_STATE), F32),
        jax.ShapeDtypeStruct((n_batch, g8, GROUP_W, D_STATE), F32),
    ]
    scratch = [
        pltpu.VMEM((g8, GROUP_W, D_STATE), F32),
        pltpu.VMEM((g8, SUBLANES, GROUP_W), F32),
        pltpu.VMEM((g8, SUBLANES, D_STATE), F32),
        pltpu.VMEM((g8, SUBLANES, D_STATE), F32),
        pltpu.VMEM((SUBLANES + q, GROUP_W), F32),
        pltpu.VMEM((SUBLANES + q, D_STATE), F32),
        pltpu.VMEM((SUBLANES + q, D_STATE), F32),
        pltpu.VMEM((g8, q, HEADS_PER_GROUP), F32),
        pltpu.VMEM((g8, q, HEADS_PER_GROUP), F32),
        pltpu.VMEM((2, nheads, q), F32),
    ]
    kern = functools.partial(_ssd_kernel, q=q, has_state=has_state, nchunks=nchunks)
    return pl.pallas_call(
        kern,
        grid=(n_batch, nchunks),
        in_specs=in_specs,
        out_specs=out_specs,
        out_shape=out_shape,
        scratch_shapes=scratch,
        compiler_params=_params(("arbitrary", "arbitrary")),
        name="ssd_state" if has_state else "ssd",
    )(*args)


def _merge_kernel(ap_ref, as_ref, yp_ref, ys_ref, gap_ref, gmp_ref, gas_ref, gms_ref, wa_ref, wm_ref,
                  op_ref, os_ref, wab, wmb):
    def emit(a_ref, y_ref, ga_ref, gm_ref, o_ref):
        oa = _dot(a_ref[...], wab[...])
        om = _dot(y_ref[0], wmb[0])
        for g in range(1, N_GROUPS):
            om = om + _dot(y_ref[g], wmb[g])
        o_ref[...] = (jax.nn.sigmoid(ga_ref[...]) * oa + jax.nn.sigmoid(gm_ref[...]) * om).astype(BF16)

    @pl.when(pl.program_id(1) == 0)
    def _():
        wab[...] = wa_ref[...].astype(BF16)
        wmb[...] = wm_ref[...].astype(BF16)
        emit(as_ref, ys_ref, gas_ref, gms_ref, os_ref)

    emit(ap_ref, yp_ref, gap_ref, gmp_ref, op_ref)


def _merge(abv_p, abv_s, y_p, y_s, g_p, g_s, w_a_out, w_m_out3):
    tp, dc = abv_p.shape
    ts = abv_s.shape[0]
    dm = w_a_out.shape[1]
    tn = 512
    tm = _pick(tp, (512, 256, 128))
    ncol = dm // tn
    return pl.pallas_call(
        _merge_kernel,
        grid=(ncol, tp // tm),
        in_specs=[
            pl.BlockSpec((tm, dc), lambda j, i: (i, 0)),
            pl.BlockSpec((ts, dc), lambda j, i: (0, 0)),
            pl.BlockSpec((N_GROUPS, tm, GROUP_W), lambda j, i: (0, i, 0)),
            pl.BlockSpec((N_GROUPS, ts, GROUP_W), lambda j, i: (0, 0, 0)),
            pl.BlockSpec((tm, tn), lambda j, i: (i, j)),
            pl.BlockSpec((tm, tn), lambda j, i: (i, j + ncol)),
            pl.BlockSpec((ts, tn), lambda j, i: (0, j)),
            pl.BlockSpec((ts, tn), lambda j, i: (0, j + ncol)),
            pl.BlockSpec((dc, tn), lambda j, i: (0, j)),
            pl.BlockSpec((N_GROUPS, GROUP_W, tn), lambda j, i: (0, 0, j)),
        ],
        out_specs=[
            pl.BlockSpec((tm, tn), lambda j, i: (i, j)),
            pl.BlockSpec((ts, tn), lambda j, i: (0, j)),
        ],
        out_shape=[jax.ShapeDtypeStruct((tp, dm), BF16), jax.ShapeDtypeStruct((ts, dm), BF16)],
        scratch_shapes=[pltpu.VMEM((dc, tn), BF16), pltpu.VMEM((N_GROUPS, GROUP_W, tn), BF16)],
        compiler_params=_params(("arbitrary", "arbitrary")),
        name="merge",
    )(abv_p, abv_s, y_p, y_s, g_p, g_p, g_s, g_s, w_a_out, w_m_out3)


def _res_kernel(ap_ref, as_ref, rp_ref, rs_ref, w_ref, g_ref, *outs, n_prompt_tiles, merged_out, ts):
    if merged_out:
        h_ref, hn_ref, wbf = outs
    else:
        hp_ref, hs_ref, hnp_ref, hns_ref, wbf = outs
    i = pl.program_id(0)

    @pl.when(i == 0)
    def _():
        wbf[...] = w_ref[...].astype(BF16)

    def emit(a_ref, r_ref, store_h, store_hn):
        h = r_ref[...] + _dot(a_ref[...], wbf[...])
        store_h(h)
        store_hn(_rms(h, g_ref[...]))

    if merged_out:
        @pl.when(i < n_prompt_tiles)
        def _():
            def sh(h):
                h_ref[...] = h

            def shn(hn):
                hn_ref[...] = hn

            emit(ap_ref, rp_ref, sh, shn)

        @pl.when(i == n_prompt_tiles)
        def _():
            def sh(h):
                h_ref[pl.ds(0, ts), :] = h

            def shn(hn):
                hn_ref[pl.ds(0, ts), :] = hn

            emit(as_ref, rs_ref, sh, shn)
    else:
        def shp(h):
            hp_ref[...] = h

        def shnp(hn):
            hnp_ref[...] = hn.astype(hnp_ref.dtype)

        emit(ap_ref, rp_ref, shp, shnp)

        @pl.when(i == 0)
        def _():
            def shs(h):
                hs_ref[...] = h

            def shns(hn):
                hns_ref[...] = hn.astype(hns_ref.dtype)

            emit(as_ref, rs_ref, shs, shns)


def _res(a_p, a_s, r_p, r_s, w, g, merged_out, hn_dtype):
    tp, d = a_p.shape
    ts = a_s.shape[0]
    tm = 256
    assert tp % tm == 0 and ts <= tm
    npt = tp // tm
    last = npt - 1
    in_specs = [
        pl.BlockSpec((tm, d), lambda i: (jnp.minimum(i, last), 0)),
        pl.BlockSpec((ts, d), lambda i: (0, 0)),
        pl.BlockSpec((tm, d), lambda i: (jnp.minimum(i, last), 0)),
        pl.BlockSpec((ts, d), lambda i: (0, 0)),
        pl.BlockSpec((d, d), lambda i: (0, 0), pipeline_mode=pl.Buffered(1)),
        pl.BlockSpec((1, d), lambda i: (0, 0)),
    ]
    if merged_out:
        grid = (npt + 1,)
        out_specs = [pl.BlockSpec((tm, d), lambda i: (i, 0)), pl.BlockSpec((tm, d), lambda i: (i, 0))]
        out_shape = [jax.ShapeDtypeStruct((tp + ts, d), F32), jax.ShapeDtypeStruct((tp + ts, d), hn_dtype)]
    else:
        grid = (npt,)
        out_specs = [pl.BlockSpec((tm, d), lambda i: (i, 0)), pl.BlockSpec((ts, d), lambda i: (0, 0)),
                     pl.BlockSpec((tm, d), lambda i: (i, 0)), pl.BlockSpec((ts, d), lambda i: (0, 0))]
        out_shape = [jax.ShapeDtypeStruct((tp, d), F32), jax.ShapeDtypeStruct((ts, d), F32),
                     jax.ShapeDtypeStruct((tp, d), hn_dtype), jax.ShapeDtypeStruct((ts, d), hn_dtype)]
    kern = functools.partial(_res_kernel, n_prompt_tiles=npt, merged_out=merged_out, ts=ts)
    return pl.pallas_call(
        kern,
        grid=grid,
        in_specs=in_specs,
        out_specs=out_specs,
        out_shape=out_shape,
        scratch_shapes=[pltpu.VMEM((d, d), BF16)],
        compiler_params=_params(("arbitrary",)),
        name="res_merged" if merged_out else "res",
    )(a_p, a_s, r_p, r_s, w, g)


def _kv_kernel(m_ref, g_ref, wk_ref, wv_ref, k_ref, v_ref, mn):
    @pl.when(pl.program_id(0) == 0)
    def _():
        mn[...] = _rms(m_ref[...], g_ref[...]).astype(BF16)

    k_ref[...] = _dot(mn[...], wk_ref[...].astype(BF16))
    v_ref[...] = _dot(mn[...], wv_ref[...].astype(BF16))


def _memory_kv(mem2d, g, w_k, w_v):
    m, d = mem2d.shape
    tn = 512
    return pl.pallas_call(
        _kv_kernel,
        grid=(d // tn,),
        in_specs=[
            pl.BlockSpec((m, d), lambda j: (0, 0)),
            pl.BlockSpec((1, d), lambda j: (0, 0)),
            pl.BlockSpec((d, tn), lambda j: (0, j)),
            pl.BlockSpec((d, tn), lambda j: (0, j)),
        ],
        out_specs=[pl.BlockSpec((m, tn), lambda j: (0, j)), pl.BlockSpec((m, tn), lambda j: (0, j))],
        out_shape=[jax.ShapeDtypeStruct((m, d), F32), jax.ShapeDtypeStruct((m, d), F32)],
        scratch_shapes=[pltpu.VMEM((m, d), BF16)],
        compiler_params=_params(("arbitrary",)),
        name="memory_kv",
    )(mem2d, g, w_k, w_v)


def _attn_kernel(hn_ref, k_ref, v_ref, wq_ref, o_ref, wqb, kb, vb):
    b = pl.program_id(0)
    i = pl.program_id(1)

    @pl.when((b == 0) & (i == 0))
    def _():
        wqb[...] = wq_ref[...].astype(BF16)

    @pl.when(i == 0)
    def _():
        kb[...] = k_ref[...].astype(BF16)
        vb[...] = v_ref[...].astype(BF16)

    d = wqb.shape[1]
    dh = d // N_XHEADS
    q = _dot(hn_ref[...], wqb[...])
    outs = []
    for h in range(N_XHEADS):
        sl = slice(h * dh, (h + 1) * dh)
        s = lax.dot_general(q[:, sl].astype(BF16), kb[:, sl], NT_DIMS, preferred_element_type=F32)
        s = s * (dh ** -0.5)
        e = jnp.exp(s - jnp.max(s, axis=-1, keepdims=True))
        p = e / jnp.sum(e, axis=-1, keepdims=True)
        outs.append(_dot(p.astype(BF16), vb[:, sl]))
    o_ref[...] = jnp.concatenate(outs, axis=1).astype(BF16)


def _attn(hn, k3, v3, w_q, n_batch, seq):
    t, d = hn.shape
    nm = k3.shape[1]
    tm = _pick(seq, (256, 128, 64, 32, 16))
    tpb = seq // tm
    return pl.pallas_call(
        _attn_kernel,
        grid=(n_batch, tpb),
        in_specs=[
            pl.BlockSpec((tm, d), lambda b, i: (b * tpb + i, 0)),
            pl.BlockSpec((None, nm, d), lambda b, i: (b, 0, 0)),
            pl.BlockSpec((None, nm, d), lambda b, i: (b, 0, 0)),
            pl.BlockSpec((d, d), lambda b, i: (0, 0), pipeline_mode=pl.Buffered(1)),
        ],
        out_specs=pl.BlockSpec((tm, d), lambda b, i: (b * tpb + i, 0)),
        out_shape=jax.ShapeDtypeStruct((t, d), BF16),
        scratch_shapes=[pltpu.VMEM((d, d), BF16), pltpu.VMEM((nm, d), BF16), pltpu.VMEM((nm, d), BF16)],
        compiler_params=_params(("arbitrary", "arbitrary")),
        name=f"attn_{seq}",
    )(hn, k3, v3, w_q)


def _router_kernel(x_ref, w_ref, b_ref, ri_ref, rw_ref):
    logits = _dot(x_ref[...].astype(BF16), w_ref[...].astype(BF16)) + b_ref[...]
    lane_i = lax.broadcasted_iota(jnp.int32, logits.shape, 1)
    lane = lane_i.astype(F32)
    ninf = -jnp.inf
    big = float(LANES)
    is_g = lane < N_EXPERT_GROUPS
    gl = jnp.where(is_g, logits, ninf)
    gmax = jnp.max(gl, axis=-1, keepdims=True)
    gsel = jnp.min(jnp.where(gl == gmax, lane, big), axis=-1, keepdims=True)
    pg = 1.0 / jnp.sum(jnp.where(is_g, jnp.exp(gl - gmax), 0.0), axis=-1, keepdims=True)
    lo = N_EXPERT_GROUPS + EXPERTS_PER_GROUP * gsel
    el = jnp.where(lane >= lo, jnp.where(lane < lo + EXPERTS_PER_GROUP, logits, ninf), ninf)
    m1 = jnp.max(el, axis=-1, keepdims=True)
    i1 = jnp.min(jnp.where(el == m1, lane, big), axis=-1, keepdims=True)
    el2 = jnp.where(lane == i1, ninf, el)
    m2 = jnp.max(el2, axis=-1, keepdims=True)
    i2 = jnp.min(jnp.where(el2 == m2, lane, big), axis=-1, keepdims=True)
    e = jnp.exp(m2 - m1)
    w1 = pg / (1.0 + e)
    w2 = pg * e / (1.0 + e)
    ri_ref[...] = jnp.where(lane_i == 0, i1 - N_EXPERT_GROUPS,
                            jnp.where(lane_i == 1, i2 - N_EXPERT_GROUPS, 0.0)).astype(jnp.int32)
    rw_ref[...] = jnp.where(lane_i == 0, w1, jnp.where(lane_i == 1, w2, 0.0))


def _router(hn_all, w_r, b_r):
    t, d = hn_all.shape
    tm = _pick(t, (640, 512, 384, 256, 128))
    return pl.pallas_call(
        _router_kernel,
        grid=(t // tm,),
        in_specs=[
            pl.BlockSpec((tm, d), lambda i: (i, 0)),
            pl.BlockSpec((d, LANES), lambda i: (0, 0)),
            pl.BlockSpec((1, LANES), lambda i: (0, 0)),
        ],
        out_specs=[pl.BlockSpec((tm, LANES), lambda i: (i, 0)), pl.BlockSpec((tm, LANES), lambda i: (i, 0))],
        out_shape=[jax.ShapeDtypeStruct((t, LANES), jnp.int32), jax.ShapeDtypeStruct((t, LANES), F32)],
        compiler_params=_params(("arbitrary",)),
        name="router",
    )(hn_all, w_r, b_r)


def _row_copy(src, dst, s, d, sem):
    return pltpu.make_async_copy(src.at[pl.ds(s, 1)], dst.at[pl.ds(d, 1)], sem)


def _dispatch_kernel(dest_ref, x_ref, o_ref, sem, *, chunk):
    base = pl.program_id(0) * chunk

    def issue(r, carry):
        t = base + r
        _row_copy(x_ref, o_ref, t, dest_ref[2 * t], sem).start()
        _row_copy(x_ref, o_ref, t, dest_ref[2 * t + 1], sem).start()
        return carry

    lax.fori_loop(0, chunk, issue, 0)

    def drain(r, carry):
        _row_copy(x_ref, o_ref, 0, 0, sem).wait()
        _row_copy(x_ref, o_ref, 0, 0, sem).wait()
        return carry

    lax.fori_loop(0, chunk, drain, 0)


def _dispatch(dest, hn_all):
    t, d = hn_all.shape
    chunk = _pick(t, (640, 512, 384, 256, 128))
    kern = functools.partial(_dispatch_kernel, chunk=chunk)
    return pl.pallas_call(
        kern,
        grid_spec=pltpu.PrefetchScalarGridSpec(
            num_scalar_prefetch=1,
            grid=(t // chunk,),
            in_specs=[pl.BlockSpec(memory_space=pl.ANY)],
            out_specs=pl.BlockSpec(memory_space=pl.ANY),
            scratch_shapes=[pltpu.SemaphoreType.DMA(())],
        ),
        out_shape=jax.ShapeDtypeStruct((2 * t, d), F32),
        compiler_params=_params(("arbitrary",)),
        name="dispatch",
    )(dest, hn_all)


def _expert_kernel(blk_ref, exp_ref, lo_ref, hi_ref, first_ref, x_ref, wg_ref, wu_ref, wd_ref, y_ref, xb):
    it = pl.program_id(0)
    k = pl.program_id(1)
    lo = lo_ref[it]
    hi = hi_ref[it]

    @pl.when((k == 0) & (first_ref[it] == 1))
    def _():
        y_ref[...] = jnp.zeros_like(y_ref)

    @pl.when(hi > lo)
    def _():
        @pl.when(k == 0)
        def _():
            rows = lax.broadcasted_iota(jnp.int32, (x_ref.shape[0], 1), 0)
            xb[...] = jnp.where((rows >= lo) & (rows < hi), x_ref[...], 0.0).astype(BF16)

        x = xb[...]
        gate = _dot(x, wg_ref[...].astype(BF16))
        up = _dot(x, wu_ref[...].astype(BF16))
        hid = (jax.nn.silu(gate) * up).astype(BF16)
        y_ref[...] += _dot(hid, wd_ref[...].astype(BF16))


def _experts(items, x_sorted, w_gate, w_up, w_down, rows_per_block, kchunk):
    blk, exp, lo, hi, first = items
    n_items = blk.shape[0]
    npairs, d = x_sorted.shape
    de = w_gate.shape[2]
    nk = de // kchunk
    r = rows_per_block

    def keff(it, k, lo_r, hi_r):
        return jnp.where(hi_r[it] > lo_r[it], k, nk - 1)

    return pl.pallas_call(
        _expert_kernel,
        grid_spec=pltpu.PrefetchScalarGridSpec(
            num_scalar_prefetch=5,
            grid=(n_items, nk),
            in_specs=[
                pl.BlockSpec((r, d), lambda it, k, b, e, l, h, f: (b[it], 0)),
                pl.BlockSpec((None, d, kchunk), lambda it, k, b, e, l, h, f: (e[it], 0, keff(it, k, l, h))),
                pl.BlockSpec((None, d, kchunk), lambda it, k, b, e, l, h, f: (e[it], 0, keff(it, k, l, h))),
                pl.BlockSpec((None, kchunk, d), lambda it, k, b, e, l, h, f: (e[it], keff(it, k, l, h), 0)),
            ],
            out_specs=pl.BlockSpec((r, d), lambda it, k, b, e, l, h, f: (b[it], 0)),
            scratch_shapes=[pltpu.VMEM((r, d), BF16)],
        ),
        out_shape=jax.ShapeDtypeStruct((npairs, d), F32),
        compiler_params=_params(("arbitrary", "arbitrary")),
        name="experts",
    )(blk, exp, lo, hi, first, x_sorted, w_gate, w_up, w_down)


def _final_kernel(dest_ref, h_ref, rw_ref, ys_ref, g_ref, yp_ref, yss_ref, ya, yb, sem, *,
                  tm, n_prompt_tiles, ts):
    i = pl.program_id(0)
    base = i * tm
    nrows = jnp.where(i < n_prompt_tiles, tm, ts)

    def issue(r, carry):
        t = base + r
        _row_copy(ys_ref, ya, dest_ref[2 * t], r, sem).start()
        _row_copy(ys_ref, yb, dest_ref[2 * t + 1], r, sem).start()
        return carry

    lax.fori_loop(0, nrows, issue, 0)

    def drain(r, carry):
        _row_copy(ys_ref, ya, 0, 0, sem).wait()
        _row_copy(ys_ref, yb, 0, 0, sem).wait()
        return carry

    lax.fori_loop(0, nrows, drain, 0)

    def emit(n, o_ref):
        w = rw_ref[pl.ds(0, n), :]
        h = h_ref[pl.ds(0, n), :] + w[:, 0:1] * ya[pl.ds(0, n), :] + w[:, 1:2] * yb[pl.ds(0, n), :]
        o_ref[...] = _rms(h, g_ref[...])

    @pl.when(i < n_prompt_tiles)
    def _():
        emit(tm, yp_ref)

    @pl.when(i == n_prompt_tiles)
    def _():
        emit(ts, yss_ref)


def _final(dest, h_all, rw, y_sorted, g, tp, ts):
    t, d = h_all.shape
    tm = 256
    assert tp % tm == 0 and ts <= tm and t == tp + ts
    npt = tp // tm
    kern = functools.partial(_final_kernel, tm=tm, n_prompt_tiles=npt, ts=ts)
    return pl.pallas_call(
        kern,
        grid_spec=pltpu.PrefetchScalarGridSpec(
            num_scalar_prefetch=1,
            grid=(npt + 1,),
            in_specs=[
                pl.BlockSpec((tm, d), lambda i, dref: (i, 0)),
                pl.BlockSpec((tm, LANES), lambda i, dref: (i, 0)),
                pl.BlockSpec(memory_space=pl.ANY),
                pl.BlockSpec((1, d), lambda i, dref: (0, 0)),
            ],
            out_specs=[
                pl.BlockSpec((tm, d), lambda i, dref: (jnp.minimum(i, npt - 1), 0)),
                pl.BlockSpec((ts, d), lambda i, dref: (0, 0)),
            ],
            scratch_shapes=[pltpu.VMEM((tm, d), F32), pltpu.VMEM((tm, d), F32), pltpu.SemaphoreType.DMA(())],
        ),
        out_shape=[jax.ShapeDtypeStruct((tp, d), F32), jax.ShapeDtypeStruct((ts, d), F32)],
        compiler_params=_params(("arbitrary",)),
        name="final",
    )(dest, h_all, rw, y_sorted, g)


def _routing_tables(eid, rows_per_block):
    npairs = eid.shape[0]
    r = rows_per_block
    i32 = jnp.int32
    onehot = (eid[:, None] == jnp.arange(N_EXPERTS, dtype=i32)[None, :]).astype(i32)
    csum = jnp.cumsum(onehot, axis=0)
    counts = csum[-1]
    starts = jnp.cumsum(counts) - counts
    dest = jnp.sum(onehot * (csum - 1 + starts[None, :]), axis=1).astype(i32)

    nblocks = -(-npairs // r)
    a = jnp.arange(nblocks, dtype=i32) * r
    s = starts[1:].astype(i32)
    pos_a = jnp.arange(nblocks, dtype=i32) + jnp.sum((s[None, :] < a[:, None]).astype(i32), axis=1)
    pos_s = jnp.arange(N_EXPERTS - 1, dtype=i32) + jnp.sum((a[None, :] <= s[:, None]).astype(i32), axis=1)
    n_items = nblocks + N_EXPERTS - 1
    idx = jnp.arange(n_items, dtype=i32)
    bps = (jnp.sum(jnp.where(pos_a[None, :] == idx[:, None], a[None, :], 0), axis=1)
           + jnp.sum(jnp.where(pos_s[None, :] == idx[:, None], s[None, :], 0), axis=1)).astype(i32)
    ends = jnp.concatenate([bps[1:], jnp.array([npairs], i32)])
    blk = jnp.minimum(bps // r, nblocks - 1)
    length = ends - bps
    exp = jnp.sum((starts[None, :] <= bps[:, None]).astype(i32), axis=1) - 1
    last_real = jnp.max(jnp.where((idx[None, :] <= idx[:, None]) & (length[None, :] > 0), idx[None, :], 0), axis=1)
    exp = jnp.sum(jnp.where(idx[None, :] == last_real[:, None], exp[None, :], 0), axis=1).astype(i32)
    lo = bps - blk * r
    hi = lo + length
    first = jnp.concatenate([jnp.ones((1,), i32), (blk[1:] != blk[:-1]).astype(i32)])
    return dest, (blk.astype(i32), exp, lo.astype(i32), hi.astype(i32), first)


def _group_major(a, width):
    lead = a.shape[:-1]
    g = a.shape[-1] // width
    return jnp.moveaxis(a.reshape(lead + (g, width)), -2, 0)


def kernel(x_prompt, x_sample, cache_mem_k, cache_mem_v, state_conv_a, state_conv_m, state_ssm, mem_prompt,
           norm_mix, w_in, conv_a_w, w_a_out, conv_m_w, conv_m_b, dt_bias, a_log, d_skip, ssm_norm,
           w_m_out, w_o, norm_cross, norm_mem, w_q, w_k, w_v, w_co, norm_ffn, w_rg, b_rg, w_re, b_re,
           w_gate, w_up, w_down, norm_final):
    depth = w_in.shape[0]
    assert depth == 1, "single-layer step"
    nbp, seq_p, d = x_prompt.shape
    nbs, seq_s, _ = x_sample.shape
    tp, ts = nbp * seq_p, nbs * seq_s
    n_mem = mem_prompt.shape[1]
    d_conv = conv_a_w.shape[2]
    d_inner = w_m_out.shape[1]
    nheads = dt_bias.shape[1]
    bc_w = N_GROUPS * D_STATE
    l = 0

    xp = x_prompt.reshape(tp, d)
    xs = x_sample.reshape(ts, d)
    w_in_l = w_in[l]
    col_z = 3 * d_conv
    col_x = col_z + d_inner
    col_b = col_x + d_inner
    col_dt = col_b + 2 * bc_w
    col_g = col_dt + nheads
    w_dt = w_in_l[:, col_dt:col_g]
    w_gates = w_in_l[:, col_g:]

    mk2, mv2 = _memory_kv(mem_prompt.reshape(nbp * n_mem, d), norm_mem[l][None], w_k[l], w_v[l])

    xnp, xns, dt_p, dt_s = _norm_dt(xp, xs, norm_mix[l][None], w_dt)
    abv_p, abv_s, ca_p, ca_s = _proj_a(xnp, xns, w_in_l, conv_a_w[l], state_conv_a[l], nbp, seq_p, d_conv)
    zx_p, zx_s = _proj_raw(xnp, xns, w_in_l, col_z, 2 * d_inner, GROUP_W)
    bc_p, bc_s = _proj_raw(xnp, xns, w_in_l, col_b, 2 * bc_w, D_STATE)
    g_p, g_s = _proj_raw(xnp, xns, w_gates, 0, 2 * d, 0)

    cw = conv_m_w[l]
    cbias = conv_m_b[l][None]
    prm = {
        "cwx": _group_major(cw[:, :d_inner], GROUP_W),
        "cwb": _group_major(cw[:, d_inner:d_inner + bc_w], D_STATE),
        "cwc": _group_major(cw[:, d_inner + bc_w:], D_STATE),
        "cbx": _group_major(cbias[:, :d_inner], GROUP_W),
        "cbb": _group_major(cbias[:, d_inner:d_inner + bc_w], D_STATE),
        "cbc": _group_major(cbias[:, d_inner + bc_w:], D_STATE),
        "dtb": dt_bias[l][None],
        "alog": a_log[l][None],
        "dsk": _group_major(jnp.repeat(d_skip[l], HEAD_DIM)[None], GROUP_W),
        "gn": _group_major(ssm_norm[l][None], GROUP_W),
    }
    scm = state_conv_m[l]
    state_s = (
        jnp.moveaxis(_group_major(scm[..., :d_inner], GROUP_W), 0, 1),
        jnp.moveaxis(_group_major(scm[..., d_inner:d_inner + bc_w], D_STATE), 0, 1),
        jnp.moveaxis(_group_major(scm[..., d_inner + bc_w:], D_STATE), 0, 1),
        state_ssm[l].reshape(nbs, N_GROUPS, GROUP_W, D_STATE),
    )
    q_p = _pick(seq_p, (128, 64, 32, 16, 8))
    y_p, cx_p, cb_p, cc_p, h_p = _ssd(zx_p, bc_p, dt_p, prm, nbp, seq_p, q_p)
    y_s, cx_s, cb_s, cc_s, h_s = _ssd(zx_s, bc_s, dt_s, prm, nbs, seq_s, seq_s, state=state_s)

    def conv_m_state(cx, cb_, cc_):
        def flat(a):
            return jnp.moveaxis(a, 1, 2).reshape(a.shape[0], a.shape[2], -1)
        return jnp.concatenate([flat(cx), flat(cb_), flat(cc_)], axis=-1)

    merged_p, merged_s = _merge(abv_p, abv_s, y_p, y_s, g_p, g_s, w_a_out[l],
                                w_m_out[l].reshape(N_GROUPS, GROUP_W, d))
    h1_p, h1_s, hn1_p, hn1_s = _res(merged_p, merged_s, xp, xs, w_o[l], norm_cross[l][None], False, BF16)

    att_p = _attn(hn1_p, mk2.reshape(nbp, n_mem, d), mv2.reshape(nbp, n_mem, d), w_q[l], nbp, seq_p)
    att_s = _attn(hn1_s, cache_mem_k[l].reshape(nbs, n_mem, d), cache_mem_v[l].reshape(nbs, n_mem, d),
                  w_q[l], nbs, seq_s)
    h2_all, hn2_all = _res(att_p, att_s, h1_p, h1_s, w_co[l], norm_ffn[l][None], True, F32)

    npad = LANES - N_EXPERT_GROUPS - N_EXPERTS
    w_r = jnp.concatenate([w_rg[l], w_re[l], jnp.zeros((d, npad), F32)], axis=1)
    b_r = jnp.concatenate([b_rg[l], b_re[l], jnp.zeros((npad,), F32)])[None]
    ri, rw = _router(hn2_all, w_r, b_r)
    rows_per_block = 512
    dest, items = _routing_tables(ri[:, :2].reshape(-1), rows_per_block)
    x_sorted = _dispatch(dest, hn2_all)
    y_sorted = _experts(items, x_sorted, w_gate[l], w_up[l], w_down[l], rows_per_block, 256)
    y_prompt, y_sample = _final(dest, h2_all, rw, y_sorted, norm_final[None], tp, ts)

    xh = d // N_XHEADS
    return (
        y_prompt.reshape(nbp, seq_p, d),
        y_sample.reshape(nbs, seq_s, d),
        mk2.reshape(1, nbp, n_mem, N_XHEADS, xh),
        mv2.reshape(1, nbp, n_mem, N_XHEADS, xh),
        ca_p[None],
        conv_m_state(cx_p, cb_p, cc_p)[None],
        h_p.reshape(1, nbp, nheads, HEAD_DIM, D_STATE),
        ca_s[None],
        conv_m_state(cx_s, cb_s, cc_s)[None],
        h_s.reshape(1, nbs, nheads, HEAD_DIM, D_STATE),
    )
```

```python
import functools

import jax
import jax.numpy as jnp
from jax import lax
from jax.experimental import pallas as pl
from jax.experimental.pallas import tpu as pltpu

F32 = jnp.float32
BF16 = jnp.bfloat16
EPS = 1e-6

V7X_VMEM_BYTES = 64 * 1024 * 1024
VMEM_LIMIT = V7X_VMEM_BYTES - 8 * 1024 * 1024
LANES = 128
SUBLANES = 8

N_GROUPS = 8
HEADS_PER_GROUP = 8
HEAD_DIM = 64
D_STATE = 128
GROUP_W = HEADS_PER_GROUP * HEAD_DIM
N_XHEADS = 4
N_EXPERTS = 32
N_EXPERT_GROUPS = 4
EXPERTS_PER_GROUP = 8
CONV_A_K = 3
CONV_M_K = 4

NT_DIMS = (((1,), (1,)), ((), ()))
TN_DIMS = (((0,), (0,)), ((), ()))


def _params(sem):
    return pltpu.CompilerParams(dimension_semantics=sem, vmem_limit_bytes=VMEM_LIMIT)


def _pick(n, cands):
    for c in cands:
        if n % c == 0:
            return c
    raise ValueError(f"no tile for {n} in {cands}")


def _dot(a, b):
    return jnp.dot(a, b, preferred_element_type=F32)


def _dot_nt(a, b):
    return lax.dot_general(a, b, NT_DIMS, preferred_element_type=F32)


def _rms(x, g):
    return x * lax.rsqrt(jnp.mean(x * x, axis=-1, keepdims=True) + EPS) * g


def _split3(x):
    hi = x.astype(BF16)
    r = x - hi.astype(F32)
    mid = r.astype(BF16)
    lo = (r - mid.astype(F32)).astype(BF16)
    return hi, mid, lo


def _softplus(x):
    return jnp.maximum(x, 0.0) + jnp.log1p(jnp.exp(-jnp.abs(x)))


def _norm_dt_kernel(xp_ref, xs_ref, g_ref, wdt_ref, xnp_ref, xns_ref, dtp_ref, dts_ref):
    wdt = wdt_ref[...].astype(BF16)

    def one(x_ref, xn_ref, dt_ref):
        xn = _rms(x_ref[...], g_ref[...]).astype(BF16)
        xn_ref[...] = xn
        dt_ref[...] = _dot_nt(xn, wdt)

    one(xp_ref, xnp_ref, dtp_ref)

    @pl.when(pl.program_id(0) == 0)
    def _():
        one(xs_ref, xns_ref, dts_ref)


def _norm_dt(xp, xs, g, wdt):
    tp, d = xp.shape
    ts = xs.shape[0]
    nh = wdt.shape[0]
    tm = _pick(tp, (512, 256, 128))
    return pl.pallas_call(
        _norm_dt_kernel,
        grid=(tp // tm,),
        in_specs=[
            pl.BlockSpec((tm, d), lambda i: (i, 0)),
            pl.BlockSpec((ts, d), lambda i: (0, 0)),
            pl.BlockSpec((1, d), lambda i: (0, 0)),
            pl.BlockSpec((nh, d), lambda i: (0, 0)),
        ],
        out_specs=[
            pl.BlockSpec((tm, d), lambda i: (i, 0)),
            pl.BlockSpec((ts, d), lambda i: (0, 0)),
            pl.BlockSpec((tm, nh), lambda i: (i, 0)),
            pl.BlockSpec((ts, nh), lambda i: (0, 0)),
        ],
        out_shape=[
            jax.ShapeDtypeStruct((tp, d), BF16),
            jax.ShapeDtypeStruct((ts, d), BF16),
            jax.ShapeDtypeStruct((tp, nh), F32),
            jax.ShapeDtypeStruct((ts, nh), F32),
        ],
        compiler_params=_params(("arbitrary",)),
        name="norm_dt",
    )(xp, xs, g, wdt)


def _proj_a_kernel(xp_ref, xs_ref, wb_ref, wc_ref, wh_ref, cw_ref, st_ref,
                   op_ref, os_ref, cap_ref, cas_ref,
                   wbf, ubuf, sbuf, s1buf, s2buf, *, tiles_per_batch, nb_s, l_s):
    i = pl.program_id(1)
    tm = xp_ref.shape[0]
    ts = xs_ref.shape[0]
    cw = cw_ref[...]

    @pl.when(i == 0)
    def _():
        wbf[0] = wb_ref[...].astype(BF16)
        wbf[1] = wc_ref[...].astype(BF16)
        wbf[2] = wh_ref[...].astype(BF16)
        x = xs_ref[...]
        u = _dot_nt(x, wbf[1]) * _dot_nt(x, wbf[2])
        sbuf[pl.ds(0, SUBLANES), :] = jnp.zeros((SUBLANES, u.shape[1]), F32)
        sbuf[pl.ds(SUBLANES, ts), :] = u
        s1buf[...] = jnp.zeros_like(s1buf)
        s2buf[...] = jnp.zeros_like(s2buf)
        for b in range(nb_s):
            s1buf[pl.ds(b * l_s, 1), :] = st_ref[b, pl.ds(1, 1), :]
            s2buf[pl.ds(b * l_s, 1), :] = st_ref[b, pl.ds(0, 1), :]
            s2buf[pl.ds(b * l_s + 1, 1), :] = st_ref[b, pl.ds(1, 1), :]
        rmod = lax.broadcasted_iota(jnp.int32, (ts, 1), 0) % l_s
        prev1 = jnp.where(rmod == 0, s1buf[...], sbuf[pl.ds(SUBLANES - 1, ts), :])
        prev2 = jnp.where(rmod < 2, s2buf[...], sbuf[pl.ds(SUBLANES - 2, ts), :])
        v = prev2 * cw[0:1, :] + prev1 * cw[1:2, :] + u * cw[2:3, :]
        os_ref[...] = (_dot_nt(x, wbf[0]) * v).astype(BF16)
        for b in range(nb_s):
            cas_ref[b] = sbuf[pl.ds(SUBLANES + (b + 1) * l_s - 2, 2), :]

    @pl.when(i % tiles_per_batch == 0)
    def _():
        ubuf[pl.ds(0, SUBLANES), :] = jnp.zeros((SUBLANES, ubuf.shape[1]), F32)

    x = xp_ref[...]
    u = _dot_nt(x, wbf[1]) * _dot_nt(x, wbf[2])
    ubuf[pl.ds(SUBLANES, tm), :] = u
    v = (ubuf[pl.ds(SUBLANES - 2, tm), :] * cw[0:1, :]
         + ubuf[pl.ds(SUBLANES - 1, tm), :] * cw[1:2, :] + u * cw[2:3, :])
    op_ref[...] = (_dot_nt(x, wbf[0]) * v).astype(BF16)
    ubuf[pl.ds(0, SUBLANES), :] = ubuf[pl.ds(tm, SUBLANES), :]

    @pl.when(i % tiles_per_batch == tiles_per_batch - 1)
    def _():
        cap_ref[0] = ubuf[pl.ds(SUBLANES + tm - 2, 2), :]


def _proj_a(xnp, xns, w_in, conv_w, state_s, n_batch_p, seq_p, d_conv):
    tp, d = xnp.shape
    ts = xns.shape[0]
    nb_s = state_s.shape[0]
    l_s = ts // nb_s
    tn = 512
    tm = _pick(seq_p, (1024, 512, 256, 128))
    tpb = seq_p // tm
    ncol = d_conv // tn
    kern = functools.partial(_proj_a_kernel, tiles_per_batch=tpb, nb_s=nb_s, l_s=l_s)
    return pl.pallas_call(
        kern,
        grid=(ncol, tp // tm),
        in_specs=[
            pl.BlockSpec((tm, d), lambda j, i: (i, 0)),
            pl.BlockSpec((ts, d), lambda j, i: (0, 0)),
            pl.BlockSpec((tn, d), lambda j, i: (j, 0)),
            pl.BlockSpec((tn, d), lambda j, i: (j + ncol, 0)),
            pl.BlockSpec((tn, d), lambda j, i: (j + 2 * ncol, 0)),
            pl.BlockSpec((CONV_A_K, tn), lambda j, i: (0, j)),
            pl.BlockSpec((nb_s, CONV_A_K - 1, tn), lambda j, i: (0, 0, j)),
        ],
        out_specs=[
            pl.BlockSpec((tm, tn), lambda j, i: (i, j)),
            pl.BlockSpec((ts, tn), lambda j, i: (0, j)),
            pl.BlockSpec((1, CONV_A_K - 1, tn), lambda j, i: (i // tpb, 0, j)),
            pl.BlockSpec((nb_s, CONV_A_K - 1, tn), lambda j, i: (0, 0, j)),
        ],
        out_shape=[
            jax.ShapeDtypeStruct((tp, d_conv), BF16),
            jax.ShapeDtypeStruct((ts, d_conv), BF16),
            jax.ShapeDtypeStruct((n_batch_p, CONV_A_K - 1, d_conv), F32),
            jax.ShapeDtypeStruct((nb_s, CONV_A_K - 1, d_conv), F32),
        ],
        scratch_shapes=[
            pltpu.VMEM((3, tn, d), BF16),
            pltpu.VMEM((SUBLANES + tm, tn), F32),
            pltpu.VMEM((SUBLANES + ts, tn), F32),
            pltpu.VMEM((ts, tn), F32),
            pltpu.VMEM((ts, tn), F32),
        ],
        compiler_params=_params(("arbitrary", "arbitrary")),
        name="proj_a",
    )(xnp, xns, w_in, w_in, w_in, conv_w, state_s)


def _proj_raw_kernel(xp_ref, xs_ref, w_ref, op_ref, os_ref, wbf, *, nsplit, width):
    def emit(x_ref, o_ref):
        acc = _dot_nt(x_ref[...], wbf[...])
        if nsplit == 0:
            o_ref[...] = acc
        else:
            for s in range(nsplit):
                o_ref[s] = acc[:, s * width:(s + 1) * width]

    @pl.when(pl.program_id(1) == 0)
    def _():
        wbf[...] = w_ref[...].astype(BF16)
        emit(xs_ref, os_ref)

    emit(xp_ref, op_ref)


def _proj_raw(xnp, xns, w, col0, ncols, width):
    tp, d = xnp.shape
    ts = xns.shape[0]
    tn = 1024
    tm = _pick(tp, (1024, 512, 256, 128))
    assert col0 % tn == 0 and ncols % tn == 0
    jb = col0 // tn
    if width == 0:
        nsplit = 0
        out_specs = [pl.BlockSpec((tm, tn), lambda j, i: (i, j)),
                     pl.BlockSpec((ts, tn), lambda j, i: (0, j))]
        out_shape = [jax.ShapeDtypeStruct((tp, ncols), F32), jax.ShapeDtypeStruct((ts, ncols), F32)]
    else:
        nsplit = tn // width
        out_specs = [pl.BlockSpec((nsplit, tm, width), lambda j, i: (j, i, 0)),
                     pl.BlockSpec((nsplit, ts, width), lambda j, i: (j, 0, 0))]
        out_shape = [jax.ShapeDtypeStruct((ncols // width, tp, width), F32),
                     jax.ShapeDtypeStruct((ncols // width, ts, width), F32)]
    kern = functools.partial(_proj_raw_kernel, nsplit=nsplit, width=width)
    return pl.pallas_call(
        kern,
        grid=(ncols // tn, tp // tm),
        in_specs=[
            pl.BlockSpec((tm, d), lambda j, i: (i, 0)),
            pl.BlockSpec((ts, d), lambda j, i: (0, 0)),
            pl.BlockSpec((tn, d), lambda j, i: (j + jb, 0)),
        ],
        out_specs=out_specs,
        out_shape=out_shape,
        scratch_shapes=[pltpu.VMEM((tn, d), BF16)],
        compiler_params=_params(("arbitrary", "arbitrary")),
        name=f"proj_raw_{col0}",
    )(xnp, xns, w)


def _ssd_kernel(*refs, q, has_state, nchunks):
    (z_ref, xs_ref, b_ref, c_ref, dt_ref, cwx, cwb, cwc, cbx, cbb, cbc,
     dtb_ref, alog_ref, dsk_ref, gn_ref, *rest) = refs
    if has_state:
        cpx, cpb, cpc, hprev, *rest = rest
    (y_ref, ocx, ocb, occ, oh, h_s, ccx, ccb, ccc, xbuf, bbuf, cbuf, acg, dtg, rowt) = rest
    c = pl.program_id(1)
    nheads = N_GROUPS * HEADS_PER_GROUP
    tail = CONV_M_K - 1

    @pl.when(c == 0)
    def _init():
        ccx[...] = jnp.zeros_like(ccx)
        ccb[...] = jnp.zeros_like(ccb)
        ccc[...] = jnp.zeros_like(ccc)
        if has_state:
            for g in range(N_GROUPS):
                ccx[g, pl.ds(SUBLANES - tail, tail), :] = cpx[g]
                ccb[g, pl.ds(SUBLANES - tail, tail), :] = cpb[g]
                ccc[g, pl.ds(SUBLANES - tail, tail), :] = cpc[g]
            h_s[...] = hprev[...]
        else:
            h_s[...] = jnp.zeros_like(h_s)

    dt = _softplus(dt_ref[...] + dtb_ref[...])
    da = dt * (-jnp.exp(alog_ref[...]))
    ri = lax.broadcasted_iota(jnp.int32, (q, q), 0)
    ci = lax.broadcasted_iota(jnp.int32, (q, q), 1)
    causal = ri >= ci
    tril = jnp.where(causal, 1.0, 0.0).astype(BF16)
    acum = sum(_dot(tril, p) for p in _split3(da))
    eye = jnp.where(lax.broadcasted_iota(jnp.int32, (nheads, nheads), 0)
                    == lax.broadcasted_iota(jnp.int32, (nheads, nheads), 1), 1.0, 0.0).astype(BF16)
    rowt[0] = sum(lax.dot_general(eye, p, NT_DIMS, preferred_element_type=F32) for p in _split3(acum))
    rowt[1] = sum(lax.dot_general(eye, p, NT_DIMS, preferred_element_type=F32) for p in _split3(dt))
    for g in range(N_GROUPS):
        acg[g] = acum[:, g * HEADS_PER_GROUP:(g + 1) * HEADS_PER_GROUP]
        dtg[g] = dt[:, g * HEADS_PER_GROUP:(g + 1) * HEADS_PER_GROUP]

    lane = lax.broadcasted_iota(jnp.int32, (1, LANES), 1)
    rowi = lax.broadcasted_iota(jnp.int32, (LANES, 1), 0)
    half_w = LANES // 2

    def conv(raw_ref, cc, buf, cw, cb, ost, g):
        buf[pl.ds(0, SUBLANES), :] = cc[g]
        raw = raw_ref[g]
        buf[pl.ds(SUBLANES, q), :] = raw
        w = cw[g]
        acc = raw * w[tail:tail + 1, :]
        for k in range(tail):
            acc = acc + buf[pl.ds(SUBLANES - tail + k, q), :] * w[k:k + 1, :]
        cc[g] = buf[pl.ds(q, SUBLANES), :]

        @pl.when(c == nchunks - 1)
        def _():
            ost[g] = buf[pl.ds(SUBLANES + q - tail, tail), :]

        return jax.nn.silu(acc + cb[g])

    def group_body(g, carry):
        xs = conv(xs_ref, ccx, xbuf, cwx, cbx, ocx, g)
        bc = conv(b_ref, ccb, bbuf, cwb, cbb, ocb, g)
        cc_ = conv(c_ref, ccc, cbuf, cwc, cbc, occ, g)
        bb = bc.astype(BF16)
        cb_ = lax.dot_general(cc_.astype(BF16), bb, NT_DIMS, preferred_element_type=F32)
        ac8 = acg[g]
        dt8 = dtg[g]
        dsk = dsk_ref[g]
        ys = []
        for pair in range(HEADS_PER_GROUP // 2):
            sl = slice(pair * LANES, (pair + 1) * LANES)
            xp = xs[:, sl]
            hp = h_s[g, pl.ds(pair * LANES, LANES), :]
            ypair = dsk[:, sl] * xp
            wpair = None
            dpair = None
            for half in range(2):
                rr = 2 * pair + half
                acol = jnp.broadcast_to(ac8[:, rr:rr + 1], (q, LANES))
                dcol = jnp.broadcast_to(dt8[:, rr:rr + 1], (q, LANES))
                arow = rowt[0, pl.ds(g * HEADS_PER_GROUP + rr, 1), :]
                drow = rowt[1, pl.ds(g * HEADS_PER_GROUP + rr, 1), :]
                decay = jnp.exp(jnp.where(causal, acol[:, :q] - arow, -jnp.inf))
                m = (cb_ * decay * drow).astype(BF16)
                in_half = (lane < half_w) if half == 0 else (lane >= half_w)
                ypair = ypair + _dot(m, jnp.where(in_half, xp, 0.0).astype(BF16))
                ec = (jnp.exp(acol) * cc_).astype(BF16)
                row_half = (rowi < half_w) if half == 0 else (rowi >= half_w)
                hm = jnp.where(row_half, hp, 0.0).astype(BF16)
                ypair = ypair + lax.dot_general(ec, hm, NT_DIMS, preferred_element_type=F32)
                alast = acol[q - 1:q, :]
                wcol = jnp.exp(alast - acol) * dcol
                dlast = jnp.broadcast_to(jnp.exp(alast), (LANES, LANES))
                if half == 0:
                    wpair, dpair = wcol, dlast
                else:
                    wpair = jnp.where(lane < half_w, wpair, wcol)
                    dpair = jnp.where(rowi < half_w, dpair, dlast)
            st = lax.dot_general((xp * wpair).astype(BF16), bb, TN_DIMS, preferred_element_type=F32)
            h_s[g, pl.ds(pair * LANES, LANES), :] = hp * dpair + st
            ys.append(ypair)
        yg = jnp.concatenate(ys, axis=1)
        hh = yg * jax.nn.silu(z_ref[g])
        ms = jnp.mean(hh * hh, axis=-1, keepdims=True)
        y_ref[g] = (hh * lax.rsqrt(ms + EPS) * gn_ref[g]).astype(BF16)
        return carry

    lax.fori_loop(0, N_GROUPS, group_body, 0)

    @pl.when(c == nchunks - 1)
    def _():
        oh[...] = h_s[...]


def _ssd(zx, bc, dt_raw, prm, n_batch, seq, q, state=None):
    nchunks = seq // q
    nheads = N_GROUPS * HEADS_PER_GROUP
    has_state = state is not None
    g8 = N_GROUPS

    def tok(first):
        return lambda b, c: (first, b * nchunks + c, 0)

    def const3(b, c):
        return (0, 0, 0)

    in_specs = [
        pl.BlockSpec((g8, q, GROUP_W), tok(0)),
        pl.BlockSpec((g8, q, GROUP_W), tok(1)),
        pl.BlockSpec((g8, q, D_STATE), tok(0)),
        pl.BlockSpec((g8, q, D_STATE), tok(1)),
        pl.BlockSpec((q, nheads), lambda b, c: (b * nchunks + c, 0)),
        pl.BlockSpec((g8, CONV_M_K, GROUP_W), const3),
        pl.BlockSpec((g8, CONV_M_K, D_STATE), const3),
        pl.BlockSpec((g8, CONV_M_K, D_STATE), const3),
        pl.BlockSpec((g8, 1, GROUP_W), const3),
        pl.BlockSpec((g8, 1, D_STATE), const3),
        pl.BlockSpec((g8, 1, D_STATE), const3),
        pl.BlockSpec((1, nheads), lambda b, c: (0, 0)),
        pl.BlockSpec((1, nheads), lambda b, c: (0, 0)),
        pl.BlockSpec((g8, 1, GROUP_W), const3),
        pl.BlockSpec((g8, 1, GROUP_W), const3),
    ]
    args = [zx, zx, bc, bc, dt_raw, prm["cwx"], prm["cwb"], prm["cwc"], prm["cbx"], prm["cbb"], prm["cbc"],
            prm["dtb"], prm["alog"], prm["dsk"], prm["gn"]]
    tail = CONV_M_K - 1

    def per_batch(shape):
        return pl.BlockSpec((None,) + shape, lambda b, c: (b,) + (0,) * len(shape))

    if has_state:
        in_specs += [per_batch((g8, tail, GROUP_W)), per_batch((g8, tail, D_STATE)),
                     per_batch((g8, tail, D_STATE)), per_batch((g8, GROUP_W, D_STATE))]
        args += list(state)
    t = n_batch * seq
    out_specs = [
        pl.BlockSpec((g8, q, GROUP_W), lambda b, c: (0, b * nchunks + c, 0)),
        per_batch((g8, tail, GROUP_W)), per_batch((g8, tail, D_STATE)), per_batch((g8, tail, D_STATE)),
        per_batch((g8, GROUP_W, D_STATE)),
    ]
    out_shape = [
        jax.ShapeDtypeStruct((g8, t, GROUP_W), BF16),
        jax.ShapeDtypeStruct((n_batch, g8, tail, GROUP_W), F32),
        jax.ShapeDtypeStruct((n_batch, g8, tail, D_STATE), F32),
        jax.ShapeDtypeStruct((n_batch, g8, tail, D_STATE), F32),
        jax.ShapeDtypeStruct((n_batch, g8, GROUP_W, D_STATE), F32),
    ]
    scratch = [
        pltpu.VMEM((g8, GROUP_W, D_STATE), F32),
        pltpu.VMEM((g8, SUBLANES, GROUP_W), F32),
        pltpu.VMEM((g8, SUBLANES, D_STATE), F32),
        pltpu.VMEM((g8, SUBLANES, D_STATE), F32),
        pltpu.VMEM((SUBLANES + q, GROUP_W), F32),
        pltpu.VMEM((SUBLANES + q, D_STATE), F32),
        pltpu.VMEM((SUBLANES + q, D_STATE), F32),
        pltpu.VMEM((g8, q, HEADS_PER_GROUP), F32),
        pltpu.VMEM((g8, q, HEADS_PER_GROUP), F32),
        pltpu.VMEM((2, nheads, q), F32),
    ]
    kern = functools.partial(_ssd_kernel, q=q, has_state=has_state, nchunks=nchunks)
    return pl.pallas_call(
        kern,
        grid=(n_batch, nchunks),
        in_specs=in_specs,
        out_specs=out_specs,
        out_shape=out_shape,
        scratch_shapes=scratch,
        compiler_params=_params(("arbitrary", "arbitrary")),
        name="ssd_state" if has_state else "ssd",
    )(*args)


def _merge_kernel(ap_ref, as_ref, yp_ref, ys_ref, gap_ref, gmp_ref, gas_ref, gms_ref, wa_ref, wm_ref,
                  op_ref, os_ref, wab, wmb):
    def emit(a_ref, y_ref, ga_ref, gm_ref, o_ref):
        oa = _dot(a_ref[...], wab[...])
        om = _dot(y_ref[0], wmb[0])
        for g in range(1, N_GROUPS):
            om = om + _dot(y_ref[g], wmb[g])
        o_ref[...] = (jax.nn.sigmoid(ga_ref[...]) * oa + jax.nn.sigmoid(gm_ref[...]) * om).astype(BF16)

    @pl.when(pl.program_id(1) == 0)
    def _():
        wab[...] = wa_ref[...].astype(BF16)
        wmb[...] = wm_ref[...].astype(BF16)
        emit(as_ref, ys_ref, gas_ref, gms_ref, os_ref)

    emit(ap_ref, yp_ref, gap_ref, gmp_ref, op_ref)


def _merge(abv_p, abv_s, y_p, y_s, g_p, g_s, w_a_out, w_m_out3):
    tp, dc = abv_p.shape
    ts = abv_s.shape[0]
    dm = w_a_out.shape[1]
    tn = 512
    tm = _pick(tp, (512, 256, 128))
    ncol = dm // tn
    return pl.pallas_call(
        _merge_kernel,
        grid=(ncol, tp // tm),
        in_specs=[
            pl.BlockSpec((tm, dc), lambda j, i: (i, 0)),
            pl.BlockSpec((ts, dc), lambda j, i: (0, 0)),
            pl.BlockSpec((N_GROUPS, tm, GROUP_W), lambda j, i: (0, i, 0)),
            pl.BlockSpec((N_GROUPS, ts, GROUP_W), lambda j, i: (0, 0, 0)),
            pl.BlockSpec((tm, tn), lambda j, i: (i, j)),
            pl.BlockSpec((tm, tn), lambda j, i: (i, j + ncol)),
            pl.BlockSpec((ts, tn), lambda j, i: (0, j)),
            pl.BlockSpec((ts, tn), lambda j, i: (0, j + ncol)),
            pl.BlockSpec((dc, tn), lambda j, i: (0, j)),
            pl.BlockSpec((N_GROUPS, GROUP_W, tn), lambda j, i: (0, 0, j)),
        ],
        out_specs=[
            pl.BlockSpec((tm, tn), lambda j, i: (i, j)),
            pl.BlockSpec((ts, tn), lambda j, i: (0, j)),
        ],
        out_shape=[jax.ShapeDtypeStruct((tp, dm), BF16), jax.ShapeDtypeStruct((ts, dm), BF16)],
        scratch_shapes=[pltpu.VMEM((dc, tn), BF16), pltpu.VMEM((N_GROUPS, GROUP_W, tn), BF16)],
        compiler_params=_params(("arbitrary", "arbitrary")),
        name="merge",
    )(abv_p, abv_s, y_p, y_s, g_p, g_p, g_s, g_s, w_a_out, w_m_out3)


def _res_kernel(ap_ref, as_ref, rp_ref, rs_ref, w_ref, g_ref, *outs, n_prompt_tiles, merged_out, ts):
    if merged_out:
        h_ref, hn_ref, wbf = outs
    else:
        hp_ref, hs_ref, hnp_ref, hns_ref, wbf = outs
    i = pl.program_id(0)

    @pl.when(i == 0)
    def _():
        wbf[...] = w_ref[...].astype(BF16)

    def emit(a_ref, r_ref, store_h, store_hn):
        h = r_ref[...] + _dot(a_ref[...], wbf[...])
        store_h(h)
        store_hn(_rms(h, g_ref[...]))

    if merged_out:
        @pl.when(i < n_prompt_tiles)
        def _():
            def sh(h):
                h_ref[...] = h

            def shn(hn):
                hn_ref[...] = hn

            emit(ap_ref, rp_ref, sh, shn)

        @pl.when(i == n_prompt_tiles)
        def _():
            def sh(h):
                h_ref[pl.ds(0, ts), :] = h

            def shn(hn):
                hn_ref[pl.ds(0, ts), :] = hn

            emit(as_ref, rs_ref, sh, shn)
    else:
        def shp(h):
            hp_ref[...] = h

        def shnp(hn):
            hnp_ref[...] = hn.astype(hnp_ref.dtype)

        emit(ap_ref, rp_ref, shp, shnp)

        @pl.when(i == 0)
        def _():
            def shs(h):
                hs_ref[...] = h

            def shns(hn):
                hns_ref[...] = hn.astype(hns_ref.dtype)

            emit(as_ref, rs_ref, shs, shns)


def _res(a_p, a_s, r_p, r_s, w, g, merged_out, hn_dtype):
    tp, d = a_p.shape
    ts = a_s.shape[0]
    tm = 256
    assert tp % tm == 0 and ts <= tm
    npt = tp // tm
    last = npt - 1
    in_specs = [
        pl.BlockSpec((tm, d), lambda i: (jnp.minimum(i, last), 0)),
        pl.BlockSpec((ts, d), lambda i: (0, 0)),
        pl.BlockSpec((tm, d), lambda i: (jnp.minimum(i, last), 0)),
        pl.BlockSpec((ts, d), lambda i: (0, 0)),
        pl.BlockSpec((d, d), lambda i: (0, 0), pipeline_mode=pl.Buffered(1)),
        pl.BlockSpec((1, d), lambda i: (0, 0)),
    ]
    if merged_out:
        grid = (npt + 1,)
        out_specs = [pl.BlockSpec((tm, d), lambda i: (i, 0)), pl.BlockSpec((tm, d), lambda i: (i, 0))]
        out_shape = [jax.ShapeDtypeStruct((tp + ts, d), F32), jax.ShapeDtypeStruct((tp + ts, d), hn_dtype)]
    else:
        grid = (npt,)
        out_specs = [pl.BlockSpec((tm, d), lambda i: (i, 0)), pl.BlockSpec((ts, d), lambda i: (0, 0)),
                     pl.BlockSpec((tm, d), lambda i: (i, 0)), pl.BlockSpec((ts, d), lambda i: (0, 0))]
        out_shape = [jax.ShapeDtypeStruct((tp, d), F32), jax.ShapeDtypeStruct((ts, d), F32),
                     jax.ShapeDtypeStruct((tp, d), hn_dtype), jax.ShapeDtypeStruct((ts, d), hn_dtype)]
    kern = functools.partial(_res_kernel, n_prompt_tiles=npt, merged_out=merged_out, ts=ts)
    return pl.pallas_call(
        kern,
        grid=grid,
        in_specs=in_specs,
        out_specs=out_specs,
        out_shape=out_shape,
        scratch_shapes=[pltpu.VMEM((d, d), BF16)],
        compiler_params=_params(("arbitrary",)),
        name="res_merged" if merged_out else "res",
    )(a_p, a_s, r_p, r_s, w, g)


def _kv_kernel(m_ref, g_ref, wk_ref, wv_ref, k_ref, v_ref, mn):
    @pl.when(pl.program_id(0) == 0)
    def _():
        mn[...] = _rms(m_ref[...], g_ref[...]).astype(BF16)

    k_ref[...] = _dot(mn[...], wk_ref[...].astype(BF16))
    v_ref[...] = _dot(mn[...], wv_ref[...].astype(BF16))


def _memory_kv(mem2d, g, w_k, w_v):
    m, d = mem2d.shape
    tn = 512
    return pl.pallas_call(
        _kv_kernel,
        grid=(d // tn,),
        in_specs=[
            pl.BlockSpec((m, d), lambda j: (0, 0)),
            pl.BlockSpec((1, d), lambda j: (0, 0)),
            pl.BlockSpec((d, tn), lambda j: (0, j)),
            pl.BlockSpec((d, tn), lambda j: (0, j)),
        ],
        out_specs=[pl.BlockSpec((m, tn), lambda j: (0, j)), pl.BlockSpec((m, tn), lambda j: (0, j))],
        out_shape=[jax.ShapeDtypeStruct((m, d), F32), jax.ShapeDtypeStruct((m, d), F32)],
        scratch_shapes=[pltpu.VMEM((m, d), BF16)],
        compiler_params=_params(("arbitrary",)),
        name="memory_kv",
    )(mem2d, g, w_k, w_v)


def _attn_kernel(hn_ref, k_ref, v_ref, wq_ref, o_ref, wqb, kb, vb):
    b = pl.program_id(0)
    i = pl.program_id(1)

    @pl.when((b == 0) & (i == 0))
    def _():
        wqb[...] = wq_ref[...].astype(BF16)

    @pl.when(i == 0)
    def _():
        kb[...] = k_ref[...].astype(BF16)
        vb[...] = v_ref[...].astype(BF16)

    d = wqb.shape[1]
    dh = d // N_XHEADS
    q = _dot(hn_ref[...], wqb[...])
    outs = []
    for h in range(N_XHEADS):
        sl = slice(h * dh, (h + 1) * dh)
        s = lax.dot_general(q[:, sl].astype(BF16), kb[:, sl], NT_DIMS, preferred_element_type=F32)
        s = s * (dh ** -0.5)
        e = jnp.exp(s - jnp.max(s, axis=-1, keepdims=True))
        p = e / jnp.sum(e, axis=-1, keepdims=True)
        outs.append(_dot(p.astype(BF16), vb[:, sl]))
    o_ref[...] = jnp.concatenate(outs, axis=1).astype(BF16)


def _attn(hn, k3, v3, w_q, n_batch, seq):
    t, d = hn.shape
    nm = k3.shape[1]
    tm = _pick(seq, (256, 128, 64, 32, 16))
    tpb = seq // tm
    return pl.pallas_call(
        _attn_kernel,
        grid=(n_batch, tpb),
        in_specs=[
            pl.BlockSpec((tm, d), lambda b, i: (b * tpb + i, 0)),
            pl.BlockSpec((None, nm, d), lambda b, i: (b, 0, 0)),
            pl.BlockSpec((None, nm, d), lambda b, i: (b, 0, 0)),
            pl.BlockSpec((d, d), lambda b, i: (0, 0), pipeline_mode=pl.Buffered(1)),
        ],
        out_specs=pl.BlockSpec((tm, d), lambda b, i: (b * tpb + i, 0)),
        out_shape=jax.ShapeDtypeStruct((t, d), BF16),
        scratch_shapes=[pltpu.VMEM((d, d), BF16), pltpu.VMEM((nm, d), BF16), pltpu.VMEM((nm, d), BF16)],
        compiler_params=_params(("arbitrary", "arbitrary")),
        name=f"attn_{seq}",
    )(hn, k3, v3, w_q)


def _router_kernel(x_ref, w_ref, b_ref, ri_ref, rw_ref):
    logits = _dot(x_ref[...].astype(BF16), w_ref[...].astype(BF16)) + b_ref[...]
    lane_i = lax.broadcasted_iota(jnp.int32, logits.shape, 1)
    lane = lane_i.astype(F32)
    ninf = -jnp.inf
    big = float(LANES)
    is_g = lane < N_EXPERT_GROUPS
    gl = jnp.where(is_g, logits, ninf)
    gmax = jnp.max(gl, axis=-1, keepdims=True)
    gsel = jnp.min(jnp.where(gl == gmax, lane, big), axis=-1, keepdims=True)
    pg = 1.0 / jnp.sum(jnp.where(is_g, jnp.exp(gl - gmax), 0.0), axis=-1, keepdims=True)
    lo = N_EXPERT_GROUPS + EXPERTS_PER_GROUP * gsel
    el = jnp.where(lane >= lo, jnp.where(lane < lo + EXPERTS_PER_GROUP, logits, ninf), ninf)
    m1 = jnp.max(el, axis=-1, keepdims=True)
    i1 = jnp.min(jnp.where(el == m1, lane, big), axis=-1, keepdims=True)
    el2 = jnp.where(lane == i1, ninf, el)
    m2 = jnp.max(el2, axis=-1, keepdims=True)
    i2 = jnp.min(jnp.where(el2 == m2, lane, big), axis=-1, keepdims=True)
    e = jnp.exp(m2 - m1)
    w1 = pg / (1.0 + e)
    w2 = pg * e / (1.0 + e)
    ri_ref[...] = jnp.where(lane_i == 0, i1 - N_EXPERT_GROUPS,
                            jnp.where(lane_i == 1, i2 - N_EXPERT_GROUPS, 0.0)).astype(jnp.int32)
    rw_ref[...] = jnp.where(lane_i == 0, w1, jnp.where(lane_i == 1, w2, 0.0))


def _router(hn_all, w_r, b_r):
    t, d = hn_all.shape
    tm = _pick(t, (640, 512, 384, 256, 128))
    return pl.pallas_call(
        _router_kernel,
        grid=(t // tm,),
        in_specs=[
            pl.BlockSpec((tm, d), lambda i: (i, 0)),
            pl.BlockSpec((d, LANES), lambda i: (0, 0)),
            pl.BlockSpec((1, LANES), lambda i: (0, 0)),
        ],
        out_specs=[pl.BlockSpec((tm, LANES), lambda i: (i, 0)), pl.BlockSpec((tm, LANES), lambda i: (i, 0))],
        out_shape=[jax.ShapeDtypeStruct((t, LANES), jnp.int32), jax.ShapeDtypeStruct((t, LANES), F32)],
        compiler_params=_params(("arbitrary",)),
        name="router",
    )(hn_all, w_r, b_r)


def _row_copy(src, dst, s, d, sem):
    return pltpu.make_async_copy(src.at[pl.ds(s, 1)], dst.at[pl.ds(d, 1)], sem)


def _dispatch_kernel(dest_ref, x_ref, o_ref, sem, *, chunk):
    base = pl.program_id(0) * chunk

    def issue(r, carry):
        t = base + r
        _row_copy(x_ref, o_ref, r, dest_ref[2 * t], sem).start()
        _row_copy(x_ref, o_ref, r, dest_ref[2 * t + 1], sem).start()
        return carry

    lax.fori_loop(0, chunk, issue, 0)

    def drain(r, carry):
        _row_copy(x_ref, o_ref, 0, 0, sem).wait()
        _row_copy(x_ref, o_ref, 0, 0, sem).wait()
        return carry

    lax.fori_loop(0, chunk, drain, 0)


def _dispatch(dest, hn_all):
    t, d = hn_all.shape
    chunk = _pick(t, (640, 512, 384, 256, 128))
    kern = functools.partial(_dispatch_kernel, chunk=chunk)
    return pl.pallas_call(
        kern,
        grid_spec=pltpu.PrefetchScalarGridSpec(
            num_scalar_prefetch=1,
            grid=(t // chunk,),
            in_specs=[pl.BlockSpec((chunk, d), lambda i, dref: (i, 0))],
            out_specs=pl.BlockSpec(memory_space=pl.ANY),
            scratch_shapes=[pltpu.SemaphoreType.DMA(())],
        ),
        out_shape=jax.ShapeDtypeStruct((2 * t, d), F32),
        compiler_params=_params(("arbitrary",)),
        name="dispatch",
    )(dest, hn_all)


def _expert_kernel(blk_ref, exp_ref, lo_ref, hi_ref, first_ref, x_ref, wg_ref, wu_ref, wd_ref, y_ref, xb):
    it = pl.program_id(0)
    k = pl.program_id(1)
    lo = lo_ref[it]
    hi = hi_ref[it]

    @pl.when((k == 0) & (first_ref[it] == 1))
    def _():
        y_ref[...] = jnp.zeros_like(y_ref)

    @pl.when(hi > lo)
    def _():
        @pl.when(k == 0)
        def _():
            rows = lax.broadcasted_iota(jnp.int32, (x_ref.shape[0], 1), 0)
            xb[...] = jnp.where((rows >= lo) & (rows < hi), x_ref[...], 0.0).astype(BF16)

        x = xb[...]
        gate = _dot(x, wg_ref[...].astype(BF16))
        up = _dot(x, wu_ref[...].astype(BF16))
        hid = (jax.nn.silu(gate) * up).astype(BF16)
        y_ref[...] += _dot(hid, wd_ref[...].astype(BF16))


def _experts(items, x_sorted, w_gate, w_up, w_down, rows_per_block, kchunk):
    blk, exp, lo, hi, first = items
    n_items = blk.shape[0]
    npairs, d = x_sorted.shape
    de = w_gate.shape[2]
    nk = de // kchunk
    r = rows_per_block

    def keff(it, k, lo_r, hi_r):
        return jnp.where(hi_r[it] > lo_r[it], k, nk - 1)

    return pl.pallas_call(
        _expert_kernel,
        grid_spec=pltpu.PrefetchScalarGridSpec(
            num_scalar_prefetch=5,
            grid=(n_items, nk),
            in_specs=[
                pl.BlockSpec((r, d), lambda it, k, b, e, l, h, f: (b[it], 0)),
                pl.BlockSpec((None, d, kchunk), lambda it, k, b, e, l, h, f: (e[it], 0, keff(it, k, l, h))),
                pl.BlockSpec((None, d, kchunk), lambda it, k, b, e, l, h, f: (e[it], 0, keff(it, k, l, h))),
                pl.BlockSpec((None, kchunk, d), lambda it, k, b, e, l, h, f: (e[it], keff(it, k, l, h), 0)),
            ],
            out_specs=pl.BlockSpec((r, d), lambda it, k, b, e, l, h, f: (b[it], 0)),
            scratch_shapes=[pltpu.VMEM((r, d), BF16)],
        ),
        out_shape=jax.ShapeDtypeStruct((npairs, d), F32),
        compiler_params=_params(("arbitrary", "arbitrary")),
        name="experts",
    )(blk, exp, lo, hi, first, x_sorted, w_gate, w_up, w_down)


def _final_kernel(dest_ref, h_ref, rw_ref, ys_ref, g_ref, yp_ref, yss_ref, ya, yb, sem, *,
                  tm, n_prompt_tiles, ts):
    i = pl.program_id(0)
    base = i * tm
    nrows = jnp.where(i < n_prompt_tiles, tm, ts)

    def issue(r, carry):
        t = base + r
        _row_copy(ys_ref, ya, dest_ref[2 * t], r, sem).start()
        _row_copy(ys_ref, yb, dest_ref[2 * t + 1], r, sem).start()
        return carry

    lax.fori_loop(0, nrows, issue, 0)

    def drain(r, carry):
        _row_copy(ys_ref, ya, 0, 0, sem).wait()
        _row_copy(ys_ref, yb, 0, 0, sem).wait()
        return carry

    lax.fori_loop(0, nrows, drain, 0)

    def emit(n, o_ref):
        w = rw_ref[pl.ds(0, n), :]
        h = h_ref[pl.ds(0, n), :] + w[:, 0:1] * ya[pl.ds(0, n), :] + w[:, 1:2] * yb[pl.ds(0, n), :]
        o_ref[...] = _rms(h, g_ref[...])

    @pl.when(i < n_prompt_tiles)
    def _():
        emit(tm, yp_ref)

    @pl.when(i == n_prompt_tiles)
    def _():
        emit(ts, yss_ref)


def _final(dest, h_all, rw, y_sorted, g, tp, ts):
    t, d = h_all.shape
    tm = 256
    assert tp % tm == 0 and ts <= tm and t == tp + ts
    npt = tp // tm
    kern = functools.partial(_final_kernel, tm=tm, n_prompt_tiles=npt, ts=ts)
    return pl.pallas_call(
        kern,
        grid_spec=pltpu.PrefetchScalarGridSpec(
            num_scalar_prefetch=1,
            grid=(npt + 1,),
            in_specs=[
                pl.BlockSpec((tm, d), lambda i, dref: (i, 0)),
                pl.BlockSpec((tm, LANES), lambda i, dref: (i, 0)),
                pl.BlockSpec(memory_space=pl.ANY),
                pl.BlockSpec((1, d), lambda i, dref: (0, 0)),
            ],
            out_specs=[
                pl.BlockSpec((tm, d), lambda i, dref: (jnp.minimum(i, npt - 1), 0)),
                pl.BlockSpec((ts, d), lambda i, dref: (0, 0)),
            ],
            scratch_shapes=[pltpu.VMEM((tm, d), F32), pltpu.VMEM((tm, d), F32), pltpu.SemaphoreType.DMA(())],
        ),
        out_shape=[jax.ShapeDtypeStruct((tp, d), F32), jax.ShapeDtypeStruct((ts, d), F32)],
        compiler_params=_params(("arbitrary",)),
        name="final",
    )(dest, h_all, rw, y_sorted, g)


def _routing_tables(eid, rows_per_block):
    npairs = eid.shape[0]
    r = rows_per_block
    i32 = jnp.int32
    onehot = (eid[:, None] == jnp.arange(N_EXPERTS, dtype=i32)[None, :]).astype(i32)
    csum = jnp.cumsum(onehot, axis=0)
    counts = csum[-1]
    starts = jnp.cumsum(counts) - counts
    dest = jnp.sum(onehot * (csum - 1 + starts[None, :]), axis=1).astype(i32)

    nblocks = -(-npairs // r)
    a = jnp.arange(nblocks, dtype=i32) * r
    s = starts[1:].astype(i32)
    pos_a = jnp.arange(nblocks, dtype=i32) + jnp.sum((s[None, :] < a[:, None]).astype(i32), axis=1)
    pos_s = jnp.arange(N_EXPERTS - 1, dtype=i32) + jnp.sum((a[None, :] <= s[:, None]).astype(i32), axis=1)
    n_items = nblocks + N_EXPERTS - 1
    idx = jnp.arange(n_items, dtype=i32)
    bps = (jnp.sum(jnp.where(pos_a[None, :] == idx[:, None], a[None, :], 0), axis=1)
           + jnp.sum(jnp.where(pos_s[None, :] == idx[:, None], s[None, :], 0), axis=1)).astype(i32)
    ends = jnp.concatenate([bps[1:], jnp.array([npairs], i32)])
    blk = jnp.minimum(bps // r, nblocks - 1)
    length = ends - bps
    exp = jnp.sum((starts[None, :] <= bps[:, None]).astype(i32), axis=1) - 1
    last_real = jnp.max(jnp.where((idx[None, :] <= idx[:, None]) & (length[None, :] > 0), idx[None, :], 0), axis=1)
    exp = jnp.sum(jnp.where(idx[None, :] == last_real[:, None], exp[None, :], 0), axis=1).astype(i32)
    lo = bps - blk * r
    hi = lo + length
    first = jnp.concatenate([jnp.ones((1,), i32), (blk[1:] != blk[:-1]).astype(i32)])
    return dest, (blk.astype(i32), exp, lo.astype(i32), hi.astype(i32), first)


def _group_major(a, width):
    lead = a.shape[:-1]
    g = a.shape[-1] // width
    return jnp.moveaxis(a.reshape(lead + (g, width)), -2, 0)


def kernel(x_prompt, x_sample, cache_mem_k, cache_mem_v, state_conv_a, state_conv_m, state_ssm, mem_prompt,
           norm_mix, w_in, conv_a_w, w_a_out, conv_m_w, conv_m_b, dt_bias, a_log, d_skip, ssm_norm,
           w_m_out, w_o, norm_cross, norm_mem, w_q, w_k, w_v, w_co, norm_ffn, w_rg, b_rg, w_re, b_re,
           w_gate, w_up, w_down, norm_final):
    depth = w_in.shape[0]
    assert depth == 1, "single-layer step"
    nbp, seq_p, d = x_prompt.shape
    nbs, seq_s, _ = x_sample.shape
    tp, ts = nbp * seq_p, nbs * seq_s
    n_mem = mem_prompt.shape[1]
    d_conv = conv_a_w.shape[2]
    d_inner = w_m_out.shape[1]
    nheads = dt_bias.shape[1]
    bc_w = N_GROUPS * D_STATE
    l = 0

    xp = x_prompt.reshape(tp, d)
    xs = x_sample.reshape(ts, d)
    w_in_t = jnp.swapaxes(w_in[l], 0, 1)
    col_z = 3 * d_conv
    col_x = col_z + d_inner
    col_b = col_x + d_inner
    col_dt = col_b + 2 * bc_w
    col_g = col_dt + nheads
    w_dt = w_in_t[col_dt:col_g]
    w_gates = w_in_t[col_g:]

    mk2, mv2 = _memory_kv(mem_prompt.reshape(nbp * n_mem, d), norm_mem[l][None], w_k[l], w_v[l])

    xnp, xns, dt_p, dt_s = _norm_dt(xp, xs, norm_mix[l][None], w_dt)
    abv_p, abv_s, ca_p, ca_s = _proj_a(xnp, xns, w_in_t, conv_a_w[l], state_conv_a[l], nbp, seq_p, d_conv)
    zx_p, zx_s = _proj_raw(xnp, xns, w_in_t, col_z, 2 * d_inner, GROUP_W)
    bc_p, bc_s = _proj_raw(xnp, xns, w_in_t, col_b, 2 * bc_w, D_STATE)
    g_p, g_s = _proj_raw(xnp, xns, w_gates, 0, 2 * d, 0)

    cw = conv_m_w[l]
    cbias = conv_m_b[l][None]
    prm = {
        "cwx": _group_major(cw[:, :d_inner], GROUP_W),
        "cwb": _group_major(cw[:, d_inner:d_inner + bc_w], D_STATE),
        "cwc": _group_major(cw[:, d_inner + bc_w:], D_STATE),
        "cbx": _group_major(cbias[:, :d_inner], GROUP_W),
        "cbb": _group_major(cbias[:, d_inner:d_inner + bc_w], D_STATE),
        "cbc": _group_major(cbias[:, d_inner + bc_w:], D_STATE),
        "dtb": dt_bias[l][None],
        "alog": a_log[l][None],
        "dsk": _group_major(jnp.repeat(d_skip[l], HEAD_DIM)[None], GROUP_W),
        "gn": _group_major(ssm_norm[l][None], GROUP_W),
    }
    scm = state_conv_m[l]
    state_s = (
        jnp.moveaxis(_group_major(scm[..., :d_inner], GROUP_W), 0, 1),
        jnp.moveaxis(_group_major(scm[..., d_inner:d_inner + bc_w], D_STATE), 0, 1),
        jnp.moveaxis(_group_major(scm[..., d_inner + bc_w:], D_STATE), 0, 1),
        state_ssm[l].reshape(nbs, N_GROUPS, GROUP_W, D_STATE),
    )
    q_p = _pick(seq_p, (128, 64, 32, 16, 8))
    y_p, cx_p, cb_p, cc_p, h_p = _ssd(zx_p, bc_p, dt_p, prm, nbp, seq_p, q_p)
    y_s, cx_s, cb_s, cc_s, h_s = _ssd(zx_s, bc_s, dt_s, prm, nbs, seq_s, seq_s, state=state_s)

    def conv_m_state(cx, cb_, cc_):
        def flat(a):
            return jnp.moveaxis(a, 1, 2).reshape(a.shape[0], a.shape[2], -1)
        return jnp.concatenate([flat(cx), flat(cb_), flat(cc_)], axis=-1)

    merged_p, merged_s = _merge(abv_p, abv_s, y_p, y_s, g_p, g_s, w_a_out[l],
                                w_m_out[l].reshape(N_GROUPS, GROUP_W, d))
    h1_p, h1_s, hn1_p, hn1_s = _res(merged_p, merged_s, xp, xs, w_o[l], norm_cross[l][None], False, BF16)

    att_p = _attn(hn1_p, mk2.reshape(nbp, n_mem, d), mv2.reshape(nbp, n_mem, d), w_q[l], nbp, seq_p)
    att_s = _attn(hn1_s, cache_mem_k[l].reshape(nbs, n_mem, d), cache_mem_v[l].reshape(nbs, n_mem, d),
                  w_q[l], nbs, seq_s)
    h2_all, hn2_all = _res(att_p, att_s, h1_p, h1_s, w_co[l], norm_ffn[l][None], True, F32)

    npad = LANES - N_EXPERT_GROUPS - N_EXPERTS
    w_r = jnp.concatenate([w_rg[l], w_re[l], jnp.zeros((d, npad), F32)], axis=1)
    b_r = jnp.concatenate([b_rg[l], b_re[l], jnp.zeros((npad,), F32)])[None]
    ri, rw = _router(hn2_all, w_r, b_r)
    rows_per_block = 512
    dest, items = _routing_tables(ri[:, :2].reshape(-1), rows_per_block)
    x_sorted = _dispatch(dest, hn2_all)
    y_sorted = _experts(items, x_sorted, w_gate[l], w_up[l], w_down[l], rows_per_block, 256)
    y_prompt, y_sample = _final(dest, h2_all, rw, y_sorted, norm_final[None], tp, ts)

    xh = d // N_XHEADS
    return (
        y_prompt.reshape(nbp, seq_p, d),
        y_sample.reshape(nbs, seq_s, d),
        mk2.reshape(1, nbp, n_mem, N_XHEADS, xh),
        mv2.reshape(1, nbp, n_mem, N_XHEADS, xh),
        ca_p[None],
        conv_m_state(cx_p, cb_p, cc_p)[None],
        h_p.reshape(1, nbp, nheads, HEAD_DIM, D_STATE),
        ca_s[None],
        conv_m_state(cx_s, cb_s, cc_s)[None],
        h_s.reshape(1, nbs, nheads, HEAD_DIM, D_STATE),
    )
```

```python
import functools

import jax
import jax.numpy as jnp
from jax import lax
from jax.experimental import pallas as pl
from jax.experimental.pallas import tpu as pltpu

F32 = jnp.float32
BF16 = jnp.bfloat16
EPS = 1e-6

V7X_VMEM_BYTES = 64 * 1024 * 1024
VMEM_LIMIT = V7X_VMEM_BYTES - 8 * 1024 * 1024
LANES = 128
SUBLANES = 8

N_GROUPS = 8
HEADS_PER_GROUP = 8
HEAD_DIM = 64
D_STATE = 128
GROUP_W = HEADS_PER_GROUP * HEAD_DIM
N_XHEADS = 4
N_EXPERTS = 32
N_EXPERT_GROUPS = 4
EXPERTS_PER_GROUP = 8
CONV_A_K = 3
CONV_M_K = 4

NT_DIMS = (((1,), (1,)), ((), ()))
TN_DIMS = (((0,), (0,)), ((), ()))


def _params(sem):
    return pltpu.CompilerParams(dimension_semantics=sem, vmem_limit_bytes=VMEM_LIMIT)


def _pick(n, cands):
    for c in cands:
        if n % c == 0:
            return c
    raise ValueError(f"no tile for {n} in {cands}")


def _dot(a, b):
    return jnp.dot(a, b, preferred_element_type=F32)


def _dot_nt(a, b):
    return lax.dot_general(a, b, NT_DIMS, preferred_element_type=F32)


def _rms(x, g):
    return x * lax.rsqrt(jnp.mean(x * x, axis=-1, keepdims=True) + EPS) * g


def _split3(x):
    hi = x.astype(BF16)
    r = x - hi.astype(F32)
    mid = r.astype(BF16)
    lo = (r - mid.astype(F32)).astype(BF16)
    return hi, mid, lo


def _softplus(x):
    return jnp.maximum(x, 0.0) + jnp.log1p(jnp.exp(-jnp.abs(x)))


def _norm_dt_kernel(xp_ref, xs_ref, g_ref, wdt_ref, xnp_ref, xns_ref, dtp_ref, dts_ref):
    wdt = wdt_ref[...].astype(BF16)

    def one(x_ref, xn_ref, dt_ref):
        xn = _rms(x_ref[...], g_ref[...]).astype(BF16)
        xn_ref[...] = xn
        dt_ref[...] = _dot_nt(xn, wdt)

    one(xp_ref, xnp_ref, dtp_ref)

    @pl.when(pl.program_id(0) == 0)
    def _():
        one(xs_ref, xns_ref, dts_ref)


def _norm_dt(xp, xs, g, wdt):
    tp, d = xp.shape
    ts = xs.shape[0]
    nh = wdt.shape[0]
    tm = _pick(tp, (512, 256, 128))
    return pl.pallas_call(
        _norm_dt_kernel,
        grid=(tp // tm,),
        in_specs=[
            pl.BlockSpec((tm, d), lambda i: (i, 0)),
            pl.BlockSpec((ts, d), lambda i: (0, 0)),
            pl.BlockSpec((1, d), lambda i: (0, 0)),
            pl.BlockSpec((nh, d), lambda i: (0, 0)),
        ],
        out_specs=[
            pl.BlockSpec((tm, d), lambda i: (i, 0)),
            pl.BlockSpec((ts, d), lambda i: (0, 0)),
            pl.BlockSpec((tm, nh), lambda i: (i, 0)),
            pl.BlockSpec((ts, nh), lambda i: (0, 0)),
        ],
        out_shape=[
            jax.ShapeDtypeStruct((tp, d), BF16),
            jax.ShapeDtypeStruct((ts, d), BF16),
            jax.ShapeDtypeStruct((tp, nh), F32),
            jax.ShapeDtypeStruct((ts, nh), F32),
        ],
        compiler_params=_params(("arbitrary",)),
        name="norm_dt",
    )(xp, xs, g, wdt)


def _proj_a_kernel(xp_ref, xs_ref, wb_ref, wc_ref, wh_ref, cw_ref, st_ref,
                   op_ref, os_ref, cap_ref, cas_ref,
                   wbf, ubuf, sbuf, s1buf, s2buf, *, tiles_per_batch, nb_s, l_s):
    i = pl.program_id(1)
    tm = xp_ref.shape[0]
    ts = xs_ref.shape[0]
    cw = cw_ref[...]

    @pl.when(i == 0)
    def _():
        wbf[0] = wb_ref[...].astype(BF16)
        wbf[1] = wc_ref[...].astype(BF16)
        wbf[2] = wh_ref[...].astype(BF16)
        x = xs_ref[...]
        u = _dot_nt(x, wbf[1]) * _dot_nt(x, wbf[2])
        sbuf[pl.ds(0, SUBLANES), :] = jnp.zeros((SUBLANES, u.shape[1]), F32)
        sbuf[pl.ds(SUBLANES, ts), :] = u
        s1buf[...] = jnp.zeros_like(s1buf)
        s2buf[...] = jnp.zeros_like(s2buf)
        for b in range(nb_s):
            s1buf[pl.ds(b * l_s, 1), :] = st_ref[b, pl.ds(1, 1), :]
            s2buf[pl.ds(b * l_s, 1), :] = st_ref[b, pl.ds(0, 1), :]
            s2buf[pl.ds(b * l_s + 1, 1), :] = st_ref[b, pl.ds(1, 1), :]
        rmod = lax.broadcasted_iota(jnp.int32, (ts, 1), 0) % l_s
        prev1 = jnp.where(rmod == 0, s1buf[...], sbuf[pl.ds(SUBLANES - 1, ts), :])
        prev2 = jnp.where(rmod < 2, s2buf[...], sbuf[pl.ds(SUBLANES - 2, ts), :])
        v = prev2 * cw[0:1, :] + prev1 * cw[1:2, :] + u * cw[2:3, :]
        os_ref[...] = (_dot_nt(x, wbf[0]) * v).astype(BF16)
        for b in range(nb_s):
            cas_ref[b] = sbuf[pl.ds(SUBLANES + (b + 1) * l_s - 2, 2), :]

    @pl.when(i % tiles_per_batch == 0)
    def _():
        ubuf[pl.ds(0, SUBLANES), :] = jnp.zeros((SUBLANES, ubuf.shape[1]), F32)

    x = xp_ref[...]
    u = _dot_nt(x, wbf[1]) * _dot_nt(x, wbf[2])
    ubuf[pl.ds(SUBLANES, tm), :] = u
    v = (ubuf[pl.ds(SUBLANES - 2, tm), :] * cw[0:1, :]
         + ubuf[pl.ds(SUBLANES - 1, tm), :] * cw[1:2, :] + u * cw[2:3, :])
    op_ref[...] = (_dot_nt(x, wbf[0]) * v).astype(BF16)
    ubuf[pl.ds(0, SUBLANES), :] = ubuf[pl.ds(tm, SUBLANES), :]

    @pl.when(i % tiles_per_batch == tiles_per_batch - 1)
    def _():
        cap_ref[0] = ubuf[pl.ds(SUBLANES + tm - 2, 2), :]


def _proj_a(xnp, xns, w_in, conv_w, state_s, n_batch_p, seq_p, d_conv):
    tp, d = xnp.shape
    ts = xns.shape[0]
    nb_s = state_s.shape[0]
    l_s = ts // nb_s
    tn = 512
    tm = _pick(seq_p, (1024, 512, 256, 128))
    tpb = seq_p // tm
    ncol = d_conv // tn
    kern = functools.partial(_proj_a_kernel, tiles_per_batch=tpb, nb_s=nb_s, l_s=l_s)
    return pl.pallas_call(
        kern,
        grid=(ncol, tp // tm),
        in_specs=[
            pl.BlockSpec((tm, d), lambda j, i: (i, 0)),
            pl.BlockSpec((ts, d), lambda j, i: (0, 0)),
            pl.BlockSpec((tn, d), lambda j, i: (j, 0)),
            pl.BlockSpec((tn, d), lambda j, i: (j + ncol, 0)),
            pl.BlockSpec((tn, d), lambda j, i: (j + 2 * ncol, 0)),
            pl.BlockSpec((CONV_A_K, tn), lambda j, i: (0, j)),
            pl.BlockSpec((nb_s, CONV_A_K - 1, tn), lambda j, i: (0, 0, j)),
        ],
        out_specs=[
            pl.BlockSpec((tm, tn), lambda j, i: (i, j)),
            pl.BlockSpec((ts, tn), lambda j, i: (0, j)),
            pl.BlockSpec((1, CONV_A_K - 1, tn), lambda j, i: (i // tpb, 0, j)),
            pl.BlockSpec((nb_s, CONV_A_K - 1, tn), lambda j, i: (0, 0, j)),
        ],
        out_shape=[
            jax.ShapeDtypeStruct((tp, d_conv), BF16),
            jax.ShapeDtypeStruct((ts, d_conv), BF16),
            jax.ShapeDtypeStruct((n_batch_p, CONV_A_K - 1, d_conv), F32),
            jax.ShapeDtypeStruct((nb_s, CONV_A_K - 1, d_conv), F32),
        ],
        scratch_shapes=[
            pltpu.VMEM((3, tn, d), BF16),
            pltpu.VMEM((SUBLANES + tm, tn), F32),
            pltpu.VMEM((SUBLANES + ts, tn), F32),
            pltpu.VMEM((ts, tn), F32),
            pltpu.VMEM((ts, tn), F32),
        ],
        compiler_params=_params(("arbitrary", "arbitrary")),
        name="proj_a",
    )(xnp, xns, w_in, w_in, w_in, conv_w, state_s)


def _proj_raw_kernel(xp_ref, xs_ref, w_ref, op_ref, os_ref, wbf, *, nsplit, width):
    def emit(x_ref, o_ref):
        acc = _dot_nt(x_ref[...], wbf[...])
        if nsplit == 0:
            o_ref[...] = acc
        else:
            for s in range(nsplit):
                o_ref[s] = acc[:, s * width:(s + 1) * width]

    @pl.when(pl.program_id(1) == 0)
    def _():
        wbf[...] = w_ref[...].astype(BF16)
        emit(xs_ref, os_ref)

    emit(xp_ref, op_ref)


def _proj_raw(xnp, xns, w, col0, ncols, width):
    tp, d = xnp.shape
    ts = xns.shape[0]
    tn = 1024
    tm = _pick(tp, (1024, 512, 256, 128))
    assert col0 % tn == 0 and ncols % tn == 0
    jb = col0 // tn
    if width == 0:
        nsplit = 0
        out_specs = [pl.BlockSpec((tm, tn), lambda j, i: (i, j)),
                     pl.BlockSpec((ts, tn), lambda j, i: (0, j))]
        out_shape = [jax.ShapeDtypeStruct((tp, ncols), F32), jax.ShapeDtypeStruct((ts, ncols), F32)]
    else:
        nsplit = tn // width
        out_specs = [pl.BlockSpec((nsplit, tm, width), lambda j, i: (j, i, 0)),
                     pl.BlockSpec((nsplit, ts, width), lambda j, i: (j, 0, 0))]
        out_shape = [jax.ShapeDtypeStruct((ncols // width, tp, width), F32),
                     jax.ShapeDtypeStruct((ncols // width, ts, width), F32)]
    kern = functools.partial(_proj_raw_kernel, nsplit=nsplit, width=width)
    return pl.pallas_call(
        kern,
        grid=(ncols // tn, tp // tm),
        in_specs=[
            pl.BlockSpec((tm, d), lambda j, i: (i, 0)),
            pl.BlockSpec((ts, d), lambda j, i: (0, 0)),
            pl.BlockSpec((tn, d), lambda j, i: (j + jb, 0)),
        ],
        out_specs=out_specs,
        out_shape=out_shape,
        scratch_shapes=[pltpu.VMEM((tn, d), BF16)],
        compiler_params=_params(("arbitrary", "arbitrary")),
        name=f"proj_raw_{col0}",
    )(xnp, xns, w)


def _ssd_kernel(*refs, q, has_state, nchunks):
    (z_ref, xs_ref, b_ref, c_ref, dt_ref, cwx, cwb, cwc, cbx, cbb, cbc,
     dtb_ref, alog_ref, dsk_ref, gn_ref, *rest) = refs
    if has_state:
        cpx, cpb, cpc, hprev, *rest = rest
    (y_ref, ocx, ocb, occ, oh, h_s, ccx, ccb, ccc, xbuf, bbuf, cbuf, acg, rowt) = rest
    c = pl.program_id(1)
    nheads = N_GROUPS * HEADS_PER_GROUP
    tail = CONV_M_K - 1

    @pl.when(c == 0)
    def _init():
        ccx[...] = jnp.zeros_like(ccx)
        ccb[...] = jnp.zeros_like(ccb)
        ccc[...] = jnp.zeros_like(ccc)
        if has_state:
            for g in range(N_GROUPS):
                ccx[g, pl.ds(SUBLANES - tail, tail), :] = cpx[g]
                ccb[g, pl.ds(SUBLANES - tail, tail), :] = cpb[g]
                ccc[g, pl.ds(SUBLANES - tail, tail), :] = cpc[g]
            h_s[...] = hprev[...]
        else:
            h_s[...] = jnp.zeros_like(h_s)

    dt = _softplus(dt_ref[...] + dtb_ref[...])
    da = dt * (-jnp.exp(alog_ref[...]))
    ri = lax.broadcasted_iota(jnp.int32, (q, q), 0)
    ci = lax.broadcasted_iota(jnp.int32, (q, q), 1)
    causal = ri >= ci
    tril = jnp.where(causal, 1.0, 0.0).astype(BF16)
    acum = sum(_dot(tril, p) for p in _split3(da))
    eye = jnp.where(lax.broadcasted_iota(jnp.int32, (nheads, nheads), 0)
                    == lax.broadcasted_iota(jnp.int32, (nheads, nheads), 1), 1.0, 0.0).astype(BF16)
    rowt[0] = sum(lax.dot_general(eye, p, NT_DIMS, preferred_element_type=F32) for p in _split3(acum))
    rowt[1] = sum(lax.dot_general(eye, p, NT_DIMS, preferred_element_type=F32) for p in _split3(dt))
    wend = jnp.exp(acum[q - 1:q, :] - acum) * dt
    rowt[2] = sum(lax.dot_general(eye, p, NT_DIMS, preferred_element_type=F32) for p in _split3(wend))
    for g in range(N_GROUPS):
        acg[g] = acum[:, g * HEADS_PER_GROUP:(g + 1) * HEADS_PER_GROUP]

    lane = lax.broadcasted_iota(jnp.int32, (1, LANES), 1)
    rowi = lax.broadcasted_iota(jnp.int32, (LANES, 1), 0)
    half_w = LANES // 2

    def conv(raw_ref, cc, buf, cw, cb, ost, g):
        buf[pl.ds(0, SUBLANES), :] = cc[g]
        raw = raw_ref[g]
        buf[pl.ds(SUBLANES, q), :] = raw
        w = cw[g]
        acc = raw * w[tail:tail + 1, :]
        for k in range(tail):
            acc = acc + buf[pl.ds(SUBLANES - tail + k, q), :] * w[k:k + 1, :]
        cc[g] = buf[pl.ds(q, SUBLANES), :]

        @pl.when(c == nchunks - 1)
        def _():
            ost[g] = buf[pl.ds(SUBLANES + q - tail, tail), :]

        return jax.nn.silu(acc + cb[g])

    def group_body(g, carry):
        xs = conv(xs_ref, ccx, xbuf, cwx, cbx, ocx, g)
        bc = conv(b_ref, ccb, bbuf, cwb, cbb, ocb, g)
        cc_ = conv(c_ref, ccc, cbuf, cwc, cbc, occ, g)
        bb = bc.astype(BF16)
        ccb16 = cc_.astype(BF16)
        cb_ = lax.dot_general(ccb16, bb, NT_DIMS, preferred_element_type=F32)
        ac8 = acg[g]
        dsk = dsk_ref[g]
        lo_half = lane < half_w
        lo_rows = rowi < half_w
        ys = []
        for pair in range(HEADS_PER_GROUP // 2):
            sl = slice(pair * LANES, (pair + 1) * LANES)
            xp = xs[:, sl]
            hp = h_s[g, pl.ds(pair * LANES, LANES), :]
            ms_, acols, wrows = [], [], []
            for r in (2 * pair, 2 * pair + 1):
                head = g * HEADS_PER_GROUP + r
                acol = jnp.broadcast_to(ac8[:, r:r + 1], (q, LANES))
                arow = rowt[0, pl.ds(head, 1), :]
                drow = rowt[1, pl.ds(head, 1), :]
                decay = jnp.exp(jnp.where(causal, acol[:, :q] - arow, -jnp.inf))
                ms_.append((cb_ * decay * drow).astype(BF16))
                acols.append(acol)
                wrows.append(jnp.broadcast_to(rowt[2, pl.ds(head, 1), :], (half_w, q)))
            xlo = jnp.where(lo_half, xp, 0.0).astype(BF16)
            xhi = jnp.where(lo_half, 0.0, xp).astype(BF16)
            if q % LANES == 0:
                ydiag = _dot(jnp.concatenate(ms_, axis=1), jnp.concatenate([xlo, xhi], axis=0))
            else:
                ydiag = _dot(ms_[0], xlo) + _dot(ms_[1], xhi)
            ea = jnp.where(lo_half, jnp.exp(acols[0]), jnp.exp(acols[1]))
            yoff = ea * lax.dot_general(ccb16, hp.astype(BF16), NT_DIMS, preferred_element_type=F32)
            ys.append(dsk[:, sl] * xp + ydiag + yoff)
            xwt = (xp.T * jnp.concatenate(wrows, axis=0)).astype(BF16)
            dlast = [jnp.broadcast_to(jnp.exp(a[q - 1:q, :]), (LANES, LANES)) for a in acols]
            h_s[g, pl.ds(pair * LANES, LANES), :] = hp * jnp.where(lo_rows, dlast[0], dlast[1]) + _dot(xwt, bb)
        yg = jnp.concatenate(ys, axis=1)
        hh = yg * jax.nn.silu(z_ref[g])
        ms = jnp.mean(hh * hh, axis=-1, keepdims=True)
        y_ref[g] = (hh * lax.rsqrt(ms + EPS) * gn_ref[g]).astype(BF16)
        return carry

    lax.fori_loop(0, N_GROUPS, group_body, 0)

    @pl.when(c == nchunks - 1)
    def _():
        oh[...] = h_s[...]


def _ssd(zx, bc, dt_raw, prm, n_batch, seq, q, state=None):
    nchunks = seq // q
    nheads = N_GROUPS * HEADS_PER_GROUP
    has_state = state is not None
    g8 = N_GROUPS

    def tok(first):
        return lambda b, c: (first, b * nchunks + c, 0)

    def const3(b, c):
        return (0, 0, 0)

    in_specs = [
        pl.BlockSpec((g8, q, GROUP_W), tok(0)),
        pl.BlockSpec((g8, q, GROUP_W), tok(1)),
        pl.BlockSpec((g8, q, D_STATE), tok(0)),
        pl.BlockSpec((g8, q, D_STATE), tok(1)),
        pl.BlockSpec((q, nheads), lambda b, c: (b * nchunks + c, 0)),
        pl.BlockSpec((g8, CONV_M_K, GROUP_W), const3),
        pl.BlockSpec((g8, CONV_M_K, D_STATE), const3),
        pl.BlockSpec((g8, CONV_M_K, D_STATE), const3),
        pl.BlockSpec((g8, 1, GROUP_W), const3),
        pl.BlockSpec((g8, 1, D_STATE), const3),
        pl.BlockSpec((g8, 1, D_STATE), const3),
        pl.BlockSpec((1, nheads), lambda b, c: (0, 0)),
        pl.BlockSpec((1, nheads), lambda b, c: (0, 0)),
        pl.BlockSpec((g8, 1, GROUP_W), const3),
        pl.BlockSpec((g8, 1, GROUP_W), const3),
    ]
    args = [zx, zx, bc, bc, dt_raw, prm["cwx"], prm["cwb"], prm["cwc"], prm["cbx"], prm["cbb"], prm["cbc"],
            prm["dtb"], prm["alog"], prm["dsk"], prm["gn"]]
    tail = CONV_M_K - 1

    def per_batch(shape):
        return pl.BlockSpec((None,) + shape, lambda b, c: (b,) + (0,) * len(shape))

    if has_state:
        in_specs += [per_batch((g8, tail, GROUP_W)), per_batch((g8, tail, D_STATE)),
                     per_batch((g8, tail, D_STATE)), per_batch((g8, GROUP_W, D_STATE))]
        args += list(state)
    t = n_batch * seq
    out_specs = [
        pl.BlockSpec((g8, q, GROUP_W), lambda b, c: (0, b * nchunks + c, 0)),
        per_batch((g8, tail, GROUP_W)), per_batch((g8, tail, D_STATE)), per_batch((g8, tail, D_STATE)),
        per_batch((g8, GROUP_W, D_STATE)),
    ]
    out_shape = [
        jax.ShapeDtypeStruct((g8, t, GROUP_W), BF16),
        jax.ShapeDtypeStruct((n_batch, g8, tail, GROUP_W), F32),
        jax.ShapeDtypeStruct((n_batch, g8, tail, D_STATE), F32),
        jax.ShapeDtypeStruct((n_batch, g8, tail, D_STATE), F32),
        jax.ShapeDtypeStruct((n_batch, g8, GROUP_W, D_STATE), F32),
    ]
    scratch = [
        pltpu.VMEM((g8, GROUP_W, D_STATE), F32),
        pltpu.VMEM((g8, SUBLANES, GROUP_W), F32),
        pltpu.VMEM((g8, SUBLANES, D_STATE), F32),
        pltpu.VMEM((g8, SUBLANES, D_STATE), F32),
        pltpu.VMEM((SUBLANES + q, GROUP_W), F32),
        pltpu.VMEM((SUBLANES + q, D_STATE), F32),
        pltpu.VMEM((SUBLANES + q, D_STATE), F32),
        pltpu.VMEM((g8, q, HEADS_PER_GROUP), F32),
        pltpu.VMEM((3, nheads, q), F32),
    ]
    kern = functools.partial(_ssd_kernel, q=q, has_state=has_state, nchunks=nchunks)
    return pl.pallas_call(
        kern,
        grid=(n_batch, nchunks),
        in_specs=in_specs,
        out_specs=out_specs,
        out_shape=out_shape,
        scratch_shapes=scratch,
        compiler_params=_params(("arbitrary", "arbitrary")),
        name="ssd_state" if has_state else "ssd",
    )(*args)


def _merge_kernel(ap_ref, as_ref, yp_ref, ys_ref, gap_ref, gmp_ref, gas_ref, gms_ref, wa_ref, wm_ref,
                  op_ref, os_ref, wab, wmb):
    def emit(a_ref, y_ref, ga_ref, gm_ref, o_ref):
        oa = _dot(a_ref[...], wab[...])
        om = _dot(y_ref[0], wmb[0])
        for g in range(1, N_GROUPS):
            om = om + _dot(y_ref[g], wmb[g])
        o_ref[...] = (jax.nn.sigmoid(ga_ref[...]) * oa + jax.nn.sigmoid(gm_ref[...]) * om).astype(BF16)

    @pl.when(pl.program_id(1) == 0)
    def _():
        wab[...] = wa_ref[...].astype(BF16)
        wmb[...] = wm_ref[...].astype(BF16)
        emit(as_ref, ys_ref, gas_ref, gms_ref, os_ref)

    emit(ap_ref, yp_ref, gap_ref, gmp_ref, op_ref)


def _merge(abv_p, abv_s, y_p, y_s, g_p, g_s, w_a_out, w_m_out3):
    tp, dc = abv_p.shape
    ts = abv_s.shape[0]
    dm = w_a_out.shape[1]
    tn = 512
    tm = _pick(tp, (512, 256, 128))
    ncol = dm // tn
    return pl.pallas_call(
        _merge_kernel,
        grid=(ncol, tp // tm),
        in_specs=[
            pl.BlockSpec((tm, dc), lambda j, i: (i, 0)),
            pl.BlockSpec((ts, dc), lambda j, i: (0, 0)),
            pl.BlockSpec((N_GROUPS, tm, GROUP_W), lambda j, i: (0, i, 0)),
            pl.BlockSpec((N_GROUPS, ts, GROUP_W), lambda j, i: (0, 0, 0)),
            pl.BlockSpec((tm, tn), lambda j, i: (i, j)),
            pl.BlockSpec((tm, tn), lambda j, i: (i, j + ncol)),
            pl.BlockSpec((ts, tn), lambda j, i: (0, j)),
            pl.BlockSpec((ts, tn), lambda j, i: (0, j + ncol)),
            pl.BlockSpec((dc, tn), lambda j, i: (0, j)),
            pl.BlockSpec((N_GROUPS, GROUP_W, tn), lambda j, i: (0, 0, j)),
        ],
        out_specs=[
            pl.BlockSpec((tm, tn), lambda j, i: (i, j)),
            pl.BlockSpec((ts, tn), lambda j, i: (0, j)),
        ],
        out_shape=[jax.ShapeDtypeStruct((tp, dm), BF16), jax.ShapeDtypeStruct((ts, dm), BF16)],
        scratch_shapes=[pltpu.VMEM((dc, tn), BF16), pltpu.VMEM((N_GROUPS, GROUP_W, tn), BF16)],
        compiler_params=_params(("arbitrary", "arbitrary")),
        name="merge",
    )(abv_p, abv_s, y_p, y_s, g_p, g_p, g_s, g_s, w_a_out, w_m_out3)


def _res_kernel(ap_ref, as_ref, rp_ref, rs_ref, w_ref, g_ref, *outs, n_prompt_tiles, merged_out, ts):
    if merged_out:
        h_ref, hn_ref, wbf = outs
    else:
        hp_ref, hs_ref, hnp_ref, hns_ref, wbf = outs
    i = pl.program_id(0)

    @pl.when(i == 0)
    def _():
        wbf[...] = w_ref[...].astype(BF16)

    def emit(a_ref, r_ref, store_h, store_hn):
        h = r_ref[...] + _dot(a_ref[...], wbf[...])
        store_h(h)
        store_hn(_rms(h, g_ref[...]))

    if merged_out:
        @pl.when(i < n_prompt_tiles)
        def _():
            def sh(h):
                h_ref[...] = h

            def shn(hn):
                hn_ref[...] = hn

            emit(ap_ref, rp_ref, sh, shn)

        @pl.when(i == n_prompt_tiles)
        def _():
            def sh(h):
                h_ref[pl.ds(0, ts), :] = h

            def shn(hn):
                hn_ref[pl.ds(0, ts), :] = hn

            emit(as_ref, rs_ref, sh, shn)
    else:
        def shp(h):
            hp_ref[...] = h

        def shnp(hn):
            hnp_ref[...] = hn.astype(hnp_ref.dtype)

        emit(ap_ref, rp_ref, shp, shnp)

        @pl.when(i == 0)
        def _():
            def shs(h):
                hs_ref[...] = h

            def shns(hn):
                hns_ref[...] = hn.astype(hns_ref.dtype)

            emit(as_ref, rs_ref, shs, shns)


def _res(a_p, a_s, r_p, r_s, w, g, merged_out, hn_dtype):
    tp, d = a_p.shape
    ts = a_s.shape[0]
    tm = 256
    assert tp % tm == 0 and ts <= tm
    npt = tp // tm
    last = npt - 1
    in_specs = [
        pl.BlockSpec((tm, d), lambda i: (jnp.minimum(i, last), 0)),
        pl.BlockSpec((ts, d), lambda i: (0, 0)),
        pl.BlockSpec((tm, d), lambda i: (jnp.minimum(i, last), 0)),
        pl.BlockSpec((ts, d), lambda i: (0, 0)),
        pl.BlockSpec((d, d), lambda i: (0, 0), pipeline_mode=pl.Buffered(1)),
        pl.BlockSpec((1, d), lambda i: (0, 0)),
    ]
    if merged_out:
        grid = (npt + 1,)
        out_specs = [pl.BlockSpec((tm, d), lambda i: (i, 0)), pl.BlockSpec((tm, d), lambda i: (i, 0))]
        out_shape = [jax.ShapeDtypeStruct((tp + ts, d), F32), jax.ShapeDtypeStruct((tp + ts, d), hn_dtype)]
    else:
        grid = (npt,)
        out_specs = [pl.BlockSpec((tm, d), lambda i: (i, 0)), pl.BlockSpec((ts, d), lambda i: (0, 0)),
                     pl.BlockSpec((tm, d), lambda i: (i, 0)), pl.BlockSpec((ts, d), lambda i: (0, 0))]
        out_shape = [jax.ShapeDtypeStruct((tp, d), F32), jax.ShapeDtypeStruct((ts, d), F32),
                     jax.ShapeDtypeStruct((tp, d), hn_dtype), jax.ShapeDtypeStruct((ts, d), hn_dtype)]
    kern = functools.partial(_res_kernel, n_prompt_tiles=npt, merged_out=merged_out, ts=ts)
    return pl.pallas_call(
        kern,
        grid=grid,
        in_specs=in_specs,
        out_specs=out_specs,
        out_shape=out_shape,
        scratch_shapes=[pltpu.VMEM((d, d), BF16)],
        compiler_params=_params(("arbitrary",)),
        name="res_merged" if merged_out else "res",
    )(a_p, a_s, r_p, r_s, w, g)


def _kv_kernel(m_ref, g_ref, wk_ref, wv_ref, k_ref, v_ref, mn):
    @pl.when(pl.program_id(0) == 0)
    def _():
        mn[...] = _rms(m_ref[...], g_ref[...]).astype(BF16)

    k_ref[...] = _dot(mn[...], wk_ref[...].astype(BF16))
    v_ref[...] = _dot(mn[...], wv_ref[...].astype(BF16))


def _memory_kv(mem2d, g, w_k, w_v):
    m, d = mem2d.shape
    tn = 512
    return pl.pallas_call(
        _kv_kernel,
        grid=(d // tn,),
        in_specs=[
            pl.BlockSpec((m, d), lambda j: (0, 0)),
            pl.BlockSpec((1, d), lambda j: (0, 0)),
            pl.BlockSpec((d, tn), lambda j: (0, j)),
            pl.BlockSpec((d, tn), lambda j: (0, j)),
        ],
        out_specs=[pl.BlockSpec((m, tn), lambda j: (0, j)), pl.BlockSpec((m, tn), lambda j: (0, j))],
        out_shape=[jax.ShapeDtypeStruct((m, d), F32), jax.ShapeDtypeStruct((m, d), F32)],
        scratch_shapes=[pltpu.VMEM((m, d), BF16)],
        compiler_params=_params(("arbitrary",)),
        name="memory_kv",
    )(mem2d, g, w_k, w_v)


def _attn_kernel(hn_ref, k_ref, v_ref, wq_ref, o_ref, wqb, kb, vb):
    b = pl.program_id(0)
    i = pl.program_id(1)

    @pl.when((b == 0) & (i == 0))
    def _():
        wqb[...] = wq_ref[...].astype(BF16)

    @pl.when(i == 0)
    def _():
        kb[...] = k_ref[...].astype(BF16)
        vb[...] = v_ref[...].astype(BF16)

    d = wqb.shape[1]
    dh = d // N_XHEADS
    q = _dot(hn_ref[...], wqb[...])
    outs = []
    for h in range(N_XHEADS):
        sl = slice(h * dh, (h + 1) * dh)
        s = lax.dot_general(q[:, sl].astype(BF16), kb[:, sl], NT_DIMS, preferred_element_type=F32)
        s = s * (dh ** -0.5)
        e = jnp.exp(s - jnp.max(s, axis=-1, keepdims=True))
        p = e / jnp.sum(e, axis=-1, keepdims=True)
        outs.append(_dot(p.astype(BF16), vb[:, sl]))
    o_ref[...] = jnp.concatenate(outs, axis=1).astype(BF16)


def _attn(hn, k3, v3, w_q, n_batch, seq):
    t, d = hn.shape
    nm = k3.shape[1]
    tm = _pick(seq, (256, 128, 64, 32, 16))
    tpb = seq // tm
    return pl.pallas_call(
        _attn_kernel,
        grid=(n_batch, tpb),
        in_specs=[
            pl.BlockSpec((tm, d), lambda b, i: (b * tpb + i, 0)),
            pl.BlockSpec((None, nm, d), lambda b, i: (b, 0, 0)),
            pl.BlockSpec((None, nm, d), lambda b, i: (b, 0, 0)),
            pl.BlockSpec((d, d), lambda b, i: (0, 0), pipeline_mode=pl.Buffered(1)),
        ],
        out_specs=pl.BlockSpec((tm, d), lambda b, i: (b * tpb + i, 0)),
        out_shape=jax.ShapeDtypeStruct((t, d), BF16),
        scratch_shapes=[pltpu.VMEM((d, d), BF16), pltpu.VMEM((nm, d), BF16), pltpu.VMEM((nm, d), BF16)],
        compiler_params=_params(("arbitrary", "arbitrary")),
        name=f"attn_{seq}",
    )(hn, k3, v3, w_q)


def _router_kernel(x_ref, w_ref, b_ref, ri_ref, rw_ref):
    logits = _dot(x_ref[...].astype(BF16), w_ref[...].astype(BF16)) + b_ref[...]
    lane_i = lax.broadcasted_iota(jnp.int32, logits.shape, 1)
    lane = lane_i.astype(F32)
    ninf = -jnp.inf
    big = float(LANES)
    is_g = lane < N_EXPERT_GROUPS
    gl = jnp.where(is_g, logits, ninf)
    gmax = jnp.max(gl, axis=-1, keepdims=True)
    gsel = jnp.min(jnp.where(gl == gmax, lane, big), axis=-1, keepdims=True)
    pg = 1.0 / jnp.sum(jnp.where(is_g, jnp.exp(gl - gmax), 0.0), axis=-1, keepdims=True)
    lo = N_EXPERT_GROUPS + EXPERTS_PER_GROUP * gsel
    el = jnp.where(lane >= lo, jnp.where(lane < lo + EXPERTS_PER_GROUP, logits, ninf), ninf)
    m1 = jnp.max(el, axis=-1, keepdims=True)
    i1 = jnp.min(jnp.where(el == m1, lane, big), axis=-1, keepdims=True)
    el2 = jnp.where(lane == i1, ninf, el)
    m2 = jnp.max(el2, axis=-1, keepdims=True)
    i2 = jnp.min(jnp.where(el2 == m2, lane, big), axis=-1, keepdims=True)
    e = jnp.exp(m2 - m1)
    w1 = pg / (1.0 + e)
    w2 = pg * e / (1.0 + e)
    ri_ref[...] = jnp.where(lane_i == 0, i1 - N_EXPERT_GROUPS,
                            jnp.where(lane_i == 1, i2 - N_EXPERT_GROUPS, 0.0)).astype(jnp.int32)
    rw_ref[...] = jnp.where(lane_i == 0, w1, jnp.where(lane_i == 1, w2, 0.0))


def _router(hn_all, w_r, b_r):
    t, d = hn_all.shape
    tm = _pick(t, (640, 512, 384, 256, 128))
    return pl.pallas_call(
        _router_kernel,
        grid=(t // tm,),
        in_specs=[
            pl.BlockSpec((tm, d), lambda i: (i, 0)),
            pl.BlockSpec((d, LANES), lambda i: (0, 0)),
            pl.BlockSpec((1, LANES), lambda i: (0, 0)),
        ],
        out_specs=[pl.BlockSpec((tm, LANES), lambda i: (i, 0)), pl.BlockSpec((tm, LANES), lambda i: (i, 0))],
        out_shape=[jax.ShapeDtypeStruct((t, LANES), jnp.int32), jax.ShapeDtypeStruct((t, LANES), F32)],
        compiler_params=_params(("arbitrary",)),
        name="router",
    )(hn_all, w_r, b_r)


def _row_copy(src, dst, s, d, sem):
    return pltpu.make_async_copy(src.at[pl.ds(s, 1)], dst.at[pl.ds(d, 1)], sem)


def _dispatch_kernel(dest_ref, x_ref, o_ref, sem, *, chunk):
    base = pl.program_id(0) * chunk

    def issue(r, carry):
        t = base + r
        _row_copy(x_ref, o_ref, r, dest_ref[2 * t], sem).start()
        _row_copy(x_ref, o_ref, r, dest_ref[2 * t + 1], sem).start()
        return carry

    lax.fori_loop(0, chunk, issue, 0, unroll=8)

    for _ in range(2):
        pltpu.make_async_copy(x_ref, o_ref.at[pl.ds(0, chunk)], sem).wait()


def _dispatch(dest, hn_all):
    t, d = hn_all.shape
    chunk = _pick(t, (640, 512, 384, 256, 128))
    kern = functools.partial(_dispatch_kernel, chunk=chunk)
    return pl.pallas_call(
        kern,
        grid_spec=pltpu.PrefetchScalarGridSpec(
            num_scalar_prefetch=1,
            grid=(t // chunk,),
            in_specs=[pl.BlockSpec((chunk, d), lambda i, dref: (i, 0))],
            out_specs=pl.BlockSpec(memory_space=pl.ANY),
            scratch_shapes=[pltpu.SemaphoreType.DMA(())],
        ),
        out_shape=jax.ShapeDtypeStruct((2 * t, d), F32),
        compiler_params=_params(("arbitrary",)),
        name="dispatch",
    )(dest, hn_all)


def _expert_kernel(blk_ref, exp_ref, lo_ref, hi_ref, first_ref, x_ref, wg_ref, wu_ref, wd_ref, y_ref,
                   xb, wgb, wub, wdb, *, sub):
    it = pl.program_id(0)
    k = pl.program_id(1)
    lo = lo_ref[it]
    hi = hi_ref[it]

    @pl.when((k == 0) & (first_ref[it] == 1))
    def _():
        y_ref[...] = jnp.zeros_like(y_ref)

    @pl.when(hi > lo)
    def _():
        wgb[...] = wg_ref[...].astype(BF16)
        wub[...] = wu_ref[...].astype(BF16)
        wdb[...] = wd_ref[...].astype(BF16)
        for s in range(x_ref.shape[0] // sub):
            @pl.when((lo < (s + 1) * sub) & (hi > s * sub))
            def _(s=s):
                rs = pl.ds(s * sub, sub)

                @pl.when(k == 0)
                def _():
                    rows = lax.broadcasted_iota(jnp.int32, (sub, 1), 0) + s * sub
                    xb[rs, :] = jnp.where((rows >= lo) & (rows < hi), x_ref[rs, :], 0.0).astype(BF16)

                x = xb[rs, :]
                hid = (jax.nn.silu(_dot(x, wgb[...])) * _dot(x, wub[...])).astype(BF16)
                y_ref[rs, :] += _dot(hid, wdb[...])


def _experts(items, x_sorted, w_gate, w_up, w_down, rows_per_block, kchunk):
    blk, exp, lo, hi, first = items
    n_items = blk.shape[0]
    npairs, d = x_sorted.shape
    de = w_gate.shape[2]
    nk = de // kchunk
    r = rows_per_block

    def keff(it, k, lo_r, hi_r):
        return jnp.where(hi_r[it] > lo_r[it], k, nk - 1)

    kern = functools.partial(_expert_kernel, sub=256)
    return pl.pallas_call(
        kern,
        grid_spec=pltpu.PrefetchScalarGridSpec(
            num_scalar_prefetch=5,
            grid=(n_items, nk),
            in_specs=[
                pl.BlockSpec((r, d), lambda it, k, b, e, l, h, f: (b[it], 0)),
                pl.BlockSpec((None, d, kchunk), lambda it, k, b, e, l, h, f: (e[it], 0, keff(it, k, l, h))),
                pl.BlockSpec((None, d, kchunk), lambda it, k, b, e, l, h, f: (e[it], 0, keff(it, k, l, h))),
                pl.BlockSpec((None, kchunk, d), lambda it, k, b, e, l, h, f: (e[it], keff(it, k, l, h), 0)),
            ],
            out_specs=pl.BlockSpec((r, d), lambda it, k, b, e, l, h, f: (b[it], 0)),
            scratch_shapes=[pltpu.VMEM((r, d), BF16), pltpu.VMEM((d, kchunk), BF16),
                            pltpu.VMEM((d, kchunk), BF16), pltpu.VMEM((kchunk, d), BF16)],
        ),
        out_shape=jax.ShapeDtypeStruct((npairs, d), F32),
        compiler_params=_params(("arbitrary", "arbitrary")),
        name="experts",
    )(blk, exp, lo, hi, first, x_sorted, w_gate, w_up, w_down)


def _final_kernel(dest_ref, h_ref, rw_ref, ys_ref, g_ref, yp_ref, yss_ref, ya, yb, sem, *,
                  tm, n_prompt_tiles, ts):
    i = pl.program_id(0)
    base = i * tm

    def issue(r, carry):
        t = base + r
        _row_copy(ys_ref, ya, dest_ref[2 * t], r, sem).start()
        _row_copy(ys_ref, yb, dest_ref[2 * t + 1], r, sem).start()
        return carry

    def emit(n, o_ref):
        lax.fori_loop(0, n, issue, 0, unroll=8)
        for buf in (ya, yb):
            pltpu.make_async_copy(ys_ref.at[pl.ds(0, n)], buf.at[pl.ds(0, n)], sem).wait()
        w = rw_ref[pl.ds(0, n), :]
        h = h_ref[pl.ds(0, n), :] + w[:, 0:1] * ya[pl.ds(0, n), :] + w[:, 1:2] * yb[pl.ds(0, n), :]
        o_ref[...] = _rms(h, g_ref[...])

    @pl.when(i < n_prompt_tiles)
    def _():
        emit(tm, yp_ref)

    @pl.when(i == n_prompt_tiles)
    def _():
        emit(ts, yss_ref)


def _final(dest, h_all, rw, y_sorted, g, tp, ts):
    t, d = h_all.shape
    tm = 256
    assert tp % tm == 0 and ts <= tm and t == tp + ts
    npt = tp // tm
    kern = functools.partial(_final_kernel, tm=tm, n_prompt_tiles=npt, ts=ts)
    return pl.pallas_call(
        kern,
        grid_spec=pltpu.PrefetchScalarGridSpec(
            num_scalar_prefetch=1,
            grid=(npt + 1,),
            in_specs=[
                pl.BlockSpec((tm, d), lambda i, dref: (i, 0)),
                pl.BlockSpec((tm, LANES), lambda i, dref: (i, 0)),
                pl.BlockSpec(memory_space=pl.ANY),
                pl.BlockSpec((1, d), lambda i, dref: (0, 0)),
            ],
            out_specs=[
                pl.BlockSpec((tm, d), lambda i, dref: (jnp.minimum(i, npt - 1), 0)),
                pl.BlockSpec((ts, d), lambda i, dref: (0, 0)),
            ],
            scratch_shapes=[pltpu.VMEM((tm, d), F32), pltpu.VMEM((tm, d), F32), pltpu.SemaphoreType.DMA(())],
        ),
        out_shape=[jax.ShapeDtypeStruct((tp, d), F32), jax.ShapeDtypeStruct((ts, d), F32)],
        compiler_params=_params(("arbitrary",)),
        name="final",
    )(dest, h_all, rw, y_sorted, g)


def _routing_tables(eid, rows_per_block):
    npairs = eid.shape[0]
    r = rows_per_block
    i32 = jnp.int32
    onehot = (eid[:, None] == jnp.arange(N_EXPERTS, dtype=i32)[None, :]).astype(i32)
    csum = jnp.cumsum(onehot, axis=0)
    counts = csum[-1]
    starts = jnp.cumsum(counts) - counts
    dest = jnp.sum(onehot * (csum - 1 + starts[None, :]), axis=1).astype(i32)

    nblocks = -(-npairs // r)
    a = jnp.arange(nblocks, dtype=i32) * r
    s = starts[1:].astype(i32)
    pos_a = jnp.arange(nblocks, dtype=i32) + jnp.sum((s[None, :] < a[:, None]).astype(i32), axis=1)
    pos_s = jnp.arange(N_EXPERTS - 1, dtype=i32) + jnp.sum((a[None, :] <= s[:, None]).astype(i32), axis=1)
    n_items = nblocks + N_EXPERTS - 1
    idx = jnp.arange(n_items, dtype=i32)
    bps = (jnp.sum(jnp.where(pos_a[None, :] == idx[:, None], a[None, :], 0), axis=1)
           + jnp.sum(jnp.where(pos_s[None, :] == idx[:, None], s[None, :], 0), axis=1)).astype(i32)
    ends = jnp.concatenate([bps[1:], jnp.array([npairs], i32)])
    blk = jnp.minimum(bps // r, nblocks - 1)
    length = ends - bps
    exp = jnp.sum((starts[None, :] <= bps[:, None]).astype(i32), axis=1) - 1
    last_real = jnp.max(jnp.where((idx[None, :] <= idx[:, None]) & (length[None, :] > 0), idx[None, :], 0), axis=1)
    exp = jnp.sum(jnp.where(idx[None, :] == last_real[:, None], exp[None, :], 0), axis=1).astype(i32)
    lo = bps - blk * r
    hi = lo + length
    first = jnp.concatenate([jnp.ones((1,), i32), (blk[1:] != blk[:-1]).astype(i32)])
    return dest, (blk.astype(i32), exp, lo.astype(i32), hi.astype(i32), first)


def _group_major(a, width):
    lead = a.shape[:-1]
    g = a.shape[-1] // width
    return jnp.moveaxis(a.reshape(lead + (g, width)), -2, 0)


def kernel(x_prompt, x_sample, cache_mem_k, cache_mem_v, state_conv_a, state_conv_m, state_ssm, mem_prompt,
           norm_mix, w_in, conv_a_w, w_a_out, conv_m_w, conv_m_b, dt_bias, a_log, d_skip, ssm_norm,
           w_m_out, w_o, norm_cross, norm_mem, w_q, w_k, w_v, w_co, norm_ffn, w_rg, b_rg, w_re, b_re,
           w_gate, w_up, w_down, norm_final):
    depth = w_in.shape[0]
    assert depth == 1, "single-layer step"
    nbp, seq_p, d = x_prompt.shape
    nbs, seq_s, _ = x_sample.shape
    tp, ts = nbp * seq_p, nbs * seq_s
    n_mem = mem_prompt.shape[1]
    d_conv = conv_a_w.shape[2]
    d_inner = w_m_out.shape[1]
    nheads = dt_bias.shape[1]
    bc_w = N_GROUPS * D_STATE
    l = 0

    xp = x_prompt.reshape(tp, d)
    xs = x_sample.reshape(ts, d)
    w_in_t = jnp.swapaxes(w_in[l], 0, 1)
    col_z = 3 * d_conv
    col_x = col_z + d_inner
    col_b = col_x + d_inner
    col_dt = col_b + 2 * bc_w
    col_g = col_dt + nheads
    w_dt = w_in_t[col_dt:col_g]
    w_gates = w_in_t[col_g:]

    mk2, mv2 = _memory_kv(mem_prompt.reshape(nbp * n_mem, d), norm_mem[l][None], w_k[l], w_v[l])

    xnp, xns, dt_p, dt_s = _norm_dt(xp, xs, norm_mix[l][None], w_dt)
    abv_p, abv_s, ca_p, ca_s = _proj_a(xnp, xns, w_in_t, conv_a_w[l], state_conv_a[l], nbp, seq_p, d_conv)
    zx_p, zx_s = _proj_raw(xnp, xns, w_in_t, col_z, 2 * d_inner, GROUP_W)
    bc_p, bc_s = _proj_raw(xnp, xns, w_in_t, col_b, 2 * bc_w, D_STATE)
    g_p, g_s = _proj_raw(xnp, xns, w_gates, 0, 2 * d, 0)

    cw = conv_m_w[l]
    cbias = conv_m_b[l][None]
    prm = {
        "cwx": _group_major(cw[:, :d_inner], GROUP_W),
        "cwb": _group_major(cw[:, d_inner:d_inner + bc_w], D_STATE),
        "cwc": _group_major(cw[:, d_inner + bc_w:], D_STATE),
        "cbx": _group_major(cbias[:, :d_inner], GROUP_W),
        "cbb": _group_major(cbias[:, d_inner:d_inner + bc_w], D_STATE),
        "cbc": _group_major(cbias[:, d_inner + bc_w:], D_STATE),
        "dtb": dt_bias[l][None],
        "alog": a_log[l][None],
        "dsk": _group_major(jnp.repeat(d_skip[l], HEAD_DIM)[None], GROUP_W),
        "gn": _group_major(ssm_norm[l][None], GROUP_W),
    }
    scm = state_conv_m[l]
    state_s = (
        jnp.moveaxis(_group_major(scm[..., :d_inner], GROUP_W), 0, 1),
        jnp.moveaxis(_group_major(scm[..., d_inner:d_inner + bc_w], D_STATE), 0, 1),
        jnp.moveaxis(_group_major(scm[..., d_inner + bc_w:], D_STATE), 0, 1),
        state_ssm[l].reshape(nbs, N_GROUPS, GROUP_W, D_STATE),
    )
    q_p = _pick(seq_p, (128, 64, 32, 16, 8))
    y_p, cx_p, cb_p, cc_p, h_p = _ssd(zx_p, bc_p, dt_p, prm, nbp, seq_p, q_p)
    y_s, cx_s, cb_s, cc_s, h_s = _ssd(zx_s, bc_s, dt_s, prm, nbs, seq_s, seq_s, state=state_s)

    def conv_m_state(cx, cb_, cc_):
        def flat(a):
            return jnp.moveaxis(a, 1, 2).reshape(a.shape[0], a.shape[2], -1)
        return jnp.concatenate([flat(cx), flat(cb_), flat(cc_)], axis=-1)

    merged_p, merged_s = _merge(abv_p, abv_s, y_p, y_s, g_p, g_s, w_a_out[l],
                                w_m_out[l].reshape(N_GROUPS, GROUP_W, d))
    h1_p, h1_s, hn1_p, hn1_s = _res(merged_p, merged_s, xp, xs, w_o[l], norm_cross[l][None], False, BF16)

    att_p = _attn(hn1_p, mk2.reshape(nbp, n_mem, d), mv2.reshape(nbp, n_mem, d), w_q[l], nbp, seq_p)
    att_s = _attn(hn1_s, cache_mem_k[l].reshape(nbs, n_mem, d), cache_mem_v[l].reshape(nbs, n_mem, d),
                  w_q[l], nbs, seq_s)
    h2_all, hn2_all = _res(att_p, att_s, h1_p, h1_s, w_co[l], norm_ffn[l][None], True, F32)

    npad = LANES - N_EXPERT_GROUPS - N_EXPERTS
    w_r = jnp.concatenate([w_rg[l], w_re[l], jnp.zeros((d, npad), F32)], axis=1)
    b_r = jnp.concatenate([b_rg[l], b_re[l], jnp.zeros((npad,), F32)])[None]
    ri, rw = _router(hn2_all, w_r, b_r)
    rows_per_block = 1024
    dest, items = _routing_tables(ri[:, :2].reshape(-1), rows_per_block)
    x_sorted = _dispatch(dest, hn2_all)
    y_sorted = _experts(items, x_sorted, w_gate[l], w_up[l], w_down[l], rows_per_block, 256)
    y_prompt, y_sample = _final(dest, h2_all, rw, y_sorted, norm_final[None], tp, ts)

    xh = d // N_XHEADS
    return (
        y_prompt.reshape(nbp, seq_p, d),
        y_sample.reshape(nbs, seq_s, d),
        mk2.reshape(1, nbp, n_mem, N_XHEADS, xh),
        mv2.reshape(1, nbp, n_mem, N_XHEADS, xh),
        ca_p[None],
        conv_m_state(cx_p, cb_p, cc_p)[None],
        h_p.reshape(1, nbp, nheads, HEAD_DIM, D_STATE),
        ca_s[None],
        conv_m_state(cx_s, cb_s, cc_s)[None],
        h_s.reshape(1, nbs, nheads, HEAD_DIM, D_STATE),
    )
```

```python
import functools

import jax
import jax.numpy as jnp
from jax import lax
from jax.experimental import pallas as pl
from jax.experimental.pallas import tpu as pltpu

F32 = jnp.float32
BF16 = jnp.bfloat16
EPS = 1e-6

V7X_VMEM_BYTES = 64 * 1024 * 1024
VMEM_LIMIT = V7X_VMEM_BYTES - 8 * 1024 * 1024
LANES = 128
SUBLANES = 8

N_GROUPS = 8
HEADS_PER_GROUP = 8
HEAD_DIM = 64
D_STATE = 128
GROUP_W = HEADS_PER_GROUP * HEAD_DIM
N_XHEADS = 4
N_EXPERTS = 32
N_EXPERT_GROUPS = 4
EXPERTS_PER_GROUP = 8
EXPERT_SUB = 256
CONV_A_K = 3
CONV_M_K = 4

NT_DIMS = (((1,), (1,)), ((), ()))
TN_DIMS = (((0,), (0,)), ((), ()))


def _params(sem):
    return pltpu.CompilerParams(dimension_semantics=sem, vmem_limit_bytes=VMEM_LIMIT)


def _pick(n, cands):
    for c in cands:
        if n % c == 0:
            return c
    raise ValueError(f"no tile for {n} in {cands}")


def _dot(a, b):
    return jnp.dot(a, b, preferred_element_type=F32)


def _dot_nt(a, b):
    return lax.dot_general(a, b, NT_DIMS, preferred_element_type=F32)


def _rms(x, g):
    return x * lax.rsqrt(jnp.mean(x * x, axis=-1, keepdims=True) + EPS) * g


def _split3(x):
    hi = x.astype(BF16)
    r = x - hi.astype(F32)
    mid = r.astype(BF16)
    lo = (r - mid.astype(F32)).astype(BF16)
    return hi, mid, lo


def _softplus(x):
    return jnp.maximum(x, 0.0) + jnp.log1p(jnp.exp(-jnp.abs(x)))


def _norm_dt_kernel(xp_ref, xs_ref, g_ref, wdt_ref, xnp_ref, xns_ref, dtp_ref, dts_ref):
    wdt = wdt_ref[...].astype(BF16)

    def one(x_ref, xn_ref, dt_ref):
        xn = _rms(x_ref[...], g_ref[...]).astype(BF16)
        xn_ref[...] = xn
        dt_ref[...] = _dot_nt(xn, wdt)

    one(xp_ref, xnp_ref, dtp_ref)

    @pl.when(pl.program_id(0) == 0)
    def _():
        one(xs_ref, xns_ref, dts_ref)


def _norm_dt(xp, xs, g, wdt):
    tp, d = xp.shape
    ts = xs.shape[0]
    nh = wdt.shape[0]
    tm = _pick(tp, (512, 256, 128))
    return pl.pallas_call(
        _norm_dt_kernel,
        grid=(tp // tm,),
        in_specs=[
            pl.BlockSpec((tm, d), lambda i: (i, 0)),
            pl.BlockSpec((ts, d), lambda i: (0, 0)),
            pl.BlockSpec((1, d), lambda i: (0, 0)),
            pl.BlockSpec((nh, d), lambda i: (0, 0)),
        ],
        out_specs=[
            pl.BlockSpec((tm, d), lambda i: (i, 0)),
            pl.BlockSpec((ts, d), lambda i: (0, 0)),
            pl.BlockSpec((tm, nh), lambda i: (i, 0)),
            pl.BlockSpec((ts, nh), lambda i: (0, 0)),
        ],
        out_shape=[
            jax.ShapeDtypeStruct((tp, d), BF16),
            jax.ShapeDtypeStruct((ts, d), BF16),
            jax.ShapeDtypeStruct((tp, nh), F32),
            jax.ShapeDtypeStruct((ts, nh), F32),
        ],
        compiler_params=_params(("arbitrary",)),
        name="norm_dt",
    )(xp, xs, g, wdt)


def _proj_a_kernel(xp_ref, xs_ref, wb_ref, wc_ref, wh_ref, cw_ref, st_ref,
                   op_ref, os_ref, cap_ref, cas_ref,
                   wbf, ubuf, sbuf, s1buf, s2buf, *, tiles_per_batch, nb_s, l_s):
    i = pl.program_id(1)
    tm = xp_ref.shape[0]
    ts = xs_ref.shape[0]
    cw = cw_ref[...]

    @pl.when(i == 0)
    def _():
        wbf[0] = wb_ref[...].astype(BF16)
        wbf[1] = wc_ref[...].astype(BF16)
        wbf[2] = wh_ref[...].astype(BF16)
        x = xs_ref[...]
        u = _dot_nt(x, wbf[1]) * _dot_nt(x, wbf[2])
        sbuf[pl.ds(0, SUBLANES), :] = jnp.zeros((SUBLANES, u.shape[1]), F32)
        sbuf[pl.ds(SUBLANES, ts), :] = u
        s1buf[...] = jnp.zeros_like(s1buf)
        s2buf[...] = jnp.zeros_like(s2buf)
        for b in range(nb_s):
            s1buf[pl.ds(b * l_s, 1), :] = st_ref[b, pl.ds(1, 1), :]
            s2buf[pl.ds(b * l_s, 1), :] = st_ref[b, pl.ds(0, 1), :]
            s2buf[pl.ds(b * l_s + 1, 1), :] = st_ref[b, pl.ds(1, 1), :]
        rmod = lax.broadcasted_iota(jnp.int32, (ts, 1), 0) % l_s
        prev1 = jnp.where(rmod == 0, s1buf[...], sbuf[pl.ds(SUBLANES - 1, ts), :])
        prev2 = jnp.where(rmod < 2, s2buf[...], sbuf[pl.ds(SUBLANES - 2, ts), :])
        v = prev2 * cw[0:1, :] + prev1 * cw[1:2, :] + u * cw[2:3, :]
        os_ref[...] = (_dot_nt(x, wbf[0]) * v).astype(BF16)
        for b in range(nb_s):
            cas_ref[b] = sbuf[pl.ds(SUBLANES + (b + 1) * l_s - 2, 2), :]

    @pl.when(i % tiles_per_batch == 0)
    def _():
        ubuf[pl.ds(0, SUBLANES), :] = jnp.zeros((SUBLANES, ubuf.shape[1]), F32)

    x = xp_ref[...]
    u = _dot_nt(x, wbf[1]) * _dot_nt(x, wbf[2])
    ubuf[pl.ds(SUBLANES, tm), :] = u
    v = (ubuf[pl.ds(SUBLANES - 2, tm), :] * cw[0:1, :]
         + ubuf[pl.ds(SUBLANES - 1, tm), :] * cw[1:2, :] + u * cw[2:3, :])
    op_ref[...] = (_dot_nt(x, wbf[0]) * v).astype(BF16)
    ubuf[pl.ds(0, SUBLANES), :] = ubuf[pl.ds(tm, SUBLANES), :]

    @pl.when(i % tiles_per_batch == tiles_per_batch - 1)
    def _():
        cap_ref[0] = ubuf[pl.ds(SUBLANES + tm - 2, 2), :]


def _proj_a(xnp, xns, w_in, conv_w, state_s, n_batch_p, seq_p, d_conv):
    tp, d = xnp.shape
    ts = xns.shape[0]
    nb_s = state_s.shape[0]
    l_s = ts // nb_s
    tn = 512
    tm = _pick(seq_p, (1024, 512, 256, 128))
    tpb = seq_p // tm
    ncol = d_conv // tn
    kern = functools.partial(_proj_a_kernel, tiles_per_batch=tpb, nb_s=nb_s, l_s=l_s)
    return pl.pallas_call(
        kern,
        grid=(ncol, tp // tm),
        in_specs=[
            pl.BlockSpec((tm, d), lambda j, i: (i, 0)),
            pl.BlockSpec((ts, d), lambda j, i: (0, 0)),
            pl.BlockSpec((tn, d), lambda j, i: (j, 0)),
            pl.BlockSpec((tn, d), lambda j, i: (j + ncol, 0)),
            pl.BlockSpec((tn, d), lambda j, i: (j + 2 * ncol, 0)),
            pl.BlockSpec((CONV_A_K, tn), lambda j, i: (0, j)),
            pl.BlockSpec((nb_s, CONV_A_K - 1, tn), lambda j, i: (0, 0, j)),
        ],
        out_specs=[
            pl.BlockSpec((tm, tn), lambda j, i: (i, j)),
            pl.BlockSpec((ts, tn), lambda j, i: (0, j)),
            pl.BlockSpec((1, CONV_A_K - 1, tn), lambda j, i: (i // tpb, 0, j)),
            pl.BlockSpec((nb_s, CONV_A_K - 1, tn), lambda j, i: (0, 0, j)),
        ],
        out_shape=[
            jax.ShapeDtypeStruct((tp, d_conv), BF16),
            jax.ShapeDtypeStruct((ts, d_conv), BF16),
            jax.ShapeDtypeStruct((n_batch_p, CONV_A_K - 1, d_conv), F32),
            jax.ShapeDtypeStruct((nb_s, CONV_A_K - 1, d_conv), F32),
        ],
        scratch_shapes=[
            pltpu.VMEM((3, tn, d), BF16),
            pltpu.VMEM((SUBLANES + tm, tn), F32),
            pltpu.VMEM((SUBLANES + ts, tn), F32),
            pltpu.VMEM((ts, tn), F32),
            pltpu.VMEM((ts, tn), F32),
        ],
        compiler_params=_params(("arbitrary", "arbitrary")),
        name="proj_a",
    )(xnp, xns, w_in, w_in, w_in, conv_w, state_s)


def _proj_raw_kernel(xp_ref, xs_ref, w_ref, op_ref, os_ref, wbf, *, nsplit, width):
    def emit(x_ref, o_ref):
        acc = _dot_nt(x_ref[...], wbf[...])
        if nsplit == 0:
            o_ref[...] = acc
        else:
            for s in range(nsplit):
                o_ref[s] = acc[:, s * width:(s + 1) * width]

    @pl.when(pl.program_id(1) == 0)
    def _():
        wbf[...] = w_ref[...].astype(BF16)
        emit(xs_ref, os_ref)

    emit(xp_ref, op_ref)


def _proj_raw(xnp, xns, w, col0, ncols, width):
    tp, d = xnp.shape
    ts = xns.shape[0]
    tn = 1024
    tm = _pick(tp, (1024, 512, 256, 128))
    assert col0 % tn == 0 and ncols % tn == 0
    jb = col0 // tn
    if width == 0:
        nsplit = 0
        out_specs = [pl.BlockSpec((tm, tn), lambda j, i: (i, j)),
                     pl.BlockSpec((ts, tn), lambda j, i: (0, j))]
        out_shape = [jax.ShapeDtypeStruct((tp, ncols), F32), jax.ShapeDtypeStruct((ts, ncols), F32)]
    else:
        nsplit = tn // width
        out_specs = [pl.BlockSpec((nsplit, tm, width), lambda j, i: (j, i, 0)),
                     pl.BlockSpec((nsplit, ts, width), lambda j, i: (j, 0, 0))]
        out_shape = [jax.ShapeDtypeStruct((ncols // width, tp, width), F32),
                     jax.ShapeDtypeStruct((ncols // width, ts, width), F32)]
    kern = functools.partial(_proj_raw_kernel, nsplit=nsplit, width=width)
    return pl.pallas_call(
        kern,
        grid=(ncols // tn, tp // tm),
        in_specs=[
            pl.BlockSpec((tm, d), lambda j, i: (i, 0)),
            pl.BlockSpec((ts, d), lambda j, i: (0, 0)),
            pl.BlockSpec((tn, d), lambda j, i: (j + jb, 0)),
        ],
        out_specs=out_specs,
        out_shape=out_shape,
        scratch_shapes=[pltpu.VMEM((tn, d), BF16)],
        compiler_params=_params(("arbitrary", "arbitrary")),
        name=f"proj_raw_{col0}",
    )(xnp, xns, w)


def _ssd_kernel(*refs, q, has_state, nchunks):
    (z_ref, xs_ref, b_ref, c_ref, dt_ref, cwx, cwb, cwc, cbx, cbb, cbc,
     dtb_ref, alog_ref, dsk_ref, gn_ref, *rest) = refs
    if has_state:
        cpx, cpb, cpc, hprev, *rest = rest
    (y_ref, ocx, ocb, occ, oh, h_s, ccx, ccb, ccc, xbuf, bbuf, cbuf, acg, rowt) = rest
    c = pl.program_id(1)
    nheads = N_GROUPS * HEADS_PER_GROUP
    tail = CONV_M_K - 1

    @pl.when(c == 0)
    def _init():
        ccx[...] = jnp.zeros_like(ccx)
        ccb[...] = jnp.zeros_like(ccb)
        ccc[...] = jnp.zeros_like(ccc)
        if has_state:
            for g in range(N_GROUPS):
                ccx[g, pl.ds(SUBLANES - tail, tail), :] = cpx[g]
                ccb[g, pl.ds(SUBLANES - tail, tail), :] = cpb[g]
                ccc[g, pl.ds(SUBLANES - tail, tail), :] = cpc[g]
            h_s[...] = hprev[...]
        else:
            h_s[...] = jnp.zeros_like(h_s)

    dt = _softplus(dt_ref[...] + dtb_ref[...])
    da = dt * (-jnp.exp(alog_ref[...]))
    ri = lax.broadcasted_iota(jnp.int32, (q, q), 0)
    ci = lax.broadcasted_iota(jnp.int32, (q, q), 1)
    causal = ri >= ci
    tril = jnp.where(causal, 1.0, 0.0).astype(BF16)
    acum = sum(_dot(tril, p) for p in _split3(da))
    eye = jnp.where(lax.broadcasted_iota(jnp.int32, (nheads, nheads), 0)
                    == lax.broadcasted_iota(jnp.int32, (nheads, nheads), 1), 1.0, 0.0).astype(BF16)
    rowt[0] = sum(lax.dot_general(eye, p, NT_DIMS, preferred_element_type=F32) for p in _split3(acum))
    rowt[1] = sum(lax.dot_general(eye, p, NT_DIMS, preferred_element_type=F32) for p in _split3(dt))
    wend = jnp.exp(acum[q - 1:q, :] - acum) * dt
    rowt[2] = sum(lax.dot_general(eye, p, NT_DIMS, preferred_element_type=F32) for p in _split3(wend))
    for g in range(N_GROUPS):
        acg[g] = acum[:, g * HEADS_PER_GROUP:(g + 1) * HEADS_PER_GROUP]

    lane = lax.broadcasted_iota(jnp.int32, (1, LANES), 1)
    rowi = lax.broadcasted_iota(jnp.int32, (LANES, 1), 0)
    half_w = LANES // 2

    def conv(raw_ref, cc, buf, cw, cb, ost, g):
        buf[pl.ds(0, SUBLANES), :] = cc[g]
        raw = raw_ref[g]
        buf[pl.ds(SUBLANES, q), :] = raw
        w = cw[g]
        acc = raw * w[tail:tail + 1, :]
        for k in range(tail):
            acc = acc + buf[pl.ds(SUBLANES - tail + k, q), :] * w[k:k + 1, :]
        cc[g] = buf[pl.ds(q, SUBLANES), :]

        @pl.when(c == nchunks - 1)
        def _():
            ost[g] = buf[pl.ds(SUBLANES + q - tail, tail), :]

        return jax.nn.silu(acc + cb[g])

    def group_body(g, carry):
        xs = conv(xs_ref, ccx, xbuf, cwx, cbx, ocx, g)
        bc = conv(b_ref, ccb, bbuf, cwb, cbb, ocb, g)
        cc_ = conv(c_ref, ccc, cbuf, cwc, cbc, occ, g)
        bb = bc.astype(BF16)
        ccb16 = cc_.astype(BF16)
        cb_ = lax.dot_general(ccb16, bb, NT_DIMS, preferred_element_type=F32)
        ac8 = acg[g]
        dsk = dsk_ref[g]
        lo_half = lane < half_w
        lo_rows = rowi < half_w
        ys = []
        for pair in range(HEADS_PER_GROUP // 2):
            sl = slice(pair * LANES, (pair + 1) * LANES)
            xp = xs[:, sl]
            hp = h_s[g, pl.ds(pair * LANES, LANES), :]
            ms_, acols, wrows = [], [], []
            for r in (2 * pair, 2 * pair + 1):
                head = g * HEADS_PER_GROUP + r
                acol = jnp.broadcast_to(ac8[:, r:r + 1], (q, LANES))
                arow = rowt[0, pl.ds(head, 1), :]
                drow = rowt[1, pl.ds(head, 1), :]
                decay = jnp.exp(jnp.where(causal, acol[:, :q] - arow, -jnp.inf))
                ms_.append((cb_ * decay * drow).astype(BF16))
                acols.append(acol)
                wrows.append(jnp.broadcast_to(rowt[2, pl.ds(head, 1), :], (half_w, q)))
            xlo = jnp.where(lo_half, xp, 0.0).astype(BF16)
            xhi = jnp.where(lo_half, 0.0, xp).astype(BF16)
            if q % LANES == 0:
                ydiag = _dot(jnp.concatenate(ms_, axis=1), jnp.concatenate([xlo, xhi], axis=0))
            else:
                ydiag = _dot(ms_[0], xlo) + _dot(ms_[1], xhi)
            ea = jnp.where(lo_half, jnp.exp(acols[0]), jnp.exp(acols[1]))
            yoff = ea * lax.dot_general(ccb16, hp.astype(BF16), NT_DIMS, preferred_element_type=F32)
            ys.append(dsk[:, sl] * xp + ydiag + yoff)
            xwt = (xp.T * jnp.concatenate(wrows, axis=0)).astype(BF16)
            dlast = [jnp.broadcast_to(jnp.exp(a[q - 1:q, :]), (LANES, LANES)) for a in acols]
            h_s[g, pl.ds(pair * LANES, LANES), :] = hp * jnp.where(lo_rows, dlast[0], dlast[1]) + _dot(xwt, bb)
        yg = jnp.concatenate(ys, axis=1)
        hh = yg * jax.nn.silu(z_ref[g])
        ms = jnp.mean(hh * hh, axis=-1, keepdims=True)
        y_ref[g] = (hh * lax.rsqrt(ms + EPS) * gn_ref[g]).astype(BF16)
        return carry

    lax.fori_loop(0, N_GROUPS, group_body, 0)

    @pl.when(c == nchunks - 1)
    def _():
        oh[...] = h_s[...]


def _ssd(zx, bc, dt_raw, prm, n_batch, seq, q, state=None):
    nchunks = seq // q
    nheads = N_GROUPS * HEADS_PER_GROUP
    has_state = state is not None
    g8 = N_GROUPS

    def tok(first):
        return lambda b, c: (first, b * nchunks + c, 0)

    def const3(b, c):
        return (0, 0, 0)

    in_specs = [
        pl.BlockSpec((g8, q, GROUP_W), tok(0)),
        pl.BlockSpec((g8, q, GROUP_W), tok(1)),
        pl.BlockSpec((g8, q, D_STATE), tok(0)),
        pl.BlockSpec((g8, q, D_STATE), tok(1)),
        pl.BlockSpec((q, nheads), lambda b, c: (b * nchunks + c, 0)),
        pl.BlockSpec((g8, CONV_M_K, GROUP_W), const3),
        pl.BlockSpec((g8, CONV_M_K, D_STATE), const3),
        pl.BlockSpec((g8, CONV_M_K, D_STATE), const3),
        pl.BlockSpec((g8, 1, GROUP_W), const3),
        pl.BlockSpec((g8, 1, D_STATE), const3),
        pl.BlockSpec((g8, 1, D_STATE), const3),
        pl.BlockSpec((1, nheads), lambda b, c: (0, 0)),
        pl.BlockSpec((1, nheads), lambda b, c: (0, 0)),
        pl.BlockSpec((g8, 1, GROUP_W), const3),
        pl.BlockSpec((g8, 1, GROUP_W), const3),
    ]
    args = [zx, zx, bc, bc, dt_raw, prm["cwx"], prm["cwb"], prm["cwc"], prm["cbx"], prm["cbb"], prm["cbc"],
            prm["dtb"], prm["alog"], prm["dsk"], prm["gn"]]
    tail = CONV_M_K - 1

    def per_batch(shape):
        return pl.BlockSpec((None,) + shape, lambda b, c: (b,) + (0,) * len(shape))

    if has_state:
        in_specs += [per_batch((g8, tail, GROUP_W)), per_batch((g8, tail, D_STATE)),
                     per_batch((g8, tail, D_STATE)), per_batch((g8, GROUP_W, D_STATE))]
        args += list(state)
    t = n_batch * seq
    out_specs = [
        pl.BlockSpec((g8, q, GROUP_W), lambda b, c: (0, b * nchunks + c, 0)),
        per_batch((g8, tail, GROUP_W)), per_batch((g8, tail, D_STATE)), per_batch((g8, tail, D_STATE)),
        per_batch((g8, GROUP_W, D_STATE)),
    ]
    out_shape = [
        jax.ShapeDtypeStruct((g8, t, GROUP_W), BF16),
        jax.ShapeDtypeStruct((n_batch, g8, tail, GROUP_W), F32),
        jax.ShapeDtypeStruct((n_batch, g8, tail, D_STATE), F32),
        jax.ShapeDtypeStruct((n_batch, g8, tail, D_STATE), F32),
        jax.ShapeDtypeStruct((n_batch, g8, GROUP_W, D_STATE), F32),
    ]
    scratch = [
        pltpu.VMEM((g8, GROUP_W, D_STATE), F32),
        pltpu.VMEM((g8, SUBLANES, GROUP_W), F32),
        pltpu.VMEM((g8, SUBLANES, D_STATE), F32),
        pltpu.VMEM((g8, SUBLANES, D_STATE), F32),
        pltpu.VMEM((SUBLANES + q, GROUP_W), F32),
        pltpu.VMEM((SUBLANES + q, D_STATE), F32),
        pltpu.VMEM((SUBLANES + q, D_STATE), F32),
        pltpu.VMEM((g8, q, HEADS_PER_GROUP), F32),
        pltpu.VMEM((3, nheads, q), F32),
    ]
    kern = functools.partial(_ssd_kernel, q=q, has_state=has_state, nchunks=nchunks)
    return pl.pallas_call(
        kern,
        grid=(n_batch, nchunks),
        in_specs=in_specs,
        out_specs=out_specs,
        out_shape=out_shape,
        scratch_shapes=scratch,
        compiler_params=_params(("arbitrary", "arbitrary")),
        name="ssd_state" if has_state else "ssd",
    )(*args)


def _merge_kernel(ap_ref, as_ref, yp_ref, ys_ref, gap_ref, gmp_ref, gas_ref, gms_ref, wa_ref, wm_ref,
                  op_ref, os_ref, wab, wmb):
    def emit(a_ref, y_ref, ga_ref, gm_ref, o_ref):
        oa = _dot(a_ref[...], wab[...])
        om = _dot(y_ref[0], wmb[0])
        for g in range(1, N_GROUPS):
            om = om + _dot(y_ref[g], wmb[g])
        o_ref[...] = (jax.nn.sigmoid(ga_ref[...]) * oa + jax.nn.sigmoid(gm_ref[...]) * om).astype(BF16)

    @pl.when(pl.program_id(1) == 0)
    def _():
        wab[...] = wa_ref[...].astype(BF16)
        wmb[...] = wm_ref[...].astype(BF16)
        emit(as_ref, ys_ref, gas_ref, gms_ref, os_ref)

    emit(ap_ref, yp_ref, gap_ref, gmp_ref, op_ref)


def _merge(abv_p, abv_s, y_p, y_s, g_p, g_s, w_a_out, w_m_out3):
    tp, dc = abv_p.shape
    ts = abv_s.shape[0]
    dm = w_a_out.shape[1]
    tn = 512
    tm = _pick(tp, (512, 256, 128))
    ncol = dm // tn
    return pl.pallas_call(
        _merge_kernel,
        grid=(ncol, tp // tm),
        in_specs=[
            pl.BlockSpec((tm, dc), lambda j, i: (i, 0)),
            pl.BlockSpec((ts, dc), lambda j, i: (0, 0)),
            pl.BlockSpec((N_GROUPS, tm, GROUP_W), lambda j, i: (0, i, 0)),
            pl.BlockSpec((N_GROUPS, ts, GROUP_W), lambda j, i: (0, 0, 0)),
            pl.BlockSpec((tm, tn), lambda j, i: (i, j)),
            pl.BlockSpec((tm, tn), lambda j, i: (i, j + ncol)),
            pl.BlockSpec((ts, tn), lambda j, i: (0, j)),
            pl.BlockSpec((ts, tn), lambda j, i: (0, j + ncol)),
            pl.BlockSpec((dc, tn), lambda j, i: (0, j)),
            pl.BlockSpec((N_GROUPS, GROUP_W, tn), lambda j, i: (0, 0, j)),
        ],
        out_specs=[
            pl.BlockSpec((tm, tn), lambda j, i: (i, j)),
            pl.BlockSpec((ts, tn), lambda j, i: (0, j)),
        ],
        out_shape=[jax.ShapeDtypeStruct((tp, dm), BF16), jax.ShapeDtypeStruct((ts, dm), BF16)],
        scratch_shapes=[pltpu.VMEM((dc, tn), BF16), pltpu.VMEM((N_GROUPS, GROUP_W, tn), BF16)],
        compiler_params=_params(("arbitrary", "arbitrary")),
        name="merge",
    )(abv_p, abv_s, y_p, y_s, g_p, g_p, g_s, g_s, w_a_out, w_m_out3)


def _res_kernel(ap_ref, as_ref, rp_ref, rs_ref, w_ref, g_ref, *outs, n_prompt_tiles, merged_out, ts):
    if merged_out:
        h_ref, hn_ref, wbf = outs
    else:
        hp_ref, hs_ref, hnp_ref, hns_ref, wbf = outs
    i = pl.program_id(0)

    @pl.when(i == 0)
    def _():
        wbf[...] = w_ref[...].astype(BF16)

    def emit(a_ref, r_ref, store_h, store_hn):
        h = r_ref[...] + _dot(a_ref[...], wbf[...])
        store_h(h)
        store_hn(_rms(h, g_ref[...]))

    if merged_out:
        @pl.when(i < n_prompt_tiles)
        def _():
            def sh(h):
                h_ref[...] = h

            def shn(hn):
                hn_ref[...] = hn

            emit(ap_ref, rp_ref, sh, shn)

        @pl.when(i == n_prompt_tiles)
        def _():
            def sh(h):
                h_ref[pl.ds(0, ts), :] = h

            def shn(hn):
                hn_ref[pl.ds(0, ts), :] = hn

            emit(as_ref, rs_ref, sh, shn)
    else:
        def shp(h):
            hp_ref[...] = h

        def shnp(hn):
            hnp_ref[...] = hn.astype(hnp_ref.dtype)

        emit(ap_ref, rp_ref, shp, shnp)

        @pl.when(i == 0)
        def _():
            def shs(h):
                hs_ref[...] = h

            def shns(hn):
                hns_ref[...] = hn.astype(hns_ref.dtype)

            emit(as_ref, rs_ref, shs, shns)


def _res(a_p, a_s, r_p, r_s, w, g, merged_out, hn_dtype):
    tp, d = a_p.shape
    ts = a_s.shape[0]
    tm = 256
    assert tp % tm == 0 and ts <= tm
    npt = tp // tm
    last = npt - 1
    in_specs = [
        pl.BlockSpec((tm, d), lambda i: (jnp.minimum(i, last), 0)),
        pl.BlockSpec((ts, d), lambda i: (0, 0)),
        pl.BlockSpec((tm, d), lambda i: (jnp.minimum(i, last), 0)),
        pl.BlockSpec((ts, d), lambda i: (0, 0)),
        pl.BlockSpec((d, d), lambda i: (0, 0), pipeline_mode=pl.Buffered(1)),
        pl.BlockSpec((1, d), lambda i: (0, 0)),
    ]
    if merged_out:
        grid = (npt + 1,)
        out_specs = [pl.BlockSpec((tm, d), lambda i: (i, 0)), pl.BlockSpec((tm, d), lambda i: (i, 0))]
        out_shape = [jax.ShapeDtypeStruct((tp + ts, d), F32), jax.ShapeDtypeStruct((tp + ts, d), hn_dtype)]
    else:
        grid = (npt,)
        out_specs = [pl.BlockSpec((tm, d), lambda i: (i, 0)), pl.BlockSpec((ts, d), lambda i: (0, 0)),
                     pl.BlockSpec((tm, d), lambda i: (i, 0)), pl.BlockSpec((ts, d), lambda i: (0, 0))]
        out_shape = [jax.ShapeDtypeStruct((tp, d), F32), jax.ShapeDtypeStruct((ts, d), F32),
                     jax.ShapeDtypeStruct((tp, d), hn_dtype), jax.ShapeDtypeStruct((ts, d), hn_dtype)]
    kern = functools.partial(_res_kernel, n_prompt_tiles=npt, merged_out=merged_out, ts=ts)
    return pl.pallas_call(
        kern,
        grid=grid,
        in_specs=in_specs,
        out_specs=out_specs,
        out_shape=out_shape,
        scratch_shapes=[pltpu.VMEM((d, d), BF16)],
        compiler_params=_params(("arbitrary",)),
        name="res_merged" if merged_out else "res",
    )(a_p, a_s, r_p, r_s, w, g)


def _kv_kernel(m_ref, g_ref, wk_ref, wv_ref, k_ref, v_ref, mn):
    @pl.when(pl.program_id(0) == 0)
    def _():
        mn[...] = _rms(m_ref[...], g_ref[...]).astype(BF16)

    k_ref[...] = _dot(mn[...], wk_ref[...].astype(BF16))
    v_ref[...] = _dot(mn[...], wv_ref[...].astype(BF16))


def _memory_kv(mem2d, g, w_k, w_v):
    m, d = mem2d.shape
    tn = 512
    return pl.pallas_call(
        _kv_kernel,
        grid=(d // tn,),
        in_specs=[
            pl.BlockSpec((m, d), lambda j: (0, 0)),
            pl.BlockSpec((1, d), lambda j: (0, 0)),
            pl.BlockSpec((d, tn), lambda j: (0, j)),
            pl.BlockSpec((d, tn), lambda j: (0, j)),
        ],
        out_specs=[pl.BlockSpec((m, tn), lambda j: (0, j)), pl.BlockSpec((m, tn), lambda j: (0, j))],
        out_shape=[jax.ShapeDtypeStruct((m, d), F32), jax.ShapeDtypeStruct((m, d), F32)],
        scratch_shapes=[pltpu.VMEM((m, d), BF16)],
        compiler_params=_params(("arbitrary",)),
        name="memory_kv",
    )(mem2d, g, w_k, w_v)


def _attn_kernel(hn_ref, k_ref, v_ref, wq_ref, o_ref, wqb, kb, vb):
    b = pl.program_id(0)
    i = pl.program_id(1)

    @pl.when((b == 0) & (i == 0))
    def _():
        wqb[...] = wq_ref[...].astype(BF16)

    @pl.when(i == 0)
    def _():
        kb[...] = k_ref[...].astype(BF16)
        vb[...] = v_ref[...].astype(BF16)

    d = wqb.shape[1]
    dh = d // N_XHEADS
    q = _dot(hn_ref[...], wqb[...])
    outs = []
    for h in range(N_XHEADS):
        sl = slice(h * dh, (h + 1) * dh)
        s = lax.dot_general(q[:, sl].astype(BF16), kb[:, sl], NT_DIMS, preferred_element_type=F32)
        s = s * (dh ** -0.5)
        e = jnp.exp(s - jnp.max(s, axis=-1, keepdims=True))
        p = e / jnp.sum(e, axis=-1, keepdims=True)
        outs.append(_dot(p.astype(BF16), vb[:, sl]))
    o_ref[...] = jnp.concatenate(outs, axis=1).astype(BF16)


def _attn(hn, k3, v3, w_q, n_batch, seq):
    t, d = hn.shape
    nm = k3.shape[1]
    tm = _pick(seq, (256, 128, 64, 32, 16))
    tpb = seq // tm
    return pl.pallas_call(
        _attn_kernel,
        grid=(n_batch, tpb),
        in_specs=[
            pl.BlockSpec((tm, d), lambda b, i: (b * tpb + i, 0)),
            pl.BlockSpec((None, nm, d), lambda b, i: (b, 0, 0)),
            pl.BlockSpec((None, nm, d), lambda b, i: (b, 0, 0)),
            pl.BlockSpec((d, d), lambda b, i: (0, 0), pipeline_mode=pl.Buffered(1)),
        ],
        out_specs=pl.BlockSpec((tm, d), lambda b, i: (b * tpb + i, 0)),
        out_shape=jax.ShapeDtypeStruct((t, d), BF16),
        scratch_shapes=[pltpu.VMEM((d, d), BF16), pltpu.VMEM((nm, d), BF16), pltpu.VMEM((nm, d), BF16)],
        compiler_params=_params(("arbitrary", "arbitrary")),
        name=f"attn_{seq}",
    )(hn, k3, v3, w_q)


def _router_kernel(x_ref, w_ref, b_ref, ri_ref, rw_ref):
    logits = _dot(x_ref[...].astype(BF16), w_ref[...].astype(BF16)) + b_ref[...]
    lane_i = lax.broadcasted_iota(jnp.int32, logits.shape, 1)
    lane = lane_i.astype(F32)
    ninf = -jnp.inf
    big = float(LANES)
    is_g = lane < N_EXPERT_GROUPS
    gl = jnp.where(is_g, logits, ninf)
    gmax = jnp.max(gl, axis=-1, keepdims=True)
    gsel = jnp.min(jnp.where(gl == gmax, lane, big), axis=-1, keepdims=True)
    pg = 1.0 / jnp.sum(jnp.where(is_g, jnp.exp(gl - gmax), 0.0), axis=-1, keepdims=True)
    lo = N_EXPERT_GROUPS + EXPERTS_PER_GROUP * gsel
    el = jnp.where(lane >= lo, jnp.where(lane < lo + EXPERTS_PER_GROUP, logits, ninf), ninf)
    m1 = jnp.max(el, axis=-1, keepdims=True)
    i1 = jnp.min(jnp.where(el == m1, lane, big), axis=-1, keepdims=True)
    el2 = jnp.where(lane == i1, ninf, el)
    m2 = jnp.max(el2, axis=-1, keepdims=True)
    i2 = jnp.min(jnp.where(el2 == m2, lane, big), axis=-1, keepdims=True)
    e = jnp.exp(m2 - m1)
    w1 = pg / (1.0 + e)
    w2 = pg * e / (1.0 + e)
    ri_ref[...] = jnp.where(lane_i == 0, i1 - N_EXPERT_GROUPS,
                            jnp.where(lane_i == 1, i2 - N_EXPERT_GROUPS, 0.0)).astype(jnp.int32)
    rw_ref[...] = jnp.where(lane_i == 0, w1, jnp.where(lane_i == 1, w2, 0.0))


def _router(hn_all, w_r, b_r):
    t, d = hn_all.shape
    tm = _pick(t, (640, 512, 384, 256, 128))
    return pl.pallas_call(
        _router_kernel,
        grid=(t // tm,),
        in_specs=[
            pl.BlockSpec((tm, d), lambda i: (i, 0)),
            pl.BlockSpec((d, LANES), lambda i: (0, 0)),
            pl.BlockSpec((1, LANES), lambda i: (0, 0)),
        ],
        out_specs=[pl.BlockSpec((tm, LANES), lambda i: (i, 0)), pl.BlockSpec((tm, LANES), lambda i: (i, 0))],
        out_shape=[jax.ShapeDtypeStruct((t, LANES), jnp.int32), jax.ShapeDtypeStruct((t, LANES), F32)],
        compiler_params=_params(("arbitrary",)),
        name="router",
    )(hn_all, w_r, b_r)


def _row_copy(src, dst, s, d, sem):
    return pltpu.make_async_copy(src.at[pl.ds(s, 1)], dst.at[pl.ds(d, 1)], sem)


def _dispatch_kernel(dest_ref, pad0_ref, padn_ref, nsub_ref, x_ref, o_ref, zrow, zblk, sem, zsem, *,
                     chunk, n_blocks):
    base = pl.program_id(0) * chunk

    @pl.when(pl.program_id(0) == 0)
    def _():
        zrow[...] = jnp.zeros_like(zrow)
        zblk[...] = jnp.zeros_like(zblk)

        def per_expert(e, carry):
            def start(r, c):
                _row_copy(zrow, o_ref, 0, pad0_ref[e] + r, zsem).start()
                return c

            def wait(r, c):
                _row_copy(zrow, o_ref, 0, 0, zsem).wait()
                return c

            lax.fori_loop(0, padn_ref[e], start, 0)
            lax.fori_loop(0, padn_ref[e], wait, 0)
            return carry

        lax.fori_loop(0, N_EXPERTS, per_expert, 0)

        def blk_copy(b):
            return pltpu.make_async_copy(zblk, o_ref.at[pl.ds(pl.multiple_of(b * EXPERT_SUB, EXPERT_SUB),
                                                             EXPERT_SUB)], zsem)

        def tail_start(b, c):
            blk_copy(b).start()
            return c

        def tail_wait(b, c):
            blk_copy(b).wait()
            return c

        lax.fori_loop(nsub_ref[0], n_blocks, tail_start, 0)
        lax.fori_loop(nsub_ref[0], n_blocks, tail_wait, 0)

    def issue(r, carry):
        t = base + r
        _row_copy(x_ref, o_ref, r, dest_ref[2 * t], sem).start()
        _row_copy(x_ref, o_ref, r, dest_ref[2 * t + 1], sem).start()
        return carry

    lax.fori_loop(0, chunk, issue, 0, unroll=8)

    for _ in range(2):
        pltpu.make_async_copy(x_ref, o_ref.at[pl.ds(0, chunk)], sem).wait()


def _dispatch(tables, hn_all, n_blocks):
    dest, pad0, padn, nsub = tables
    t, d = hn_all.shape
    chunk = _pick(t, (640, 512, 384, 256, 128))
    kern = functools.partial(_dispatch_kernel, chunk=chunk, n_blocks=n_blocks)
    return pl.pallas_call(
        kern,
        grid_spec=pltpu.PrefetchScalarGridSpec(
            num_scalar_prefetch=4,
            grid=(t // chunk,),
            in_specs=[pl.BlockSpec((chunk, d), lambda i, *_: (i, 0))],
            out_specs=pl.BlockSpec(memory_space=pl.ANY),
            scratch_shapes=[pltpu.VMEM((SUBLANES, d), F32), pltpu.VMEM((EXPERT_SUB, d), F32),
                            pltpu.SemaphoreType.DMA(()), pltpu.SemaphoreType.DMA(())],
        ),
        out_shape=jax.ShapeDtypeStruct((n_blocks * EXPERT_SUB, d), F32),
        compiler_params=_params(("arbitrary",)),
        name="dispatch",
    )(dest, pad0, padn, nsub, hn_all)


def _expert_kernel(bstart_ref, wg_ref, wu_ref, wd_ref, x_hbm, y_hbm, gcache, ucache, dcache, xbuf, ybuf,
                   xsem, ysem, *, n_blocks):
    e = pl.program_id(0)
    k = pl.program_id(1)
    nk = pl.num_programs(1)
    total = bstart_ref[N_EXPERTS]

    def x_copy(b, slot):
        rows = pl.ds(pl.multiple_of(b * EXPERT_SUB, EXPERT_SUB), EXPERT_SUB)
        return pltpu.make_async_copy(x_hbm.at[rows], xbuf.at[slot], xsem.at[slot])

    def y_copy(b, slot):
        rows = pl.ds(pl.multiple_of(b * EXPERT_SUB, EXPERT_SUB), EXPERT_SUB)
        return pltpu.make_async_copy(ybuf.at[slot], y_hbm.at[rows], ysem.at[slot])

    @pl.when(e < N_EXPERTS)
    def _():
        slot = e % 2
        gcache[slot, k] = wg_ref[...].astype(BF16)
        ucache[slot, k] = wu_ref[...].astype(BF16)
        dcache[slot, k] = wd_ref[...].astype(BF16)

    @pl.when((e == 0) & (k == 0) & (total > 0))
    def _():
        x_copy(0, 0).start()

    @pl.when(e >= 1)
    def _():
        owner = e - 1
        wslot = owner % 2
        b0 = bstart_ref[owner]
        n = bstart_ref[owner + 1] - b0

        def block(b, carry):
            slot = b % 2
            x_copy(b, slot).wait()

            @pl.when(b + 1 < total)
            def _():
                x_copy(b + 1, 1 - slot).start()

            x = xbuf[slot].astype(BF16)
            y = None
            for kk in range(gcache.shape[1]):
                hid = (jax.nn.silu(_dot(x, gcache[wslot, kk])) * _dot(x, ucache[wslot, kk])).astype(BF16)
                part = _dot(hid, dcache[wslot, kk])
                y = part if y is None else y + part

            @pl.when(b >= 2)
            def _():
                y_copy(b - 2, slot).wait()

            ybuf[slot] = y
            y_copy(b, slot).start()
            return carry

        lax.fori_loop(b0 + (n * k) // nk, b0 + (n * (k + 1)) // nk, block, 0)

    @pl.when((e == N_EXPERTS) & (k == nk - 1))
    def _():
        @pl.when(total >= 2)
        def _():
            y_copy(total - 2, total % 2).wait()

        @pl.when(total >= 1)
        def _():
            y_copy(total - 1, (total - 1) % 2).wait()

        ybuf[0] = jnp.zeros(ybuf.shape[1:], F32)

        def zfill(b, carry):
            y_copy(b, 0).start()
            y_copy(b, 0).wait()
            return carry

        lax.fori_loop(total, n_blocks, zfill, 0)


def _experts(bstart, x_sorted, w_gate, w_up, w_down, kchunk):
    nrows, d = x_sorted.shape
    n_blocks = nrows // EXPERT_SUB
    de = w_gate.shape[2]
    nk = de // kchunk
    last = N_EXPERTS - 1

    def widx(e, k):
        return jnp.minimum(e, last), jnp.where(e <= last, k, nk - 1)

    def in_idx(e, k, b):
        ee, kk = widx(e, k)
        return (ee, 0, kk)

    def down_idx(e, k, b):
        ee, kk = widx(e, k)
        return (ee, kk, 0)

    kern = functools.partial(_expert_kernel, n_blocks=n_blocks)
    return pl.pallas_call(
        kern,
        grid_spec=pltpu.PrefetchScalarGridSpec(
            num_scalar_prefetch=1,
            grid=(N_EXPERTS + 1, nk),
            in_specs=[
                pl.BlockSpec((None, d, kchunk), in_idx),
                pl.BlockSpec((None, d, kchunk), in_idx),
                pl.BlockSpec((None, kchunk, d), down_idx),
                pl.BlockSpec(memory_space=pl.ANY),
            ],
            out_specs=pl.BlockSpec(memory_space=pl.ANY),
            scratch_shapes=[
                pltpu.VMEM((2, nk, d, kchunk), BF16),
                pltpu.VMEM((2, nk, d, kchunk), BF16),
                pltpu.VMEM((2, nk, kchunk, d), BF16),
                pltpu.VMEM((2, EXPERT_SUB, d), F32),
                pltpu.VMEM((2, EXPERT_SUB, d), F32),
                pltpu.SemaphoreType.DMA((2,)),
                pltpu.SemaphoreType.DMA((2,)),
            ],
        ),
        out_shape=jax.ShapeDtypeStruct((nrows, d), F32),
        compiler_params=_params(("arbitrary", "arbitrary")),
        name="experts",
    )(bstart, w_gate, w_up, w_down, x_sorted)


def _final_kernel(dest_ref, h_ref, rw_ref, ys_ref, g_ref, yp_ref, yss_ref, ya, yb, sem, *,
                  tm, n_prompt_tiles, ts):
    i = pl.program_id(0)
    base = i * tm

    def issue(r, carry):
        t = base + r
        _row_copy(ys_ref, ya, dest_ref[2 * t], r, sem).start()
        _row_copy(ys_ref, yb, dest_ref[2 * t + 1], r, sem).start()
        return carry

    def emit(n, o_ref):
        lax.fori_loop(0, n, issue, 0, unroll=8)
        for buf in (ya, yb):
            pltpu.make_async_copy(ys_ref.at[pl.ds(0, n)], buf.at[pl.ds(0, n)], sem).wait()
        w = rw_ref[pl.ds(0, n), :]
        h = h_ref[pl.ds(0, n), :] + w[:, 0:1] * ya[pl.ds(0, n), :] + w[:, 1:2] * yb[pl.ds(0, n), :]
        o_ref[...] = _rms(h, g_ref[...])

    @pl.when(i < n_prompt_tiles)
    def _():
        emit(tm, yp_ref)

    @pl.when(i == n_prompt_tiles)
    def _():
        emit(ts, yss_ref)


def _final(dest, h_all, rw, y_sorted, g, tp, ts):
    t, d = h_all.shape
    tm = 256
    assert tp % tm == 0 and ts <= tm and t == tp + ts
    npt = tp // tm
    kern = functools.partial(_final_kernel, tm=tm, n_prompt_tiles=npt, ts=ts)
    return pl.pallas_call(
        kern,
        grid_spec=pltpu.PrefetchScalarGridSpec(
            num_scalar_prefetch=1,
            grid=(npt + 1,),
            in_specs=[
                pl.BlockSpec((tm, d), lambda i, dref: (i, 0)),
                pl.BlockSpec((tm, LANES), lambda i, dref: (i, 0)),
                pl.BlockSpec(memory_space=pl.ANY),
                pl.BlockSpec((1, d), lambda i, dref: (0, 0)),
            ],
            out_specs=[
                pl.BlockSpec((tm, d), lambda i, dref: (jnp.minimum(i, npt - 1), 0)),
                pl.BlockSpec((ts, d), lambda i, dref: (0, 0)),
            ],
            scratch_shapes=[pltpu.VMEM((tm, d), F32), pltpu.VMEM((tm, d), F32), pltpu.SemaphoreType.DMA(())],
        ),
        out_shape=[jax.ShapeDtypeStruct((tp, d), F32), jax.ShapeDtypeStruct((ts, d), F32)],
        compiler_params=_params(("arbitrary",)),
        name="final",
    )(dest, h_all, rw, y_sorted, g)


def _routing_tables(eid):
    i32 = jnp.int32
    onehot = (eid[:, None] == jnp.arange(N_EXPERTS, dtype=i32)[None, :]).astype(i32)
    csum = jnp.cumsum(onehot, axis=0)
    counts = csum[-1]
    nsub = (counts + EXPERT_SUB - 1) // EXPERT_SUB
    bend = jnp.cumsum(nsub)
    bstart = bend - nsub
    seg = bstart * EXPERT_SUB
    dest = jnp.sum(onehot * (csum - 1 + seg[None, :]), axis=1).astype(i32)
    pad0 = (seg + counts).astype(i32)
    padn = (nsub * EXPERT_SUB - counts).astype(i32)
    bstart_all = jnp.concatenate([bstart, bend[-1:]]).astype(i32)
    return (dest, pad0, padn, bend[-1:].astype(i32)), bstart_all


def _group_major(a, width):
    lead = a.shape[:-1]
    g = a.shape[-1] // width
    return jnp.moveaxis(a.reshape(lead + (g, width)), -2, 0)


def kernel(x_prompt, x_sample, cache_mem_k, cache_mem_v, state_conv_a, state_conv_m, state_ssm, mem_prompt,
           norm_mix, w_in, conv_a_w, w_a_out, conv_m_w, conv_m_b, dt_bias, a_log, d_skip, ssm_norm,
           w_m_out, w_o, norm_cross, norm_mem, w_q, w_k, w_v, w_co, norm_ffn, w_rg, b_rg, w_re, b_re,
           w_gate, w_up, w_down, norm_final):
    depth = w_in.shape[0]
    assert depth == 1, "single-layer step"
    nbp, seq_p, d = x_prompt.shape
    nbs, seq_s, _ = x_sample.shape
    tp, ts = nbp * seq_p, nbs * seq_s
    n_mem = mem_prompt.shape[1]
    d_conv = conv_a_w.shape[2]
    d_inner = w_m_out.shape[1]
    nheads = dt_bias.shape[1]
    bc_w = N_GROUPS * D_STATE
    l = 0

    xp = x_prompt.reshape(tp, d)
    xs = x_sample.reshape(ts, d)
    w_in_t = jnp.swapaxes(w_in[l], 0, 1)
    col_z = 3 * d_conv
    col_x = col_z + d_inner
    col_b = col_x + d_inner
    col_dt = col_b + 2 * bc_w
    col_g = col_dt + nheads
    w_dt = w_in_t[col_dt:col_g]
    w_gates = w_in_t[col_g:]

    mk2, mv2 = _memory_kv(mem_prompt.reshape(nbp * n_mem, d), norm_mem[l][None], w_k[l], w_v[l])

    xnp, xns, dt_p, dt_s = _norm_dt(xp, xs, norm_mix[l][None], w_dt)
    abv_p, abv_s, ca_p, ca_s = _proj_a(xnp, xns, w_in_t, conv_a_w[l], state_conv_a[l], nbp, seq_p, d_conv)
    zx_p, zx_s = _proj_raw(xnp, xns, w_in_t, col_z, 2 * d_inner, GROUP_W)
    bc_p, bc_s = _proj_raw(xnp, xns, w_in_t, col_b, 2 * bc_w, D_STATE)
    g_p, g_s = _proj_raw(xnp, xns, w_gates, 0, 2 * d, 0)

    cw = conv_m_w[l]
    cbias = conv_m_b[l][None]
    prm = {
        "cwx": _group_major(cw[:, :d_inner], GROUP_W),
        "cwb": _group_major(cw[:, d_inner:d_inner + bc_w], D_STATE),
        "cwc": _group_major(cw[:, d_inner + bc_w:], D_STATE),
        "cbx": _group_major(cbias[:, :d_inner], GROUP_W),
        "cbb": _group_major(cbias[:, d_inner:d_inner + bc_w], D_STATE),
        "cbc": _group_major(cbias[:, d_inner + bc_w:], D_STATE),
        "dtb": dt_bias[l][None],
        "alog": a_log[l][None],
        "dsk": _group_major(jnp.repeat(d_skip[l], HEAD_DIM)[None], GROUP_W),
        "gn": _group_major(ssm_norm[l][None], GROUP_W),
    }
    scm = state_conv_m[l]
    state_s = (
        jnp.moveaxis(_group_major(scm[..., :d_inner], GROUP_W), 0, 1),
        jnp.moveaxis(_group_major(scm[..., d_inner:d_inner + bc_w], D_STATE), 0, 1),
        jnp.moveaxis(_group_major(scm[..., d_inner + bc_w:], D_STATE), 0, 1),
        state_ssm[l].reshape(nbs, N_GROUPS, GROUP_W, D_STATE),
    )
    q_p = _pick(seq_p, (128, 64, 32, 16, 8))
    y_p, cx_p, cb_p, cc_p, h_p = _ssd(zx_p, bc_p, dt_p, prm, nbp, seq_p, q_p)
    y_s, cx_s, cb_s, cc_s, h_s = _ssd(zx_s, bc_s, dt_s, prm, nbs, seq_s, seq_s, state=state_s)

    def conv_m_state(cx, cb_, cc_):
        def flat(a):
            return jnp.moveaxis(a, 1, 2).reshape(a.shape[0], a.shape[2], -1)
        return jnp.concatenate([flat(cx), flat(cb_), flat(cc_)], axis=-1)

    merged_p, merged_s = _merge(abv_p, abv_s, y_p, y_s, g_p, g_s, w_a_out[l],
                                w_m_out[l].reshape(N_GROUPS, GROUP_W, d))
    h1_p, h1_s, hn1_p, hn1_s = _res(merged_p, merged_s, xp, xs, w_o[l], norm_cross[l][None], False, BF16)

    att_p = _attn(hn1_p, mk2.reshape(nbp, n_mem, d), mv2.reshape(nbp, n_mem, d), w_q[l], nbp, seq_p)
    att_s = _attn(hn1_s, cache_mem_k[l].reshape(nbs, n_mem, d), cache_mem_v[l].reshape(nbs, n_mem, d),
                  w_q[l], nbs, seq_s)
    h2_all, hn2_all = _res(att_p, att_s, h1_p, h1_s, w_co[l], norm_ffn[l][None], True, F32)

    npad = LANES - N_EXPERT_GROUPS - N_EXPERTS
    w_r = jnp.concatenate([w_rg[l], w_re[l], jnp.zeros((d, npad), F32)], axis=1)
    b_r = jnp.concatenate([b_rg[l], b_re[l], jnp.zeros((npad,), F32)])[None]
    ri, rw = _router(hn2_all, w_r, b_r)
    tables, bstart = _routing_tables(ri[:, :2].reshape(-1))
    n_blocks = -(-2 * (tp + ts) // EXPERT_SUB) + N_EXPERTS
    x_sorted = _dispatch(tables, hn2_all, n_blocks)
    y_sorted = _experts(bstart, x_sorted, w_gate[l], w_up[l], w_down[l], 256)
    y_prompt, y_sample = _final(tables[0], h2_all, rw, y_sorted, norm_final[None], tp, ts)

    xh = d // N_XHEADS
    return (
        y_prompt.reshape(nbp, seq_p, d),
        y_sample.reshape(nbs, seq_s, d),
        mk2.reshape(1, nbp, n_mem, N_XHEADS, xh),
        mv2.reshape(1, nbp, n_mem, N_XHEADS, xh),
        ca_p[None],
        conv_m_state(cx_p, cb_p, cc_p)[None],
        h_p.reshape(1, nbp, nheads, HEAD_DIM, D_STATE),
        ca_s[None],
        conv_m_state(cx_s, cb_s, cc_s)[None],
        h_s.reshape(1, nbs, nheads, HEAD_DIM, D_STATE),
    )
```

```python
import functools

import jax
import jax.numpy as jnp
from jax import lax
from jax.experimental import pallas as pl
from jax.experimental.pallas import tpu as pltpu

F32 = jnp.float32
BF16 = jnp.bfloat16
EPS = 1e-6

V7X_VMEM_BYTES = 64 * 1024 * 1024
VMEM_LIMIT = V7X_VMEM_BYTES - 8 * 1024 * 1024
LANES = 128
SUBLANES = 8

N_GROUPS = 8
HEADS_PER_GROUP = 8
HEAD_DIM = 64
D_STATE = 128
GROUP_W = HEADS_PER_GROUP * HEAD_DIM
N_XHEADS = 4
N_EXPERTS = 32
N_EXPERT_GROUPS = 4
EXPERTS_PER_GROUP = 8
EXPERT_SUB = 256
CONV_A_K = 3
CONV_M_K = 4

NT_DIMS = (((1,), (1,)), ((), ()))
TN_DIMS = (((0,), (0,)), ((), ()))


def _params(sem):
    return pltpu.CompilerParams(dimension_semantics=sem, vmem_limit_bytes=VMEM_LIMIT)


def _pick(n, cands):
    for c in cands:
        if n % c == 0:
            return c
    raise ValueError(f"no tile for {n} in {cands}")


def _dot(a, b):
    return jnp.dot(a, b, preferred_element_type=F32)


def _dot_nt(a, b):
    return lax.dot_general(a, b, NT_DIMS, preferred_element_type=F32)


def _rms(x, g):
    return x * lax.rsqrt(jnp.mean(x * x, axis=-1, keepdims=True) + EPS) * g


def _split3(x):
    hi = x.astype(BF16)
    r = x - hi.astype(F32)
    mid = r.astype(BF16)
    lo = (r - mid.astype(F32)).astype(BF16)
    return hi, mid, lo


def _softplus(x):
    return jnp.maximum(x, 0.0) + jnp.log1p(jnp.exp(-jnp.abs(x)))


def _norm_dt_kernel(xp_ref, xs_ref, g_ref, wdt_ref, xnp_ref, xns_ref, dtp_ref, dts_ref):
    wdt = wdt_ref[...].astype(BF16)

    def one(x_ref, xn_ref, dt_ref):
        xn = _rms(x_ref[...], g_ref[...]).astype(BF16)
        xn_ref[...] = xn
        dt_ref[...] = _dot_nt(xn, wdt)

    one(xp_ref, xnp_ref, dtp_ref)

    @pl.when(pl.program_id(0) == 0)
    def _():
        one(xs_ref, xns_ref, dts_ref)


def _norm_dt(xp, xs, g, wdt):
    tp, d = xp.shape
    ts = xs.shape[0]
    nh = wdt.shape[0]
    tm = _pick(tp, (512, 256, 128))
    return pl.pallas_call(
        _norm_dt_kernel,
        grid=(tp // tm,),
        in_specs=[
            pl.BlockSpec((tm, d), lambda i: (i, 0)),
            pl.BlockSpec((ts, d), lambda i: (0, 0)),
            pl.BlockSpec((1, d), lambda i: (0, 0)),
            pl.BlockSpec((nh, d), lambda i: (0, 0)),
        ],
        out_specs=[
            pl.BlockSpec((tm, d), lambda i: (i, 0)),
            pl.BlockSpec((ts, d), lambda i: (0, 0)),
            pl.BlockSpec((tm, nh), lambda i: (i, 0)),
            pl.BlockSpec((ts, nh), lambda i: (0, 0)),
        ],
        out_shape=[
            jax.ShapeDtypeStruct((tp, d), BF16),
            jax.ShapeDtypeStruct((ts, d), BF16),
            jax.ShapeDtypeStruct((tp, nh), F32),
            jax.ShapeDtypeStruct((ts, nh), F32),
        ],
        compiler_params=_params(("arbitrary",)),
        name="norm_dt",
    )(xp, xs, g, wdt)


def _proj_a_kernel(xp_ref, xs_ref, wb_ref, wc_ref, wh_ref, cw_ref, st_ref,
                   op_ref, os_ref, cap_ref, cas_ref,
                   wbf, ubuf, sbuf, s1buf, s2buf, *, tiles_per_batch, nb_s, l_s):
    i = pl.program_id(1)
    tm = xp_ref.shape[0]
    ts = xs_ref.shape[0]
    cw = cw_ref[...]

    @pl.when(i == 0)
    def _():
        wbf[0] = wb_ref[...].astype(BF16)
        wbf[1] = wc_ref[...].astype(BF16)
        wbf[2] = wh_ref[...].astype(BF16)
        x = xs_ref[...]
        u = _dot_nt(x, wbf[1]) * _dot_nt(x, wbf[2])
        sbuf[pl.ds(0, SUBLANES), :] = jnp.zeros((SUBLANES, u.shape[1]), F32)
        sbuf[pl.ds(SUBLANES, ts), :] = u
        s1buf[...] = jnp.zeros_like(s1buf)
        s2buf[...] = jnp.zeros_like(s2buf)
        for b in range(nb_s):
            s1buf[pl.ds(b * l_s, 1), :] = st_ref[b, pl.ds(1, 1), :]
            s2buf[pl.ds(b * l_s, 1), :] = st_ref[b, pl.ds(0, 1), :]
            s2buf[pl.ds(b * l_s + 1, 1), :] = st_ref[b, pl.ds(1, 1), :]
        rmod = lax.broadcasted_iota(jnp.int32, (ts, 1), 0) % l_s
        prev1 = jnp.where(rmod == 0, s1buf[...], sbuf[pl.ds(SUBLANES - 1, ts), :])
        prev2 = jnp.where(rmod < 2, s2buf[...], sbuf[pl.ds(SUBLANES - 2, ts), :])
        v = prev2 * cw[0:1, :] + prev1 * cw[1:2, :] + u * cw[2:3, :]
        os_ref[...] = (_dot_nt(x, wbf[0]) * v).astype(BF16)
        for b in range(nb_s):
            cas_ref[b] = sbuf[pl.ds(SUBLANES + (b + 1) * l_s - 2, 2), :]

    @pl.when(i % tiles_per_batch == 0)
    def _():
        ubuf[pl.ds(0, SUBLANES), :] = jnp.zeros((SUBLANES, ubuf.shape[1]), F32)

    x = xp_ref[...]
    u = _dot_nt(x, wbf[1]) * _dot_nt(x, wbf[2])
    ubuf[pl.ds(SUBLANES, tm), :] = u
    v = (ubuf[pl.ds(SUBLANES - 2, tm), :] * cw[0:1, :]
         + ubuf[pl.ds(SUBLANES - 1, tm), :] * cw[1:2, :] + u * cw[2:3, :])
    op_ref[...] = (_dot_nt(x, wbf[0]) * v).astype(BF16)
    ubuf[pl.ds(0, SUBLANES), :] = ubuf[pl.ds(tm, SUBLANES), :]

    @pl.when(i % tiles_per_batch == tiles_per_batch - 1)
    def _():
        cap_ref[0] = ubuf[pl.ds(SUBLANES + tm - 2, 2), :]


def _proj_a(xnp, xns, w_in, conv_w, state_s, n_batch_p, seq_p, d_conv):
    tp, d = xnp.shape
    ts = xns.shape[0]
    nb_s = state_s.shape[0]
    l_s = ts // nb_s
    tn = 512
    tm = _pick(seq_p, (1024, 512, 256, 128))
    tpb = seq_p // tm
    ncol = d_conv // tn
    kern = functools.partial(_proj_a_kernel, tiles_per_batch=tpb, nb_s=nb_s, l_s=l_s)
    return pl.pallas_call(
        kern,
        grid=(ncol, tp // tm),
        in_specs=[
            pl.BlockSpec((tm, d), lambda j, i: (i, 0)),
            pl.BlockSpec((ts, d), lambda j, i: (0, 0)),
            pl.BlockSpec((tn, d), lambda j, i: (j, 0)),
            pl.BlockSpec((tn, d), lambda j, i: (j + ncol, 0)),
            pl.BlockSpec((tn, d), lambda j, i: (j + 2 * ncol, 0)),
            pl.BlockSpec((CONV_A_K, tn), lambda j, i: (0, j)),
            pl.BlockSpec((nb_s, CONV_A_K - 1, tn), lambda j, i: (0, 0, j)),
        ],
        out_specs=[
            pl.BlockSpec((tm, tn), lambda j, i: (i, j)),
            pl.BlockSpec((ts, tn), lambda j, i: (0, j)),
            pl.BlockSpec((1, CONV_A_K - 1, tn), lambda j, i: (i // tpb, 0, j)),
            pl.BlockSpec((nb_s, CONV_A_K - 1, tn), lambda j, i: (0, 0, j)),
        ],
        out_shape=[
            jax.ShapeDtypeStruct((tp, d_conv), BF16),
            jax.ShapeDtypeStruct((ts, d_conv), BF16),
            jax.ShapeDtypeStruct((n_batch_p, CONV_A_K - 1, d_conv), F32),
            jax.ShapeDtypeStruct((nb_s, CONV_A_K - 1, d_conv), F32),
        ],
        scratch_shapes=[
            pltpu.VMEM((3, tn, d), BF16),
            pltpu.VMEM((SUBLANES + tm, tn), F32),
            pltpu.VMEM((SUBLANES + ts, tn), F32),
            pltpu.VMEM((ts, tn), F32),
            pltpu.VMEM((ts, tn), F32),
        ],
        compiler_params=_params(("arbitrary", "arbitrary")),
        name="proj_a",
    )(xnp, xns, w_in, w_in, w_in, conv_w, state_s)


def _proj_raw_kernel(xp_ref, xs_ref, w_ref, op_ref, os_ref, wbf, *, nsplit, width):
    def emit(x_ref, o_ref):
        acc = _dot_nt(x_ref[...], wbf[...])
        if nsplit == 0:
            o_ref[...] = acc
        else:
            for s in range(nsplit):
                o_ref[s] = acc[:, s * width:(s + 1) * width]

    @pl.when(pl.program_id(1) == 0)
    def _():
        wbf[...] = w_ref[...].astype(BF16)
        emit(xs_ref, os_ref)

    emit(xp_ref, op_ref)


def _proj_raw(xnp, xns, w, col0, ncols, width):
    tp, d = xnp.shape
    ts = xns.shape[0]
    tn = 1024
    tm = _pick(tp, (1024, 512, 256, 128))
    assert col0 % tn == 0 and ncols % tn == 0
    jb = col0 // tn
    if width == 0:
        nsplit = 0
        out_specs = [pl.BlockSpec((tm, tn), lambda j, i: (i, j)),
                     pl.BlockSpec((ts, tn), lambda j, i: (0, j))]
        out_shape = [jax.ShapeDtypeStruct((tp, ncols), F32), jax.ShapeDtypeStruct((ts, ncols), F32)]
    else:
        nsplit = tn // width
        out_specs = [pl.BlockSpec((nsplit, tm, width), lambda j, i: (j, i, 0)),
                     pl.BlockSpec((nsplit, ts, width), lambda j, i: (j, 0, 0))]
        out_shape = [jax.ShapeDtypeStruct((ncols // width, tp, width), F32),
                     jax.ShapeDtypeStruct((ncols // width, ts, width), F32)]
    kern = functools.partial(_proj_raw_kernel, nsplit=nsplit, width=width)
    return pl.pallas_call(
        kern,
        grid=(ncols // tn, tp // tm),
        in_specs=[
            pl.BlockSpec((tm, d), lambda j, i: (i, 0)),
            pl.BlockSpec((ts, d), lambda j, i: (0, 0)),
            pl.BlockSpec((tn, d), lambda j, i: (j + jb, 0)),
        ],
        out_specs=out_specs,
        out_shape=out_shape,
        scratch_shapes=[pltpu.VMEM((tn, d), BF16)],
        compiler_params=_params(("arbitrary", "arbitrary")),
        name=f"proj_raw_{col0}",
    )(xnp, xns, w)


def _ssd_kernel(*refs, q, has_state, nchunks):
    (z_ref, xs_ref, b_ref, c_ref, dt_ref, cwx, cwb, cwc, cbx, cbb, cbc,
     dtb_ref, alog_ref, dsk_ref, gn_ref, *rest) = refs
    if has_state:
        cpx, cpb, cpc, hprev, *rest = rest
    (y_ref, ocx, ocb, occ, oh, h_s, ccx, ccb, ccc, xbuf, bbuf, cbuf, acg, rowt) = rest
    c = pl.program_id(1)
    nheads = N_GROUPS * HEADS_PER_GROUP
    tail = CONV_M_K - 1

    @pl.when(c == 0)
    def _init():
        ccx[...] = jnp.zeros_like(ccx)
        ccb[...] = jnp.zeros_like(ccb)
        ccc[...] = jnp.zeros_like(ccc)
        if has_state:
            for g in range(N_GROUPS):
                ccx[g, pl.ds(SUBLANES - tail, tail), :] = cpx[g]
                ccb[g, pl.ds(SUBLANES - tail, tail), :] = cpb[g]
                ccc[g, pl.ds(SUBLANES - tail, tail), :] = cpc[g]
            h_s[...] = hprev[...]
        else:
            h_s[...] = jnp.zeros_like(h_s)

    dt = _softplus(dt_ref[...] + dtb_ref[...])
    da = dt * (-jnp.exp(alog_ref[...]))
    ri = lax.broadcasted_iota(jnp.int32, (q, q), 0)
    ci = lax.broadcasted_iota(jnp.int32, (q, q), 1)
    causal = ri >= ci
    tril = jnp.where(causal, 1.0, 0.0).astype(BF16)
    acum = sum(_dot(tril, p) for p in _split3(da))
    eye = jnp.where(lax.broadcasted_iota(jnp.int32, (nheads, nheads), 0)
                    == lax.broadcasted_iota(jnp.int32, (nheads, nheads), 1), 1.0, 0.0).astype(BF16)
    rowt[0] = sum(lax.dot_general(eye, p, NT_DIMS, preferred_element_type=F32) for p in _split3(acum))
    rowt[1] = sum(lax.dot_general(eye, p, NT_DIMS, preferred_element_type=F32) for p in _split3(dt))
    wend = jnp.exp(acum[q - 1:q, :] - acum) * dt
    rowt[2] = sum(lax.dot_general(eye, p, NT_DIMS, preferred_element_type=F32) for p in _split3(wend))
    for g in range(N_GROUPS):
        acg[g] = acum[:, g * HEADS_PER_GROUP:(g + 1) * HEADS_PER_GROUP]

    lane = lax.broadcasted_iota(jnp.int32, (1, LANES), 1)
    rowi = lax.broadcasted_iota(jnp.int32, (LANES, 1), 0)
    half_w = LANES // 2

    def conv(raw_ref, cc, buf, cw, cb, ost, g):
        buf[pl.ds(0, SUBLANES), :] = cc[g]
        raw = raw_ref[g]
        buf[pl.ds(SUBLANES, q), :] = raw
        w = cw[g]
        acc = raw * w[tail:tail + 1, :]
        for k in range(tail):
            acc = acc + buf[pl.ds(SUBLANES - tail + k, q), :] * w[k:k + 1, :]
        cc[g] = buf[pl.ds(q, SUBLANES), :]

        @pl.when(c == nchunks - 1)
        def _():
            ost[g] = buf[pl.ds(SUBLANES + q - tail, tail), :]

        return jax.nn.silu(acc + cb[g])

    def group_body(g, carry):
        xs = conv(xs_ref, ccx, xbuf, cwx, cbx, ocx, g)
        bc = conv(b_ref, ccb, bbuf, cwb, cbb, ocb, g)
        cc_ = conv(c_ref, ccc, cbuf, cwc, cbc, occ, g)
        bb = bc.astype(BF16)
        ccb16 = cc_.astype(BF16)
        cb_ = lax.dot_general(ccb16, bb, NT_DIMS, preferred_element_type=F32)
        ac8 = acg[g]
        dsk = dsk_ref[g]
        lo_half = lane < half_w
        lo_rows = rowi < half_w
        ys = []
        for pair in range(HEADS_PER_GROUP // 2):
            sl = slice(pair * LANES, (pair + 1) * LANES)
            xp = xs[:, sl]
            hp = h_s[g, pl.ds(pair * LANES, LANES), :]
            ms_, acols, wrows = [], [], []
            for r in (2 * pair, 2 * pair + 1):
                head = g * HEADS_PER_GROUP + r
                acol = jnp.broadcast_to(ac8[:, r:r + 1], (q, LANES))
                arow = rowt[0, pl.ds(head, 1), :]
                drow = rowt[1, pl.ds(head, 1), :]
                decay = jnp.exp(jnp.where(causal, acol[:, :q] - arow, -jnp.inf))
                ms_.append((cb_ * decay * drow).astype(BF16))
                acols.append(acol)
                wrows.append(jnp.broadcast_to(rowt[2, pl.ds(head, 1), :], (half_w, q)))
            xlo = jnp.where(lo_half, xp, 0.0).astype(BF16)
            xhi = jnp.where(lo_half, 0.0, xp).astype(BF16)
            if q % LANES == 0:
                ydiag = _dot(jnp.concatenate(ms_, axis=1), jnp.concatenate([xlo, xhi], axis=0))
            else:
                ydiag = _dot(ms_[0], xlo) + _dot(ms_[1], xhi)
            ea = jnp.where(lo_half, jnp.exp(acols[0]), jnp.exp(acols[1]))
            yoff = ea * lax.dot_general(ccb16, hp.astype(BF16), NT_DIMS, preferred_element_type=F32)
            ys.append(dsk[:, sl] * xp + ydiag + yoff)
            xwt = (xp.T * jnp.concatenate(wrows, axis=0)).astype(BF16)
            dlast = [jnp.broadcast_to(jnp.exp(a[q - 1:q, :]), (LANES, LANES)) for a in acols]
            h_s[g, pl.ds(pair * LANES, LANES), :] = hp * jnp.where(lo_rows, dlast[0], dlast[1]) + _dot(xwt, bb)
        yg = jnp.concatenate(ys, axis=1)
        hh = yg * jax.nn.silu(z_ref[g])
        ms = jnp.mean(hh * hh, axis=-1, keepdims=True)
        y_ref[g] = (hh * lax.rsqrt(ms + EPS) * gn_ref[g]).astype(BF16)
        return carry

    lax.fori_loop(0, N_GROUPS, group_body, 0)

    @pl.when(c == nchunks - 1)
    def _():
        oh[...] = h_s[...]


def _ssd(zx, bc, dt_raw, prm, n_batch, seq, q, state=None):
    nchunks = seq // q
    nheads = N_GROUPS * HEADS_PER_GROUP
    has_state = state is not None
    g8 = N_GROUPS

    def tok(first):
        return lambda b, c: (first, b * nchunks + c, 0)

    def const3(b, c):
        return (0, 0, 0)

    in_specs = [
        pl.BlockSpec((g8, q, GROUP_W), tok(0)),
        pl.BlockSpec((g8, q, GROUP_W), tok(1)),
        pl.BlockSpec((g8, q, D_STATE), tok(0)),
        pl.BlockSpec((g8, q, D_STATE), tok(1)),
        pl.BlockSpec((q, nheads), lambda b, c: (b * nchunks + c, 0)),
        pl.BlockSpec((g8, CONV_M_K, GROUP_W), const3),
        pl.BlockSpec((g8, CONV_M_K, D_STATE), const3),
        pl.BlockSpec((g8, CONV_M_K, D_STATE), const3),
        pl.BlockSpec((g8, 1, GROUP_W), const3),
        pl.BlockSpec((g8, 1, D_STATE), const3),
        pl.BlockSpec((g8, 1, D_STATE), const3),
        pl.BlockSpec((1, nheads), lambda b, c: (0, 0)),
        pl.BlockSpec((1, nheads), lambda b, c: (0, 0)),
        pl.BlockSpec((g8, 1, GROUP_W), const3),
        pl.BlockSpec((g8, 1, GROUP_W), const3),
    ]
    args = [zx, zx, bc, bc, dt_raw, prm["cwx"], prm["cwb"], prm["cwc"], prm["cbx"], prm["cbb"], prm["cbc"],
            prm["dtb"], prm["alog"], prm["dsk"], prm["gn"]]
    tail = CONV_M_K - 1

    def per_batch(shape):
        return pl.BlockSpec((None,) + shape, lambda b, c: (b,) + (0,) * len(shape))

    if has_state:
        in_specs += [per_batch((g8, tail, GROUP_W)), per_batch((g8, tail, D_STATE)),
                     per_batch((g8, tail, D_STATE)), per_batch((g8, GROUP_W, D_STATE))]
        args += list(state)
    t = n_batch * seq
    out_specs = [
        pl.BlockSpec((g8, q, GROUP_W), lambda b, c: (0, b * nchunks + c, 0)),
        per_batch((g8, tail, GROUP_W)), per_batch((g8, tail, D_STATE)), per_batch((g8, tail, D_STATE)),
        per_batch((g8, GROUP_W, D_STATE)),
    ]
    out_shape = [
        jax.ShapeDtypeStruct((g8, t, GROUP_W), BF16),
        jax.ShapeDtypeStruct((n_batch, g8, tail, GROUP_W), F32),
        jax.ShapeDtypeStruct((n_batch, g8, tail, D_STATE), F32),
        jax.ShapeDtypeStruct((n_batch, g8, tail, D_STATE), F32),
        jax.ShapeDtypeStruct((n_batch, g8, GROUP_W, D_STATE), F32),
    ]
    scratch = [
        pltpu.VMEM((g8, GROUP_W, D_STATE), F32),
        pltpu.VMEM((g8, SUBLANES, GROUP_W), F32),
        pltpu.VMEM((g8, SUBLANES, D_STATE), F32),
        pltpu.VMEM((g8, SUBLANES, D_STATE), F32),
        pltpu.VMEM((SUBLANES + q, GROUP_W), F32),
        pltpu.VMEM((SUBLANES + q, D_STATE), F32),
        pltpu.VMEM((SUBLANES + q, D_STATE), F32),
        pltpu.VMEM((g8, q, HEADS_PER_GROUP), F32),
        pltpu.VMEM((3, nheads, q), F32),
    ]
    kern = functools.partial(_ssd_kernel, q=q, has_state=has_state, nchunks=nchunks)
    return pl.pallas_call(
        kern,
        grid=(n_batch, nchunks),
        in_specs=in_specs,
        out_specs=out_specs,
        out_shape=out_shape,
        scratch_shapes=scratch,
        compiler_params=_params(("arbitrary", "arbitrary")),
        name="ssd_state" if has_state else "ssd",
    )(*args)


def _merge_kernel(ap_ref, as_ref, yp_ref, ys_ref, gap_ref, gmp_ref, gas_ref, gms_ref, wa_ref, wm_ref,
                  op_ref, os_ref, wab, wmb):
    def emit(a_ref, y_ref, ga_ref, gm_ref, o_ref):
        oa = _dot(a_ref[...], wab[...])
        om = _dot(y_ref[0], wmb[0])
        for g in range(1, N_GROUPS):
            om = om + _dot(y_ref[g], wmb[g])
        o_ref[...] = (jax.nn.sigmoid(ga_ref[...]) * oa + jax.nn.sigmoid(gm_ref[...]) * om).astype(BF16)

    @pl.when(pl.program_id(1) == 0)
    def _():
        wab[...] = wa_ref[...].astype(BF16)
        wmb[...] = wm_ref[...].astype(BF16)
        emit(as_ref, ys_ref, gas_ref, gms_ref, os_ref)

    emit(ap_ref, yp_ref, gap_ref, gmp_ref, op_ref)


def _merge(abv_p, abv_s, y_p, y_s, g_p, g_s, w_a_out, w_m_out3):
    tp, dc = abv_p.shape
    ts = abv_s.shape[0]
    dm = w_a_out.shape[1]
    tn = 256
    tm = _pick(tp, (1024, 512, 256, 128))
    ncol = dm // tn
    return pl.pallas_call(
        _merge_kernel,
        grid=(ncol, tp // tm),
        in_specs=[
            pl.BlockSpec((tm, dc), lambda j, i: (i, 0)),
            pl.BlockSpec((ts, dc), lambda j, i: (0, 0)),
            pl.BlockSpec((N_GROUPS, tm, GROUP_W), lambda j, i: (0, i, 0)),
            pl.BlockSpec((N_GROUPS, ts, GROUP_W), lambda j, i: (0, 0, 0)),
            pl.BlockSpec((tm, tn), lambda j, i: (i, j)),
            pl.BlockSpec((tm, tn), lambda j, i: (i, j + ncol)),
            pl.BlockSpec((ts, tn), lambda j, i: (0, j)),
            pl.BlockSpec((ts, tn), lambda j, i: (0, j + ncol)),
            pl.BlockSpec((dc, tn), lambda j, i: (0, j)),
            pl.BlockSpec((N_GROUPS, GROUP_W, tn), lambda j, i: (0, 0, j)),
        ],
        out_specs=[
            pl.BlockSpec((tm, tn), lambda j, i: (i, j)),
            pl.BlockSpec((ts, tn), lambda j, i: (0, j)),
        ],
        out_shape=[jax.ShapeDtypeStruct((tp, dm), BF16), jax.ShapeDtypeStruct((ts, dm), BF16)],
        scratch_shapes=[pltpu.VMEM((dc, tn), BF16), pltpu.VMEM((N_GROUPS, GROUP_W, tn), BF16)],
        compiler_params=_params(("arbitrary", "arbitrary")),
        name="merge",
    )(abv_p, abv_s, y_p, y_s, g_p, g_p, g_s, g_s, w_a_out, w_m_out3)


def _res_kernel(ap_ref, as_ref, rp_ref, rs_ref, w_ref, g_ref, *outs, n_prompt_tiles, merged_out, ts):
    if merged_out:
        h_ref, hn_ref, wbf = outs
    else:
        hp_ref, hs_ref, hnp_ref, hns_ref, wbf = outs
    i = pl.program_id(0)

    @pl.when(i == 0)
    def _():
        wbf[...] = w_ref[...].astype(BF16)

    def emit(a_ref, r_ref, store_h, store_hn):
        h = r_ref[...] + _dot(a_ref[...], wbf[...])
        store_h(h)
        store_hn(_rms(h, g_ref[...]))

    if merged_out:
        @pl.when(i < n_prompt_tiles)
        def _():
            def sh(h):
                h_ref[...] = h

            def shn(hn):
                hn_ref[...] = hn

            emit(ap_ref, rp_ref, sh, shn)

        @pl.when(i == n_prompt_tiles)
        def _():
            def sh(h):
                h_ref[pl.ds(0, ts), :] = h

            def shn(hn):
                hn_ref[pl.ds(0, ts), :] = hn

            emit(as_ref, rs_ref, sh, shn)
    else:
        def shp(h):
            hp_ref[...] = h

        def shnp(hn):
            hnp_ref[...] = hn.astype(hnp_ref.dtype)

        emit(ap_ref, rp_ref, shp, shnp)

        @pl.when(i == 0)
        def _():
            def shs(h):
                hs_ref[...] = h

            def shns(hn):
                hns_ref[...] = hn.astype(hns_ref.dtype)

            emit(as_ref, rs_ref, shs, shns)


def _res(a_p, a_s, r_p, r_s, w, g, merged_out, hn_dtype):
    tp, d = a_p.shape
    ts = a_s.shape[0]
    tm = 256
    assert tp % tm == 0 and ts <= tm
    npt = tp // tm
    last = npt - 1
    in_specs = [
        pl.BlockSpec((tm, d), lambda i: (jnp.minimum(i, last), 0)),
        pl.BlockSpec((ts, d), lambda i: (0, 0)),
        pl.BlockSpec((tm, d), lambda i: (jnp.minimum(i, last), 0)),
        pl.BlockSpec((ts, d), lambda i: (0, 0)),
        pl.BlockSpec((d, d), lambda i: (0, 0), pipeline_mode=pl.Buffered(1)),
        pl.BlockSpec((1, d), lambda i: (0, 0)),
    ]
    if merged_out:
        grid = (npt + 1,)
        out_specs = [pl.BlockSpec((tm, d), lambda i: (i, 0)), pl.BlockSpec((tm, d), lambda i: (i, 0))]
        out_shape = [jax.ShapeDtypeStruct((tp + ts, d), F32), jax.ShapeDtypeStruct((tp + ts, d), hn_dtype)]
    else:
        grid = (npt,)
        out_specs = [pl.BlockSpec((tm, d), lambda i: (i, 0)), pl.BlockSpec((ts, d), lambda i: (0, 0)),
                     pl.BlockSpec((tm, d), lambda i: (i, 0)), pl.BlockSpec((ts, d), lambda i: (0, 0))]
        out_shape = [jax.ShapeDtypeStruct((tp, d), F32), jax.ShapeDtypeStruct((ts, d), F32),
                     jax.ShapeDtypeStruct((tp, d), hn_dtype), jax.ShapeDtypeStruct((ts, d), hn_dtype)]
    kern = functools.partial(_res_kernel, n_prompt_tiles=npt, merged_out=merged_out, ts=ts)
    return pl.pallas_call(
        kern,
        grid=grid,
        in_specs=in_specs,
        out_specs=out_specs,
        out_shape=out_shape,
        scratch_shapes=[pltpu.VMEM((d, d), BF16)],
        compiler_params=_params(("arbitrary",)),
        name="res_merged" if merged_out else "res",
    )(a_p, a_s, r_p, r_s, w, g)


def _kv_kernel(m_ref, g_ref, wk_ref, wv_ref, k_ref, v_ref, mn):
    @pl.when(pl.program_id(0) == 0)
    def _():
        mn[...] = _rms(m_ref[...], g_ref[...]).astype(BF16)

    k_ref[...] = _dot(mn[...], wk_ref[...].astype(BF16))
    v_ref[...] = _dot(mn[...], wv_ref[...].astype(BF16))


def _memory_kv(mem2d, g, w_k, w_v):
    m, d = mem2d.shape
    tn = 512
    return pl.pallas_call(
        _kv_kernel,
        grid=(d // tn,),
        in_specs=[
            pl.BlockSpec((m, d), lambda j: (0, 0)),
            pl.BlockSpec((1, d), lambda j: (0, 0)),
            pl.BlockSpec((d, tn), lambda j: (0, j)),
            pl.BlockSpec((d, tn), lambda j: (0, j)),
        ],
        out_specs=[pl.BlockSpec((m, tn), lambda j: (0, j)), pl.BlockSpec((m, tn), lambda j: (0, j))],
        out_shape=[jax.ShapeDtypeStruct((m, d), F32), jax.ShapeDtypeStruct((m, d), F32)],
        scratch_shapes=[pltpu.VMEM((m, d), BF16)],
        compiler_params=_params(("arbitrary",)),
        name="memory_kv",
    )(mem2d, g, w_k, w_v)


def _attn_kernel(hn_ref, k_ref, v_ref, wq_ref, o_ref, wqb, kb, vb):
    b = pl.program_id(0)
    i = pl.program_id(1)

    @pl.when((b == 0) & (i == 0))
    def _():
        wqb[...] = wq_ref[...].astype(BF16)

    @pl.when(i == 0)
    def _():
        kb[...] = k_ref[...].astype(BF16)
        vb[...] = v_ref[...].astype(BF16)

    d = wqb.shape[1]
    dh = d // N_XHEADS
    q = _dot(hn_ref[...], wqb[...])
    outs = []
    for h in range(N_XHEADS):
        sl = slice(h * dh, (h + 1) * dh)
        s = lax.dot_general(q[:, sl].astype(BF16), kb[:, sl], NT_DIMS, preferred_element_type=F32)
        s = s * (dh ** -0.5)
        e = jnp.exp(s - jnp.max(s, axis=-1, keepdims=True))
        p = e / jnp.sum(e, axis=-1, keepdims=True)
        outs.append(_dot(p.astype(BF16), vb[:, sl]))
    o_ref[...] = jnp.concatenate(outs, axis=1).astype(BF16)


def _attn(hn, k3, v3, w_q, n_batch, seq):
    t, d = hn.shape
    nm = k3.shape[1]
    tm = _pick(seq, (256, 128, 64, 32, 16))
    tpb = seq // tm
    return pl.pallas_call(
        _attn_kernel,
        grid=(n_batch, tpb),
        in_specs=[
            pl.BlockSpec((tm, d), lambda b, i: (b * tpb + i, 0)),
            pl.BlockSpec((None, nm, d), lambda b, i: (b, 0, 0)),
            pl.BlockSpec((None, nm, d), lambda b, i: (b, 0, 0)),
            pl.BlockSpec((d, d), lambda b, i: (0, 0), pipeline_mode=pl.Buffered(1)),
        ],
        out_specs=pl.BlockSpec((tm, d), lambda b, i: (b * tpb + i, 0)),
        out_shape=jax.ShapeDtypeStruct((t, d), BF16),
        scratch_shapes=[pltpu.VMEM((d, d), BF16), pltpu.VMEM((nm, d), BF16), pltpu.VMEM((nm, d), BF16)],
        compiler_params=_params(("arbitrary", "arbitrary")),
        name=f"attn_{seq}",
    )(hn, k3, v3, w_q)


def _router_kernel(x_ref, w_ref, b_ref, ri_ref, rw_ref):
    logits = _dot(x_ref[...].astype(BF16), w_ref[...].astype(BF16)) + b_ref[...]
    lane_i = lax.broadcasted_iota(jnp.int32, logits.shape, 1)
    lane = lane_i.astype(F32)
    ninf = -jnp.inf
    big = float(LANES)
    is_g = lane < N_EXPERT_GROUPS
    gl = jnp.where(is_g, logits, ninf)
    gmax = jnp.max(gl, axis=-1, keepdims=True)
    gsel = jnp.min(jnp.where(gl == gmax, lane, big), axis=-1, keepdims=True)
    pg = 1.0 / jnp.sum(jnp.where(is_g, jnp.exp(gl - gmax), 0.0), axis=-1, keepdims=True)
    lo = N_EXPERT_GROUPS + EXPERTS_PER_GROUP * gsel
    el = jnp.where(lane >= lo, jnp.where(lane < lo + EXPERTS_PER_GROUP, logits, ninf), ninf)
    m1 = jnp.max(el, axis=-1, keepdims=True)
    i1 = jnp.min(jnp.where(el == m1, lane, big), axis=-1, keepdims=True)
    el2 = jnp.where(lane == i1, ninf, el)
    m2 = jnp.max(el2, axis=-1, keepdims=True)
    i2 = jnp.min(jnp.where(el2 == m2, lane, big), axis=-1, keepdims=True)
    e = jnp.exp(m2 - m1)
    w1 = pg / (1.0 + e)
    w2 = pg * e / (1.0 + e)
    ri_ref[...] = jnp.where(lane_i == 0, i1 - N_EXPERT_GROUPS,
                            jnp.where(lane_i == 1, i2 - N_EXPERT_GROUPS, 0.0)).astype(jnp.int32)
    rw_ref[...] = jnp.where(lane_i == 0, w1, jnp.where(lane_i == 1, w2, 0.0))


def _router(hn_all, w_r, b_r):
    t, d = hn_all.shape
    tm = _pick(t, (640, 512, 384, 256, 128))
    return pl.pallas_call(
        _router_kernel,
        grid=(t // tm,),
        in_specs=[
            pl.BlockSpec((tm, d), lambda i: (i, 0)),
            pl.BlockSpec((d, LANES), lambda i: (0, 0)),
            pl.BlockSpec((1, LANES), lambda i: (0, 0)),
        ],
        out_specs=[pl.BlockSpec((tm, LANES), lambda i: (i, 0)), pl.BlockSpec((tm, LANES), lambda i: (i, 0))],
        out_shape=[jax.ShapeDtypeStruct((t, LANES), jnp.int32), jax.ShapeDtypeStruct((t, LANES), F32)],
        compiler_params=_params(("arbitrary",)),
        name="router",
    )(hn_all, w_r, b_r)


def _row_copy(src, dst, s, d, sem):
    return pltpu.make_async_copy(src.at[pl.ds(s, 1)], dst.at[pl.ds(d, 1)], sem)


def _dispatch_kernel(dest_ref, pad0_ref, padn_ref, nsub_ref, x_ref, o_ref, zrow, zblk, sem, zsem, bsem, *,
                     chunk, n_blocks):
    base = pl.program_id(0) * chunk

    @pl.when(pl.program_id(0) == 0)
    def _():
        zrow[...] = jnp.zeros_like(zrow)
        zblk[...] = jnp.zeros_like(zblk)

        def pad_start(e, carry):
            def start(r, c):
                _row_copy(zrow, o_ref, 0, pad0_ref[e] + r, zsem).start()
                return c

            return lax.fori_loop(0, padn_ref[e], start, carry)

        def pad_wait(e, carry):
            def wait(r, c):
                _row_copy(zrow, o_ref, 0, 0, zsem).wait()
                return c

            return lax.fori_loop(0, padn_ref[e], wait, carry)

        lax.fori_loop(0, N_EXPERTS, pad_start, 0)

        def blk_copy(b):
            return pltpu.make_async_copy(zblk, o_ref.at[pl.ds(pl.multiple_of(b * EXPERT_SUB, EXPERT_SUB),
                                                             EXPERT_SUB)], bsem)

        def tail_start(b, c):
            blk_copy(b).start()
            return c

        def tail_wait(b, c):
            blk_copy(b).wait()
            return c

        lax.fori_loop(nsub_ref[0], n_blocks, tail_start, 0)
        lax.fori_loop(0, N_EXPERTS, pad_wait, 0)
        lax.fori_loop(nsub_ref[0], n_blocks, tail_wait, 0)

    def issue(r, carry):
        t = base + r
        _row_copy(x_ref, o_ref, r, dest_ref[2 * t], sem).start()
        _row_copy(x_ref, o_ref, r, dest_ref[2 * t + 1], sem).start()
        return carry

    lax.fori_loop(0, chunk, issue, 0, unroll=8)

    for _ in range(2):
        pltpu.make_async_copy(x_ref, o_ref.at[pl.ds(0, chunk)], sem).wait()


def _dispatch(tables, hn_all, n_blocks):
    dest, pad0, padn, nsub = tables
    t, d = hn_all.shape
    chunk = _pick(t, (640, 512, 384, 256, 128))
    kern = functools.partial(_dispatch_kernel, chunk=chunk, n_blocks=n_blocks)
    return pl.pallas_call(
        kern,
        grid_spec=pltpu.PrefetchScalarGridSpec(
            num_scalar_prefetch=4,
            grid=(t // chunk,),
            in_specs=[pl.BlockSpec((chunk, d), lambda i, *_: (i, 0))],
            out_specs=pl.BlockSpec(memory_space=pl.ANY),
            scratch_shapes=[pltpu.VMEM((SUBLANES, d), F32), pltpu.VMEM((EXPERT_SUB, d), F32),
                            pltpu.SemaphoreType.DMA(()), pltpu.SemaphoreType.DMA(()),
                            pltpu.SemaphoreType.DMA(())],
        ),
        out_shape=jax.ShapeDtypeStruct((n_blocks * EXPERT_SUB, d), F32),
        compiler_params=_params(("arbitrary",)),
        name="dispatch",
    )(dest, pad0, padn, nsub, hn_all)


def _expert_kernel(bstart_ref, wg_ref, wu_ref, wd_ref, x_hbm, y_hbm, gcache, ucache, dcache, xbuf, ybuf, xb,
                   xsem, ysem, *, n_blocks):
    e = pl.program_id(0)
    k = pl.program_id(1)
    nk = gcache.shape[1]
    total = bstart_ref[N_EXPERTS]

    def x_copy(b, slot):
        rows = pl.ds(pl.multiple_of(b * EXPERT_SUB, EXPERT_SUB), EXPERT_SUB)
        return pltpu.make_async_copy(x_hbm.at[rows], xbuf.at[slot], xsem.at[slot])

    def y_copy(b, slot):
        rows = pl.ds(pl.multiple_of(b * EXPERT_SUB, EXPERT_SUB), EXPERT_SUB)
        return pltpu.make_async_copy(ybuf.at[slot], y_hbm.at[rows], ysem.at[slot])

    @pl.when(e < N_EXPERTS)
    def _():
        slot = e % 2
        gcache[slot, k] = wg_ref[...].astype(BF16)
        ucache[slot, k] = wu_ref[...].astype(BF16)
        dcache[slot, k] = wd_ref[...].astype(BF16)

    @pl.when((e == 0) & (k == 0) & (total > 0))
    def _():
        x_copy(0, 0).start()

    @pl.when(e >= 1)
    def _():
        owner = e - 1
        wslot = owner % 2
        b0 = bstart_ref[owner]
        n = bstart_ref[owner + 1] - b0

        def unit(u, carry):
            b = b0 + u // nk
            kk = u % nk
            slot = b % 2

            @pl.when(kk == 0)
            def _():
                x_copy(b, slot).wait()

                @pl.when(b + 1 < total)
                def _():
                    x_copy(b + 1, 1 - slot).start()

                xb[...] = xbuf[slot].astype(BF16)

                @pl.when(b >= 2)
                def _():
                    y_copy(b - 2, slot).wait()

                ybuf[slot] = jnp.zeros(ybuf.shape[1:], F32)

            x = xb[...]
            hid = (jax.nn.silu(_dot(x, gcache[wslot, kk])) * _dot(x, ucache[wslot, kk])).astype(BF16)
            ybuf[slot] += _dot(hid, dcache[wslot, kk])

            @pl.when(kk == nk - 1)
            def _():
                y_copy(b, slot).start()

            return carry

        lax.fori_loop(n * k, n * (k + 1), unit, 0)

    @pl.when((e == N_EXPERTS) & (k == nk - 1))
    def _():
        @pl.when(total >= 2)
        def _():
            y_copy(total - 2, total % 2).wait()

        @pl.when(total >= 1)
        def _():
            y_copy(total - 1, (total - 1) % 2).wait()

        ybuf[0] = jnp.zeros(ybuf.shape[1:], F32)

        def zfill(b, carry):
            y_copy(b, 0).start()
            y_copy(b, 0).wait()
            return carry

        lax.fori_loop(total, n_blocks, zfill, 0)


def _experts(bstart, x_sorted, w_gate, w_up, w_down, kchunk):
    nrows, d = x_sorted.shape
    n_blocks = nrows // EXPERT_SUB
    de = w_gate.shape[2]
    nk = de // kchunk
    last = N_EXPERTS - 1

    def widx(e, k):
        return jnp.minimum(e, last), jnp.where(e <= last, k, nk - 1)

    def in_idx(e, k, b):
        ee, kk = widx(e, k)
        return (ee, 0, kk)

    def down_idx(e, k, b):
        ee, kk = widx(e, k)
        return (ee, kk, 0)

    kern = functools.partial(_expert_kernel, n_blocks=n_blocks)
    return pl.pallas_call(
        kern,
        grid_spec=pltpu.PrefetchScalarGridSpec(
            num_scalar_prefetch=1,
            grid=(N_EXPERTS + 1, nk),
            in_specs=[
                pl.BlockSpec((None, d, kchunk), in_idx),
                pl.BlockSpec((None, d, kchunk), in_idx),
                pl.BlockSpec((None, kchunk, d), down_idx),
                pl.BlockSpec(memory_space=pl.ANY),
            ],
            out_specs=pl.BlockSpec(memory_space=pl.ANY),
            scratch_shapes=[
                pltpu.VMEM((2, nk, d, kchunk), BF16),
                pltpu.VMEM((2, nk, d, kchunk), BF16),
                pltpu.VMEM((2, nk, kchunk, d), BF16),
                pltpu.VMEM((2, EXPERT_SUB, d), F32),
                pltpu.VMEM((2, EXPERT_SUB, d), F32),
                pltpu.VMEM((EXPERT_SUB, d), BF16),
                pltpu.SemaphoreType.DMA((2,)),
                pltpu.SemaphoreType.DMA((2,)),
            ],
        ),
        out_shape=jax.ShapeDtypeStruct((nrows, d), F32),
        compiler_params=_params(("arbitrary", "arbitrary")),
        name="experts",
    )(bstart, w_gate, w_up, w_down, x_sorted)


def _final_kernel(dest_ref, h_ref, rw_ref, ys_ref, g_ref, yp_ref, yss_ref, ya, yb, sem, *,
                  tm, n_prompt_tiles, ts):
    i = pl.program_id(0)

    def gather(tile, n):
        slot = tile % 2

        def issue(r, carry):
            t = tile * tm + r
            _row_copy(ys_ref, ya.at[slot], dest_ref[2 * t], r, sem.at[slot]).start()
            _row_copy(ys_ref, yb.at[slot], dest_ref[2 * t + 1], r, sem.at[slot]).start()
            return carry

        lax.fori_loop(0, n, issue, 0, unroll=8)

    def emit(n, o_ref):
        slot = i % 2
        for buf in (ya, yb):
            pltpu.make_async_copy(ys_ref.at[pl.ds(0, n)], buf.at[slot, pl.ds(0, n)], sem.at[slot]).wait()
        w = rw_ref[pl.ds(0, n), :]
        h = (h_ref[pl.ds(0, n), :] + w[:, 0:1] * ya[slot, pl.ds(0, n), :]
             + w[:, 1:2] * yb[slot, pl.ds(0, n), :])
        o_ref[...] = _rms(h, g_ref[...])

    @pl.when(i == 0)
    def _():
        gather(i, tm)

    @pl.when(i + 1 < n_prompt_tiles)
    def _():
        gather(i + 1, tm)

    @pl.when(i + 1 == n_prompt_tiles)
    def _():
        gather(i + 1, ts)

    @pl.when(i < n_prompt_tiles)
    def _():
        emit(tm, yp_ref)

    @pl.when(i == n_prompt_tiles)
    def _():
        emit(ts, yss_ref)


def _final(dest, h_all, rw, y_sorted, g, tp, ts):
    t, d = h_all.shape
    tm = 256
    assert tp % tm == 0 and ts <= tm and t == tp + ts
    npt = tp // tm
    kern = functools.partial(_final_kernel, tm=tm, n_prompt_tiles=npt, ts=ts)
    return pl.pallas_call(
        kern,
        grid_spec=pltpu.PrefetchScalarGridSpec(
            num_scalar_prefetch=1,
            grid=(npt + 1,),
            in_specs=[
                pl.BlockSpec((tm, d), lambda i, dref: (i, 0)),
                pl.BlockSpec((tm, LANES), lambda i, dref: (i, 0)),
                pl.BlockSpec(memory_space=pl.ANY),
                pl.BlockSpec((1, d), lambda i, dref: (0, 0)),
            ],
            out_specs=[
                pl.BlockSpec((tm, d), lambda i, dref: (jnp.minimum(i, npt - 1), 0)),
                pl.BlockSpec((ts, d), lambda i, dref: (0, 0)),
            ],
            scratch_shapes=[pltpu.VMEM((2, tm, d), F32), pltpu.VMEM((2, tm, d), F32),
                            pltpu.SemaphoreType.DMA((2,))],
        ),
        out_shape=[jax.ShapeDtypeStruct((tp, d), F32), jax.ShapeDtypeStruct((ts, d), F32)],
        compiler_params=_params(("arbitrary",)),
        name="final",
    )(dest, h_all, rw, y_sorted, g)


def _routing_tables(eid):
    i32 = jnp.int32
    onehot = (eid[:, None] == jnp.arange(N_EXPERTS, dtype=i32)[None, :]).astype(i32)
    csum = jnp.cumsum(onehot, axis=0)
    counts = csum[-1]
    nsub = (counts + EXPERT_SUB - 1) // EXPERT_SUB
    bend = jnp.cumsum(nsub)
    bstart = bend - nsub
    seg = bstart * EXPERT_SUB
    dest = jnp.sum(onehot * (csum - 1 + seg[None, :]), axis=1).astype(i32)
    pad0 = (seg + counts).astype(i32)
    padn = (nsub * EXPERT_SUB - counts).astype(i32)
    bstart_all = jnp.concatenate([bstart, bend[-1:]]).astype(i32)
    return (dest, pad0, padn, bend[-1:].astype(i32)), bstart_all


def _group_major(a, width):
    lead = a.shape[:-1]
    g = a.shape[-1] // width
    return jnp.moveaxis(a.reshape(lead + (g, width)), -2, 0)


def kernel(x_prompt, x_sample, cache_mem_k, cache_mem_v, state_conv_a, state_conv_m, state_ssm, mem_prompt,
           norm_mix, w_in, conv_a_w, w_a_out, conv_m_w, conv_m_b, dt_bias, a_log, d_skip, ssm_norm,
           w_m_out, w_o, norm_cross, norm_mem, w_q, w_k, w_v, w_co, norm_ffn, w_rg, b_rg, w_re, b_re,
           w_gate, w_up, w_down, norm_final):
    depth = w_in.shape[0]
    assert depth == 1, "single-layer step"
    nbp, seq_p, d = x_prompt.shape
    nbs, seq_s, _ = x_sample.shape
    tp, ts = nbp * seq_p, nbs * seq_s
    n_mem = mem_prompt.shape[1]
    d_conv = conv_a_w.shape[2]
    d_inner = w_m_out.shape[1]
    nheads = dt_bias.shape[1]
    bc_w = N_GROUPS * D_STATE
    l = 0

    xp = x_prompt.reshape(tp, d)
    xs = x_sample.reshape(ts, d)
    w_in_t = jnp.swapaxes(w_in[l], 0, 1)
    col_z = 3 * d_conv
    col_x = col_z + d_inner
    col_b = col_x + d_inner
    col_dt = col_b + 2 * bc_w
    col_g = col_dt + nheads
    w_dt = w_in_t[col_dt:col_g]
    w_gates = w_in_t[col_g:]

    mk2, mv2 = _memory_kv(mem_prompt.reshape(nbp * n_mem, d), norm_mem[l][None], w_k[l], w_v[l])

    xnp, xns, dt_p, dt_s = _norm_dt(xp, xs, norm_mix[l][None], w_dt)
    abv_p, abv_s, ca_p, ca_s = _proj_a(xnp, xns, w_in_t, conv_a_w[l], state_conv_a[l], nbp, seq_p, d_conv)
    zx_p, zx_s = _proj_raw(xnp, xns, w_in_t, col_z, 2 * d_inner, GROUP_W)
    bc_p, bc_s = _proj_raw(xnp, xns, w_in_t, col_b, 2 * bc_w, D_STATE)
    g_p, g_s = _proj_raw(xnp, xns, w_gates, 0, 2 * d, 0)

    cw = conv_m_w[l]
    cbias = conv_m_b[l][None]
    prm = {
        "cwx": _group_major(cw[:, :d_inner], GROUP_W),
        "cwb": _group_major(cw[:, d_inner:d_inner + bc_w], D_STATE),
        "cwc": _group_major(cw[:, d_inner + bc_w:], D_STATE),
        "cbx": _group_major(cbias[:, :d_inner], GROUP_W),
        "cbb": _group_major(cbias[:, d_inner:d_inner + bc_w], D_STATE),
        "cbc": _group_major(cbias[:, d_inner + bc_w:], D_STATE),
        "dtb": dt_bias[l][None],
        "alog": a_log[l][None],
        "dsk": _group_major(jnp.repeat(d_skip[l], HEAD_DIM)[None], GROUP_W),
        "gn": _group_major(ssm_norm[l][None], GROUP_W),
    }
    scm = state_conv_m[l]
    state_s = (
        jnp.moveaxis(_group_major(scm[..., :d_inner], GROUP_W), 0, 1),
        jnp.moveaxis(_group_major(scm[..., d_inner:d_inner + bc_w], D_STATE), 0, 1),
        jnp.moveaxis(_group_major(scm[..., d_inner + bc_w:], D_STATE), 0, 1),
        state_ssm[l].reshape(nbs, N_GROUPS, GROUP_W, D_STATE),
    )
    q_p = _pick(seq_p, (128, 64, 32, 16, 8))
    y_p, cx_p, cb_p, cc_p, h_p = _ssd(zx_p, bc_p, dt_p, prm, nbp, seq_p, q_p)
    y_s, cx_s, cb_s, cc_s, h_s = _ssd(zx_s, bc_s, dt_s, prm, nbs, seq_s, seq_s, state=state_s)

    def conv_m_state(cx, cb_, cc_):
        def flat(a):
            return jnp.moveaxis(a, 1, 2).reshape(a.shape[0], a.shape[2], -1)
        return jnp.concatenate([flat(cx), flat(cb_), flat(cc_)], axis=-1)

    merged_p, merged_s = _merge(abv_p, abv_s, y_p, y_s, g_p, g_s, w_a_out[l],
                                w_m_out[l].reshape(N_GROUPS, GROUP_W, d))
    h1_p, h1_s, hn1_p, hn1_s = _res(merged_p, merged_s, xp, xs, w_o[l], norm_cross[l][None], False, BF16)

    att_p = _attn(hn1_p, mk2.reshape(nbp, n_mem, d), mv2.reshape(nbp, n_mem, d), w_q[l], nbp, seq_p)
    att_s = _attn(hn1_s, cache_mem_k[l].reshape(nbs, n_mem, d), cache_mem_v[l].reshape(nbs, n_mem, d),
                  w_q[l], nbs, seq_s)
    h2_all, hn2_all = _res(att_p, att_s, h1_p, h1_s, w_co[l], norm_ffn[l][None], True, F32)

    npad = LANES - N_EXPERT_GROUPS - N_EXPERTS
    w_r = jnp.concatenate([w_rg[l], w_re[l], jnp.zeros((d, npad), F32)], axis=1)
    b_r = jnp.concatenate([b_rg[l], b_re[l], jnp.zeros((npad,), F32)])[None]
    ri, rw = _router(hn2_all, w_r, b_r)
    tables, bstart = _routing_tables(ri[:, :2].reshape(-1))
    n_blocks = -(-2 * (tp + ts) // EXPERT_SUB) + N_EXPERTS
    x_sorted = _dispatch(tables, hn2_all, n_blocks)
    y_sorted = _experts(bstart, x_sorted, w_gate[l], w_up[l], w_down[l], 256)
    y_prompt, y_sample = _final(tables[0], h2_all, rw, y_sorted, norm_final[None], tp, ts)

    xh = d // N_XHEADS
    return (
        y_prompt.reshape(nbp, seq_p, d),
        y_sample.reshape(nbs, seq_s, d),
        mk2.reshape(1, nbp, n_mem, N_XHEADS, xh),
        mv2.reshape(1, nbp, n_mem, N_XHEADS, xh),
        ca_p[None],
        conv_m_state(cx_p, cb_p, cc_p)[None],
        h_p.reshape(1, nbp, nheads, HEAD_DIM, D_STATE),
        ca_s[None],
        conv_m_state(cx_s, cb_s, cc_s)[None],
        h_s.reshape(1, nbs, nheads, HEAD_DIM, D_STATE),
    )
```

```python
import functools

import jax
import jax.numpy as jnp
from jax import lax
from jax.experimental import pallas as pl
from jax.experimental.pallas import tpu as pltpu

F32 = jnp.float32
BF16 = jnp.bfloat16
EPS = 1e-6

V7X_VMEM_BYTES = 64 * 1024 * 1024
VMEM_LIMIT = V7X_VMEM_BYTES - 8 * 1024 * 1024
LANES = 128
SUBLANES = 8

N_GROUPS = 8
HEADS_PER_GROUP = 8
HEAD_DIM = 64
D_STATE = 128
GROUP_W = HEADS_PER_GROUP * HEAD_DIM
N_XHEADS = 4
N_EXPERTS = 32
N_EXPERT_GROUPS = 4
EXPERTS_PER_GROUP = 8
EXPERT_SUB = 256
CONV_A_K = 3
CONV_M_K = 4

NT_DIMS = (((1,), (1,)), ((), ()))
TN_DIMS = (((0,), (0,)), ((), ()))


def _params(sem):
    return pltpu.CompilerParams(dimension_semantics=sem, vmem_limit_bytes=VMEM_LIMIT)


def _pick(n, cands):
    for c in cands:
        if n % c == 0:
            return c
    raise ValueError(f"no tile for {n} in {cands}")


def _dot(a, b):
    return jnp.dot(a, b, preferred_element_type=F32)


def _dot_nt(a, b):
    return lax.dot_general(a, b, NT_DIMS, preferred_element_type=F32)


def _rms(x, g):
    return x * lax.rsqrt(jnp.mean(x * x, axis=-1, keepdims=True) + EPS) * g


def _split3(x):
    hi = x.astype(BF16)
    r = x - hi.astype(F32)
    mid = r.astype(BF16)
    lo = (r - mid.astype(F32)).astype(BF16)
    return hi, mid, lo


def _softplus(x):
    return jnp.maximum(x, 0.0) + jnp.log1p(jnp.exp(-jnp.abs(x)))


def _norm_dt_kernel(xp_ref, xs_ref, g_ref, wdt_ref, xnp_ref, xns_ref, dtp_ref, dts_ref):
    wdt = wdt_ref[...].astype(BF16)

    def one(x_ref, xn_ref, dt_ref):
        xn = _rms(x_ref[...], g_ref[...]).astype(BF16)
        xn_ref[...] = xn
        dt_ref[...] = _dot_nt(xn, wdt)

    one(xp_ref, xnp_ref, dtp_ref)

    @pl.when(pl.program_id(0) == 0)
    def _():
        one(xs_ref, xns_ref, dts_ref)


def _norm_dt(xp, xs, g, wdt):
    tp, d = xp.shape
    ts = xs.shape[0]
    nh = wdt.shape[0]
    tm = _pick(tp, (512, 256, 128))
    return pl.pallas_call(
        _norm_dt_kernel,
        grid=(tp // tm,),
        in_specs=[
            pl.BlockSpec((tm, d), lambda i: (i, 0)),
            pl.BlockSpec((ts, d), lambda i: (0, 0)),
            pl.BlockSpec((1, d), lambda i: (0, 0)),
            pl.BlockSpec((nh, d), lambda i: (0, 0)),
        ],
        out_specs=[
            pl.BlockSpec((tm, d), lambda i: (i, 0)),
            pl.BlockSpec((ts, d), lambda i: (0, 0)),
            pl.BlockSpec((tm, nh), lambda i: (i, 0)),
            pl.BlockSpec((ts, nh), lambda i: (0, 0)),
        ],
        out_shape=[
            jax.ShapeDtypeStruct((tp, d), BF16),
            jax.ShapeDtypeStruct((ts, d), BF16),
            jax.ShapeDtypeStruct((tp, nh), F32),
            jax.ShapeDtypeStruct((ts, nh), F32),
        ],
        compiler_params=_params(("arbitrary",)),
        name="norm_dt",
    )(xp, xs, g, wdt)


def _proj_a_kernel(xp_ref, xs_ref, wb_ref, wc_ref, wh_ref, cw_ref, st_ref,
                   op_ref, os_ref, cap_ref, cas_ref,
                   wbf, ubuf, sbuf, s1buf, s2buf, *, tiles_per_batch, nb_s, l_s):
    i = pl.program_id(1)
    tm = xp_ref.shape[0]
    ts = xs_ref.shape[0]
    cw = cw_ref[...]

    @pl.when(i == 0)
    def _():
        wbf[0] = wb_ref[...].astype(BF16)
        wbf[1] = wc_ref[...].astype(BF16)
        wbf[2] = wh_ref[...].astype(BF16)
        x = xs_ref[...]
        u = _dot_nt(x, wbf[1]) * _dot_nt(x, wbf[2])
        sbuf[pl.ds(0, SUBLANES), :] = jnp.zeros((SUBLANES, u.shape[1]), F32)
        sbuf[pl.ds(SUBLANES, ts), :] = u
        s1buf[...] = jnp.zeros_like(s1buf)
        s2buf[...] = jnp.zeros_like(s2buf)
        for b in range(nb_s):
            s1buf[pl.ds(b * l_s, 1), :] = st_ref[b, pl.ds(1, 1), :]
            s2buf[pl.ds(b * l_s, 1), :] = st_ref[b, pl.ds(0, 1), :]
            s2buf[pl.ds(b * l_s + 1, 1), :] = st_ref[b, pl.ds(1, 1), :]
        rmod = lax.broadcasted_iota(jnp.int32, (ts, 1), 0) % l_s
        prev1 = jnp.where(rmod == 0, s1buf[...], sbuf[pl.ds(SUBLANES - 1, ts), :])
        prev2 = jnp.where(rmod < 2, s2buf[...], sbuf[pl.ds(SUBLANES - 2, ts), :])
        v = prev2 * cw[0:1, :] + prev1 * cw[1:2, :] + u * cw[2:3, :]
        os_ref[...] = (_dot_nt(x, wbf[0]) * v).astype(BF16)
        for b in range(nb_s):
            cas_ref[b] = sbuf[pl.ds(SUBLANES + (b + 1) * l_s - 2, 2), :]

    @pl.when(i % tiles_per_batch == 0)
    def _():
        ubuf[pl.ds(0, SUBLANES), :] = jnp.zeros((SUBLANES, ubuf.shape[1]), F32)

    x = xp_ref[...]
    u = _dot_nt(x, wbf[1]) * _dot_nt(x, wbf[2])
    ubuf[pl.ds(SUBLANES, tm), :] = u
    v = (ubuf[pl.ds(SUBLANES - 2, tm), :] * cw[0:1, :]
         + ubuf[pl.ds(SUBLANES - 1, tm), :] * cw[1:2, :] + u * cw[2:3, :])
    op_ref[...] = (_dot_nt(x, wbf[0]) * v).astype(BF16)
    ubuf[pl.ds(0, SUBLANES), :] = ubuf[pl.ds(tm, SUBLANES), :]

    @pl.when(i % tiles_per_batch == tiles_per_batch - 1)
    def _():
        cap_ref[0] = ubuf[pl.ds(SUBLANES + tm - 2, 2), :]


def _proj_a(xnp, xns, w_in, conv_w, state_s, n_batch_p, seq_p, d_conv):
    tp, d = xnp.shape
    ts = xns.shape[0]
    nb_s = state_s.shape[0]
    l_s = ts // nb_s
    tn = 512
    tm = _pick(seq_p, (1024, 512, 256, 128))
    tpb = seq_p // tm
    ncol = d_conv // tn
    kern = functools.partial(_proj_a_kernel, tiles_per_batch=tpb, nb_s=nb_s, l_s=l_s)
    return pl.pallas_call(
        kern,
        grid=(ncol, tp // tm),
        in_specs=[
            pl.BlockSpec((tm, d), lambda j, i: (i, 0)),
            pl.BlockSpec((ts, d), lambda j, i: (0, 0)),
            pl.BlockSpec((tn, d), lambda j, i: (j, 0)),
            pl.BlockSpec((tn, d), lambda j, i: (j + ncol, 0)),
            pl.BlockSpec((tn, d), lambda j, i: (j + 2 * ncol, 0)),
            pl.BlockSpec((CONV_A_K, tn), lambda j, i: (0, j)),
            pl.BlockSpec((nb_s, CONV_A_K - 1, tn), lambda j, i: (0, 0, j)),
        ],
        out_specs=[
            pl.BlockSpec((tm, tn), lambda j, i: (i, j)),
            pl.BlockSpec((ts, tn), lambda j, i: (0, j)),
            pl.BlockSpec((1, CONV_A_K - 1, tn), lambda j, i: (i // tpb, 0, j)),
            pl.BlockSpec((nb_s, CONV_A_K - 1, tn), lambda j, i: (0, 0, j)),
        ],
        out_shape=[
            jax.ShapeDtypeStruct((tp, d_conv), BF16),
            jax.ShapeDtypeStruct((ts, d_conv), BF16),
            jax.ShapeDtypeStruct((n_batch_p, CONV_A_K - 1, d_conv), F32),
            jax.ShapeDtypeStruct((nb_s, CONV_A_K - 1, d_conv), F32),
        ],
        scratch_shapes=[
            pltpu.VMEM((3, tn, d), BF16),
            pltpu.VMEM((SUBLANES + tm, tn), F32),
            pltpu.VMEM((SUBLANES + ts, tn), F32),
            pltpu.VMEM((ts, tn), F32),
            pltpu.VMEM((ts, tn), F32),
        ],
        compiler_params=_params(("arbitrary", "arbitrary")),
        name="proj_a",
    )(xnp, xns, w_in, w_in, w_in, conv_w, state_s)


def _proj_raw_kernel(xp_ref, xs_ref, w_ref, op_ref, os_ref, wbf, *, nsplit, width):
    def emit(x_ref, o_ref):
        acc = _dot_nt(x_ref[...], wbf[...])
        if nsplit == 0:
            o_ref[...] = acc
        else:
            for s in range(nsplit):
                o_ref[s] = acc[:, s * width:(s + 1) * width]

    @pl.when(pl.program_id(1) == 0)
    def _():
        wbf[...] = w_ref[...].astype(BF16)
        emit(xs_ref, os_ref)

    emit(xp_ref, op_ref)


def _proj_raw(xnp, xns, w, col0, ncols, width):
    tp, d = xnp.shape
    ts = xns.shape[0]
    tn = 1024
    tm = _pick(tp, (1024, 512, 256, 128))
    assert col0 % tn == 0 and ncols % tn == 0
    jb = col0 // tn
    if width == 0:
        nsplit = 0
        out_specs = [pl.BlockSpec((tm, tn), lambda j, i: (i, j)),
                     pl.BlockSpec((ts, tn), lambda j, i: (0, j))]
        out_shape = [jax.ShapeDtypeStruct((tp, ncols), F32), jax.ShapeDtypeStruct((ts, ncols), F32)]
    else:
        nsplit = tn // width
        out_specs = [pl.BlockSpec((nsplit, tm, width), lambda j, i: (j, i, 0)),
                     pl.BlockSpec((nsplit, ts, width), lambda j, i: (j, 0, 0))]
        out_shape = [jax.ShapeDtypeStruct((ncols // width, tp, width), F32),
                     jax.ShapeDtypeStruct((ncols // width, ts, width), F32)]
    kern = functools.partial(_proj_raw_kernel, nsplit=nsplit, width=width)
    return pl.pallas_call(
        kern,
        grid=(ncols // tn, tp // tm),
        in_specs=[
            pl.BlockSpec((tm, d), lambda j, i: (i, 0)),
            pl.BlockSpec((ts, d), lambda j, i: (0, 0)),
            pl.BlockSpec((tn, d), lambda j, i: (j + jb, 0)),
        ],
        out_specs=out_specs,
        out_shape=out_shape,
        scratch_shapes=[pltpu.VMEM((tn, d), BF16)],
        compiler_params=_params(("arbitrary", "arbitrary")),
        name=f"proj_raw_{col0}",
    )(xnp, xns, w)


def _proj_conv_kernel(xp_ref, xs_ref, w_ref, cw_ref, cb_ref, st_ref, op_ref, os_ref, cmp_ref, cms_ref,
                      wbf, ubuf, sbuf, fix, *, tiles_per_batch, nb_s, l_s, nsplit, width):
    i = pl.program_id(1)
    tm = xp_ref.shape[0]
    ts = xs_ref.shape[0]
    tail = CONV_M_K - 1
    cw = cw_ref[...]
    bias = cb_ref[...]

    def store(o_ref, act):
        for s in range(nsplit):
            o_ref[s] = act[:, s * width:(s + 1) * width]

    @pl.when(i == 0)
    def _():
        wbf[...] = w_ref[...].astype(BF16)
        raw = _dot_nt(xs_ref[...], wbf[...])
        sbuf[pl.ds(0, SUBLANES), :] = jnp.zeros((SUBLANES, raw.shape[1]), F32)
        sbuf[pl.ds(SUBLANES, ts), :] = raw
        fix[...] = jnp.zeros_like(fix)
        for b in range(nb_s):
            for back in range(1, tail + 1):
                for m in range(back):
                    fix[back - 1, pl.ds(b * l_s + m, 1), :] = st_ref[b, pl.ds(tail + m - back, 1), :]
        rmod = lax.broadcasted_iota(jnp.int32, (ts, 1), 0) % l_s
        acc = raw * cw[tail:tail + 1, :]
        for back in range(1, tail + 1):
            tap = jnp.where(rmod < back, fix[back - 1], sbuf[pl.ds(SUBLANES - back, ts), :])
            acc = acc + tap * cw[tail - back:tail - back + 1, :]
        store(os_ref, jax.nn.silu(acc + bias))
        for b in range(nb_s):
            cms_ref[b] = sbuf[pl.ds(SUBLANES + (b + 1) * l_s - tail, tail), :]

    @pl.when(i % tiles_per_batch == 0)
    def _():
        ubuf[pl.ds(0, SUBLANES), :] = jnp.zeros((SUBLANES, ubuf.shape[1]), F32)

    raw = _dot_nt(xp_ref[...], wbf[...])
    ubuf[pl.ds(SUBLANES, tm), :] = raw
    acc = raw * cw[tail:tail + 1, :]
    for back in range(1, tail + 1):
        acc = acc + ubuf[pl.ds(SUBLANES - back, tm), :] * cw[tail - back:tail - back + 1, :]
    store(op_ref, jax.nn.silu(acc + bias))
    ubuf[pl.ds(0, SUBLANES), :] = ubuf[pl.ds(tm, SUBLANES), :]

    @pl.when(i % tiles_per_batch == tiles_per_batch - 1)
    def _():
        cmp_ref[0] = ubuf[pl.ds(SUBLANES + tm - tail, tail), :]


def _proj_conv(xnp, xns, w, conv_w, conv_b, state_s, col0, ch0, ncols, width, n_batch_p, seq_p):
    tp, d = xnp.shape
    ts = xns.shape[0]
    nb_s = state_s.shape[0]
    l_s = ts // nb_s
    tail = CONV_M_K - 1
    tn = 1024
    tm = _pick(seq_p, (1024, 512, 256, 128))
    tpb = seq_p // tm
    assert col0 % tn == 0 and ch0 % tn == 0 and ncols % tn == 0 and l_s >= tail
    jb, cb0 = col0 // tn, ch0 // tn
    nsplit = tn // width
    kern = functools.partial(_proj_conv_kernel, tiles_per_batch=tpb, nb_s=nb_s, l_s=l_s,
                             nsplit=nsplit, width=width)
    return pl.pallas_call(
        kern,
        grid=(ncols // tn, tp // tm),
        in_specs=[
            pl.BlockSpec((tm, d), lambda j, i: (i, 0)),
            pl.BlockSpec((ts, d), lambda j, i: (0, 0)),
            pl.BlockSpec((tn, d), lambda j, i: (j + jb, 0)),
            pl.BlockSpec((CONV_M_K, tn), lambda j, i: (0, j + cb0)),
            pl.BlockSpec((1, tn), lambda j, i: (0, j + cb0)),
            pl.BlockSpec((nb_s, tail, tn), lambda j, i: (0, 0, j + cb0)),
        ],
        out_specs=[
            pl.BlockSpec((nsplit, tm, width), lambda j, i: (j, i, 0)),
            pl.BlockSpec((nsplit, ts, width), lambda j, i: (j, 0, 0)),
            pl.BlockSpec((1, tail, tn), lambda j, i: (i // tpb, 0, j)),
            pl.BlockSpec((nb_s, tail, tn), lambda j, i: (0, 0, j)),
        ],
        out_shape=[
            jax.ShapeDtypeStruct((ncols // width, tp, width), F32),
            jax.ShapeDtypeStruct((ncols // width, ts, width), F32),
            jax.ShapeDtypeStruct((n_batch_p, tail, ncols), F32),
            jax.ShapeDtypeStruct((nb_s, tail, ncols), F32),
        ],
        scratch_shapes=[
            pltpu.VMEM((tn, d), BF16),
            pltpu.VMEM((SUBLANES + tm, tn), F32),
            pltpu.VMEM((SUBLANES + ts, tn), F32),
            pltpu.VMEM((tail, ts, tn), F32),
        ],
        compiler_params=_params(("arbitrary", "arbitrary")),
        name=f"proj_conv_{col0}",
    )(xnp, xns, w, conv_w, conv_b, state_s)


def _ssd_kernel(*refs, q, has_state, nchunks):
    (z_ref, xs_ref, b_ref, c_ref, dt_ref, dtb_ref, alog_ref, dsk_ref, gn_ref, *rest) = refs
    if has_state:
        hprev, *rest = rest
    (y_ref, oh, h_s, acg, rowt) = rest
    c = pl.program_id(1)
    nheads = N_GROUPS * HEADS_PER_GROUP

    @pl.when(c == 0)
    def _init():
        if has_state:
            h_s[...] = hprev[...]
        else:
            h_s[...] = jnp.zeros_like(h_s)

    dt = _softplus(dt_ref[...] + dtb_ref[...])
    da = dt * (-jnp.exp(alog_ref[...]))
    ri = lax.broadcasted_iota(jnp.int32, (q, q), 0)
    ci = lax.broadcasted_iota(jnp.int32, (q, q), 1)
    causal = ri >= ci
    tril = jnp.where(causal, 1.0, 0.0).astype(BF16)
    acum = sum(_dot(tril, p) for p in _split3(da))
    eye = jnp.where(lax.broadcasted_iota(jnp.int32, (nheads, nheads), 0)
                    == lax.broadcasted_iota(jnp.int32, (nheads, nheads), 1), 1.0, 0.0).astype(BF16)
    rowt[0] = sum(lax.dot_general(eye, p, NT_DIMS, preferred_element_type=F32) for p in _split3(acum))
    rowt[1] = sum(lax.dot_general(eye, p, NT_DIMS, preferred_element_type=F32) for p in _split3(dt))
    wend = jnp.exp(acum[q - 1:q, :] - acum) * dt
    rowt[2] = sum(lax.dot_general(eye, p, NT_DIMS, preferred_element_type=F32) for p in _split3(wend))
    for g in range(N_GROUPS):
        acg[g] = acum[:, g * HEADS_PER_GROUP:(g + 1) * HEADS_PER_GROUP]

    lane = lax.broadcasted_iota(jnp.int32, (1, LANES), 1)
    rowi = lax.broadcasted_iota(jnp.int32, (LANES, 1), 0)
    half_w = LANES // 2

    def group_body(g, carry):
        xs = xs_ref[g]
        bb = b_ref[g].astype(BF16)
        ccb16 = c_ref[g].astype(BF16)
        cb_ = lax.dot_general(ccb16, bb, NT_DIMS, preferred_element_type=F32)
        ac8 = acg[g]
        dsk = dsk_ref[g]
        lo_half = lane < half_w
        lo_rows = rowi < half_w
        ys = []
        for pair in range(HEADS_PER_GROUP // 2):
            sl = slice(pair * LANES, (pair + 1) * LANES)
            xp = xs[:, sl]
            hp = h_s[g, pl.ds(pair * LANES, LANES), :]
            ms_, acols, wrows = [], [], []
            for r in (2 * pair, 2 * pair + 1):
                head = g * HEADS_PER_GROUP + r
                acol = jnp.broadcast_to(ac8[:, r:r + 1], (q, LANES))
                arow = rowt[0, pl.ds(head, 1), :]
                drow = rowt[1, pl.ds(head, 1), :]
                decay = jnp.exp(jnp.where(causal, acol[:, :q] - arow, -jnp.inf))
                ms_.append((cb_ * decay * drow).astype(BF16))
                acols.append(acol)
                wrows.append(jnp.broadcast_to(rowt[2, pl.ds(head, 1), :], (half_w, q)))
            xlo = jnp.where(lo_half, xp, 0.0).astype(BF16)
            xhi = jnp.where(lo_half, 0.0, xp).astype(BF16)
            if q % LANES == 0:
                ydiag = _dot(jnp.concatenate(ms_, axis=1), jnp.concatenate([xlo, xhi], axis=0))
            else:
                ydiag = _dot(ms_[0], xlo) + _dot(ms_[1], xhi)
            ea = jnp.where(lo_half, jnp.exp(acols[0]), jnp.exp(acols[1]))
            yoff = ea * lax.dot_general(ccb16, hp.astype(BF16), NT_DIMS, preferred_element_type=F32)
            ys.append(dsk[:, sl] * xp + ydiag + yoff)
            xwt = (xp.T * jnp.concatenate(wrows, axis=0)).astype(BF16)
            dlast = [jnp.broadcast_to(jnp.exp(a[q - 1:q, :]), (LANES, LANES)) for a in acols]
            h_s[g, pl.ds(pair * LANES, LANES), :] = hp * jnp.where(lo_rows, dlast[0], dlast[1]) + _dot(xwt, bb)
        yg = jnp.concatenate(ys, axis=1)
        hh = yg * jax.nn.silu(z_ref[g])
        ms = jnp.mean(hh * hh, axis=-1, keepdims=True)
        y_ref[g] = (hh * lax.rsqrt(ms + EPS) * gn_ref[g]).astype(BF16)
        return carry

    lax.fori_loop(0, N_GROUPS, group_body, 0)

    @pl.when(c == nchunks - 1)
    def _():
        oh[...] = h_s[...]


def _ssd(z, xc, bc, dt_raw, prm, n_batch, seq, q, h_prev=None):
    nchunks = seq // q
    nheads = N_GROUPS * HEADS_PER_GROUP
    has_state = h_prev is not None
    g8 = N_GROUPS

    def tok(first):
        return lambda b, c: (first, b * nchunks + c, 0)

    def const3(b, c):
        return (0, 0, 0)

    in_specs = [
        pl.BlockSpec((g8, q, GROUP_W), tok(0)),
        pl.BlockSpec((g8, q, GROUP_W), tok(0)),
        pl.BlockSpec((g8, q, D_STATE), tok(0)),
        pl.BlockSpec((g8, q, D_STATE), tok(1)),
        pl.BlockSpec((q, nheads), lambda b, c: (b * nchunks + c, 0)),
        pl.BlockSpec((1, nheads), lambda b, c: (0, 0)),
        pl.BlockSpec((1, nheads), lambda b, c: (0, 0)),
        pl.BlockSpec((g8, 1, GROUP_W), const3),
        pl.BlockSpec((g8, 1, GROUP_W), const3),
    ]
    args = [z, xc, bc, bc, dt_raw, prm["dtb"], prm["alog"], prm["dsk"], prm["gn"]]

    def per_batch(shape):
        return pl.BlockSpec((None,) + shape, lambda b, c: (b,) + (0,) * len(shape))

    if has_state:
        in_specs.append(per_batch((g8, GROUP_W, D_STATE)))
        args.append(h_prev)
    t = n_batch * seq
    out_specs = [
        pl.BlockSpec((g8, q, GROUP_W), lambda b, c: (0, b * nchunks + c, 0)),
        per_batch((g8, GROUP_W, D_STATE)),
    ]
    out_shape = [
        jax.ShapeDtypeStruct((g8, t, GROUP_W), BF16),
        jax.ShapeDtypeStruct((n_batch, g8, GROUP_W, D_STATE), F32),
    ]
    scratch = [
        pltpu.VMEM((g8, GROUP_W, D_STATE), F32),
        pltpu.VMEM((g8, q, HEADS_PER_GROUP), F32),
        pltpu.VMEM((3, nheads, q), F32),
    ]
    kern = functools.partial(_ssd_kernel, q=q, has_state=has_state, nchunks=nchunks)
    return pl.pallas_call(
        kern,
        grid=(n_batch, nchunks),
        in_specs=in_specs,
        out_specs=out_specs,
        out_shape=out_shape,
        scratch_shapes=scratch,
        compiler_params=_params(("arbitrary", "arbitrary")),
        name="ssd_state" if has_state else "ssd",
    )(*args)


def _merge_kernel(ap_ref, as_ref, yp_ref, ys_ref, gap_ref, gmp_ref, gas_ref, gms_ref, wa_ref, wm_ref,
                  op_ref, os_ref, wab, wmb):
    def emit(a_ref, y_ref, ga_ref, gm_ref, o_ref):
        oa = _dot(a_ref[...], wab[...])
        om = _dot(y_ref[0], wmb[0])
        for g in range(1, N_GROUPS):
            om = om + _dot(y_ref[g], wmb[g])
        o_ref[...] = (jax.nn.sigmoid(ga_ref[...]) * oa + jax.nn.sigmoid(gm_ref[...]) * om).astype(BF16)

    @pl.when(pl.program_id(1) == 0)
    def _():
        wab[...] = wa_ref[...].astype(BF16)
        wmb[...] = wm_ref[...].astype(BF16)
        emit(as_ref, ys_ref, gas_ref, gms_ref, os_ref)

    emit(ap_ref, yp_ref, gap_ref, gmp_ref, op_ref)


def _merge(abv_p, abv_s, y_p, y_s, g_p, g_s, w_a_out, w_m_out3):
    tp, dc = abv_p.shape
    ts = abv_s.shape[0]
    dm = w_a_out.shape[1]
    tn = 512
    tm = _pick(tp, (512, 256, 128))
    ncol = dm // tn
    return pl.pallas_call(
        _merge_kernel,
        grid=(ncol, tp // tm),
        in_specs=[
            pl.BlockSpec((tm, dc), lambda j, i: (i, 0)),
            pl.BlockSpec((ts, dc), lambda j, i: (0, 0)),
            pl.BlockSpec((N_GROUPS, tm, GROUP_W), lambda j, i: (0, i, 0)),
            pl.BlockSpec((N_GROUPS, ts, GROUP_W), lambda j, i: (0, 0, 0)),
            pl.BlockSpec((tm, tn), lambda j, i: (i, j)),
            pl.BlockSpec((tm, tn), lambda j, i: (i, j + ncol)),
            pl.BlockSpec((ts, tn), lambda j, i: (0, j)),
            pl.BlockSpec((ts, tn), lambda j, i: (0, j + ncol)),
            pl.BlockSpec((dc, tn), lambda j, i: (0, j)),
            pl.BlockSpec((N_GROUPS, GROUP_W, tn), lambda j, i: (0, 0, j)),
        ],
        out_specs=[
            pl.BlockSpec((tm, tn), lambda j, i: (i, j)),
            pl.BlockSpec((ts, tn), lambda j, i: (0, j)),
        ],
        out_shape=[jax.ShapeDtypeStruct((tp, dm), BF16), jax.ShapeDtypeStruct((ts, dm), BF16)],
        scratch_shapes=[pltpu.VMEM((dc, tn), BF16), pltpu.VMEM((N_GROUPS, GROUP_W, tn), BF16)],
        compiler_params=_params(("arbitrary", "arbitrary")),
        name="merge",
    )(abv_p, abv_s, y_p, y_s, g_p, g_p, g_s, g_s, w_a_out, w_m_out3)


def _res_kernel(ap_ref, as_ref, rp_ref, rs_ref, w_ref, g_ref, *outs, n_prompt_tiles, merged_out, ts):
    if merged_out:
        h_ref, hn_ref, wbf = outs
    else:
        hp_ref, hs_ref, hnp_ref, hns_ref, wbf = outs
    i = pl.program_id(0)

    @pl.when(i == 0)
    def _():
        wbf[...] = w_ref[...].astype(BF16)

    def emit(a_ref, r_ref, store_h, store_hn):
        h = r_ref[...] + _dot(a_ref[...], wbf[...])
        store_h(h)
        store_hn(_rms(h, g_ref[...]))

    if merged_out:
        @pl.when(i < n_prompt_tiles)
        def _():
            def sh(h):
                h_ref[...] = h

            def shn(hn):
                hn_ref[...] = hn

            emit(ap_ref, rp_ref, sh, shn)

        @pl.when(i == n_prompt_tiles)
        def _():
            def sh(h):
                h_ref[pl.ds(0, ts), :] = h

            def shn(hn):
                hn_ref[pl.ds(0, ts), :] = hn

            emit(as_ref, rs_ref, sh, shn)
    else:
        def shp(h):
            hp_ref[...] = h

        def shnp(hn):
            hnp_ref[...] = hn.astype(hnp_ref.dtype)

        emit(ap_ref, rp_ref, shp, shnp)

        @pl.when(i == 0)
        def _():
            def shs(h):
                hs_ref[...] = h

            def shns(hn):
                hns_ref[...] = hn.astype(hns_ref.dtype)

            emit(as_ref, rs_ref, shs, shns)


def _res(a_p, a_s, r_p, r_s, w, g, merged_out, hn_dtype):
    tp, d = a_p.shape
    ts = a_s.shape[0]
    tm = 256
    assert tp % tm == 0 and ts <= tm
    npt = tp // tm
    last = npt - 1
    in_specs = [
        pl.BlockSpec((tm, d), lambda i: (jnp.minimum(i, last), 0)),
        pl.BlockSpec((ts, d), lambda i: (0, 0)),
        pl.BlockSpec((tm, d), lambda i: (jnp.minimum(i, last), 0)),
        pl.BlockSpec((ts, d), lambda i: (0, 0)),
        pl.BlockSpec((d, d), lambda i: (0, 0), pipeline_mode=pl.Buffered(1)),
        pl.BlockSpec((1, d), lambda i: (0, 0)),
    ]
    if merged_out:
        grid = (npt + 1,)
        out_specs = [pl.BlockSpec((tm, d), lambda i: (i, 0)), pl.BlockSpec((tm, d), lambda i: (i, 0))]
        out_shape = [jax.ShapeDtypeStruct((tp + ts, d), F32), jax.ShapeDtypeStruct((tp + ts, d), hn_dtype)]
    else:
        grid = (npt,)
        out_specs = [pl.BlockSpec((tm, d), lambda i: (i, 0)), pl.BlockSpec((ts, d), lambda i: (0, 0)),
                     pl.BlockSpec((tm, d), lambda i: (i, 0)), pl.BlockSpec((ts, d), lambda i: (0, 0))]
        out_shape = [jax.ShapeDtypeStruct((tp, d), F32), jax.ShapeDtypeStruct((ts, d), F32),
                     jax.ShapeDtypeStruct((tp, d), hn_dtype), jax.ShapeDtypeStruct((ts, d), hn_dtype)]
    kern = functools.partial(_res_kernel, n_prompt_tiles=npt, merged_out=merged_out, ts=ts)
    return pl.pallas_call(
        kern,
        grid=grid,
        in_specs=in_specs,
        out_specs=out_specs,
        out_shape=out_shape,
        scratch_shapes=[pltpu.VMEM((d, d), BF16)],
        compiler_params=_params(("arbitrary",)),
        name="res_merged" if merged_out else "res",
    )(a_p, a_s, r_p, r_s, w, g)


def _kv_kernel(m_ref, g_ref, wk_ref, wv_ref, k_ref, v_ref, mn):
    @pl.when(pl.program_id(0) == 0)
    def _():
        mn[...] = _rms(m_ref[...], g_ref[...]).astype(BF16)

    k_ref[...] = _dot(mn[...], wk_ref[...].astype(BF16))
    v_ref[...] = _dot(mn[...], wv_ref[...].astype(BF16))


def _memory_kv(mem2d, g, w_k, w_v):
    m, d = mem2d.shape
    tn = 512
    return pl.pallas_call(
        _kv_kernel,
        grid=(d // tn,),
        in_specs=[
            pl.BlockSpec((m, d), lambda j: (0, 0)),
            pl.BlockSpec((1, d), lambda j: (0, 0)),
            pl.BlockSpec((d, tn), lambda j: (0, j)),
            pl.BlockSpec((d, tn), lambda j: (0, j)),
        ],
        out_specs=[pl.BlockSpec((m, tn), lambda j: (0, j)), pl.BlockSpec((m, tn), lambda j: (0, j))],
        out_shape=[jax.ShapeDtypeStruct((m, d), F32), jax.ShapeDtypeStruct((m, d), F32)],
        scratch_shapes=[pltpu.VMEM((m, d), BF16)],
        compiler_params=_params(("arbitrary",)),
        name="memory_kv",
    )(mem2d, g, w_k, w_v)


def _attn_kernel(hn_ref, k_ref, v_ref, wq_ref, o_ref, wqb, kb, vb):
    b = pl.program_id(0)
    i = pl.program_id(1)

    @pl.when((b == 0) & (i == 0))
    def _():
        wqb[...] = wq_ref[...].astype(BF16)

    d = wqb.shape[1]
    dh = d // N_XHEADS

    @pl.when(i == 0)
    def _():
        kb[...] = k_ref[...].astype(BF16)
        vb[...] = v_ref[...].astype(BF16)

    q = _dot(hn_ref[...], wqb[...])
    outs = []
    for h in range(N_XHEADS):
        sl = slice(h * dh, (h + 1) * dh)
        s = lax.dot_general(q[:, sl].astype(BF16), kb[:, sl], NT_DIMS, preferred_element_type=F32)
        s = s * (dh ** -0.5)
        e = jnp.exp(s - jnp.max(s, axis=-1, keepdims=True))
        p = e / jnp.sum(e, axis=-1, keepdims=True)
        outs.append(_dot(p.astype(BF16), vb[:, sl]))
    o_ref[...] = jnp.concatenate(outs, axis=1).astype(BF16)


def _attn(hn, k, v, w_q, n_batch, seq):
    t, d = hn.shape
    nm = k.shape[1]
    tm = _pick(seq, (512, 256, 128, 64, 32, 16))
    tpb = seq // tm
    kv_spec = pl.BlockSpec((None, nm, d), lambda b, i: (b, 0, 0))
    return pl.pallas_call(
        _attn_kernel,
        grid=(n_batch, tpb),
        in_specs=[
            pl.BlockSpec((tm, d), lambda b, i: (b * tpb + i, 0)),
            kv_spec,
            kv_spec,
            pl.BlockSpec((d, d), lambda b, i: (0, 0), pipeline_mode=pl.Buffered(1)),
        ],
        out_specs=pl.BlockSpec((tm, d), lambda b, i: (b * tpb + i, 0)),
        out_shape=jax.ShapeDtypeStruct((t, d), BF16),
        scratch_shapes=[pltpu.VMEM((d, d), BF16), pltpu.VMEM((nm, d), BF16), pltpu.VMEM((nm, d), BF16)],
        compiler_params=_params(("arbitrary", "arbitrary")),
        name=f"attn_{seq}",
    )(hn, k, v, w_q)


def _router_kernel(x_ref, w_ref, b_ref, ri_ref, rw_ref):
    logits = _dot(x_ref[...].astype(BF16), w_ref[...].astype(BF16)) + b_ref[...]
    lane_i = lax.broadcasted_iota(jnp.int32, logits.shape, 1)
    lane = lane_i.astype(F32)
    ninf = -jnp.inf
    big = float(LANES)
    is_g = lane < N_EXPERT_GROUPS
    gl = jnp.where(is_g, logits, ninf)
    gmax = jnp.max(gl, axis=-1, keepdims=True)
    gsel = jnp.min(jnp.where(gl == gmax, lane, big), axis=-1, keepdims=True)
    pg = 1.0 / jnp.sum(jnp.where(is_g, jnp.exp(gl - gmax), 0.0), axis=-1, keepdims=True)
    lo = N_EXPERT_GROUPS + EXPERTS_PER_GROUP * gsel
    el = jnp.where(lane >= lo, jnp.where(lane < lo + EXPERTS_PER_GROUP, logits, ninf), ninf)
    m1 = jnp.max(el, axis=-1, keepdims=True)
    i1 = jnp.min(jnp.where(el == m1, lane, big), axis=-1, keepdims=True)
    el2 = jnp.where(lane == i1, ninf, el)
    m2 = jnp.max(el2, axis=-1, keepdims=True)
    i2 = jnp.min(jnp.where(el2 == m2, lane, big), axis=-1, keepdims=True)
    e = jnp.exp(m2 - m1)
    w1 = pg / (1.0 + e)
    w2 = pg * e / (1.0 + e)
    ri_ref[...] = jnp.where(lane_i == 0, i1 - N_EXPERT_GROUPS,
                            jnp.where(lane_i == 1, i2 - N_EXPERT_GROUPS, 0.0)).astype(jnp.int32)
    rw_ref[...] = jnp.where(lane_i == 0, w1, jnp.where(lane_i == 1, w2, 0.0))


def _router(hn_all, w_r, b_r):
    t, d = hn_all.shape
    tm = _pick(t, (640, 512, 384, 256, 128))
    return pl.pallas_call(
        _router_kernel,
        grid=(t // tm,),
        in_specs=[
            pl.BlockSpec((tm, d), lambda i: (i, 0)),
            pl.BlockSpec((d, LANES), lambda i: (0, 0)),
            pl.BlockSpec((1, LANES), lambda i: (0, 0)),
        ],
        out_specs=[pl.BlockSpec((tm, LANES), lambda i: (i, 0)), pl.BlockSpec((tm, LANES), lambda i: (i, 0))],
        out_shape=[jax.ShapeDtypeStruct((t, LANES), jnp.int32), jax.ShapeDtypeStruct((t, LANES), F32)],
        compiler_params=_params(("arbitrary",)),
        name="router",
    )(hn_all, w_r, b_r)


def _row_copy(src, dst, s, d, sem):
    return pltpu.make_async_copy(src.at[pl.ds(s, 1)], dst.at[pl.ds(d, 1)], sem)


def _dispatch_kernel(dest_ref, pad0_ref, padn_ref, nsub_ref, x_ref, o_ref, zrow, zblk, sem, zsem, csem, bsem, *,
                     chunk, n_blocks):
    base = pl.program_id(0) * chunk

    @pl.when(pl.program_id(0) == 0)
    def _():
        zrow[...] = jnp.zeros_like(zrow)
        zblk[...] = jnp.zeros_like(zblk)

        def pad_plan(e):
            p0 = pad0_ref[e]
            head = jnp.minimum(padn_ref[e], (SUBLANES - p0 % SUBLANES) % SUBLANES)
            return p0, head, (padn_ref[e] - head) // SUBLANES

        def group_copy(row):
            return pltpu.make_async_copy(zrow, o_ref.at[pl.ds(pl.multiple_of(row, SUBLANES), SUBLANES)], csem)

        def pad_start(e, carry):
            p0, head, ngroups = pad_plan(e)

            def start_row(r, c):
                _row_copy(zrow, o_ref, 0, p0 + r, zsem).start()
                return c

            def start_group(j, c):
                group_copy(p0 + head + j * SUBLANES).start()
                return c

            lax.fori_loop(0, head, start_row, 0)
            return lax.fori_loop(0, ngroups, start_group, carry)

        def pad_wait(e, carry):
            _, head, ngroups = pad_plan(e)

            def wait_row(r, c):
                _row_copy(zrow, o_ref, 0, 0, zsem).wait()
                return c

            def wait_group(j, c):
                group_copy(0).wait()
                return c

            lax.fori_loop(0, head, wait_row, 0)
            return lax.fori_loop(0, ngroups, wait_group, carry)

        lax.fori_loop(0, N_EXPERTS, pad_start, 0)

        def blk_copy(b):
            return pltpu.make_async_copy(zblk, o_ref.at[pl.ds(pl.multiple_of(b * EXPERT_SUB, EXPERT_SUB),
                                                             EXPERT_SUB)], bsem)

        def tail_start(b, c):
            blk_copy(b).start()
            return c

        def tail_wait(b, c):
            blk_copy(b).wait()
            return c

        lax.fori_loop(nsub_ref[0], n_blocks, tail_start, 0)
        lax.fori_loop(0, N_EXPERTS, pad_wait, 0)
        lax.fori_loop(nsub_ref[0], n_blocks, tail_wait, 0)

    def issue(r, carry):
        t = base + r
        _row_copy(x_ref, o_ref, r, dest_ref[2 * t], sem).start()
        _row_copy(x_ref, o_ref, r, dest_ref[2 * t + 1], sem).start()
        return carry

    lax.fori_loop(0, chunk, issue, 0, unroll=8)

    for _ in range(2):
        pltpu.make_async_copy(x_ref, o_ref.at[pl.ds(0, chunk)], sem).wait()


def _dispatch(tables, hn_all, n_blocks):
    dest, pad0, padn, nsub = tables
    t, d = hn_all.shape
    chunk = _pick(t, (640, 512, 384, 256, 128))
    kern = functools.partial(_dispatch_kernel, chunk=chunk, n_blocks=n_blocks)
    return pl.pallas_call(
        kern,
        grid_spec=pltpu.PrefetchScalarGridSpec(
            num_scalar_prefetch=4,
            grid=(t // chunk,),
            in_specs=[pl.BlockSpec((chunk, d), lambda i, *_: (i, 0))],
            out_specs=pl.BlockSpec(memory_space=pl.ANY),
            scratch_shapes=[pltpu.VMEM((SUBLANES, d), F32), pltpu.VMEM((EXPERT_SUB, d), F32),
                            pltpu.SemaphoreType.DMA(()), pltpu.SemaphoreType.DMA(()),
                            pltpu.SemaphoreType.DMA(()), pltpu.SemaphoreType.DMA(())],
        ),
        out_shape=jax.ShapeDtypeStruct((n_blocks * EXPERT_SUB, d), F32),
        compiler_params=_params(("arbitrary",)),
        name="dispatch",
    )(dest, pad0, padn, nsub, hn_all)


def _expert_kernel(bstart_ref, wg_ref, wu_ref, wd_ref, x_hbm, y_hbm, gcache, ucache, dcache, xbuf, ybuf, xb,
                   xsem, ysem, *, n_blocks):
    e = pl.program_id(0)
    k = pl.program_id(1)
    nk = gcache.shape[1]
    total = bstart_ref[N_EXPERTS]

    def x_copy(b, slot):
        rows = pl.ds(pl.multiple_of(b * EXPERT_SUB, EXPERT_SUB), EXPERT_SUB)
        return pltpu.make_async_copy(x_hbm.at[rows], xbuf.at[slot], xsem.at[slot])

    def y_copy(b, slot):
        rows = pl.ds(pl.multiple_of(b * EXPERT_SUB, EXPERT_SUB), EXPERT_SUB)
        return pltpu.make_async_copy(ybuf.at[slot], y_hbm.at[rows], ysem.at[slot])

    @pl.when(e < N_EXPERTS)
    def _():
        slot = e % 2
        gcache[slot, k] = wg_ref[...].astype(BF16)
        ucache[slot, k] = wu_ref[...].astype(BF16)
        dcache[slot, k] = wd_ref[...].astype(BF16)

    @pl.when((e == 0) & (k == 0) & (total > 0))
    def _():
        x_copy(0, 0).start()

    @pl.when(e >= 1)
    def _():
        owner = e - 1
        wslot = owner % 2
        b0 = bstart_ref[owner]
        n = bstart_ref[owner + 1] - b0

        cpu = 2
        upb = nk // cpu

        def unit(u, carry):
            b = b0 + u // upb
            part = u % upb
            slot = b % 2

            @pl.when(part == 0)
            def _():
                x_copy(b, slot).wait()

                @pl.when(b + 1 < total)
                def _():
                    x_copy(b + 1, 1 - slot).start()

                xb[...] = xbuf[slot].astype(BF16)

                @pl.when(b >= 2)
                def _():
                    y_copy(b - 2, slot).wait()

                ybuf[slot] = jnp.zeros(ybuf.shape[1:], F32)

            x = xb[...]
            acc = None
            for c in range(cpu):
                kk = part * cpu + c
                hid = (jax.nn.silu(_dot(x, gcache[wslot, kk])) * _dot(x, ucache[wslot, kk])).astype(BF16)
                y = _dot(hid, dcache[wslot, kk])
                acc = y if acc is None else acc + y
            ybuf[slot] += acc

            @pl.when(part == upb - 1)
            def _():
                y_copy(b, slot).start()

            return carry

        lax.fori_loop((n * upb * k) // nk, (n * upb * (k + 1)) // nk, unit, 0)

    @pl.when((e == N_EXPERTS) & (k == nk - 1))
    def _():
        @pl.when(total >= 2)
        def _():
            y_copy(total - 2, total % 2).wait()

        @pl.when(total >= 1)
        def _():
            y_copy(total - 1, (total - 1) % 2).wait()

        ybuf[0] = jnp.zeros(ybuf.shape[1:], F32)

        def zfill(b, carry):
            y_copy(b, 0).start()
            y_copy(b, 0).wait()
            return carry

        lax.fori_loop(total, n_blocks, zfill, 0)


def _experts(bstart, x_sorted, w_gate, w_up, w_down, kchunk):
    nrows, d = x_sorted.shape
    n_blocks = nrows // EXPERT_SUB
    de = w_gate.shape[2]
    nk = de // kchunk
    last = N_EXPERTS - 1

    def widx(e, k):
        return jnp.minimum(e, last), jnp.where(e <= last, k, nk - 1)

    def in_idx(e, k, b):
        ee, kk = widx(e, k)
        return (ee, 0, kk)

    def down_idx(e, k, b):
        ee, kk = widx(e, k)
        return (ee, kk, 0)

    kern = functools.partial(_expert_kernel, n_blocks=n_blocks)
    return pl.pallas_call(
        kern,
        grid_spec=pltpu.PrefetchScalarGridSpec(
            num_scalar_prefetch=1,
            grid=(N_EXPERTS + 1, nk),
            in_specs=[
                pl.BlockSpec((None, d, kchunk), in_idx),
                pl.BlockSpec((None, d, kchunk), in_idx),
                pl.BlockSpec((None, kchunk, d), down_idx),
                pl.BlockSpec(memory_space=pl.ANY),
            ],
            out_specs=pl.BlockSpec(memory_space=pl.ANY),
            scratch_shapes=[
                pltpu.VMEM((2, nk, d, kchunk), BF16),
                pltpu.VMEM((2, nk, d, kchunk), BF16),
                pltpu.VMEM((2, nk, kchunk, d), BF16),
                pltpu.VMEM((2, EXPERT_SUB, d), F32),
                pltpu.VMEM((2, EXPERT_SUB, d), F32),
                pltpu.VMEM((EXPERT_SUB, d), BF16),
                pltpu.SemaphoreType.DMA((2,)),
                pltpu.SemaphoreType.DMA((2,)),
            ],
        ),
        out_shape=jax.ShapeDtypeStruct((nrows, d), F32),
        compiler_params=_params(("arbitrary", "arbitrary")),
        name="experts",
    )(bstart, w_gate, w_up, w_down, x_sorted)


def _final_kernel(dest_ref, h_ref, rw_ref, ys_ref, g_ref, yp_ref, yss_ref, ya, yb, sem, *,
                  tm, n_prompt_tiles, ts):
    i = pl.program_id(0)

    def gather(tile, n):
        slot = tile % 2

        def issue(r, carry):
            t = tile * tm + r
            _row_copy(ys_ref, ya.at[slot], dest_ref[2 * t], r, sem.at[slot]).start()
            _row_copy(ys_ref, yb.at[slot], dest_ref[2 * t + 1], r, sem.at[slot]).start()
            return carry

        lax.fori_loop(0, n, issue, 0, unroll=8)

    def emit(n, o_ref):
        slot = i % 2
        for buf in (ya, yb):
            pltpu.make_async_copy(ys_ref.at[pl.ds(0, n)], buf.at[slot, pl.ds(0, n)], sem.at[slot]).wait()
        w = rw_ref[pl.ds(0, n), :]
        h = (h_ref[pl.ds(0, n), :] + w[:, 0:1] * ya[slot, pl.ds(0, n), :]
             + w[:, 1:2] * yb[slot, pl.ds(0, n), :])
        o_ref[...] = _rms(h, g_ref[...])

    @pl.when(i == 0)
    def _():
        gather(i, tm)

    @pl.when(i + 1 < n_prompt_tiles)
    def _():
        gather(i + 1, tm)

    @pl.when(i + 1 == n_prompt_tiles)
    def _():
        gather(i + 1, ts)

    @pl.when(i < n_prompt_tiles)
    def _():
        emit(tm, yp_ref)

    @pl.when(i == n_prompt_tiles)
    def _():
        emit(ts, yss_ref)


def _final(dest, h_all, rw, y_sorted, g, tp, ts):
    t, d = h_all.shape
    tm = 256
    assert tp % tm == 0 and ts <= tm and t == tp + ts
    npt = tp // tm
    kern = functools.partial(_final_kernel, tm=tm, n_prompt_tiles=npt, ts=ts)
    return pl.pallas_call(
        kern,
        grid_spec=pltpu.PrefetchScalarGridSpec(
            num_scalar_prefetch=1,
            grid=(npt + 1,),
            in_specs=[
                pl.BlockSpec((tm, d), lambda i, dref: (i, 0)),
                pl.BlockSpec((tm, LANES), lambda i, dref: (i, 0)),
                pl.BlockSpec(memory_space=pl.ANY),
                pl.BlockSpec((1, d), lambda i, dref: (0, 0)),
            ],
            out_specs=[
                pl.BlockSpec((tm, d), lambda i, dref: (jnp.minimum(i, npt - 1), 0)),
                pl.BlockSpec((ts, d), lambda i, dref: (0, 0)),
            ],
            scratch_shapes=[pltpu.VMEM((2, tm, d), F32), pltpu.VMEM((2, tm, d), F32),
                            pltpu.SemaphoreType.DMA((2,))],
        ),
        out_shape=[jax.ShapeDtypeStruct((tp, d), F32), jax.ShapeDtypeStruct((ts, d), F32)],
        compiler_params=_params(("arbitrary",)),
        name="final",
    )(dest, h_all, rw, y_sorted, g)


def _routing_tables(eid):
    i32 = jnp.int32
    onehot = (eid[:, None] == jnp.arange(N_EXPERTS, dtype=i32)[None, :]).astype(i32)
    csum = jnp.cumsum(onehot, axis=0)
    counts = csum[-1]
    nsub = (counts + EXPERT_SUB - 1) // EXPERT_SUB
    bend = jnp.cumsum(nsub)
    bstart = bend - nsub
    seg = bstart * EXPERT_SUB
    dest = jnp.sum(onehot * (csum - 1 + seg[None, :]), axis=1).astype(i32)
    pad0 = (seg + counts).astype(i32)
    padn = (nsub * EXPERT_SUB - counts).astype(i32)
    bstart_all = jnp.concatenate([bstart, bend[-1:]]).astype(i32)
    return (dest, pad0, padn, bend[-1:].astype(i32)), bstart_all


def _group_major(a, width):
    lead = a.shape[:-1]
    g = a.shape[-1] // width
    return jnp.moveaxis(a.reshape(lead + (g, width)), -2, 0)


def kernel(x_prompt, x_sample, cache_mem_k, cache_mem_v, state_conv_a, state_conv_m, state_ssm, mem_prompt,
           norm_mix, w_in, conv_a_w, w_a_out, conv_m_w, conv_m_b, dt_bias, a_log, d_skip, ssm_norm,
           w_m_out, w_o, norm_cross, norm_mem, w_q, w_k, w_v, w_co, norm_ffn, w_rg, b_rg, w_re, b_re,
           w_gate, w_up, w_down, norm_final):
    depth = w_in.shape[0]
    assert depth == 1, "single-layer step"
    nbp, seq_p, d = x_prompt.shape
    nbs, seq_s, _ = x_sample.shape
    tp, ts = nbp * seq_p, nbs * seq_s
    n_mem = mem_prompt.shape[1]
    d_conv = conv_a_w.shape[2]
    d_inner = w_m_out.shape[1]
    nheads = dt_bias.shape[1]
    bc_w = N_GROUPS * D_STATE
    l = 0

    xp = x_prompt.reshape(tp, d)
    xs = x_sample.reshape(ts, d)
    w_in_t = jnp.swapaxes(w_in[l], 0, 1)
    col_z = 3 * d_conv
    col_x = col_z + d_inner
    col_b = col_x + d_inner
    col_dt = col_b + 2 * bc_w
    col_g = col_dt + nheads
    w_dt = w_in_t[col_dt:col_g]
    w_gates = w_in_t[col_g:]

    mk2, mv2 = _memory_kv(mem_prompt.reshape(nbp * n_mem, d), norm_mem[l][None], w_k[l], w_v[l])

    xnp, xns, dt_p, dt_s = _norm_dt(xp, xs, norm_mix[l][None], w_dt)
    abv_p, abv_s, ca_p, ca_s = _proj_a(xnp, xns, w_in_t, conv_a_w[l], state_conv_a[l], nbp, seq_p, d_conv)
    z_p, z_s = _proj_raw(xnp, xns, w_in_t, col_z, d_inner, GROUP_W)
    cw, cbias, scm = conv_m_w[l], conv_m_b[l][None], state_conv_m[l]
    xc_p, xc_s, cmx_p, cmx_s = _proj_conv(xnp, xns, w_in_t, cw, cbias, scm, col_x, 0, d_inner, GROUP_W,
                                           nbp, seq_p)
    bc_p, bc_s, cmb_p, cmb_s = _proj_conv(xnp, xns, w_in_t, cw, cbias, scm, col_b, d_inner, 2 * bc_w, D_STATE,
                                           nbp, seq_p)
    g_p, g_s = _proj_raw(xnp, xns, w_gates, 0, 2 * d, 0)

    prm = {
        "dtb": dt_bias[l][None],
        "alog": a_log[l][None],
        "dsk": _group_major(jnp.repeat(d_skip[l], HEAD_DIM)[None], GROUP_W),
        "gn": _group_major(ssm_norm[l][None], GROUP_W),
    }
    q_p = _pick(seq_p, (128, 64, 32, 16, 8))
    y_p, h_p = _ssd(z_p, xc_p, bc_p, dt_p, prm, nbp, seq_p, q_p)
    y_s, h_s = _ssd(z_s, xc_s, bc_s, dt_s, prm, nbs, seq_s, seq_s,
                    h_prev=state_ssm[l].reshape(nbs, N_GROUPS, GROUP_W, D_STATE))

    merged_p, merged_s = _merge(abv_p, abv_s, y_p, y_s, g_p, g_s, w_a_out[l],
                                w_m_out[l].reshape(N_GROUPS, GROUP_W, d))
    h1_p, h1_s, hn1_p, hn1_s = _res(merged_p, merged_s, xp, xs, w_o[l], norm_cross[l][None], False, BF16)

    att_p = _attn(hn1_p, mk2.reshape(nbp, n_mem, d), mv2.reshape(nbp, n_mem, d), w_q[l], nbp, seq_p)
    att_s = _attn(hn1_s, cache_mem_k[l].reshape(nbs, n_mem, d), cache_mem_v[l].reshape(nbs, n_mem, d),
                  w_q[l], nbs, seq_s)
    h2_all, hn2_all = _res(att_p, att_s, h1_p, h1_s, w_co[l], norm_ffn[l][None], True, F32)

    npad = LANES - N_EXPERT_GROUPS - N_EXPERTS
    w_r = jnp.concatenate([w_rg[l], w_re[l], jnp.zeros((d, npad), F32)], axis=1)
    b_r = jnp.concatenate([b_rg[l], b_re[l], jnp.zeros((npad,), F32)])[None]
    ri, rw = _router(hn2_all, w_r, b_r)
    tables, bstart = _routing_tables(ri[:, :2].reshape(-1))
    n_blocks = -(-2 * (tp + ts) // EXPERT_SUB) + N_EXPERTS
    x_sorted = _dispatch(tables, hn2_all, n_blocks)
    y_sorted = _experts(bstart, x_sorted, w_gate[l], w_up[l], w_down[l], 256)
    y_prompt, y_sample = _final(tables[0], h2_all, rw, y_sorted, norm_final[None], tp, ts)

    xh = d // N_XHEADS
    return (
        y_prompt.reshape(nbp, seq_p, d),
        y_sample.reshape(nbs, seq_s, d),
        mk2.reshape(1, nbp, n_mem, N_XHEADS, xh),
        mv2.reshape(1, nbp, n_mem, N_XHEADS, xh),
        ca_p[None],
        jnp.concatenate([cmx_p, cmb_p], axis=-1)[None],
        h_p.reshape(1, nbp, nheads, HEAD_DIM, D_STATE),
        ca_s[None],
        jnp.concatenate([cmx_s, cmb_s], axis=-1)[None],
        h_s.reshape(1, nbs, nheads, HEAD_DIM, D_STATE),
    )
```

```python
import functools

import jax
import jax.numpy as jnp
from jax import lax
from jax.experimental import pallas as pl
from jax.experimental.pallas import tpu as pltpu

F32 = jnp.float32
BF16 = jnp.bfloat16
EPS = 1e-6

V7X_VMEM_BYTES = 64 * 1024 * 1024
VMEM_LIMIT = V7X_VMEM_BYTES - 8 * 1024 * 1024
LANES = 128
SUBLANES = 8

N_GROUPS = 8
HEADS_PER_GROUP = 8
HEAD_DIM = 64
D_STATE = 128
GROUP_W = HEADS_PER_GROUP * HEAD_DIM
N_XHEADS = 4
N_EXPERTS = 32
N_EXPERT_GROUPS = 4
EXPERTS_PER_GROUP = 8
EXPERT_SUB = 256
CONV_A_K = 3
CONV_M_K = 4

NT_DIMS = (((1,), (1,)), ((), ()))
TN_DIMS = (((0,), (0,)), ((), ()))


def _params(sem):
    return pltpu.CompilerParams(dimension_semantics=sem, vmem_limit_bytes=VMEM_LIMIT)


def _pick(n, cands):
    for c in cands:
        if n % c == 0:
            return c
    raise ValueError(f"no tile for {n} in {cands}")


def _dot(a, b):
    return jnp.dot(a, b, preferred_element_type=F32)


def _dot_nt(a, b):
    return lax.dot_general(a, b, NT_DIMS, preferred_element_type=F32)


def _rms(x, g):
    return x * lax.rsqrt(jnp.mean(x * x, axis=-1, keepdims=True) + EPS) * g


def _split3(x):
    hi = x.astype(BF16)
    r = x - hi.astype(F32)
    mid = r.astype(BF16)
    lo = (r - mid.astype(F32)).astype(BF16)
    return hi, mid, lo


def _softplus(x):
    return jnp.maximum(x, 0.0) + jnp.log1p(jnp.exp(-jnp.abs(x)))


def _norm_dt_kernel(xp_ref, xs_ref, g_ref, wdt_ref, xnp_ref, xns_ref, dtp_ref, dts_ref):
    wdt = wdt_ref[...].astype(BF16)

    def one(x_ref, xn_ref, dt_ref):
        xn = _rms(x_ref[...], g_ref[...]).astype(BF16)
        xn_ref[...] = xn
        dt_ref[...] = _dot_nt(xn, wdt)

    one(xp_ref, xnp_ref, dtp_ref)

    @pl.when(pl.program_id(0) == 0)
    def _():
        one(xs_ref, xns_ref, dts_ref)


def _norm_dt(xp, xs, g, wdt):
    tp, d = xp.shape
    ts = xs.shape[0]
    nh = wdt.shape[0]
    tm = _pick(tp, (512, 256, 128))
    return pl.pallas_call(
        _norm_dt_kernel,
        grid=(tp // tm,),
        in_specs=[
            pl.BlockSpec((tm, d), lambda i: (i, 0)),
            pl.BlockSpec((ts, d), lambda i: (0, 0)),
            pl.BlockSpec((1, d), lambda i: (0, 0)),
            pl.BlockSpec((nh, d), lambda i: (0, 0)),
        ],
        out_specs=[
            pl.BlockSpec((tm, d), lambda i: (i, 0)),
            pl.BlockSpec((ts, d), lambda i: (0, 0)),
            pl.BlockSpec((tm, nh), lambda i: (i, 0)),
            pl.BlockSpec((ts, nh), lambda i: (0, 0)),
        ],
        out_shape=[
            jax.ShapeDtypeStruct((tp, d), BF16),
            jax.ShapeDtypeStruct((ts, d), BF16),
            jax.ShapeDtypeStruct((tp, nh), F32),
            jax.ShapeDtypeStruct((ts, nh), F32),
        ],
        compiler_params=_params(("arbitrary",)),
        name="norm_dt",
    )(xp, xs, g, wdt)


def _proj_a_kernel(xp_ref, xs_ref, wb_ref, wc_ref, wh_ref, cw_ref, st_ref,
                   op_ref, os_ref, cap_ref, cas_ref,
                   wbf, ubuf, sbuf, s1buf, s2buf, *, tiles_per_batch, nb_s, l_s):
    i = pl.program_id(1)
    tm = xp_ref.shape[0]
    ts = xs_ref.shape[0]
    cw = cw_ref[...]

    @pl.when(i == 0)
    def _():
        wbf[0] = wb_ref[...].astype(BF16)
        wbf[1] = wc_ref[...].astype(BF16)
        wbf[2] = wh_ref[...].astype(BF16)
        x = xs_ref[...]
        u = _dot_nt(x, wbf[1]) * _dot_nt(x, wbf[2])
        sbuf[pl.ds(0, SUBLANES), :] = jnp.zeros((SUBLANES, u.shape[1]), F32)
        sbuf[pl.ds(SUBLANES, ts), :] = u
        s1buf[...] = jnp.zeros_like(s1buf)
        s2buf[...] = jnp.zeros_like(s2buf)
        for b in range(nb_s):
            s1buf[pl.ds(b * l_s, 1), :] = st_ref[b, pl.ds(1, 1), :]
            s2buf[pl.ds(b * l_s, 1), :] = st_ref[b, pl.ds(0, 1), :]
            s2buf[pl.ds(b * l_s + 1, 1), :] = st_ref[b, pl.ds(1, 1), :]
        rmod = lax.broadcasted_iota(jnp.int32, (ts, 1), 0) % l_s
        prev1 = jnp.where(rmod == 0, s1buf[...], sbuf[pl.ds(SUBLANES - 1, ts), :])
        prev2 = jnp.where(rmod < 2, s2buf[...], sbuf[pl.ds(SUBLANES - 2, ts), :])
        v = prev2 * cw[0:1, :] + prev1 * cw[1:2, :] + u * cw[2:3, :]
        os_ref[...] = (_dot_nt(x, wbf[0]) * v).astype(BF16)
        for b in range(nb_s):
            cas_ref[b] = sbuf[pl.ds(SUBLANES + (b + 1) * l_s - 2, 2), :]

    @pl.when(i % tiles_per_batch == 0)
    def _():
        ubuf[pl.ds(0, SUBLANES), :] = jnp.zeros((SUBLANES, ubuf.shape[1]), F32)

    x = xp_ref[...]
    u = _dot_nt(x, wbf[1]) * _dot_nt(x, wbf[2])
    ubuf[pl.ds(SUBLANES, tm), :] = u
    v = (ubuf[pl.ds(SUBLANES - 2, tm), :] * cw[0:1, :]
         + ubuf[pl.ds(SUBLANES - 1, tm), :] * cw[1:2, :] + u * cw[2:3, :])
    op_ref[...] = (_dot_nt(x, wbf[0]) * v).astype(BF16)
    ubuf[pl.ds(0, SUBLANES), :] = ubuf[pl.ds(tm, SUBLANES), :]

    @pl.when(i % tiles_per_batch == tiles_per_batch - 1)
    def _():
        cap_ref[0] = ubuf[pl.ds(SUBLANES + tm - 2, 2), :]


def _proj_a(xnp, xns, w_in, conv_w, state_s, n_batch_p, seq_p, d_conv):
    tp, d = xnp.shape
    ts = xns.shape[0]
    nb_s = state_s.shape[0]
    l_s = ts // nb_s
    tn = 512
    tm = _pick(seq_p, (1024, 512, 256, 128))
    tpb = seq_p // tm
    ncol = d_conv // tn
    kern = functools.partial(_proj_a_kernel, tiles_per_batch=tpb, nb_s=nb_s, l_s=l_s)
    return pl.pallas_call(
        kern,
        grid=(ncol, tp // tm),
        in_specs=[
            pl.BlockSpec((tm, d), lambda j, i: (i, 0)),
            pl.BlockSpec((ts, d), lambda j, i: (0, 0)),
            pl.BlockSpec((tn, d), lambda j, i: (j, 0)),
            pl.BlockSpec((tn, d), lambda j, i: (j + ncol, 0)),
            pl.BlockSpec((tn, d), lambda j, i: (j + 2 * ncol, 0)),
            pl.BlockSpec((CONV_A_K, tn), lambda j, i: (0, j)),
            pl.BlockSpec((nb_s, CONV_A_K - 1, tn), lambda j, i: (0, 0, j)),
        ],
        out_specs=[
            pl.BlockSpec((tm, tn), lambda j, i: (i, j)),
            pl.BlockSpec((ts, tn), lambda j, i: (0, j)),
            pl.BlockSpec((1, CONV_A_K - 1, tn), lambda j, i: (i // tpb, 0, j)),
            pl.BlockSpec((nb_s, CONV_A_K - 1, tn), lambda j, i: (0, 0, j)),
        ],
        out_shape=[
            jax.ShapeDtypeStruct((tp, d_conv), BF16),
            jax.ShapeDtypeStruct((ts, d_conv), BF16),
            jax.ShapeDtypeStruct((n_batch_p, CONV_A_K - 1, d_conv), F32),
            jax.ShapeDtypeStruct((nb_s, CONV_A_K - 1, d_conv), F32),
        ],
        scratch_shapes=[
            pltpu.VMEM((3, tn, d), BF16),
            pltpu.VMEM((SUBLANES + tm, tn), F32),
            pltpu.VMEM((SUBLANES + ts, tn), F32),
            pltpu.VMEM((ts, tn), F32),
            pltpu.VMEM((ts, tn), F32),
        ],
        compiler_params=_params(("arbitrary", "arbitrary")),
        name="proj_a",
    )(xnp, xns, w_in, w_in, w_in, conv_w, state_s)


def _proj_raw_kernel(xp_ref, xs_ref, w_hbm, op_ref, os_ref, wbf, wstage, wsem, *, nsplit, width, row0):
    j = pl.program_id(0)
    tn = wbf.shape[0]

    def w_copy(jj, slot):
        rows = pl.ds(pl.multiple_of(row0 + jj * tn, SUBLANES), tn)
        return pltpu.make_async_copy(w_hbm.at[rows], wstage.at[slot], wsem.at[slot])

    def emit(x_ref, o_ref):
        acc = _dot_nt(x_ref[...], wbf[...])
        if nsplit == 0:
            o_ref[...] = acc
        else:
            for s in range(nsplit):
                o_ref[s] = acc[:, s * width:(s + 1) * width]

    @pl.when(pl.program_id(1) == 0)
    def _():
        @pl.when(j == 0)
        def _():
            w_copy(0, 0).start()

        w_copy(j, j % 2).wait()

        @pl.when(j + 1 < pl.num_programs(0))
        def _():
            w_copy(j + 1, (j + 1) % 2).start()

        wbf[...] = wstage[j % 2].astype(BF16)
        emit(xs_ref, os_ref)

    emit(xp_ref, op_ref)


def _proj_raw(xnp, xns, w, col0, ncols, width):
    tp, d = xnp.shape
    ts = xns.shape[0]
    tn = 1024
    tm = _pick(tp, (1024, 512, 256, 128))
    assert col0 % SUBLANES == 0 and ncols % tn == 0
    if width == 0:
        nsplit = 0
        out_specs = [pl.BlockSpec((tm, tn), lambda j, i: (i, j)),
                     pl.BlockSpec((ts, tn), lambda j, i: (0, j))]
        out_shape = [jax.ShapeDtypeStruct((tp, ncols), F32), jax.ShapeDtypeStruct((ts, ncols), F32)]
    else:
        nsplit = tn // width
        out_specs = [pl.BlockSpec((nsplit, tm, width), lambda j, i: (j, i, 0)),
                     pl.BlockSpec((nsplit, ts, width), lambda j, i: (j, 0, 0))]
        out_shape = [jax.ShapeDtypeStruct((ncols // width, tp, width), F32),
                     jax.ShapeDtypeStruct((ncols // width, ts, width), F32)]
    kern = functools.partial(_proj_raw_kernel, nsplit=nsplit, width=width, row0=col0)
    return pl.pallas_call(
        kern,
        grid=(ncols // tn, tp // tm),
        in_specs=[
            pl.BlockSpec((tm, d), lambda j, i: (i, 0)),
            pl.BlockSpec((ts, d), lambda j, i: (0, 0)),
            pl.BlockSpec(memory_space=pl.ANY),
        ],
        out_specs=out_specs,
        out_shape=out_shape,
        scratch_shapes=[pltpu.VMEM((tn, d), BF16), pltpu.VMEM((2, tn, d), F32), pltpu.SemaphoreType.DMA((2,))],
        compiler_params=_params(("arbitrary", "arbitrary")),
        name=f"proj_raw_{col0}",
    )(xnp, xns, w)


def _proj_conv_kernel(xp_ref, xs_ref, w_ref, cw_ref, cb_ref, st_ref, op_ref, os_ref, cmp_ref, cms_ref,
                      wbf, ubuf, sbuf, fix, *, tiles_per_batch, nb_s, l_s, nsplit, width):
    i = pl.program_id(1)
    tm = xp_ref.shape[0]
    ts = xs_ref.shape[0]
    tail = CONV_M_K - 1
    cw = cw_ref[...]
    bias = cb_ref[...]

    def store(o_ref, act):
        for s in range(nsplit):
            o_ref[s] = act[:, s * width:(s + 1) * width]

    @pl.when(i == 0)
    def _():
        wbf[...] = w_ref[...].astype(BF16)
        raw = _dot_nt(xs_ref[...], wbf[...])
        sbuf[pl.ds(0, SUBLANES), :] = jnp.zeros((SUBLANES, raw.shape[1]), F32)
        sbuf[pl.ds(SUBLANES, ts), :] = raw
        fix[...] = jnp.zeros_like(fix)
        for b in range(nb_s):
            for back in range(1, tail + 1):
                for m in range(back):
                    fix[back - 1, pl.ds(b * l_s + m, 1), :] = st_ref[b, pl.ds(tail + m - back, 1), :]
        rmod = lax.broadcasted_iota(jnp.int32, (ts, 1), 0) % l_s
        acc = raw * cw[tail:tail + 1, :]
        for back in range(1, tail + 1):
            tap = jnp.where(rmod < back, fix[back - 1], sbuf[pl.ds(SUBLANES - back, ts), :])
            acc = acc + tap * cw[tail - back:tail - back + 1, :]
        store(os_ref, jax.nn.silu(acc + bias))
        for b in range(nb_s):
            cms_ref[b] = sbuf[pl.ds(SUBLANES + (b + 1) * l_s - tail, tail), :]

    @pl.when(i % tiles_per_batch == 0)
    def _():
        ubuf[pl.ds(0, SUBLANES), :] = jnp.zeros((SUBLANES, ubuf.shape[1]), F32)

    raw = _dot_nt(xp_ref[...], wbf[...])
    ubuf[pl.ds(SUBLANES, tm), :] = raw
    acc = raw * cw[tail:tail + 1, :]
    for back in range(1, tail + 1):
        acc = acc + ubuf[pl.ds(SUBLANES - back, tm), :] * cw[tail - back:tail - back + 1, :]
    store(op_ref, jax.nn.silu(acc + bias))
    ubuf[pl.ds(0, SUBLANES), :] = ubuf[pl.ds(tm, SUBLANES), :]

    @pl.when(i % tiles_per_batch == tiles_per_batch - 1)
    def _():
        cmp_ref[0] = ubuf[pl.ds(SUBLANES + tm - tail, tail), :]


def _proj_conv(xnp, xns, w, conv_w, conv_b, state_s, col0, ch0, ncols, width, n_batch_p, seq_p):
    tp, d = xnp.shape
    ts = xns.shape[0]
    nb_s = state_s.shape[0]
    l_s = ts // nb_s
    tail = CONV_M_K - 1
    tn = 1024
    tm = _pick(seq_p, (1024, 512, 256, 128))
    tpb = seq_p // tm
    assert col0 % tn == 0 and ch0 % tn == 0 and ncols % tn == 0 and l_s >= tail
    jb, cb0 = col0 // tn, ch0 // tn
    nsplit = tn // width
    kern = functools.partial(_proj_conv_kernel, tiles_per_batch=tpb, nb_s=nb_s, l_s=l_s,
                             nsplit=nsplit, width=width)
    return pl.pallas_call(
        kern,
        grid=(ncols // tn, tp // tm),
        in_specs=[
            pl.BlockSpec((tm, d), lambda j, i: (i, 0)),
            pl.BlockSpec((ts, d), lambda j, i: (0, 0)),
            pl.BlockSpec((tn, d), lambda j, i: (j + jb, 0)),
            pl.BlockSpec((CONV_M_K, tn), lambda j, i: (0, j + cb0)),
            pl.BlockSpec((1, tn), lambda j, i: (0, j + cb0)),
            pl.BlockSpec((nb_s, tail, tn), lambda j, i: (0, 0, j + cb0)),
        ],
        out_specs=[
            pl.BlockSpec((nsplit, tm, width), lambda j, i: (j, i, 0)),
            pl.BlockSpec((nsplit, ts, width), lambda j, i: (j, 0, 0)),
            pl.BlockSpec((1, tail, tn), lambda j, i: (i // tpb, 0, j)),
            pl.BlockSpec((nb_s, tail, tn), lambda j, i: (0, 0, j)),
        ],
        out_shape=[
            jax.ShapeDtypeStruct((ncols // width, tp, width), F32),
            jax.ShapeDtypeStruct((ncols // width, ts, width), F32),
            jax.ShapeDtypeStruct((n_batch_p, tail, ncols), F32),
            jax.ShapeDtypeStruct((nb_s, tail, ncols), F32),
        ],
        scratch_shapes=[
            pltpu.VMEM((tn, d), BF16),
            pltpu.VMEM((SUBLANES + tm, tn), F32),
            pltpu.VMEM((SUBLANES + ts, tn), F32),
            pltpu.VMEM((tail, ts, tn), F32),
        ],
        compiler_params=_params(("arbitrary", "arbitrary")),
        name=f"proj_conv_{col0}",
    )(xnp, xns, w, conv_w, conv_b, state_s)


def _ssd_kernel(*refs, q, has_state, nchunks):
    (z_ref, xs_ref, b_ref, c_ref, dt_ref, dtb_ref, alog_ref, dsk_ref, gn_ref, *rest) = refs
    if has_state:
        hprev, *rest = rest
    (y_ref, oh, h_s, acg, rowt) = rest
    c = pl.program_id(1)
    nheads = N_GROUPS * HEADS_PER_GROUP

    @pl.when(c == 0)
    def _init():
        if has_state:
            h_s[...] = hprev[...]
        else:
            h_s[...] = jnp.zeros_like(h_s)

    dt = _softplus(dt_ref[...] + dtb_ref[...])
    da = dt * (-jnp.exp(alog_ref[...]))
    ri = lax.broadcasted_iota(jnp.int32, (q, q), 0)
    ci = lax.broadcasted_iota(jnp.int32, (q, q), 1)
    causal = ri >= ci
    tril = jnp.where(causal, 1.0, 0.0).astype(BF16)
    acum = sum(_dot(tril, p) for p in _split3(da))
    eye = jnp.where(lax.broadcasted_iota(jnp.int32, (nheads, nheads), 0)
                    == lax.broadcasted_iota(jnp.int32, (nheads, nheads), 1), 1.0, 0.0).astype(BF16)
    rowt[0] = sum(lax.dot_general(eye, p, NT_DIMS, preferred_element_type=F32) for p in _split3(acum))
    rowt[1] = sum(lax.dot_general(eye, p, NT_DIMS, preferred_element_type=F32) for p in _split3(dt))
    wend = jnp.exp(acum[q - 1:q, :] - acum) * dt
    rowt[2] = sum(lax.dot_general(eye, p, NT_DIMS, preferred_element_type=F32) for p in _split3(wend))
    for g in range(N_GROUPS):
        acg[g] = acum[:, g * HEADS_PER_GROUP:(g + 1) * HEADS_PER_GROUP]

    lane = lax.broadcasted_iota(jnp.int32, (1, LANES), 1)
    rowi = lax.broadcasted_iota(jnp.int32, (LANES, 1), 0)
    half_w = LANES // 2

    def group_body(g, carry):
        xs = xs_ref[g]
        bb = b_ref[g].astype(BF16)
        ccb16 = c_ref[g].astype(BF16)
        cb_ = lax.dot_general(ccb16, bb, NT_DIMS, preferred_element_type=F32)
        ac8 = acg[g]
        dsk = dsk_ref[g]
        lo_half = lane < half_w
        lo_rows = rowi < half_w
        ys = []
        for pair in range(HEADS_PER_GROUP // 2):
            sl = slice(pair * LANES, (pair + 1) * LANES)
            xp = xs[:, sl]
            hp = h_s[g, pl.ds(pair * LANES, LANES), :]
            ms_, acols, wrows = [], [], []
            for r in (2 * pair, 2 * pair + 1):
                head = g * HEADS_PER_GROUP + r
                acol = jnp.broadcast_to(ac8[:, r:r + 1], (q, LANES))
                arow = rowt[0, pl.ds(head, 1), :]
                drow = rowt[1, pl.ds(head, 1), :]
                decay = jnp.exp(jnp.where(causal, acol[:, :q] - arow, -jnp.inf))
                ms_.append((cb_ * decay * drow).astype(BF16))
                acols.append(acol)
                wrows.append(jnp.broadcast_to(rowt[2, pl.ds(head, 1), :], (half_w, q)))
            xlo = jnp.where(lo_half, xp, 0.0).astype(BF16)
            xhi = jnp.where(lo_half, 0.0, xp).astype(BF16)
            if q % LANES == 0:
                ydiag = _dot(jnp.concatenate(ms_, axis=1), jnp.concatenate([xlo, xhi], axis=0))
            else:
                ydiag = _dot(ms_[0], xlo) + _dot(ms_[1], xhi)
            ea = jnp.where(lo_half, jnp.exp(acols[0]), jnp.exp(acols[1]))
            yoff = ea * lax.dot_general(ccb16, hp.astype(BF16), NT_DIMS, preferred_element_type=F32)
            ys.append(dsk[:, sl] * xp + ydiag + yoff)
            xwt = (xp.T * jnp.concatenate(wrows, axis=0)).astype(BF16)
            dlast = [jnp.broadcast_to(jnp.exp(a[q - 1:q, :]), (LANES, LANES)) for a in acols]
            h_s[g, pl.ds(pair * LANES, LANES), :] = hp * jnp.where(lo_rows, dlast[0], dlast[1]) + _dot(xwt, bb)
        yg = jnp.concatenate(ys, axis=1)
        hh = yg * jax.nn.silu(z_ref[g])
        ms = jnp.mean(hh * hh, axis=-1, keepdims=True)
        y_ref[g] = (hh * lax.rsqrt(ms + EPS) * gn_ref[g]).astype(BF16)
        return carry

    lax.fori_loop(0, N_GROUPS, group_body, 0)

    @pl.when(c == nchunks - 1)
    def _():
        oh[...] = h_s[...]


def _ssd(z, xc, bc, dt_raw, prm, n_batch, seq, q, h_prev=None):
    nchunks = seq // q
    nheads = N_GROUPS * HEADS_PER_GROUP
    has_state = h_prev is not None
    g8 = N_GROUPS

    def tok(first):
        return lambda b, c: (first, b * nchunks + c, 0)

    def const3(b, c):
        return (0, 0, 0)

    in_specs = [
        pl.BlockSpec((g8, q, GROUP_W), tok(0)),
        pl.BlockSpec((g8, q, GROUP_W), tok(0)),
        pl.BlockSpec((g8, q, D_STATE), tok(0)),
        pl.BlockSpec((g8, q, D_STATE), tok(1)),
        pl.BlockSpec((q, nheads), lambda b, c: (b * nchunks + c, 0)),
        pl.BlockSpec((1, nheads), lambda b, c: (0, 0)),
        pl.BlockSpec((1, nheads), lambda b, c: (0, 0)),
        pl.BlockSpec((g8, 1, GROUP_W), const3),
        pl.BlockSpec((g8, 1, GROUP_W), const3),
    ]
    args = [z, xc, bc, bc, dt_raw, prm["dtb"], prm["alog"], prm["dsk"], prm["gn"]]

    def per_batch(shape):
        return pl.BlockSpec((None,) + shape, lambda b, c: (b,) + (0,) * len(shape))

    if has_state:
        in_specs.append(per_batch((g8, GROUP_W, D_STATE)))
        args.append(h_prev)
    t = n_batch * seq
    out_specs = [
        pl.BlockSpec((g8, q, GROUP_W), lambda b, c: (0, b * nchunks + c, 0)),
        per_batch((g8, GROUP_W, D_STATE)),
    ]
    out_shape = [
        jax.ShapeDtypeStruct((g8, t, GROUP_W), BF16),
        jax.ShapeDtypeStruct((n_batch, g8, GROUP_W, D_STATE), F32),
    ]
    scratch = [
        pltpu.VMEM((g8, GROUP_W, D_STATE), F32),
        pltpu.VMEM((g8, q, HEADS_PER_GROUP), F32),
        pltpu.VMEM((3, nheads, q), F32),
    ]
    kern = functools.partial(_ssd_kernel, q=q, has_state=has_state, nchunks=nchunks)
    return pl.pallas_call(
        kern,
        grid=(n_batch, nchunks),
        in_specs=in_specs,
        out_specs=out_specs,
        out_shape=out_shape,
        scratch_shapes=scratch,
        compiler_params=_params(("arbitrary", "arbitrary")),
        name="ssd_state" if has_state else "ssd",
    )(*args)


def _merge_kernel(ap_ref, as_ref, yp_ref, ys_ref, gap_ref, gmp_ref, gas_ref, gms_ref, wa_ref, wm_ref,
                  op_ref, os_ref, wab, wmb):
    def emit(a_ref, y_ref, ga_ref, gm_ref, o_ref):
        oa = _dot(a_ref[...], wab[...])
        om = _dot(y_ref[0], wmb[0])
        for g in range(1, N_GROUPS):
            om = om + _dot(y_ref[g], wmb[g])
        o_ref[...] = (jax.nn.sigmoid(ga_ref[...]) * oa + jax.nn.sigmoid(gm_ref[...]) * om).astype(BF16)

    @pl.when(pl.program_id(1) == 0)
    def _():
        wab[...] = wa_ref[...].astype(BF16)
        wmb[...] = wm_ref[...].astype(BF16)
        emit(as_ref, ys_ref, gas_ref, gms_ref, os_ref)

    emit(ap_ref, yp_ref, gap_ref, gmp_ref, op_ref)


def _merge(abv_p, abv_s, y_p, y_s, g_p, g_s, w_a_out, w_m_out3):
    tp, dc = abv_p.shape
    ts = abv_s.shape[0]
    dm = w_a_out.shape[1]
    tn = 512
    tm = _pick(tp, (512, 256, 128))
    ncol = dm // tn
    return pl.pallas_call(
        _merge_kernel,
        grid=(ncol, tp // tm),
        in_specs=[
            pl.BlockSpec((tm, dc), lambda j, i: (i, 0)),
            pl.BlockSpec((ts, dc), lambda j, i: (0, 0)),
            pl.BlockSpec((N_GROUPS, tm, GROUP_W), lambda j, i: (0, i, 0)),
            pl.BlockSpec((N_GROUPS, ts, GROUP_W), lambda j, i: (0, 0, 0)),
            pl.BlockSpec((tm, tn), lambda j, i: (i, j)),
            pl.BlockSpec((tm, tn), lambda j, i: (i, j + ncol)),
            pl.BlockSpec((ts, tn), lambda j, i: (0, j)),
            pl.BlockSpec((ts, tn), lambda j, i: (0, j + ncol)),
            pl.BlockSpec((dc, tn), lambda j, i: (0, j)),
            pl.BlockSpec((N_GROUPS, GROUP_W, tn), lambda j, i: (0, 0, j)),
        ],
        out_specs=[
            pl.BlockSpec((tm, tn), lambda j, i: (i, j)),
            pl.BlockSpec((ts, tn), lambda j, i: (0, j)),
        ],
        out_shape=[jax.ShapeDtypeStruct((tp, dm), BF16), jax.ShapeDtypeStruct((ts, dm), BF16)],
        scratch_shapes=[pltpu.VMEM((dc, tn), BF16), pltpu.VMEM((N_GROUPS, GROUP_W, tn), BF16)],
        compiler_params=_params(("arbitrary", "arbitrary")),
        name="merge",
    )(abv_p, abv_s, y_p, y_s, g_p, g_p, g_s, g_s, w_a_out, w_m_out3)


def _res_kernel(ap_ref, as_ref, rp_ref, rs_ref, w_ref, g_ref, *outs, n_prompt_tiles, merged_out, ts):
    if merged_out:
        h_ref, hn_ref, wbf = outs
    else:
        hp_ref, hs_ref, hnp_ref, hns_ref, wbf = outs
    i = pl.program_id(0)

    @pl.when(i == 0)
    def _():
        wbf[...] = w_ref[...].astype(BF16)

    def emit(a_ref, r_ref, store_h, store_hn):
        h = r_ref[...] + _dot(a_ref[...], wbf[...])
        store_h(h)
        store_hn(_rms(h, g_ref[...]))

    if merged_out:
        @pl.when(i < n_prompt_tiles)
        def _():
            def sh(h):
                h_ref[...] = h

            def shn(hn):
                hn_ref[...] = hn

            emit(ap_ref, rp_ref, sh, shn)

        @pl.when(i == n_prompt_tiles)
        def _():
            def sh(h):
                h_ref[pl.ds(0, ts), :] = h

            def shn(hn):
                hn_ref[pl.ds(0, ts), :] = hn

            emit(as_ref, rs_ref, sh, shn)
    else:
        def shp(h):
            hp_ref[...] = h

        def shnp(hn):
            hnp_ref[...] = hn.astype(hnp_ref.dtype)

        emit(ap_ref, rp_ref, shp, shnp)

        @pl.when(i == 0)
        def _():
            def shs(h):
                hs_ref[...] = h

            def shns(hn):
                hns_ref[...] = hn.astype(hns_ref.dtype)

            emit(as_ref, rs_ref, shs, shns)


def _res(a_p, a_s, r_p, r_s, w, g, merged_out, hn_dtype):
    tp, d = a_p.shape
    ts = a_s.shape[0]
    tm = 256
    assert tp % tm == 0 and ts <= tm
    npt = tp // tm
    last = npt - 1
    in_specs = [
        pl.BlockSpec((tm, d), lambda i: (jnp.minimum(i, last), 0)),
        pl.BlockSpec((ts, d), lambda i: (0, 0)),
        pl.BlockSpec((tm, d), lambda i: (jnp.minimum(i, last), 0)),
        pl.BlockSpec((ts, d), lambda i: (0, 0)),
        pl.BlockSpec((d, d), lambda i: (0, 0), pipeline_mode=pl.Buffered(1)),
        pl.BlockSpec((1, d), lambda i: (0, 0)),
    ]
    if merged_out:
        grid = (npt + 1,)
        out_specs = [pl.BlockSpec((tm, d), lambda i: (i, 0)), pl.BlockSpec((tm, d), lambda i: (i, 0))]
        out_shape = [jax.ShapeDtypeStruct((tp + ts, d), F32), jax.ShapeDtypeStruct((tp + ts, d), hn_dtype)]
    else:
        grid = (npt,)
        out_specs = [pl.BlockSpec((tm, d), lambda i: (i, 0)), pl.BlockSpec((ts, d), lambda i: (0, 0)),
                     pl.BlockSpec((tm, d), lambda i: (i, 0)), pl.BlockSpec((ts, d), lambda i: (0, 0))]
        out_shape = [jax.ShapeDtypeStruct((tp, d), F32), jax.ShapeDtypeStruct((ts, d), F32),
                     jax.ShapeDtypeStruct((tp, d), hn_dtype), jax.ShapeDtypeStruct((ts, d), hn_dtype)]
    kern = functools.partial(_res_kernel, n_prompt_tiles=npt, merged_out=merged_out, ts=ts)
    return pl.pallas_call(
        kern,
        grid=grid,
        in_specs=in_specs,
        out_specs=out_specs,
        out_shape=out_shape,
        scratch_shapes=[pltpu.VMEM((d, d), BF16)],
        compiler_params=_params(("arbitrary",)),
        name="res_merged" if merged_out else "res",
    )(a_p, a_s, r_p, r_s, w, g)


def _kv_kernel(m_ref, g_ref, wk_ref, wv_ref, k_hbm, v_hbm, mn, kvb, sem, *, n_batch):
    j = pl.program_id(0)

    @pl.when(j == 0)
    def _():
        mn[...] = _rms(m_ref[...], g_ref[...]).astype(BF16)

    kvb[0] = _dot(mn[...], wk_ref[...].astype(BF16))
    kvb[1] = _dot(mn[...], wv_ref[...].astype(BF16))
    nm = kvb.shape[1] // n_batch
    for h in range(N_XHEADS):
        @pl.when(j == h)
        def _(h=h):
            copies = [pltpu.make_async_copy(kvb.at[t, pl.ds(b * nm, nm)], dst.at[b, :, h, :], sem.at[t, b])
                      for t, dst in enumerate((k_hbm, v_hbm)) for b in range(n_batch)]
            for c in copies:
                c.start()
            for c in copies:
                c.wait()


def _memory_kv(mem2d, g, w_k, w_v, n_batch):
    m, d = mem2d.shape
    dh = d // N_XHEADS
    out_sds = jax.ShapeDtypeStruct((n_batch, m // n_batch, N_XHEADS, dh), F32)
    kern = functools.partial(_kv_kernel, n_batch=n_batch)
    return pl.pallas_call(
        kern,
        grid=(N_XHEADS,),
        in_specs=[
            pl.BlockSpec((m, d), lambda j: (0, 0)),
            pl.BlockSpec((1, d), lambda j: (0, 0)),
            pl.BlockSpec((d, dh), lambda j: (0, j)),
            pl.BlockSpec((d, dh), lambda j: (0, j)),
        ],
        out_specs=[pl.BlockSpec(memory_space=pl.ANY), pl.BlockSpec(memory_space=pl.ANY)],
        out_shape=[out_sds, out_sds],
        scratch_shapes=[pltpu.VMEM((m, d), BF16), pltpu.VMEM((2, m, dh), F32),
                        pltpu.SemaphoreType.DMA((2, n_batch))],
        compiler_params=_params(("arbitrary",)),
        name="memory_kv",
    )(mem2d, g, w_k, w_v)


def _attn_kernel(hn_ref, k_ref, v_ref, wq_ref, o_ref, wqb, kb, vb, kvf, sem):
    b = pl.program_id(0)
    i = pl.program_id(1)

    @pl.when((b == 0) & (i == 0))
    def _():
        wqb[...] = wq_ref[...].astype(BF16)

    d = wqb.shape[1]
    dh = d // N_XHEADS

    @pl.when(i == 0)
    def _():
        copies = [pltpu.make_async_copy(src.at[b, :, h, :], kvf.at[t, h], sem.at[t, h])
                  for t, src in enumerate((k_ref, v_ref)) for h in range(N_XHEADS)]
        for c in copies:
            c.start()
        for c in copies:
            c.wait()
        for h in range(N_XHEADS):
            kb[:, h * dh:(h + 1) * dh] = kvf[0, h].astype(BF16)
            vb[:, h * dh:(h + 1) * dh] = kvf[1, h].astype(BF16)

    q = _dot(hn_ref[...], wqb[...])
    outs = []
    for h in range(N_XHEADS):
        sl = slice(h * dh, (h + 1) * dh)
        s = lax.dot_general(q[:, sl].astype(BF16), kb[:, sl], NT_DIMS, preferred_element_type=F32)
        s = s * (dh ** -0.5)
        e = jnp.exp(s - jnp.max(s, axis=-1, keepdims=True))
        p = e / jnp.sum(e, axis=-1, keepdims=True)
        outs.append(_dot(p.astype(BF16), vb[:, sl]))
    o_ref[...] = jnp.concatenate(outs, axis=1).astype(BF16)


def _attn(hn, k, v, w_q, n_batch, seq):
    t, d = hn.shape
    nm, nh, dh = k.shape[1:]
    tm = _pick(seq, (512, 256, 128, 64, 32, 16))
    tpb = seq // tm
    kv_spec = pl.BlockSpec(memory_space=pl.ANY)
    scratch = [pltpu.VMEM((d, d), BF16), pltpu.VMEM((nm, d), BF16), pltpu.VMEM((nm, d), BF16),
               pltpu.VMEM((2, nh, nm, dh), F32), pltpu.SemaphoreType.DMA((2, nh))]
    return pl.pallas_call(
        _attn_kernel,
        grid=(n_batch, tpb),
        in_specs=[
            pl.BlockSpec((tm, d), lambda b, i: (b * tpb + i, 0)),
            kv_spec,
            kv_spec,
            pl.BlockSpec((d, d), lambda b, i: (0, 0), pipeline_mode=pl.Buffered(1)),
        ],
        out_specs=pl.BlockSpec((tm, d), lambda b, i: (b * tpb + i, 0)),
        out_shape=jax.ShapeDtypeStruct((t, d), BF16),
        scratch_shapes=scratch,
        compiler_params=_params(("arbitrary", "arbitrary")),
        name=f"attn_{seq}",
    )(hn, k, v, w_q)


def _router_kernel(x_ref, w_ref, b_ref, ri_ref, rw_ref):
    logits = _dot(x_ref[...].astype(BF16), w_ref[...].astype(BF16)) + b_ref[...]
    lane_i = lax.broadcasted_iota(jnp.int32, logits.shape, 1)
    lane = lane_i.astype(F32)
    ninf = -jnp.inf
    big = float(LANES)
    is_g = lane < N_EXPERT_GROUPS
    gl = jnp.where(is_g, logits, ninf)
    gmax = jnp.max(gl, axis=-1, keepdims=True)
    gsel = jnp.min(jnp.where(gl == gmax, lane, big), axis=-1, keepdims=True)
    pg = 1.0 / jnp.sum(jnp.where(is_g, jnp.exp(gl - gmax), 0.0), axis=-1, keepdims=True)
    lo = N_EXPERT_GROUPS + EXPERTS_PER_GROUP * gsel
    el = jnp.where(lane >= lo, jnp.where(lane < lo + EXPERTS_PER_GROUP, logits, ninf), ninf)
    m1 = jnp.max(el, axis=-1, keepdims=True)
    i1 = jnp.min(jnp.where(el == m1, lane, big), axis=-1, keepdims=True)
    el2 = jnp.where(lane == i1, ninf, el)
    m2 = jnp.max(el2, axis=-1, keepdims=True)
    i2 = jnp.min(jnp.where(el2 == m2, lane, big), axis=-1, keepdims=True)
    e = jnp.exp(m2 - m1)
    w1 = pg / (1.0 + e)
    w2 = pg * e / (1.0 + e)
    ri_ref[...] = jnp.where(lane_i == 0, i1 - N_EXPERT_GROUPS,
                            jnp.where(lane_i == 1, i2 - N_EXPERT_GROUPS, 0.0)).astype(jnp.int32)
    rw_ref[...] = jnp.where(lane_i == 0, w1, jnp.where(lane_i == 1, w2, 0.0))


def _router(hn_all, w_r, b_r):
    t, d = hn_all.shape
    tm = _pick(t, (640, 512, 384, 256, 128))
    return pl.pallas_call(
        _router_kernel,
        grid=(t // tm,),
        in_specs=[
            pl.BlockSpec((tm, d), lambda i: (i, 0)),
            pl.BlockSpec((d, LANES), lambda i: (0, 0)),
            pl.BlockSpec((1, LANES), lambda i: (0, 0)),
        ],
        out_specs=[pl.BlockSpec((tm, LANES), lambda i: (i, 0)), pl.BlockSpec((tm, LANES), lambda i: (i, 0))],
        out_shape=[jax.ShapeDtypeStruct((t, LANES), jnp.int32), jax.ShapeDtypeStruct((t, LANES), F32)],
        compiler_params=_params(("arbitrary",)),
        name="router",
    )(hn_all, w_r, b_r)


def _row_copy(src, dst, s, d, sem):
    return pltpu.make_async_copy(src.at[pl.ds(s, 1)], dst.at[pl.ds(d, 1)], sem)


def _dispatch_kernel(dest_ref, pad0_ref, padn_ref, nsub_ref, x_ref, o_ref, zrow, zblk, sem, zsem, csem, bsem, *,
                     chunk, n_blocks):
    base = pl.program_id(0) * chunk

    @pl.when(pl.program_id(0) == 0)
    def _():
        zrow[...] = jnp.zeros_like(zrow)
        zblk[...] = jnp.zeros_like(zblk)

        def pad_plan(e):
            p0 = pad0_ref[e]
            head = jnp.minimum(padn_ref[e], (SUBLANES - p0 % SUBLANES) % SUBLANES)
            return p0, head, (padn_ref[e] - head) // SUBLANES

        def group_copy(row):
            return pltpu.make_async_copy(zrow, o_ref.at[pl.ds(pl.multiple_of(row, SUBLANES), SUBLANES)], csem)

        def pad_start(e, carry):
            p0, head, ngroups = pad_plan(e)

            def start_row(r, c):
                _row_copy(zrow, o_ref, 0, p0 + r, zsem).start()
                return c

            def start_group(j, c):
                group_copy(p0 + head + j * SUBLANES).start()
                return c

            lax.fori_loop(0, head, start_row, 0)
            return lax.fori_loop(0, ngroups, start_group, carry)

        def pad_wait(e, carry):
            _, head, ngroups = pad_plan(e)

            def wait_row(r, c):
                _row_copy(zrow, o_ref, 0, 0, zsem).wait()
                return c

            def wait_group(j, c):
                group_copy(0).wait()
                return c

            lax.fori_loop(0, head, wait_row, 0)
            return lax.fori_loop(0, ngroups, wait_group, carry)

        lax.fori_loop(0, N_EXPERTS, pad_start, 0)

        def blk_copy(b):
            return pltpu.make_async_copy(zblk, o_ref.at[pl.ds(pl.multiple_of(b * EXPERT_SUB, EXPERT_SUB),
                                                             EXPERT_SUB)], bsem)

        def tail_start(b, c):
            blk_copy(b).start()
            return c

        def tail_wait(b, c):
            blk_copy(b).wait()
            return c

        lax.fori_loop(nsub_ref[0], n_blocks, tail_start, 0)
        lax.fori_loop(0, N_EXPERTS, pad_wait, 0)
        lax.fori_loop(nsub_ref[0], n_blocks, tail_wait, 0)

    def issue(r, carry):
        t = base + r
        _row_copy(x_ref, o_ref, r, dest_ref[2 * t], sem).start()
        _row_copy(x_ref, o_ref, r, dest_ref[2 * t + 1], sem).start()
        return carry

    lax.fori_loop(0, chunk, issue, 0, unroll=8)

    for _ in range(2):
        pltpu.make_async_copy(x_ref, o_ref.at[pl.ds(0, chunk)], sem).wait()


def _dispatch(tables, hn_all, n_blocks):
    dest, pad0, padn, nsub = tables
    t, d = hn_all.shape
    chunk = _pick(t, (640, 512, 384, 256, 128))
    kern = functools.partial(_dispatch_kernel, chunk=chunk, n_blocks=n_blocks)
    return pl.pallas_call(
        kern,
        grid_spec=pltpu.PrefetchScalarGridSpec(
            num_scalar_prefetch=4,
            grid=(t // chunk,),
            in_specs=[pl.BlockSpec((chunk, d), lambda i, *_: (i, 0))],
            out_specs=pl.BlockSpec(memory_space=pl.ANY),
            scratch_shapes=[pltpu.VMEM((SUBLANES, d), F32), pltpu.VMEM((EXPERT_SUB, d), F32),
                            pltpu.SemaphoreType.DMA(()), pltpu.SemaphoreType.DMA(()),
                            pltpu.SemaphoreType.DMA(()), pltpu.SemaphoreType.DMA(())],
        ),
        out_shape=jax.ShapeDtypeStruct((n_blocks * EXPERT_SUB, d), F32),
        compiler_params=_params(("arbitrary",)),
        name="dispatch",
    )(dest, pad0, padn, nsub, hn_all)


def _expert_kernel(bstart_ref, wg_ref, wu_ref, wd_ref, x_hbm, y_hbm, gcache, ucache, dcache, xbuf, ybuf, xb,
                   xsem, ysem, *, n_blocks):
    e = pl.program_id(0)
    k = pl.program_id(1)
    nk = gcache.shape[1]
    total = bstart_ref[N_EXPERTS]

    def x_copy(b, slot):
        rows = pl.ds(pl.multiple_of(b * EXPERT_SUB, EXPERT_SUB), EXPERT_SUB)
        return pltpu.make_async_copy(x_hbm.at[rows], xbuf.at[slot], xsem.at[slot])

    def y_copy(b, slot):
        rows = pl.ds(pl.multiple_of(b * EXPERT_SUB, EXPERT_SUB), EXPERT_SUB)
        return pltpu.make_async_copy(ybuf.at[slot], y_hbm.at[rows], ysem.at[slot])

    @pl.when(e < N_EXPERTS)
    def _():
        slot = e % 2
        gcache[slot, k] = wg_ref[...].astype(BF16)
        ucache[slot, k] = wu_ref[...].astype(BF16)
        dcache[slot, k] = wd_ref[...].astype(BF16)

    @pl.when((e == 0) & (k == 0) & (total > 0))
    def _():
        x_copy(0, 0).start()

    @pl.when(e >= 1)
    def _():
        owner = e - 1
        wslot = owner % 2
        b0 = bstart_ref[owner]
        n = bstart_ref[owner + 1] - b0

        cpu = 2
        upb = nk // cpu

        def unit(u, carry):
            b = b0 + u // upb
            part = u % upb
            slot = b % 2

            @pl.when(part == 0)
            def _():
                x_copy(b, slot).wait()

                @pl.when(b + 1 < total)
                def _():
                    x_copy(b + 1, 1 - slot).start()

                xb[...] = xbuf[slot].astype(BF16)

                @pl.when(b >= 2)
                def _():
                    y_copy(b - 2, slot).wait()

                ybuf[slot] = jnp.zeros(ybuf.shape[1:], F32)

            x = xb[...]
            acc = None
            for c in range(cpu):
                kk = part * cpu + c
                hid = (jax.nn.silu(_dot(x, gcache[wslot, kk])) * _dot(x, ucache[wslot, kk])).astype(BF16)
                y = _dot(hid, dcache[wslot, kk])
                acc = y if acc is None else acc + y
            ybuf[slot] += acc

            @pl.when(part == upb - 1)
            def _():
                y_copy(b, slot).start()

            return carry

        lax.fori_loop((n * upb * k) // nk, (n * upb * (k + 1)) // nk, unit, 0)

    @pl.when((e == N_EXPERTS) & (k == nk - 1))
    def _():
        @pl.when(total >= 2)
        def _():
            y_copy(total - 2, total % 2).wait()

        @pl.when(total >= 1)
        def _():
            y_copy(total - 1, (total - 1) % 2).wait()

        ybuf[0] = jnp.zeros(ybuf.shape[1:], F32)

        def zfill(b, carry):
            y_copy(b, 0).start()
            y_copy(b, 0).wait()
            return carry

        lax.fori_loop(total, n_blocks, zfill, 0)


def _experts(bstart, x_sorted, w_gate, w_up, w_down, kchunk):
    nrows, d = x_sorted.shape
    n_blocks = nrows // EXPERT_SUB
    de = w_gate.shape[2]
    nk = de // kchunk
    last = N_EXPERTS - 1

    def widx(e, k):
        return jnp.minimum(e, last), jnp.where(e <= last, k, nk - 1)

    def in_idx(e, k, b):
        ee, kk = widx(e, k)
        return (ee, 0, kk)

    def down_idx(e, k, b):
        ee, kk = widx(e, k)
        return (ee, kk, 0)

    kern = functools.partial(_expert_kernel, n_blocks=n_blocks)
    return pl.pallas_call(
        kern,
        grid_spec=pltpu.PrefetchScalarGridSpec(
            num_scalar_prefetch=1,
            grid=(N_EXPERTS + 1, nk),
            in_specs=[
                pl.BlockSpec((None, d, kchunk), in_idx),
                pl.BlockSpec((None, d, kchunk), in_idx),
                pl.BlockSpec((None, kchunk, d), down_idx),
                pl.BlockSpec(memory_space=pl.ANY),
            ],
            out_specs=pl.BlockSpec(memory_space=pl.ANY),
            scratch_shapes=[
                pltpu.VMEM((2, nk, d, kchunk), BF16),
                pltpu.VMEM((2, nk, d, kchunk), BF16),
                pltpu.VMEM((2, nk, kchunk, d), BF16),
                pltpu.VMEM((2, EXPERT_SUB, d), F32),
                pltpu.VMEM((2, EXPERT_SUB, d), F32),
                pltpu.VMEM((EXPERT_SUB, d), BF16),
                pltpu.SemaphoreType.DMA((2,)),
                pltpu.SemaphoreType.DMA((2,)),
            ],
        ),
        out_shape=jax.ShapeDtypeStruct((nrows, d), F32),
        compiler_params=_params(("arbitrary", "arbitrary")),
        name="experts",
    )(bstart, w_gate, w_up, w_down, x_sorted)


def _final_kernel(dest_ref, h_ref, rw_ref, ys_ref, g_ref, yp_ref, yss_ref, ya, yb, sem, *,
                  tm, n_prompt_tiles, ts):
    i = pl.program_id(0)

    def gather(tile, n):
        slot = tile % 2

        def issue(r, carry):
            t = tile * tm + r
            _row_copy(ys_ref, ya.at[slot], dest_ref[2 * t], r, sem.at[slot]).start()
            _row_copy(ys_ref, yb.at[slot], dest_ref[2 * t + 1], r, sem.at[slot]).start()
            return carry

        lax.fori_loop(0, n, issue, 0, unroll=8)

    def emit(n, o_ref):
        slot = i % 2
        for buf in (ya, yb):
            pltpu.make_async_copy(ys_ref.at[pl.ds(0, n)], buf.at[slot, pl.ds(0, n)], sem.at[slot]).wait()
        w = rw_ref[pl.ds(0, n), :]
        h = (h_ref[pl.ds(0, n), :] + w[:, 0:1] * ya[slot, pl.ds(0, n), :]
             + w[:, 1:2] * yb[slot, pl.ds(0, n), :])
        o_ref[...] = _rms(h, g_ref[...])

    @pl.when(i == 0)
    def _():
        gather(i, tm)

    @pl.when(i + 1 < n_prompt_tiles)
    def _():
        gather(i + 1, tm)

    @pl.when(i + 1 == n_prompt_tiles)
    def _():
        gather(i + 1, ts)

    @pl.when(i < n_prompt_tiles)
    def _():
        emit(tm, yp_ref)

    @pl.when(i == n_prompt_tiles)
    def _():
        emit(ts, yss_ref)


def _final(dest, h_all, rw, y_sorted, g, tp, ts):
    t, d = h_all.shape
    tm = 256
    assert tp % tm == 0 and ts <= tm and t == tp + ts
    npt = tp // tm
    kern = functools.partial(_final_kernel, tm=tm, n_prompt_tiles=npt, ts=ts)
    return pl.pallas_call(
        kern,
        grid_spec=pltpu.PrefetchScalarGridSpec(
            num_scalar_prefetch=1,
            grid=(npt + 1,),
            in_specs=[
                pl.BlockSpec((tm, d), lambda i, dref: (i, 0)),
                pl.BlockSpec((tm, LANES), lambda i, dref: (i, 0)),
                pl.BlockSpec(memory_space=pl.ANY),
                pl.BlockSpec((1, d), lambda i, dref: (0, 0)),
            ],
            out_specs=[
                pl.BlockSpec((tm, d), lambda i, dref: (jnp.minimum(i, npt - 1), 0)),
                pl.BlockSpec((ts, d), lambda i, dref: (0, 0)),
            ],
            scratch_shapes=[pltpu.VMEM((2, tm, d), F32), pltpu.VMEM((2, tm, d), F32),
                            pltpu.SemaphoreType.DMA((2,))],
        ),
        out_shape=[jax.ShapeDtypeStruct((tp, d), F32), jax.ShapeDtypeStruct((ts, d), F32)],
        compiler_params=_params(("arbitrary",)),
        name="final",
    )(dest, h_all, rw, y_sorted, g)


def _routing_tables(eid):
    i32 = jnp.int32
    onehot = (eid[:, None] == jnp.arange(N_EXPERTS, dtype=i32)[None, :]).astype(i32)
    csum = jnp.cumsum(onehot, axis=0)
    counts = csum[-1]
    nsub = (counts + EXPERT_SUB - 1) // EXPERT_SUB
    bend = jnp.cumsum(nsub)
    bstart = bend - nsub
    seg = bstart * EXPERT_SUB
    dest = jnp.sum(onehot * (csum - 1 + seg[None, :]), axis=1).astype(i32)
    pad0 = (seg + counts).astype(i32)
    padn = (nsub * EXPERT_SUB - counts).astype(i32)
    bstart_all = jnp.concatenate([bstart, bend[-1:]]).astype(i32)
    return (dest, pad0, padn, bend[-1:].astype(i32)), bstart_all


def _group_major(a, width):
    lead = a.shape[:-1]
    g = a.shape[-1] // width
    return jnp.moveaxis(a.reshape(lead + (g, width)), -2, 0)


def kernel(x_prompt, x_sample, cache_mem_k, cache_mem_v, state_conv_a, state_conv_m, state_ssm, mem_prompt,
           norm_mix, w_in, conv_a_w, w_a_out, conv_m_w, conv_m_b, dt_bias, a_log, d_skip, ssm_norm,
           w_m_out, w_o, norm_cross, norm_mem, w_q, w_k, w_v, w_co, norm_ffn, w_rg, b_rg, w_re, b_re,
           w_gate, w_up, w_down, norm_final):
    depth = w_in.shape[0]
    assert depth == 1, "single-layer step"
    nbp, seq_p, d = x_prompt.shape
    nbs, seq_s, _ = x_sample.shape
    tp, ts = nbp * seq_p, nbs * seq_s
    n_mem = mem_prompt.shape[1]
    d_conv = conv_a_w.shape[2]
    d_inner = w_m_out.shape[1]
    nheads = dt_bias.shape[1]
    bc_w = N_GROUPS * D_STATE
    l = 0

    xp = x_prompt.reshape(tp, d)
    xs = x_sample.reshape(ts, d)
    w_in_t = jnp.swapaxes(w_in[l], 0, 1)
    col_z = 3 * d_conv
    col_x = col_z + d_inner
    col_b = col_x + d_inner
    col_dt = col_b + 2 * bc_w
    col_g = col_dt + nheads
    w_dt = w_in_t[col_dt:col_g]

    mk4, mv4 = _memory_kv(mem_prompt.reshape(nbp * n_mem, d), norm_mem[l][None], w_k[l], w_v[l], nbp)

    xnp, xns, dt_p, dt_s = _norm_dt(xp, xs, norm_mix[l][None], w_dt)
    abv_p, abv_s, ca_p, ca_s = _proj_a(xnp, xns, w_in_t, conv_a_w[l], state_conv_a[l], nbp, seq_p, d_conv)
    z_p, z_s = _proj_raw(xnp, xns, w_in_t, col_z, d_inner, GROUP_W)
    cw, cbias, scm = conv_m_w[l], conv_m_b[l][None], state_conv_m[l]
    xc_p, xc_s, cmx_p, cmx_s = _proj_conv(xnp, xns, w_in_t, cw, cbias, scm, col_x, 0, d_inner, GROUP_W,
                                           nbp, seq_p)
    bc_p, bc_s, cmb_p, cmb_s = _proj_conv(xnp, xns, w_in_t, cw, cbias, scm, col_b, d_inner, 2 * bc_w, D_STATE,
                                           nbp, seq_p)
    g_p, g_s = _proj_raw(xnp, xns, w_in_t, col_g, 2 * d, 0)

    prm = {
        "dtb": dt_bias[l][None],
        "alog": a_log[l][None],
        "dsk": _group_major(jnp.repeat(d_skip[l], HEAD_DIM)[None], GROUP_W),
        "gn": _group_major(ssm_norm[l][None], GROUP_W),
    }
    q_p = _pick(seq_p, (128, 64, 32, 16, 8))
    y_p, h_p = _ssd(z_p, xc_p, bc_p, dt_p, prm, nbp, seq_p, q_p)
    y_s, h_s = _ssd(z_s, xc_s, bc_s, dt_s, prm, nbs, seq_s, seq_s,
                    h_prev=state_ssm[l].reshape(nbs, N_GROUPS, GROUP_W, D_STATE))

    merged_p, merged_s = _merge(abv_p, abv_s, y_p, y_s, g_p, g_s, w_a_out[l],
                                w_m_out[l].reshape(N_GROUPS, GROUP_W, d))
    h1_p, h1_s, hn1_p, hn1_s = _res(merged_p, merged_s, xp, xs, w_o[l], norm_cross[l][None], False, BF16)

    att_p = _attn(hn1_p, mk4, mv4, w_q[l], nbp, seq_p)
    att_s = _attn(hn1_s, cache_mem_k[l], cache_mem_v[l], w_q[l], nbs, seq_s)
    h2_all, hn2_all = _res(att_p, att_s, h1_p, h1_s, w_co[l], norm_ffn[l][None], True, F32)

    npad = LANES - N_EXPERT_GROUPS - N_EXPERTS
    w_r = jnp.concatenate([w_rg[l], w_re[l], jnp.zeros((d, npad), F32)], axis=1)
    b_r = jnp.concatenate([b_rg[l], b_re[l], jnp.zeros((npad,), F32)])[None]
    ri, rw = _router(hn2_all, w_r, b_r)
    tables, bstart = _routing_tables(ri[:, :2].reshape(-1))
    n_blocks = -(-2 * (tp + ts) // EXPERT_SUB) + N_EXPERTS
    x_sorted = _dispatch(tables, hn2_all, n_blocks)
    y_sorted = _experts(bstart, x_sorted, w_gate[l], w_up[l], w_down[l], 256)
    y_prompt, y_sample = _final(tables[0], h2_all, rw, y_sorted, norm_final[None], tp, ts)

    xh = d // N_XHEADS
    return (
        y_prompt.reshape(nbp, seq_p, d),
        y_sample.reshape(nbs, seq_s, d),
        mk4[None],
        mv4[None],
        ca_p[None],
        jnp.concatenate([cmx_p, cmb_p], axis=-1)[None],
        h_p.reshape(1, nbp, nheads, HEAD_DIM, D_STATE),
        ca_s[None],
        jnp.concatenate([cmx_s, cmb_s], axis=-1)[None],
        h_s.reshape(1, nbs, nheads, HEAD_DIM, D_STATE),
    )
```

```python
import functools

import jax
import jax.numpy as jnp
from jax import lax
from jax.experimental import pallas as pl
from jax.experimental.pallas import tpu as pltpu

F32 = jnp.float32
BF16 = jnp.bfloat16
EPS = 1e-6

V7X_VMEM_BYTES = 64 * 1024 * 1024
VMEM_LIMIT = V7X_VMEM_BYTES - 8 * 1024 * 1024
LANES = 128
SUBLANES = 8

N_GROUPS = 8
HEADS_PER_GROUP = 8
HEAD_DIM = 64
D_STATE = 128
GROUP_W = HEADS_PER_GROUP * HEAD_DIM
N_XHEADS = 4
N_EXPERTS = 32
N_EXPERT_GROUPS = 4
EXPERTS_PER_GROUP = 8
EXPERT_SUB = 256
CONV_A_K = 3
CONV_M_K = 4

NT_DIMS = (((1,), (1,)), ((), ()))
TN_DIMS = (((0,), (0,)), ((), ()))


def _params(sem):
    return pltpu.CompilerParams(dimension_semantics=sem, vmem_limit_bytes=VMEM_LIMIT)


def _pick(n, cands):
    for c in cands:
        if n % c == 0:
            return c
    raise ValueError(f"no tile for {n} in {cands}")


def _dot(a, b):
    return jnp.dot(a, b, preferred_element_type=F32)


def _dot_nt(a, b):
    return lax.dot_general(a, b, NT_DIMS, preferred_element_type=F32)


def _rms(x, g):
    return x * lax.rsqrt(jnp.mean(x * x, axis=-1, keepdims=True) + EPS) * g


def _split3(x):
    hi = x.astype(BF16)
    r = x - hi.astype(F32)
    mid = r.astype(BF16)
    lo = (r - mid.astype(F32)).astype(BF16)
    return hi, mid, lo


def _softplus(x):
    return jnp.maximum(x, 0.0) + jnp.log1p(jnp.exp(-jnp.abs(x)))


def _norm_dt_kernel(xp_ref, xs_ref, g_ref, wdt_ref, xnp_ref, xns_ref, dtp_ref, dts_ref):
    wdt = wdt_ref[...].astype(BF16)

    def one(x_ref, xn_ref, dt_ref):
        xn = _rms(x_ref[...], g_ref[...]).astype(BF16)
        xn_ref[...] = xn
        dt_ref[...] = _dot_nt(xn, wdt)

    one(xp_ref, xnp_ref, dtp_ref)

    @pl.when(pl.program_id(0) == 0)
    def _():
        one(xs_ref, xns_ref, dts_ref)


def _norm_dt(xp, xs, g, wdt):
    tp, d = xp.shape
    ts = xs.shape[0]
    nh = wdt.shape[0]
    tm = _pick(tp, (512, 256, 128))
    return pl.pallas_call(
        _norm_dt_kernel,
        grid=(tp // tm,),
        in_specs=[
            pl.BlockSpec((tm, d), lambda i: (i, 0)),
            pl.BlockSpec((ts, d), lambda i: (0, 0)),
            pl.BlockSpec((1, d), lambda i: (0, 0)),
            pl.BlockSpec((nh, d), lambda i: (0, 0)),
        ],
        out_specs=[
            pl.BlockSpec((tm, d), lambda i: (i, 0)),
            pl.BlockSpec((ts, d), lambda i: (0, 0)),
            pl.BlockSpec((tm, nh), lambda i: (i, 0)),
            pl.BlockSpec((ts, nh), lambda i: (0, 0)),
        ],
        out_shape=[
            jax.ShapeDtypeStruct((tp, d), BF16),
            jax.ShapeDtypeStruct((ts, d), BF16),
            jax.ShapeDtypeStruct((tp, nh), F32),
            jax.ShapeDtypeStruct((ts, nh), F32),
        ],
        compiler_params=_params(("arbitrary",)),
        name="norm_dt",
    )(xp, xs, g, wdt)


def _proj_a_kernel(xp_ref, xs_ref, wb_ref, wc_ref, wh_ref, cw_ref, st_ref,
                   op_ref, os_ref, cap_ref, cas_ref,
                   wbf, ubuf, sbuf, s1buf, s2buf, *, tiles_per_batch, nb_s, l_s):
    i = pl.program_id(1)
    tm = xp_ref.shape[0]
    ts = xs_ref.shape[0]
    cw = cw_ref[...]

    @pl.when(i == 0)
    def _():
        wbf[0] = wb_ref[...].astype(BF16)
        wbf[1] = wc_ref[...].astype(BF16)
        wbf[2] = wh_ref[...].astype(BF16)
        x = xs_ref[...]
        u = _dot_nt(x, wbf[1]) * _dot_nt(x, wbf[2])
        sbuf[pl.ds(0, SUBLANES), :] = jnp.zeros((SUBLANES, u.shape[1]), F32)
        sbuf[pl.ds(SUBLANES, ts), :] = u
        s1buf[...] = jnp.zeros_like(s1buf)
        s2buf[...] = jnp.zeros_like(s2buf)
        for b in range(nb_s):
            s1buf[pl.ds(b * l_s, 1), :] = st_ref[b, pl.ds(1, 1), :]
            s2buf[pl.ds(b * l_s, 1), :] = st_ref[b, pl.ds(0, 1), :]
            s2buf[pl.ds(b * l_s + 1, 1), :] = st_ref[b, pl.ds(1, 1), :]
        rmod = lax.broadcasted_iota(jnp.int32, (ts, 1), 0) % l_s
        prev1 = jnp.where(rmod == 0, s1buf[...], sbuf[pl.ds(SUBLANES - 1, ts), :])
        prev2 = jnp.where(rmod < 2, s2buf[...], sbuf[pl.ds(SUBLANES - 2, ts), :])
        v = prev2 * cw[0:1, :] + prev1 * cw[1:2, :] + u * cw[2:3, :]
        os_ref[...] = (_dot_nt(x, wbf[0]) * v).astype(BF16)
        for b in range(nb_s):
            cas_ref[b] = sbuf[pl.ds(SUBLANES + (b + 1) * l_s - 2, 2), :]

    @pl.when(i % tiles_per_batch == 0)
    def _():
        ubuf[pl.ds(0, SUBLANES), :] = jnp.zeros((SUBLANES, ubuf.shape[1]), F32)

    x = xp_ref[...]
    u = _dot_nt(x, wbf[1]) * _dot_nt(x, wbf[2])
    ubuf[pl.ds(SUBLANES, tm), :] = u
    v = (ubuf[pl.ds(SUBLANES - 2, tm), :] * cw[0:1, :]
         + ubuf[pl.ds(SUBLANES - 1, tm), :] * cw[1:2, :] + u * cw[2:3, :])
    op_ref[...] = (_dot_nt(x, wbf[0]) * v).astype(BF16)
    ubuf[pl.ds(0, SUBLANES), :] = ubuf[pl.ds(tm, SUBLANES), :]

    @pl.when(i % tiles_per_batch == tiles_per_batch - 1)
    def _():
        cap_ref[0] = ubuf[pl.ds(SUBLANES + tm - 2, 2), :]


def _proj_a(xnp, xns, w_in, conv_w, state_s, n_batch_p, seq_p, d_conv):
    tp, d = xnp.shape
    ts = xns.shape[0]
    nb_s = state_s.shape[0]
    l_s = ts // nb_s
    tn = 512
    tm = _pick(seq_p, (1024, 512, 256, 128))
    tpb = seq_p // tm
    ncol = d_conv // tn
    kern = functools.partial(_proj_a_kernel, tiles_per_batch=tpb, nb_s=nb_s, l_s=l_s)
    return pl.pallas_call(
        kern,
        grid=(ncol, tp // tm),
        in_specs=[
            pl.BlockSpec((tm, d), lambda j, i: (i, 0)),
            pl.BlockSpec((ts, d), lambda j, i: (0, 0)),
            pl.BlockSpec((tn, d), lambda j, i: (j, 0)),
            pl.BlockSpec((tn, d), lambda j, i: (j + ncol, 0)),
            pl.BlockSpec((tn, d), lambda j, i: (j + 2 * ncol, 0)),
            pl.BlockSpec((CONV_A_K, tn), lambda j, i: (0, j)),
            pl.BlockSpec((nb_s, CONV_A_K - 1, tn), lambda j, i: (0, 0, j)),
        ],
        out_specs=[
            pl.BlockSpec((tm, tn), lambda j, i: (i, j)),
            pl.BlockSpec((ts, tn), lambda j, i: (0, j)),
            pl.BlockSpec((1, CONV_A_K - 1, tn), lambda j, i: (i // tpb, 0, j)),
            pl.BlockSpec((nb_s, CONV_A_K - 1, tn), lambda j, i: (0, 0, j)),
        ],
        out_shape=[
            jax.ShapeDtypeStruct((tp, d_conv), BF16),
            jax.ShapeDtypeStruct((ts, d_conv), BF16),
            jax.ShapeDtypeStruct((n_batch_p, CONV_A_K - 1, d_conv), F32),
            jax.ShapeDtypeStruct((nb_s, CONV_A_K - 1, d_conv), F32),
        ],
        scratch_shapes=[
            pltpu.VMEM((3, tn, d), BF16),
            pltpu.VMEM((SUBLANES + tm, tn), F32),
            pltpu.VMEM((SUBLANES + ts, tn), F32),
            pltpu.VMEM((ts, tn), F32),
            pltpu.VMEM((ts, tn), F32),
        ],
        compiler_params=_params(("arbitrary", "arbitrary")),
        name="proj_a",
    )(xnp, xns, w_in, w_in, w_in, conv_w, state_s)


def _proj_raw_kernel(xp_ref, xs_ref, w_hbm, op_ref, os_ref, wbf, wstage, wsem, *, nsplit, width, row0):
    j = pl.program_id(0)
    tn = wbf.shape[0]

    def w_copy(jj, slot):
        rows = pl.ds(pl.multiple_of(row0 + jj * tn, SUBLANES), tn)
        return pltpu.make_async_copy(w_hbm.at[rows], wstage.at[slot], wsem.at[slot])

    def emit(x_ref, o_ref):
        acc = _dot_nt(x_ref[...], wbf[...]).astype(o_ref.dtype)
        if nsplit == 0:
            o_ref[...] = acc
        else:
            for s in range(nsplit):
                o_ref[s] = acc[:, s * width:(s + 1) * width]

    @pl.when(pl.program_id(1) == 0)
    def _():
        @pl.when(j == 0)
        def _():
            w_copy(0, 0).start()

        w_copy(j, j % 2).wait()

        @pl.when(j + 1 < pl.num_programs(0))
        def _():
            w_copy(j + 1, (j + 1) % 2).start()

        wbf[...] = wstage[j % 2].astype(BF16)
        emit(xs_ref, os_ref)

    emit(xp_ref, op_ref)


def _proj_raw(xnp, xns, w, col0, ncols, width):
    tp, d = xnp.shape
    ts = xns.shape[0]
    tn = 1024
    tm = _pick(tp, (1024, 512, 256, 128))
    assert col0 % SUBLANES == 0 and ncols % tn == 0
    if width == 0:
        nsplit = 0
        out_specs = [pl.BlockSpec((tm, tn), lambda j, i: (i, j)),
                     pl.BlockSpec((ts, tn), lambda j, i: (0, j))]
        out_shape = [jax.ShapeDtypeStruct((tp, ncols), BF16), jax.ShapeDtypeStruct((ts, ncols), BF16)]
    else:
        nsplit = tn // width
        out_specs = [pl.BlockSpec((nsplit, tm, width), lambda j, i: (j, i, 0)),
                     pl.BlockSpec((nsplit, ts, width), lambda j, i: (j, 0, 0))]
        out_shape = [jax.ShapeDtypeStruct((ncols // width, tp, width), BF16),
                     jax.ShapeDtypeStruct((ncols // width, ts, width), BF16)]
    kern = functools.partial(_proj_raw_kernel, nsplit=nsplit, width=width, row0=col0)
    return pl.pallas_call(
        kern,
        grid=(ncols // tn, tp // tm),
        in_specs=[
            pl.BlockSpec((tm, d), lambda j, i: (i, 0)),
            pl.BlockSpec((ts, d), lambda j, i: (0, 0)),
            pl.BlockSpec(memory_space=pl.ANY),
        ],
        out_specs=out_specs,
        out_shape=out_shape,
        scratch_shapes=[pltpu.VMEM((tn, d), BF16), pltpu.VMEM((2, tn, d), F32), pltpu.SemaphoreType.DMA((2,))],
        compiler_params=_params(("arbitrary", "arbitrary")),
        name=f"proj_raw_{col0}",
    )(xnp, xns, w)


def _proj_conv_kernel(xp_ref, xs_ref, w_ref, cw_ref, cb_ref, st_ref, op_ref, os_ref, cmp_ref, cms_ref,
                      wbf, ubuf, sbuf, fix, *, tiles_per_batch, nb_s, l_s, nsplit, width):
    i = pl.program_id(1)
    tm = xp_ref.shape[0]
    ts = xs_ref.shape[0]
    tail = CONV_M_K - 1
    cw = cw_ref[...]
    bias = cb_ref[...]

    def store(o_ref, act):
        act = act.astype(o_ref.dtype)
        for s in range(nsplit):
            o_ref[s] = act[:, s * width:(s + 1) * width]

    @pl.when(i == 0)
    def _():
        wbf[...] = w_ref[...].astype(BF16)
        raw = _dot_nt(xs_ref[...], wbf[...])
        sbuf[pl.ds(0, SUBLANES), :] = jnp.zeros((SUBLANES, raw.shape[1]), F32)
        sbuf[pl.ds(SUBLANES, ts), :] = raw
        fix[...] = jnp.zeros_like(fix)
        for b in range(nb_s):
            for back in range(1, tail + 1):
                for m in range(back):
                    fix[back - 1, pl.ds(b * l_s + m, 1), :] = st_ref[b, pl.ds(tail + m - back, 1), :]
        rmod = lax.broadcasted_iota(jnp.int32, (ts, 1), 0) % l_s
        acc = raw * cw[tail:tail + 1, :]
        for back in range(1, tail + 1):
            tap = jnp.where(rmod < back, fix[back - 1], sbuf[pl.ds(SUBLANES - back, ts), :])
            acc = acc + tap * cw[tail - back:tail - back + 1, :]
        store(os_ref, jax.nn.silu(acc + bias))
        for b in range(nb_s):
            cms_ref[b] = sbuf[pl.ds(SUBLANES + (b + 1) * l_s - tail, tail), :]

    @pl.when(i % tiles_per_batch == 0)
    def _():
        ubuf[pl.ds(0, SUBLANES), :] = jnp.zeros((SUBLANES, ubuf.shape[1]), F32)

    raw = _dot_nt(xp_ref[...], wbf[...])
    ubuf[pl.ds(SUBLANES, tm), :] = raw
    acc = raw * cw[tail:tail + 1, :]
    for back in range(1, tail + 1):
        acc = acc + ubuf[pl.ds(SUBLANES - back, tm), :] * cw[tail - back:tail - back + 1, :]
    store(op_ref, jax.nn.silu(acc + bias))
    ubuf[pl.ds(0, SUBLANES), :] = ubuf[pl.ds(tm, SUBLANES), :]

    @pl.when(i % tiles_per_batch == tiles_per_batch - 1)
    def _():
        cmp_ref[0] = ubuf[pl.ds(SUBLANES + tm - tail, tail), :]


def _proj_conv(xnp, xns, w, conv_w, conv_b, state_s, col0, ch0, ncols, width, n_batch_p, seq_p):
    tp, d = xnp.shape
    ts = xns.shape[0]
    nb_s = state_s.shape[0]
    l_s = ts // nb_s
    tail = CONV_M_K - 1
    tn = 1024
    tm = _pick(seq_p, (1024, 512, 256, 128))
    tpb = seq_p // tm
    assert col0 % tn == 0 and ch0 % tn == 0 and ncols % tn == 0 and l_s >= tail
    jb, cb0 = col0 // tn, ch0 // tn
    nsplit = tn // width
    kern = functools.partial(_proj_conv_kernel, tiles_per_batch=tpb, nb_s=nb_s, l_s=l_s,
                             nsplit=nsplit, width=width)
    return pl.pallas_call(
        kern,
        grid=(ncols // tn, tp // tm),
        in_specs=[
            pl.BlockSpec((tm, d), lambda j, i: (i, 0)),
            pl.BlockSpec((ts, d), lambda j, i: (0, 0)),
            pl.BlockSpec((tn, d), lambda j, i: (j + jb, 0)),
            pl.BlockSpec((CONV_M_K, tn), lambda j, i: (0, j + cb0)),
            pl.BlockSpec((1, tn), lambda j, i: (0, j + cb0)),
            pl.BlockSpec((nb_s, tail, tn), lambda j, i: (0, 0, j + cb0)),
        ],
        out_specs=[
            pl.BlockSpec((nsplit, tm, width), lambda j, i: (j, i, 0)),
            pl.BlockSpec((nsplit, ts, width), lambda j, i: (j, 0, 0)),
            pl.BlockSpec((1, tail, tn), lambda j, i: (i // tpb, 0, j)),
            pl.BlockSpec((nb_s, tail, tn), lambda j, i: (0, 0, j)),
        ],
        out_shape=[
            jax.ShapeDtypeStruct((ncols // width, tp, width), BF16),
            jax.ShapeDtypeStruct((ncols // width, ts, width), BF16),
            jax.ShapeDtypeStruct((n_batch_p, tail, ncols), F32),
            jax.ShapeDtypeStruct((nb_s, tail, ncols), F32),
        ],
        scratch_shapes=[
            pltpu.VMEM((tn, d), BF16),
            pltpu.VMEM((SUBLANES + tm, tn), F32),
            pltpu.VMEM((SUBLANES + ts, tn), F32),
            pltpu.VMEM((tail, ts, tn), F32),
        ],
        compiler_params=_params(("arbitrary", "arbitrary")),
        name=f"proj_conv_{col0}",
    )(xnp, xns, w, conv_w, conv_b, state_s)


def _ssd_kernel(*refs, q, has_state, nchunks):
    (z_ref, xs_ref, b_ref, c_ref, dt_ref, dtb_ref, alog_ref, dsk_ref, gn_ref, *rest) = refs
    if has_state:
        hprev, *rest = rest
    (y_ref, oh, h_s, acg, rowt) = rest
    c = pl.program_id(1)
    nheads = N_GROUPS * HEADS_PER_GROUP

    @pl.when(c == 0)
    def _init():
        if has_state:
            h_s[...] = hprev[...]
        else:
            h_s[...] = jnp.zeros_like(h_s)

    dt = _softplus(dt_ref[...] + dtb_ref[...])
    da = dt * (-jnp.exp(alog_ref[...]))
    ri = lax.broadcasted_iota(jnp.int32, (q, q), 0)
    ci = lax.broadcasted_iota(jnp.int32, (q, q), 1)
    causal = ri >= ci
    tril = jnp.where(causal, 1.0, 0.0).astype(BF16)
    acum = sum(_dot(tril, p) for p in _split3(da))
    eye = jnp.where(lax.broadcasted_iota(jnp.int32, (nheads, nheads), 0)
                    == lax.broadcasted_iota(jnp.int32, (nheads, nheads), 1), 1.0, 0.0).astype(BF16)
    rowt[0] = sum(lax.dot_general(eye, p, NT_DIMS, preferred_element_type=F32) for p in _split3(acum))
    rowt[1] = sum(lax.dot_general(eye, p, NT_DIMS, preferred_element_type=F32) for p in _split3(dt))
    wend = jnp.exp(acum[q - 1:q, :] - acum) * dt
    rowt[2] = sum(lax.dot_general(eye, p, NT_DIMS, preferred_element_type=F32) for p in _split3(wend))
    for g in range(N_GROUPS):
        acg[g] = acum[:, g * HEADS_PER_GROUP:(g + 1) * HEADS_PER_GROUP]

    lane = lax.broadcasted_iota(jnp.int32, (1, LANES), 1)
    rowi = lax.broadcasted_iota(jnp.int32, (LANES, 1), 0)
    half_w = LANES // 2

    def group_body(g, carry):
        xs = xs_ref[g].astype(F32)
        bb = b_ref[g]
        ccb16 = c_ref[g]
        cb_ = lax.dot_general(ccb16, bb, NT_DIMS, preferred_element_type=F32)
        ac8 = acg[g]
        dsk = dsk_ref[g]
        lo_half = lane < half_w
        lo_rows = rowi < half_w
        ys = []
        for pair in range(HEADS_PER_GROUP // 2):
            sl = slice(pair * LANES, (pair + 1) * LANES)
            xp = xs[:, sl]
            hp = h_s[g, pl.ds(pair * LANES, LANES), :]
            ms_, acols, wrows = [], [], []
            for r in (2 * pair, 2 * pair + 1):
                head = g * HEADS_PER_GROUP + r
                acol = jnp.broadcast_to(ac8[:, r:r + 1], (q, LANES))
                arow = rowt[0, pl.ds(head, 1), :]
                drow = rowt[1, pl.ds(head, 1), :]
                decay = jnp.exp(jnp.where(causal, acol[:, :q] - arow, -jnp.inf))
                ms_.append((cb_ * decay * drow).astype(BF16))
                acols.append(acol)
                wrows.append(jnp.broadcast_to(rowt[2, pl.ds(head, 1), :], (half_w, q)))
            xlo = jnp.where(lo_half, xp, 0.0).astype(BF16)
            xhi = jnp.where(lo_half, 0.0, xp).astype(BF16)
            if q % LANES == 0:
                ydiag = _dot(jnp.concatenate(ms_, axis=1), jnp.concatenate([xlo, xhi], axis=0))
            else:
                ydiag = _dot(ms_[0], xlo) + _dot(ms_[1], xhi)
            ea = jnp.where(lo_half, jnp.exp(acols[0]), jnp.exp(acols[1]))
            yoff = ea * lax.dot_general(ccb16, hp.astype(BF16), NT_DIMS, preferred_element_type=F32)
            ys.append(dsk[:, sl] * xp + ydiag + yoff)
            xwt = (xp.T * jnp.concatenate(wrows, axis=0)).astype(BF16)
            dlast = [jnp.broadcast_to(jnp.exp(a[q - 1:q, :]), (LANES, LANES)) for a in acols]
            h_s[g, pl.ds(pair * LANES, LANES), :] = hp * jnp.where(lo_rows, dlast[0], dlast[1]) + _dot(xwt, bb)
        yg = jnp.concatenate(ys, axis=1)
        hh = yg * jax.nn.silu(z_ref[g].astype(F32))
        ms = jnp.mean(hh * hh, axis=-1, keepdims=True)
        y_ref[g] = (hh * lax.rsqrt(ms + EPS) * gn_ref[g]).astype(BF16)
        return carry

    lax.fori_loop(0, N_GROUPS, group_body, 0)

    @pl.when(c == nchunks - 1)
    def _():
        oh[...] = h_s[...]


def _ssd(z, xc, bc, dt_raw, prm, n_batch, seq, q, h_prev=None):
    nchunks = seq // q
    nheads = N_GROUPS * HEADS_PER_GROUP
    has_state = h_prev is not None
    g8 = N_GROUPS

    def tok(first):
        return lambda b, c: (first, b * nchunks + c, 0)

    def const3(b, c):
        return (0, 0, 0)

    in_specs = [
        pl.BlockSpec((g8, q, GROUP_W), tok(0)),
        pl.BlockSpec((g8, q, GROUP_W), tok(0)),
        pl.BlockSpec((g8, q, D_STATE), tok(0)),
        pl.BlockSpec((g8, q, D_STATE), tok(1)),
        pl.BlockSpec((q, nheads), lambda b, c: (b * nchunks + c, 0)),
        pl.BlockSpec((1, nheads), lambda b, c: (0, 0)),
        pl.BlockSpec((1, nheads), lambda b, c: (0, 0)),
        pl.BlockSpec((g8, 1, GROUP_W), const3),
        pl.BlockSpec((g8, 1, GROUP_W), const3),
    ]
    args = [z, xc, bc, bc, dt_raw, prm["dtb"], prm["alog"], prm["dsk"], prm["gn"]]

    def per_batch(shape):
        return pl.BlockSpec((None,) + shape, lambda b, c: (b,) + (0,) * len(shape))

    if has_state:
        in_specs.append(per_batch((g8, GROUP_W, D_STATE)))
        args.append(h_prev)
    t = n_batch * seq
    out_specs = [
        pl.BlockSpec((g8, q, GROUP_W), lambda b, c: (0, b * nchunks + c, 0)),
        per_batch((g8, GROUP_W, D_STATE)),
    ]
    out_shape = [
        jax.ShapeDtypeStruct((g8, t, GROUP_W), BF16),
        jax.ShapeDtypeStruct((n_batch, g8, GROUP_W, D_STATE), F32),
    ]
    scratch = [
        pltpu.VMEM((g8, GROUP_W, D_STATE), F32),
        pltpu.VMEM((g8, q, HEADS_PER_GROUP), F32),
        pltpu.VMEM((3, nheads, q), F32),
    ]
    kern = functools.partial(_ssd_kernel, q=q, has_state=has_state, nchunks=nchunks)
    return pl.pallas_call(
        kern,
        grid=(n_batch, nchunks),
        in_specs=in_specs,
        out_specs=out_specs,
        out_shape=out_shape,
        scratch_shapes=scratch,
        compiler_params=_params(("arbitrary", "arbitrary")),
        name="ssd_state" if has_state else "ssd",
    )(*args)


def _merge_kernel(ap_ref, as_ref, yp_ref, ys_ref, gap_ref, gmp_ref, gas_ref, gms_ref, wa_ref, wm_ref,
                  op_ref, os_ref, wab, wmb):
    def emit(a_ref, y_ref, ga_ref, gm_ref, o_ref):
        oa = _dot(a_ref[...], wab[...])
        y = jnp.concatenate([y_ref[g] for g in range(N_GROUPS)], axis=1)
        om = _dot(y, wmb[...])
        ga = jax.nn.sigmoid(ga_ref[...].astype(F32))
        gm = jax.nn.sigmoid(gm_ref[...].astype(F32))
        o_ref[...] = (ga * oa + gm * om).astype(BF16)

    @pl.when(pl.program_id(1) == 0)
    def _():
        wab[...] = wa_ref[...].astype(BF16)
        wmb[...] = wm_ref[...].astype(BF16)
        emit(as_ref, ys_ref, gas_ref, gms_ref, os_ref)

    emit(ap_ref, yp_ref, gap_ref, gmp_ref, op_ref)


def _merge(abv_p, abv_s, y_p, y_s, g_p, g_s, w_a_out, w_m_out):
    tp, dc = abv_p.shape
    ts = abv_s.shape[0]
    dm = w_a_out.shape[1]
    tn = 512
    tm = _pick(tp, (512, 256, 128))
    ncol = dm // tn
    return pl.pallas_call(
        _merge_kernel,
        grid=(ncol, tp // tm),
        in_specs=[
            pl.BlockSpec((tm, dc), lambda j, i: (i, 0)),
            pl.BlockSpec((ts, dc), lambda j, i: (0, 0)),
            pl.BlockSpec((N_GROUPS, tm, GROUP_W), lambda j, i: (0, i, 0)),
            pl.BlockSpec((N_GROUPS, ts, GROUP_W), lambda j, i: (0, 0, 0)),
            pl.BlockSpec((tm, tn), lambda j, i: (i, j)),
            pl.BlockSpec((tm, tn), lambda j, i: (i, j + ncol)),
            pl.BlockSpec((ts, tn), lambda j, i: (0, j)),
            pl.BlockSpec((ts, tn), lambda j, i: (0, j + ncol)),
            pl.BlockSpec((dc, tn), lambda j, i: (0, j)),
            pl.BlockSpec((N_GROUPS * GROUP_W, tn), lambda j, i: (0, j)),
        ],
        out_specs=[
            pl.BlockSpec((tm, tn), lambda j, i: (i, j)),
            pl.BlockSpec((ts, tn), lambda j, i: (0, j)),
        ],
        out_shape=[jax.ShapeDtypeStruct((tp, dm), BF16), jax.ShapeDtypeStruct((ts, dm), BF16)],
        scratch_shapes=[pltpu.VMEM((dc, tn), BF16), pltpu.VMEM((N_GROUPS * GROUP_W, tn), BF16)],
        compiler_params=_params(("arbitrary", "arbitrary")),
        name="merge",
    )(abv_p, abv_s, y_p, y_s, g_p, g_p, g_s, g_s, w_a_out, w_m_out)


def _res_kernel(ap_ref, as_ref, rp_ref, rs_ref, w_ref, g_ref, *outs, n_prompt_tiles, merged_out, ts):
    if merged_out:
        h_ref, hn_ref, wbf = outs
    else:
        hp_ref, hs_ref, hnp_ref, hns_ref, wbf = outs
    i = pl.program_id(0)

    @pl.when(i == 0)
    def _():
        wbf[...] = w_ref[...].astype(BF16)

    def emit(a_ref, r_ref, store_h, store_hn):
        h = r_ref[...] + _dot(a_ref[...], wbf[...])
        store_h(h)
        store_hn(_rms(h, g_ref[...]))

    if merged_out:
        @pl.when(i < n_prompt_tiles)
        def _():
            def sh(h):
                h_ref[...] = h

            def shn(hn):
                hn_ref[...] = hn

            emit(ap_ref, rp_ref, sh, shn)

        @pl.when(i == n_prompt_tiles)
        def _():
            def sh(h):
                h_ref[pl.ds(0, ts), :] = h

            def shn(hn):
                hn_ref[pl.ds(0, ts), :] = hn

            emit(as_ref, rs_ref, sh, shn)
    else:
        def shp(h):
            hp_ref[...] = h

        def shnp(hn):
            hnp_ref[...] = hn.astype(hnp_ref.dtype)

        emit(ap_ref, rp_ref, shp, shnp)

        @pl.when(i == 0)
        def _():
            def shs(h):
                hs_ref[...] = h

            def shns(hn):
                hns_ref[...] = hn.astype(hns_ref.dtype)

            emit(as_ref, rs_ref, shs, shns)


def _res(a_p, a_s, r_p, r_s, w, g, merged_out, hn_dtype):
    tp, d = a_p.shape
    ts = a_s.shape[0]
    tm = 256
    assert tp % tm == 0 and ts <= tm
    npt = tp // tm
    last = npt - 1
    in_specs = [
        pl.BlockSpec((tm, d), lambda i: (jnp.minimum(i, last), 0)),
        pl.BlockSpec((ts, d), lambda i: (0, 0)),
        pl.BlockSpec((tm, d), lambda i: (jnp.minimum(i, last), 0)),
        pl.BlockSpec((ts, d), lambda i: (0, 0)),
        pl.BlockSpec((d, d), lambda i: (0, 0), pipeline_mode=pl.Buffered(1)),
        pl.BlockSpec((1, d), lambda i: (0, 0)),
    ]
    if merged_out:
        grid = (npt + 1,)
        out_specs = [pl.BlockSpec((tm, d), lambda i: (i, 0)), pl.BlockSpec((tm, d), lambda i: (i, 0))]
        out_shape = [jax.ShapeDtypeStruct((tp + ts, d), F32), jax.ShapeDtypeStruct((tp + ts, d), hn_dtype)]
    else:
        grid = (npt,)
        out_specs = [pl.BlockSpec((tm, d), lambda i: (i, 0)), pl.BlockSpec((ts, d), lambda i: (0, 0)),
                     pl.BlockSpec((tm, d), lambda i: (i, 0)), pl.BlockSpec((ts, d), lambda i: (0, 0))]
        out_shape = [jax.ShapeDtypeStruct((tp, d), F32), jax.ShapeDtypeStruct((ts, d), F32),
                     jax.ShapeDtypeStruct((tp, d), hn_dtype), jax.ShapeDtypeStruct((ts, d), hn_dtype)]
    kern = functools.partial(_res_kernel, n_prompt_tiles=npt, merged_out=merged_out, ts=ts)
    return pl.pallas_call(
        kern,
        grid=grid,
        in_specs=in_specs,
        out_specs=out_specs,
        out_shape=out_shape,
        scratch_shapes=[pltpu.VMEM((d, d), BF16)],
        compiler_params=_params(("arbitrary",)),
        name="res_merged" if merged_out else "res",
    )(a_p, a_s, r_p, r_s, w, g)


def _kv_kernel(m_ref, g_ref, wk_ref, wv_ref, k_hbm, v_hbm, mn, kvb, sem, *, n_batch):
    j = pl.program_id(0)

    @pl.when(j == 0)
    def _():
        mn[...] = _rms(m_ref[...], g_ref[...]).astype(BF16)

    kvb[0] = _dot(mn[...], wk_ref[...].astype(BF16))
    kvb[1] = _dot(mn[...], wv_ref[...].astype(BF16))
    nm = kvb.shape[1] // n_batch
    for h in range(N_XHEADS):
        @pl.when(j == h)
        def _(h=h):
            copies = [pltpu.make_async_copy(kvb.at[t, pl.ds(b * nm, nm)], dst.at[b, :, h, :], sem.at[t, b])
                      for t, dst in enumerate((k_hbm, v_hbm)) for b in range(n_batch)]
            for c in copies:
                c.start()
            for c in copies:
                c.wait()


def _memory_kv(mem2d, g, w_k, w_v, n_batch):
    m, d = mem2d.shape
    dh = d // N_XHEADS
    out_sds = jax.ShapeDtypeStruct((n_batch, m // n_batch, N_XHEADS, dh), F32)
    kern = functools.partial(_kv_kernel, n_batch=n_batch)
    return pl.pallas_call(
        kern,
        grid=(N_XHEADS,),
        in_specs=[
            pl.BlockSpec((m, d), lambda j: (0, 0)),
            pl.BlockSpec((1, d), lambda j: (0, 0)),
            pl.BlockSpec((d, dh), lambda j: (0, j)),
            pl.BlockSpec((d, dh), lambda j: (0, j)),
        ],
        out_specs=[pl.BlockSpec(memory_space=pl.ANY), pl.BlockSpec(memory_space=pl.ANY)],
        out_shape=[out_sds, out_sds],
        scratch_shapes=[pltpu.VMEM((m, d), BF16), pltpu.VMEM((2, m, dh), F32),
                        pltpu.SemaphoreType.DMA((2, n_batch))],
        compiler_params=_params(("arbitrary",)),
        name="memory_kv",
    )(mem2d, g, w_k, w_v)


def _attn_kernel(hn_ref, k_ref, v_ref, wq_ref, o_ref, wqb, kb, vb, kvf, sem):
    b = pl.program_id(0)
    i = pl.program_id(1)

    @pl.when((b == 0) & (i == 0))
    def _():
        wqb[...] = wq_ref[...].astype(BF16)

    d = wqb.shape[1]
    dh = d // N_XHEADS

    @pl.when(i == 0)
    def _():
        copies = [pltpu.make_async_copy(src.at[b, :, h, :], kvf.at[t, h], sem.at[t, h])
                  for t, src in enumerate((k_ref, v_ref)) for h in range(N_XHEADS)]
        for c in copies:
            c.start()
        for c in copies:
            c.wait()
        for h in range(N_XHEADS):
            kb[:, h * dh:(h + 1) * dh] = kvf[0, h].astype(BF16)
            vb[:, h * dh:(h + 1) * dh] = kvf[1, h].astype(BF16)

    q = _dot(hn_ref[...], wqb[...])
    outs = []
    for h in range(N_XHEADS):
        sl = slice(h * dh, (h + 1) * dh)
        s = lax.dot_general(q[:, sl].astype(BF16), kb[:, sl], NT_DIMS, preferred_element_type=F32)
        s = s * (dh ** -0.5)
        e = jnp.exp(s - jnp.max(s, axis=-1, keepdims=True))
        p = e / jnp.sum(e, axis=-1, keepdims=True)
        outs.append(_dot(p.astype(BF16), vb[:, sl]))
    o_ref[...] = jnp.concatenate(outs, axis=1).astype(BF16)


def _attn(hn, k, v, w_q, n_batch, seq):
    t, d = hn.shape
    nm, nh, dh = k.shape[1:]
    tm = _pick(seq, (512, 256, 128, 64, 32, 16))
    tpb = seq // tm
    kv_spec = pl.BlockSpec(memory_space=pl.ANY)
    scratch = [pltpu.VMEM((d, d), BF16), pltpu.VMEM((nm, d), BF16), pltpu.VMEM((nm, d), BF16),
               pltpu.VMEM((2, nh, nm, dh), F32), pltpu.SemaphoreType.DMA((2, nh))]
    return pl.pallas_call(
        _attn_kernel,
        grid=(n_batch, tpb),
        in_specs=[
            pl.BlockSpec((tm, d), lambda b, i: (b * tpb + i, 0)),
            kv_spec,
            kv_spec,
            pl.BlockSpec((d, d), lambda b, i: (0, 0), pipeline_mode=pl.Buffered(1)),
        ],
        out_specs=pl.BlockSpec((tm, d), lambda b, i: (b * tpb + i, 0)),
        out_shape=jax.ShapeDtypeStruct((t, d), BF16),
        scratch_shapes=scratch,
        compiler_params=_params(("arbitrary", "arbitrary")),
        name=f"attn_{seq}",
    )(hn, k, v, w_q)


def _router_kernel(x_ref, w_ref, b_ref, ri_ref, rw_ref):
    logits = _dot(x_ref[...].astype(BF16), w_ref[...].astype(BF16)) + b_ref[...]
    lane_i = lax.broadcasted_iota(jnp.int32, logits.shape, 1)
    lane = lane_i.astype(F32)
    ninf = -jnp.inf
    big = float(LANES)
    is_g = lane < N_EXPERT_GROUPS
    gl = jnp.where(is_g, logits, ninf)
    gmax = jnp.max(gl, axis=-1, keepdims=True)
    gsel = jnp.min(jnp.where(gl == gmax, lane, big), axis=-1, keepdims=True)
    pg = 1.0 / jnp.sum(jnp.where(is_g, jnp.exp(gl - gmax), 0.0), axis=-1, keepdims=True)
    lo = N_EXPERT_GROUPS + EXPERTS_PER_GROUP * gsel
    el = jnp.where(lane >= lo, jnp.where(lane < lo + EXPERTS_PER_GROUP, logits, ninf), ninf)
    m1 = jnp.max(el, axis=-1, keepdims=True)
    i1 = jnp.min(jnp.where(el == m1, lane, big), axis=-1, keepdims=True)
    el2 = jnp.where(lane == i1, ninf, el)
    m2 = jnp.max(el2, axis=-1, keepdims=True)
    i2 = jnp.min(jnp.where(el2 == m2, lane, big), axis=-1, keepdims=True)
    e = jnp.exp(m2 - m1)
    w1 = pg / (1.0 + e)
    w2 = pg * e / (1.0 + e)
    ri_ref[...] = jnp.where(lane_i == 0, i1 - N_EXPERT_GROUPS,
                            jnp.where(lane_i == 1, i2 - N_EXPERT_GROUPS, 0.0)).astype(jnp.int32)
    rw_ref[...] = jnp.where(lane_i == 0, w1, jnp.where(lane_i == 1, w2, 0.0))


def _router(hn_all, w_r, b_r):
    t, d = hn_all.shape
    tm = _pick(t, (640, 512, 384, 256, 128))
    return pl.pallas_call(
        _router_kernel,
        grid=(t // tm,),
        in_specs=[
            pl.BlockSpec((tm, d), lambda i: (i, 0)),
            pl.BlockSpec((d, LANES), lambda i: (0, 0)),
            pl.BlockSpec((1, LANES), lambda i: (0, 0)),
        ],
        out_specs=[pl.BlockSpec((tm, LANES), lambda i: (i, 0)), pl.BlockSpec((tm, LANES), lambda i: (i, 0))],
        out_shape=[jax.ShapeDtypeStruct((t, LANES), jnp.int32), jax.ShapeDtypeStruct((t, LANES), F32)],
        compiler_params=_params(("arbitrary",)),
        name="router",
    )(hn_all, w_r, b_r)


def _row_copy(src, dst, s, d, sem):
    return pltpu.make_async_copy(src.at[pl.ds(s, 1)], dst.at[pl.ds(d, 1)], sem)


def _dispatch_kernel(dest_ref, pad0_ref, padn_ref, nsub_ref, x_ref, o_ref, zrow, zblk, sem, zsem, csem, bsem, *,
                     chunk, n_blocks):
    base = pl.program_id(0) * chunk

    @pl.when(pl.program_id(0) == 0)
    def _():
        zrow[...] = jnp.zeros_like(zrow)
        zblk[...] = jnp.zeros_like(zblk)

        def pad_plan(e):
            p0 = pad0_ref[e]
            head = jnp.minimum(padn_ref[e], (SUBLANES - p0 % SUBLANES) % SUBLANES)
            return p0, head, (padn_ref[e] - head) // SUBLANES

        def group_copy(row):
            return pltpu.make_async_copy(zrow, o_ref.at[pl.ds(pl.multiple_of(row, SUBLANES), SUBLANES)], csem)

        def pad_start(e, carry):
            p0, head, ngroups = pad_plan(e)

            def start_row(r, c):
                _row_copy(zrow, o_ref, 0, p0 + r, zsem).start()
                return c

            def start_group(j, c):
                group_copy(p0 + head + j * SUBLANES).start()
                return c

            lax.fori_loop(0, head, start_row, 0)
            return lax.fori_loop(0, ngroups, start_group, carry)

        def pad_wait(e, carry):
            _, head, ngroups = pad_plan(e)

            def wait_row(r, c):
                _row_copy(zrow, o_ref, 0, 0, zsem).wait()
                return c

            def wait_group(j, c):
                group_copy(0).wait()
                return c

            lax.fori_loop(0, head, wait_row, 0)
            return lax.fori_loop(0, ngroups, wait_group, carry)

        lax.fori_loop(0, N_EXPERTS, pad_start, 0)

        def blk_copy(b):
            return pltpu.make_async_copy(zblk, o_ref.at[pl.ds(pl.multiple_of(b * EXPERT_SUB, EXPERT_SUB),
                                                             EXPERT_SUB)], bsem)

        def tail_start(b, c):
            blk_copy(b).start()
            return c

        def tail_wait(b, c):
            blk_copy(b).wait()
            return c

        lax.fori_loop(nsub_ref[0], n_blocks, tail_start, 0)
        lax.fori_loop(0, N_EXPERTS, pad_wait, 0)
        lax.fori_loop(nsub_ref[0], n_blocks, tail_wait, 0)

    def issue(r, carry):
        t = base + r
        _row_copy(x_ref, o_ref, r, dest_ref[2 * t], sem).start()
        _row_copy(x_ref, o_ref, r, dest_ref[2 * t + 1], sem).start()
        return carry

    lax.fori_loop(0, chunk, issue, 0, unroll=8)

    for _ in range(2):
        pltpu.make_async_copy(x_ref, o_ref.at[pl.ds(0, chunk)], sem).wait()


def _dispatch(tables, hn_all, n_blocks):
    dest, pad0, padn, nsub = tables
    t, d = hn_all.shape
    chunk = _pick(t, (640, 512, 384, 256, 128))
    kern = functools.partial(_dispatch_kernel, chunk=chunk, n_blocks=n_blocks)
    return pl.pallas_call(
        kern,
        grid_spec=pltpu.PrefetchScalarGridSpec(
            num_scalar_prefetch=4,
            grid=(t // chunk,),
            in_specs=[pl.BlockSpec((chunk, d), lambda i, *_: (i, 0))],
            out_specs=pl.BlockSpec(memory_space=pl.ANY),
            scratch_shapes=[pltpu.VMEM((SUBLANES, d), F32), pltpu.VMEM((EXPERT_SUB, d), F32),
                            pltpu.SemaphoreType.DMA(()), pltpu.SemaphoreType.DMA(()),
                            pltpu.SemaphoreType.DMA(()), pltpu.SemaphoreType.DMA(())],
        ),
        out_shape=jax.ShapeDtypeStruct((n_blocks * EXPERT_SUB, d), F32),
        compiler_params=_params(("arbitrary",)),
        name="dispatch",
    )(dest, pad0, padn, nsub, hn_all)


def _expert_kernel(bstart_ref, wg_ref, wu_ref, wd_ref, x_hbm, y_hbm, gcache, ucache, dcache, xbuf, ybuf, xb,
                   xsem, ysem, *, n_blocks):
    e = pl.program_id(0)
    k = pl.program_id(1)
    nk = gcache.shape[1]
    total = bstart_ref[N_EXPERTS]

    def x_copy(b, slot):
        rows = pl.ds(pl.multiple_of(b * EXPERT_SUB, EXPERT_SUB), EXPERT_SUB)
        return pltpu.make_async_copy(x_hbm.at[rows], xbuf.at[slot], xsem.at[slot])

    def y_copy(b, slot):
        rows = pl.ds(pl.multiple_of(b * EXPERT_SUB, EXPERT_SUB), EXPERT_SUB)
        return pltpu.make_async_copy(ybuf.at[slot], y_hbm.at[rows], ysem.at[slot])

    @pl.when(e < N_EXPERTS)
    def _():
        slot = e % 2
        gcache[slot, k] = wg_ref[...].astype(BF16)
        ucache[slot, k] = wu_ref[...].astype(BF16)
        dcache[slot, k] = wd_ref[...].astype(BF16)

    @pl.when((e == 0) & (k == 0) & (total > 0))
    def _():
        x_copy(0, 0).start()

    @pl.when(e >= 1)
    def _():
        owner = e - 1
        wslot = owner % 2
        b0 = bstart_ref[owner]
        n = bstart_ref[owner + 1] - b0

        cpu = 2
        upb = nk // cpu

        def unit(u, carry):
            b = b0 + u // upb
            part = u % upb
            slot = b % 2

            @pl.when(part == 0)
            def _():
                x_copy(b, slot).wait()

                @pl.when(b + 1 < total)
                def _():
                    x_copy(b + 1, 1 - slot).start()

                xb[...] = xbuf[slot].astype(BF16)

                @pl.when(b >= 2)
                def _():
                    y_copy(b - 2, slot).wait()

                ybuf[slot] = jnp.zeros(ybuf.shape[1:], F32)

            x = xb[...]
            acc = None
            for c in range(cpu):
                kk = part * cpu + c
                hid = (jax.nn.silu(_dot(x, gcache[wslot, kk])) * _dot(x, ucache[wslot, kk])).astype(BF16)
                y = _dot(hid, dcache[wslot, kk])
                acc = y if acc is None else acc + y
            ybuf[slot] += acc

            @pl.when(part == upb - 1)
            def _():
                y_copy(b, slot).start()

            return carry

        lax.fori_loop((n * upb * k) // nk, (n * upb * (k + 1)) // nk, unit, 0)

    @pl.when((e == N_EXPERTS) & (k == nk - 1))
    def _():
        @pl.when(total >= 2)
        def _():
            y_copy(total - 2, total % 2).wait()

        @pl.when(total >= 1)
        def _():
            y_copy(total - 1, (total - 1) % 2).wait()

        ybuf[0] = jnp.zeros(ybuf.shape[1:], F32)

        def zfill(b, carry):
            y_copy(b, 0).start()
            y_copy(b, 0).wait()
            return carry

        lax.fori_loop(total, n_blocks, zfill, 0)


def _experts(bstart, x_sorted, w_gate, w_up, w_down, kchunk):
    nrows, d = x_sorted.shape
    n_blocks = nrows // EXPERT_SUB
    de = w_gate.shape[2]
    nk = de // kchunk
    last = N_EXPERTS - 1

    def widx(e, k):
        return jnp.minimum(e, last), jnp.where(e <= last, k, nk - 1)

    def in_idx(e, k, b):
        ee, kk = widx(e, k)
        return (ee, 0, kk)

    def down_idx(e, k, b):
        ee, kk = widx(e, k)
        return (ee, kk, 0)

    kern = functools.partial(_expert_kernel, n_blocks=n_blocks)
    return pl.pallas_call(
        kern,
        grid_spec=pltpu.PrefetchScalarGridSpec(
            num_scalar_prefetch=1,
            grid=(N_EXPERTS + 1, nk),
            in_specs=[
                pl.BlockSpec((None, d, kchunk), in_idx),
                pl.BlockSpec((None, d, kchunk), in_idx),
                pl.BlockSpec((None, kchunk, d), down_idx),
                pl.BlockSpec(memory_space=pl.ANY),
            ],
            out_specs=pl.BlockSpec(memory_space=pl.ANY),
            scratch_shapes=[
                pltpu.VMEM((2, nk, d, kchunk), BF16),
                pltpu.VMEM((2, nk, d, kchunk), BF16),
                pltpu.VMEM((2, nk, kchunk, d), BF16),
                pltpu.VMEM((2, EXPERT_SUB, d), F32),
                pltpu.VMEM((2, EXPERT_SUB, d), F32),
                pltpu.VMEM((EXPERT_SUB, d), BF16),
                pltpu.SemaphoreType.DMA((2,)),
                pltpu.SemaphoreType.DMA((2,)),
            ],
        ),
        out_shape=jax.ShapeDtypeStruct((nrows, d), F32),
        compiler_params=_params(("arbitrary", "arbitrary")),
        name="experts",
    )(bstart, w_gate, w_up, w_down, x_sorted)


def _final_kernel(dest_ref, h_ref, rw_ref, ys_ref, g_ref, yp_ref, yss_ref, ya, yb, sem, *,
                  tm, n_prompt_tiles, ts):
    i = pl.program_id(0)

    def gather(tile, n):
        slot = tile % 2

        def issue(r, carry):
            t = tile * tm + r
            _row_copy(ys_ref, ya.at[slot], dest_ref[2 * t], r, sem.at[slot]).start()
            _row_copy(ys_ref, yb.at[slot], dest_ref[2 * t + 1], r, sem.at[slot]).start()
            return carry

        lax.fori_loop(0, n, issue, 0, unroll=8)

    def emit(n, o_ref):
        slot = i % 2
        for buf in (ya, yb):
            pltpu.make_async_copy(ys_ref.at[pl.ds(0, n)], buf.at[slot, pl.ds(0, n)], sem.at[slot]).wait()
        w = rw_ref[pl.ds(0, n), :]
        h = (h_ref[pl.ds(0, n), :] + w[:, 0:1] * ya[slot, pl.ds(0, n), :]
             + w[:, 1:2] * yb[slot, pl.ds(0, n), :])
        o_ref[...] = _rms(h, g_ref[...])

    @pl.when(i == 0)
    def _():
        gather(i, tm)

    @pl.when(i + 1 < n_prompt_tiles)
    def _():
        gather(i + 1, tm)

    @pl.when(i + 1 == n_prompt_tiles)
    def _():
        gather(i + 1, ts)

    @pl.when(i < n_prompt_tiles)
    def _():
        emit(tm, yp_ref)

    @pl.when(i == n_prompt_tiles)
    def _():
        emit(ts, yss_ref)


def _final(dest, h_all, rw, y_sorted, g, tp, ts):
    t, d = h_all.shape
    tm = 256
    assert tp % tm == 0 and ts <= tm and t == tp + ts
    npt = tp // tm
    kern = functools.partial(_final_kernel, tm=tm, n_prompt_tiles=npt, ts=ts)
    return pl.pallas_call(
        kern,
        grid_spec=pltpu.PrefetchScalarGridSpec(
            num_scalar_prefetch=1,
            grid=(npt + 1,),
            in_specs=[
                pl.BlockSpec((tm, d), lambda i, dref: (i, 0)),
                pl.BlockSpec((tm, LANES), lambda i, dref: (i, 0)),
                pl.BlockSpec(memory_space=pl.ANY),
                pl.BlockSpec((1, d), lambda i, dref: (0, 0)),
            ],
            out_specs=[
                pl.BlockSpec((tm, d), lambda i, dref: (jnp.minimum(i, npt - 1), 0)),
                pl.BlockSpec((ts, d), lambda i, dref: (0, 0)),
            ],
            scratch_shapes=[pltpu.VMEM((2, tm, d), F32), pltpu.VMEM((2, tm, d), F32),
                            pltpu.SemaphoreType.DMA((2,))],
        ),
        out_shape=[jax.ShapeDtypeStruct((tp, d), F32), jax.ShapeDtypeStruct((ts, d), F32)],
        compiler_params=_params(("arbitrary",)),
        name="final",
    )(dest, h_all, rw, y_sorted, g)


def _routing_tables(eid):
    i32 = jnp.int32
    onehot = (eid[:, None] == jnp.arange(N_EXPERTS, dtype=i32)[None, :]).astype(i32)
    csum = jnp.cumsum(onehot, axis=0)
    counts = csum[-1]
    nsub = (counts + EXPERT_SUB - 1) // EXPERT_SUB
    bend = jnp.cumsum(nsub)
    bstart = bend - nsub
    seg = bstart * EXPERT_SUB
    dest = jnp.sum(onehot * (csum - 1 + seg[None, :]), axis=1).astype(i32)
    pad0 = (seg + counts).astype(i32)
    padn = (nsub * EXPERT_SUB - counts).astype(i32)
    bstart_all = jnp.concatenate([bstart, bend[-1:]]).astype(i32)
    return (dest, pad0, padn, bend[-1:].astype(i32)), bstart_all


def _group_major(a, width):
    lead = a.shape[:-1]
    g = a.shape[-1] // width
    return jnp.moveaxis(a.reshape(lead + (g, width)), -2, 0)


def kernel(x_prompt, x_sample, cache_mem_k, cache_mem_v, state_conv_a, state_conv_m, state_ssm, mem_prompt,
           norm_mix, w_in, conv_a_w, w_a_out, conv_m_w, conv_m_b, dt_bias, a_log, d_skip, ssm_norm,
           w_m_out, w_o, norm_cross, norm_mem, w_q, w_k, w_v, w_co, norm_ffn, w_rg, b_rg, w_re, b_re,
           w_gate, w_up, w_down, norm_final):
    depth = w_in.shape[0]
    assert depth == 1, "single-layer step"
    nbp, seq_p, d = x_prompt.shape
    nbs, seq_s, _ = x_sample.shape
    tp, ts = nbp * seq_p, nbs * seq_s
    n_mem = mem_prompt.shape[1]
    d_conv = conv_a_w.shape[2]
    d_inner = w_m_out.shape[1]
    nheads = dt_bias.shape[1]
    bc_w = N_GROUPS * D_STATE
    l = 0

    xp = x_prompt.reshape(tp, d)
    xs = x_sample.reshape(ts, d)
    w_in_t = jnp.swapaxes(w_in[l], 0, 1)
    col_z = 3 * d_conv
    col_x = col_z + d_inner
    col_b = col_x + d_inner
    col_dt = col_b + 2 * bc_w
    col_g = col_dt + nheads
    w_dt = w_in_t[col_dt:col_g]

    mk4, mv4 = _memory_kv(mem_prompt.reshape(nbp * n_mem, d), norm_mem[l][None], w_k[l], w_v[l], nbp)

    xnp, xns, dt_p, dt_s = _norm_dt(xp, xs, norm_mix[l][None], w_dt)
    abv_p, abv_s, ca_p, ca_s = _proj_a(xnp, xns, w_in_t, conv_a_w[l], state_conv_a[l], nbp, seq_p, d_conv)
    z_p, z_s = _proj_raw(xnp, xns, w_in_t, col_z, d_inner, GROUP_W)
    cw, cbias, scm = conv_m_w[l], conv_m_b[l][None], state_conv_m[l]
    xc_p, xc_s, cmx_p, cmx_s = _proj_conv(xnp, xns, w_in_t, cw, cbias, scm, col_x, 0, d_inner, GROUP_W,
                                           nbp, seq_p)
    bc_p, bc_s, cmb_p, cmb_s = _proj_conv(xnp, xns, w_in_t, cw, cbias, scm, col_b, d_inner, 2 * bc_w, D_STATE,
                                           nbp, seq_p)
    g_p, g_s = _proj_raw(xnp, xns, w_in_t, col_g, 2 * d, 0)

    prm = {
        "dtb": dt_bias[l][None],
        "alog": a_log[l][None],
        "dsk": _group_major(jnp.repeat(d_skip[l], HEAD_DIM)[None], GROUP_W),
        "gn": _group_major(ssm_norm[l][None], GROUP_W),
    }
    q_p = _pick(seq_p, (128, 64, 32, 16, 8))
    y_p, h_p = _ssd(z_p, xc_p, bc_p, dt_p, prm, nbp, seq_p, q_p)
    y_s, h_s = _ssd(z_s, xc_s, bc_s, dt_s, prm, nbs, seq_s, seq_s,
                    h_prev=state_ssm[l].reshape(nbs, N_GROUPS, GROUP_W, D_STATE))

    merged_p, merged_s = _merge(abv_p, abv_s, y_p, y_s, g_p, g_s, w_a_out[l],
                                w_m_out[l])
    h1_p, h1_s, hn1_p, hn1_s = _res(merged_p, merged_s, xp, xs, w_o[l], norm_cross[l][None], False, BF16)

    att_p = _attn(hn1_p, mk4, mv4, w_q[l], nbp, seq_p)
    att_s = _attn(hn1_s, cache_mem_k[l], cache_mem_v[l], w_q[l], nbs, seq_s)
    h2_all, hn2_all = _res(att_p, att_s, h1_p, h1_s, w_co[l], norm_ffn[l][None], True, F32)

    npad = LANES - N_EXPERT_GROUPS - N_EXPERTS
    w_r = jnp.concatenate([w_rg[l], w_re[l], jnp.zeros((d, npad), F32)], axis=1)
    b_r = jnp.concatenate([b_rg[l], b_re[l], jnp.zeros((npad,), F32)])[None]
    ri, rw = _router(hn2_all, w_r, b_r)
    tables, bstart = _routing_tables(ri[:, :2].reshape(-1))
    n_blocks = -(-2 * (tp + ts) // EXPERT_SUB) + N_EXPERTS
    x_sorted = _dispatch(tables, hn2_all, n_blocks)
    y_sorted = _experts(bstart, x_sorted, w_gate[l], w_up[l], w_down[l], 256)
    y_prompt, y_sample = _final(tables[0], h2_all, rw, y_sorted, norm_final[None], tp, ts)

    xh = d // N_XHEADS
    return (
        y_prompt.reshape(nbp, seq_p, d),
        y_sample.reshape(nbs, seq_s, d),
        mk4[None],
        mv4[None],
        ca_p[None],
        jnp.concatenate([cmx_p, cmb_p], axis=-1)[None],
        h_p.reshape(1, nbp, nheads, HEAD_DIM, D_STATE),
        ca_s[None],
        jnp.concatenate([cmx_s, cmb_s], axis=-1)[None],
        h_s.reshape(1, nbs, nheads, HEAD_DIM, D_STATE),
    )
```

```python
import functools

import jax
import jax.numpy as jnp
from jax import lax
from jax.experimental import pallas as pl
from jax.experimental.pallas import tpu as pltpu

F32 = jnp.float32
BF16 = jnp.bfloat16
EPS = 1e-6
LOG2_E = 1.4426950408889634

V7X_VMEM_BYTES = 64 * 1024 * 1024
VMEM_LIMIT = V7X_VMEM_BYTES - 8 * 1024 * 1024
LANES = 128
SUBLANES = 8

N_GROUPS = 8
HEADS_PER_GROUP = 8
HEAD_DIM = 64
D_STATE = 128
GROUP_W = HEADS_PER_GROUP * HEAD_DIM
N_XHEADS = 4
N_EXPERTS = 32
N_EXPERT_GROUPS = 4
EXPERTS_PER_GROUP = 8
EXPERT_SUB = 256
CONV_A_K = 3
CONV_M_K = 4

NT_DIMS = (((1,), (1,)), ((), ()))
TN_DIMS = (((0,), (0,)), ((), ()))


def _params(sem):
    return pltpu.CompilerParams(dimension_semantics=sem, vmem_limit_bytes=VMEM_LIMIT)


def _pick(n, cands):
    for c in cands:
        if n % c == 0:
            return c
    raise ValueError(f"no tile for {n} in {cands}")


def _dot(a, b):
    return jnp.dot(a, b, preferred_element_type=F32)


def _dot_nt(a, b):
    return lax.dot_general(a, b, NT_DIMS, preferred_element_type=F32)


def _rms(x, g):
    return x * lax.rsqrt(jnp.mean(x * x, axis=-1, keepdims=True) + EPS) * g


def _split3(x):
    hi = x.astype(BF16)
    r = x - hi.astype(F32)
    mid = r.astype(BF16)
    lo = (r - mid.astype(F32)).astype(BF16)
    return hi, mid, lo


def _softplus(x):
    return jnp.maximum(x, 0.0) + jnp.log1p(jnp.exp(-jnp.abs(x)))


def _norm_dt_kernel(xp_ref, xs_ref, g_ref, wdt_ref, xnp_ref, xns_ref, dtp_ref, dts_ref):
    wdt = wdt_ref[...].astype(BF16)

    def one(x_ref, xn_ref, dt_ref):
        xn = _rms(x_ref[...], g_ref[...]).astype(BF16)
        xn_ref[...] = xn
        dt_ref[...] = _dot_nt(xn, wdt)

    one(xp_ref, xnp_ref, dtp_ref)

    @pl.when(pl.program_id(0) == 0)
    def _():
        one(xs_ref, xns_ref, dts_ref)


def _norm_dt(xp, xs, g, wdt):
    tp, d = xp.shape
    ts = xs.shape[0]
    nh = wdt.shape[0]
    tm = _pick(tp, (512, 256, 128))
    return pl.pallas_call(
        _norm_dt_kernel,
        grid=(tp // tm,),
        in_specs=[
            pl.BlockSpec((tm, d), lambda i: (i, 0)),
            pl.BlockSpec((ts, d), lambda i: (0, 0)),
            pl.BlockSpec((1, d), lambda i: (0, 0)),
            pl.BlockSpec((nh, d), lambda i: (0, 0)),
        ],
        out_specs=[
            pl.BlockSpec((tm, d), lambda i: (i, 0)),
            pl.BlockSpec((ts, d), lambda i: (0, 0)),
            pl.BlockSpec((tm, nh), lambda i: (i, 0)),
            pl.BlockSpec((ts, nh), lambda i: (0, 0)),
        ],
        out_shape=[
            jax.ShapeDtypeStruct((tp, d), BF16),
            jax.ShapeDtypeStruct((ts, d), BF16),
            jax.ShapeDtypeStruct((tp, nh), F32),
            jax.ShapeDtypeStruct((ts, nh), F32),
        ],
        compiler_params=_params(("arbitrary",)),
        name="norm_dt",
    )(xp, xs, g, wdt)


def _proj_a_kernel(xp_ref, xs_ref, wb_ref, wc_ref, wh_ref, cw_ref, st_ref,
                   op_ref, os_ref, cap_ref, cas_ref,
                   wbf, ubuf, sbuf, s1buf, s2buf, *, tiles_per_batch, nb_s, l_s):
    i = pl.program_id(1)
    tm = xp_ref.shape[0]
    ts = xs_ref.shape[0]
    cw = cw_ref[...]

    @pl.when(i == 0)
    def _():
        wbf[0] = wb_ref[...].astype(BF16)
        wbf[1] = wc_ref[...].astype(BF16)
        wbf[2] = wh_ref[...].astype(BF16)
        x = xs_ref[...]
        u = _dot_nt(x, wbf[1]) * _dot_nt(x, wbf[2])
        sbuf[pl.ds(0, SUBLANES), :] = jnp.zeros((SUBLANES, u.shape[1]), F32)
        sbuf[pl.ds(SUBLANES, ts), :] = u
        s1buf[...] = jnp.zeros_like(s1buf)
        s2buf[...] = jnp.zeros_like(s2buf)
        for b in range(nb_s):
            s1buf[pl.ds(b * l_s, 1), :] = st_ref[b, pl.ds(1, 1), :]
            s2buf[pl.ds(b * l_s, 1), :] = st_ref[b, pl.ds(0, 1), :]
            s2buf[pl.ds(b * l_s + 1, 1), :] = st_ref[b, pl.ds(1, 1), :]
        rmod = lax.broadcasted_iota(jnp.int32, (ts, 1), 0) % l_s
        prev1 = jnp.where(rmod == 0, s1buf[...], sbuf[pl.ds(SUBLANES - 1, ts), :])
        prev2 = jnp.where(rmod < 2, s2buf[...], sbuf[pl.ds(SUBLANES - 2, ts), :])
        v = prev2 * cw[0:1, :] + prev1 * cw[1:2, :] + u * cw[2:3, :]
        os_ref[...] = (_dot_nt(x, wbf[0]) * v).astype(BF16)
        for b in range(nb_s):
            cas_ref[b] = sbuf[pl.ds(SUBLANES + (b + 1) * l_s - 2, 2), :]

    @pl.when(i % tiles_per_batch == 0)
    def _():
        ubuf[pl.ds(0, SUBLANES), :] = jnp.zeros((SUBLANES, ubuf.shape[1]), F32)

    x = xp_ref[...]
    u = _dot_nt(x, wbf[1]) * _dot_nt(x, wbf[2])
    ubuf[pl.ds(SUBLANES, tm), :] = u
    v = (ubuf[pl.ds(SUBLANES - 2, tm), :] * cw[0:1, :]
         + ubuf[pl.ds(SUBLANES - 1, tm), :] * cw[1:2, :] + u * cw[2:3, :])
    op_ref[...] = (_dot_nt(x, wbf[0]) * v).astype(BF16)
    ubuf[pl.ds(0, SUBLANES), :] = ubuf[pl.ds(tm, SUBLANES), :]

    @pl.when(i % tiles_per_batch == tiles_per_batch - 1)
    def _():
        cap_ref[0] = ubuf[pl.ds(SUBLANES + tm - 2, 2), :]


def _proj_a(xnp, xns, w_in, conv_w, state_s, n_batch_p, seq_p, d_conv):
    tp, d = xnp.shape
    ts = xns.shape[0]
    nb_s = state_s.shape[0]
    l_s = ts // nb_s
    tn = 512
    tm = _pick(seq_p, (1024, 512, 256, 128))
    tpb = seq_p // tm
    ncol = d_conv // tn
    kern = functools.partial(_proj_a_kernel, tiles_per_batch=tpb, nb_s=nb_s, l_s=l_s)
    return pl.pallas_call(
        kern,
        grid=(ncol, tp // tm),
        in_specs=[
            pl.BlockSpec((tm, d), lambda j, i: (i, 0)),
            pl.BlockSpec((ts, d), lambda j, i: (0, 0)),
            pl.BlockSpec((tn, d), lambda j, i: (j, 0)),
            pl.BlockSpec((tn, d), lambda j, i: (j + ncol, 0)),
            pl.BlockSpec((tn, d), lambda j, i: (j + 2 * ncol, 0)),
            pl.BlockSpec((CONV_A_K, tn), lambda j, i: (0, j)),
            pl.BlockSpec((nb_s, CONV_A_K - 1, tn), lambda j, i: (0, 0, j)),
        ],
        out_specs=[
            pl.BlockSpec((tm, tn), lambda j, i: (i, j)),
            pl.BlockSpec((ts, tn), lambda j, i: (0, j)),
            pl.BlockSpec((1, CONV_A_K - 1, tn), lambda j, i: (i // tpb, 0, j)),
            pl.BlockSpec((nb_s, CONV_A_K - 1, tn), lambda j, i: (0, 0, j)),
        ],
        out_shape=[
            jax.ShapeDtypeStruct((tp, d_conv), BF16),
            jax.ShapeDtypeStruct((ts, d_conv), BF16),
            jax.ShapeDtypeStruct((n_batch_p, CONV_A_K - 1, d_conv), F32),
            jax.ShapeDtypeStruct((nb_s, CONV_A_K - 1, d_conv), F32),
        ],
        scratch_shapes=[
            pltpu.VMEM((3, tn, d), BF16),
            pltpu.VMEM((SUBLANES + tm, tn), F32),
            pltpu.VMEM((SUBLANES + ts, tn), F32),
            pltpu.VMEM((ts, tn), F32),
            pltpu.VMEM((ts, tn), F32),
        ],
        compiler_params=_params(("arbitrary", "arbitrary")),
        name="proj_a",
    )(xnp, xns, w_in, w_in, w_in, conv_w, state_s)


def _proj_raw_kernel(xp_ref, xs_ref, w_hbm, op_ref, os_ref, wbf, wstage, wsem, *, nsplit, width, row0):
    j = pl.program_id(0)
    tn = wbf.shape[0]

    def w_copy(jj, slot):
        rows = pl.ds(pl.multiple_of(row0 + jj * tn, SUBLANES), tn)
        return pltpu.make_async_copy(w_hbm.at[rows], wstage.at[slot], wsem.at[slot])

    def emit(x_ref, o_ref):
        acc = _dot_nt(x_ref[...], wbf[...])
        if nsplit == 0:
            o_ref[...] = acc
        else:
            for s in range(nsplit):
                o_ref[s] = acc[:, s * width:(s + 1) * width]

    @pl.when(pl.program_id(1) == 0)
    def _():
        @pl.when(j == 0)
        def _():
            w_copy(0, 0).start()

        w_copy(j, j % 2).wait()

        @pl.when(j + 1 < pl.num_programs(0))
        def _():
            w_copy(j + 1, (j + 1) % 2).start()

        wbf[...] = wstage[j % 2].astype(BF16)
        emit(xs_ref, os_ref)

    emit(xp_ref, op_ref)


def _proj_raw(xnp, xns, w, col0, ncols, width):
    tp, d = xnp.shape
    ts = xns.shape[0]
    tn = 1024
    tm = _pick(tp, (1024, 512, 256, 128))
    assert col0 % SUBLANES == 0 and ncols % tn == 0
    if width == 0:
        nsplit = 0
        out_specs = [pl.BlockSpec((tm, tn), lambda j, i: (i, j)),
                     pl.BlockSpec((ts, tn), lambda j, i: (0, j))]
        out_shape = [jax.ShapeDtypeStruct((tp, ncols), F32), jax.ShapeDtypeStruct((ts, ncols), F32)]
    else:
        nsplit = tn // width
        out_specs = [pl.BlockSpec((nsplit, tm, width), lambda j, i: (j, i, 0)),
                     pl.BlockSpec((nsplit, ts, width), lambda j, i: (j, 0, 0))]
        out_shape = [jax.ShapeDtypeStruct((ncols // width, tp, width), F32),
                     jax.ShapeDtypeStruct((ncols // width, ts, width), F32)]
    kern = functools.partial(_proj_raw_kernel, nsplit=nsplit, width=width, row0=col0)
    return pl.pallas_call(
        kern,
        grid=(ncols // tn, tp // tm),
        in_specs=[
            pl.BlockSpec((tm, d), lambda j, i: (i, 0)),
            pl.BlockSpec((ts, d), lambda j, i: (0, 0)),
            pl.BlockSpec(memory_space=pl.ANY),
        ],
        out_specs=out_specs,
        out_shape=out_shape,
        scratch_shapes=[pltpu.VMEM((tn, d), BF16), pltpu.VMEM((2, tn, d), F32), pltpu.SemaphoreType.DMA((2,))],
        compiler_params=_params(("arbitrary", "arbitrary")),
        name=f"proj_raw_{col0}",
    )(xnp, xns, w)


def _proj_conv_kernel(xp_ref, xs_ref, w_ref, cw_ref, cb_ref, st_ref, op_ref, os_ref, cmp_ref, cms_ref,
                      wbf, ubuf, sbuf, fix, *, tiles_per_batch, nb_s, l_s, nsplit, width):
    i = pl.program_id(1)
    tm = xp_ref.shape[0]
    ts = xs_ref.shape[0]
    tail = CONV_M_K - 1
    cw = cw_ref[...]
    bias = cb_ref[...]

    def store(o_ref, act):
        for s in range(nsplit):
            o_ref[s] = act[:, s * width:(s + 1) * width]

    @pl.when(i == 0)
    def _():
        wbf[...] = w_ref[...].astype(BF16)
        raw = _dot_nt(xs_ref[...], wbf[...])
        sbuf[pl.ds(0, SUBLANES), :] = jnp.zeros((SUBLANES, raw.shape[1]), F32)
        sbuf[pl.ds(SUBLANES, ts), :] = raw
        fix[...] = jnp.zeros_like(fix)
        for b in range(nb_s):
            for back in range(1, tail + 1):
                for m in range(back):
                    fix[back - 1, pl.ds(b * l_s + m, 1), :] = st_ref[b, pl.ds(tail + m - back, 1), :]
        rmod = lax.broadcasted_iota(jnp.int32, (ts, 1), 0) % l_s
        acc = raw * cw[tail:tail + 1, :]
        for back in range(1, tail + 1):
            tap = jnp.where(rmod < back, fix[back - 1], sbuf[pl.ds(SUBLANES - back, ts), :])
            acc = acc + tap * cw[tail - back:tail - back + 1, :]
        store(os_ref, jax.nn.silu(acc + bias))
        for b in range(nb_s):
            cms_ref[b] = sbuf[pl.ds(SUBLANES + (b + 1) * l_s - tail, tail), :]

    @pl.when(i % tiles_per_batch == 0)
    def _():
        ubuf[pl.ds(0, SUBLANES), :] = jnp.zeros((SUBLANES, ubuf.shape[1]), F32)

    raw = _dot_nt(xp_ref[...], wbf[...])
    ubuf[pl.ds(SUBLANES, tm), :] = raw
    acc = raw * cw[tail:tail + 1, :]
    for back in range(1, tail + 1):
        acc = acc + ubuf[pl.ds(SUBLANES - back, tm), :] * cw[tail - back:tail - back + 1, :]
    store(op_ref, jax.nn.silu(acc + bias))
    ubuf[pl.ds(0, SUBLANES), :] = ubuf[pl.ds(tm, SUBLANES), :]

    @pl.when(i % tiles_per_batch == tiles_per_batch - 1)
    def _():
        cmp_ref[0] = ubuf[pl.ds(SUBLANES + tm - tail, tail), :]


def _proj_conv(xnp, xns, w, conv_w, conv_b, state_s, col0, ch0, ncols, width, n_batch_p, seq_p):
    tp, d = xnp.shape
    ts = xns.shape[0]
    nb_s = state_s.shape[0]
    l_s = ts // nb_s
    tail = CONV_M_K - 1
    tn = 1024
    tm = _pick(seq_p, (1024, 512, 256, 128))
    tpb = seq_p // tm
    assert col0 % tn == 0 and ch0 % tn == 0 and ncols % tn == 0 and l_s >= tail
    jb, cb0 = col0 // tn, ch0 // tn
    nsplit = tn // width
    kern = functools.partial(_proj_conv_kernel, tiles_per_batch=tpb, nb_s=nb_s, l_s=l_s,
                             nsplit=nsplit, width=width)
    return pl.pallas_call(
        kern,
        grid=(ncols // tn, tp // tm),
        in_specs=[
            pl.BlockSpec((tm, d), lambda j, i: (i, 0)),
            pl.BlockSpec((ts, d), lambda j, i: (0, 0)),
            pl.BlockSpec((tn, d), lambda j, i: (j + jb, 0)),
            pl.BlockSpec((CONV_M_K, tn), lambda j, i: (0, j + cb0)),
            pl.BlockSpec((1, tn), lambda j, i: (0, j + cb0)),
            pl.BlockSpec((nb_s, tail, tn), lambda j, i: (0, 0, j + cb0)),
        ],
        out_specs=[
            pl.BlockSpec((nsplit, tm, width), lambda j, i: (j, i, 0)),
            pl.BlockSpec((nsplit, ts, width), lambda j, i: (j, 0, 0)),
            pl.BlockSpec((1, tail, tn), lambda j, i: (i // tpb, 0, j)),
            pl.BlockSpec((nb_s, tail, tn), lambda j, i: (0, 0, j)),
        ],
        out_shape=[
            jax.ShapeDtypeStruct((ncols // width, tp, width), F32),
            jax.ShapeDtypeStruct((ncols // width, ts, width), F32),
            jax.ShapeDtypeStruct((n_batch_p, tail, ncols), F32),
            jax.ShapeDtypeStruct((nb_s, tail, ncols), F32),
        ],
        scratch_shapes=[
            pltpu.VMEM((tn, d), BF16),
            pltpu.VMEM((SUBLANES + tm, tn), F32),
            pltpu.VMEM((SUBLANES + ts, tn), F32),
            pltpu.VMEM((tail, ts, tn), F32),
        ],
        compiler_params=_params(("arbitrary", "arbitrary")),
        name=f"proj_conv_{col0}",
    )(xnp, xns, w, conv_w, conv_b, state_s)


def _ssd_kernel(*refs, q, has_state, nchunks):
    (z_ref, xs_ref, b_ref, c_ref, dt_ref, dtb_ref, alog_ref, dsk_ref, gn_ref, *rest) = refs
    if has_state:
        hprev, *rest = rest
    (y_ref, oh, h_s, acg, rowt) = rest
    c = pl.program_id(1)
    nheads = N_GROUPS * HEADS_PER_GROUP

    @pl.when(c == 0)
    def _init():
        if has_state:
            h_s[...] = hprev[...]
        else:
            h_s[...] = jnp.zeros_like(h_s)

    dt = _softplus(dt_ref[...] + dtb_ref[...])
    da = dt * (-jnp.exp(alog_ref[...]))
    ri = lax.broadcasted_iota(jnp.int32, (q, q), 0)
    ci = lax.broadcasted_iota(jnp.int32, (q, q), 1)
    causal = ri >= ci
    tril = jnp.where(causal, 1.0, 0.0).astype(BF16)
    acum = sum(_dot(tril, p) for p in _split3(da))
    eye = jnp.where(lax.broadcasted_iota(jnp.int32, (nheads, nheads), 0)
                    == lax.broadcasted_iota(jnp.int32, (nheads, nheads), 1), 1.0, 0.0).astype(BF16)
    acum2 = acum * LOG2_E
    rowt[0] = sum(lax.dot_general(eye, p, NT_DIMS, preferred_element_type=F32) for p in _split3(acum2))
    rowt[1] = sum(lax.dot_general(eye, p, NT_DIMS, preferred_element_type=F32) for p in _split3(dt))
    wend = jnp.exp(acum[q - 1:q, :] - acum) * dt
    rowt[2] = sum(lax.dot_general(eye, p, NT_DIMS, preferred_element_type=F32) for p in _split3(wend))
    for g in range(N_GROUPS):
        acg[g] = acum2[:, g * HEADS_PER_GROUP:(g + 1) * HEADS_PER_GROUP]

    lane = lax.broadcasted_iota(jnp.int32, (1, LANES), 1)
    rowi = lax.broadcasted_iota(jnp.int32, (LANES, 1), 0)
    half_w = LANES // 2

    def group_body(g, carry):
        xs = xs_ref[g]
        bb = b_ref[g].astype(BF16)
        ccb16 = c_ref[g].astype(BF16)
        cb_ = lax.dot_general(ccb16, bb, NT_DIMS, preferred_element_type=F32)
        ac8 = acg[g]
        dsk = dsk_ref[g]
        lo_half = lane < half_w
        lo_rows = rowi < half_w
        ys = []
        for pair in range(HEADS_PER_GROUP // 2):
            sl = slice(pair * LANES, (pair + 1) * LANES)
            xp = xs[:, sl]
            hp = h_s[g, pl.ds(pair * LANES, LANES), :]
            ms_, acols, wrows = [], [], []
            for r in (2 * pair, 2 * pair + 1):
                head = g * HEADS_PER_GROUP + r
                acol = jnp.broadcast_to(ac8[:, r:r + 1], (q, LANES))
                arow = rowt[0, pl.ds(head, 1), :]
                drow = rowt[1, pl.ds(head, 1), :]
                decay = jnp.exp2(jnp.where(causal, acol[:, :q] - arow, -jnp.inf))
                ms_.append((cb_ * decay * drow).astype(BF16))
                acols.append(acol)
                wrows.append(jnp.broadcast_to(rowt[2, pl.ds(head, 1), :], (half_w, q)))
            xlo = jnp.where(lo_half, xp, 0.0).astype(BF16)
            xhi = jnp.where(lo_half, 0.0, xp).astype(BF16)
            if q % LANES == 0:
                ydiag = _dot(jnp.concatenate(ms_, axis=1), jnp.concatenate([xlo, xhi], axis=0))
            else:
                ydiag = _dot(ms_[0], xlo) + _dot(ms_[1], xhi)
            ea = jnp.where(lo_half, jnp.exp2(acols[0]), jnp.exp2(acols[1]))
            yoff = ea * lax.dot_general(ccb16, hp.astype(BF16), NT_DIMS, preferred_element_type=F32)
            ys.append(dsk[:, sl] * xp + ydiag + yoff)
            xwt = (xp.T * jnp.concatenate(wrows, axis=0)).astype(BF16)
            dlast = [jnp.broadcast_to(jnp.exp2(a[q - 1:q, :]), (LANES, LANES)) for a in acols]
            h_s[g, pl.ds(pair * LANES, LANES), :] = hp * jnp.where(lo_rows, dlast[0], dlast[1]) + _dot(xwt, bb)
        yg = jnp.concatenate(ys, axis=1)
        hh = yg * jax.nn.silu(z_ref[g])
        ms = jnp.mean(hh * hh, axis=-1, keepdims=True)
        y_ref[g] = (hh * lax.rsqrt(ms + EPS) * gn_ref[g]).astype(BF16)
        return carry

    lax.fori_loop(0, N_GROUPS, group_body, 0, unroll=2)

    @pl.when(c == nchunks - 1)
    def _():
        oh[...] = h_s[...]


def _ssd(z, xc, bc, dt_raw, prm, n_batch, seq, q, h_prev=None):
    nchunks = seq // q
    nheads = N_GROUPS * HEADS_PER_GROUP
    has_state = h_prev is not None
    g8 = N_GROUPS

    def tok(first):
        return lambda b, c: (first, b * nchunks + c, 0)

    def const3(b, c):
        return (0, 0, 0)

    in_specs = [
        pl.BlockSpec((g8, q, GROUP_W), tok(0)),
        pl.BlockSpec((g8, q, GROUP_W), tok(0)),
        pl.BlockSpec((g8, q, D_STATE), tok(0)),
        pl.BlockSpec((g8, q, D_STATE), tok(1)),
        pl.BlockSpec((q, nheads), lambda b, c: (b * nchunks + c, 0)),
        pl.BlockSpec((1, nheads), lambda b, c: (0, 0)),
        pl.BlockSpec((1, nheads), lambda b, c: (0, 0)),
        pl.BlockSpec((g8, 1, GROUP_W), const3),
        pl.BlockSpec((g8, 1, GROUP_W), const3),
    ]
    args = [z, xc, bc, bc, dt_raw, prm["dtb"], prm["alog"], prm["dsk"], prm["gn"]]

    def per_batch(shape):
        return pl.BlockSpec((None,) + shape, lambda b, c: (b,) + (0,) * len(shape))

    if has_state:
        in_specs.append(per_batch((g8, GROUP_W, D_STATE)))
        args.append(h_prev)
    t = n_batch * seq
    out_specs = [
        pl.BlockSpec((g8, q, GROUP_W), lambda b, c: (0, b * nchunks + c, 0)),
        per_batch((g8, GROUP_W, D_STATE)),
    ]
    out_shape = [
        jax.ShapeDtypeStruct((g8, t, GROUP_W), BF16),
        jax.ShapeDtypeStruct((n_batch, g8, GROUP_W, D_STATE), F32),
    ]
    scratch = [
        pltpu.VMEM((g8, GROUP_W, D_STATE), F32),
        pltpu.VMEM((g8, q, HEADS_PER_GROUP), F32),
        pltpu.VMEM((3, nheads, q), F32),
    ]
    kern = functools.partial(_ssd_kernel, q=q, has_state=has_state, nchunks=nchunks)
    return pl.pallas_call(
        kern,
        grid=(n_batch, nchunks),
        in_specs=in_specs,
        out_specs=out_specs,
        out_shape=out_shape,
        scratch_shapes=scratch,
        compiler_params=_params(("arbitrary", "arbitrary")),
        name="ssd_state" if has_state else "ssd",
    )(*args)


def _merge_kernel(ap_ref, as_ref, yp_ref, ys_ref, gap_ref, gmp_ref, gas_ref, gms_ref, wa_ref, wm_ref,
                  op_ref, os_ref, wab, wmb):
    def emit(a_ref, y_ref, ga_ref, gm_ref, o_ref):
        oa = _dot(a_ref[...], wab[...])
        y = jnp.concatenate([y_ref[g] for g in range(N_GROUPS)], axis=1)
        om = _dot(y, wmb[...])
        o_ref[...] = (jax.nn.sigmoid(ga_ref[...]) * oa + jax.nn.sigmoid(gm_ref[...]) * om).astype(BF16)

    @pl.when(pl.program_id(1) == 0)
    def _():
        wab[...] = wa_ref[...].astype(BF16)
        wmb[...] = wm_ref[...].astype(BF16)
        emit(as_ref, ys_ref, gas_ref, gms_ref, os_ref)

    emit(ap_ref, yp_ref, gap_ref, gmp_ref, op_ref)


def _merge(abv_p, abv_s, y_p, y_s, g_p, g_s, w_a_out, w_m_out):
    tp, dc = abv_p.shape
    ts = abv_s.shape[0]
    dm = w_a_out.shape[1]
    tn = 512
    tm = _pick(tp, (512, 256, 128))
    ncol = dm // tn
    return pl.pallas_call(
        _merge_kernel,
        grid=(ncol, tp // tm),
        in_specs=[
            pl.BlockSpec((tm, dc), lambda j, i: (i, 0)),
            pl.BlockSpec((ts, dc), lambda j, i: (0, 0)),
            pl.BlockSpec((N_GROUPS, tm, GROUP_W), lambda j, i: (0, i, 0)),
            pl.BlockSpec((N_GROUPS, ts, GROUP_W), lambda j, i: (0, 0, 0)),
            pl.BlockSpec((tm, tn), lambda j, i: (i, j)),
            pl.BlockSpec((tm, tn), lambda j, i: (i, j + ncol)),
            pl.BlockSpec((ts, tn), lambda j, i: (0, j)),
            pl.BlockSpec((ts, tn), lambda j, i: (0, j + ncol)),
            pl.BlockSpec((dc, tn), lambda j, i: (0, j)),
            pl.BlockSpec((N_GROUPS * GROUP_W, tn), lambda j, i: (0, j)),
        ],
        out_specs=[
            pl.BlockSpec((tm, tn), lambda j, i: (i, j)),
            pl.BlockSpec((ts, tn), lambda j, i: (0, j)),
        ],
        out_shape=[jax.ShapeDtypeStruct((tp, dm), BF16), jax.ShapeDtypeStruct((ts, dm), BF16)],
        scratch_shapes=[pltpu.VMEM((dc, tn), BF16), pltpu.VMEM((N_GROUPS * GROUP_W, tn), BF16)],
        compiler_params=_params(("arbitrary", "arbitrary")),
        name="merge",
    )(abv_p, abv_s, y_p, y_s, g_p, g_p, g_s, g_s, w_a_out, w_m_out)


def _res_kernel(ap_ref, as_ref, rp_ref, rs_ref, w_ref, g_ref, *outs, n_prompt_tiles, merged_out, ts):
    if merged_out:
        h_ref, hn_ref, wbf = outs
    else:
        hp_ref, hs_ref, hnp_ref, hns_ref, wbf = outs
    i = pl.program_id(0)

    @pl.when(i == 0)
    def _():
        wbf[...] = w_ref[...].astype(BF16)

    def emit(a_ref, r_ref, store_h, store_hn):
        h = r_ref[...] + _dot(a_ref[...], wbf[...])
        store_h(h)
        store_hn(_rms(h, g_ref[...]))

    if merged_out:
        @pl.when(i < n_prompt_tiles)
        def _():
            def sh(h):
                h_ref[...] = h

            def shn(hn):
                hn_ref[...] = hn

            emit(ap_ref, rp_ref, sh, shn)

        @pl.when(i == n_prompt_tiles)
        def _():
            def sh(h):
                h_ref[pl.ds(0, ts), :] = h

            def shn(hn):
                hn_ref[pl.ds(0, ts), :] = hn

            emit(as_ref, rs_ref, sh, shn)
    else:
        def shp(h):
            hp_ref[...] = h

        def shnp(hn):
            hnp_ref[...] = hn.astype(hnp_ref.dtype)

        emit(ap_ref, rp_ref, shp, shnp)

        @pl.when(i == 0)
        def _():
            def shs(h):
                hs_ref[...] = h

            def shns(hn):
                hns_ref[...] = hn.astype(hns_ref.dtype)

            emit(as_ref, rs_ref, shs, shns)


def _res(a_p, a_s, r_p, r_s, w, g, merged_out, hn_dtype):
    tp, d = a_p.shape
    ts = a_s.shape[0]
    tm = 256
    assert tp % tm == 0 and ts <= tm
    npt = tp // tm
    last = npt - 1
    in_specs = [
        pl.BlockSpec((tm, d), lambda i: (jnp.minimum(i, last), 0)),
        pl.BlockSpec((ts, d), lambda i: (0, 0)),
        pl.BlockSpec((tm, d), lambda i: (jnp.minimum(i, last), 0)),
        pl.BlockSpec((ts, d), lambda i: (0, 0)),
        pl.BlockSpec((d, d), lambda i: (0, 0), pipeline_mode=pl.Buffered(1)),
        pl.BlockSpec((1, d), lambda i: (0, 0)),
    ]
    if merged_out:
        grid = (npt + 1,)
        out_specs = [pl.BlockSpec((tm, d), lambda i: (i, 0)), pl.BlockSpec((tm, d), lambda i: (i, 0))]
        out_shape = [jax.ShapeDtypeStruct((tp + ts, d), F32), jax.ShapeDtypeStruct((tp + ts, d), hn_dtype)]
    else:
        grid = (npt,)
        out_specs = [pl.BlockSpec((tm, d), lambda i: (i, 0)), pl.BlockSpec((ts, d), lambda i: (0, 0)),
                     pl.BlockSpec((tm, d), lambda i: (i, 0)), pl.BlockSpec((ts, d), lambda i: (0, 0))]
        out_shape = [jax.ShapeDtypeStruct((tp, d), F32), jax.ShapeDtypeStruct((ts, d), F32),
                     jax.ShapeDtypeStruct((tp, d), hn_dtype), jax.ShapeDtypeStruct((ts, d), hn_dtype)]
    kern = functools.partial(_res_kernel, n_prompt_tiles=npt, merged_out=merged_out, ts=ts)
    return pl.pallas_call(
        kern,
        grid=grid,
        in_specs=in_specs,
        out_specs=out_specs,
        out_shape=out_shape,
        scratch_shapes=[pltpu.VMEM((d, d), BF16)],
        compiler_params=_params(("arbitrary",)),
        name="res_merged" if merged_out else "res",
    )(a_p, a_s, r_p, r_s, w, g)


def _kv_kernel(m_ref, g_ref, wk_ref, wv_ref, k_hbm, v_hbm, mn, kvb, sem, *, n_batch):
    j = pl.program_id(0)

    @pl.when(j == 0)
    def _():
        mn[...] = _rms(m_ref[...], g_ref[...]).astype(BF16)

    kvb[0] = _dot(mn[...], wk_ref[...].astype(BF16))
    kvb[1] = _dot(mn[...], wv_ref[...].astype(BF16))
    nm = kvb.shape[1] // n_batch
    for h in range(N_XHEADS):
        @pl.when(j == h)
        def _(h=h):
            copies = [pltpu.make_async_copy(kvb.at[t, pl.ds(b * nm, nm)], dst.at[b, :, h, :], sem.at[t, b])
                      for t, dst in enumerate((k_hbm, v_hbm)) for b in range(n_batch)]
            for c in copies:
                c.start()
            for c in copies:
                c.wait()


def _memory_kv(mem2d, g, w_k, w_v, n_batch):
    m, d = mem2d.shape
    dh = d // N_XHEADS
    out_sds = jax.ShapeDtypeStruct((n_batch, m // n_batch, N_XHEADS, dh), F32)
    kern = functools.partial(_kv_kernel, n_batch=n_batch)
    return pl.pallas_call(
        kern,
        grid=(N_XHEADS,),
        in_specs=[
            pl.BlockSpec((m, d), lambda j: (0, 0)),
            pl.BlockSpec((1, d), lambda j: (0, 0)),
            pl.BlockSpec((d, dh), lambda j: (0, j)),
            pl.BlockSpec((d, dh), lambda j: (0, j)),
        ],
        out_specs=[pl.BlockSpec(memory_space=pl.ANY), pl.BlockSpec(memory_space=pl.ANY)],
        out_shape=[out_sds, out_sds],
        scratch_shapes=[pltpu.VMEM((m, d), BF16), pltpu.VMEM((2, m, dh), F32),
                        pltpu.SemaphoreType.DMA((2, n_batch))],
        compiler_params=_params(("arbitrary",)),
        name="memory_kv",
    )(mem2d, g, w_k, w_v)


def _attn_kernel(hn_ref, k_ref, v_ref, wq_ref, o_ref, wqb, kb, vb, kvf, sem):
    b = pl.program_id(0)
    i = pl.program_id(1)

    @pl.when((b == 0) & (i == 0))
    def _():
        wqb[...] = wq_ref[...].astype(BF16)

    d = wqb.shape[1]
    dh = d // N_XHEADS

    @pl.when(i == 0)
    def _():
        def copies(bb):
            slot = bb % 2
            return [pltpu.make_async_copy(src.at[bb, :, h, :], kvf.at[slot, t, h], sem.at[slot, t, h])
                    for t, src in enumerate((k_ref, v_ref)) for h in range(N_XHEADS)]

        @pl.when(b == 0)
        def _():
            for c in copies(b):
                c.start()

        for c in copies(b):
            c.wait()

        @pl.when(b + 1 < pl.num_programs(0))
        def _():
            for c in copies(b + 1):
                c.start()

        for h in range(N_XHEADS):
            kb[:, h * dh:(h + 1) * dh] = kvf[b % 2, 0, h].astype(BF16)
            vb[:, h * dh:(h + 1) * dh] = kvf[b % 2, 1, h].astype(BF16)

    q = _dot(hn_ref[...], wqb[...])
    outs = []
    for h in range(N_XHEADS):
        sl = slice(h * dh, (h + 1) * dh)
        s = lax.dot_general(q[:, sl].astype(BF16), kb[:, sl], NT_DIMS, preferred_element_type=F32)
        s = s * (dh ** -0.5)
        e = jnp.exp(s - jnp.max(s, axis=-1, keepdims=True))
        p = e / jnp.sum(e, axis=-1, keepdims=True)
        outs.append(_dot(p.astype(BF16), vb[:, sl]))
    o_ref[...] = jnp.concatenate(outs, axis=1).astype(BF16)


def _attn(hn, k, v, w_q, n_batch, seq):
    t, d = hn.shape
    nm, nh, dh = k.shape[1:]
    tm = _pick(seq, (512, 256, 128, 64, 32, 16))
    tpb = seq // tm
    kv_spec = pl.BlockSpec(memory_space=pl.ANY)
    scratch = [pltpu.VMEM((d, d), BF16), pltpu.VMEM((nm, d), BF16), pltpu.VMEM((nm, d), BF16),
               pltpu.VMEM((2, 2, nh, nm, dh), F32), pltpu.SemaphoreType.DMA((2, 2, nh))]
    return pl.pallas_call(
        _attn_kernel,
        grid=(n_batch, tpb),
        in_specs=[
            pl.BlockSpec((tm, d), lambda b, i: (b * tpb + i, 0)),
            kv_spec,
            kv_spec,
            pl.BlockSpec((d, d), lambda b, i: (0, 0), pipeline_mode=pl.Buffered(1)),
        ],
        out_specs=pl.BlockSpec((tm, d), lambda b, i: (b * tpb + i, 0)),
        out_shape=jax.ShapeDtypeStruct((t, d), BF16),
        scratch_shapes=scratch,
        compiler_params=_params(("arbitrary", "arbitrary")),
        name=f"attn_{seq}",
    )(hn, k, v, w_q)


def _router_kernel(x_ref, w_ref, b_ref, ri_ref, rw_ref):
    logits = _dot(x_ref[...].astype(BF16), w_ref[...].astype(BF16)) + b_ref[...]
    lane_i = lax.broadcasted_iota(jnp.int32, logits.shape, 1)
    lane = lane_i.astype(F32)
    ninf = -jnp.inf
    big = float(LANES)
    is_g = lane < N_EXPERT_GROUPS
    gl = jnp.where(is_g, logits, ninf)
    gmax = jnp.max(gl, axis=-1, keepdims=True)
    gsel = jnp.min(jnp.where(gl == gmax, lane, big), axis=-1, keepdims=True)
    pg = 1.0 / jnp.sum(jnp.where(is_g, jnp.exp(gl - gmax), 0.0), axis=-1, keepdims=True)
    lo = N_EXPERT_GROUPS + EXPERTS_PER_GROUP * gsel
    el = jnp.where(lane >= lo, jnp.where(lane < lo + EXPERTS_PER_GROUP, logits, ninf), ninf)
    m1 = jnp.max(el, axis=-1, keepdims=True)
    i1 = jnp.min(jnp.where(el == m1, lane, big), axis=-1, keepdims=True)
    el2 = jnp.where(lane == i1, ninf, el)
    m2 = jnp.max(el2, axis=-1, keepdims=True)
    i2 = jnp.min(jnp.where(el2 == m2, lane, big), axis=-1, keepdims=True)
    e = jnp.exp(m2 - m1)
    w1 = pg / (1.0 + e)
    w2 = pg * e / (1.0 + e)
    ri_ref[...] = jnp.where(lane_i == 0, i1 - N_EXPERT_GROUPS,
                            jnp.where(lane_i == 1, i2 - N_EXPERT_GROUPS, 0.0)).astype(jnp.int32)
    rw_ref[...] = jnp.where(lane_i == 0, w1, jnp.where(lane_i == 1, w2, 0.0))


def _router(hn_all, w_r, b_r):
    t, d = hn_all.shape
    tm = _pick(t, (640, 512, 384, 256, 128))
    return pl.pallas_call(
        _router_kernel,
        grid=(t // tm,),
        in_specs=[
            pl.BlockSpec((tm, d), lambda i: (i, 0)),
            pl.BlockSpec((d, LANES), lambda i: (0, 0)),
            pl.BlockSpec((1, LANES), lambda i: (0, 0)),
        ],
        out_specs=[pl.BlockSpec((tm, LANES), lambda i: (i, 0)), pl.BlockSpec((tm, LANES), lambda i: (i, 0))],
        out_shape=[jax.ShapeDtypeStruct((t, LANES), jnp.int32), jax.ShapeDtypeStruct((t, LANES), F32)],
        compiler_params=_params(("arbitrary",)),
        name="router",
    )(hn_all, w_r, b_r)


def _row_copy(src, dst, s, d, sem):
    return pltpu.make_async_copy(src.at[pl.ds(s, 1)], dst.at[pl.ds(d, 1)], sem)


def _dispatch_kernel(dest_ref, pad0_ref, padn_ref, nsub_ref, x_ref, o_ref, zrow, zblk, sem, zsem, csem, bsem, *,
                     chunk, n_blocks):
    base = pl.program_id(0) * chunk

    @pl.when(pl.program_id(0) == 0)
    def _():
        zrow[...] = jnp.zeros_like(zrow)
        zblk[...] = jnp.zeros_like(zblk)

        def pad_plan(e):
            p0 = pad0_ref[e]
            head = jnp.minimum(padn_ref[e], (SUBLANES - p0 % SUBLANES) % SUBLANES)
            return p0, head, (padn_ref[e] - head) // SUBLANES

        def group_copy(row):
            return pltpu.make_async_copy(zrow, o_ref.at[pl.ds(pl.multiple_of(row, SUBLANES), SUBLANES)], csem)

        def pad_start(e, carry):
            p0, head, ngroups = pad_plan(e)

            def start_row(r, c):
                _row_copy(zrow, o_ref, 0, p0 + r, zsem).start()
                return c

            def start_group(j, c):
                group_copy(p0 + head + j * SUBLANES).start()
                return c

            lax.fori_loop(0, head, start_row, 0)
            return lax.fori_loop(0, ngroups, start_group, carry)

        def pad_wait(e, carry):
            _, head, ngroups = pad_plan(e)

            def wait_row(r, c):
                _row_copy(zrow, o_ref, 0, 0, zsem).wait()
                return c

            def wait_group(j, c):
                group_copy(0).wait()
                return c

            lax.fori_loop(0, head, wait_row, 0)
            return lax.fori_loop(0, ngroups, wait_group, carry)

        lax.fori_loop(0, N_EXPERTS, pad_start, 0)

        def blk_copy(b):
            return pltpu.make_async_copy(zblk, o_ref.at[pl.ds(pl.multiple_of(b * EXPERT_SUB, EXPERT_SUB),
                                                             EXPERT_SUB)], bsem)

        def tail_start(b, c):
            blk_copy(b).start()
            return c

        def tail_wait(b, c):
            blk_copy(b).wait()
            return c

        lax.fori_loop(nsub_ref[0], n_blocks, tail_start, 0)
        lax.fori_loop(0, N_EXPERTS, pad_wait, 0)
        lax.fori_loop(nsub_ref[0], n_blocks, tail_wait, 0)

    def issue(r, carry):
        t = base + r
        _row_copy(x_ref, o_ref, r, dest_ref[2 * t], sem).start()
        _row_copy(x_ref, o_ref, r, dest_ref[2 * t + 1], sem).start()
        return carry

    lax.fori_loop(0, chunk, issue, 0, unroll=8)

    for _ in range(2):
        pltpu.make_async_copy(x_ref, o_ref.at[pl.ds(0, chunk)], sem).wait()


def _dispatch(tables, hn_all, n_blocks):
    dest, pad0, padn, nsub = tables
    t, d = hn_all.shape
    chunk = _pick(t, (640, 512, 384, 256, 128))
    kern = functools.partial(_dispatch_kernel, chunk=chunk, n_blocks=n_blocks)
    return pl.pallas_call(
        kern,
        grid_spec=pltpu.PrefetchScalarGridSpec(
            num_scalar_prefetch=4,
            grid=(t // chunk,),
            in_specs=[pl.BlockSpec((chunk, d), lambda i, *_: (i, 0))],
            out_specs=pl.BlockSpec(memory_space=pl.ANY),
            scratch_shapes=[pltpu.VMEM((SUBLANES, d), F32), pltpu.VMEM((EXPERT_SUB, d), F32),
                            pltpu.SemaphoreType.DMA(()), pltpu.SemaphoreType.DMA(()),
                            pltpu.SemaphoreType.DMA(()), pltpu.SemaphoreType.DMA(())],
        ),
        out_shape=jax.ShapeDtypeStruct((n_blocks * EXPERT_SUB, d), F32),
        compiler_params=_params(("arbitrary",)),
        name="dispatch",
    )(dest, pad0, padn, nsub, hn_all)


def _expert_kernel(bstart_ref, wg_ref, wu_ref, wd_ref, x_hbm, y_hbm, gcache, ucache, dcache, xbuf, ybuf, xb,
                   xsem, ysem, *, n_blocks):
    e = pl.program_id(0)
    k = pl.program_id(1)
    nk = gcache.shape[1]
    total = bstart_ref[N_EXPERTS]

    def x_copy(b, slot):
        rows = pl.ds(pl.multiple_of(b * EXPERT_SUB, EXPERT_SUB), EXPERT_SUB)
        return pltpu.make_async_copy(x_hbm.at[rows], xbuf.at[slot], xsem.at[slot])

    def y_copy(b, slot):
        rows = pl.ds(pl.multiple_of(b * EXPERT_SUB, EXPERT_SUB), EXPERT_SUB)
        return pltpu.make_async_copy(ybuf.at[slot], y_hbm.at[rows], ysem.at[slot])

    @pl.when(e < N_EXPERTS)
    def _():
        slot = e % 2
        gcache[slot, k] = wg_ref[...].astype(BF16)
        ucache[slot, k] = wu_ref[...].astype(BF16)
        dcache[slot, k] = wd_ref[...].astype(BF16)

    @pl.when((e == 0) & (k == 0) & (total > 0))
    def _():
        x_copy(0, 0).start()

    @pl.when(e >= 1)
    def _():
        owner = e - 1
        wslot = owner % 2
        b0 = bstart_ref[owner]
        n = bstart_ref[owner + 1] - b0

        cpu = 2
        upb = nk // cpu

        def unit(u, carry):
            b = b0 + u // upb
            part = u % upb
            slot = b % 2

            @pl.when(part == 0)
            def _():
                x_copy(b, slot).wait()

                @pl.when(b + 1 < total)
                def _():
                    x_copy(b + 1, 1 - slot).start()

                xb[...] = xbuf[slot].astype(BF16)

                @pl.when(b >= 2)
                def _():
                    y_copy(b - 2, slot).wait()

                ybuf[slot] = jnp.zeros(ybuf.shape[1:], F32)

            x = xb[...]
            kks = [part * cpu + c for c in range(cpu)]
            wg = jnp.concatenate([gcache[wslot, kk] for kk in kks], axis=1)
            wu = jnp.concatenate([ucache[wslot, kk] for kk in kks], axis=1)
            wd = jnp.concatenate([dcache[wslot, kk] for kk in kks], axis=0)
            hid = (jax.nn.silu(_dot(x, wg)) * _dot(x, wu)).astype(BF16)
            ybuf[slot] += _dot(hid, wd)

            @pl.when(part == upb - 1)
            def _():
                y_copy(b, slot).start()

            return carry

        lax.fori_loop((n * upb * k) // nk, (n * upb * (k + 1)) // nk, unit, 0)

    @pl.when((e == N_EXPERTS) & (k == nk - 1))
    def _():
        @pl.when(total >= 2)
        def _():
            y_copy(total - 2, total % 2).wait()

        @pl.when(total >= 1)
        def _():
            y_copy(total - 1, (total - 1) % 2).wait()

        ybuf[0] = jnp.zeros(ybuf.shape[1:], F32)

        def zfill(b, carry):
            y_copy(b, 0).start()
            y_copy(b, 0).wait()
            return carry

        lax.fori_loop(total, n_blocks, zfill, 0)


def _experts(bstart, x_sorted, w_gate, w_up, w_down, kchunk):
    nrows, d = x_sorted.shape
    n_blocks = nrows // EXPERT_SUB
    de = w_gate.shape[2]
    nk = de // kchunk
    last = N_EXPERTS - 1

    def widx(e, k):
        return jnp.minimum(e, last), jnp.where(e <= last, k, nk - 1)

    def in_idx(e, k, b):
        ee, kk = widx(e, k)
        return (ee, 0, kk)

    def down_idx(e, k, b):
        ee, kk = widx(e, k)
        return (ee, kk, 0)

    kern = functools.partial(_expert_kernel, n_blocks=n_blocks)
    return pl.pallas_call(
        kern,
        grid_spec=pltpu.PrefetchScalarGridSpec(
            num_scalar_prefetch=1,
            grid=(N_EXPERTS + 1, nk),
            in_specs=[
                pl.BlockSpec((None, d, kchunk), in_idx),
                pl.BlockSpec((None, d, kchunk), in_idx),
                pl.BlockSpec((None, kchunk, d), down_idx),
                pl.BlockSpec(memory_space=pl.ANY),
            ],
            out_specs=pl.BlockSpec(memory_space=pl.ANY),
            scratch_shapes=[
                pltpu.VMEM((2, nk, d, kchunk), BF16),
                pltpu.VMEM((2, nk, d, kchunk), BF16),
                pltpu.VMEM((2, nk, kchunk, d), BF16),
                pltpu.VMEM((2, EXPERT_SUB, d), F32),
                pltpu.VMEM((2, EXPERT_SUB, d), F32),
                pltpu.VMEM((EXPERT_SUB, d), BF16),
                pltpu.SemaphoreType.DMA((2,)),
                pltpu.SemaphoreType.DMA((2,)),
            ],
        ),
        out_shape=jax.ShapeDtypeStruct((nrows, d), F32),
        compiler_params=_params(("arbitrary", "arbitrary")),
        name="experts",
    )(bstart, w_gate, w_up, w_down, x_sorted)


def _final_kernel(dest_ref, h_ref, rw_ref, ys_ref, g_ref, yp_ref, yss_ref, ya, yb, sem, *,
                  tm, n_prompt_tiles, ts):
    i = pl.program_id(0)

    def gather(tile, n):
        slot = tile % 2

        def issue(r, carry):
            t = tile * tm + r
            _row_copy(ys_ref, ya.at[slot], dest_ref[2 * t], r, sem.at[slot]).start()
            _row_copy(ys_ref, yb.at[slot], dest_ref[2 * t + 1], r, sem.at[slot]).start()
            return carry

        lax.fori_loop(0, n, issue, 0, unroll=8)

    def emit(n, o_ref):
        slot = i % 2
        for buf in (ya, yb):
            pltpu.make_async_copy(ys_ref.at[pl.ds(0, n)], buf.at[slot, pl.ds(0, n)], sem.at[slot]).wait()
        w = rw_ref[pl.ds(0, n), :]
        h = (h_ref[pl.ds(0, n), :] + w[:, 0:1] * ya[slot, pl.ds(0, n), :]
             + w[:, 1:2] * yb[slot, pl.ds(0, n), :])
        o_ref[...] = _rms(h, g_ref[...])

    @pl.when(i == 0)
    def _():
        gather(i, tm)

    @pl.when(i + 1 < n_prompt_tiles)
    def _():
        gather(i + 1, tm)

    @pl.when(i + 1 == n_prompt_tiles)
    def _():
        gather(i + 1, ts)

    @pl.when(i < n_prompt_tiles)
    def _():
        emit(tm, yp_ref)

    @pl.when(i == n_prompt_tiles)
    def _():
        emit(ts, yss_ref)


def _final(dest, h_all, rw, y_sorted, g, tp, ts):
    t, d = h_all.shape
    tm = 256
    assert tp % tm == 0 and ts <= tm and t == tp + ts
    npt = tp // tm
    kern = functools.partial(_final_kernel, tm=tm, n_prompt_tiles=npt, ts=ts)
    return pl.pallas_call(
        kern,
        grid_spec=pltpu.PrefetchScalarGridSpec(
            num_scalar_prefetch=1,
            grid=(npt + 1,),
            in_specs=[
                pl.BlockSpec((tm, d), lambda i, dref: (i, 0)),
                pl.BlockSpec((tm, LANES), lambda i, dref: (i, 0)),
                pl.BlockSpec(memory_space=pl.ANY),
                pl.BlockSpec((1, d), lambda i, dref: (0, 0)),
            ],
            out_specs=[
                pl.BlockSpec((tm, d), lambda i, dref: (jnp.minimum(i, npt - 1), 0)),
                pl.BlockSpec((ts, d), lambda i, dref: (0, 0)),
            ],
            scratch_shapes=[pltpu.VMEM((2, tm, d), F32), pltpu.VMEM((2, tm, d), F32),
                            pltpu.SemaphoreType.DMA((2,))],
        ),
        out_shape=[jax.ShapeDtypeStruct((tp, d), F32), jax.ShapeDtypeStruct((ts, d), F32)],
        compiler_params=_params(("arbitrary",)),
        name="final",
    )(dest, h_all, rw, y_sorted, g)


def _routing_tables(eid):
    i32 = jnp.int32
    onehot = (eid[:, None] == jnp.arange(N_EXPERTS, dtype=i32)[None, :]).astype(i32)
    csum = jnp.cumsum(onehot, axis=0)
    counts = csum[-1]
    nsub = (counts + EXPERT_SUB - 1) // EXPERT_SUB
    bend = jnp.cumsum(nsub)
    bstart = bend - nsub
    seg = bstart * EXPERT_SUB
    dest = jnp.sum(onehot * (csum - 1 + seg[None, :]), axis=1).astype(i32)
    pad0 = (seg + counts).astype(i32)
    padn = (nsub * EXPERT_SUB - counts).astype(i32)
    bstart_all = jnp.concatenate([bstart, bend[-1:]]).astype(i32)
    return (dest, pad0, padn, bend[-1:].astype(i32)), bstart_all


def _group_major(a, width):
    lead = a.shape[:-1]
    g = a.shape[-1] // width
    return jnp.moveaxis(a.reshape(lead + (g, width)), -2, 0)


def kernel(x_prompt, x_sample, cache_mem_k, cache_mem_v, state_conv_a, state_conv_m, state_ssm, mem_prompt,
           norm_mix, w_in, conv_a_w, w_a_out, conv_m_w, conv_m_b, dt_bias, a_log, d_skip, ssm_norm,
           w_m_out, w_o, norm_cross, norm_mem, w_q, w_k, w_v, w_co, norm_ffn, w_rg, b_rg, w_re, b_re,
           w_gate, w_up, w_down, norm_final):
    depth = w_in.shape[0]
    assert depth == 1, "single-layer step"
    nbp, seq_p, d = x_prompt.shape
    nbs, seq_s, _ = x_sample.shape
    tp, ts = nbp * seq_p, nbs * seq_s
    n_mem = mem_prompt.shape[1]
    d_conv = conv_a_w.shape[2]
    d_inner = w_m_out.shape[1]
    nheads = dt_bias.shape[1]
    bc_w = N_GROUPS * D_STATE
    l = 0

    xp = x_prompt.reshape(tp, d)
    xs = x_sample.reshape(ts, d)
    w_in_t = jnp.swapaxes(w_in[l], 0, 1)
    col_z = 3 * d_conv
    col_x = col_z + d_inner
    col_b = col_x + d_inner
    col_dt = col_b + 2 * bc_w
    col_g = col_dt + nheads
    w_dt = w_in_t[col_dt:col_g]

    mk4, mv4 = _memory_kv(mem_prompt.reshape(nbp * n_mem, d), norm_mem[l][None], w_k[l], w_v[l], nbp)

    xnp, xns, dt_p, dt_s = _norm_dt(xp, xs, norm_mix[l][None], w_dt)
    abv_p, abv_s, ca_p, ca_s = _proj_a(xnp, xns, w_in_t, conv_a_w[l], state_conv_a[l], nbp, seq_p, d_conv)
    z_p, z_s = _proj_raw(xnp, xns, w_in_t, col_z, d_inner, GROUP_W)
    cw, cbias, scm = conv_m_w[l], conv_m_b[l][None], state_conv_m[l]
    xc_p, xc_s, cmx_p, cmx_s = _proj_conv(xnp, xns, w_in_t, cw, cbias, scm, col_x, 0, d_inner, GROUP_W,
                                           nbp, seq_p)
    bc_p, bc_s, cmb_p, cmb_s = _proj_conv(xnp, xns, w_in_t, cw, cbias, scm, col_b, d_inner, 2 * bc_w, D_STATE,
                                           nbp, seq_p)
    g_p, g_s = _proj_raw(xnp, xns, w_in_t, col_g, 2 * d, 0)

    prm = {
        "dtb": dt_bias[l][None],
        "alog": a_log[l][None],
        "dsk": _group_major(jnp.repeat(d_skip[l], HEAD_DIM)[None], GROUP_W),
        "gn": _group_major(ssm_norm[l][None], GROUP_W),
    }
    q_p = _pick(seq_p, (128, 64, 32, 16, 8))
    y_p, h_p = _ssd(z_p, xc_p, bc_p, dt_p, prm, nbp, seq_p, q_p)
    y_s, h_s = _ssd(z_s, xc_s, bc_s, dt_s, prm, nbs, seq_s, seq_s,
                    h_prev=state_ssm[l].reshape(nbs, N_GROUPS, GROUP_W, D_STATE))

    merged_p, merged_s = _merge(abv_p, abv_s, y_p, y_s, g_p, g_s, w_a_out[l],
                                w_m_out[l])
    h1_p, h1_s, hn1_p, hn1_s = _res(merged_p, merged_s, xp, xs, w_o[l], norm_cross[l][None], False, BF16)

    att_p = _attn(hn1_p, mk4, mv4, w_q[l], nbp, seq_p)
    att_s = _attn(hn1_s, cache_mem_k[l], cache_mem_v[l], w_q[l], nbs, seq_s)
    h2_all, hn2_all = _res(att_p, att_s, h1_p, h1_s, w_co[l], norm_ffn[l][None], True, F32)

    npad = LANES - N_EXPERT_GROUPS - N_EXPERTS
    w_r = jnp.concatenate([w_rg[l], w_re[l], jnp.zeros((d, npad), F32)], axis=1)
    b_r = jnp.concatenate([b_rg[l], b_re[l], jnp.zeros((npad,), F32)])[None]
    ri, rw = _router(hn2_all, w_r, b_r)
    tables, bstart = _routing_tables(ri[:, :2].reshape(-1))
    n_blocks = -(-2 * (tp + ts) // EXPERT_SUB) + N_EXPERTS
    x_sorted = _dispatch(tables, hn2_all, n_blocks)
    y_sorted = _experts(bstart, x_sorted, w_gate[l], w_up[l], w_down[l], 256)
    y_prompt, y_sample = _final(tables[0], h2_all, rw, y_sorted, norm_final[None], tp, ts)

    xh = d // N_XHEADS
    return (
        y_prompt.reshape(nbp, seq_p, d),
        y_sample.reshape(nbs, seq_s, d),
        mk4[None],
        mv4[None],
        ca_p[None],
        jnp.concatenate([cmx_p, cmb_p], axis=-1)[None],
        h_p.reshape(1, nbp, nheads, HEAD_DIM, D_STATE),
        ca_s[None],
        jnp.concatenate([cmx_s, cmb_s], axis=-1)[None],
        h_s.reshape(1, nbs, nheads, HEAD_DIM, D_STATE),
    )
```

```python
import functools

import jax
import jax.numpy as jnp
from jax import lax
from jax.experimental import pallas as pl
from jax.experimental.pallas import tpu as pltpu

F32 = jnp.float32
BF16 = jnp.bfloat16
EPS = 1e-6
LOG2_E = 1.4426950408889634

V7X_VMEM_BYTES = 64 * 1024 * 1024
VMEM_LIMIT = V7X_VMEM_BYTES - 8 * 1024 * 1024
LANES = 128
SUBLANES = 8

N_GROUPS = 8
HEADS_PER_GROUP = 8
HEAD_DIM = 64
D_STATE = 128
GROUP_W = HEADS_PER_GROUP * HEAD_DIM
N_XHEADS = 4
N_EXPERTS = 32
N_EXPERT_GROUPS = 4
EXPERTS_PER_GROUP = 8
EXPERT_SUB = 256
CONV_A_K = 3
CONV_M_K = 4

NT_DIMS = (((1,), (1,)), ((), ()))
TN_DIMS = (((0,), (0,)), ((), ()))


def _params(sem):
    return pltpu.CompilerParams(dimension_semantics=sem, vmem_limit_bytes=VMEM_LIMIT)


def _pick(n, cands):
    for c in cands:
        if n % c == 0:
            return c
    raise ValueError(f"no tile for {n} in {cands}")


def _dot(a, b):
    return jnp.dot(a, b, preferred_element_type=F32)


def _dot_nt(a, b):
    return lax.dot_general(a, b, NT_DIMS, preferred_element_type=F32)


def _rms(x, g):
    return x * lax.rsqrt(jnp.mean(x * x, axis=-1, keepdims=True) + EPS) * g


def _split3(x):
    hi = x.astype(BF16)
    r = x - hi.astype(F32)
    mid = r.astype(BF16)
    lo = (r - mid.astype(F32)).astype(BF16)
    return hi, mid, lo


def _softplus(x):
    return jnp.maximum(x, 0.0) + jnp.log1p(jnp.exp(-jnp.abs(x)))


def _norm_dt_kernel(xp_ref, xs_ref, g_ref, wdt_ref, xnp_ref, xns_ref, dtp_ref, dts_ref):
    wdt = wdt_ref[...].astype(BF16)

    def one(x_ref, xn_ref, dt_ref):
        xn = _rms(x_ref[...], g_ref[...]).astype(BF16)
        xn_ref[...] = xn
        dt_ref[...] = _dot_nt(xn, wdt)

    one(xp_ref, xnp_ref, dtp_ref)

    @pl.when(pl.program_id(0) == 0)
    def _():
        one(xs_ref, xns_ref, dts_ref)


def _norm_dt(xp, xs, g, wdt):
    tp, d = xp.shape
    ts = xs.shape[0]
    nh = wdt.shape[0]
    tm = _pick(tp, (512, 256, 128))
    return pl.pallas_call(
        _norm_dt_kernel,
        grid=(tp // tm,),
        in_specs=[
            pl.BlockSpec((tm, d), lambda i: (i, 0)),
            pl.BlockSpec((ts, d), lambda i: (0, 0)),
            pl.BlockSpec((1, d), lambda i: (0, 0)),
            pl.BlockSpec((nh, d), lambda i: (0, 0)),
        ],
        out_specs=[
            pl.BlockSpec((tm, d), lambda i: (i, 0)),
            pl.BlockSpec((ts, d), lambda i: (0, 0)),
            pl.BlockSpec((tm, nh), lambda i: (i, 0)),
            pl.BlockSpec((ts, nh), lambda i: (0, 0)),
        ],
        out_shape=[
            jax.ShapeDtypeStruct((tp, d), BF16),
            jax.ShapeDtypeStruct((ts, d), BF16),
            jax.ShapeDtypeStruct((tp, nh), F32),
            jax.ShapeDtypeStruct((ts, nh), F32),
        ],
        compiler_params=_params(("arbitrary",)),
        name="norm_dt",
    )(xp, xs, g, wdt)


def _proj_a_kernel(xp_ref, xs_ref, wb_ref, wc_ref, wh_ref, cw_ref, st_ref,
                   op_ref, os_ref, cap_ref, cas_ref,
                   wbf, ubuf, sbuf, s1buf, s2buf, *, tiles_per_batch, nb_s, l_s):
    i = pl.program_id(1)
    tm = xp_ref.shape[0]
    ts = xs_ref.shape[0]
    cw = cw_ref[...]

    @pl.when(i == 0)
    def _():
        wbf[0] = wb_ref[...].astype(BF16)
        wbf[1] = wc_ref[...].astype(BF16)
        wbf[2] = wh_ref[...].astype(BF16)
        x = xs_ref[...]
        u = _dot_nt(x, wbf[1]) * _dot_nt(x, wbf[2])
        sbuf[pl.ds(0, SUBLANES), :] = jnp.zeros((SUBLANES, u.shape[1]), F32)
        sbuf[pl.ds(SUBLANES, ts), :] = u
        s1buf[...] = jnp.zeros_like(s1buf)
        s2buf[...] = jnp.zeros_like(s2buf)
        for b in range(nb_s):
            s1buf[pl.ds(b * l_s, 1), :] = st_ref[b, pl.ds(1, 1), :]
            s2buf[pl.ds(b * l_s, 1), :] = st_ref[b, pl.ds(0, 1), :]
            s2buf[pl.ds(b * l_s + 1, 1), :] = st_ref[b, pl.ds(1, 1), :]
        rmod = lax.broadcasted_iota(jnp.int32, (ts, 1), 0) % l_s
        prev1 = jnp.where(rmod == 0, s1buf[...], sbuf[pl.ds(SUBLANES - 1, ts), :])
        prev2 = jnp.where(rmod < 2, s2buf[...], sbuf[pl.ds(SUBLANES - 2, ts), :])
        v = prev2 * cw[0:1, :] + prev1 * cw[1:2, :] + u * cw[2:3, :]
        os_ref[...] = (_dot_nt(x, wbf[0]) * v).astype(BF16)
        for b in range(nb_s):
            cas_ref[b] = sbuf[pl.ds(SUBLANES + (b + 1) * l_s - 2, 2), :]

    @pl.when(i % tiles_per_batch == 0)
    def _():
        ubuf[pl.ds(0, SUBLANES), :] = jnp.zeros((SUBLANES, ubuf.shape[1]), F32)

    x = xp_ref[...]
    u = _dot_nt(x, wbf[1]) * _dot_nt(x, wbf[2])
    ubuf[pl.ds(SUBLANES, tm), :] = u
    v = (ubuf[pl.ds(SUBLANES - 2, tm), :] * cw[0:1, :]
         + ubuf[pl.ds(SUBLANES - 1, tm), :] * cw[1:2, :] + u * cw[2:3, :])
    op_ref[...] = (_dot_nt(x, wbf[0]) * v).astype(BF16)
    ubuf[pl.ds(0, SUBLANES), :] = ubuf[pl.ds(tm, SUBLANES), :]

    @pl.when(i % tiles_per_batch == tiles_per_batch - 1)
    def _():
        cap_ref[0] = ubuf[pl.ds(SUBLANES + tm - 2, 2), :]


def _proj_a(xnp, xns, w_in, conv_w, state_s, n_batch_p, seq_p, d_conv):
    tp, d = xnp.shape
    ts = xns.shape[0]
    nb_s = state_s.shape[0]
    l_s = ts // nb_s
    tn = 512
    tm = _pick(seq_p, (1024, 512, 256, 128))
    tpb = seq_p // tm
    ncol = d_conv // tn
    kern = functools.partial(_proj_a_kernel, tiles_per_batch=tpb, nb_s=nb_s, l_s=l_s)
    return pl.pallas_call(
        kern,
        grid=(ncol, tp // tm),
        in_specs=[
            pl.BlockSpec((tm, d), lambda j, i: (i, 0)),
            pl.BlockSpec((ts, d), lambda j, i: (0, 0)),
            pl.BlockSpec((tn, d), lambda j, i: (j, 0)),
            pl.BlockSpec((tn, d), lambda j, i: (j + ncol, 0)),
            pl.BlockSpec((tn, d), lambda j, i: (j + 2 * ncol, 0)),
            pl.BlockSpec((CONV_A_K, tn), lambda j, i: (0, j)),
            pl.BlockSpec((nb_s, CONV_A_K - 1, tn), lambda j, i: (0, 0, j)),
        ],
        out_specs=[
            pl.BlockSpec((tm, tn), lambda j, i: (i, j)),
            pl.BlockSpec((ts, tn), lambda j, i: (0, j)),
            pl.BlockSpec((1, CONV_A_K - 1, tn), lambda j, i: (i // tpb, 0, j)),
            pl.BlockSpec((nb_s, CONV_A_K - 1, tn), lambda j, i: (0, 0, j)),
        ],
        out_shape=[
            jax.ShapeDtypeStruct((tp, d_conv), BF16),
            jax.ShapeDtypeStruct((ts, d_conv), BF16),
            jax.ShapeDtypeStruct((n_batch_p, CONV_A_K - 1, d_conv), F32),
            jax.ShapeDtypeStruct((nb_s, CONV_A_K - 1, d_conv), F32),
        ],
        scratch_shapes=[
            pltpu.VMEM((3, tn, d), BF16),
            pltpu.VMEM((SUBLANES + tm, tn), F32),
            pltpu.VMEM((SUBLANES + ts, tn), F32),
            pltpu.VMEM((ts, tn), F32),
            pltpu.VMEM((ts, tn), F32),
        ],
        compiler_params=_params(("arbitrary", "arbitrary")),
        name="proj_a",
    )(xnp, xns, w_in, w_in, w_in, conv_w, state_s)


def _proj_raw_kernel(xp_ref, xs_ref, w_hbm, op_ref, os_ref, wbf, wstage, wsem, *, nsplit, width, row0):
    j = pl.program_id(0)
    tn = wbf.shape[0]

    def w_copy(jj, slot):
        rows = pl.ds(pl.multiple_of(row0 + jj * tn, SUBLANES), tn)
        return pltpu.make_async_copy(w_hbm.at[rows], wstage.at[slot], wsem.at[slot])

    def emit(x_ref, o_ref):
        acc = _dot_nt(x_ref[...], wbf[...])
        if nsplit == 0:
            o_ref[...] = acc
        else:
            for s in range(nsplit):
                o_ref[s] = acc[:, s * width:(s + 1) * width]

    @pl.when(pl.program_id(1) == 0)
    def _():
        @pl.when(j == 0)
        def _():
            w_copy(0, 0).start()

        w_copy(j, j % 2).wait()

        @pl.when(j + 1 < pl.num_programs(0))
        def _():
            w_copy(j + 1, (j + 1) % 2).start()

        wbf[...] = wstage[j % 2].astype(BF16)
        emit(xs_ref, os_ref)

    emit(xp_ref, op_ref)


def _proj_raw(xnp, xns, w, col0, ncols, width):
    tp, d = xnp.shape
    ts = xns.shape[0]
    tn = 1024
    tm = _pick(tp, (1024, 512, 256, 128))
    assert col0 % SUBLANES == 0 and ncols % tn == 0
    if width == 0:
        nsplit = 0
        out_specs = [pl.BlockSpec((tm, tn), lambda j, i: (i, j)),
                     pl.BlockSpec((ts, tn), lambda j, i: (0, j))]
        out_shape = [jax.ShapeDtypeStruct((tp, ncols), F32), jax.ShapeDtypeStruct((ts, ncols), F32)]
    else:
        nsplit = tn // width
        out_specs = [pl.BlockSpec((nsplit, tm, width), lambda j, i: (j, i, 0)),
                     pl.BlockSpec((nsplit, ts, width), lambda j, i: (j, 0, 0))]
        out_shape = [jax.ShapeDtypeStruct((ncols // width, tp, width), F32),
                     jax.ShapeDtypeStruct((ncols // width, ts, width), F32)]
    kern = functools.partial(_proj_raw_kernel, nsplit=nsplit, width=width, row0=col0)
    return pl.pallas_call(
        kern,
        grid=(ncols // tn, tp // tm),
        in_specs=[
            pl.BlockSpec((tm, d), lambda j, i: (i, 0)),
            pl.BlockSpec((ts, d), lambda j, i: (0, 0)),
            pl.BlockSpec(memory_space=pl.ANY),
        ],
        out_specs=out_specs,
        out_shape=out_shape,
        scratch_shapes=[pltpu.VMEM((tn, d), BF16), pltpu.VMEM((2, tn, d), F32), pltpu.SemaphoreType.DMA((2,))],
        compiler_params=_params(("arbitrary", "arbitrary")),
        name=f"proj_raw_{col0}",
    )(xnp, xns, w)


def _proj_conv_kernel(xp_ref, xs_ref, w_ref, cw_ref, cb_ref, st_ref, op_ref, os_ref, cmp_ref, cms_ref,
                      wbf, ubuf, sbuf, fix, *, tiles_per_batch, nb_s, l_s, nsplit, width):
    i = pl.program_id(1)
    tm = xp_ref.shape[0]
    ts = xs_ref.shape[0]
    tail = CONV_M_K - 1
    cw = cw_ref[...]
    bias = cb_ref[...]

    def store(o_ref, act):
        for s in range(nsplit):
            o_ref[s] = act[:, s * width:(s + 1) * width]

    @pl.when(i == 0)
    def _():
        wbf[...] = w_ref[...].astype(BF16)
        raw = _dot_nt(xs_ref[...], wbf[...])
        sbuf[pl.ds(0, SUBLANES), :] = jnp.zeros((SUBLANES, raw.shape[1]), F32)
        sbuf[pl.ds(SUBLANES, ts), :] = raw
        fix[...] = jnp.zeros_like(fix)
        for b in range(nb_s):
            for back in range(1, tail + 1):
                for m in range(back):
                    fix[back - 1, pl.ds(b * l_s + m, 1), :] = st_ref[b, pl.ds(tail + m - back, 1), :]
        rmod = lax.broadcasted_iota(jnp.int32, (ts, 1), 0) % l_s
        acc = raw * cw[tail:tail + 1, :]
        for back in range(1, tail + 1):
            tap = jnp.where(rmod < back, fix[back - 1], sbuf[pl.ds(SUBLANES - back, ts), :])
            acc = acc + tap * cw[tail - back:tail - back + 1, :]
        store(os_ref, jax.nn.silu(acc + bias))
        for b in range(nb_s):
            cms_ref[b] = sbuf[pl.ds(SUBLANES + (b + 1) * l_s - tail, tail), :]

    @pl.when(i % tiles_per_batch == 0)
    def _():
        ubuf[pl.ds(0, SUBLANES), :] = jnp.zeros((SUBLANES, ubuf.shape[1]), F32)

    raw = _dot_nt(xp_ref[...], wbf[...])
    ubuf[pl.ds(SUBLANES, tm), :] = raw
    acc = raw * cw[tail:tail + 1, :]
    for back in range(1, tail + 1):
        acc = acc + ubuf[pl.ds(SUBLANES - back, tm), :] * cw[tail - back:tail - back + 1, :]
    store(op_ref, jax.nn.silu(acc + bias))
    ubuf[pl.ds(0, SUBLANES), :] = ubuf[pl.ds(tm, SUBLANES), :]

    @pl.when(i % tiles_per_batch == tiles_per_batch - 1)
    def _():
        cmp_ref[0] = ubuf[pl.ds(SUBLANES + tm - tail, tail), :]


def _proj_conv(xnp, xns, w, conv_w, conv_b, state_s, col0, ch0, ncols, width, n_batch_p, seq_p):
    tp, d = xnp.shape
    ts = xns.shape[0]
    nb_s = state_s.shape[0]
    l_s = ts // nb_s
    tail = CONV_M_K - 1
    tn = 1024
    tm = _pick(seq_p, (1024, 512, 256, 128))
    tpb = seq_p // tm
    assert col0 % tn == 0 and ch0 % tn == 0 and ncols % tn == 0 and l_s >= tail
    jb, cb0 = col0 // tn, ch0 // tn
    nsplit = tn // width
    kern = functools.partial(_proj_conv_kernel, tiles_per_batch=tpb, nb_s=nb_s, l_s=l_s,
                             nsplit=nsplit, width=width)
    return pl.pallas_call(
        kern,
        grid=(ncols // tn, tp // tm),
        in_specs=[
            pl.BlockSpec((tm, d), lambda j, i: (i, 0)),
            pl.BlockSpec((ts, d), lambda j, i: (0, 0)),
            pl.BlockSpec((tn, d), lambda j, i: (j + jb, 0)),
            pl.BlockSpec((CONV_M_K, tn), lambda j, i: (0, j + cb0)),
            pl.BlockSpec((1, tn), lambda j, i: (0, j + cb0)),
            pl.BlockSpec((nb_s, tail, tn), lambda j, i: (0, 0, j + cb0)),
        ],
        out_specs=[
            pl.BlockSpec((nsplit, tm, width), lambda j, i: (j, i, 0)),
            pl.BlockSpec((nsplit, ts, width), lambda j, i: (j, 0, 0)),
            pl.BlockSpec((1, tail, tn), lambda j, i: (i // tpb, 0, j)),
            pl.BlockSpec((nb_s, tail, tn), lambda j, i: (0, 0, j)),
        ],
        out_shape=[
            jax.ShapeDtypeStruct((ncols // width, tp, width), F32),
            jax.ShapeDtypeStruct((ncols // width, ts, width), F32),
            jax.ShapeDtypeStruct((n_batch_p, tail, ncols), F32),
            jax.ShapeDtypeStruct((nb_s, tail, ncols), F32),
        ],
        scratch_shapes=[
            pltpu.VMEM((tn, d), BF16),
            pltpu.VMEM((SUBLANES + tm, tn), F32),
            pltpu.VMEM((SUBLANES + ts, tn), F32),
            pltpu.VMEM((tail, ts, tn), F32),
        ],
        compiler_params=_params(("arbitrary", "arbitrary")),
        name=f"proj_conv_{col0}",
    )(xnp, xns, w, conv_w, conv_b, state_s)


def _ssd_kernel(*refs, q, has_state, nchunks):
    (z_ref, xs_ref, b_ref, c_ref, dt_ref, dtb_ref, alog_ref, dsk_ref, gn_ref, *rest) = refs
    if has_state:
        hprev, *rest = rest
    (y_ref, oh, h_s, acg, rowt) = rest
    c = pl.program_id(1)
    nheads = N_GROUPS * HEADS_PER_GROUP

    @pl.when(c == 0)
    def _init():
        if has_state:
            h_s[...] = hprev[...]
        else:
            h_s[...] = jnp.zeros_like(h_s)

    dt = _softplus(dt_ref[...] + dtb_ref[...])
    da = dt * (-jnp.exp(alog_ref[...]))
    ri = lax.broadcasted_iota(jnp.int32, (q, q), 0)
    ci = lax.broadcasted_iota(jnp.int32, (q, q), 1)
    causal = ri >= ci
    tril = jnp.where(causal, 1.0, 0.0).astype(BF16)
    acum = sum(_dot(tril, p) for p in _split3(da))
    eye = jnp.where(lax.broadcasted_iota(jnp.int32, (nheads, nheads), 0)
                    == lax.broadcasted_iota(jnp.int32, (nheads, nheads), 1), 1.0, 0.0).astype(BF16)
    acum2 = acum * LOG2_E
    rowt[0] = sum(lax.dot_general(eye, p, NT_DIMS, preferred_element_type=F32) for p in _split3(acum2))
    rowt[1] = sum(lax.dot_general(eye, p, NT_DIMS, preferred_element_type=F32) for p in _split3(dt))
    wend = jnp.exp(acum[q - 1:q, :] - acum) * dt
    rowt[2] = sum(lax.dot_general(eye, p, NT_DIMS, preferred_element_type=F32) for p in _split3(wend))
    for g in range(N_GROUPS):
        acg[g] = acum2[:, g * HEADS_PER_GROUP:(g + 1) * HEADS_PER_GROUP]

    lane = lax.broadcasted_iota(jnp.int32, (1, LANES), 1)
    rowi = lax.broadcasted_iota(jnp.int32, (LANES, 1), 0)
    half_w = LANES // 2

    def group_body(g, carry):
        xs = xs_ref[g]
        bb = b_ref[g].astype(BF16)
        ccb16 = c_ref[g].astype(BF16)
        cb_ = lax.dot_general(ccb16, bb, NT_DIMS, preferred_element_type=F32)
        ac8 = acg[g]
        dsk = dsk_ref[g]
        lo_half = lane < half_w
        lo_rows = rowi < half_w
        ys = []
        for pair in range(HEADS_PER_GROUP // 2):
            sl = slice(pair * LANES, (pair + 1) * LANES)
            xp = xs[:, sl]
            hp = h_s[g, pl.ds(pair * LANES, LANES), :]
            ms_, acols, wrows = [], [], []
            for r in (2 * pair, 2 * pair + 1):
                head = g * HEADS_PER_GROUP + r
                acol = jnp.broadcast_to(ac8[:, r:r + 1], (q, LANES))
                arow = rowt[0, pl.ds(head, 1), :]
                drow = rowt[1, pl.ds(head, 1), :]
                decay = jnp.exp2(jnp.where(causal, acol[:, :q] - arow, -jnp.inf))
                ms_.append((cb_ * decay * drow).astype(BF16))
                acols.append(acol)
                wrows.append(jnp.broadcast_to(rowt[2, pl.ds(head, 1), :], (half_w, q)))
            xlo = jnp.where(lo_half, xp, 0.0).astype(BF16)
            xhi = jnp.where(lo_half, 0.0, xp).astype(BF16)
            if q % LANES == 0:
                ydiag = _dot(jnp.concatenate(ms_, axis=1), jnp.concatenate([xlo, xhi], axis=0))
            else:
                ydiag = _dot(ms_[0], xlo) + _dot(ms_[1], xhi)
            ea = jnp.where(lo_half, jnp.exp2(acols[0]), jnp.exp2(acols[1]))
            yoff = ea * lax.dot_general(ccb16, hp.astype(BF16), NT_DIMS, preferred_element_type=F32)
            ys.append(dsk[:, sl] * xp + ydiag + yoff)
            xwt = (xp.T * jnp.concatenate(wrows, axis=0)).astype(BF16)
            dlast = [jnp.broadcast_to(jnp.exp2(a[q - 1:q, :]), (LANES, LANES)) for a in acols]
            h_s[g, pl.ds(pair * LANES, LANES), :] = hp * jnp.where(lo_rows, dlast[0], dlast[1]) + _dot(xwt, bb)
        yg = jnp.concatenate(ys, axis=1)
        hh = yg * jax.nn.silu(z_ref[g])
        ms = jnp.mean(hh * hh, axis=-1, keepdims=True)
        y_ref[g] = (hh * lax.rsqrt(ms + EPS) * gn_ref[g]).astype(BF16)
        return carry

    lax.fori_loop(0, N_GROUPS, group_body, 0, unroll=2)

    @pl.when(c == nchunks - 1)
    def _():
        oh[...] = h_s[...]


def _ssd(z, xc, bc, dt_raw, prm, n_batch, seq, q, h_prev=None):
    nchunks = seq // q
    nheads = N_GROUPS * HEADS_PER_GROUP
    has_state = h_prev is not None
    g8 = N_GROUPS

    def tok(first):
        return lambda b, c: (first, b * nchunks + c, 0)

    def const3(b, c):
        return (0, 0, 0)

    in_specs = [
        pl.BlockSpec((g8, q, GROUP_W), tok(0)),
        pl.BlockSpec((g8, q, GROUP_W), tok(0)),
        pl.BlockSpec((g8, q, D_STATE), tok(0)),
        pl.BlockSpec((g8, q, D_STATE), tok(1)),
        pl.BlockSpec((q, nheads), lambda b, c: (b * nchunks + c, 0)),
        pl.BlockSpec((1, nheads), lambda b, c: (0, 0)),
        pl.BlockSpec((1, nheads), lambda b, c: (0, 0)),
        pl.BlockSpec((g8, 1, GROUP_W), const3),
        pl.BlockSpec((g8, 1, GROUP_W), const3),
    ]
    args = [z, xc, bc, bc, dt_raw, prm["dtb"], prm["alog"], prm["dsk"], prm["gn"]]

    def per_batch(shape):
        return pl.BlockSpec((None,) + shape, lambda b, c: (b,) + (0,) * len(shape))

    if has_state:
        in_specs.append(per_batch((g8, GROUP_W, D_STATE)))
        args.append(h_prev)
    t = n_batch * seq
    out_specs = [
        pl.BlockSpec((g8, q, GROUP_W), lambda b, c: (0, b * nchunks + c, 0)),
        per_batch((g8, GROUP_W, D_STATE)),
    ]
    out_shape = [
        jax.ShapeDtypeStruct((g8, t, GROUP_W), BF16),
        jax.ShapeDtypeStruct((n_batch, g8, GROUP_W, D_STATE), F32),
    ]
    scratch = [
        pltpu.VMEM((g8, GROUP_W, D_STATE), F32),
        pltpu.VMEM((g8, q, HEADS_PER_GROUP), F32),
        pltpu.VMEM((3, nheads, q), F32),
    ]
    kern = functools.partial(_ssd_kernel, q=q, has_state=has_state, nchunks=nchunks)
    return pl.pallas_call(
        kern,
        grid=(n_batch, nchunks),
        in_specs=in_specs,
        out_specs=out_specs,
        out_shape=out_shape,
        scratch_shapes=scratch,
        compiler_params=_params(("arbitrary", "arbitrary")),
        name="ssd_state" if has_state else "ssd",
    )(*args)


def _merge_kernel(ap_ref, as_ref, yp_ref, ys_ref, gap_ref, gmp_ref, gas_ref, gms_ref, wa_ref, wm_ref,
                  op_ref, os_ref, wab, wmb):
    def emit(a_ref, y_ref, ga_ref, gm_ref, o_ref):
        oa = _dot(a_ref[...], wab[...])
        y = jnp.concatenate([y_ref[g] for g in range(N_GROUPS)], axis=1)
        om = _dot(y, wmb[...])
        o_ref[...] = (jax.nn.sigmoid(ga_ref[...]) * oa + jax.nn.sigmoid(gm_ref[...]) * om).astype(BF16)

    @pl.when(pl.program_id(1) == 0)
    def _():
        wab[...] = wa_ref[...].astype(BF16)
        wmb[...] = wm_ref[...].astype(BF16)
        emit(as_ref, ys_ref, gas_ref, gms_ref, os_ref)

    emit(ap_ref, yp_ref, gap_ref, gmp_ref, op_ref)


def _merge(abv_p, abv_s, y_p, y_s, g_p, g_s, w_a_out, w_m_out):
    tp, dc = abv_p.shape
    ts = abv_s.shape[0]
    dm = w_a_out.shape[1]
    tn = 512
    tm = _pick(tp, (512, 256, 128))
    ncol = dm // tn
    return pl.pallas_call(
        _merge_kernel,
        grid=(ncol, tp // tm),
        in_specs=[
            pl.BlockSpec((tm, dc), lambda j, i: (i, 0)),
            pl.BlockSpec((ts, dc), lambda j, i: (0, 0)),
            pl.BlockSpec((N_GROUPS, tm, GROUP_W), lambda j, i: (0, i, 0)),
            pl.BlockSpec((N_GROUPS, ts, GROUP_W), lambda j, i: (0, 0, 0)),
            pl.BlockSpec((tm, tn), lambda j, i: (i, j)),
            pl.BlockSpec((tm, tn), lambda j, i: (i, j + ncol)),
            pl.BlockSpec((ts, tn), lambda j, i: (0, j)),
            pl.BlockSpec((ts, tn), lambda j, i: (0, j + ncol)),
            pl.BlockSpec((dc, tn), lambda j, i: (0, j)),
            pl.BlockSpec((N_GROUPS * GROUP_W, tn), lambda j, i: (0, j)),
        ],
        out_specs=[
            pl.BlockSpec((tm, tn), lambda j, i: (i, j)),
            pl.BlockSpec((ts, tn), lambda j, i: (0, j)),
        ],
        out_shape=[jax.ShapeDtypeStruct((tp, dm), BF16), jax.ShapeDtypeStruct((ts, dm), BF16)],
        scratch_shapes=[pltpu.VMEM((dc, tn), BF16), pltpu.VMEM((N_GROUPS * GROUP_W, tn), BF16)],
        compiler_params=_params(("arbitrary", "arbitrary")),
        name="merge",
    )(abv_p, abv_s, y_p, y_s, g_p, g_p, g_s, g_s, w_a_out, w_m_out)


def _res_kernel(ap_ref, as_ref, rp_ref, rs_ref, w_ref, g_ref, *outs, n_prompt_tiles, merged_out, ts):
    if merged_out:
        h_ref, hn_ref, wbf = outs
    else:
        hp_ref, hs_ref, hnp_ref, hns_ref, wbf = outs
    i = pl.program_id(0)

    @pl.when(i == 0)
    def _():
        wbf[...] = w_ref[...].astype(BF16)

    def emit(a_ref, r_ref, store_h, store_hn):
        h = r_ref[...] + _dot(a_ref[...], wbf[...])
        store_h(h)
        store_hn(_rms(h, g_ref[...]))

    if merged_out:
        @pl.when(i < n_prompt_tiles)
        def _():
            def sh(h):
                h_ref[...] = h

            def shn(hn):
                hn_ref[...] = hn

            emit(ap_ref, rp_ref, sh, shn)

        @pl.when(i == n_prompt_tiles)
        def _():
            def sh(h):
                h_ref[pl.ds(0, ts), :] = h

            def shn(hn):
                hn_ref[pl.ds(0, ts), :] = hn

            emit(as_ref, rs_ref, sh, shn)
    else:
        def shp(h):
            hp_ref[...] = h

        def shnp(hn):
            hnp_ref[...] = hn.astype(hnp_ref.dtype)

        emit(ap_ref, rp_ref, shp, shnp)

        @pl.when(i == 0)
        def _():
            def shs(h):
                hs_ref[...] = h

            def shns(hn):
                hns_ref[...] = hn.astype(hns_ref.dtype)

            emit(as_ref, rs_ref, shs, shns)


def _res(a_p, a_s, r_p, r_s, w, g, merged_out, hn_dtype):
    tp, d = a_p.shape
    ts = a_s.shape[0]
    tm = 256 if merged_out else _pick(tp, (512, 256))
    assert tp % tm == 0 and ts <= tm
    npt = tp // tm
    last = npt - 1
    in_specs = [
        pl.BlockSpec((tm, d), lambda i: (jnp.minimum(i, last), 0)),
        pl.BlockSpec((ts, d), lambda i: (0, 0)),
        pl.BlockSpec((tm, d), lambda i: (jnp.minimum(i, last), 0)),
        pl.BlockSpec((ts, d), lambda i: (0, 0)),
        pl.BlockSpec((d, d), lambda i: (0, 0), pipeline_mode=pl.Buffered(1)),
        pl.BlockSpec((1, d), lambda i: (0, 0)),
    ]
    if merged_out:
        grid = (npt + 1,)
        out_specs = [pl.BlockSpec((tm, d), lambda i: (i, 0)), pl.BlockSpec((tm, d), lambda i: (i, 0))]
        out_shape = [jax.ShapeDtypeStruct((tp + ts, d), F32), jax.ShapeDtypeStruct((tp + ts, d), hn_dtype)]
    else:
        grid = (npt,)
        out_specs = [pl.BlockSpec((tm, d), lambda i: (i, 0)), pl.BlockSpec((ts, d), lambda i: (0, 0)),
                     pl.BlockSpec((tm, d), lambda i: (i, 0)), pl.BlockSpec((ts, d), lambda i: (0, 0))]
        out_shape = [jax.ShapeDtypeStruct((tp, d), F32), jax.ShapeDtypeStruct((ts, d), F32),
                     jax.ShapeDtypeStruct((tp, d), hn_dtype), jax.ShapeDtypeStruct((ts, d), hn_dtype)]
    kern = functools.partial(_res_kernel, n_prompt_tiles=npt, merged_out=merged_out, ts=ts)
    return pl.pallas_call(
        kern,
        grid=grid,
        in_specs=in_specs,
        out_specs=out_specs,
        out_shape=out_shape,
        scratch_shapes=[pltpu.VMEM((d, d), BF16)],
        compiler_params=_params(("arbitrary",)),
        name="res_merged" if merged_out else "res",
    )(a_p, a_s, r_p, r_s, w, g)


def _kv_kernel(m_ref, g_ref, wk_ref, wv_ref, k_hbm, v_hbm, mn, kvb, sem, *, n_batch):
    j = pl.program_id(0)

    @pl.when(j == 0)
    def _():
        mn[...] = _rms(m_ref[...], g_ref[...]).astype(BF16)

    kvb[0] = _dot(mn[...], wk_ref[...].astype(BF16))
    kvb[1] = _dot(mn[...], wv_ref[...].astype(BF16))
    nm = kvb.shape[1] // n_batch
    for h in range(N_XHEADS):
        @pl.when(j == h)
        def _(h=h):
            copies = [pltpu.make_async_copy(kvb.at[t, pl.ds(b * nm, nm)], dst.at[b, :, h, :], sem.at[t, b])
                      for t, dst in enumerate((k_hbm, v_hbm)) for b in range(n_batch)]
            for c in copies:
                c.start()
            for c in copies:
                c.wait()


def _memory_kv(mem2d, g, w_k, w_v, n_batch):
    m, d = mem2d.shape
    dh = d // N_XHEADS
    out_sds = jax.ShapeDtypeStruct((n_batch, m // n_batch, N_XHEADS, dh), F32)
    kern = functools.partial(_kv_kernel, n_batch=n_batch)
    return pl.pallas_call(
        kern,
        grid=(N_XHEADS,),
        in_specs=[
            pl.BlockSpec((m, d), lambda j: (0, 0)),
            pl.BlockSpec((1, d), lambda j: (0, 0)),
            pl.BlockSpec((d, dh), lambda j: (0, j)),
            pl.BlockSpec((d, dh), lambda j: (0, j)),
        ],
        out_specs=[pl.BlockSpec(memory_space=pl.ANY), pl.BlockSpec(memory_space=pl.ANY)],
        out_shape=[out_sds, out_sds],
        scratch_shapes=[pltpu.VMEM((m, d), BF16), pltpu.VMEM((2, m, dh), F32),
                        pltpu.SemaphoreType.DMA((2, n_batch))],
        compiler_params=_params(("arbitrary",)),
        name="memory_kv",
    )(mem2d, g, w_k, w_v)


def _attn_kernel(hn_ref, k_ref, v_ref, wq_ref, o_ref, wqb, kb, vb, kvf, sem):
    b = pl.program_id(0)
    i = pl.program_id(1)

    @pl.when((b == 0) & (i == 0))
    def _():
        wqb[...] = wq_ref[...].astype(BF16)

    d = wqb.shape[1]
    dh = d // N_XHEADS

    @pl.when(i == 0)
    def _():
        def copies(bb):
            slot = bb % 2
            return [pltpu.make_async_copy(src.at[bb, :, h, :], kvf.at[slot, t, h], sem.at[slot, t, h])
                    for t, src in enumerate((k_ref, v_ref)) for h in range(N_XHEADS)]

        @pl.when(b == 0)
        def _():
            for c in copies(b):
                c.start()

        for c in copies(b):
            c.wait()

        @pl.when(b + 1 < pl.num_programs(0))
        def _():
            for c in copies(b + 1):
                c.start()

        for h in range(N_XHEADS):
            kb[:, h * dh:(h + 1) * dh] = kvf[b % 2, 0, h].astype(BF16)
            vb[:, h * dh:(h + 1) * dh] = kvf[b % 2, 1, h].astype(BF16)

    q = _dot(hn_ref[...], wqb[...])
    outs = []
    for h in range(N_XHEADS):
        sl = slice(h * dh, (h + 1) * dh)
        s = lax.dot_general(q[:, sl].astype(BF16), kb[:, sl], NT_DIMS, preferred_element_type=F32)
        s = s * (dh ** -0.5)
        e = jnp.exp(s - jnp.max(s, axis=-1, keepdims=True))
        p = e / jnp.sum(e, axis=-1, keepdims=True)
        outs.append(_dot(p.astype(BF16), vb[:, sl]))
    o_ref[...] = jnp.concatenate(outs, axis=1).astype(BF16)


def _attn(hn, k, v, w_q, n_batch, seq):
    t, d = hn.shape
    nm, nh, dh = k.shape[1:]
    tm = _pick(seq, (512, 256, 128, 64, 32, 16))
    tpb = seq // tm
    kv_spec = pl.BlockSpec(memory_space=pl.ANY)
    scratch = [pltpu.VMEM((d, d), BF16), pltpu.VMEM((nm, d), BF16), pltpu.VMEM((nm, d), BF16),
               pltpu.VMEM((2, 2, nh, nm, dh), F32), pltpu.SemaphoreType.DMA((2, 2, nh))]
    return pl.pallas_call(
        _attn_kernel,
        grid=(n_batch, tpb),
        in_specs=[
            pl.BlockSpec((tm, d), lambda b, i: (b * tpb + i, 0)),
            kv_spec,
            kv_spec,
            pl.BlockSpec((d, d), lambda b, i: (0, 0), pipeline_mode=pl.Buffered(1)),
        ],
        out_specs=pl.BlockSpec((tm, d), lambda b, i: (b * tpb + i, 0)),
        out_shape=jax.ShapeDtypeStruct((t, d), BF16),
        scratch_shapes=scratch,
        compiler_params=_params(("arbitrary", "arbitrary")),
        name=f"attn_{seq}",
    )(hn, k, v, w_q)


def _router_kernel(x_ref, w_ref, b_ref, ri_ref, rw_ref):
    logits = _dot(x_ref[...].astype(BF16), w_ref[...].astype(BF16)) + b_ref[...]
    lane_i = lax.broadcasted_iota(jnp.int32, logits.shape, 1)
    lane = lane_i.astype(F32)
    ninf = -jnp.inf
    big = float(LANES)
    is_g = lane < N_EXPERT_GROUPS
    gl = jnp.where(is_g, logits, ninf)
    gmax = jnp.max(gl, axis=-1, keepdims=True)
    gsel = jnp.min(jnp.where(gl == gmax, lane, big), axis=-1, keepdims=True)
    pg = 1.0 / jnp.sum(jnp.where(is_g, jnp.exp(gl - gmax), 0.0), axis=-1, keepdims=True)
    lo = N_EXPERT_GROUPS + EXPERTS_PER_GROUP * gsel
    el = jnp.where(lane >= lo, jnp.where(lane < lo + EXPERTS_PER_GROUP, logits, ninf), ninf)
    m1 = jnp.max(el, axis=-1, keepdims=True)
    i1 = jnp.min(jnp.where(el == m1, lane, big), axis=-1, keepdims=True)
    el2 = jnp.where(lane == i1, ninf, el)
    m2 = jnp.max(el2, axis=-1, keepdims=True)
    i2 = jnp.min(jnp.where(el2 == m2, lane, big), axis=-1, keepdims=True)
    e = jnp.exp(m2 - m1)
    w1 = pg / (1.0 + e)
    w2 = pg * e / (1.0 + e)
    ri_ref[...] = jnp.where(lane_i == 0, i1 - N_EXPERT_GROUPS,
                            jnp.where(lane_i == 1, i2 - N_EXPERT_GROUPS, 0.0)).astype(jnp.int32)
    rw_ref[...] = jnp.where(lane_i == 0, w1, jnp.where(lane_i == 1, w2, 0.0))


def _router(hn_all, w_r, b_r):
    t, d = hn_all.shape
    tm = _pick(t, (640, 512, 384, 256, 128))
    return pl.pallas_call(
        _router_kernel,
        grid=(t // tm,),
        in_specs=[
            pl.BlockSpec((tm, d), lambda i: (i, 0)),
            pl.BlockSpec((d, LANES), lambda i: (0, 0)),
            pl.BlockSpec((1, LANES), lambda i: (0, 0)),
        ],
        out_specs=[pl.BlockSpec((tm, LANES), lambda i: (i, 0)), pl.BlockSpec((tm, LANES), lambda i: (i, 0))],
        out_shape=[jax.ShapeDtypeStruct((t, LANES), jnp.int32), jax.ShapeDtypeStruct((t, LANES), F32)],
        compiler_params=_params(("arbitrary",)),
        name="router",
    )(hn_all, w_r, b_r)


def _row_copy(src, dst, s, d, sem):
    return pltpu.make_async_copy(src.at[pl.ds(s, 1)], dst.at[pl.ds(d, 1)], sem)


def _dispatch_kernel(dest_ref, pad0_ref, padn_ref, nsub_ref, x_ref, o_ref, zrow, zblk, sem, zsem, csem, bsem, *,
                     chunk, n_blocks):
    base = pl.program_id(0) * chunk

    @pl.when(pl.program_id(0) == 0)
    def _():
        zrow[...] = jnp.zeros_like(zrow)
        zblk[...] = jnp.zeros_like(zblk)

        def pad_plan(e):
            p0 = pad0_ref[e]
            head = jnp.minimum(padn_ref[e], (SUBLANES - p0 % SUBLANES) % SUBLANES)
            return p0, head, (padn_ref[e] - head) // SUBLANES

        def group_copy(row):
            return pltpu.make_async_copy(zrow, o_ref.at[pl.ds(pl.multiple_of(row, SUBLANES), SUBLANES)], csem)

        def pad_start(e, carry):
            p0, head, ngroups = pad_plan(e)

            def start_row(r, c):
                _row_copy(zrow, o_ref, 0, p0 + r, zsem).start()
                return c

            def start_group(j, c):
                group_copy(p0 + head + j * SUBLANES).start()
                return c

            lax.fori_loop(0, head, start_row, 0)
            return lax.fori_loop(0, ngroups, start_group, carry)

        def pad_wait(e, carry):
            _, head, ngroups = pad_plan(e)

            def wait_row(r, c):
                _row_copy(zrow, o_ref, 0, 0, zsem).wait()
                return c

            def wait_group(j, c):
                group_copy(0).wait()
                return c

            lax.fori_loop(0, head, wait_row, 0)
            return lax.fori_loop(0, ngroups, wait_group, carry)

        lax.fori_loop(0, N_EXPERTS, pad_start, 0)

        def blk_copy(b):
            return pltpu.make_async_copy(zblk, o_ref.at[pl.ds(pl.multiple_of(b * EXPERT_SUB, EXPERT_SUB),
                                                             EXPERT_SUB)], bsem)

        def tail_start(b, c):
            blk_copy(b).start()
            return c

        def tail_wait(b, c):
            blk_copy(b).wait()
            return c

        lax.fori_loop(nsub_ref[0], n_blocks, tail_start, 0)
        lax.fori_loop(0, N_EXPERTS, pad_wait, 0)
        lax.fori_loop(nsub_ref[0], n_blocks, tail_wait, 0)

    def issue(r, carry):
        t = base + r
        _row_copy(x_ref, o_ref, r, dest_ref[2 * t], sem).start()
        _row_copy(x_ref, o_ref, r, dest_ref[2 * t + 1], sem).start()
        return carry

    lax.fori_loop(0, chunk, issue, 0, unroll=8)

    for _ in range(2):
        pltpu.make_async_copy(x_ref, o_ref.at[pl.ds(0, chunk)], sem).wait()


def _dispatch(tables, hn_all, n_blocks):
    dest, pad0, padn, nsub = tables
    t, d = hn_all.shape
    chunk = _pick(t, (640, 512, 384, 256, 128))
    kern = functools.partial(_dispatch_kernel, chunk=chunk, n_blocks=n_blocks)
    return pl.pallas_call(
        kern,
        grid_spec=pltpu.PrefetchScalarGridSpec(
            num_scalar_prefetch=4,
            grid=(t // chunk,),
            in_specs=[pl.BlockSpec((chunk, d), lambda i, *_: (i, 0))],
            out_specs=pl.BlockSpec(memory_space=pl.ANY),
            scratch_shapes=[pltpu.VMEM((SUBLANES, d), F32), pltpu.VMEM((EXPERT_SUB, d), F32),
                            pltpu.SemaphoreType.DMA(()), pltpu.SemaphoreType.DMA(()),
                            pltpu.SemaphoreType.DMA(()), pltpu.SemaphoreType.DMA(())],
        ),
        out_shape=jax.ShapeDtypeStruct((n_blocks * EXPERT_SUB, d), F32),
        compiler_params=_params(("arbitrary",)),
        name="dispatch",
    )(dest, pad0, padn, nsub, hn_all)


def _expert_kernel(bstart_ref, wg_ref, wu_ref, wd_ref, x_hbm, y_hbm, gcache, ucache, dcache, xbuf, ybuf, xb,
                   xsem, ysem, *, n_blocks):
    e = pl.program_id(0)
    k = pl.program_id(1)
    nk = gcache.shape[1]
    total = bstart_ref[N_EXPERTS]

    def x_copy(b, slot):
        rows = pl.ds(pl.multiple_of(b * EXPERT_SUB, EXPERT_SUB), EXPERT_SUB)
        return pltpu.make_async_copy(x_hbm.at[rows], xbuf.at[slot], xsem.at[slot])

    def y_copy(b, slot):
        rows = pl.ds(pl.multiple_of(b * EXPERT_SUB, EXPERT_SUB), EXPERT_SUB)
        return pltpu.make_async_copy(ybuf.at[slot], y_hbm.at[rows], ysem.at[slot])

    @pl.when(e < N_EXPERTS)
    def _():
        slot = e % 2
        gcache[slot, k] = wg_ref[...].astype(BF16)
        ucache[slot, k] = wu_ref[...].astype(BF16)
        dcache[slot, k] = wd_ref[...].astype(BF16)

    @pl.when((e == 0) & (k == 0) & (total > 0))
    def _():
        x_copy(0, 0).start()

    @pl.when(e >= 1)
    def _():
        owner = e - 1
        wslot = owner % 2
        b0 = bstart_ref[owner]
        n = bstart_ref[owner + 1] - b0

        cpu = 2
        upb = nk // cpu

        def unit(u, carry):
            b = b0 + u // upb
            part = u % upb
            slot = b % 2

            def swiglu(x, p):
                kks = [p * cpu + c for c in range(cpu)]
                wg = jnp.concatenate([gcache[wslot, kk] for kk in kks], axis=1)
                wu = jnp.concatenate([ucache[wslot, kk] for kk in kks], axis=1)
                wd = jnp.concatenate([dcache[wslot, kk] for kk in kks], axis=0)
                hid = (jax.nn.silu(_dot(x, wg)) * _dot(x, wu)).astype(BF16)
                return _dot(hid, wd)

            @pl.when(part == 0)
            def _():
                x_copy(b, slot).wait()

                @pl.when(b + 1 < total)
                def _():
                    x_copy(b + 1, 1 - slot).start()

                x = xbuf[slot].astype(BF16)
                xb[...] = x

                @pl.when(b >= 2)
                def _():
                    y_copy(b - 2, slot).wait()

                ybuf[slot] = swiglu(x, 0)

            for p in range(1, upb):
                @pl.when(part == p)
                def _(p=p):
                    ybuf[slot] += swiglu(xb[...], p)
                    if p == upb - 1:
                        y_copy(b, slot).start()

            return carry

        lax.fori_loop((n * upb * k) // nk, (n * upb * (k + 1)) // nk, unit, 0)

    @pl.when((e == N_EXPERTS) & (k == nk - 1))
    def _():
        @pl.when(total >= 2)
        def _():
            y_copy(total - 2, total % 2).wait()

        @pl.when(total >= 1)
        def _():
            y_copy(total - 1, (total - 1) % 2).wait()

        ybuf[0] = jnp.zeros(ybuf.shape[1:], F32)

        def zfill(b, carry):
            y_copy(b, 0).start()
            y_copy(b, 0).wait()
            return carry

        lax.fori_loop(total, n_blocks, zfill, 0)


def _experts(bstart, x_sorted, w_gate, w_up, w_down, kchunk):
    nrows, d = x_sorted.shape
    n_blocks = nrows // EXPERT_SUB
    de = w_gate.shape[2]
    nk = de // kchunk
    last = N_EXPERTS - 1

    def widx(e, k):
        return jnp.minimum(e, last), jnp.where(e <= last, k, nk - 1)

    def in_idx(e, k, b):
        ee, kk = widx(e, k)
        return (ee, 0, kk)

    def down_idx(e, k, b):
        ee, kk = widx(e, k)
        return (ee, kk, 0)

    kern = functools.partial(_expert_kernel, n_blocks=n_blocks)
    return pl.pallas_call(
        kern,
        grid_spec=pltpu.PrefetchScalarGridSpec(
            num_scalar_prefetch=1,
            grid=(N_EXPERTS + 1, nk),
            in_specs=[
                pl.BlockSpec((None, d, kchunk), in_idx),
                pl.BlockSpec((None, d, kchunk), in_idx),
                pl.BlockSpec((None, kchunk, d), down_idx),
                pl.BlockSpec(memory_space=pl.ANY),
            ],
            out_specs=pl.BlockSpec(memory_space=pl.ANY),
            scratch_shapes=[
                pltpu.VMEM((2, nk, d, kchunk), BF16),
                pltpu.VMEM((2, nk, d, kchunk), BF16),
                pltpu.VMEM((2, nk, kchunk, d), BF16),
                pltpu.VMEM((2, EXPERT_SUB, d), F32),
                pltpu.VMEM((2, EXPERT_SUB, d), F32),
                pltpu.VMEM((EXPERT_SUB, d), BF16),
                pltpu.SemaphoreType.DMA((2,)),
                pltpu.SemaphoreType.DMA((2,)),
            ],
        ),
        out_shape=jax.ShapeDtypeStruct((nrows, d), F32),
        compiler_params=_params(("arbitrary", "arbitrary")),
        name="experts",
    )(bstart, w_gate, w_up, w_down, x_sorted)


def _final_kernel(dest_ref, h_ref, rw_ref, ys_ref, g_ref, yp_ref, yss_ref, ya, yb, sem, *,
                  tm, n_prompt_tiles, ts):
    i = pl.program_id(0)

    def gather(tile, n):
        slot = tile % 2

        def issue(r, carry):
            t = tile * tm + r
            _row_copy(ys_ref, ya.at[slot], dest_ref[2 * t], r, sem.at[slot]).start()
            _row_copy(ys_ref, yb.at[slot], dest_ref[2 * t + 1], r, sem.at[slot]).start()
            return carry

        lax.fori_loop(0, n, issue, 0, unroll=8)

    def emit(n, o_ref):
        slot = i % 2
        for buf in (ya, yb):
            pltpu.make_async_copy(ys_ref.at[pl.ds(0, n)], buf.at[slot, pl.ds(0, n)], sem.at[slot]).wait()
        w = rw_ref[pl.ds(0, n), :]
        h = (h_ref[pl.ds(0, n), :] + w[:, 0:1] * ya[slot, pl.ds(0, n), :]
             + w[:, 1:2] * yb[slot, pl.ds(0, n), :])
        o_ref[...] = _rms(h, g_ref[...])

    @pl.when(i == 0)
    def _():
        gather(i, tm)

    @pl.when(i + 1 < n_prompt_tiles)
    def _():
        gather(i + 1, tm)

    @pl.when(i + 1 == n_prompt_tiles)
    def _():
        gather(i + 1, ts)

    @pl.when(i < n_prompt_tiles)
    def _():
        emit(tm, yp_ref)

    @pl.when(i == n_prompt_tiles)
    def _():
        emit(ts, yss_ref)


def _final(dest, h_all, rw, y_sorted, g, tp, ts):
    t, d = h_all.shape
    tm = 256
    assert tp % tm == 0 and ts <= tm and t == tp + ts
    npt = tp // tm
    kern = functools.partial(_final_kernel, tm=tm, n_prompt_tiles=npt, ts=ts)
    return pl.pallas_call(
        kern,
        grid_spec=pltpu.PrefetchScalarGridSpec(
            num_scalar_prefetch=1,
            grid=(npt + 1,),
            in_specs=[
                pl.BlockSpec((tm, d), lambda i, dref: (i, 0)),
                pl.BlockSpec((tm, LANES), lambda i, dref: (i, 0)),
                pl.BlockSpec(memory_space=pl.ANY),
                pl.BlockSpec((1, d), lambda i, dref: (0, 0)),
            ],
            out_specs=[
                pl.BlockSpec((tm, d), lambda i, dref: (jnp.minimum(i, npt - 1), 0)),
                pl.BlockSpec((ts, d), lambda i, dref: (0, 0)),
            ],
            scratch_shapes=[pltpu.VMEM((2, tm, d), F32), pltpu.VMEM((2, tm, d), F32),
                            pltpu.SemaphoreType.DMA((2,))],
        ),
        out_shape=[jax.ShapeDtypeStruct((tp, d), F32), jax.ShapeDtypeStruct((ts, d), F32)],
        compiler_params=_params(("arbitrary",)),
        name="final",
    )(dest, h_all, rw, y_sorted, g)


def _routing_tables(eid):
    i32 = jnp.int32
    onehot = (eid[:, None] == jnp.arange(N_EXPERTS, dtype=i32)[None, :]).astype(i32)
    csum = jnp.cumsum(onehot, axis=0)
    counts = csum[-1]
    nsub = (counts + EXPERT_SUB - 1) // EXPERT_SUB
    bend = jnp.cumsum(nsub)
    bstart = bend - nsub
    seg = bstart * EXPERT_SUB
    dest = jnp.sum(onehot * (csum - 1 + seg[None, :]), axis=1).astype(i32)
    pad0 = (seg + counts).astype(i32)
    padn = (nsub * EXPERT_SUB - counts).astype(i32)
    bstart_all = jnp.concatenate([bstart, bend[-1:]]).astype(i32)
    return (dest, pad0, padn, bend[-1:].astype(i32)), bstart_all


def _group_major(a, width):
    lead = a.shape[:-1]
    g = a.shape[-1] // width
    return jnp.moveaxis(a.reshape(lead + (g, width)), -2, 0)


def kernel(x_prompt, x_sample, cache_mem_k, cache_mem_v, state_conv_a, state_conv_m, state_ssm, mem_prompt,
           norm_mix, w_in, conv_a_w, w_a_out, conv_m_w, conv_m_b, dt_bias, a_log, d_skip, ssm_norm,
           w_m_out, w_o, norm_cross, norm_mem, w_q, w_k, w_v, w_co, norm_ffn, w_rg, b_rg, w_re, b_re,
           w_gate, w_up, w_down, norm_final):
    depth = w_in.shape[0]
    assert depth == 1, "single-layer step"
    nbp, seq_p, d = x_prompt.shape
    nbs, seq_s, _ = x_sample.shape
    tp, ts = nbp * seq_p, nbs * seq_s
    n_mem = mem_prompt.shape[1]
    d_conv = conv_a_w.shape[2]
    d_inner = w_m_out.shape[1]
    nheads = dt_bias.shape[1]
    bc_w = N_GROUPS * D_STATE
    l = 0

    xp = x_prompt.reshape(tp, d)
    xs = x_sample.reshape(ts, d)
    w_in_t = jnp.swapaxes(w_in[l], 0, 1)
    col_z = 3 * d_conv
    col_x = col_z + d_inner
    col_b = col_x + d_inner
    col_dt = col_b + 2 * bc_w
    col_g = col_dt + nheads
    w_dt = w_in_t[col_dt:col_g]

    mk4, mv4 = _memory_kv(mem_prompt.reshape(nbp * n_mem, d), norm_mem[l][None], w_k[l], w_v[l], nbp)

    xnp, xns, dt_p, dt_s = _norm_dt(xp, xs, norm_mix[l][None], w_dt)
    abv_p, abv_s, ca_p, ca_s = _proj_a(xnp, xns, w_in_t, conv_a_w[l], state_conv_a[l], nbp, seq_p, d_conv)
    z_p, z_s = _proj_raw(xnp, xns, w_in_t, col_z, d_inner, GROUP_W)
    cw, cbias, scm = conv_m_w[l], conv_m_b[l][None], state_conv_m[l]
    xc_p, xc_s, cmx_p, cmx_s = _proj_conv(xnp, xns, w_in_t, cw, cbias, scm, col_x, 0, d_inner, GROUP_W,
                                           nbp, seq_p)
    bc_p, bc_s, cmb_p, cmb_s = _proj_conv(xnp, xns, w_in_t, cw, cbias, scm, col_b, d_inner, 2 * bc_w, D_STATE,
                                           nbp, seq_p)
    g_p, g_s = _proj_raw(xnp, xns, w_in_t, col_g, 2 * d, 0)

    prm = {
        "dtb": dt_bias[l][None],
        "alog": a_log[l][None],
        "dsk": _group_major(jnp.repeat(d_skip[l], HEAD_DIM)[None], GROUP_W),
        "gn": _group_major(ssm_norm[l][None], GROUP_W),
    }
    q_p = _pick(seq_p, (128, 64, 32, 16, 8))
    y_p, h_p = _ssd(z_p, xc_p, bc_p, dt_p, prm, nbp, seq_p, q_p)
    y_s, h_s = _ssd(z_s, xc_s, bc_s, dt_s, prm, nbs, seq_s, seq_s,
                    h_prev=state_ssm[l].reshape(nbs, N_GROUPS, GROUP_W, D_STATE))

    merged_p, merged_s = _merge(abv_p, abv_s, y_p, y_s, g_p, g_s, w_a_out[l],
                                w_m_out[l])
    h1_p, h1_s, hn1_p, hn1_s = _res(merged_p, merged_s, xp, xs, w_o[l], norm_cross[l][None], False, BF16)

    att_p = _attn(hn1_p, mk4, mv4, w_q[l], nbp, seq_p)
    att_s = _attn(hn1_s, cache_mem_k[l], cache_mem_v[l], w_q[l], nbs, seq_s)
    h2_all, hn2_all = _res(att_p, att_s, h1_p, h1_s, w_co[l], norm_ffn[l][None], True, F32)

    npad = LANES - N_EXPERT_GROUPS - N_EXPERTS
    w_r = jnp.concatenate([w_rg[l], w_re[l], jnp.zeros((d, npad), F32)], axis=1)
    b_r = jnp.concatenate([b_rg[l], b_re[l], jnp.zeros((npad,), F32)])[None]
    ri, rw = _router(hn2_all, w_r, b_r)
    tables, bstart = _routing_tables(ri[:, :2].reshape(-1))
    n_blocks = -(-2 * (tp + ts) // EXPERT_SUB) + N_EXPERTS
    x_sorted = _dispatch(tables, hn2_all, n_blocks)
    y_sorted = _experts(bstart, x_sorted, w_gate[l], w_up[l], w_down[l], 256)
    y_prompt, y_sample = _final(tables[0], h2_all, rw, y_sorted, norm_final[None], tp, ts)

    xh = d // N_XHEADS
    return (
        y_prompt.reshape(nbp, seq_p, d),
        y_sample.reshape(nbs, seq_s, d),
        mk4[None],
        mv4[None],
        ca_p[None],
        jnp.concatenate([cmx_p, cmb_p], axis=-1)[None],
        h_p.reshape(1, nbp, nheads, HEAD_DIM, D_STATE),
        ca_s[None],
        jnp.concatenate([cmx_s, cmb_s], axis=-1)[None],
        h_s.reshape(1, nbs, nheads, HEAD_DIM, D_STATE),
    )
```

```python
import functools

import jax
import jax.numpy as jnp
from jax import lax
from jax.experimental import pallas as pl
from jax.experimental.pallas import tpu as pltpu

F32 = jnp.float32
BF16 = jnp.bfloat16
EPS = 1e-6
LOG2_E = 1.4426950408889634

V7X_VMEM_BYTES = 64 * 1024 * 1024
VMEM_LIMIT = V7X_VMEM_BYTES - 8 * 1024 * 1024
LANES = 128
SUBLANES = 8

N_GROUPS = 8
HEADS_PER_GROUP = 8
HEAD_DIM = 64
D_STATE = 128
GROUP_W = HEADS_PER_GROUP * HEAD_DIM
N_XHEADS = 4
N_EXPERTS = 32
N_EXPERT_GROUPS = 4
EXPERTS_PER_GROUP = 8
EXPERT_SUB = 256
CONV_A_K = 3
CONV_M_K = 4

NT_DIMS = (((1,), (1,)), ((), ()))
TN_DIMS = (((0,), (0,)), ((), ()))


def _params(sem):
    return pltpu.CompilerParams(dimension_semantics=sem, vmem_limit_bytes=VMEM_LIMIT)


def _pick(n, cands):
    for c in cands:
        if n % c == 0:
            return c
    raise ValueError(f"no tile for {n} in {cands}")


def _dot(a, b):
    return jnp.dot(a, b, preferred_element_type=F32)


def _dot_nt(a, b):
    return lax.dot_general(a, b, NT_DIMS, preferred_element_type=F32)


def _rms(x, g):
    return x * lax.rsqrt(jnp.mean(x * x, axis=-1, keepdims=True) + EPS) * g


def _split3(x):
    hi = x.astype(BF16)
    r = x - hi.astype(F32)
    mid = r.astype(BF16)
    lo = (r - mid.astype(F32)).astype(BF16)
    return hi, mid, lo


def _softplus(x):
    return jnp.maximum(x, 0.0) + jnp.log1p(jnp.exp(-jnp.abs(x)))


def _norm_dt_kernel(xp_ref, xs_ref, g_ref, wdt_ref, xnp_ref, xns_ref, dtp_ref, dts_ref):
    wdt = wdt_ref[...].astype(BF16)

    def one(x_ref, xn_ref, dt_ref):
        xn = _rms(x_ref[...], g_ref[...]).astype(BF16)
        xn_ref[...] = xn
        dt_ref[...] = _dot_nt(xn, wdt)

    one(xp_ref, xnp_ref, dtp_ref)

    @pl.when(pl.program_id(0) == 0)
    def _():
        one(xs_ref, xns_ref, dts_ref)


def _norm_dt(xp, xs, g, wdt):
    tp, d = xp.shape
    ts = xs.shape[0]
    nh = wdt.shape[0]
    tm = _pick(tp, (512, 256, 128))
    return pl.pallas_call(
        _norm_dt_kernel,
        grid=(tp // tm,),
        in_specs=[
            pl.BlockSpec((tm, d), lambda i: (i, 0)),
            pl.BlockSpec((ts, d), lambda i: (0, 0)),
            pl.BlockSpec((1, d), lambda i: (0, 0)),
            pl.BlockSpec((nh, d), lambda i: (0, 0)),
        ],
        out_specs=[
            pl.BlockSpec((tm, d), lambda i: (i, 0)),
            pl.BlockSpec((ts, d), lambda i: (0, 0)),
            pl.BlockSpec((tm, nh), lambda i: (i, 0)),
            pl.BlockSpec((ts, nh), lambda i: (0, 0)),
        ],
        out_shape=[
            jax.ShapeDtypeStruct((tp, d), BF16),
            jax.ShapeDtypeStruct((ts, d), BF16),
            jax.ShapeDtypeStruct((tp, nh), F32),
            jax.ShapeDtypeStruct((ts, nh), F32),
        ],
        compiler_params=_params(("arbitrary",)),
        name="norm_dt",
    )(xp, xs, g, wdt)


def _proj_a_kernel(xp_ref, xs_ref, wb_ref, wc_ref, wh_ref, cw_ref, st_ref,
                   op_ref, os_ref, cap_ref, cas_ref,
                   wbf, ubuf, sbuf, s1buf, s2buf, *, tiles_per_batch, nb_s, l_s):
    i = pl.program_id(1)
    tm = xp_ref.shape[0]
    ts = xs_ref.shape[0]
    cw = cw_ref[...]

    @pl.when(i == 0)
    def _():
        wbf[0] = wb_ref[...].astype(BF16)
        wbf[1] = wc_ref[...].astype(BF16)
        wbf[2] = wh_ref[...].astype(BF16)
        x = xs_ref[...]
        u = _dot_nt(x, wbf[1]) * _dot_nt(x, wbf[2])
        sbuf[pl.ds(0, SUBLANES), :] = jnp.zeros((SUBLANES, u.shape[1]), F32)
        sbuf[pl.ds(SUBLANES, ts), :] = u
        s1buf[...] = jnp.zeros_like(s1buf)
        s2buf[...] = jnp.zeros_like(s2buf)
        for b in range(nb_s):
            s1buf[pl.ds(b * l_s, 1), :] = st_ref[b, pl.ds(1, 1), :]
            s2buf[pl.ds(b * l_s, 1), :] = st_ref[b, pl.ds(0, 1), :]
            s2buf[pl.ds(b * l_s + 1, 1), :] = st_ref[b, pl.ds(1, 1), :]
        rmod = lax.broadcasted_iota(jnp.int32, (ts, 1), 0) % l_s
        prev1 = jnp.where(rmod == 0, s1buf[...], sbuf[pl.ds(SUBLANES - 1, ts), :])
        prev2 = jnp.where(rmod < 2, s2buf[...], sbuf[pl.ds(SUBLANES - 2, ts), :])
        v = prev2 * cw[0:1, :] + prev1 * cw[1:2, :] + u * cw[2:3, :]
        os_ref[...] = (_dot_nt(x, wbf[0]) * v).astype(BF16)
        for b in range(nb_s):
            cas_ref[b] = sbuf[pl.ds(SUBLANES + (b + 1) * l_s - 2, 2), :]

    @pl.when(i % tiles_per_batch == 0)
    def _():
        ubuf[pl.ds(0, SUBLANES), :] = jnp.zeros((SUBLANES, ubuf.shape[1]), F32)

    x = xp_ref[...]
    u = _dot_nt(x, wbf[1]) * _dot_nt(x, wbf[2])
    ubuf[pl.ds(SUBLANES, tm), :] = u
    v = (ubuf[pl.ds(SUBLANES - 2, tm), :] * cw[0:1, :]
         + ubuf[pl.ds(SUBLANES - 1, tm), :] * cw[1:2, :] + u * cw[2:3, :])
    op_ref[...] = (_dot_nt(x, wbf[0]) * v).astype(BF16)
    ubuf[pl.ds(0, SUBLANES), :] = ubuf[pl.ds(tm, SUBLANES), :]

    @pl.when(i % tiles_per_batch == tiles_per_batch - 1)
    def _():
        cap_ref[0] = ubuf[pl.ds(SUBLANES + tm - 2, 2), :]


def _proj_a(xnp, xns, w_in, conv_w, state_s, n_batch_p, seq_p, d_conv):
    tp, d = xnp.shape
    ts = xns.shape[0]
    nb_s = state_s.shape[0]
    l_s = ts // nb_s
    tn = 512
    tm = _pick(seq_p, (1024, 512, 256, 128))
    tpb = seq_p // tm
    ncol = d_conv // tn
    kern = functools.partial(_proj_a_kernel, tiles_per_batch=tpb, nb_s=nb_s, l_s=l_s)
    return pl.pallas_call(
        kern,
        grid=(ncol, tp // tm),
        in_specs=[
            pl.BlockSpec((tm, d), lambda j, i: (i, 0)),
            pl.BlockSpec((ts, d), lambda j, i: (0, 0)),
            pl.BlockSpec((tn, d), lambda j, i: (j, 0)),
            pl.BlockSpec((tn, d), lambda j, i: (j + ncol, 0)),
            pl.BlockSpec((tn, d), lambda j, i: (j + 2 * ncol, 0)),
            pl.BlockSpec((CONV_A_K, tn), lambda j, i: (0, j)),
            pl.BlockSpec((nb_s, CONV_A_K - 1, tn), lambda j, i: (0, 0, j)),
        ],
        out_specs=[
            pl.BlockSpec((tm, tn), lambda j, i: (i, j)),
            pl.BlockSpec((ts, tn), lambda j, i: (0, j)),
            pl.BlockSpec((1, CONV_A_K - 1, tn), lambda j, i: (i // tpb, 0, j)),
            pl.BlockSpec((nb_s, CONV_A_K - 1, tn), lambda j, i: (0, 0, j)),
        ],
        out_shape=[
            jax.ShapeDtypeStruct((tp, d_conv), BF16),
            jax.ShapeDtypeStruct((ts, d_conv), BF16),
            jax.ShapeDtypeStruct((n_batch_p, CONV_A_K - 1, d_conv), F32),
            jax.ShapeDtypeStruct((nb_s, CONV_A_K - 1, d_conv), F32),
        ],
        scratch_shapes=[
            pltpu.VMEM((3, tn, d), BF16),
            pltpu.VMEM((SUBLANES + tm, tn), F32),
            pltpu.VMEM((SUBLANES + ts, tn), F32),
            pltpu.VMEM((ts, tn), F32),
            pltpu.VMEM((ts, tn), F32),
        ],
        compiler_params=_params(("arbitrary", "arbitrary")),
        name="proj_a",
    )(xnp, xns, w_in, w_in, w_in, conv_w, state_s)


def _proj_raw_kernel(xp_ref, xs_ref, w_hbm, op_ref, os_ref, wbf, wstage, wsem, *, nsplit, width, row0):
    j = pl.program_id(0)
    tn = wbf.shape[0]

    def w_copy(jj, slot):
        rows = pl.ds(pl.multiple_of(row0 + jj * tn, SUBLANES), tn)
        return pltpu.make_async_copy(w_hbm.at[rows], wstage.at[slot], wsem.at[slot])

    def emit(x_ref, o_ref):
        acc = _dot_nt(x_ref[...], wbf[...])
        if nsplit == 0:
            o_ref[...] = acc
        else:
            for s in range(nsplit):
                o_ref[s] = acc[:, s * width:(s + 1) * width]

    @pl.when(pl.program_id(1) == 0)
    def _():
        @pl.when(j == 0)
        def _():
            w_copy(0, 0).start()

        w_copy(j, j % 2).wait()

        @pl.when(j + 1 < pl.num_programs(0))
        def _():
            w_copy(j + 1, (j + 1) % 2).start()

        wbf[...] = wstage[j % 2].astype(BF16)
        emit(xs_ref, os_ref)

    emit(xp_ref, op_ref)


def _proj_raw(xnp, xns, w, col0, ncols, width):
    tp, d = xnp.shape
    ts = xns.shape[0]
    tn = 1024
    tm = _pick(tp, (1024, 512, 256, 128))
    assert col0 % SUBLANES == 0 and ncols % tn == 0
    if width == 0:
        nsplit = 0
        out_specs = [pl.BlockSpec((tm, tn), lambda j, i: (i, j)),
                     pl.BlockSpec((ts, tn), lambda j, i: (0, j))]
        out_shape = [jax.ShapeDtypeStruct((tp, ncols), F32), jax.ShapeDtypeStruct((ts, ncols), F32)]
    else:
        nsplit = tn // width
        out_specs = [pl.BlockSpec((nsplit, tm, width), lambda j, i: (j, i, 0)),
                     pl.BlockSpec((nsplit, ts, width), lambda j, i: (j, 0, 0))]
        out_shape = [jax.ShapeDtypeStruct((ncols // width, tp, width), F32),
                     jax.ShapeDtypeStruct((ncols // width, ts, width), F32)]
    kern = functools.partial(_proj_raw_kernel, nsplit=nsplit, width=width, row0=col0)
    return pl.pallas_call(
        kern,
        grid=(ncols // tn, tp // tm),
        in_specs=[
            pl.BlockSpec((tm, d), lambda j, i: (i, 0)),
            pl.BlockSpec((ts, d), lambda j, i: (0, 0)),
            pl.BlockSpec(memory_space=pl.ANY),
        ],
        out_specs=out_specs,
        out_shape=out_shape,
        scratch_shapes=[pltpu.VMEM((tn, d), BF16), pltpu.VMEM((2, tn, d), F32), pltpu.SemaphoreType.DMA((2,))],
        compiler_params=_params(("arbitrary", "arbitrary")),
        name=f"proj_raw_{col0}",
    )(xnp, xns, w)


def _proj_conv_kernel(xp_ref, xs_ref, w_ref, cw_ref, cb_ref, st_ref, op_ref, os_ref, cmp_ref, cms_ref,
                      wbf, ubuf, sbuf, fix, *, tiles_per_batch, nb_s, l_s, nsplit, width):
    i = pl.program_id(1)
    tm = xp_ref.shape[0]
    ts = xs_ref.shape[0]
    tail = CONV_M_K - 1
    cw = cw_ref[...]
    bias = cb_ref[...]

    def store(o_ref, act):
        for s in range(nsplit):
            o_ref[s] = act[:, s * width:(s + 1) * width]

    @pl.when(i == 0)
    def _():
        wbf[...] = w_ref[...].astype(BF16)
        raw = _dot_nt(xs_ref[...], wbf[...])
        sbuf[pl.ds(0, SUBLANES), :] = jnp.zeros((SUBLANES, raw.shape[1]), F32)
        sbuf[pl.ds(SUBLANES, ts), :] = raw
        fix[...] = jnp.zeros_like(fix)
        for b in range(nb_s):
            for back in range(1, tail + 1):
                for m in range(back):
                    fix[back - 1, pl.ds(b * l_s + m, 1), :] = st_ref[b, pl.ds(tail + m - back, 1), :]
        rmod = lax.broadcasted_iota(jnp.int32, (ts, 1), 0) % l_s
        acc = raw * cw[tail:tail + 1, :]
        for back in range(1, tail + 1):
            tap = jnp.where(rmod < back, fix[back - 1], sbuf[pl.ds(SUBLANES - back, ts), :])
            acc = acc + tap * cw[tail - back:tail - back + 1, :]
        store(os_ref, jax.nn.silu(acc + bias))
        for b in range(nb_s):
            cms_ref[b] = sbuf[pl.ds(SUBLANES + (b + 1) * l_s - tail, tail), :]

    @pl.when(i % tiles_per_batch == 0)
    def _():
        ubuf[pl.ds(0, SUBLANES), :] = jnp.zeros((SUBLANES, ubuf.shape[1]), F32)

    raw = _dot_nt(xp_ref[...], wbf[...])
    ubuf[pl.ds(SUBLANES, tm), :] = raw
    acc = raw * cw[tail:tail + 1, :]
    for back in range(1, tail + 1):
        acc = acc + ubuf[pl.ds(SUBLANES - back, tm), :] * cw[tail - back:tail - back + 1, :]
    store(op_ref, jax.nn.silu(acc + bias))
    ubuf[pl.ds(0, SUBLANES), :] = ubuf[pl.ds(tm, SUBLANES), :]

    @pl.when(i % tiles_per_batch == tiles_per_batch - 1)
    def _():
        cmp_ref[0] = ubuf[pl.ds(SUBLANES + tm - tail, tail), :]


def _proj_conv(xnp, xns, w, conv_w, conv_b, state_s, col0, ch0, ncols, width, n_batch_p, seq_p):
    tp, d = xnp.shape
    ts = xns.shape[0]
    nb_s = state_s.shape[0]
    l_s = ts // nb_s
    tail = CONV_M_K - 1
    tn = 1024
    tm = _pick(seq_p, (1024, 512, 256, 128))
    tpb = seq_p // tm
    assert col0 % tn == 0 and ch0 % tn == 0 and ncols % tn == 0 and l_s >= tail
    jb, cb0 = col0 // tn, ch0 // tn
    nsplit = tn // width
    kern = functools.partial(_proj_conv_kernel, tiles_per_batch=tpb, nb_s=nb_s, l_s=l_s,
                             nsplit=nsplit, width=width)
    return pl.pallas_call(
        kern,
        grid=(ncols // tn, tp // tm),
        in_specs=[
            pl.BlockSpec((tm, d), lambda j, i: (i, 0)),
            pl.BlockSpec((ts, d), lambda j, i: (0, 0)),
            pl.BlockSpec((tn, d), lambda j, i: (j + jb, 0)),
            pl.BlockSpec((CONV_M_K, tn), lambda j, i: (0, j + cb0)),
            pl.BlockSpec((1, tn), lambda j, i: (0, j + cb0)),
            pl.BlockSpec((nb_s, tail, tn), lambda j, i: (0, 0, j + cb0)),
        ],
        out_specs=[
            pl.BlockSpec((nsplit, tm, width), lambda j, i: (j, i, 0)),
            pl.BlockSpec((nsplit, ts, width), lambda j, i: (j, 0, 0)),
            pl.BlockSpec((1, tail, tn), lambda j, i: (i // tpb, 0, j)),
            pl.BlockSpec((nb_s, tail, tn), lambda j, i: (0, 0, j)),
        ],
        out_shape=[
            jax.ShapeDtypeStruct((ncols // width, tp, width), F32),
            jax.ShapeDtypeStruct((ncols // width, ts, width), F32),
            jax.ShapeDtypeStruct((n_batch_p, tail, ncols), F32),
            jax.ShapeDtypeStruct((nb_s, tail, ncols), F32),
        ],
        scratch_shapes=[
            pltpu.VMEM((tn, d), BF16),
            pltpu.VMEM((SUBLANES + tm, tn), F32),
            pltpu.VMEM((SUBLANES + ts, tn), F32),
            pltpu.VMEM((tail, ts, tn), F32),
        ],
        compiler_params=_params(("arbitrary", "arbitrary")),
        name=f"proj_conv_{col0}",
    )(xnp, xns, w, conv_w, conv_b, state_s)


def _ssd_kernel(*refs, q, has_state, nchunks):
    (z_ref, xs_ref, b_ref, c_ref, dt_ref, dtb_ref, alog_ref, dsk_ref, gn_ref, *rest) = refs
    if has_state:
        hprev, *rest = rest
    (y_ref, oh, h_s, acg, rowt) = rest
    c = pl.program_id(1)
    nheads = N_GROUPS * HEADS_PER_GROUP

    @pl.when(c == 0)
    def _init():
        if has_state:
            h_s[...] = hprev[...]
        else:
            h_s[...] = jnp.zeros_like(h_s)

    dt = _softplus(dt_ref[...] + dtb_ref[...])
    da = dt * (-jnp.exp(alog_ref[...]))
    ri = lax.broadcasted_iota(jnp.int32, (q, q), 0)
    ci = lax.broadcasted_iota(jnp.int32, (q, q), 1)
    causal = ri >= ci
    tril = jnp.where(causal, 1.0, 0.0).astype(BF16)
    acum = sum(_dot(tril, p) for p in _split3(da))
    eye = jnp.where(lax.broadcasted_iota(jnp.int32, (nheads, nheads), 0)
                    == lax.broadcasted_iota(jnp.int32, (nheads, nheads), 1), 1.0, 0.0).astype(BF16)
    acum2 = acum * LOG2_E
    rowt[0] = sum(lax.dot_general(eye, p, NT_DIMS, preferred_element_type=F32) for p in _split3(acum2))
    rowt[1] = sum(lax.dot_general(eye, p, NT_DIMS, preferred_element_type=F32) for p in _split3(dt))
    wend = jnp.exp(acum[q - 1:q, :] - acum) * dt
    rowt[2] = sum(lax.dot_general(eye, p, NT_DIMS, preferred_element_type=F32) for p in _split3(wend))
    for g in range(N_GROUPS):
        acg[g] = acum2[:, g * HEADS_PER_GROUP:(g + 1) * HEADS_PER_GROUP]

    lane = lax.broadcasted_iota(jnp.int32, (1, LANES), 1)
    rowi = lax.broadcasted_iota(jnp.int32, (LANES, 1), 0)
    half_w = LANES // 2

    def group_body(g, carry):
        xs = xs_ref[g]
        bb = b_ref[g].astype(BF16)
        ccb16 = c_ref[g].astype(BF16)
        cb_ = lax.dot_general(ccb16, bb, NT_DIMS, preferred_element_type=F32)
        ac8 = acg[g]
        dsk = dsk_ref[g]
        lo_half = lane < half_w
        lo_rows = rowi < half_w
        ys = []
        for pair in range(HEADS_PER_GROUP // 2):
            sl = slice(pair * LANES, (pair + 1) * LANES)
            xp = xs[:, sl]
            hp = h_s[g, pl.ds(pair * LANES, LANES), :]
            ms_, acols, wrows = [], [], []
            for r in (2 * pair, 2 * pair + 1):
                head = g * HEADS_PER_GROUP + r
                acol = jnp.broadcast_to(ac8[:, r:r + 1], (q, LANES))
                arow = rowt[0, pl.ds(head, 1), :]
                drow = rowt[1, pl.ds(head, 1), :]
                decay = jnp.exp2(jnp.where(causal, acol[:, :q] - arow, -jnp.inf))
                ms_.append((cb_ * decay * drow).astype(BF16))
                acols.append(acol)
                wrows.append(jnp.broadcast_to(rowt[2, pl.ds(head, 1), :], (half_w, q)))
            xlo = jnp.where(lo_half, xp, 0.0).astype(BF16)
            xhi = jnp.where(lo_half, 0.0, xp).astype(BF16)
            if q % LANES == 0:
                ydiag = _dot(jnp.concatenate(ms_, axis=1), jnp.concatenate([xlo, xhi], axis=0))
            else:
                ydiag = _dot(ms_[0], xlo) + _dot(ms_[1], xhi)
            ea = jnp.where(lo_half, jnp.exp2(acols[0]), jnp.exp2(acols[1]))
            yoff = ea * lax.dot_general(ccb16, hp.astype(BF16), NT_DIMS, preferred_element_type=F32)
            ys.append(dsk[:, sl] * xp + ydiag + yoff)
            xwt = (xp.T * jnp.concatenate(wrows, axis=0)).astype(BF16)
            dlast = [jnp.broadcast_to(jnp.exp2(a[q - 1:q, :]), (LANES, LANES)) for a in acols]
            h_s[g, pl.ds(pair * LANES, LANES), :] = hp * jnp.where(lo_rows, dlast[0], dlast[1]) + _dot(xwt, bb)
        yg = jnp.concatenate(ys, axis=1)
        hh = yg * jax.nn.silu(z_ref[g])
        ms = jnp.mean(hh * hh, axis=-1, keepdims=True)
        y_ref[g] = (hh * lax.rsqrt(ms + EPS) * gn_ref[g]).astype(BF16)
        return carry

    lax.fori_loop(0, N_GROUPS, group_body, 0, unroll=2)

    @pl.when(c == nchunks - 1)
    def _():
        oh[...] = h_s[...]


def _ssd(z, xc, bc, dt_raw, prm, n_batch, seq, q, h_prev=None):
    nchunks = seq // q
    nheads = N_GROUPS * HEADS_PER_GROUP
    has_state = h_prev is not None
    g8 = N_GROUPS

    def tok(first):
        return lambda b, c: (first, b * nchunks + c, 0)

    def const3(b, c):
        return (0, 0, 0)

    in_specs = [
        pl.BlockSpec((g8, q, GROUP_W), tok(0)),
        pl.BlockSpec((g8, q, GROUP_W), tok(0)),
        pl.BlockSpec((g8, q, D_STATE), tok(0)),
        pl.BlockSpec((g8, q, D_STATE), tok(1)),
        pl.BlockSpec((q, nheads), lambda b, c: (b * nchunks + c, 0)),
        pl.BlockSpec((1, nheads), lambda b, c: (0, 0)),
        pl.BlockSpec((1, nheads), lambda b, c: (0, 0)),
        pl.BlockSpec((g8, 1, GROUP_W), const3),
        pl.BlockSpec((g8, 1, GROUP_W), const3),
    ]
    args = [z, xc, bc, bc, dt_raw, prm["dtb"], prm["alog"], prm["dsk"], prm["gn"]]

    def per_batch(shape):
        return pl.BlockSpec((None,) + shape, lambda b, c: (b,) + (0,) * len(shape))

    if has_state:
        in_specs.append(per_batch((g8, GROUP_W, D_STATE)))
        args.append(h_prev)
    t = n_batch * seq
    out_specs = [
        pl.BlockSpec((g8, q, GROUP_W), lambda b, c: (0, b * nchunks + c, 0)),
        per_batch((g8, GROUP_W, D_STATE)),
    ]
    out_shape = [
        jax.ShapeDtypeStruct((g8, t, GROUP_W), BF16),
        jax.ShapeDtypeStruct((n_batch, g8, GROUP_W, D_STATE), F32),
    ]
    scratch = [
        pltpu.VMEM((g8, GROUP_W, D_STATE), F32),
        pltpu.VMEM((g8, q, HEADS_PER_GROUP), F32),
        pltpu.VMEM((3, nheads, q), F32),
    ]
    kern = functools.partial(_ssd_kernel, q=q, has_state=has_state, nchunks=nchunks)
    return pl.pallas_call(
        kern,
        grid=(n_batch, nchunks),
        in_specs=in_specs,
        out_specs=out_specs,
        out_shape=out_shape,
        scratch_shapes=scratch,
        compiler_params=_params(("arbitrary", "arbitrary")),
        name="ssd_state" if has_state else "ssd",
    )(*args)


def _merge_kernel(ap_ref, as_ref, yp_ref, ys_ref, gap_ref, gmp_ref, gas_ref, gms_ref, wa_ref, wm_ref,
                  op_ref, os_ref, wab, wmb):
    def emit(a_ref, y_ref, ga_ref, gm_ref, o_ref):
        oa = _dot(a_ref[...], wab[...])
        y = jnp.concatenate([y_ref[g] for g in range(N_GROUPS)], axis=1)
        om = _dot(y, wmb[...])
        o_ref[...] = (jax.nn.sigmoid(ga_ref[...]) * oa + jax.nn.sigmoid(gm_ref[...]) * om).astype(BF16)

    @pl.when(pl.program_id(1) == 0)
    def _():
        wab[...] = wa_ref[...].astype(BF16)
        wmb[...] = wm_ref[...].astype(BF16)
        emit(as_ref, ys_ref, gas_ref, gms_ref, os_ref)

    emit(ap_ref, yp_ref, gap_ref, gmp_ref, op_ref)


def _merge(abv_p, abv_s, y_p, y_s, g_p, g_s, w_a_out, w_m_out):
    tp, dc = abv_p.shape
    ts = abv_s.shape[0]
    dm = w_a_out.shape[1]
    tn = 512
    tm = _pick(tp, (512, 256, 128))
    ncol = dm // tn
    return pl.pallas_call(
        _merge_kernel,
        grid=(ncol, tp // tm),
        in_specs=[
            pl.BlockSpec((tm, dc), lambda j, i: (i, 0)),
            pl.BlockSpec((ts, dc), lambda j, i: (0, 0)),
            pl.BlockSpec((N_GROUPS, tm, GROUP_W), lambda j, i: (0, i, 0)),
            pl.BlockSpec((N_GROUPS, ts, GROUP_W), lambda j, i: (0, 0, 0)),
            pl.BlockSpec((tm, tn), lambda j, i: (i, j)),
            pl.BlockSpec((tm, tn), lambda j, i: (i, j + ncol)),
            pl.BlockSpec((ts, tn), lambda j, i: (0, j)),
            pl.BlockSpec((ts, tn), lambda j, i: (0, j + ncol)),
            pl.BlockSpec((dc, tn), lambda j, i: (0, j)),
            pl.BlockSpec((N_GROUPS * GROUP_W, tn), lambda j, i: (0, j)),
        ],
        out_specs=[
            pl.BlockSpec((tm, tn), lambda j, i: (i, j)),
            pl.BlockSpec((ts, tn), lambda j, i: (0, j)),
        ],
        out_shape=[jax.ShapeDtypeStruct((tp, dm), BF16), jax.ShapeDtypeStruct((ts, dm), BF16)],
        scratch_shapes=[pltpu.VMEM((dc, tn), BF16), pltpu.VMEM((N_GROUPS * GROUP_W, tn), BF16)],
        compiler_params=_params(("arbitrary", "arbitrary")),
        name="merge",
    )(abv_p, abv_s, y_p, y_s, g_p, g_p, g_s, g_s, w_a_out, w_m_out)


def _res_kernel(ap_ref, as_ref, rp_ref, rs_ref, w_ref, g_ref, *outs, n_prompt_tiles, merged_out, ts):
    if merged_out:
        h_ref, hn_ref, wbf = outs
    else:
        hp_ref, hs_ref, hnp_ref, hns_ref, wbf = outs
    i = pl.program_id(0)

    @pl.when(i == 0)
    def _():
        wbf[...] = w_ref[...].astype(BF16)

    def emit(a_ref, r_ref, store_h, store_hn):
        h = r_ref[...] + _dot(a_ref[...], wbf[...])
        store_h(h)
        store_hn(_rms(h, g_ref[...]))

    if merged_out:
        @pl.when(i < n_prompt_tiles)
        def _():
            def sh(h):
                h_ref[...] = h

            def shn(hn):
                hn_ref[...] = hn

            emit(ap_ref, rp_ref, sh, shn)

        @pl.when(i == n_prompt_tiles)
        def _():
            def sh(h):
                h_ref[pl.ds(0, ts), :] = h

            def shn(hn):
                hn_ref[pl.ds(0, ts), :] = hn

            emit(as_ref, rs_ref, sh, shn)
    else:
        def shp(h):
            hp_ref[...] = h

        def shnp(hn):
            hnp_ref[...] = hn.astype(hnp_ref.dtype)

        emit(ap_ref, rp_ref, shp, shnp)

        @pl.when(i == 0)
        def _():
            def shs(h):
                hs_ref[...] = h

            def shns(hn):
                hns_ref[...] = hn.astype(hns_ref.dtype)

            emit(as_ref, rs_ref, shs, shns)


def _res(a_p, a_s, r_p, r_s, w, g, merged_out, hn_dtype):
    tp, d = a_p.shape
    ts = a_s.shape[0]
    tm = 256 if merged_out else _pick(tp, (512, 256))
    assert tp % tm == 0 and ts <= tm
    npt = tp // tm
    last = npt - 1
    in_specs = [
        pl.BlockSpec((tm, d), lambda i: (jnp.minimum(i, last), 0)),
        pl.BlockSpec((ts, d), lambda i: (0, 0)),
        pl.BlockSpec((tm, d), lambda i: (jnp.minimum(i, last), 0)),
        pl.BlockSpec((ts, d), lambda i: (0, 0)),
        pl.BlockSpec((d, d), lambda i: (0, 0), pipeline_mode=pl.Buffered(1)),
        pl.BlockSpec((1, d), lambda i: (0, 0)),
    ]
    if merged_out:
        grid = (npt + 1,)
        out_specs = [pl.BlockSpec((tm, d), lambda i: (i, 0)), pl.BlockSpec((tm, d), lambda i: (i, 0))]
        out_shape = [jax.ShapeDtypeStruct((tp + ts, d), F32), jax.ShapeDtypeStruct((tp + ts, d), hn_dtype)]
    else:
        grid = (npt,)
        out_specs = [pl.BlockSpec((tm, d), lambda i: (i, 0)), pl.BlockSpec((ts, d), lambda i: (0, 0)),
                     pl.BlockSpec((tm, d), lambda i: (i, 0)), pl.BlockSpec((ts, d), lambda i: (0, 0))]
        out_shape = [jax.ShapeDtypeStruct((tp, d), F32), jax.ShapeDtypeStruct((ts, d), F32),
                     jax.ShapeDtypeStruct((tp, d), hn_dtype), jax.ShapeDtypeStruct((ts, d), hn_dtype)]
    kern = functools.partial(_res_kernel, n_prompt_tiles=npt, merged_out=merged_out, ts=ts)
    return pl.pallas_call(
        kern,
        grid=grid,
        in_specs=in_specs,
        out_specs=out_specs,
        out_shape=out_shape,
        scratch_shapes=[pltpu.VMEM((d, d), BF16)],
        compiler_params=_params(("arbitrary",)),
        name="res_merged" if merged_out else "res",
    )(a_p, a_s, r_p, r_s, w, g)


def _kv_kernel(m_ref, g_ref, wk_ref, wv_ref, k_hbm, v_hbm, mn, kvb, sem, *, n_batch):
    j = pl.program_id(0)

    @pl.when(j == 0)
    def _():
        mn[...] = _rms(m_ref[...], g_ref[...]).astype(BF16)

    kvb[0] = _dot(mn[...], wk_ref[...].astype(BF16))
    kvb[1] = _dot(mn[...], wv_ref[...].astype(BF16))
    nm = kvb.shape[1] // n_batch
    for h in range(N_XHEADS):
        @pl.when(j == h)
        def _(h=h):
            copies = [pltpu.make_async_copy(kvb.at[t, pl.ds(b * nm, nm)], dst.at[b, :, h, :], sem.at[t, b])
                      for t, dst in enumerate((k_hbm, v_hbm)) for b in range(n_batch)]
            for c in copies:
                c.start()
            for c in copies:
                c.wait()


def _memory_kv(mem2d, g, w_k, w_v, n_batch):
    m, d = mem2d.shape
    dh = d // N_XHEADS
    out_sds = jax.ShapeDtypeStruct((n_batch, m // n_batch, N_XHEADS, dh), F32)
    kern = functools.partial(_kv_kernel, n_batch=n_batch)
    return pl.pallas_call(
        kern,
        grid=(N_XHEADS,),
        in_specs=[
            pl.BlockSpec((m, d), lambda j: (0, 0)),
            pl.BlockSpec((1, d), lambda j: (0, 0)),
            pl.BlockSpec((d, dh), lambda j: (0, j)),
            pl.BlockSpec((d, dh), lambda j: (0, j)),
        ],
        out_specs=[pl.BlockSpec(memory_space=pl.ANY), pl.BlockSpec(memory_space=pl.ANY)],
        out_shape=[out_sds, out_sds],
        scratch_shapes=[pltpu.VMEM((m, d), BF16), pltpu.VMEM((2, m, dh), F32),
                        pltpu.SemaphoreType.DMA((2, n_batch))],
        compiler_params=_params(("arbitrary",)),
        name="memory_kv",
    )(mem2d, g, w_k, w_v)


def _attn_kernel(hn_ref, k_ref, v_ref, wq_ref, o_ref, wqb, kb, vb, kvf, sem, *q_all):
    b = pl.program_id(0)
    i = pl.program_id(1)
    tm = o_ref.shape[0]

    @pl.when((b == 0) & (i == 0))
    def _():
        wqb[...] = wq_ref[...].astype(BF16)
        if q_all:
            q_all[0][...] = _dot(hn_ref[...], wqb[...])

    d = wqb.shape[1]
    dh = d // N_XHEADS

    @pl.when(i == 0)
    def _():
        def copies(bb):
            slot = bb % 2
            return [pltpu.make_async_copy(src.at[bb, :, h, :], kvf.at[slot, t, h], sem.at[slot, t, h])
                    for t, src in enumerate((k_ref, v_ref)) for h in range(N_XHEADS)]

        @pl.when(b == 0)
        def _():
            for c in copies(b):
                c.start()

        for c in copies(b):
            c.wait()

        @pl.when(b + 1 < pl.num_programs(0))
        def _():
            for c in copies(b + 1):
                c.start()

        for h in range(N_XHEADS):
            kb[:, h * dh:(h + 1) * dh] = kvf[b % 2, 0, h].astype(BF16)
            vb[:, h * dh:(h + 1) * dh] = kvf[b % 2, 1, h].astype(BF16)

    if q_all:
        row0 = pl.multiple_of((b * pl.num_programs(1) + i) * tm, SUBLANES)
        q = q_all[0][pl.ds(row0, tm), :]
    else:
        q = _dot(hn_ref[...], wqb[...])
    outs = []
    for h in range(N_XHEADS):
        sl = slice(h * dh, (h + 1) * dh)
        s = lax.dot_general(q[:, sl].astype(BF16), kb[:, sl], NT_DIMS, preferred_element_type=F32)
        s = s * (dh ** -0.5)
        e = jnp.exp(s - jnp.max(s, axis=-1, keepdims=True))
        p = e / jnp.sum(e, axis=-1, keepdims=True)
        outs.append(_dot(p.astype(BF16), vb[:, sl]))
    o_ref[...] = jnp.concatenate(outs, axis=1).astype(BF16)


def _attn(hn, k, v, w_q, n_batch, seq):
    t, d = hn.shape
    nm, nh, dh = k.shape[1:]
    tm = _pick(seq, (512, 256, 128, 64, 32, 16))
    tpb = seq // tm
    kv_spec = pl.BlockSpec(memory_space=pl.ANY)
    scratch = [pltpu.VMEM((d, d), BF16), pltpu.VMEM((nm, d), BF16), pltpu.VMEM((nm, d), BF16),
               pltpu.VMEM((2, 2, nh, nm, dh), F32), pltpu.SemaphoreType.DMA((2, 2, nh))]
    if tm < 128 and t <= 512:
        hn_spec = pl.BlockSpec((t, d), lambda b, i: (0, 0))
        scratch.append(pltpu.VMEM((t, d), F32))
    else:
        hn_spec = pl.BlockSpec((tm, d), lambda b, i: (b * tpb + i, 0))
    return pl.pallas_call(
        _attn_kernel,
        grid=(n_batch, tpb),
        in_specs=[
            hn_spec,
            kv_spec,
            kv_spec,
            pl.BlockSpec((d, d), lambda b, i: (0, 0), pipeline_mode=pl.Buffered(1)),
        ],
        out_specs=pl.BlockSpec((tm, d), lambda b, i: (b * tpb + i, 0)),
        out_shape=jax.ShapeDtypeStruct((t, d), BF16),
        scratch_shapes=scratch,
        compiler_params=_params(("arbitrary", "arbitrary")),
        name=f"attn_{seq}",
    )(hn, k, v, w_q)


def _router_kernel(x_ref, w_ref, b_ref, ri_ref, rw_ref):
    logits = _dot(x_ref[...].astype(BF16), w_ref[...].astype(BF16)) + b_ref[...]
    lane_i = lax.broadcasted_iota(jnp.int32, logits.shape, 1)
    lane = lane_i.astype(F32)
    ninf = -jnp.inf
    big = float(LANES)
    is_g = lane < N_EXPERT_GROUPS
    gl = jnp.where(is_g, logits, ninf)
    gmax = jnp.max(gl, axis=-1, keepdims=True)
    gsel = jnp.min(jnp.where(gl == gmax, lane, big), axis=-1, keepdims=True)
    pg = 1.0 / jnp.sum(jnp.where(is_g, jnp.exp(gl - gmax), 0.0), axis=-1, keepdims=True)
    lo = N_EXPERT_GROUPS + EXPERTS_PER_GROUP * gsel
    el = jnp.where(lane >= lo, jnp.where(lane < lo + EXPERTS_PER_GROUP, logits, ninf), ninf)
    m1 = jnp.max(el, axis=-1, keepdims=True)
    i1 = jnp.min(jnp.where(el == m1, lane, big), axis=-1, keepdims=True)
    el2 = jnp.where(lane == i1, ninf, el)
    m2 = jnp.max(el2, axis=-1, keepdims=True)
    i2 = jnp.min(jnp.where(el2 == m2, lane, big), axis=-1, keepdims=True)
    e = jnp.exp(m2 - m1)
    w1 = pg / (1.0 + e)
    w2 = pg * e / (1.0 + e)
    ri_ref[...] = jnp.where(lane_i == 0, i1 - N_EXPERT_GROUPS,
                            jnp.where(lane_i == 1, i2 - N_EXPERT_GROUPS, 0.0)).astype(jnp.int32)
    rw_ref[...] = jnp.where(lane_i == 0, w1, jnp.where(lane_i == 1, w2, 0.0))


def _router(hn_all, w_r, b_r):
    t, d = hn_all.shape
    tm = _pick(t, (640, 512, 384, 256, 128))
    return pl.pallas_call(
        _router_kernel,
        grid=(t // tm,),
        in_specs=[
            pl.BlockSpec((tm, d), lambda i: (i, 0)),
            pl.BlockSpec((d, LANES), lambda i: (0, 0)),
            pl.BlockSpec((1, LANES), lambda i: (0, 0)),
        ],
        out_specs=[pl.BlockSpec((tm, LANES), lambda i: (i, 0)), pl.BlockSpec((tm, LANES), lambda i: (i, 0))],
        out_shape=[jax.ShapeDtypeStruct((t, LANES), jnp.int32), jax.ShapeDtypeStruct((t, LANES), F32)],
        compiler_params=_params(("arbitrary",)),
        name="router",
    )(hn_all, w_r, b_r)


def _row_copy(src, dst, s, d, sem):
    return pltpu.make_async_copy(src.at[pl.ds(s, 1)], dst.at[pl.ds(d, 1)], sem)


def _dispatch_kernel(dest_ref, pad0_ref, padn_ref, nsub_ref, x_ref, o_ref, zrow, zblk, sem, zsem, csem, bsem, *,
                     chunk, n_blocks):
    base = pl.program_id(0) * chunk

    @pl.when(pl.program_id(0) == 0)
    def _():
        zrow[...] = jnp.zeros_like(zrow)
        zblk[...] = jnp.zeros_like(zblk)

        def pad_plan(e):
            p0 = pad0_ref[e]
            head = jnp.minimum(padn_ref[e], (SUBLANES - p0 % SUBLANES) % SUBLANES)
            return p0, head, (padn_ref[e] - head) // SUBLANES

        def group_copy(row):
            return pltpu.make_async_copy(zrow, o_ref.at[pl.ds(pl.multiple_of(row, SUBLANES), SUBLANES)], csem)

        def pad_start(e, carry):
            p0, head, ngroups = pad_plan(e)

            def start_row(r, c):
                _row_copy(zrow, o_ref, 0, p0 + r, zsem).start()
                return c

            def start_group(j, c):
                group_copy(p0 + head + j * SUBLANES).start()
                return c

            lax.fori_loop(0, head, start_row, 0)
            return lax.fori_loop(0, ngroups, start_group, carry)

        def pad_wait(e, carry):
            _, head, ngroups = pad_plan(e)

            def wait_row(r, c):
                _row_copy(zrow, o_ref, 0, 0, zsem).wait()
                return c

            def wait_group(j, c):
                group_copy(0).wait()
                return c

            lax.fori_loop(0, head, wait_row, 0)
            return lax.fori_loop(0, ngroups, wait_group, carry)

        lax.fori_loop(0, N_EXPERTS, pad_start, 0)

        def blk_copy(b):
            return pltpu.make_async_copy(zblk, o_ref.at[pl.ds(pl.multiple_of(b * EXPERT_SUB, EXPERT_SUB),
                                                             EXPERT_SUB)], bsem)

        def tail_start(b, c):
            blk_copy(b).start()
            return c

        def tail_wait(b, c):
            blk_copy(b).wait()
            return c

        lax.fori_loop(nsub_ref[0], n_blocks, tail_start, 0)
        lax.fori_loop(0, N_EXPERTS, pad_wait, 0)
        lax.fori_loop(nsub_ref[0], n_blocks, tail_wait, 0)

    def issue(r, carry):
        t = base + r
        _row_copy(x_ref, o_ref, r, dest_ref[2 * t], sem).start()
        _row_copy(x_ref, o_ref, r, dest_ref[2 * t + 1], sem).start()
        return carry

    lax.fori_loop(0, chunk, issue, 0, unroll=8)

    for _ in range(2):
        pltpu.make_async_copy(x_ref, o_ref.at[pl.ds(0, chunk)], sem).wait()


def _dispatch(tables, hn_all, n_blocks):
    dest, pad0, padn, nsub = tables
    t, d = hn_all.shape
    chunk = _pick(t, (640, 512, 384, 256, 128))
    kern = functools.partial(_dispatch_kernel, chunk=chunk, n_blocks=n_blocks)
    return pl.pallas_call(
        kern,
        grid_spec=pltpu.PrefetchScalarGridSpec(
            num_scalar_prefetch=4,
            grid=(t // chunk,),
            in_specs=[pl.BlockSpec((chunk, d), lambda i, *_: (i, 0))],
            out_specs=pl.BlockSpec(memory_space=pl.ANY),
            scratch_shapes=[pltpu.VMEM((SUBLANES, d), F32), pltpu.VMEM((EXPERT_SUB, d), F32),
                            pltpu.SemaphoreType.DMA(()), pltpu.SemaphoreType.DMA(()),
                            pltpu.SemaphoreType.DMA(()), pltpu.SemaphoreType.DMA(())],
        ),
        out_shape=jax.ShapeDtypeStruct((n_blocks * EXPERT_SUB, d), F32),
        compiler_params=_params(("arbitrary",)),
        name="dispatch",
    )(dest, pad0, padn, nsub, hn_all)


def _expert_kernel(bstart_ref, wg_ref, wu_ref, wd_ref, x_hbm, y_hbm, gcache, ucache, dcache, xbuf, ybuf, xb,
                   xsem, ysem, *, n_blocks):
    e = pl.program_id(0)
    k = pl.program_id(1)
    nk = gcache.shape[1]
    total = bstart_ref[N_EXPERTS]

    def x_copy(b, slot):
        rows = pl.ds(pl.multiple_of(b * EXPERT_SUB, EXPERT_SUB), EXPERT_SUB)
        return pltpu.make_async_copy(x_hbm.at[rows], xbuf.at[slot], xsem.at[slot])

    def y_copy(b, slot):
        rows = pl.ds(pl.multiple_of(b * EXPERT_SUB, EXPERT_SUB), EXPERT_SUB)
        return pltpu.make_async_copy(ybuf.at[slot], y_hbm.at[rows], ysem.at[slot])

    @pl.when(e < N_EXPERTS)
    def _():
        slot = e % 2
        gcache[slot, k] = wg_ref[...].astype(BF16)
        ucache[slot, k] = wu_ref[...].astype(BF16)
        dcache[slot, k] = wd_ref[...].astype(BF16)

    @pl.when((e == 0) & (k == 0) & (total > 0))
    def _():
        x_copy(0, 0).start()

    @pl.when(e >= 1)
    def _():
        owner = e - 1
        wslot = owner % 2
        b0 = bstart_ref[owner]
        n = bstart_ref[owner + 1] - b0

        cpu = 2
        upb = nk // cpu

        def unit(u, carry):
            b = b0 + u // upb
            part = u % upb
            slot = b % 2

            def swiglu(x, p):
                kks = [p * cpu + c for c in range(cpu)]
                wg = jnp.concatenate([gcache[wslot, kk] for kk in kks], axis=1)
                wu = jnp.concatenate([ucache[wslot, kk] for kk in kks], axis=1)
                wd = jnp.concatenate([dcache[wslot, kk] for kk in kks], axis=0)
                hid = (jax.nn.silu(_dot(x, wg)) * _dot(x, wu)).astype(BF16)
                return _dot(hid, wd)

            @pl.when(part == 0)
            def _():
                x_copy(b, slot).wait()

                @pl.when(b + 1 < total)
                def _():
                    x_copy(b + 1, 1 - slot).start()

                x = xbuf[slot].astype(BF16)
                xb[...] = x

                @pl.when(b >= 2)
                def _():
                    y_copy(b - 2, slot).wait()

                ybuf[slot] = swiglu(x, 0)

            for p in range(1, upb):
                @pl.when(part == p)
                def _(p=p):
                    ybuf[slot] += swiglu(xb[...], p)
                    if p == upb - 1:
                        y_copy(b, slot).start()

            return carry

        lax.fori_loop((n * upb * k) // nk, (n * upb * (k + 1)) // nk, unit, 0)

    @pl.when((e == N_EXPERTS) & (k == nk - 1))
    def _():
        @pl.when(total >= 2)
        def _():
            y_copy(total - 2, total % 2).wait()

        @pl.when(total >= 1)
        def _():
            y_copy(total - 1, (total - 1) % 2).wait()

        ybuf[0] = jnp.zeros(ybuf.shape[1:], F32)

        def zfill_start(b, carry):
            y_copy(b, 0).start()
            return carry

        def zfill_wait(b, carry):
            y_copy(b, 0).wait()
            return carry

        lax.fori_loop(total, n_blocks, zfill_start, 0)
        lax.fori_loop(total, n_blocks, zfill_wait, 0)


def _experts(bstart, x_sorted, w_gate, w_up, w_down, kchunk):
    nrows, d = x_sorted.shape
    n_blocks = nrows // EXPERT_SUB
    de = w_gate.shape[2]
    nk = de // kchunk
    last = N_EXPERTS - 1

    def widx(e, k):
        return jnp.minimum(e, last), jnp.where(e <= last, k, nk - 1)

    def in_idx(e, k, b):
        ee, kk = widx(e, k)
        return (ee, 0, kk)

    def down_idx(e, k, b):
        ee, kk = widx(e, k)
        return (ee, kk, 0)

    kern = functools.partial(_expert_kernel, n_blocks=n_blocks)
    return pl.pallas_call(
        kern,
        grid_spec=pltpu.PrefetchScalarGridSpec(
            num_scalar_prefetch=1,
            grid=(N_EXPERTS + 1, nk),
            in_specs=[
                pl.BlockSpec((None, d, kchunk), in_idx),
                pl.BlockSpec((None, d, kchunk), in_idx),
                pl.BlockSpec((None, kchunk, d), down_idx),
                pl.BlockSpec(memory_space=pl.ANY),
            ],
            out_specs=pl.BlockSpec(memory_space=pl.ANY),
            scratch_shapes=[
                pltpu.VMEM((2, nk, d, kchunk), BF16),
                pltpu.VMEM((2, nk, d, kchunk), BF16),
                pltpu.VMEM((2, nk, kchunk, d), BF16),
                pltpu.VMEM((2, EXPERT_SUB, d), F32),
                pltpu.VMEM((2, EXPERT_SUB, d), F32),
                pltpu.VMEM((EXPERT_SUB, d), BF16),
                pltpu.SemaphoreType.DMA((2,)),
                pltpu.SemaphoreType.DMA((2,)),
            ],
        ),
        out_shape=jax.ShapeDtypeStruct((nrows, d), F32),
        compiler_params=_params(("arbitrary", "arbitrary")),
        name="experts",
    )(bstart, w_gate, w_up, w_down, x_sorted)


def _final_kernel(dest_ref, h_ref, rw_ref, ys_ref, g_ref, yp_ref, yss_ref, ya, yb, sem, *,
                  tm, n_prompt_tiles, ts):
    i = pl.program_id(0)

    def gather(tile, n):
        slot = tile % 2

        def issue(r, carry):
            t = tile * tm + r
            _row_copy(ys_ref, ya.at[slot], dest_ref[2 * t], r, sem.at[slot]).start()
            _row_copy(ys_ref, yb.at[slot], dest_ref[2 * t + 1], r, sem.at[slot]).start()
            return carry

        lax.fori_loop(0, n, issue, 0, unroll=8)

    def emit(n, o_ref):
        slot = i % 2
        for buf in (ya, yb):
            pltpu.make_async_copy(ys_ref.at[pl.ds(0, n)], buf.at[slot, pl.ds(0, n)], sem.at[slot]).wait()
        w = rw_ref[pl.ds(0, n), :]
        h = (h_ref[pl.ds(0, n), :] + w[:, 0:1] * ya[slot, pl.ds(0, n), :]
             + w[:, 1:2] * yb[slot, pl.ds(0, n), :])
        o_ref[...] = _rms(h, g_ref[...])

    @pl.when(i == 0)
    def _():
        gather(i, tm)

    @pl.when(i + 1 < n_prompt_tiles)
    def _():
        gather(i + 1, tm)

    @pl.when(i + 1 == n_prompt_tiles)
    def _():
        gather(i + 1, ts)

    @pl.when(i < n_prompt_tiles)
    def _():
        emit(tm, yp_ref)

    @pl.when(i == n_prompt_tiles)
    def _():
        emit(ts, yss_ref)


def _final(dest, h_all, rw, y_sorted, g, tp, ts):
    t, d = h_all.shape
    tm = 256
    assert tp % tm == 0 and ts <= tm and t == tp + ts
    npt = tp // tm
    kern = functools.partial(_final_kernel, tm=tm, n_prompt_tiles=npt, ts=ts)
    return pl.pallas_call(
        kern,
        grid_spec=pltpu.PrefetchScalarGridSpec(
            num_scalar_prefetch=1,
            grid=(npt + 1,),
            in_specs=[
                pl.BlockSpec((tm, d), lambda i, dref: (i, 0)),
                pl.BlockSpec((tm, LANES), lambda i, dref: (i, 0)),
                pl.BlockSpec(memory_space=pl.ANY),
                pl.BlockSpec((1, d), lambda i, dref: (0, 0)),
            ],
            out_specs=[
                pl.BlockSpec((tm, d), lambda i, dref: (jnp.minimum(i, npt - 1), 0)),
                pl.BlockSpec((ts, d), lambda i, dref: (0, 0)),
            ],
            scratch_shapes=[pltpu.VMEM((2, tm, d), F32), pltpu.VMEM((2, tm, d), F32),
                            pltpu.SemaphoreType.DMA((2,))],
        ),
        out_shape=[jax.ShapeDtypeStruct((tp, d), F32), jax.ShapeDtypeStruct((ts, d), F32)],
        compiler_params=_params(("arbitrary",)),
        name="final",
    )(dest, h_all, rw, y_sorted, g)


def _routing_tables(eid):
    i32 = jnp.int32
    onehot = (eid[:, None] == jnp.arange(N_EXPERTS, dtype=i32)[None, :]).astype(i32)
    csum = jnp.cumsum(onehot, axis=0)
    counts = csum[-1]
    nsub = (counts + EXPERT_SUB - 1) // EXPERT_SUB
    bend = jnp.cumsum(nsub)
    bstart = bend - nsub
    seg = bstart * EXPERT_SUB
    dest = jnp.sum(onehot * (csum - 1 + seg[None, :]), axis=1).astype(i32)
    pad0 = (seg + counts).astype(i32)
    padn = (nsub * EXPERT_SUB - counts).astype(i32)
    bstart_all = jnp.concatenate([bstart, bend[-1:]]).astype(i32)
    return (dest, pad0, padn, bend[-1:].astype(i32)), bstart_all


def _group_major(a, width):
    lead = a.shape[:-1]
    g = a.shape[-1] // width
    return jnp.moveaxis(a.reshape(lead + (g, width)), -2, 0)


def kernel(x_prompt, x_sample, cache_mem_k, cache_mem_v, state_conv_a, state_conv_m, state_ssm, mem_prompt,
           norm_mix, w_in, conv_a_w, w_a_out, conv_m_w, conv_m_b, dt_bias, a_log, d_skip, ssm_norm,
           w_m_out, w_o, norm_cross, norm_mem, w_q, w_k, w_v, w_co, norm_ffn, w_rg, b_rg, w_re, b_re,
           w_gate, w_up, w_down, norm_final):
    depth = w_in.shape[0]
    assert depth == 1, "single-layer step"
    nbp, seq_p, d = x_prompt.shape
    nbs, seq_s, _ = x_sample.shape
    tp, ts = nbp * seq_p, nbs * seq_s
    n_mem = mem_prompt.shape[1]
    d_conv = conv_a_w.shape[2]
    d_inner = w_m_out.shape[1]
    nheads = dt_bias.shape[1]
    bc_w = N_GROUPS * D_STATE
    l = 0

    xp = x_prompt.reshape(tp, d)
    xs = x_sample.reshape(ts, d)
    w_in_t = jnp.swapaxes(w_in[l], 0, 1)
    col_z = 3 * d_conv
    col_x = col_z + d_inner
    col_b = col_x + d_inner
    col_dt = col_b + 2 * bc_w
    col_g = col_dt + nheads
    w_dt = w_in_t[col_dt:col_g]

    mk4, mv4 = _memory_kv(mem_prompt.reshape(nbp * n_mem, d), norm_mem[l][None], w_k[l], w_v[l], nbp)

    xnp, xns, dt_p, dt_s = _norm_dt(xp, xs, norm_mix[l][None], w_dt)
    abv_p, abv_s, ca_p, ca_s = _proj_a(xnp, xns, w_in_t, conv_a_w[l], state_conv_a[l], nbp, seq_p, d_conv)
    z_p, z_s = _proj_raw(xnp, xns, w_in_t, col_z, d_inner, GROUP_W)
    cw, cbias, scm = conv_m_w[l], conv_m_b[l][None], state_conv_m[l]
    xc_p, xc_s, cmx_p, cmx_s = _proj_conv(xnp, xns, w_in_t, cw, cbias, scm, col_x, 0, d_inner, GROUP_W,
                                           nbp, seq_p)
    bc_p, bc_s, cmb_p, cmb_s = _proj_conv(xnp, xns, w_in_t, cw, cbias, scm, col_b, d_inner, 2 * bc_w, D_STATE,
                                           nbp, seq_p)
    g_p, g_s = _proj_raw(xnp, xns, w_in_t, col_g, 2 * d, 0)

    prm = {
        "dtb": dt_bias[l][None],
        "alog": a_log[l][None],
        "dsk": _group_major(jnp.repeat(d_skip[l], HEAD_DIM)[None], GROUP_W),
        "gn": _group_major(ssm_norm[l][None], GROUP_W),
    }
    q_p = _pick(seq_p, (128, 64, 32, 16, 8))
    y_p, h_p = _ssd(z_p, xc_p, bc_p, dt_p, prm, nbp, seq_p, q_p)
    y_s, h_s = _ssd(z_s, xc_s, bc_s, dt_s, prm, nbs, seq_s, seq_s,
                    h_prev=state_ssm[l].reshape(nbs, N_GROUPS, GROUP_W, D_STATE))

    merged_p, merged_s = _merge(abv_p, abv_s, y_p, y_s, g_p, g_s, w_a_out[l],
                                w_m_out[l])
    h1_p, h1_s, hn1_p, hn1_s = _res(merged_p, merged_s, xp, xs, w_o[l], norm_cross[l][None], False, BF16)

    att_p = _attn(hn1_p, mk4, mv4, w_q[l], nbp, seq_p)
    att_s = _attn(hn1_s, cache_mem_k[l], cache_mem_v[l], w_q[l], nbs, seq_s)
    h2_all, hn2_all = _res(att_p, att_s, h1_p, h1_s, w_co[l], norm_ffn[l][None], True, F32)

    npad = LANES - N_EXPERT_GROUPS - N_EXPERTS
    w_r = jnp.concatenate([w_rg[l], w_re[l], jnp.zeros((d, npad), F32)], axis=1)
    b_r = jnp.concatenate([b_rg[l], b_re[l], jnp.zeros((npad,), F32)])[None]
    ri, rw = _router(hn2_all, w_r, b_r)
    tables, bstart = _routing_tables(ri[:, :2].reshape(-1))
    n_blocks = -(-2 * (tp + ts) // EXPERT_SUB) + N_EXPERTS
    x_sorted = _dispatch(tables, hn2_all, n_blocks)
    y_sorted = _experts(bstart, x_sorted, w_gate[l], w_up[l], w_down[l], 256)
    y_prompt, y_sample = _final(tables[0], h2_all, rw, y_sorted, norm_final[None], tp, ts)

    xh = d // N_XHEADS
    return (
        y_prompt.reshape(nbp, seq_p, d),
        y_sample.reshape(nbs, seq_s, d),
        mk4[None],
        mv4[None],
        ca_p[None],
        jnp.concatenate([cmx_p, cmb_p], axis=-1)[None],
        h_p.reshape(1, nbp, nheads, HEAD_DIM, D_STATE),
        ca_s[None],
        jnp.concatenate([cmx_s, cmb_s], axis=-1)[None],
        h_s.reshape(1, nbs, nheads, HEAD_DIM, D_STATE),
    )
```

```python
import functools

import jax
import jax.numpy as jnp
from jax import lax
from jax.experimental import pallas as pl
from jax.experimental.pallas import tpu as pltpu

F32 = jnp.float32
BF16 = jnp.bfloat16
EPS = 1e-6
LOG2_E = 1.4426950408889634

V7X_VMEM_BYTES = 64 * 1024 * 1024
VMEM_LIMIT = V7X_VMEM_BYTES - 8 * 1024 * 1024
LANES = 128
SUBLANES = 8

N_GROUPS = 8
HEADS_PER_GROUP = 8
HEAD_DIM = 64
D_STATE = 128
GROUP_W = HEADS_PER_GROUP * HEAD_DIM
N_XHEADS = 4
N_EXPERTS = 32
N_EXPERT_GROUPS = 4
EXPERTS_PER_GROUP = 8
EXPERT_SUB = 256
CONV_A_K = 3
CONV_M_K = 4

NT_DIMS = (((1,), (1,)), ((), ()))
TN_DIMS = (((0,), (0,)), ((), ()))


def _params(sem):
    return pltpu.CompilerParams(dimension_semantics=sem, vmem_limit_bytes=VMEM_LIMIT)


def _pick(n, cands):
    for c in cands:
        if n % c == 0:
            return c
    raise ValueError(f"no tile for {n} in {cands}")


def _dot(a, b):
    return jnp.dot(a, b, preferred_element_type=F32)


def _dot_nt(a, b):
    return lax.dot_general(a, b, NT_DIMS, preferred_element_type=F32)


def _rms(x, g):
    return x * lax.rsqrt(jnp.mean(x * x, axis=-1, keepdims=True) + EPS) * g


def _split3(x):
    hi = x.astype(BF16)
    r = x - hi.astype(F32)
    mid = r.astype(BF16)
    lo = (r - mid.astype(F32)).astype(BF16)
    return hi, mid, lo


def _softplus(x):
    return jnp.maximum(x, 0.0) + jnp.log1p(jnp.exp(-jnp.abs(x)))


def _norm_dt_kernel(xp_ref, xs_ref, g_ref, wdt_ref, xnp_ref, xns_ref, dtp_ref, dts_ref):
    wdt = wdt_ref[...].astype(BF16)

    def one(x_ref, xn_ref, dt_ref):
        xn = _rms(x_ref[...], g_ref[...]).astype(BF16)
        xn_ref[...] = xn
        dt_ref[...] = _dot_nt(xn, wdt)

    one(xp_ref, xnp_ref, dtp_ref)

    @pl.when(pl.program_id(0) == 0)
    def _():
        one(xs_ref, xns_ref, dts_ref)


def _norm_dt(xp, xs, g, wdt):
    tp, d = xp.shape
    ts = xs.shape[0]
    nh = wdt.shape[0]
    tm = _pick(tp, (512, 256, 128))
    return pl.pallas_call(
        _norm_dt_kernel,
        grid=(tp // tm,),
        in_specs=[
            pl.BlockSpec((tm, d), lambda i: (i, 0)),
            pl.BlockSpec((ts, d), lambda i: (0, 0)),
            pl.BlockSpec((1, d), lambda i: (0, 0)),
            pl.BlockSpec((nh, d), lambda i: (0, 0)),
        ],
        out_specs=[
            pl.BlockSpec((tm, d), lambda i: (i, 0)),
            pl.BlockSpec((ts, d), lambda i: (0, 0)),
            pl.BlockSpec((tm, nh), lambda i: (i, 0)),
            pl.BlockSpec((ts, nh), lambda i: (0, 0)),
        ],
        out_shape=[
            jax.ShapeDtypeStruct((tp, d), BF16),
            jax.ShapeDtypeStruct((ts, d), BF16),
            jax.ShapeDtypeStruct((tp, nh), F32),
            jax.ShapeDtypeStruct((ts, nh), F32),
        ],
        compiler_params=_params(("arbitrary",)),
        name="norm_dt",
    )(xp, xs, g, wdt)


def _proj_a_kernel(xp_ref, xs_ref, wb_ref, wc_ref, wh_ref, cw_ref, st_ref,
                   op_ref, os_ref, cap_ref, cas_ref,
                   wbf, ubuf, sbuf, s1buf, s2buf, *, tiles_per_batch, nb_s, l_s):
    i = pl.program_id(1)
    tm = xp_ref.shape[0]
    ts = xs_ref.shape[0]
    cw = cw_ref[...]

    @pl.when(i == 0)
    def _():
        wbf[0] = wb_ref[...].astype(BF16)
        wbf[1] = wc_ref[...].astype(BF16)
        wbf[2] = wh_ref[...].astype(BF16)
        x = xs_ref[...]
        u = _dot_nt(x, wbf[1]) * _dot_nt(x, wbf[2])
        sbuf[pl.ds(0, SUBLANES), :] = jnp.zeros((SUBLANES, u.shape[1]), F32)
        sbuf[pl.ds(SUBLANES, ts), :] = u
        s1buf[...] = jnp.zeros_like(s1buf)
        s2buf[...] = jnp.zeros_like(s2buf)
        for b in range(nb_s):
            s1buf[pl.ds(b * l_s, 1), :] = st_ref[b, pl.ds(1, 1), :]
            s2buf[pl.ds(b * l_s, 1), :] = st_ref[b, pl.ds(0, 1), :]
            s2buf[pl.ds(b * l_s + 1, 1), :] = st_ref[b, pl.ds(1, 1), :]
        rmod = lax.broadcasted_iota(jnp.int32, (ts, 1), 0) % l_s
        prev1 = jnp.where(rmod == 0, s1buf[...], sbuf[pl.ds(SUBLANES - 1, ts), :])
        prev2 = jnp.where(rmod < 2, s2buf[...], sbuf[pl.ds(SUBLANES - 2, ts), :])
        v = prev2 * cw[0:1, :] + prev1 * cw[1:2, :] + u * cw[2:3, :]
        os_ref[...] = (_dot_nt(x, wbf[0]) * v).astype(BF16)
        for b in range(nb_s):
            cas_ref[b] = sbuf[pl.ds(SUBLANES + (b + 1) * l_s - 2, 2), :]

    @pl.when(i % tiles_per_batch == 0)
    def _():
        ubuf[pl.ds(0, SUBLANES), :] = jnp.zeros((SUBLANES, ubuf.shape[1]), F32)

    x = xp_ref[...]
    u = _dot_nt(x, wbf[1]) * _dot_nt(x, wbf[2])
    ubuf[pl.ds(SUBLANES, tm), :] = u
    v = (ubuf[pl.ds(SUBLANES - 2, tm), :] * cw[0:1, :]
         + ubuf[pl.ds(SUBLANES - 1, tm), :] * cw[1:2, :] + u * cw[2:3, :])
    op_ref[...] = (_dot_nt(x, wbf[0]) * v).astype(BF16)
    ubuf[pl.ds(0, SUBLANES), :] = ubuf[pl.ds(tm, SUBLANES), :]

    @pl.when(i % tiles_per_batch == tiles_per_batch - 1)
    def _():
        cap_ref[0] = ubuf[pl.ds(SUBLANES + tm - 2, 2), :]


def _proj_a(xnp, xns, w_in, conv_w, state_s, n_batch_p, seq_p, d_conv):
    tp, d = xnp.shape
    ts = xns.shape[0]
    nb_s = state_s.shape[0]
    l_s = ts // nb_s
    tn = 512
    tm = _pick(seq_p, (1024, 512, 256, 128))
    tpb = seq_p // tm
    ncol = d_conv // tn
    kern = functools.partial(_proj_a_kernel, tiles_per_batch=tpb, nb_s=nb_s, l_s=l_s)
    return pl.pallas_call(
        kern,
        grid=(ncol, tp // tm),
        in_specs=[
            pl.BlockSpec((tm, d), lambda j, i: (i, 0)),
            pl.BlockSpec((ts, d), lambda j, i: (0, 0)),
            pl.BlockSpec((tn, d), lambda j, i: (j, 0)),
            pl.BlockSpec((tn, d), lambda j, i: (j + ncol, 0)),
            pl.BlockSpec((tn, d), lambda j, i: (j + 2 * ncol, 0)),
            pl.BlockSpec((CONV_A_K, tn), lambda j, i: (0, j)),
            pl.BlockSpec((nb_s, CONV_A_K - 1, tn), lambda j, i: (0, 0, j)),
        ],
        out_specs=[
            pl.BlockSpec((tm, tn), lambda j, i: (i, j)),
            pl.BlockSpec((ts, tn), lambda j, i: (0, j)),
            pl.BlockSpec((1, CONV_A_K - 1, tn), lambda j, i: (i // tpb, 0, j)),
            pl.BlockSpec((nb_s, CONV_A_K - 1, tn), lambda j, i: (0, 0, j)),
        ],
        out_shape=[
            jax.ShapeDtypeStruct((tp, d_conv), BF16),
            jax.ShapeDtypeStruct((ts, d_conv), BF16),
            jax.ShapeDtypeStruct((n_batch_p, CONV_A_K - 1, d_conv), F32),
            jax.ShapeDtypeStruct((nb_s, CONV_A_K - 1, d_conv), F32),
        ],
        scratch_shapes=[
            pltpu.VMEM((3, tn, d), BF16),
            pltpu.VMEM((SUBLANES + tm, tn), F32),
            pltpu.VMEM((SUBLANES + ts, tn), F32),
            pltpu.VMEM((ts, tn), F32),
            pltpu.VMEM((ts, tn), F32),
        ],
        compiler_params=_params(("arbitrary", "arbitrary")),
        name="proj_a",
    )(xnp, xns, w_in, w_in, w_in, conv_w, state_s)


def _proj_raw_kernel(xp_ref, xs_ref, w_hbm, op_ref, os_ref, wbf, wstage, wsem, *, nsplit, width, row0):
    j = pl.program_id(0)
    tn = wbf.shape[0]

    def w_copy(jj, slot):
        rows = pl.ds(pl.multiple_of(row0 + jj * tn, SUBLANES), tn)
        return pltpu.make_async_copy(w_hbm.at[rows], wstage.at[slot], wsem.at[slot])

    def emit(x_ref, o_ref):
        acc = _dot_nt(x_ref[...], wbf[...])
        if nsplit == 0:
            o_ref[...] = acc
        else:
            for s in range(nsplit):
                o_ref[s] = acc[:, s * width:(s + 1) * width]

    @pl.when(pl.program_id(1) == 0)
    def _():
        @pl.when(j == 0)
        def _():
            w_copy(0, 0).start()

        w_copy(j, j % 2).wait()

        @pl.when(j + 1 < pl.num_programs(0))
        def _():
            w_copy(j + 1, (j + 1) % 2).start()

        wbf[...] = wstage[j % 2].astype(BF16)
        emit(xs_ref, os_ref)

    emit(xp_ref, op_ref)


def _proj_raw(xnp, xns, w, col0, ncols, width):
    tp, d = xnp.shape
    ts = xns.shape[0]
    tn = 1024
    tm = _pick(tp, (1024, 512, 256, 128))
    assert col0 % SUBLANES == 0 and ncols % tn == 0
    if width == 0:
        nsplit = 0
        out_specs = [pl.BlockSpec((tm, tn), lambda j, i: (i, j)),
                     pl.BlockSpec((ts, tn), lambda j, i: (0, j))]
        out_shape = [jax.ShapeDtypeStruct((tp, ncols), F32), jax.ShapeDtypeStruct((ts, ncols), F32)]
    else:
        nsplit = tn // width
        out_specs = [pl.BlockSpec((nsplit, tm, width), lambda j, i: (j, i, 0)),
                     pl.BlockSpec((nsplit, ts, width), lambda j, i: (j, 0, 0))]
        out_shape = [jax.ShapeDtypeStruct((ncols // width, tp, width), F32),
                     jax.ShapeDtypeStruct((ncols // width, ts, width), F32)]
    kern = functools.partial(_proj_raw_kernel, nsplit=nsplit, width=width, row0=col0)
    return pl.pallas_call(
        kern,
        grid=(ncols // tn, tp // tm),
        in_specs=[
            pl.BlockSpec((tm, d), lambda j, i: (i, 0)),
            pl.BlockSpec((ts, d), lambda j, i: (0, 0)),
            pl.BlockSpec(memory_space=pl.ANY),
        ],
        out_specs=out_specs,
        out_shape=out_shape,
        scratch_shapes=[pltpu.VMEM((tn, d), BF16), pltpu.VMEM((2, tn, d), F32), pltpu.SemaphoreType.DMA((2,))],
        compiler_params=_params(("arbitrary", "arbitrary")),
        name=f"proj_raw_{col0}",
    )(xnp, xns, w)


def _proj_conv_kernel(xp_ref, xs_ref, w_ref, cw_ref, cb_ref, st_ref, op_ref, os_ref, cmp_ref, cms_ref,
                      wbf, ubuf, sbuf, fix, *, tiles_per_batch, nb_s, l_s, nsplit, width):
    i = pl.program_id(1)
    tm = xp_ref.shape[0]
    ts = xs_ref.shape[0]
    tail = CONV_M_K - 1
    cw = cw_ref[...]
    bias = cb_ref[...]

    def store(o_ref, act):
        for s in range(nsplit):
            o_ref[s] = act[:, s * width:(s + 1) * width]

    @pl.when(i == 0)
    def _():
        wbf[...] = w_ref[...].astype(BF16)
        raw = _dot_nt(xs_ref[...], wbf[...])
        sbuf[pl.ds(0, SUBLANES), :] = jnp.zeros((SUBLANES, raw.shape[1]), F32)
        sbuf[pl.ds(SUBLANES, ts), :] = raw
        fix[...] = jnp.zeros_like(fix)
        for b in range(nb_s):
            for back in range(1, tail + 1):
                for m in range(back):
                    fix[back - 1, pl.ds(b * l_s + m, 1), :] = st_ref[b, pl.ds(tail + m - back, 1), :]
        rmod = lax.broadcasted_iota(jnp.int32, (ts, 1), 0) % l_s
        acc = raw * cw[tail:tail + 1, :]
        for back in range(1, tail + 1):
            tap = jnp.where(rmod < back, fix[back - 1], sbuf[pl.ds(SUBLANES - back, ts), :])
            acc = acc + tap * cw[tail - back:tail - back + 1, :]
        store(os_ref, jax.nn.silu(acc + bias))
        for b in range(nb_s):
            cms_ref[b] = sbuf[pl.ds(SUBLANES + (b + 1) * l_s - tail, tail), :]

    @pl.when(i % tiles_per_batch == 0)
    def _():
        ubuf[pl.ds(0, SUBLANES), :] = jnp.zeros((SUBLANES, ubuf.shape[1]), F32)

    raw = _dot_nt(xp_ref[...], wbf[...])
    ubuf[pl.ds(SUBLANES, tm), :] = raw
    acc = raw * cw[tail:tail + 1, :]
    for back in range(1, tail + 1):
        acc = acc + ubuf[pl.ds(SUBLANES - back, tm), :] * cw[tail - back:tail - back + 1, :]
    store(op_ref, jax.nn.silu(acc + bias))
    ubuf[pl.ds(0, SUBLANES), :] = ubuf[pl.ds(tm, SUBLANES), :]

    @pl.when(i % tiles_per_batch == tiles_per_batch - 1)
    def _():
        cmp_ref[0] = ubuf[pl.ds(SUBLANES + tm - tail, tail), :]


def _proj_conv(xnp, xns, w, conv_w, conv_b, state_s, col0, ch0, ncols, width, n_batch_p, seq_p):
    tp, d = xnp.shape
    ts = xns.shape[0]
    nb_s = state_s.shape[0]
    l_s = ts // nb_s
    tail = CONV_M_K - 1
    tn = 1024
    tm = _pick(seq_p, (1024, 512, 256, 128))
    tpb = seq_p // tm
    assert col0 % tn == 0 and ch0 % tn == 0 and ncols % tn == 0 and l_s >= tail
    jb, cb0 = col0 // tn, ch0 // tn
    nsplit = tn // width
    kern = functools.partial(_proj_conv_kernel, tiles_per_batch=tpb, nb_s=nb_s, l_s=l_s,
                             nsplit=nsplit, width=width)
    return pl.pallas_call(
        kern,
        grid=(ncols // tn, tp // tm),
        in_specs=[
            pl.BlockSpec((tm, d), lambda j, i: (i, 0)),
            pl.BlockSpec((ts, d), lambda j, i: (0, 0)),
            pl.BlockSpec((tn, d), lambda j, i: (j + jb, 0)),
            pl.BlockSpec((CONV_M_K, tn), lambda j, i: (0, j + cb0)),
            pl.BlockSpec((1, tn), lambda j, i: (0, j + cb0)),
            pl.BlockSpec((nb_s, tail, tn), lambda j, i: (0, 0, j + cb0)),
        ],
        out_specs=[
            pl.BlockSpec((nsplit, tm, width), lambda j, i: (j, i, 0)),
            pl.BlockSpec((nsplit, ts, width), lambda j, i: (j, 0, 0)),
            pl.BlockSpec((1, tail, tn), lambda j, i: (i // tpb, 0, j)),
            pl.BlockSpec((nb_s, tail, tn), lambda j, i: (0, 0, j)),
        ],
        out_shape=[
            jax.ShapeDtypeStruct((ncols // width, tp, width), F32),
            jax.ShapeDtypeStruct((ncols // width, ts, width), F32),
            jax.ShapeDtypeStruct((n_batch_p, tail, ncols), F32),
            jax.ShapeDtypeStruct((nb_s, tail, ncols), F32),
        ],
        scratch_shapes=[
            pltpu.VMEM((tn, d), BF16),
            pltpu.VMEM((SUBLANES + tm, tn), F32),
            pltpu.VMEM((SUBLANES + ts, tn), F32),
            pltpu.VMEM((tail, ts, tn), F32),
        ],
        compiler_params=_params(("arbitrary", "arbitrary")),
        name=f"proj_conv_{col0}",
    )(xnp, xns, w, conv_w, conv_b, state_s)


def _ssd_kernel(*refs, q, has_state, nchunks):
    (z_ref, xs_ref, b_ref, c_ref, dt_ref, dtb_ref, alog_ref, dsk_ref, gn_ref, *rest) = refs
    if has_state:
        hprev, *rest = rest
    (y_ref, oh, h_s, acg, rowt) = rest
    c = pl.program_id(1)
    nheads = N_GROUPS * HEADS_PER_GROUP

    @pl.when(c == 0)
    def _init():
        if has_state:
            h_s[...] = hprev[...]
        else:
            h_s[...] = jnp.zeros_like(h_s)

    dt = _softplus(dt_ref[...] + dtb_ref[...])
    da = dt * (-jnp.exp(alog_ref[...]))
    ri = lax.broadcasted_iota(jnp.int32, (q, q), 0)
    ci = lax.broadcasted_iota(jnp.int32, (q, q), 1)
    causal = ri >= ci
    tril = jnp.where(causal, 1.0, 0.0).astype(BF16)
    acum = sum(_dot(tril, p) for p in _split3(da))
    eye = jnp.where(lax.broadcasted_iota(jnp.int32, (nheads, nheads), 0)
                    == lax.broadcasted_iota(jnp.int32, (nheads, nheads), 1), 1.0, 0.0).astype(BF16)
    acum2 = acum * LOG2_E
    rowt[0] = sum(lax.dot_general(eye, p, NT_DIMS, preferred_element_type=F32) for p in _split3(acum2))
    rowt[1] = sum(lax.dot_general(eye, p, NT_DIMS, preferred_element_type=F32) for p in _split3(dt))
    wend = jnp.exp(acum[q - 1:q, :] - acum) * dt
    rowt[2] = sum(lax.dot_general(eye, p, NT_DIMS, preferred_element_type=F32) for p in _split3(wend))
    for g in range(N_GROUPS):
        acg[g] = acum2[:, g * HEADS_PER_GROUP:(g + 1) * HEADS_PER_GROUP]

    lane = lax.broadcasted_iota(jnp.int32, (1, LANES), 1)
    rowi = lax.broadcasted_iota(jnp.int32, (LANES, 1), 0)
    half_w = LANES // 2

    def group_body(g, carry):
        xs = xs_ref[g]
        bb = b_ref[g].astype(BF16)
        ccb16 = c_ref[g].astype(BF16)
        cb_ = lax.dot_general(ccb16, bb, NT_DIMS, preferred_element_type=F32)
        ac8 = acg[g]
        dsk = dsk_ref[g]
        lo_half = lane < half_w
        lo_rows = rowi < half_w
        ys = []
        for pair in range(HEADS_PER_GROUP // 2):
            sl = slice(pair * LANES, (pair + 1) * LANES)
            xp = xs[:, sl]
            hp = h_s[g, pl.ds(pair * LANES, LANES), :]
            ms_, acols, wrows = [], [], []
            for r in (2 * pair, 2 * pair + 1):
                head = g * HEADS_PER_GROUP + r
                acol = jnp.broadcast_to(ac8[:, r:r + 1], (q, LANES))
                arow = rowt[0, pl.ds(head, 1), :]
                drow = rowt[1, pl.ds(head, 1), :]
                decay = jnp.exp2(jnp.where(causal, acol[:, :q] - arow, -jnp.inf))
                ms_.append((cb_ * decay * drow).astype(BF16))
                acols.append(acol)
                wrows.append(jnp.broadcast_to(rowt[2, pl.ds(head, 1), :], (half_w, q)))
            xlo = jnp.where(lo_half, xp, 0.0).astype(BF16)
            xhi = jnp.where(lo_half, 0.0, xp).astype(BF16)
            if q % LANES == 0:
                ydiag = _dot(jnp.concatenate(ms_, axis=1), jnp.concatenate([xlo, xhi], axis=0))
            else:
                ydiag = _dot(ms_[0], xlo) + _dot(ms_[1], xhi)
            ea = jnp.where(lo_half, jnp.exp2(acols[0]), jnp.exp2(acols[1]))
            yoff = ea * lax.dot_general(ccb16, hp.astype(BF16), NT_DIMS, preferred_element_type=F32)
            ys.append(dsk[:, sl] * xp + ydiag + yoff)
            xwt = (xp.T * jnp.concatenate(wrows, axis=0)).astype(BF16)
            dlast = [jnp.broadcast_to(jnp.exp2(a[q - 1:q, :]), (LANES, LANES)) for a in acols]
            h_s[g, pl.ds(pair * LANES, LANES), :] = hp * jnp.where(lo_rows, dlast[0], dlast[1]) + _dot(xwt, bb)
        yg = jnp.concatenate(ys, axis=1)
        hh = yg * jax.nn.silu(z_ref[g])
        ms = jnp.mean(hh * hh, axis=-1, keepdims=True)
        y_ref[g] = (hh * lax.rsqrt(ms + EPS) * gn_ref[g]).astype(BF16)
        return carry

    lax.fori_loop(0, N_GROUPS, group_body, 0, unroll=2)

    @pl.when(c == nchunks - 1)
    def _():
        oh[...] = h_s[...]


def _ssd(z, xc, bc, dt_raw, prm, n_batch, seq, q, h_prev=None):
    nchunks = seq // q
    nheads = N_GROUPS * HEADS_PER_GROUP
    has_state = h_prev is not None
    g8 = N_GROUPS

    def tok(first):
        return lambda b, c: (first, b * nchunks + c, 0)

    def const3(b, c):
        return (0, 0, 0)

    in_specs = [
        pl.BlockSpec((g8, q, GROUP_W), tok(0)),
        pl.BlockSpec((g8, q, GROUP_W), tok(0)),
        pl.BlockSpec((g8, q, D_STATE), tok(0)),
        pl.BlockSpec((g8, q, D_STATE), tok(1)),
        pl.BlockSpec((q, nheads), lambda b, c: (b * nchunks + c, 0)),
        pl.BlockSpec((1, nheads), lambda b, c: (0, 0)),
        pl.BlockSpec((1, nheads), lambda b, c: (0, 0)),
        pl.BlockSpec((g8, 1, GROUP_W), const3),
        pl.BlockSpec((g8, 1, GROUP_W), const3),
    ]
    args = [z, xc, bc, bc, dt_raw, prm["dtb"], prm["alog"], prm["dsk"], prm["gn"]]

    def per_batch(shape):
        return pl.BlockSpec((None,) + shape, lambda b, c: (b,) + (0,) * len(shape))

    if has_state:
        in_specs.append(per_batch((g8, GROUP_W, D_STATE)))
        args.append(h_prev)
    t = n_batch * seq
    out_specs = [
        pl.BlockSpec((g8, q, GROUP_W), lambda b, c: (0, b * nchunks + c, 0)),
        per_batch((g8, GROUP_W, D_STATE)),
    ]
    out_shape = [
        jax.ShapeDtypeStruct((g8, t, GROUP_W), BF16),
        jax.ShapeDtypeStruct((n_batch, g8, GROUP_W, D_STATE), F32),
    ]
    scratch = [
        pltpu.VMEM((g8, GROUP_W, D_STATE), F32),
        pltpu.VMEM((g8, q, HEADS_PER_GROUP), F32),
        pltpu.VMEM((3, nheads, q), F32),
    ]
    kern = functools.partial(_ssd_kernel, q=q, has_state=has_state, nchunks=nchunks)
    return pl.pallas_call(
        kern,
        grid=(n_batch, nchunks),
        in_specs=in_specs,
        out_specs=out_specs,
        out_shape=out_shape,
        scratch_shapes=scratch,
        compiler_params=_params(("arbitrary", "arbitrary")),
        name="ssd_state" if has_state else "ssd",
    )(*args)


def _merge_kernel(ap_ref, as_ref, yp_ref, ys_ref, gap_ref, gmp_ref, gas_ref, gms_ref, wa_ref, wm_ref,
                  op_ref, os_ref, wab, wmb):
    def emit(a_ref, y_ref, ga_ref, gm_ref, o_ref):
        oa = _dot(a_ref[...], wab[...])
        y = jnp.concatenate([y_ref[g] for g in range(N_GROUPS)], axis=1)
        om = _dot(y, wmb[...])
        o_ref[...] = (jax.nn.sigmoid(ga_ref[...]) * oa + jax.nn.sigmoid(gm_ref[...]) * om).astype(BF16)

    @pl.when(pl.program_id(1) == 0)
    def _():
        wab[...] = wa_ref[...].astype(BF16)
        wmb[...] = wm_ref[...].astype(BF16)
        emit(as_ref, ys_ref, gas_ref, gms_ref, os_ref)

    emit(ap_ref, yp_ref, gap_ref, gmp_ref, op_ref)


def _merge(abv_p, abv_s, y_p, y_s, g_p, g_s, w_a_out, w_m_out):
    tp, dc = abv_p.shape
    ts = abv_s.shape[0]
    dm = w_a_out.shape[1]
    tn = 512
    tm = _pick(tp, (512, 256, 128))
    ncol = dm // tn
    return pl.pallas_call(
        _merge_kernel,
        grid=(ncol, tp // tm),
        in_specs=[
            pl.BlockSpec((tm, dc), lambda j, i: (i, 0)),
            pl.BlockSpec((ts, dc), lambda j, i: (0, 0)),
            pl.BlockSpec((N_GROUPS, tm, GROUP_W), lambda j, i: (0, i, 0)),
            pl.BlockSpec((N_GROUPS, ts, GROUP_W), lambda j, i: (0, 0, 0)),
            pl.BlockSpec((tm, tn), lambda j, i: (i, j)),
            pl.BlockSpec((tm, tn), lambda j, i: (i, j + ncol)),
            pl.BlockSpec((ts, tn), lambda j, i: (0, j)),
            pl.BlockSpec((ts, tn), lambda j, i: (0, j + ncol)),
            pl.BlockSpec((dc, tn), lambda j, i: (0, j)),
            pl.BlockSpec((N_GROUPS * GROUP_W, tn), lambda j, i: (0, j)),
        ],
        out_specs=[
            pl.BlockSpec((tm, tn), lambda j, i: (i, j)),
            pl.BlockSpec((ts, tn), lambda j, i: (0, j)),
        ],
        out_shape=[jax.ShapeDtypeStruct((tp, dm), BF16), jax.ShapeDtypeStruct((ts, dm), BF16)],
        scratch_shapes=[pltpu.VMEM((dc, tn), BF16), pltpu.VMEM((N_GROUPS * GROUP_W, tn), BF16)],
        compiler_params=_params(("arbitrary", "arbitrary")),
        name="merge",
    )(abv_p, abv_s, y_p, y_s, g_p, g_p, g_s, g_s, w_a_out, w_m_out)


def _res_kernel(ap_ref, as_ref, rp_ref, rs_ref, w_ref, g_ref, *outs, n_prompt_tiles, merged_out, ts):
    if merged_out:
        h_ref, hn_ref, wbf = outs
    else:
        hp_ref, hs_ref, hnp_ref, hns_ref, wbf = outs
    i = pl.program_id(0)

    @pl.when(i == 0)
    def _():
        wbf[...] = w_ref[...].astype(BF16)

    def emit(a_ref, r_ref, store_h, store_hn):
        h = r_ref[...] + _dot(a_ref[...], wbf[...])
        store_h(h)
        store_hn(_rms(h, g_ref[...]))

    if merged_out:
        @pl.when(i < n_prompt_tiles)
        def _():
            def sh(h):
                h_ref[...] = h

            def shn(hn):
                hn_ref[...] = hn

            emit(ap_ref, rp_ref, sh, shn)

        @pl.when(i == n_prompt_tiles)
        def _():
            def sh(h):
                h_ref[pl.ds(0, ts), :] = h

            def shn(hn):
                hn_ref[pl.ds(0, ts), :] = hn

            emit(as_ref, rs_ref, sh, shn)
    else:
        def shp(h):
            hp_ref[...] = h

        def shnp(hn):
            hnp_ref[...] = hn.astype(hnp_ref.dtype)

        emit(ap_ref, rp_ref, shp, shnp)

        @pl.when(i == 0)
        def _():
            def shs(h):
                hs_ref[...] = h

            def shns(hn):
                hns_ref[...] = hn.astype(hns_ref.dtype)

            emit(as_ref, rs_ref, shs, shns)


def _res(a_p, a_s, r_p, r_s, w, g, merged_out, hn_dtype):
    tp, d = a_p.shape
    ts = a_s.shape[0]
    tm = _pick(tp, (512, 256))
    assert tp % tm == 0 and ts <= tm
    npt = tp // tm
    last = npt - 1
    in_specs = [
        pl.BlockSpec((tm, d), lambda i: (jnp.minimum(i, last), 0)),
        pl.BlockSpec((ts, d), lambda i: (0, 0)),
        pl.BlockSpec((tm, d), lambda i: (jnp.minimum(i, last), 0)),
        pl.BlockSpec((ts, d), lambda i: (0, 0)),
        pl.BlockSpec((d, d), lambda i: (0, 0), pipeline_mode=pl.Buffered(1)),
        pl.BlockSpec((1, d), lambda i: (0, 0)),
    ]
    if merged_out:
        grid = (npt + 1,)
        out_specs = [pl.BlockSpec((tm, d), lambda i: (i, 0)), pl.BlockSpec((tm, d), lambda i: (i, 0))]
        out_shape = [jax.ShapeDtypeStruct((tp + ts, d), F32), jax.ShapeDtypeStruct((tp + ts, d), hn_dtype)]
    else:
        grid = (npt,)
        out_specs = [pl.BlockSpec((tm, d), lambda i: (i, 0)), pl.BlockSpec((ts, d), lambda i: (0, 0)),
                     pl.BlockSpec((tm, d), lambda i: (i, 0)), pl.BlockSpec((ts, d), lambda i: (0, 0))]
        out_shape = [jax.ShapeDtypeStruct((tp, d), F32), jax.ShapeDtypeStruct((ts, d), F32),
                     jax.ShapeDtypeStruct((tp, d), hn_dtype), jax.ShapeDtypeStruct((ts, d), hn_dtype)]
    kern = functools.partial(_res_kernel, n_prompt_tiles=npt, merged_out=merged_out, ts=ts)
    return pl.pallas_call(
        kern,
        grid=grid,
        in_specs=in_specs,
        out_specs=out_specs,
        out_shape=out_shape,
        scratch_shapes=[pltpu.VMEM((d, d), BF16)],
        compiler_params=_params(("arbitrary",)),
        name="res_merged" if merged_out else "res",
    )(a_p, a_s, r_p, r_s, w, g)


def _kv_kernel(m_ref, g_ref, wk_ref, wv_ref, k_hbm, v_hbm, mn, kvb, sem, *, n_batch):
    j = pl.program_id(0)

    @pl.when(j == 0)
    def _():
        mn[...] = _rms(m_ref[...], g_ref[...]).astype(BF16)

    kvb[0] = _dot(mn[...], wk_ref[...].astype(BF16))
    kvb[1] = _dot(mn[...], wv_ref[...].astype(BF16))
    nm = kvb.shape[1] // n_batch
    for h in range(N_XHEADS):
        @pl.when(j == h)
        def _(h=h):
            copies = [pltpu.make_async_copy(kvb.at[t, pl.ds(b * nm, nm)], dst.at[b, :, h, :], sem.at[t, b])
                      for t, dst in enumerate((k_hbm, v_hbm)) for b in range(n_batch)]
            for c in copies:
                c.start()
            for c in copies:
                c.wait()


def _memory_kv(mem2d, g, w_k, w_v, n_batch):
    m, d = mem2d.shape
    dh = d // N_XHEADS
    out_sds = jax.ShapeDtypeStruct((n_batch, m // n_batch, N_XHEADS, dh), F32)
    kern = functools.partial(_kv_kernel, n_batch=n_batch)
    return pl.pallas_call(
        kern,
        grid=(N_XHEADS,),
        in_specs=[
            pl.BlockSpec((m, d), lambda j: (0, 0)),
            pl.BlockSpec((1, d), lambda j: (0, 0)),
            pl.BlockSpec((d, dh), lambda j: (0, j)),
            pl.BlockSpec((d, dh), lambda j: (0, j)),
        ],
        out_specs=[pl.BlockSpec(memory_space=pl.ANY), pl.BlockSpec(memory_space=pl.ANY)],
        out_shape=[out_sds, out_sds],
        scratch_shapes=[pltpu.VMEM((m, d), BF16), pltpu.VMEM((2, m, dh), F32),
                        pltpu.SemaphoreType.DMA((2, n_batch))],
        compiler_params=_params(("arbitrary",)),
        name="memory_kv",
    )(mem2d, g, w_k, w_v)


def _attn_kernel(hn_ref, k_ref, v_ref, wq_ref, o_ref, wqb, kb, vb, kvf, sem, *q_all):
    b = pl.program_id(0)
    i = pl.program_id(1)
    tm = o_ref.shape[0]

    @pl.when((b == 0) & (i == 0))
    def _():
        wqb[...] = wq_ref[...].astype(BF16)
        if q_all:
            q_all[0][...] = _dot(hn_ref[...], wqb[...])

    d = wqb.shape[1]
    dh = d // N_XHEADS

    @pl.when(i == 0)
    def _():
        def copies(bb):
            slot = bb % 2
            return [pltpu.make_async_copy(src.at[bb, :, h, :], kvf.at[slot, t, h], sem.at[slot, t, h])
                    for t, src in enumerate((k_ref, v_ref)) for h in range(N_XHEADS)]

        @pl.when(b == 0)
        def _():
            for c in copies(b):
                c.start()

        for c in copies(b):
            c.wait()

        @pl.when(b + 1 < pl.num_programs(0))
        def _():
            for c in copies(b + 1):
                c.start()

        for h in range(N_XHEADS):
            kb[:, h * dh:(h + 1) * dh] = kvf[b % 2, 0, h].astype(BF16)
            vb[:, h * dh:(h + 1) * dh] = kvf[b % 2, 1, h].astype(BF16)

    if q_all:
        row0 = pl.multiple_of((b * pl.num_programs(1) + i) * tm, SUBLANES)
        q = q_all[0][pl.ds(row0, tm), :]
    else:
        q = _dot(hn_ref[...], wqb[...])
    outs = []
    for h in range(N_XHEADS):
        sl = slice(h * dh, (h + 1) * dh)
        s = lax.dot_general(q[:, sl].astype(BF16), kb[:, sl], NT_DIMS, preferred_element_type=F32)
        s = s * (dh ** -0.5)
        e = jnp.exp(s - jnp.max(s, axis=-1, keepdims=True))
        p = e / jnp.sum(e, axis=-1, keepdims=True)
        outs.append(_dot(p.astype(BF16), vb[:, sl]))
    o_ref[...] = jnp.concatenate(outs, axis=1).astype(BF16)


def _attn(hn, k, v, w_q, n_batch, seq):
    t, d = hn.shape
    nm, nh, dh = k.shape[1:]
    tm = _pick(seq, (512, 256, 128, 64, 32, 16))
    tpb = seq // tm
    kv_spec = pl.BlockSpec(memory_space=pl.ANY)
    scratch = [pltpu.VMEM((d, d), BF16), pltpu.VMEM((nm, d), BF16), pltpu.VMEM((nm, d), BF16),
               pltpu.VMEM((2, 2, nh, nm, dh), F32), pltpu.SemaphoreType.DMA((2, 2, nh))]
    if tm < 128 and t <= 512:
        hn_spec = pl.BlockSpec((t, d), lambda b, i: (0, 0))
        scratch.append(pltpu.VMEM((t, d), F32))
    else:
        hn_spec = pl.BlockSpec((tm, d), lambda b, i: (b * tpb + i, 0))
    return pl.pallas_call(
        _attn_kernel,
        grid=(n_batch, tpb),
        in_specs=[
            hn_spec,
            kv_spec,
            kv_spec,
            pl.BlockSpec((d, d), lambda b, i: (0, 0), pipeline_mode=pl.Buffered(1)),
        ],
        out_specs=pl.BlockSpec((tm, d), lambda b, i: (b * tpb + i, 0)),
        out_shape=jax.ShapeDtypeStruct((t, d), BF16),
        scratch_shapes=scratch,
        compiler_params=_params(("arbitrary", "arbitrary")),
        name=f"attn_{seq}",
    )(hn, k, v, w_q)


def _router_kernel(x_ref, w_ref, b_ref, ri_ref, rw_ref):
    logits = _dot(x_ref[...].astype(BF16), w_ref[...].astype(BF16)) + b_ref[...]
    lane_i = lax.broadcasted_iota(jnp.int32, logits.shape, 1)
    lane = lane_i.astype(F32)
    ninf = -jnp.inf
    big = float(LANES)
    is_g = lane < N_EXPERT_GROUPS
    gl = jnp.where(is_g, logits, ninf)
    gmax = jnp.max(gl, axis=-1, keepdims=True)
    gsel = jnp.min(jnp.where(gl == gmax, lane, big), axis=-1, keepdims=True)
    pg = 1.0 / jnp.sum(jnp.where(is_g, jnp.exp(gl - gmax), 0.0), axis=-1, keepdims=True)
    lo = N_EXPERT_GROUPS + EXPERTS_PER_GROUP * gsel
    el = jnp.where(lane >= lo, jnp.where(lane < lo + EXPERTS_PER_GROUP, logits, ninf), ninf)
    m1 = jnp.max(el, axis=-1, keepdims=True)
    i1 = jnp.min(jnp.where(el == m1, lane, big), axis=-1, keepdims=True)
    el2 = jnp.where(lane == i1, ninf, el)
    m2 = jnp.max(el2, axis=-1, keepdims=True)
    i2 = jnp.min(jnp.where(el2 == m2, lane, big), axis=-1, keepdims=True)
    e = jnp.exp(m2 - m1)
    w1 = pg / (1.0 + e)
    w2 = pg * e / (1.0 + e)
    ri_ref[...] = jnp.where(lane_i == 0, i1 - N_EXPERT_GROUPS,
                            jnp.where(lane_i == 1, i2 - N_EXPERT_GROUPS, 0.0)).astype(jnp.int32)
    rw_ref[...] = jnp.where(lane_i == 0, w1, jnp.where(lane_i == 1, w2, 0.0))


def _router(hn_all, w_r, b_r):
    t, d = hn_all.shape
    tm = _pick(t, (640, 512, 384, 256, 128))
    return pl.pallas_call(
        _router_kernel,
        grid=(t // tm,),
        in_specs=[
            pl.BlockSpec((tm, d), lambda i: (i, 0)),
            pl.BlockSpec((d, LANES), lambda i: (0, 0)),
            pl.BlockSpec((1, LANES), lambda i: (0, 0)),
        ],
        out_specs=[pl.BlockSpec((tm, LANES), lambda i: (i, 0)), pl.BlockSpec((tm, LANES), lambda i: (i, 0))],
        out_shape=[jax.ShapeDtypeStruct((t, LANES), jnp.int32), jax.ShapeDtypeStruct((t, LANES), F32)],
        compiler_params=_params(("arbitrary",)),
        name="router",
    )(hn_all, w_r, b_r)


def _row_copy(src, dst, s, d, sem):
    return pltpu.make_async_copy(src.at[pl.ds(s, 1)], dst.at[pl.ds(d, 1)], sem)


def _dispatch_kernel(dest_ref, pad0_ref, padn_ref, nsub_ref, x_ref, o_ref, zrow, zblk, sem, zsem, csem, bsem, *,
                     chunk, n_blocks):
    base = pl.program_id(0) * chunk

    @pl.when(pl.program_id(0) == 0)
    def _():
        zrow[...] = jnp.zeros_like(zrow)
        zblk[...] = jnp.zeros_like(zblk)

        def pad_plan(e):
            p0 = pad0_ref[e]
            head = jnp.minimum(padn_ref[e], (SUBLANES - p0 % SUBLANES) % SUBLANES)
            return p0, head, (padn_ref[e] - head) // SUBLANES

        def group_copy(row):
            return pltpu.make_async_copy(zrow, o_ref.at[pl.ds(pl.multiple_of(row, SUBLANES), SUBLANES)], csem)

        def pad_start(e, carry):
            p0, head, ngroups = pad_plan(e)

            def start_row(r, c):
                _row_copy(zrow, o_ref, 0, p0 + r, zsem).start()
                return c

            def start_group(j, c):
                group_copy(p0 + head + j * SUBLANES).start()
                return c

            lax.fori_loop(0, head, start_row, 0)
            return lax.fori_loop(0, ngroups, start_group, carry)

        def pad_wait(e, carry):
            _, head, ngroups = pad_plan(e)

            def wait_row(r, c):
                _row_copy(zrow, o_ref, 0, 0, zsem).wait()
                return c

            def wait_group(j, c):
                group_copy(0).wait()
                return c

            lax.fori_loop(0, head, wait_row, 0)
            return lax.fori_loop(0, ngroups, wait_group, carry)

        lax.fori_loop(0, N_EXPERTS, pad_start, 0)

        def blk_copy(b):
            return pltpu.make_async_copy(zblk, o_ref.at[pl.ds(pl.multiple_of(b * EXPERT_SUB, EXPERT_SUB),
                                                             EXPERT_SUB)], bsem)

        def tail_start(b, c):
            blk_copy(b).start()
            return c

        def tail_wait(b, c):
            blk_copy(b).wait()
            return c

        lax.fori_loop(nsub_ref[0], n_blocks, tail_start, 0)
        lax.fori_loop(0, N_EXPERTS, pad_wait, 0)
        lax.fori_loop(nsub_ref[0], n_blocks, tail_wait, 0)

    def issue(r, carry):
        t = base + r
        _row_copy(x_ref, o_ref, r, dest_ref[2 * t], sem).start()
        _row_copy(x_ref, o_ref, r, dest_ref[2 * t + 1], sem).start()
        return carry

    lax.fori_loop(0, chunk, issue, 0, unroll=8)

    for _ in range(2):
        pltpu.make_async_copy(x_ref, o_ref.at[pl.ds(0, chunk)], sem).wait()


def _dispatch(tables, hn_all, n_blocks):
    dest, pad0, padn, nsub = tables
    t, d = hn_all.shape
    chunk = _pick(t, (640, 512, 384, 256, 128))
    kern = functools.partial(_dispatch_kernel, chunk=chunk, n_blocks=n_blocks)
    return pl.pallas_call(
        kern,
        grid_spec=pltpu.PrefetchScalarGridSpec(
            num_scalar_prefetch=4,
            grid=(t // chunk,),
            in_specs=[pl.BlockSpec((chunk, d), lambda i, *_: (i, 0))],
            out_specs=pl.BlockSpec(memory_space=pl.ANY),
            scratch_shapes=[pltpu.VMEM((SUBLANES, d), F32), pltpu.VMEM((EXPERT_SUB, d), F32),
                            pltpu.SemaphoreType.DMA(()), pltpu.SemaphoreType.DMA(()),
                            pltpu.SemaphoreType.DMA(()), pltpu.SemaphoreType.DMA(())],
        ),
        out_shape=jax.ShapeDtypeStruct((n_blocks * EXPERT_SUB, d), F32),
        compiler_params=_params(("arbitrary",)),
        name="dispatch",
    )(dest, pad0, padn, nsub, hn_all)


def _expert_kernel(bstart_ref, wg_ref, wu_ref, wd_ref, x_hbm, y_hbm, gcache, ucache, dcache, xbuf, ybuf, xb,
                   xsem, ysem, *, n_blocks):
    e = pl.program_id(0)
    k = pl.program_id(1)
    nk = gcache.shape[1]
    total = bstart_ref[N_EXPERTS]

    def x_copy(b, slot):
        rows = pl.ds(pl.multiple_of(b * EXPERT_SUB, EXPERT_SUB), EXPERT_SUB)
        return pltpu.make_async_copy(x_hbm.at[rows], xbuf.at[slot], xsem.at[slot])

    def y_copy(b, slot):
        rows = pl.ds(pl.multiple_of(b * EXPERT_SUB, EXPERT_SUB), EXPERT_SUB)
        return pltpu.make_async_copy(ybuf.at[slot], y_hbm.at[rows], ysem.at[slot])

    @pl.when(e < N_EXPERTS)
    def _():
        slot = e % 2
        gcache[slot, k] = wg_ref[...].astype(BF16)
        ucache[slot, k] = wu_ref[...].astype(BF16)
        dcache[slot, k] = wd_ref[...].astype(BF16)

    @pl.when((e == 0) & (k == 0) & (total > 0))
    def _():
        x_copy(0, 0).start()

    @pl.when(e >= 1)
    def _():
        owner = e - 1
        wslot = owner % 2
        b0 = bstart_ref[owner]
        n = bstart_ref[owner + 1] - b0

        cpu = 2
        upb = nk // cpu

        def unit(u, carry):
            b = b0 + u // upb
            part = u % upb
            slot = b % 2

            def swiglu(x, p):
                kks = [p * cpu + c for c in range(cpu)]
                wg = jnp.concatenate([gcache[wslot, kk] for kk in kks], axis=1)
                wu = jnp.concatenate([ucache[wslot, kk] for kk in kks], axis=1)
                wd = jnp.concatenate([dcache[wslot, kk] for kk in kks], axis=0)
                hid = (jax.nn.silu(_dot(x, wg)) * _dot(x, wu)).astype(BF16)
                return _dot(hid, wd)

            @pl.when(part == 0)
            def _():
                x_copy(b, slot).wait()

                @pl.when(b + 1 < total)
                def _():
                    x_copy(b + 1, 1 - slot).start()

                x = xbuf[slot].astype(BF16)
                xb[...] = x

                @pl.when(b >= 2)
                def _():
                    y_copy(b - 2, slot).wait()

                ybuf[slot] = swiglu(x, 0)

            for p in range(1, upb):
                @pl.when(part == p)
                def _(p=p):
                    ybuf[slot] += swiglu(xb[...], p)
                    if p == upb - 1:
                        y_copy(b, slot).start()

            return carry

        lax.fori_loop((n * upb * k) // nk, (n * upb * (k + 1)) // nk, unit, 0)

    @pl.when((e == N_EXPERTS) & (k == nk - 1))
    def _():
        @pl.when(total >= 2)
        def _():
            y_copy(total - 2, total % 2).wait()

        @pl.when(total >= 1)
        def _():
            y_copy(total - 1, (total - 1) % 2).wait()

        ybuf[0] = jnp.zeros(ybuf.shape[1:], F32)

        def zfill_start(b, carry):
            y_copy(b, 0).start()
            return carry

        def zfill_wait(b, carry):
            y_copy(b, 0).wait()
            return carry

        lax.fori_loop(total, n_blocks, zfill_start, 0)
        lax.fori_loop(total, n_blocks, zfill_wait, 0)


def _experts(bstart, x_sorted, w_gate, w_up, w_down, kchunk):
    nrows, d = x_sorted.shape
    n_blocks = nrows // EXPERT_SUB
    de = w_gate.shape[2]
    nk = de // kchunk
    last = N_EXPERTS - 1

    def widx(e, k):
        return jnp.minimum(e, last), jnp.where(e <= last, k, nk - 1)

    def in_idx(e, k, b):
        ee, kk = widx(e, k)
        return (ee, 0, kk)

    def down_idx(e, k, b):
        ee, kk = widx(e, k)
        return (ee, kk, 0)

    kern = functools.partial(_expert_kernel, n_blocks=n_blocks)
    return pl.pallas_call(
        kern,
        grid_spec=pltpu.PrefetchScalarGridSpec(
            num_scalar_prefetch=1,
            grid=(N_EXPERTS + 1, nk),
            in_specs=[
                pl.BlockSpec((None, d, kchunk), in_idx),
                pl.BlockSpec((None, d, kchunk), in_idx),
                pl.BlockSpec((None, kchunk, d), down_idx),
                pl.BlockSpec(memory_space=pl.ANY),
            ],
            out_specs=pl.BlockSpec(memory_space=pl.ANY),
            scratch_shapes=[
                pltpu.VMEM((2, nk, d, kchunk), BF16),
                pltpu.VMEM((2, nk, d, kchunk), BF16),
                pltpu.VMEM((2, nk, kchunk, d), BF16),
                pltpu.VMEM((2, EXPERT_SUB, d), F32),
                pltpu.VMEM((2, EXPERT_SUB, d), F32),
                pltpu.VMEM((EXPERT_SUB, d), BF16),
                pltpu.SemaphoreType.DMA((2,)),
                pltpu.SemaphoreType.DMA((2,)),
            ],
        ),
        out_shape=jax.ShapeDtypeStruct((nrows, d), F32),
        compiler_params=_params(("arbitrary", "arbitrary")),
        name="experts",
    )(bstart, w_gate, w_up, w_down, x_sorted)


def _final_kernel(dest_ref, h_ref, rw_ref, ys_ref, g_ref, yp_ref, yss_ref, ya, yb, sem, *,
                  tm, n_prompt_tiles, ts):
    i = pl.program_id(0)

    def gather(tile, n):
        slot = tile % 2

        def issue(r, carry):
            t = tile * tm + r
            _row_copy(ys_ref, ya.at[slot], dest_ref[2 * t], r, sem.at[slot]).start()
            _row_copy(ys_ref, yb.at[slot], dest_ref[2 * t + 1], r, sem.at[slot]).start()
            return carry

        lax.fori_loop(0, n, issue, 0, unroll=8)

    def emit(n, o_ref):
        slot = i % 2
        for buf in (ya, yb):
            pltpu.make_async_copy(ys_ref.at[pl.ds(0, n)], buf.at[slot, pl.ds(0, n)], sem.at[slot]).wait()
        w = rw_ref[pl.ds(0, n), :]
        h = (h_ref[pl.ds(0, n), :] + w[:, 0:1] * ya[slot, pl.ds(0, n), :]
             + w[:, 1:2] * yb[slot, pl.ds(0, n), :])
        o_ref[...] = _rms(h, g_ref[...])

    @pl.when(i == 0)
    def _():
        gather(i, tm)

    @pl.when(i + 1 < n_prompt_tiles)
    def _():
        gather(i + 1, tm)

    @pl.when(i + 1 == n_prompt_tiles)
    def _():
        gather(i + 1, ts)

    @pl.when(i < n_prompt_tiles)
    def _():
        emit(tm, yp_ref)

    @pl.when(i == n_prompt_tiles)
    def _():
        emit(ts, yss_ref)


def _final(dest, h_all, rw, y_sorted, g, tp, ts):
    t, d = h_all.shape
    tm = 256
    assert tp % tm == 0 and ts <= tm and t == tp + ts
    npt = tp // tm
    kern = functools.partial(_final_kernel, tm=tm, n_prompt_tiles=npt, ts=ts)
    return pl.pallas_call(
        kern,
        grid_spec=pltpu.PrefetchScalarGridSpec(
            num_scalar_prefetch=1,
            grid=(npt + 1,),
            in_specs=[
                pl.BlockSpec((tm, d), lambda i, dref: (i, 0)),
                pl.BlockSpec((tm, LANES), lambda i, dref: (i, 0)),
                pl.BlockSpec(memory_space=pl.ANY),
                pl.BlockSpec((1, d), lambda i, dref: (0, 0)),
            ],
            out_specs=[
                pl.BlockSpec((tm, d), lambda i, dref: (jnp.minimum(i, npt - 1), 0)),
                pl.BlockSpec((ts, d), lambda i, dref: (0, 0)),
            ],
            scratch_shapes=[pltpu.VMEM((2, tm, d), F32), pltpu.VMEM((2, tm, d), F32),
                            pltpu.SemaphoreType.DMA((2,))],
        ),
        out_shape=[jax.ShapeDtypeStruct((tp, d), F32), jax.ShapeDtypeStruct((ts, d), F32)],
        compiler_params=_params(("arbitrary",)),
        name="final",
    )(dest, h_all, rw, y_sorted, g)


def _routing_tables(eid):
    i32 = jnp.int32
    onehot = (eid[:, None] == jnp.arange(N_EXPERTS, dtype=i32)[None, :]).astype(i32)
    csum = jnp.cumsum(onehot, axis=0)
    counts = csum[-1]
    nsub = (counts + EXPERT_SUB - 1) // EXPERT_SUB
    bend = jnp.cumsum(nsub)
    bstart = bend - nsub
    seg = bstart * EXPERT_SUB
    dest = jnp.sum(onehot * (csum - 1 + seg[None, :]), axis=1).astype(i32)
    pad0 = (seg + counts).astype(i32)
    padn = (nsub * EXPERT_SUB - counts).astype(i32)
    bstart_all = jnp.concatenate([bstart, bend[-1:]]).astype(i32)
    return (dest, pad0, padn, bend[-1:].astype(i32)), bstart_all


def _group_major(a, width):
    lead = a.shape[:-1]
    g = a.shape[-1] // width
    return jnp.moveaxis(a.reshape(lead + (g, width)), -2, 0)


def kernel(x_prompt, x_sample, cache_mem_k, cache_mem_v, state_conv_a, state_conv_m, state_ssm, mem_prompt,
           norm_mix, w_in, conv_a_w, w_a_out, conv_m_w, conv_m_b, dt_bias, a_log, d_skip, ssm_norm,
           w_m_out, w_o, norm_cross, norm_mem, w_q, w_k, w_v, w_co, norm_ffn, w_rg, b_rg, w_re, b_re,
           w_gate, w_up, w_down, norm_final):
    depth = w_in.shape[0]
    assert depth == 1, "single-layer step"
    nbp, seq_p, d = x_prompt.shape
    nbs, seq_s, _ = x_sample.shape
    tp, ts = nbp * seq_p, nbs * seq_s
    n_mem = mem_prompt.shape[1]
    d_conv = conv_a_w.shape[2]
    d_inner = w_m_out.shape[1]
    nheads = dt_bias.shape[1]
    bc_w = N_GROUPS * D_STATE
    l = 0

    xp = x_prompt.reshape(tp, d)
    xs = x_sample.reshape(ts, d)
    w_in_t = jnp.swapaxes(w_in[l], 0, 1)
    col_z = 3 * d_conv
    col_x = col_z + d_inner
    col_b = col_x + d_inner
    col_dt = col_b + 2 * bc_w
    col_g = col_dt + nheads
    w_dt = w_in_t[col_dt:col_g]

    mk4, mv4 = _memory_kv(mem_prompt.reshape(nbp * n_mem, d), norm_mem[l][None], w_k[l], w_v[l], nbp)

    xnp, xns, dt_p, dt_s = _norm_dt(xp, xs, norm_mix[l][None], w_dt)
    abv_p, abv_s, ca_p, ca_s = _proj_a(xnp, xns, w_in_t, conv_a_w[l], state_conv_a[l], nbp, seq_p, d_conv)
    z_p, z_s = _proj_raw(xnp, xns, w_in_t, col_z, d_inner, GROUP_W)
    cw, cbias, scm = conv_m_w[l], conv_m_b[l][None], state_conv_m[l]
    xc_p, xc_s, cmx_p, cmx_s = _proj_conv(xnp, xns, w_in_t, cw, cbias, scm, col_x, 0, d_inner, GROUP_W,
                                           nbp, seq_p)
    bc_p, bc_s, cmb_p, cmb_s = _proj_conv(xnp, xns, w_in_t, cw, cbias, scm, col_b, d_inner, 2 * bc_w, D_STATE,
                                           nbp, seq_p)
    g_p, g_s = _proj_raw(xnp, xns, w_in_t, col_g, 2 * d, 0)

    prm = {
        "dtb": dt_bias[l][None],
        "alog": a_log[l][None],
        "dsk": _group_major(jnp.repeat(d_skip[l], HEAD_DIM)[None], GROUP_W),
        "gn": _group_major(ssm_norm[l][None], GROUP_W),
    }
    q_p = _pick(seq_p, (128, 64, 32, 16, 8))
    y_p, h_p = _ssd(z_p, xc_p, bc_p, dt_p, prm, nbp, seq_p, q_p)
    y_s, h_s = _ssd(z_s, xc_s, bc_s, dt_s, prm, nbs, seq_s, seq_s,
                    h_prev=state_ssm[l].reshape(nbs, N_GROUPS, GROUP_W, D_STATE))

    merged_p, merged_s = _merge(abv_p, abv_s, y_p, y_s, g_p, g_s, w_a_out[l],
                                w_m_out[l])
    h1_p, h1_s, hn1_p, hn1_s = _res(merged_p, merged_s, xp, xs, w_o[l], norm_cross[l][None], False, BF16)

    att_p = _attn(hn1_p, mk4, mv4, w_q[l], nbp, seq_p)
    att_s = _attn(hn1_s, cache_mem_k[l], cache_mem_v[l], w_q[l], nbs, seq_s)
    h2_all, hn2_all = _res(att_p, att_s, h1_p, h1_s, w_co[l], norm_ffn[l][None], True, F32)

    npad = LANES - N_EXPERT_GROUPS - N_EXPERTS
    w_r = jnp.concatenate([w_rg[l], w_re[l], jnp.zeros((d, npad), F32)], axis=1)
    b_r = jnp.concatenate([b_rg[l], b_re[l], jnp.zeros((npad,), F32)])[None]
    ri, rw = _router(hn2_all, w_r, b_r)
    tables, bstart = _routing_tables(ri[:, :2].reshape(-1))
    n_blocks = -(-2 * (tp + ts) // EXPERT_SUB) + N_EXPERTS
    x_sorted = _dispatch(tables, hn2_all, n_blocks)
    y_sorted = _experts(bstart, x_sorted, w_gate[l], w_up[l], w_down[l], 256)
    y_prompt, y_sample = _final(tables[0], h2_all, rw, y_sorted, norm_final[None], tp, ts)

    xh = d // N_XHEADS
    return (
        y_prompt.reshape(nbp, seq_p, d),
        y_sample.reshape(nbs, seq_s, d),
        mk4[None],
        mv4[None],
        ca_p[None],
        jnp.concatenate([cmx_p, cmb_p], axis=-1)[None],
        h_p.reshape(1, nbp, nheads, HEAD_DIM, D_STATE),
        ca_s[None],
        jnp.concatenate([cmx_s, cmb_s], axis=-1)[None],
        h_s.reshape(1, nbs, nheads, HEAD_DIM, D_STATE),
    )
```

```python
import functools

import jax
import jax.numpy as jnp
from jax import lax
from jax.experimental import pallas as pl
from jax.experimental.pallas import tpu as pltpu

F32 = jnp.float32
BF16 = jnp.bfloat16
EPS = 1e-6
LOG2_E = 1.4426950408889634

V7X_VMEM_BYTES = 64 * 1024 * 1024
VMEM_LIMIT = V7X_VMEM_BYTES - 8 * 1024 * 1024
LANES = 128
SUBLANES = 8

N_GROUPS = 8
HEADS_PER_GROUP = 8
HEAD_DIM = 64
D_STATE = 128
GROUP_W = HEADS_PER_GROUP * HEAD_DIM
N_XHEADS = 4
N_EXPERTS = 32
N_EXPERT_GROUPS = 4
EXPERTS_PER_GROUP = 8
EXPERT_SUB = 256
CONV_A_K = 3
CONV_M_K = 4

NT_DIMS = (((1,), (1,)), ((), ()))
TN_DIMS = (((0,), (0,)), ((), ()))


def _params(sem):
    return pltpu.CompilerParams(dimension_semantics=sem, vmem_limit_bytes=VMEM_LIMIT)


def _pick(n, cands):
    for c in cands:
        if n % c == 0:
            return c
    raise ValueError(f"no tile for {n} in {cands}")


def _dot(a, b):
    return jnp.dot(a, b, preferred_element_type=F32)


def _dot_nt(a, b):
    return lax.dot_general(a, b, NT_DIMS, preferred_element_type=F32)


def _rms(x, g):
    return x * lax.rsqrt(jnp.mean(x * x, axis=-1, keepdims=True) + EPS) * g


def _split3(x):
    hi = x.astype(BF16)
    r = x - hi.astype(F32)
    mid = r.astype(BF16)
    lo = (r - mid.astype(F32)).astype(BF16)
    return hi, mid, lo


def _softplus(x):
    return jnp.maximum(x, 0.0) + jnp.log1p(jnp.exp(-jnp.abs(x)))


def _norm_dt_kernel(xp_ref, xs_ref, g_ref, wdt_ref, xnp_ref, xns_ref, dtp_ref, dts_ref):
    wdt = wdt_ref[...].astype(BF16)

    def one(x_ref, xn_ref, dt_ref):
        xn = _rms(x_ref[...], g_ref[...]).astype(BF16)
        xn_ref[...] = xn
        dt_ref[...] = _dot_nt(xn, wdt)

    one(xp_ref, xnp_ref, dtp_ref)

    @pl.when(pl.program_id(0) == 0)
    def _():
        one(xs_ref, xns_ref, dts_ref)


def _norm_dt(xp, xs, g, wdt):
    tp, d = xp.shape
    ts = xs.shape[0]
    nh = wdt.shape[0]
    tm = _pick(tp, (512, 256, 128))
    return pl.pallas_call(
        _norm_dt_kernel,
        grid=(tp // tm,),
        in_specs=[
            pl.BlockSpec((tm, d), lambda i: (i, 0)),
            pl.BlockSpec((ts, d), lambda i: (0, 0)),
            pl.BlockSpec((1, d), lambda i: (0, 0)),
            pl.BlockSpec((nh, d), lambda i: (0, 0)),
        ],
        out_specs=[
            pl.BlockSpec((tm, d), lambda i: (i, 0)),
            pl.BlockSpec((ts, d), lambda i: (0, 0)),
            pl.BlockSpec((tm, nh), lambda i: (i, 0)),
            pl.BlockSpec((ts, nh), lambda i: (0, 0)),
        ],
        out_shape=[
            jax.ShapeDtypeStruct((tp, d), BF16),
            jax.ShapeDtypeStruct((ts, d), BF16),
            jax.ShapeDtypeStruct((tp, nh), F32),
            jax.ShapeDtypeStruct((ts, nh), F32),
        ],
        compiler_params=_params(("arbitrary",)),
        name="norm_dt",
    )(xp, xs, g, wdt)


def _proj_a_kernel(xp_ref, xs_ref, wb_ref, wc_ref, wh_ref, cw_ref, st_ref,
                   op_ref, os_ref, cap_ref, cas_ref,
                   wbf, ubuf, sbuf, s1buf, s2buf, *, tiles_per_batch, nb_s, l_s):
    i = pl.program_id(1)
    tm = xp_ref.shape[0]
    ts = xs_ref.shape[0]
    cw = cw_ref[...]

    @pl.when(i == 0)
    def _():
        wbf[0] = wb_ref[...].astype(BF16)
        wbf[1] = wc_ref[...].astype(BF16)
        wbf[2] = wh_ref[...].astype(BF16)
        x = xs_ref[...]
        u = _dot_nt(x, wbf[1]) * _dot_nt(x, wbf[2])
        sbuf[pl.ds(0, SUBLANES), :] = jnp.zeros((SUBLANES, u.shape[1]), F32)
        sbuf[pl.ds(SUBLANES, ts), :] = u
        s1buf[...] = jnp.zeros_like(s1buf)
        s2buf[...] = jnp.zeros_like(s2buf)
        for b in range(nb_s):
            s1buf[pl.ds(b * l_s, 1), :] = st_ref[b, pl.ds(1, 1), :]
            s2buf[pl.ds(b * l_s, 1), :] = st_ref[b, pl.ds(0, 1), :]
            s2buf[pl.ds(b * l_s + 1, 1), :] = st_ref[b, pl.ds(1, 1), :]
        rmod = lax.broadcasted_iota(jnp.int32, (ts, 1), 0) % l_s
        prev1 = jnp.where(rmod == 0, s1buf[...], sbuf[pl.ds(SUBLANES - 1, ts), :])
        prev2 = jnp.where(rmod < 2, s2buf[...], sbuf[pl.ds(SUBLANES - 2, ts), :])
        v = prev2 * cw[0:1, :] + prev1 * cw[1:2, :] + u * cw[2:3, :]
        os_ref[...] = (_dot_nt(x, wbf[0]) * v).astype(BF16)
        for b in range(nb_s):
            cas_ref[b] = sbuf[pl.ds(SUBLANES + (b + 1) * l_s - 2, 2), :]

    @pl.when(i % tiles_per_batch == 0)
    def _():
        ubuf[pl.ds(0, SUBLANES), :] = jnp.zeros((SUBLANES, ubuf.shape[1]), F32)

    x = xp_ref[...]
    u = _dot_nt(x, wbf[1]) * _dot_nt(x, wbf[2])
    ubuf[pl.ds(SUBLANES, tm), :] = u
    v = (ubuf[pl.ds(SUBLANES - 2, tm), :] * cw[0:1, :]
         + ubuf[pl.ds(SUBLANES - 1, tm), :] * cw[1:2, :] + u * cw[2:3, :])
    op_ref[...] = (_dot_nt(x, wbf[0]) * v).astype(BF16)
    ubuf[pl.ds(0, SUBLANES), :] = ubuf[pl.ds(tm, SUBLANES), :]

    @pl.when(i % tiles_per_batch == tiles_per_batch - 1)
    def _():
        cap_ref[0] = ubuf[pl.ds(SUBLANES + tm - 2, 2), :]


def _proj_a(xnp, xns, w_in, conv_w, state_s, n_batch_p, seq_p, d_conv):
    tp, d = xnp.shape
    ts = xns.shape[0]
    nb_s = state_s.shape[0]
    l_s = ts // nb_s
    tn = 512
    tm = _pick(seq_p, (1024, 512, 256, 128))
    tpb = seq_p // tm
    ncol = d_conv // tn
    kern = functools.partial(_proj_a_kernel, tiles_per_batch=tpb, nb_s=nb_s, l_s=l_s)
    return pl.pallas_call(
        kern,
        grid=(ncol, tp // tm),
        in_specs=[
            pl.BlockSpec((tm, d), lambda j, i: (i, 0)),
            pl.BlockSpec((ts, d), lambda j, i: (0, 0)),
            pl.BlockSpec((tn, d), lambda j, i: (j, 0)),
            pl.BlockSpec((tn, d), lambda j, i: (j + ncol, 0)),
            pl.BlockSpec((tn, d), lambda j, i: (j + 2 * ncol, 0)),
            pl.BlockSpec((CONV_A_K, tn), lambda j, i: (0, j)),
            pl.BlockSpec((nb_s, CONV_A_K - 1, tn), lambda j, i: (0, 0, j)),
        ],
        out_specs=[
            pl.BlockSpec((tm, tn), lambda j, i: (i, j)),
            pl.BlockSpec((ts, tn), lambda j, i: (0, j)),
            pl.BlockSpec((1, CONV_A_K - 1, tn), lambda j, i: (i // tpb, 0, j)),
            pl.BlockSpec((nb_s, CONV_A_K - 1, tn), lambda j, i: (0, 0, j)),
        ],
        out_shape=[
            jax.ShapeDtypeStruct((tp, d_conv), BF16),
            jax.ShapeDtypeStruct((ts, d_conv), BF16),
            jax.ShapeDtypeStruct((n_batch_p, CONV_A_K - 1, d_conv), F32),
            jax.ShapeDtypeStruct((nb_s, CONV_A_K - 1, d_conv), F32),
        ],
        scratch_shapes=[
            pltpu.VMEM((3, tn, d), BF16),
            pltpu.VMEM((SUBLANES + tm, tn), F32),
            pltpu.VMEM((SUBLANES + ts, tn), F32),
            pltpu.VMEM((ts, tn), F32),
            pltpu.VMEM((ts, tn), F32),
        ],
        compiler_params=_params(("arbitrary", "arbitrary")),
        name="proj_a",
    )(xnp, xns, w_in, w_in, w_in, conv_w, state_s)


def _proj_raw_kernel(xp_ref, xs_ref, w_hbm, op_ref, os_ref, wbf, wstage, wsem, *, nsplit, width, row0):
    j = pl.program_id(0)
    tn = wbf.shape[0]

    def w_copy(jj, slot):
        rows = pl.ds(pl.multiple_of(row0 + jj * tn, SUBLANES), tn)
        return pltpu.make_async_copy(w_hbm.at[rows], wstage.at[slot], wsem.at[slot])

    def emit(x_ref, o_ref):
        acc = _dot_nt(x_ref[...], wbf[...])
        if nsplit == 0:
            o_ref[...] = acc
        else:
            for s in range(nsplit):
                o_ref[s] = acc[:, s * width:(s + 1) * width]

    @pl.when(pl.program_id(1) == 0)
    def _():
        @pl.when(j == 0)
        def _():
            w_copy(0, 0).start()

        w_copy(j, j % 2).wait()

        @pl.when(j + 1 < pl.num_programs(0))
        def _():
            w_copy(j + 1, (j + 1) % 2).start()

        wbf[...] = wstage[j % 2].astype(BF16)
        emit(xs_ref, os_ref)

    emit(xp_ref, op_ref)


def _proj_raw(xnp, xns, w, col0, ncols, width):
    tp, d = xnp.shape
    ts = xns.shape[0]
    tn = 1024
    tm = _pick(tp, (1024, 512, 256, 128))
    assert col0 % SUBLANES == 0 and ncols % tn == 0
    if width == 0:
        nsplit = 0
        out_specs = [pl.BlockSpec((tm, tn), lambda j, i: (i, j)),
                     pl.BlockSpec((ts, tn), lambda j, i: (0, j))]
        out_shape = [jax.ShapeDtypeStruct((tp, ncols), F32), jax.ShapeDtypeStruct((ts, ncols), F32)]
    else:
        nsplit = tn // width
        out_specs = [pl.BlockSpec((nsplit, tm, width), lambda j, i: (j, i, 0)),
                     pl.BlockSpec((nsplit, ts, width), lambda j, i: (j, 0, 0))]
        out_shape = [jax.ShapeDtypeStruct((ncols // width, tp, width), F32),
                     jax.ShapeDtypeStruct((ncols // width, ts, width), F32)]
    kern = functools.partial(_proj_raw_kernel, nsplit=nsplit, width=width, row0=col0)
    return pl.pallas_call(
        kern,
        grid=(ncols // tn, tp // tm),
        in_specs=[
            pl.BlockSpec((tm, d), lambda j, i: (i, 0)),
            pl.BlockSpec((ts, d), lambda j, i: (0, 0)),
            pl.BlockSpec(memory_space=pl.ANY),
        ],
        out_specs=out_specs,
        out_shape=out_shape,
        scratch_shapes=[pltpu.VMEM((tn, d), BF16), pltpu.VMEM((2, tn, d), F32), pltpu.SemaphoreType.DMA((2,))],
        compiler_params=_params(("arbitrary", "arbitrary")),
        name=f"proj_raw_{col0}",
    )(xnp, xns, w)


def _proj_conv_kernel(xp_ref, xs_ref, w_ref, cw_ref, cb_ref, st_ref, op_ref, os_ref, cmp_ref, cms_ref,
                      wbf, ubuf, sbuf, fix, *, tiles_per_batch, nb_s, l_s, nsplit, width):
    i = pl.program_id(1)
    tm = xp_ref.shape[0]
    ts = xs_ref.shape[0]
    tail = CONV_M_K - 1
    cw = cw_ref[...]
    bias = cb_ref[...]

    def store(o_ref, act):
        for s in range(nsplit):
            o_ref[s] = act[:, s * width:(s + 1) * width]

    @pl.when(i == 0)
    def _():
        wbf[...] = w_ref[...].astype(BF16)
        raw = _dot_nt(xs_ref[...], wbf[...])
        sbuf[pl.ds(0, SUBLANES), :] = jnp.zeros((SUBLANES, raw.shape[1]), F32)
        sbuf[pl.ds(SUBLANES, ts), :] = raw
        fix[...] = jnp.zeros_like(fix)
        for b in range(nb_s):
            for back in range(1, tail + 1):
                for m in range(back):
                    fix[back - 1, pl.ds(b * l_s + m, 1), :] = st_ref[b, pl.ds(tail + m - back, 1), :]
        rmod = lax.broadcasted_iota(jnp.int32, (ts, 1), 0) % l_s
        acc = raw * cw[tail:tail + 1, :]
        for back in range(1, tail + 1):
            tap = jnp.where(rmod < back, fix[back - 1], sbuf[pl.ds(SUBLANES - back, ts), :])
            acc = acc + tap * cw[tail - back:tail - back + 1, :]
        store(os_ref, jax.nn.silu(acc + bias))
        for b in range(nb_s):
            cms_ref[b] = sbuf[pl.ds(SUBLANES + (b + 1) * l_s - tail, tail), :]

    @pl.when(i % tiles_per_batch == 0)
    def _():
        ubuf[pl.ds(0, SUBLANES), :] = jnp.zeros((SUBLANES, ubuf.shape[1]), F32)

    raw = _dot_nt(xp_ref[...], wbf[...])
    ubuf[pl.ds(SUBLANES, tm), :] = raw
    acc = raw * cw[tail:tail + 1, :]
    for back in range(1, tail + 1):
        acc = acc + ubuf[pl.ds(SUBLANES - back, tm), :] * cw[tail - back:tail - back + 1, :]
    store(op_ref, jax.nn.silu(acc + bias))
    ubuf[pl.ds(0, SUBLANES), :] = ubuf[pl.ds(tm, SUBLANES), :]

    @pl.when(i % tiles_per_batch == tiles_per_batch - 1)
    def _():
        cmp_ref[0] = ubuf[pl.ds(SUBLANES + tm - tail, tail), :]


def _proj_conv(xnp, xns, w, conv_w, conv_b, state_s, col0, ch0, ncols, width, n_batch_p, seq_p):
    tp, d = xnp.shape
    ts = xns.shape[0]
    nb_s = state_s.shape[0]
    l_s = ts // nb_s
    tail = CONV_M_K - 1
    tn = 1024
    tm = _pick(seq_p, (1024, 512, 256, 128))
    tpb = seq_p // tm
    assert col0 % tn == 0 and ch0 % tn == 0 and ncols % tn == 0 and l_s >= tail
    jb, cb0 = col0 // tn, ch0 // tn
    nsplit = tn // width
    kern = functools.partial(_proj_conv_kernel, tiles_per_batch=tpb, nb_s=nb_s, l_s=l_s,
                             nsplit=nsplit, width=width)
    return pl.pallas_call(
        kern,
        grid=(ncols // tn, tp // tm),
        in_specs=[
            pl.BlockSpec((tm, d), lambda j, i: (i, 0)),
            pl.BlockSpec((ts, d), lambda j, i: (0, 0)),
            pl.BlockSpec((tn, d), lambda j, i: (j + jb, 0)),
            pl.BlockSpec((CONV_M_K, tn), lambda j, i: (0, j + cb0)),
            pl.BlockSpec((1, tn), lambda j, i: (0, j + cb0)),
            pl.BlockSpec((nb_s, tail, tn), lambda j, i: (0, 0, j + cb0)),
        ],
        out_specs=[
            pl.BlockSpec((nsplit, tm, width), lambda j, i: (j, i, 0)),
            pl.BlockSpec((nsplit, ts, width), lambda j, i: (j, 0, 0)),
            pl.BlockSpec((1, tail, tn), lambda j, i: (i // tpb, 0, j)),
            pl.BlockSpec((nb_s, tail, tn), lambda j, i: (0, 0, j)),
        ],
        out_shape=[
            jax.ShapeDtypeStruct((ncols // width, tp, width), F32),
            jax.ShapeDtypeStruct((ncols // width, ts, width), F32),
            jax.ShapeDtypeStruct((n_batch_p, tail, ncols), F32),
            jax.ShapeDtypeStruct((nb_s, tail, ncols), F32),
        ],
        scratch_shapes=[
            pltpu.VMEM((tn, d), BF16),
            pltpu.VMEM((SUBLANES + tm, tn), F32),
            pltpu.VMEM((SUBLANES + ts, tn), F32),
            pltpu.VMEM((tail, ts, tn), F32),
        ],
        compiler_params=_params(("arbitrary", "arbitrary")),
        name=f"proj_conv_{col0}",
    )(xnp, xns, w, conv_w, conv_b, state_s)


def _ssd_kernel(*refs, q, has_state, nchunks):
    (z_ref, xs_ref, b_ref, c_ref, dt_ref, dtb_ref, alog_ref, dsk_ref, gn_ref, *rest) = refs
    if has_state:
        hprev, *rest = rest
    (y_ref, oh, h_s, acg, rowt) = rest
    c = pl.program_id(1)
    nheads = N_GROUPS * HEADS_PER_GROUP

    @pl.when(c == 0)
    def _init():
        if has_state:
            h_s[...] = hprev[...]
        else:
            h_s[...] = jnp.zeros_like(h_s)

    dt = _softplus(dt_ref[...] + dtb_ref[...])
    da = dt * (-jnp.exp(alog_ref[...]))
    ri = lax.broadcasted_iota(jnp.int32, (q, q), 0)
    ci = lax.broadcasted_iota(jnp.int32, (q, q), 1)
    causal = ri >= ci
    tril = jnp.where(causal, 1.0, 0.0).astype(BF16)
    acum = sum(_dot(tril, p) for p in _split3(da))
    eye = jnp.where(lax.broadcasted_iota(jnp.int32, (nheads, nheads), 0)
                    == lax.broadcasted_iota(jnp.int32, (nheads, nheads), 1), 1.0, 0.0).astype(BF16)
    acum2 = acum * LOG2_E
    rowt[0] = sum(lax.dot_general(eye, p, NT_DIMS, preferred_element_type=F32) for p in _split3(acum2))
    rowt[1] = sum(lax.dot_general(eye, p, NT_DIMS, preferred_element_type=F32) for p in _split3(dt))
    wend = jnp.exp(acum[q - 1:q, :] - acum) * dt
    rowt[2] = sum(lax.dot_general(eye, p, NT_DIMS, preferred_element_type=F32) for p in _split3(wend))
    for g in range(N_GROUPS):
        acg[g] = acum2[:, g * HEADS_PER_GROUP:(g + 1) * HEADS_PER_GROUP]

    lane = lax.broadcasted_iota(jnp.int32, (1, LANES), 1)
    rowi = lax.broadcasted_iota(jnp.int32, (LANES, 1), 0)
    half_w = LANES // 2

    def group_body(g, carry):
        xs = xs_ref[g]
        bb = b_ref[g].astype(BF16)
        ccb16 = c_ref[g].astype(BF16)
        cb_ = lax.dot_general(ccb16, bb, NT_DIMS, preferred_element_type=F32)
        ac8 = acg[g]
        dsk = dsk_ref[g]
        lo_half = lane < half_w
        lo_rows = rowi < half_w
        ys = []
        for pair in range(HEADS_PER_GROUP // 2):
            sl = slice(pair * LANES, (pair + 1) * LANES)
            xp = xs[:, sl]
            hp = h_s[g, pl.ds(pair * LANES, LANES), :]
            ms_, acols, wrows = [], [], []
            for r in (2 * pair, 2 * pair + 1):
                head = g * HEADS_PER_GROUP + r
                acol = jnp.broadcast_to(ac8[:, r:r + 1], (q, LANES))
                arow = rowt[0, pl.ds(head, 1), :]
                drow = rowt[1, pl.ds(head, 1), :]
                decay = jnp.exp2(jnp.where(causal, acol[:, :q] - arow, -jnp.inf))
                ms_.append((cb_ * decay * drow).astype(BF16))
                acols.append(acol)
                wrows.append(jnp.broadcast_to(rowt[2, pl.ds(head, 1), :], (half_w, q)))
            xlo = jnp.where(lo_half, xp, 0.0).astype(BF16)
            xhi = jnp.where(lo_half, 0.0, xp).astype(BF16)
            if q % LANES == 0:
                ydiag = _dot(jnp.concatenate(ms_, axis=1), jnp.concatenate([xlo, xhi], axis=0))
            else:
                ydiag = _dot(ms_[0], xlo) + _dot(ms_[1], xhi)
            ea = jnp.where(lo_half, jnp.exp2(acols[0]), jnp.exp2(acols[1]))
            yoff = ea * lax.dot_general(ccb16, hp.astype(BF16), NT_DIMS, preferred_element_type=F32)
            ys.append(dsk[:, sl] * xp + ydiag + yoff)
            xwt = (xp.T * jnp.concatenate(wrows, axis=0)).astype(BF16)
            dlast = [jnp.broadcast_to(jnp.exp2(a[q - 1:q, :]), (LANES, LANES)) for a in acols]
            h_s[g, pl.ds(pair * LANES, LANES), :] = hp * jnp.where(lo_rows, dlast[0], dlast[1]) + _dot(xwt, bb)
        yg = jnp.concatenate(ys, axis=1)
        hh = yg * jax.nn.silu(z_ref[g])
        ms = jnp.mean(hh * hh, axis=-1, keepdims=True)
        y_ref[g] = (hh * lax.rsqrt(ms + EPS) * gn_ref[g]).astype(BF16)
        return carry

    lax.fori_loop(0, N_GROUPS, group_body, 0, unroll=2)

    @pl.when(c == nchunks - 1)
    def _():
        oh[...] = h_s[...]


def _ssd(z, xc, bc, dt_raw, prm, n_batch, seq, q, h_prev=None):
    nchunks = seq // q
    nheads = N_GROUPS * HEADS_PER_GROUP
    has_state = h_prev is not None
    g8 = N_GROUPS

    def tok(first):
        return lambda b, c: (first, b * nchunks + c, 0)

    def const3(b, c):
        return (0, 0, 0)

    in_specs = [
        pl.BlockSpec((g8, q, GROUP_W), tok(0)),
        pl.BlockSpec((g8, q, GROUP_W), tok(0)),
        pl.BlockSpec((g8, q, D_STATE), tok(0)),
        pl.BlockSpec((g8, q, D_STATE), tok(1)),
        pl.BlockSpec((q, nheads), lambda b, c: (b * nchunks + c, 0)),
        pl.BlockSpec((1, nheads), lambda b, c: (0, 0)),
        pl.BlockSpec((1, nheads), lambda b, c: (0, 0)),
        pl.BlockSpec((g8, 1, GROUP_W), const3),
        pl.BlockSpec((g8, 1, GROUP_W), const3),
    ]
    args = [z, xc, bc, bc, dt_raw, prm["dtb"], prm["alog"], prm["dsk"], prm["gn"]]

    def per_batch(shape):
        return pl.BlockSpec((None,) + shape, lambda b, c: (b,) + (0,) * len(shape))

    if has_state:
        in_specs.append(per_batch((g8, GROUP_W, D_STATE)))
        args.append(h_prev)
    t = n_batch * seq
    out_specs = [
        pl.BlockSpec((g8, q, GROUP_W), lambda b, c: (0, b * nchunks + c, 0)),
        per_batch((g8, GROUP_W, D_STATE)),
    ]
    out_shape = [
        jax.ShapeDtypeStruct((g8, t, GROUP_W), BF16),
        jax.ShapeDtypeStruct((n_batch, g8, GROUP_W, D_STATE), F32),
    ]
    scratch = [
        pltpu.VMEM((g8, GROUP_W, D_STATE), F32),
        pltpu.VMEM((g8, q, HEADS_PER_GROUP), F32),
        pltpu.VMEM((3, nheads, q), F32),
    ]
    kern = functools.partial(_ssd_kernel, q=q, has_state=has_state, nchunks=nchunks)
    return pl.pallas_call(
        kern,
        grid=(n_batch, nchunks),
        in_specs=in_specs,
        out_specs=out_specs,
        out_shape=out_shape,
        scratch_shapes=scratch,
        compiler_params=_params(("arbitrary", "arbitrary")),
        name="ssd_state" if has_state else "ssd",
    )(*args)


def _merge_kernel(ap_ref, as_ref, yp_ref, ys_ref, gap_ref, gmp_ref, gas_ref, gms_ref, wa_ref, wm_ref,
                  op_ref, os_ref, wab, wmb):
    def emit(a_ref, y_ref, ga_ref, gm_ref, o_ref):
        oa = _dot(a_ref[...], wab[...])
        y = jnp.concatenate([y_ref[g] for g in range(N_GROUPS)], axis=1)
        om = _dot(y, wmb[...])
        o_ref[...] = (jax.nn.sigmoid(ga_ref[...]) * oa + jax.nn.sigmoid(gm_ref[...]) * om).astype(BF16)

    @pl.when(pl.program_id(1) == 0)
    def _():
        wab[...] = wa_ref[...].astype(BF16)
        wmb[...] = wm_ref[...].astype(BF16)
        emit(as_ref, ys_ref, gas_ref, gms_ref, os_ref)

    emit(ap_ref, yp_ref, gap_ref, gmp_ref, op_ref)


def _merge(abv_p, abv_s, y_p, y_s, g_p, g_s, w_a_out, w_m_out):
    tp, dc = abv_p.shape
    ts = abv_s.shape[0]
    dm = w_a_out.shape[1]
    tn = 512
    tm = _pick(tp, (512, 256, 128))
    ncol = dm // tn
    return pl.pallas_call(
        _merge_kernel,
        grid=(ncol, tp // tm),
        in_specs=[
            pl.BlockSpec((tm, dc), lambda j, i: (i, 0)),
            pl.BlockSpec((ts, dc), lambda j, i: (0, 0)),
            pl.BlockSpec((N_GROUPS, tm, GROUP_W), lambda j, i: (0, i, 0)),
            pl.BlockSpec((N_GROUPS, ts, GROUP_W), lambda j, i: (0, 0, 0)),
            pl.BlockSpec((tm, tn), lambda j, i: (i, j)),
            pl.BlockSpec((tm, tn), lambda j, i: (i, j + ncol)),
            pl.BlockSpec((ts, tn), lambda j, i: (0, j)),
            pl.BlockSpec((ts, tn), lambda j, i: (0, j + ncol)),
            pl.BlockSpec((dc, tn), lambda j, i: (0, j)),
            pl.BlockSpec((N_GROUPS * GROUP_W, tn), lambda j, i: (0, j)),
        ],
        out_specs=[
            pl.BlockSpec((tm, tn), lambda j, i: (i, j)),
            pl.BlockSpec((ts, tn), lambda j, i: (0, j)),
        ],
        out_shape=[jax.ShapeDtypeStruct((tp, dm), BF16), jax.ShapeDtypeStruct((ts, dm), BF16)],
        scratch_shapes=[pltpu.VMEM((dc, tn), BF16), pltpu.VMEM((N_GROUPS * GROUP_W, tn), BF16)],
        compiler_params=_params(("arbitrary", "arbitrary")),
        name="merge",
    )(abv_p, abv_s, y_p, y_s, g_p, g_p, g_s, g_s, w_a_out, w_m_out)


def _res_kernel(ap_ref, as_ref, rp_ref, rs_ref, w_ref, g_ref, *outs, n_prompt_tiles, merged_out, ts):
    if merged_out:
        h_ref, hn_ref, wbf = outs
    else:
        hp_ref, hs_ref, hnp_ref, hns_ref, wbf = outs
    i = pl.program_id(0)

    @pl.when(i == 0)
    def _():
        wbf[...] = w_ref[...].astype(BF16)

    def emit(a_ref, r_ref, store_h, store_hn):
        h = r_ref[...] + _dot(a_ref[...], wbf[...])
        store_h(h)
        store_hn(_rms(h, g_ref[...]))

    if merged_out:
        @pl.when(i < n_prompt_tiles)
        def _():
            def sh(h):
                h_ref[...] = h

            def shn(hn):
                hn_ref[...] = hn

            emit(ap_ref, rp_ref, sh, shn)

        @pl.when(i == n_prompt_tiles)
        def _():
            def sh(h):
                h_ref[pl.ds(0, ts), :] = h

            def shn(hn):
                hn_ref[pl.ds(0, ts), :] = hn

            emit(as_ref, rs_ref, sh, shn)
    else:
        def shp(h):
            hp_ref[...] = h

        def shnp(hn):
            hnp_ref[...] = hn.astype(hnp_ref.dtype)

        emit(ap_ref, rp_ref, shp, shnp)

        @pl.when(i == 0)
        def _():
            def shs(h):
                hs_ref[...] = h

            def shns(hn):
                hns_ref[...] = hn.astype(hns_ref.dtype)

            emit(as_ref, rs_ref, shs, shns)


def _res(a_p, a_s, r_p, r_s, w, g, merged_out, hn_dtype):
    tp, d = a_p.shape
    ts = a_s.shape[0]
    tm = _pick(tp, (512, 256))
    assert tp % tm == 0 and ts <= tm
    npt = tp // tm
    last = npt - 1
    in_specs = [
        pl.BlockSpec((tm, d), lambda i: (jnp.minimum(i, last), 0)),
        pl.BlockSpec((ts, d), lambda i: (0, 0)),
        pl.BlockSpec((tm, d), lambda i: (jnp.minimum(i, last), 0)),
        pl.BlockSpec((ts, d), lambda i: (0, 0)),
        pl.BlockSpec((d, d), lambda i: (0, 0), pipeline_mode=pl.Buffered(1)),
        pl.BlockSpec((1, d), lambda i: (0, 0)),
    ]
    if merged_out:
        grid = (npt + 1,)
        out_specs = [pl.BlockSpec((tm, d), lambda i: (i, 0)), pl.BlockSpec((tm, d), lambda i: (i, 0))]
        out_shape = [jax.ShapeDtypeStruct((tp + ts, d), F32), jax.ShapeDtypeStruct((tp + ts, d), hn_dtype)]
    else:
        grid = (npt,)
        out_specs = [pl.BlockSpec((tm, d), lambda i: (i, 0)), pl.BlockSpec((ts, d), lambda i: (0, 0)),
                     pl.BlockSpec((tm, d), lambda i: (i, 0)), pl.BlockSpec((ts, d), lambda i: (0, 0))]
        out_shape = [jax.ShapeDtypeStruct((tp, d), F32), jax.ShapeDtypeStruct((ts, d), F32),
                     jax.ShapeDtypeStruct((tp, d), hn_dtype), jax.ShapeDtypeStruct((ts, d), hn_dtype)]
    kern = functools.partial(_res_kernel, n_prompt_tiles=npt, merged_out=merged_out, ts=ts)
    return pl.pallas_call(
        kern,
        grid=grid,
        in_specs=in_specs,
        out_specs=out_specs,
        out_shape=out_shape,
        scratch_shapes=[pltpu.VMEM((d, d), BF16)],
        compiler_params=_params(("arbitrary",)),
        name="res_merged" if merged_out else "res",
    )(a_p, a_s, r_p, r_s, w, g)


def _kv_kernel(m_ref, g_ref, wk_ref, wv_ref, k_hbm, v_hbm, mn, kvb, sem, *, n_batch):
    j = pl.program_id(0)

    @pl.when(j == 0)
    def _():
        mn[...] = _rms(m_ref[...], g_ref[...]).astype(BF16)

    kvb[0] = _dot(mn[...], wk_ref[...].astype(BF16))
    kvb[1] = _dot(mn[...], wv_ref[...].astype(BF16))
    nm = kvb.shape[1] // n_batch
    for h in range(N_XHEADS):
        @pl.when(j == h)
        def _(h=h):
            copies = [pltpu.make_async_copy(kvb.at[t, pl.ds(b * nm, nm)], dst.at[b, :, h, :], sem.at[t, b])
                      for t, dst in enumerate((k_hbm, v_hbm)) for b in range(n_batch)]
            for c in copies:
                c.start()
            for c in copies:
                c.wait()


def _memory_kv(mem2d, g, w_k, w_v, n_batch):
    m, d = mem2d.shape
    dh = d // N_XHEADS
    out_sds = jax.ShapeDtypeStruct((n_batch, m // n_batch, N_XHEADS, dh), F32)
    kern = functools.partial(_kv_kernel, n_batch=n_batch)
    return pl.pallas_call(
        kern,
        grid=(N_XHEADS,),
        in_specs=[
            pl.BlockSpec((m, d), lambda j: (0, 0)),
            pl.BlockSpec((1, d), lambda j: (0, 0)),
            pl.BlockSpec((d, dh), lambda j: (0, j)),
            pl.BlockSpec((d, dh), lambda j: (0, j)),
        ],
        out_specs=[pl.BlockSpec(memory_space=pl.ANY), pl.BlockSpec(memory_space=pl.ANY)],
        out_shape=[out_sds, out_sds],
        scratch_shapes=[pltpu.VMEM((m, d), BF16), pltpu.VMEM((2, m, dh), F32),
                        pltpu.SemaphoreType.DMA((2, n_batch))],
        compiler_params=_params(("arbitrary",)),
        name="memory_kv",
    )(mem2d, g, w_k, w_v)


def _attn_kernel(hn_ref, k_ref, v_ref, wq_ref, o_ref, wqb, kb, vb, kvf, sem, *q_all):
    b = pl.program_id(0)
    i = pl.program_id(1)
    tm = o_ref.shape[0]

    @pl.when((b == 0) & (i == 0))
    def _():
        wqb[...] = wq_ref[...].astype(BF16)
        if q_all:
            q_all[0][...] = _dot(hn_ref[...], wqb[...])

    d = wqb.shape[1]
    dh = d // N_XHEADS

    @pl.when(i == 0)
    def _():
        def copies(bb):
            slot = bb % 2
            return [pltpu.make_async_copy(src.at[bb, :, h, :], kvf.at[slot, t, h], sem.at[slot, t, h])
                    for t, src in enumerate((k_ref, v_ref)) for h in range(N_XHEADS)]

        @pl.when(b == 0)
        def _():
            for c in copies(b):
                c.start()

        for c in copies(b):
            c.wait()

        @pl.when(b + 1 < pl.num_programs(0))
        def _():
            for c in copies(b + 1):
                c.start()

        for h in range(N_XHEADS):
            kb[:, h * dh:(h + 1) * dh] = kvf[b % 2, 0, h].astype(BF16)
            vb[:, h * dh:(h + 1) * dh] = kvf[b % 2, 1, h].astype(BF16)

    if q_all:
        row0 = pl.multiple_of((b * pl.num_programs(1) + i) * tm, SUBLANES)
        q = q_all[0][pl.ds(row0, tm), :]
    else:
        q = _dot(hn_ref[...], wqb[...])
    outs = []
    for h in range(N_XHEADS):
        sl = slice(h * dh, (h + 1) * dh)
        s = lax.dot_general(q[:, sl].astype(BF16), kb[:, sl], NT_DIMS, preferred_element_type=F32)
        s = s * (dh ** -0.5)
        e = jnp.exp(s - jnp.max(s, axis=-1, keepdims=True))
        p = e / jnp.sum(e, axis=-1, keepdims=True)
        outs.append(_dot(p.astype(BF16), vb[:, sl]))
    o_ref[...] = jnp.concatenate(outs, axis=1).astype(BF16)


def _attn(hn, k, v, w_q, n_batch, seq):
    t, d = hn.shape
    nm, nh, dh = k.shape[1:]
    tm = _pick(seq, (512, 256, 128, 64, 32, 16))
    tpb = seq // tm
    kv_spec = pl.BlockSpec(memory_space=pl.ANY)
    scratch = [pltpu.VMEM((d, d), BF16), pltpu.VMEM((nm, d), BF16), pltpu.VMEM((nm, d), BF16),
               pltpu.VMEM((2, 2, nh, nm, dh), F32), pltpu.SemaphoreType.DMA((2, 2, nh))]
    if tm < 128 and t <= 512:
        hn_spec = pl.BlockSpec((t, d), lambda b, i: (0, 0))
        scratch.append(pltpu.VMEM((t, d), F32))
    else:
        hn_spec = pl.BlockSpec((tm, d), lambda b, i: (b * tpb + i, 0))
    return pl.pallas_call(
        _attn_kernel,
        grid=(n_batch, tpb),
        in_specs=[
            hn_spec,
            kv_spec,
            kv_spec,
            pl.BlockSpec((d, d), lambda b, i: (0, 0), pipeline_mode=pl.Buffered(1)),
        ],
        out_specs=pl.BlockSpec((tm, d), lambda b, i: (b * tpb + i, 0)),
        out_shape=jax.ShapeDtypeStruct((t, d), BF16),
        scratch_shapes=scratch,
        compiler_params=_params(("arbitrary", "arbitrary")),
        name=f"attn_{seq}",
    )(hn, k, v, w_q)


def _router_kernel(x_ref, w_ref, b_ref, ri_ref, rw_ref):
    logits = _dot(x_ref[...].astype(BF16), w_ref[...].astype(BF16)) + b_ref[...]
    lane_i = lax.broadcasted_iota(jnp.int32, logits.shape, 1)
    lane = lane_i.astype(F32)
    ninf = -jnp.inf
    big = float(LANES)
    is_g = lane < N_EXPERT_GROUPS
    gl = jnp.where(is_g, logits, ninf)
    gmax = jnp.max(gl, axis=-1, keepdims=True)
    gsel = jnp.min(jnp.where(gl == gmax, lane, big), axis=-1, keepdims=True)
    pg = 1.0 / jnp.sum(jnp.where(is_g, jnp.exp(gl - gmax), 0.0), axis=-1, keepdims=True)
    lo = N_EXPERT_GROUPS + EXPERTS_PER_GROUP * gsel
    el = jnp.where(lane >= lo, jnp.where(lane < lo + EXPERTS_PER_GROUP, logits, ninf), ninf)
    m1 = jnp.max(el, axis=-1, keepdims=True)
    i1 = jnp.min(jnp.where(el == m1, lane, big), axis=-1, keepdims=True)
    el2 = jnp.where(lane == i1, ninf, el)
    m2 = jnp.max(el2, axis=-1, keepdims=True)
    i2 = jnp.min(jnp.where(el2 == m2, lane, big), axis=-1, keepdims=True)
    e = jnp.exp(m2 - m1)
    w1 = pg / (1.0 + e)
    w2 = pg * e / (1.0 + e)
    ri_ref[...] = jnp.where(lane_i == 0, i1 - N_EXPERT_GROUPS,
                            jnp.where(lane_i == 1, i2 - N_EXPERT_GROUPS, 0.0)).astype(jnp.int32)
    rw_ref[...] = jnp.where(lane_i == 0, w1, jnp.where(lane_i == 1, w2, 0.0))


def _router(hn_all, w_r, b_r):
    t, d = hn_all.shape
    tm = _pick(t, (640, 512, 384, 256, 128))
    return pl.pallas_call(
        _router_kernel,
        grid=(t // tm,),
        in_specs=[
            pl.BlockSpec((tm, d), lambda i: (i, 0)),
            pl.BlockSpec((d, LANES), lambda i: (0, 0)),
            pl.BlockSpec((1, LANES), lambda i: (0, 0)),
        ],
        out_specs=[pl.BlockSpec((tm, LANES), lambda i: (i, 0)), pl.BlockSpec((tm, LANES), lambda i: (i, 0))],
        out_shape=[jax.ShapeDtypeStruct((t, LANES), jnp.int32), jax.ShapeDtypeStruct((t, LANES), F32)],
        compiler_params=_params(("arbitrary",)),
        name="router",
    )(hn_all, w_r, b_r)


def _row_copy(src, dst, s, d, sem):
    return pltpu.make_async_copy(src.at[pl.ds(s, 1)], dst.at[pl.ds(d, 1)], sem)


def _dispatch_kernel(dest_ref, pad0_ref, padn_ref, nsub_ref, x_ref, o_ref, zrow, zblk, sem, zsem, csem, bsem, *,
                     chunk, n_blocks):
    base = pl.program_id(0) * chunk

    @pl.when(pl.program_id(0) == 0)
    def _():
        zrow[...] = jnp.zeros_like(zrow)
        zblk[...] = jnp.zeros_like(zblk)

        def pad_plan(e):
            p0 = pad0_ref[e]
            head = jnp.minimum(padn_ref[e], (SUBLANES - p0 % SUBLANES) % SUBLANES)
            return p0, head, (padn_ref[e] - head) // SUBLANES

        def group_copy(row):
            return pltpu.make_async_copy(zrow, o_ref.at[pl.ds(pl.multiple_of(row, SUBLANES), SUBLANES)], csem)

        def pad_start(e, carry):
            p0, head, ngroups = pad_plan(e)

            def start_row(r, c):
                _row_copy(zrow, o_ref, 0, p0 + r, zsem).start()
                return c

            def start_group(j, c):
                group_copy(p0 + head + j * SUBLANES).start()
                return c

            lax.fori_loop(0, head, start_row, 0)
            return lax.fori_loop(0, ngroups, start_group, carry)

        def pad_wait(e, carry):
            _, head, ngroups = pad_plan(e)

            def wait_row(r, c):
                _row_copy(zrow, o_ref, 0, 0, zsem).wait()
                return c

            def wait_group(j, c):
                group_copy(0).wait()
                return c

            lax.fori_loop(0, head, wait_row, 0)
            return lax.fori_loop(0, ngroups, wait_group, carry)

        lax.fori_loop(0, N_EXPERTS, pad_start, 0)

        def blk_copy(b):
            return pltpu.make_async_copy(zblk, o_ref.at[pl.ds(pl.multiple_of(b * EXPERT_SUB, EXPERT_SUB),
                                                             EXPERT_SUB)], bsem)

        def tail_start(b, c):
            blk_copy(b).start()
            return c

        def tail_wait(b, c):
            blk_copy(b).wait()
            return c

        lax.fori_loop(nsub_ref[0], n_blocks, tail_start, 0)
        lax.fori_loop(0, N_EXPERTS, pad_wait, 0)
        lax.fori_loop(nsub_ref[0], n_blocks, tail_wait, 0)

    def issue(r, carry):
        t = base + r
        _row_copy(x_ref, o_ref, r, dest_ref[2 * t], sem).start()
        _row_copy(x_ref, o_ref, r, dest_ref[2 * t + 1], sem).start()
        return carry

    lax.fori_loop(0, chunk, issue, 0, unroll=8)

    for _ in range(2):
        pltpu.make_async_copy(x_ref, o_ref.at[pl.ds(0, chunk)], sem).wait()


def _dispatch(tables, hn_all, n_blocks):
    dest, pad0, padn, nsub = tables
    t, d = hn_all.shape
    chunk = _pick(t, (640, 512, 384, 256, 128))
    kern = functools.partial(_dispatch_kernel, chunk=chunk, n_blocks=n_blocks)
    return pl.pallas_call(
        kern,
        grid_spec=pltpu.PrefetchScalarGridSpec(
            num_scalar_prefetch=4,
            grid=(t // chunk,),
            in_specs=[pl.BlockSpec((chunk, d), lambda i, *_: (i, 0))],
            out_specs=pl.BlockSpec(memory_space=pl.ANY),
            scratch_shapes=[pltpu.VMEM((SUBLANES, d), F32), pltpu.VMEM((EXPERT_SUB, d), F32),
                            pltpu.SemaphoreType.DMA(()), pltpu.SemaphoreType.DMA(()),
                            pltpu.SemaphoreType.DMA(()), pltpu.SemaphoreType.DMA(())],
        ),
        out_shape=jax.ShapeDtypeStruct((n_blocks * EXPERT_SUB, d), F32),
        compiler_params=_params(("arbitrary",)),
        name="dispatch",
    )(dest, pad0, padn, nsub, hn_all)


def _expert_kernel(bstart_ref, wg_ref, wu_ref, wd_ref, x_hbm, y_hbm, gcache, ucache, dcache, xbuf, ybuf, xb,
                   xsem, ysem):
    e = pl.program_id(0)
    k = pl.program_id(1)
    nk = gcache.shape[1]
    total = bstart_ref[N_EXPERTS]

    def x_copy(b, slot):
        rows = pl.ds(pl.multiple_of(b * EXPERT_SUB, EXPERT_SUB), EXPERT_SUB)
        return pltpu.make_async_copy(x_hbm.at[rows], xbuf.at[slot], xsem.at[slot])

    def y_copy(b, slot):
        rows = pl.ds(pl.multiple_of(b * EXPERT_SUB, EXPERT_SUB), EXPERT_SUB)
        return pltpu.make_async_copy(ybuf.at[slot], y_hbm.at[rows], ysem.at[slot])

    @pl.when(e < N_EXPERTS)
    def _():
        slot = e % 2
        gcache[slot, k] = wg_ref[...].astype(BF16)
        ucache[slot, k] = wu_ref[...].astype(BF16)
        dcache[slot, k] = wd_ref[...].astype(BF16)

    @pl.when((e == 0) & (k == 0) & (total > 0))
    def _():
        x_copy(0, 0).start()

    @pl.when(e >= 1)
    def _():
        owner = e - 1
        wslot = owner % 2
        b0 = bstart_ref[owner]
        n = bstart_ref[owner + 1] - b0

        cpu = 2
        upb = nk // cpu

        def unit(u, carry):
            b = b0 + u // upb
            part = u % upb
            slot = b % 2

            def swiglu(x, p):
                kks = [p * cpu + c for c in range(cpu)]
                wg = jnp.concatenate([gcache[wslot, kk] for kk in kks], axis=1)
                wu = jnp.concatenate([ucache[wslot, kk] for kk in kks], axis=1)
                wd = jnp.concatenate([dcache[wslot, kk] for kk in kks], axis=0)
                hid = (jax.nn.silu(_dot(x, wg)) * _dot(x, wu)).astype(BF16)
                return _dot(hid, wd)

            @pl.when(part == 0)
            def _():
                x_copy(b, slot).wait()

                @pl.when(b + 1 < total)
                def _():
                    x_copy(b + 1, 1 - slot).start()

                x = xbuf[slot].astype(BF16)
                xb[...] = x

                @pl.when(b >= 2)
                def _():
                    y_copy(b - 2, slot).wait()

                ybuf[slot] = swiglu(x, 0)

            for p in range(1, upb):
                @pl.when(part == p)
                def _(p=p):
                    ybuf[slot] += swiglu(xb[...], p)
                    if p == upb - 1:
                        y_copy(b, slot).start()

            return carry

        lax.fori_loop((n * upb * k) // nk, (n * upb * (k + 1)) // nk, unit, 0)

    @pl.when((e == N_EXPERTS) & (k == nk - 1))
    def _():
        @pl.when(total >= 2)
        def _():
            y_copy(total - 2, total % 2).wait()

        @pl.when(total >= 1)
        def _():
            y_copy(total - 1, (total - 1) % 2).wait()


def _experts(bstart, x_sorted, w_gate, w_up, w_down, kchunk):
    nrows, d = x_sorted.shape
    de = w_gate.shape[2]
    nk = de // kchunk
    last = N_EXPERTS - 1

    def widx(e, k):
        return jnp.minimum(e, last), jnp.where(e <= last, k, nk - 1)

    def in_idx(e, k, b):
        ee, kk = widx(e, k)
        return (ee, 0, kk)

    def down_idx(e, k, b):
        ee, kk = widx(e, k)
        return (ee, kk, 0)

    return pl.pallas_call(
        _expert_kernel,
        grid_spec=pltpu.PrefetchScalarGridSpec(
            num_scalar_prefetch=1,
            grid=(N_EXPERTS + 1, nk),
            in_specs=[
                pl.BlockSpec((None, d, kchunk), in_idx),
                pl.BlockSpec((None, d, kchunk), in_idx),
                pl.BlockSpec((None, kchunk, d), down_idx),
                pl.BlockSpec(memory_space=pl.ANY),
            ],
            out_specs=pl.BlockSpec(memory_space=pl.ANY),
            scratch_shapes=[
                pltpu.VMEM((2, nk, d, kchunk), BF16),
                pltpu.VMEM((2, nk, d, kchunk), BF16),
                pltpu.VMEM((2, nk, kchunk, d), BF16),
                pltpu.VMEM((2, EXPERT_SUB, d), F32),
                pltpu.VMEM((2, EXPERT_SUB, d), F32),
                pltpu.VMEM((EXPERT_SUB, d), BF16),
                pltpu.SemaphoreType.DMA((2,)),
                pltpu.SemaphoreType.DMA((2,)),
            ],
        ),
        out_shape=jax.ShapeDtypeStruct((nrows, d), F32),
        input_output_aliases={4: 0},
        compiler_params=_params(("arbitrary", "arbitrary")),
        name="experts",
    )(bstart, w_gate, w_up, w_down, x_sorted)


def _final_kernel(dest_ref, h_ref, rw_ref, ys_ref, g_ref, yp_ref, yss_ref, ya, yb, sem, *,
                  tm, n_prompt_tiles, ts):
    i = pl.program_id(0)

    def gather(tile, n):
        slot = tile % 2

        def issue(r, carry):
            t = tile * tm + r
            _row_copy(ys_ref, ya.at[slot], dest_ref[2 * t], r, sem.at[slot]).start()
            _row_copy(ys_ref, yb.at[slot], dest_ref[2 * t + 1], r, sem.at[slot]).start()
            return carry

        lax.fori_loop(0, n, issue, 0, unroll=8)

    def emit(n, o_ref):
        slot = i % 2
        for buf in (ya, yb):
            pltpu.make_async_copy(ys_ref.at[pl.ds(0, n)], buf.at[slot, pl.ds(0, n)], sem.at[slot]).wait()
        w = rw_ref[pl.ds(0, n), :]
        h = (h_ref[pl.ds(0, n), :] + w[:, 0:1] * ya[slot, pl.ds(0, n), :]
             + w[:, 1:2] * yb[slot, pl.ds(0, n), :])
        o_ref[...] = _rms(h, g_ref[...])

    @pl.when(i == 0)
    def _():
        gather(i, tm)

    @pl.when(i + 1 < n_prompt_tiles)
    def _():
        gather(i + 1, tm)

    @pl.when(i + 1 == n_prompt_tiles)
    def _():
        gather(i + 1, ts)

    @pl.when(i < n_prompt_tiles)
    def _():
        emit(tm, yp_ref)

    @pl.when(i == n_prompt_tiles)
    def _():
        emit(ts, yss_ref)


def _final(dest, h_all, rw, y_sorted, g, tp, ts):
    t, d = h_all.shape
    tm = 256
    assert tp % tm == 0 and ts <= tm and t == tp + ts
    npt = tp // tm
    kern = functools.partial(_final_kernel, tm=tm, n_prompt_tiles=npt, ts=ts)
    return pl.pallas_call(
        kern,
        grid_spec=pltpu.PrefetchScalarGridSpec(
            num_scalar_prefetch=1,
            grid=(npt + 1,),
            in_specs=[
                pl.BlockSpec((tm, d), lambda i, dref: (i, 0)),
                pl.BlockSpec((tm, LANES), lambda i, dref: (i, 0)),
                pl.BlockSpec(memory_space=pl.ANY),
                pl.BlockSpec((1, d), lambda i, dref: (0, 0)),
            ],
            out_specs=[
                pl.BlockSpec((tm, d), lambda i, dref: (jnp.minimum(i, npt - 1), 0)),
                pl.BlockSpec((ts, d), lambda i, dref: (0, 0)),
            ],
            scratch_shapes=[pltpu.VMEM((2, tm, d), F32), pltpu.VMEM((2, tm, d), F32),
                            pltpu.SemaphoreType.DMA((2,))],
        ),
        out_shape=[jax.ShapeDtypeStruct((tp, d), F32), jax.ShapeDtypeStruct((ts, d), F32)],
        compiler_params=_params(("arbitrary",)),
        name="final",
    )(dest, h_all, rw, y_sorted, g)


def _routing_tables(eid):
    i32 = jnp.int32
    onehot = (eid[:, None] == jnp.arange(N_EXPERTS, dtype=i32)[None, :]).astype(i32)
    csum = jnp.cumsum(onehot, axis=0)
    counts = csum[-1]
    nsub = (counts + EXPERT_SUB - 1) // EXPERT_SUB
    bend = jnp.cumsum(nsub)
    bstart = bend - nsub
    seg = bstart * EXPERT_SUB
    dest = jnp.sum(onehot * (csum - 1 + seg[None, :]), axis=1).astype(i32)
    pad0 = (seg + counts).astype(i32)
    padn = (nsub * EXPERT_SUB - counts).astype(i32)
    bstart_all = jnp.concatenate([bstart, bend[-1:]]).astype(i32)
    return (dest, pad0, padn, bend[-1:].astype(i32)), bstart_all


def _group_major(a, width):
    lead = a.shape[:-1]
    g = a.shape[-1] // width
    return jnp.moveaxis(a.reshape(lead + (g, width)), -2, 0)


def kernel(x_prompt, x_sample, cache_mem_k, cache_mem_v, state_conv_a, state_conv_m, state_ssm, mem_prompt,
           norm_mix, w_in, conv_a_w, w_a_out, conv_m_w, conv_m_b, dt_bias, a_log, d_skip, ssm_norm,
           w_m_out, w_o, norm_cross, norm_mem, w_q, w_k, w_v, w_co, norm_ffn, w_rg, b_rg, w_re, b_re,
           w_gate, w_up, w_down, norm_final):
    depth = w_in.shape[0]
    assert depth == 1, "single-layer step"
    nbp, seq_p, d = x_prompt.shape
    nbs, seq_s, _ = x_sample.shape
    tp, ts = nbp * seq_p, nbs * seq_s
    n_mem = mem_prompt.shape[1]
    d_conv = conv_a_w.shape[2]
    d_inner = w_m_out.shape[1]
    nheads = dt_bias.shape[1]
    bc_w = N_GROUPS * D_STATE
    l = 0

    xp = x_prompt.reshape(tp, d)
    xs = x_sample.reshape(ts, d)
    w_in_t = jnp.swapaxes(w_in[l], 0, 1)
    col_z = 3 * d_conv
    col_x = col_z + d_inner
    col_b = col_x + d_inner
    col_dt = col_b + 2 * bc_w
    col_g = col_dt + nheads
    w_dt = w_in_t[col_dt:col_g]

    mk4, mv4 = _memory_kv(mem_prompt.reshape(nbp * n_mem, d), norm_mem[l][None], w_k[l], w_v[l], nbp)

    xnp, xns, dt_p, dt_s = _norm_dt(xp, xs, norm_mix[l][None], w_dt)
    abv_p, abv_s, ca_p, ca_s = _proj_a(xnp, xns, w_in_t, conv_a_w[l], state_conv_a[l], nbp, seq_p, d_conv)
    z_p, z_s = _proj_raw(xnp, xns, w_in_t, col_z, d_inner, GROUP_W)
    cw, cbias, scm = conv_m_w[l], conv_m_b[l][None], state_conv_m[l]
    xc_p, xc_s, cmx_p, cmx_s = _proj_conv(xnp, xns, w_in_t, cw, cbias, scm, col_x, 0, d_inner, GROUP_W,
                                           nbp, seq_p)
    bc_p, bc_s, cmb_p, cmb_s = _proj_conv(xnp, xns, w_in_t, cw, cbias, scm, col_b, d_inner, 2 * bc_w, D_STATE,
                                           nbp, seq_p)
    g_p, g_s = _proj_raw(xnp, xns, w_in_t, col_g, 2 * d, 0)

    prm = {
        "dtb": dt_bias[l][None],
        "alog": a_log[l][None],
        "dsk": _group_major(jnp.repeat(d_skip[l], HEAD_DIM)[None], GROUP_W),
        "gn": _group_major(ssm_norm[l][None], GROUP_W),
    }
    q_p = _pick(seq_p, (128, 64, 32, 16, 8))
    y_p, h_p = _ssd(z_p, xc_p, bc_p, dt_p, prm, nbp, seq_p, q_p)
    y_s, h_s = _ssd(z_s, xc_s, bc_s, dt_s, prm, nbs, seq_s, seq_s,
                    h_prev=state_ssm[l].reshape(nbs, N_GROUPS, GROUP_W, D_STATE))

    merged_p, merged_s = _merge(abv_p, abv_s, y_p, y_s, g_p, g_s, w_a_out[l],
                                w_m_out[l])
    h1_p, h1_s, hn1_p, hn1_s = _res(merged_p, merged_s, xp, xs, w_o[l], norm_cross[l][None], False, BF16)

    att_p = _attn(hn1_p, mk4, mv4, w_q[l], nbp, seq_p)
    att_s = _attn(hn1_s, cache_mem_k[l], cache_mem_v[l], w_q[l], nbs, seq_s)
    h2_all, hn2_all = _res(att_p, att_s, h1_p, h1_s, w_co[l], norm_ffn[l][None], True, F32)

    npad = LANES - N_EXPERT_GROUPS - N_EXPERTS
    w_r = jnp.concatenate([w_rg[l], w_re[l], jnp.zeros((d, npad), F32)], axis=1)
    b_r = jnp.concatenate([b_rg[l], b_re[l], jnp.zeros((npad,), F32)])[None]
    ri, rw = _router(hn2_all, w_r, b_r)
    tables, bstart = _routing_tables(ri[:, :2].reshape(-1))
    n_blocks = -(-2 * (tp + ts) // EXPERT_SUB) + N_EXPERTS
    x_sorted = _dispatch(tables, hn2_all, n_blocks)
    y_sorted = _experts(bstart, x_sorted, w_gate[l], w_up[l], w_down[l], 256)
    y_prompt, y_sample = _final(tables[0], h2_all, rw, y_sorted, norm_final[None], tp, ts)

    xh = d // N_XHEADS
    return (
        y_prompt.reshape(nbp, seq_p, d),
        y_sample.reshape(nbs, seq_s, d),
        mk4[None],
        mv4[None],
        ca_p[None],
        jnp.concatenate([cmx_p, cmb_p], axis=-1)[None],
        h_p.reshape(1, nbp, nheads, HEAD_DIM, D_STATE),
        ca_s[None],
        jnp.concatenate([cmx_s, cmb_s], axis=-1)[None],
        h_s.reshape(1, nbs, nheads, HEAD_DIM, D_STATE),
    )
```

```python
import functools

import jax
import jax.numpy as jnp
from jax import lax
from jax.experimental import pallas as pl
from jax.experimental.pallas import tpu as pltpu

F32 = jnp.float32
BF16 = jnp.bfloat16
EPS = 1e-6
LOG2_E = 1.4426950408889634

V7X_VMEM_BYTES = 64 * 1024 * 1024
VMEM_LIMIT = V7X_VMEM_BYTES - 8 * 1024 * 1024
LANES = 128
SUBLANES = 8

N_GROUPS = 8
HEADS_PER_GROUP = 8
HEAD_DIM = 64
D_STATE = 128
GROUP_W = HEADS_PER_GROUP * HEAD_DIM
N_XHEADS = 4
N_EXPERTS = 32
N_EXPERT_GROUPS = 4
EXPERTS_PER_GROUP = 8
EXPERT_SUB = 256
CONV_A_K = 3
CONV_M_K = 4

NT_DIMS = (((1,), (1,)), ((), ()))
TN_DIMS = (((0,), (0,)), ((), ()))


def _params(sem):
    return pltpu.CompilerParams(dimension_semantics=sem, vmem_limit_bytes=VMEM_LIMIT)


def _pick(n, cands):
    for c in cands:
        if n % c == 0:
            return c
    raise ValueError(f"no tile for {n} in {cands}")


def _dot(a, b):
    return jnp.dot(a, b, preferred_element_type=F32)


def _dot_nt(a, b):
    return lax.dot_general(a, b, NT_DIMS, preferred_element_type=F32)


def _rms(x, g):
    return x * lax.rsqrt(jnp.mean(x * x, axis=-1, keepdims=True) + EPS) * g


def _split3(x):
    hi = x.astype(BF16)
    r = x - hi.astype(F32)
    mid = r.astype(BF16)
    lo = (r - mid.astype(F32)).astype(BF16)
    return hi, mid, lo


def _softplus(x):
    return jnp.maximum(x, 0.0) + jnp.log1p(jnp.exp(-jnp.abs(x)))


def _norm_dt_kernel(xp_ref, xs_ref, g_ref, wdt_ref, xnp_ref, xns_ref, dtp_ref, dts_ref):
    wdt = wdt_ref[...].astype(BF16)

    def one(x_ref, xn_ref, dt_ref):
        xn = _rms(x_ref[...], g_ref[...]).astype(BF16)
        xn_ref[...] = xn
        dt_ref[...] = _dot_nt(xn, wdt)

    one(xp_ref, xnp_ref, dtp_ref)

    @pl.when(pl.program_id(0) == 0)
    def _():
        one(xs_ref, xns_ref, dts_ref)


def _norm_dt(xp, xs, g, wdt):
    tp, d = xp.shape
    ts = xs.shape[0]
    nh = wdt.shape[0]
    tm = _pick(tp, (512, 256, 128))
    return pl.pallas_call(
        _norm_dt_kernel,
        grid=(tp // tm,),
        in_specs=[
            pl.BlockSpec((tm, d), lambda i: (i, 0)),
            pl.BlockSpec((ts, d), lambda i: (0, 0)),
            pl.BlockSpec((1, d), lambda i: (0, 0)),
            pl.BlockSpec((nh, d), lambda i: (0, 0)),
        ],
        out_specs=[
            pl.BlockSpec((tm, d), lambda i: (i, 0)),
            pl.BlockSpec((ts, d), lambda i: (0, 0)),
            pl.BlockSpec((tm, nh), lambda i: (i, 0)),
            pl.BlockSpec((ts, nh), lambda i: (0, 0)),
        ],
        out_shape=[
            jax.ShapeDtypeStruct((tp, d), BF16),
            jax.ShapeDtypeStruct((ts, d), BF16),
            jax.ShapeDtypeStruct((tp, nh), F32),
            jax.ShapeDtypeStruct((ts, nh), F32),
        ],
        compiler_params=_params(("arbitrary",)),
        name="norm_dt",
    )(xp, xs, g, wdt)


def _proj_a_kernel(xp_ref, xs_ref, wb_ref, wc_ref, wh_ref, cw_ref, st_ref,
                   op_ref, os_ref, cap_ref, cas_ref,
                   wbf, ubuf, sbuf, s1buf, s2buf, *, tiles_per_batch, nb_s, l_s):
    i = pl.program_id(1)
    tm = xp_ref.shape[0]
    ts = xs_ref.shape[0]
    cw = cw_ref[...]

    @pl.when(i == 0)
    def _():
        wbf[0] = wb_ref[...].astype(BF16)
        wbf[1] = wc_ref[...].astype(BF16)
        wbf[2] = wh_ref[...].astype(BF16)
        x = xs_ref[...]
        u = _dot_nt(x, wbf[1]) * _dot_nt(x, wbf[2])
        sbuf[pl.ds(0, SUBLANES), :] = jnp.zeros((SUBLANES, u.shape[1]), F32)
        sbuf[pl.ds(SUBLANES, ts), :] = u
        s1buf[...] = jnp.zeros_like(s1buf)
        s2buf[...] = jnp.zeros_like(s2buf)
        for b in range(nb_s):
            s1buf[pl.ds(b * l_s, 1), :] = st_ref[b, pl.ds(1, 1), :]
            s2buf[pl.ds(b * l_s, 1), :] = st_ref[b, pl.ds(0, 1), :]
            s2buf[pl.ds(b * l_s + 1, 1), :] = st_ref[b, pl.ds(1, 1), :]
        rmod = lax.broadcasted_iota(jnp.int32, (ts, 1), 0) % l_s
        prev1 = jnp.where(rmod == 0, s1buf[...], sbuf[pl.ds(SUBLANES - 1, ts), :])
        prev2 = jnp.where(rmod < 2, s2buf[...], sbuf[pl.ds(SUBLANES - 2, ts), :])
        v = prev2 * cw[0:1, :] + prev1 * cw[1:2, :] + u * cw[2:3, :]
        os_ref[...] = (_dot_nt(x, wbf[0]) * v).astype(BF16)
        for b in range(nb_s):
            cas_ref[b] = sbuf[pl.ds(SUBLANES + (b + 1) * l_s - 2, 2), :]

    @pl.when(i % tiles_per_batch == 0)
    def _():
        ubuf[pl.ds(0, SUBLANES), :] = jnp.zeros((SUBLANES, ubuf.shape[1]), F32)

    x = xp_ref[...]
    u = _dot_nt(x, wbf[1]) * _dot_nt(x, wbf[2])
    ubuf[pl.ds(SUBLANES, tm), :] = u
    v = (ubuf[pl.ds(SUBLANES - 2, tm), :] * cw[0:1, :]
         + ubuf[pl.ds(SUBLANES - 1, tm), :] * cw[1:2, :] + u * cw[2:3, :])
    op_ref[...] = (_dot_nt(x, wbf[0]) * v).astype(BF16)
    ubuf[pl.ds(0, SUBLANES), :] = ubuf[pl.ds(tm, SUBLANES), :]

    @pl.when(i % tiles_per_batch == tiles_per_batch - 1)
    def _():
        cap_ref[0] = ubuf[pl.ds(SUBLANES + tm - 2, 2), :]


def _proj_a(xnp, xns, w_in, conv_w, state_s, n_batch_p, seq_p, d_conv):
    tp, d = xnp.shape
    ts = xns.shape[0]
    nb_s = state_s.shape[0]
    l_s = ts // nb_s
    tn = 512
    tm = _pick(seq_p, (1024, 512, 256, 128))
    tpb = seq_p // tm
    ncol = d_conv // tn
    kern = functools.partial(_proj_a_kernel, tiles_per_batch=tpb, nb_s=nb_s, l_s=l_s)
    return pl.pallas_call(
        kern,
        grid=(ncol, tp // tm),
        in_specs=[
            pl.BlockSpec((tm, d), lambda j, i: (i, 0)),
            pl.BlockSpec((ts, d), lambda j, i: (0, 0)),
            pl.BlockSpec((tn, d), lambda j, i: (j, 0)),
            pl.BlockSpec((tn, d), lambda j, i: (j + ncol, 0)),
            pl.BlockSpec((tn, d), lambda j, i: (j + 2 * ncol, 0)),
            pl.BlockSpec((CONV_A_K, tn), lambda j, i: (0, j)),
            pl.BlockSpec((nb_s, CONV_A_K - 1, tn), lambda j, i: (0, 0, j)),
        ],
        out_specs=[
            pl.BlockSpec((tm, tn), lambda j, i: (i, j)),
            pl.BlockSpec((ts, tn), lambda j, i: (0, j)),
            pl.BlockSpec((1, CONV_A_K - 1, tn), lambda j, i: (i // tpb, 0, j)),
            pl.BlockSpec((nb_s, CONV_A_K - 1, tn), lambda j, i: (0, 0, j)),
        ],
        out_shape=[
            jax.ShapeDtypeStruct((tp, d_conv), BF16),
            jax.ShapeDtypeStruct((ts, d_conv), BF16),
            jax.ShapeDtypeStruct((n_batch_p, CONV_A_K - 1, d_conv), F32),
            jax.ShapeDtypeStruct((nb_s, CONV_A_K - 1, d_conv), F32),
        ],
        scratch_shapes=[
            pltpu.VMEM((3, tn, d), BF16),
            pltpu.VMEM((SUBLANES + tm, tn), F32),
            pltpu.VMEM((SUBLANES + ts, tn), F32),
            pltpu.VMEM((ts, tn), F32),
            pltpu.VMEM((ts, tn), F32),
        ],
        compiler_params=_params(("arbitrary", "arbitrary")),
        name="proj_a",
    )(xnp, xns, w_in, w_in, w_in, conv_w, state_s)


def _proj_raw_kernel(xp_ref, xs_ref, w_hbm, op_ref, os_ref, wbf, wstage, wsem, *, nsplit, width, row0):
    j = pl.program_id(0)
    tn = wbf.shape[0]

    def w_copy(jj, slot):
        rows = pl.ds(pl.multiple_of(row0 + jj * tn, SUBLANES), tn)
        return pltpu.make_async_copy(w_hbm.at[rows], wstage.at[slot], wsem.at[slot])

    def emit(x_ref, o_ref):
        acc = _dot_nt(x_ref[...], wbf[...])
        if nsplit == 0:
            o_ref[...] = acc
        else:
            for s in range(nsplit):
                o_ref[s] = acc[:, s * width:(s + 1) * width]

    @pl.when(pl.program_id(1) == 0)
    def _():
        @pl.when(j == 0)
        def _():
            w_copy(0, 0).start()

        w_copy(j, j % 2).wait()

        @pl.when(j + 1 < pl.num_programs(0))
        def _():
            w_copy(j + 1, (j + 1) % 2).start()

        wbf[...] = wstage[j % 2].astype(BF16)
        emit(xs_ref, os_ref)

    emit(xp_ref, op_ref)


def _proj_raw(xnp, xns, w, col0, ncols, width):
    tp, d = xnp.shape
    ts = xns.shape[0]
    tn = 1024
    tm = _pick(tp, (1024, 512, 256, 128))
    assert col0 % SUBLANES == 0 and ncols % tn == 0
    if width == 0:
        nsplit = 0
        out_specs = [pl.BlockSpec((tm, tn), lambda j, i: (i, j)),
                     pl.BlockSpec((ts, tn), lambda j, i: (0, j))]
        out_shape = [jax.ShapeDtypeStruct((tp, ncols), F32), jax.ShapeDtypeStruct((ts, ncols), F32)]
    else:
        nsplit = tn // width
        out_specs = [pl.BlockSpec((nsplit, tm, width), lambda j, i: (j, i, 0)),
                     pl.BlockSpec((nsplit, ts, width), lambda j, i: (j, 0, 0))]
        out_shape = [jax.ShapeDtypeStruct((ncols // width, tp, width), F32),
                     jax.ShapeDtypeStruct((ncols // width, ts, width), F32)]
    kern = functools.partial(_proj_raw_kernel, nsplit=nsplit, width=width, row0=col0)
    return pl.pallas_call(
        kern,
        grid=(ncols // tn, tp // tm),
        in_specs=[
            pl.BlockSpec((tm, d), lambda j, i: (i, 0)),
            pl.BlockSpec((ts, d), lambda j, i: (0, 0)),
            pl.BlockSpec(memory_space=pl.ANY),
        ],
        out_specs=out_specs,
        out_shape=out_shape,
        scratch_shapes=[pltpu.VMEM((tn, d), BF16), pltpu.VMEM((2, tn, d), F32), pltpu.SemaphoreType.DMA((2,))],
        compiler_params=_params(("arbitrary", "arbitrary")),
        name=f"proj_raw_{col0}",
    )(xnp, xns, w)


def _proj_conv_kernel(xp_ref, xs_ref, w_ref, cw_ref, cb_ref, st_ref, op_ref, os_ref, cmp_ref, cms_ref,
                      wbf, ubuf, sbuf, fix, *, tiles_per_batch, nb_s, l_s, nsplit, width):
    i = pl.program_id(1)
    tm = xp_ref.shape[0]
    ts = xs_ref.shape[0]
    tail = CONV_M_K - 1
    cw = cw_ref[...]
    bias = cb_ref[...]

    def store(o_ref, act):
        for s in range(nsplit):
            o_ref[s] = act[:, s * width:(s + 1) * width]

    @pl.when(i == 0)
    def _():
        wbf[...] = w_ref[...].astype(BF16)
        raw = _dot_nt(xs_ref[...], wbf[...])
        sbuf[pl.ds(0, SUBLANES), :] = jnp.zeros((SUBLANES, raw.shape[1]), F32)
        sbuf[pl.ds(SUBLANES, ts), :] = raw
        fix[...] = jnp.zeros_like(fix)
        for b in range(nb_s):
            for back in range(1, tail + 1):
                for m in range(back):
                    fix[back - 1, pl.ds(b * l_s + m, 1), :] = st_ref[b, pl.ds(tail + m - back, 1), :]
        rmod = lax.broadcasted_iota(jnp.int32, (ts, 1), 0) % l_s
        acc = raw * cw[tail:tail + 1, :]
        for back in range(1, tail + 1):
            tap = jnp.where(rmod < back, fix[back - 1], sbuf[pl.ds(SUBLANES - back, ts), :])
            acc = acc + tap * cw[tail - back:tail - back + 1, :]
        store(os_ref, jax.nn.silu(acc + bias))
        for b in range(nb_s):
            cms_ref[b] = sbuf[pl.ds(SUBLANES + (b + 1) * l_s - tail, tail), :]

    @pl.when(i % tiles_per_batch == 0)
    def _():
        ubuf[pl.ds(0, SUBLANES), :] = jnp.zeros((SUBLANES, ubuf.shape[1]), F32)

    raw = _dot_nt(xp_ref[...], wbf[...])
    ubuf[pl.ds(SUBLANES, tm), :] = raw
    acc = raw * cw[tail:tail + 1, :]
    for back in range(1, tail + 1):
        acc = acc + ubuf[pl.ds(SUBLANES - back, tm), :] * cw[tail - back:tail - back + 1, :]
    store(op_ref, jax.nn.silu(acc + bias))
    ubuf[pl.ds(0, SUBLANES), :] = ubuf[pl.ds(tm, SUBLANES), :]

    @pl.when(i % tiles_per_batch == tiles_per_batch - 1)
    def _():
        cmp_ref[0] = ubuf[pl.ds(SUBLANES + tm - tail, tail), :]


def _proj_conv(xnp, xns, w, conv_w, conv_b, state_s, col0, ch0, ncols, width, n_batch_p, seq_p):
    tp, d = xnp.shape
    ts = xns.shape[0]
    nb_s = state_s.shape[0]
    l_s = ts // nb_s
    tail = CONV_M_K - 1
    tn = 1024
    tm = _pick(seq_p, (1024, 512, 256, 128))
    tpb = seq_p // tm
    assert col0 % tn == 0 and ch0 % tn == 0 and ncols % tn == 0 and l_s >= tail
    jb, cb0 = col0 // tn, ch0 // tn
    nsplit = tn // width
    kern = functools.partial(_proj_conv_kernel, tiles_per_batch=tpb, nb_s=nb_s, l_s=l_s,
                             nsplit=nsplit, width=width)
    return pl.pallas_call(
        kern,
        grid=(ncols // tn, tp // tm),
        in_specs=[
            pl.BlockSpec((tm, d), lambda j, i: (i, 0)),
            pl.BlockSpec((ts, d), lambda j, i: (0, 0)),
            pl.BlockSpec((tn, d), lambda j, i: (j + jb, 0)),
            pl.BlockSpec((CONV_M_K, tn), lambda j, i: (0, j + cb0)),
            pl.BlockSpec((1, tn), lambda j, i: (0, j + cb0)),
            pl.BlockSpec((nb_s, tail, tn), lambda j, i: (0, 0, j + cb0)),
        ],
        out_specs=[
            pl.BlockSpec((nsplit, tm, width), lambda j, i: (j, i, 0)),
            pl.BlockSpec((nsplit, ts, width), lambda j, i: (j, 0, 0)),
            pl.BlockSpec((1, tail, tn), lambda j, i: (i // tpb, 0, j)),
            pl.BlockSpec((nb_s, tail, tn), lambda j, i: (0, 0, j)),
        ],
        out_shape=[
            jax.ShapeDtypeStruct((ncols // width, tp, width), F32),
            jax.ShapeDtypeStruct((ncols // width, ts, width), F32),
            jax.ShapeDtypeStruct((n_batch_p, tail, ncols), F32),
            jax.ShapeDtypeStruct((nb_s, tail, ncols), F32),
        ],
        scratch_shapes=[
            pltpu.VMEM((tn, d), BF16),
            pltpu.VMEM((SUBLANES + tm, tn), F32),
            pltpu.VMEM((SUBLANES + ts, tn), F32),
            pltpu.VMEM((tail, ts, tn), F32),
        ],
        compiler_params=_params(("arbitrary", "arbitrary")),
        name=f"proj_conv_{col0}",
    )(xnp, xns, w, conv_w, conv_b, state_s)


def _ssd_kernel(*refs, q, has_state, nchunks):
    (z_ref, xs_ref, b_ref, c_ref, dt_ref, dtb_ref, alog_ref, dsk_ref, gn_ref, *rest) = refs
    if has_state:
        hprev, *rest = rest
    (y_ref, oh, h_s, acg, rowt) = rest
    c = pl.program_id(1)
    nheads = N_GROUPS * HEADS_PER_GROUP

    @pl.when(c == 0)
    def _init():
        if has_state:
            h_s[...] = hprev[...]
        else:
            h_s[...] = jnp.zeros_like(h_s)

    dt = _softplus(dt_ref[...] + dtb_ref[...])
    da = dt * (-jnp.exp(alog_ref[...]))
    ri = lax.broadcasted_iota(jnp.int32, (q, q), 0)
    ci = lax.broadcasted_iota(jnp.int32, (q, q), 1)
    causal = ri >= ci
    tril = jnp.where(causal, 1.0, 0.0).astype(BF16)
    acum = sum(_dot(tril, p) for p in _split3(da))
    eye = jnp.where(lax.broadcasted_iota(jnp.int32, (nheads, nheads), 0)
                    == lax.broadcasted_iota(jnp.int32, (nheads, nheads), 1), 1.0, 0.0).astype(BF16)
    acum2 = acum * LOG2_E
    rowt[0] = sum(lax.dot_general(eye, p, NT_DIMS, preferred_element_type=F32) for p in _split3(acum2))
    rowt[1] = sum(lax.dot_general(eye, p, NT_DIMS, preferred_element_type=F32) for p in _split3(dt))
    wend = jnp.exp(acum[q - 1:q, :] - acum) * dt
    rowt[2] = sum(lax.dot_general(eye, p, NT_DIMS, preferred_element_type=F32) for p in _split3(wend))
    for g in range(N_GROUPS):
        acg[g] = acum2[:, g * HEADS_PER_GROUP:(g + 1) * HEADS_PER_GROUP]

    lane = lax.broadcasted_iota(jnp.int32, (1, LANES), 1)
    rowi = lax.broadcasted_iota(jnp.int32, (LANES, 1), 0)
    half_w = LANES // 2

    def group_body(g, carry):
        xs = xs_ref[g]
        bb = b_ref[g].astype(BF16)
        ccb16 = c_ref[g].astype(BF16)
        cb_ = lax.dot_general(ccb16, bb, NT_DIMS, preferred_element_type=F32)
        ac8 = acg[g]
        dsk = dsk_ref[g]
        lo_half = lane < half_w
        lo_rows = rowi < half_w
        ys = []
        for pair in range(HEADS_PER_GROUP // 2):
            sl = slice(pair * LANES, (pair + 1) * LANES)
            xp = xs[:, sl]
            hp = h_s[g, pl.ds(pair * LANES, LANES), :]
            ms_, acols, wrows = [], [], []
            for r in (2 * pair, 2 * pair + 1):
                head = g * HEADS_PER_GROUP + r
                acol = jnp.broadcast_to(ac8[:, r:r + 1], (q, LANES))
                arow = rowt[0, pl.ds(head, 1), :]
                drow = rowt[1, pl.ds(head, 1), :]
                decay = jnp.exp2(jnp.where(causal, acol[:, :q] - arow, -jnp.inf))
                ms_.append((cb_ * decay * drow).astype(BF16))
                acols.append(acol)
                wrows.append(jnp.broadcast_to(rowt[2, pl.ds(head, 1), :], (half_w, q)))
            xlo = jnp.where(lo_half, xp, 0.0).astype(BF16)
            xhi = jnp.where(lo_half, 0.0, xp).astype(BF16)
            if q % LANES == 0:
                ydiag = _dot(jnp.concatenate(ms_, axis=1), jnp.concatenate([xlo, xhi], axis=0))
            else:
                ydiag = _dot(ms_[0], xlo) + _dot(ms_[1], xhi)
            ea = jnp.where(lo_half, jnp.exp2(acols[0]), jnp.exp2(acols[1]))
            yoff = ea * lax.dot_general(ccb16, hp.astype(BF16), NT_DIMS, preferred_element_type=F32)
            ys.append(dsk[:, sl] * xp + ydiag + yoff)
            xwt = (xp.T * jnp.concatenate(wrows, axis=0)).astype(BF16)
            dlast = [jnp.broadcast_to(jnp.exp2(a[q - 1:q, :]), (LANES, LANES)) for a in acols]
            h_s[g, pl.ds(pair * LANES, LANES), :] = hp * jnp.where(lo_rows, dlast[0], dlast[1]) + _dot(xwt, bb)
        yg = jnp.concatenate(ys, axis=1)
        hh = yg * jax.nn.silu(z_ref[g])
        ms = jnp.mean(hh * hh, axis=-1, keepdims=True)
        y_ref[g] = (hh * lax.rsqrt(ms + EPS) * gn_ref[g]).astype(BF16)
        return carry

    lax.fori_loop(0, N_GROUPS, group_body, 0, unroll=2)

    @pl.when(c == nchunks - 1)
    def _():
        oh[...] = h_s[...]


def _ssd(z, xc, bc, dt_raw, prm, n_batch, seq, q, h_prev=None):
    nchunks = seq // q
    nheads = N_GROUPS * HEADS_PER_GROUP
    has_state = h_prev is not None
    g8 = N_GROUPS

    def tok(first):
        return lambda b, c: (first, b * nchunks + c, 0)

    def const3(b, c):
        return (0, 0, 0)

    in_specs = [
        pl.BlockSpec((g8, q, GROUP_W), tok(0)),
        pl.BlockSpec((g8, q, GROUP_W), tok(0)),
        pl.BlockSpec((g8, q, D_STATE), tok(0)),
        pl.BlockSpec((g8, q, D_STATE), tok(1)),
        pl.BlockSpec((q, nheads), lambda b, c: (b * nchunks + c, 0)),
        pl.BlockSpec((1, nheads), lambda b, c: (0, 0)),
        pl.BlockSpec((1, nheads), lambda b, c: (0, 0)),
        pl.BlockSpec((g8, 1, GROUP_W), const3),
        pl.BlockSpec((g8, 1, GROUP_W), const3),
    ]
    args = [z, xc, bc, bc, dt_raw, prm["dtb"], prm["alog"], prm["dsk"], prm["gn"]]

    def per_batch(shape):
        return pl.BlockSpec((None,) + shape, lambda b, c: (b,) + (0,) * len(shape))

    if has_state:
        in_specs.append(per_batch((g8, GROUP_W, D_STATE)))
        args.append(h_prev)
    t = n_batch * seq
    out_specs = [
        pl.BlockSpec((g8, q, GROUP_W), lambda b, c: (0, b * nchunks + c, 0)),
        per_batch((g8, GROUP_W, D_STATE)),
    ]
    out_shape = [
        jax.ShapeDtypeStruct((g8, t, GROUP_W), BF16),
        jax.ShapeDtypeStruct((n_batch, g8, GROUP_W, D_STATE), F32),
    ]
    scratch = [
        pltpu.VMEM((g8, GROUP_W, D_STATE), F32),
        pltpu.VMEM((g8, q, HEADS_PER_GROUP), F32),
        pltpu.VMEM((3, nheads, q), F32),
    ]
    kern = functools.partial(_ssd_kernel, q=q, has_state=has_state, nchunks=nchunks)
    return pl.pallas_call(
        kern,
        grid=(n_batch, nchunks),
        in_specs=in_specs,
        out_specs=out_specs,
        out_shape=out_shape,
        scratch_shapes=scratch,
        compiler_params=_params(("arbitrary", "arbitrary")),
        name="ssd_state" if has_state else "ssd",
    )(*args)


def _merge_kernel(ap_ref, as_ref, yp_ref, ys_ref, gap_ref, gmp_ref, gas_ref, gms_ref, wa_ref, wm_ref,
                  op_ref, os_ref, wab, wmb):
    def emit(a_ref, y_ref, ga_ref, gm_ref, o_ref):
        oa = _dot(a_ref[...], wab[...])
        y = jnp.concatenate([y_ref[g] for g in range(N_GROUPS)], axis=1)
        om = _dot(y, wmb[...])
        o_ref[...] = (jax.nn.sigmoid(ga_ref[...]) * oa + jax.nn.sigmoid(gm_ref[...]) * om).astype(BF16)

    @pl.when(pl.program_id(1) == 0)
    def _():
        wab[...] = wa_ref[...].astype(BF16)
        wmb[...] = wm_ref[...].astype(BF16)
        emit(as_ref, ys_ref, gas_ref, gms_ref, os_ref)

    emit(ap_ref, yp_ref, gap_ref, gmp_ref, op_ref)


def _merge(abv_p, abv_s, y_p, y_s, g_p, g_s, w_a_out, w_m_out):
    tp, dc = abv_p.shape
    ts = abv_s.shape[0]
    dm = w_a_out.shape[1]
    tn = 512
    tm = _pick(tp, (512, 256, 128))
    ncol = dm // tn
    return pl.pallas_call(
        _merge_kernel,
        grid=(ncol, tp // tm),
        in_specs=[
            pl.BlockSpec((tm, dc), lambda j, i: (i, 0)),
            pl.BlockSpec((ts, dc), lambda j, i: (0, 0)),
            pl.BlockSpec((N_GROUPS, tm, GROUP_W), lambda j, i: (0, i, 0)),
            pl.BlockSpec((N_GROUPS, ts, GROUP_W), lambda j, i: (0, 0, 0)),
            pl.BlockSpec((tm, tn), lambda j, i: (i, j)),
            pl.BlockSpec((tm, tn), lambda j, i: (i, j + ncol)),
            pl.BlockSpec((ts, tn), lambda j, i: (0, j)),
            pl.BlockSpec((ts, tn), lambda j, i: (0, j + ncol)),
            pl.BlockSpec((dc, tn), lambda j, i: (0, j)),
            pl.BlockSpec((N_GROUPS * GROUP_W, tn), lambda j, i: (0, j)),
        ],
        out_specs=[
            pl.BlockSpec((tm, tn), lambda j, i: (i, j)),
            pl.BlockSpec((ts, tn), lambda j, i: (0, j)),
        ],
        out_shape=[jax.ShapeDtypeStruct((tp, dm), BF16), jax.ShapeDtypeStruct((ts, dm), BF16)],
        scratch_shapes=[pltpu.VMEM((dc, tn), BF16), pltpu.VMEM((N_GROUPS * GROUP_W, tn), BF16)],
        compiler_params=_params(("arbitrary", "arbitrary")),
        name="merge",
    )(abv_p, abv_s, y_p, y_s, g_p, g_p, g_s, g_s, w_a_out, w_m_out)


def _res_kernel(ap_ref, as_ref, rp_ref, rs_ref, w_ref, g_ref, *rest, n_prompt_tiles, merged_out, ts):
    if merged_out:
        wr_ref, br_ref, h_ref, hn_ref, ri_ref, rw_ref, wbf, wrb = rest
    else:
        hp_ref, hs_ref, hnp_ref, hns_ref, wbf = rest
    i = pl.program_id(0)

    @pl.when(i == 0)
    def _():
        wbf[...] = w_ref[...].astype(BF16)
        if merged_out:
            wrb[...] = wr_ref[...].astype(BF16)

    def emit(a_ref, r_ref, store_h, store_hn):
        h = r_ref[...] + _dot(a_ref[...], wbf[...])
        store_h(h)
        store_hn(_rms(h, g_ref[...]))

    if merged_out:
        def stores(rows):
            def sh(h):
                h_ref[rows, :] = h

            def shn(hn):
                hn_ref[rows, :] = hn
                ri_ref[rows, :], rw_ref[rows, :] = _route(hn.astype(BF16), wrb[...], br_ref[...])

            return sh, shn

        @pl.when(i < n_prompt_tiles)
        def _():
            emit(ap_ref, rp_ref, *stores(pl.ds(0, h_ref.shape[0])))

        @pl.when(i == n_prompt_tiles)
        def _():
            emit(as_ref, rs_ref, *stores(pl.ds(0, ts)))
    else:
        def shp(h):
            hp_ref[...] = h

        def shnp(hn):
            hnp_ref[...] = hn.astype(hnp_ref.dtype)

        emit(ap_ref, rp_ref, shp, shnp)

        @pl.when(i == 0)
        def _():
            def shs(h):
                hs_ref[...] = h

            def shns(hn):
                hns_ref[...] = hn.astype(hns_ref.dtype)

            emit(as_ref, rs_ref, shs, shns)


def _res(a_p, a_s, r_p, r_s, w, g, merged_out, hn_dtype, router=None):
    tp, d = a_p.shape
    ts = a_s.shape[0]
    tm = 256 if merged_out else _pick(tp, (512, 256))
    assert tp % tm == 0 and ts <= tm
    npt = tp // tm
    last = npt - 1
    in_specs = [
        pl.BlockSpec((tm, d), lambda i: (jnp.minimum(i, last), 0)),
        pl.BlockSpec((ts, d), lambda i: (0, 0)),
        pl.BlockSpec((tm, d), lambda i: (jnp.minimum(i, last), 0)),
        pl.BlockSpec((ts, d), lambda i: (0, 0)),
        pl.BlockSpec((d, d), lambda i: (0, 0), pipeline_mode=pl.Buffered(1)),
        pl.BlockSpec((1, d), lambda i: (0, 0)),
    ]
    args = [a_p, a_s, r_p, r_s, w, g]
    scratch = [pltpu.VMEM((d, d), BF16)]
    if merged_out:
        grid = (npt + 1,)
        in_specs += [pl.BlockSpec((d, LANES), lambda i: (0, 0)), pl.BlockSpec((1, LANES), lambda i: (0, 0))]
        args += list(router)
        scratch.append(pltpu.VMEM((d, LANES), BF16))
        out_specs = [pl.BlockSpec((tm, d), lambda i: (i, 0)), pl.BlockSpec((tm, d), lambda i: (i, 0)),
                     pl.BlockSpec((tm, LANES), lambda i: (i, 0)), pl.BlockSpec((tm, LANES), lambda i: (i, 0))]
        out_shape = [jax.ShapeDtypeStruct((tp + ts, d), F32), jax.ShapeDtypeStruct((tp + ts, d), hn_dtype),
                     jax.ShapeDtypeStruct((tp + ts, LANES), jnp.int32),
                     jax.ShapeDtypeStruct((tp + ts, LANES), F32)]
    else:
        grid = (npt,)
        out_specs = [pl.BlockSpec((tm, d), lambda i: (i, 0)), pl.BlockSpec((ts, d), lambda i: (0, 0)),
                     pl.BlockSpec((tm, d), lambda i: (i, 0)), pl.BlockSpec((ts, d), lambda i: (0, 0))]
        out_shape = [jax.ShapeDtypeStruct((tp, d), F32), jax.ShapeDtypeStruct((ts, d), F32),
                     jax.ShapeDtypeStruct((tp, d), hn_dtype), jax.ShapeDtypeStruct((ts, d), hn_dtype)]
    kern = functools.partial(_res_kernel, n_prompt_tiles=npt, merged_out=merged_out, ts=ts)
    return pl.pallas_call(
        kern,
        grid=grid,
        in_specs=in_specs,
        out_specs=out_specs,
        out_shape=out_shape,
        scratch_shapes=scratch,
        compiler_params=_params(("arbitrary",)),
        name="res_merged" if merged_out else "res",
    )(*args)


def _kv_kernel(m_ref, g_ref, wk_ref, wv_ref, k_hbm, v_hbm, mn, kvb, sem, *, n_batch):
    j = pl.program_id(0)

    @pl.when(j == 0)
    def _():
        mn[...] = _rms(m_ref[...], g_ref[...]).astype(BF16)

    kvb[0] = _dot(mn[...], wk_ref[...].astype(BF16))
    kvb[1] = _dot(mn[...], wv_ref[...].astype(BF16))
    nm = kvb.shape[1] // n_batch
    for h in range(N_XHEADS):
        @pl.when(j == h)
        def _(h=h):
            copies = [pltpu.make_async_copy(kvb.at[t, pl.ds(b * nm, nm)], dst.at[b, :, h, :], sem.at[t, b])
                      for t, dst in enumerate((k_hbm, v_hbm)) for b in range(n_batch)]
            for c in copies:
                c.start()
            for c in copies:
                c.wait()


def _memory_kv(mem2d, g, w_k, w_v, n_batch):
    m, d = mem2d.shape
    dh = d // N_XHEADS
    out_sds = jax.ShapeDtypeStruct((n_batch, m // n_batch, N_XHEADS, dh), F32)
    kern = functools.partial(_kv_kernel, n_batch=n_batch)
    return pl.pallas_call(
        kern,
        grid=(N_XHEADS,),
        in_specs=[
            pl.BlockSpec((m, d), lambda j: (0, 0)),
            pl.BlockSpec((1, d), lambda j: (0, 0)),
            pl.BlockSpec((d, dh), lambda j: (0, j)),
            pl.BlockSpec((d, dh), lambda j: (0, j)),
        ],
        out_specs=[pl.BlockSpec(memory_space=pl.ANY), pl.BlockSpec(memory_space=pl.ANY)],
        out_shape=[out_sds, out_sds],
        scratch_shapes=[pltpu.VMEM((m, d), BF16), pltpu.VMEM((2, m, dh), F32),
                        pltpu.SemaphoreType.DMA((2, n_batch))],
        compiler_params=_params(("arbitrary",)),
        name="memory_kv",
    )(mem2d, g, w_k, w_v)


def _attn_kernel(hn_ref, k_ref, v_ref, wq_ref, o_ref, wqb, kb, vb, kvf, sem, *q_all):
    b = pl.program_id(0)
    i = pl.program_id(1)
    tm = o_ref.shape[0]

    @pl.when((b == 0) & (i == 0))
    def _():
        wqb[...] = wq_ref[...].astype(BF16)
        if q_all:
            q_all[0][...] = _dot(hn_ref[...], wqb[...])

    d = wqb.shape[1]
    dh = d // N_XHEADS

    @pl.when(i == 0)
    def _():
        def copies(bb):
            slot = bb % 2
            return [pltpu.make_async_copy(src.at[bb, :, h, :], kvf.at[slot, t, h], sem.at[slot, t, h])
                    for t, src in enumerate((k_ref, v_ref)) for h in range(N_XHEADS)]

        @pl.when(b == 0)
        def _():
            for c in copies(b):
                c.start()

        for c in copies(b):
            c.wait()

        @pl.when(b + 1 < pl.num_programs(0))
        def _():
            for c in copies(b + 1):
                c.start()

        for h in range(N_XHEADS):
            kb[:, h * dh:(h + 1) * dh] = kvf[b % 2, 0, h].astype(BF16)
            vb[:, h * dh:(h + 1) * dh] = kvf[b % 2, 1, h].astype(BF16)

    if q_all:
        row0 = pl.multiple_of((b * pl.num_programs(1) + i) * tm, SUBLANES)
        q = q_all[0][pl.ds(row0, tm), :]
    else:
        q = _dot(hn_ref[...], wqb[...])
    outs = []
    for h in range(N_XHEADS):
        sl = slice(h * dh, (h + 1) * dh)
        s = lax.dot_general(q[:, sl].astype(BF16), kb[:, sl], NT_DIMS, preferred_element_type=F32)
        s = s * (dh ** -0.5)
        e = jnp.exp(s - jnp.max(s, axis=-1, keepdims=True))
        p = e / jnp.sum(e, axis=-1, keepdims=True)
        outs.append(_dot(p.astype(BF16), vb[:, sl]))
    o_ref[...] = jnp.concatenate(outs, axis=1).astype(BF16)


def _attn(hn, k, v, w_q, n_batch, seq):
    t, d = hn.shape
    nm, nh, dh = k.shape[1:]
    tm = _pick(seq, (512, 256, 128, 64, 32, 16))
    tpb = seq // tm
    kv_spec = pl.BlockSpec(memory_space=pl.ANY)
    scratch = [pltpu.VMEM((d, d), BF16), pltpu.VMEM((nm, d), BF16), pltpu.VMEM((nm, d), BF16),
               pltpu.VMEM((2, 2, nh, nm, dh), F32), pltpu.SemaphoreType.DMA((2, 2, nh))]
    if tm < 128 and t <= 512:
        hn_spec = pl.BlockSpec((t, d), lambda b, i: (0, 0))
        scratch.append(pltpu.VMEM((t, d), F32))
    else:
        hn_spec = pl.BlockSpec((tm, d), lambda b, i: (b * tpb + i, 0))
    return pl.pallas_call(
        _attn_kernel,
        grid=(n_batch, tpb),
        in_specs=[
            hn_spec,
            kv_spec,
            kv_spec,
            pl.BlockSpec((d, d), lambda b, i: (0, 0), pipeline_mode=pl.Buffered(1)),
        ],
        out_specs=pl.BlockSpec((tm, d), lambda b, i: (b * tpb + i, 0)),
        out_shape=jax.ShapeDtypeStruct((t, d), BF16),
        scratch_shapes=scratch,
        compiler_params=_params(("arbitrary", "arbitrary")),
        name=f"attn_{seq}",
    )(hn, k, v, w_q)


def _route(x, w, b):
    logits = _dot(x, w) + b
    lane_i = lax.broadcasted_iota(jnp.int32, logits.shape, 1)
    lane = lane_i.astype(F32)
    ninf = -jnp.inf
    big = float(LANES)
    is_g = lane < N_EXPERT_GROUPS
    gl = jnp.where(is_g, logits, ninf)
    gmax = jnp.max(gl, axis=-1, keepdims=True)
    gsel = jnp.min(jnp.where(gl == gmax, lane, big), axis=-1, keepdims=True)
    pg = 1.0 / jnp.sum(jnp.where(is_g, jnp.exp(gl - gmax), 0.0), axis=-1, keepdims=True)
    lo = N_EXPERT_GROUPS + EXPERTS_PER_GROUP * gsel
    el = jnp.where(lane >= lo, jnp.where(lane < lo + EXPERTS_PER_GROUP, logits, ninf), ninf)
    m1 = jnp.max(el, axis=-1, keepdims=True)
    i1 = jnp.min(jnp.where(el == m1, lane, big), axis=-1, keepdims=True)
    el2 = jnp.where(lane == i1, ninf, el)
    m2 = jnp.max(el2, axis=-1, keepdims=True)
    i2 = jnp.min(jnp.where(el2 == m2, lane, big), axis=-1, keepdims=True)
    e = jnp.exp(m2 - m1)
    w1 = pg / (1.0 + e)
    w2 = pg * e / (1.0 + e)
    ids = jnp.where(lane_i == 0, i1 - N_EXPERT_GROUPS,
                    jnp.where(lane_i == 1, i2 - N_EXPERT_GROUPS, 0.0)).astype(jnp.int32)
    return ids, jnp.where(lane_i == 0, w1, jnp.where(lane_i == 1, w2, 0.0))


def _row_copy(src, dst, s, d, sem):
    return pltpu.make_async_copy(src.at[pl.ds(s, 1)], dst.at[pl.ds(d, 1)], sem)


def _dispatch_kernel(dest_ref, pad0_ref, padn_ref, nsub_ref, x_ref, o_ref, zrow, zblk, sem, zsem, csem, bsem, *,
                     chunk, n_blocks):
    base = pl.program_id(0) * chunk

    @pl.when(pl.program_id(0) == 0)
    def _():
        zrow[...] = jnp.zeros_like(zrow)
        zblk[...] = jnp.zeros_like(zblk)

        def pad_plan(e):
            p0 = pad0_ref[e]
            head = jnp.minimum(padn_ref[e], (SUBLANES - p0 % SUBLANES) % SUBLANES)
            return p0, head, (padn_ref[e] - head) // SUBLANES

        def group_copy(row):
            return pltpu.make_async_copy(zrow, o_ref.at[pl.ds(pl.multiple_of(row, SUBLANES), SUBLANES)], csem)

        def pad_start(e, carry):
            p0, head, ngroups = pad_plan(e)

            def start_row(r, c):
                _row_copy(zrow, o_ref, 0, p0 + r, zsem).start()
                return c

            def start_group(j, c):
                group_copy(p0 + head + j * SUBLANES).start()
                return c

            lax.fori_loop(0, head, start_row, 0)
            return lax.fori_loop(0, ngroups, start_group, carry)

        def pad_wait(e, carry):
            _, head, ngroups = pad_plan(e)

            def wait_row(r, c):
                _row_copy(zrow, o_ref, 0, 0, zsem).wait()
                return c

            def wait_group(j, c):
                group_copy(0).wait()
                return c

            lax.fori_loop(0, head, wait_row, 0)
            return lax.fori_loop(0, ngroups, wait_group, carry)

        lax.fori_loop(0, N_EXPERTS, pad_start, 0)

        def blk_copy(b):
            return pltpu.make_async_copy(zblk, o_ref.at[pl.ds(pl.multiple_of(b * EXPERT_SUB, EXPERT_SUB),
                                                             EXPERT_SUB)], bsem)

        def tail_start(b, c):
            blk_copy(b).start()
            return c

        def tail_wait(b, c):
            blk_copy(b).wait()
            return c

        lax.fori_loop(nsub_ref[0], n_blocks, tail_start, 0)
        lax.fori_loop(0, N_EXPERTS, pad_wait, 0)
        lax.fori_loop(nsub_ref[0], n_blocks, tail_wait, 0)

    def issue(r, carry):
        t = base + r
        _row_copy(x_ref, o_ref, r, dest_ref[2 * t], sem).start()
        _row_copy(x_ref, o_ref, r, dest_ref[2 * t + 1], sem).start()
        return carry

    lax.fori_loop(0, chunk, issue, 0, unroll=8)

    for _ in range(2):
        pltpu.make_async_copy(x_ref, o_ref.at[pl.ds(0, chunk)], sem).wait()


def _dispatch(tables, hn_all, n_blocks):
    dest, pad0, padn, nsub = tables
    t, d = hn_all.shape
    chunk = _pick(t, (640, 512, 384, 256, 128))
    kern = functools.partial(_dispatch_kernel, chunk=chunk, n_blocks=n_blocks)
    return pl.pallas_call(
        kern,
        grid_spec=pltpu.PrefetchScalarGridSpec(
            num_scalar_prefetch=4,
            grid=(t // chunk,),
            in_specs=[pl.BlockSpec((chunk, d), lambda i, *_: (i, 0))],
            out_specs=pl.BlockSpec(memory_space=pl.ANY),
            scratch_shapes=[pltpu.VMEM((SUBLANES, d), F32), pltpu.VMEM((EXPERT_SUB, d), F32),
                            pltpu.SemaphoreType.DMA(()), pltpu.SemaphoreType.DMA(()),
                            pltpu.SemaphoreType.DMA(()), pltpu.SemaphoreType.DMA(())],
        ),
        out_shape=jax.ShapeDtypeStruct((n_blocks * EXPERT_SUB, d), F32),
        compiler_params=_params(("arbitrary",)),
        name="dispatch",
    )(dest, pad0, padn, nsub, hn_all)


def _expert_kernel(bstart_ref, wg_ref, wu_ref, wd_ref, x_hbm, y_hbm, gcache, ucache, dcache, xbuf, ybuf, xb,
                   xsem, ysem):
    e = pl.program_id(0)
    k = pl.program_id(1)
    nk = gcache.shape[1]
    total = bstart_ref[N_EXPERTS]

    def x_copy(b, slot):
        rows = pl.ds(pl.multiple_of(b * EXPERT_SUB, EXPERT_SUB), EXPERT_SUB)
        return pltpu.make_async_copy(x_hbm.at[rows], xbuf.at[slot], xsem.at[slot])

    def y_copy(b, slot):
        rows = pl.ds(pl.multiple_of(b * EXPERT_SUB, EXPERT_SUB), EXPERT_SUB)
        return pltpu.make_async_copy(ybuf.at[slot], y_hbm.at[rows], ysem.at[slot])

    @pl.when(e < N_EXPERTS)
    def _():
        slot = e % 2
        gcache[slot, k] = wg_ref[...].astype(BF16)
        ucache[slot, k] = wu_ref[...].astype(BF16)
        dcache[slot, k] = wd_ref[...].astype(BF16)

    @pl.when((e == 0) & (k == 0) & (total > 0))
    def _():
        x_copy(0, 0).start()

    @pl.when(e >= 1)
    def _():
        owner = e - 1
        wslot = owner % 2
        b0 = bstart_ref[owner]
        n = bstart_ref[owner + 1] - b0

        cpu = 2
        upb = nk // cpu

        def unit(u, carry):
            b = b0 + u // upb
            part = u % upb
            slot = b % 2

            def swiglu(x, p):
                kks = [p * cpu + c for c in range(cpu)]
                wg = jnp.concatenate([gcache[wslot, kk] for kk in kks], axis=1)
                wu = jnp.concatenate([ucache[wslot, kk] for kk in kks], axis=1)
                wd = jnp.concatenate([dcache[wslot, kk] for kk in kks], axis=0)
                hid = (jax.nn.silu(_dot(x, wg)) * _dot(x, wu)).astype(BF16)
                return _dot(hid, wd)

            @pl.when(part == 0)
            def _():
                x_copy(b, slot).wait()

                @pl.when(b + 1 < total)
                def _():
                    x_copy(b + 1, 1 - slot).start()

                x = xbuf[slot].astype(BF16)
                xb[...] = x

                @pl.when(b >= 2)
                def _():
                    y_copy(b - 2, slot).wait()

                ybuf[slot] = swiglu(x, 0)

            for p in range(1, upb):
                @pl.when(part == p)
                def _(p=p):
                    ybuf[slot] += swiglu(xb[...], p)
                    if p == upb - 1:
                        y_copy(b, slot).start()

            return carry

        lax.fori_loop((n * upb * k) // nk, (n * upb * (k + 1)) // nk, unit, 0)

    @pl.when((e == N_EXPERTS) & (k == nk - 1))
    def _():
        @pl.when(total >= 2)
        def _():
            y_copy(total - 2, total % 2).wait()

        @pl.when(total >= 1)
        def _():
            y_copy(total - 1, (total - 1) % 2).wait()


def _experts(bstart, x_sorted, w_gate, w_up, w_down, kchunk):
    nrows, d = x_sorted.shape
    de = w_gate.shape[2]
    nk = de // kchunk
    last = N_EXPERTS - 1

    def widx(e, k):
        return jnp.minimum(e, last), jnp.where(e <= last, k, nk - 1)

    def in_idx(e, k, b):
        ee, kk = widx(e, k)
        return (ee, 0, kk)

    def down_idx(e, k, b):
        ee, kk = widx(e, k)
        return (ee, kk, 0)

    return pl.pallas_call(
        _expert_kernel,
        grid_spec=pltpu.PrefetchScalarGridSpec(
            num_scalar_prefetch=1,
            grid=(N_EXPERTS + 1, nk),
            in_specs=[
                pl.BlockSpec((None, d, kchunk), in_idx),
                pl.BlockSpec((None, d, kchunk), in_idx),
                pl.BlockSpec((None, kchunk, d), down_idx),
                pl.BlockSpec(memory_space=pl.ANY),
            ],
            out_specs=pl.BlockSpec(memory_space=pl.ANY),
            scratch_shapes=[
                pltpu.VMEM((2, nk, d, kchunk), BF16),
                pltpu.VMEM((2, nk, d, kchunk), BF16),
                pltpu.VMEM((2, nk, kchunk, d), BF16),
                pltpu.VMEM((2, EXPERT_SUB, d), F32),
                pltpu.VMEM((2, EXPERT_SUB, d), F32),
                pltpu.VMEM((EXPERT_SUB, d), BF16),
                pltpu.SemaphoreType.DMA((2,)),
                pltpu.SemaphoreType.DMA((2,)),
            ],
        ),
        out_shape=jax.ShapeDtypeStruct((nrows, d), F32),
        input_output_aliases={4: 0},
        compiler_params=_params(("arbitrary", "arbitrary")),
        name="experts",
    )(bstart, w_gate, w_up, w_down, x_sorted)


def _final_kernel(dest_ref, h_ref, rw_ref, ys_ref, g_ref, yp_ref, yss_ref, ya, yb, sem, *,
                  tm, n_prompt_tiles, ts):
    i = pl.program_id(0)

    def gather(tile, n):
        slot = tile % 2

        def issue(r, carry):
            t = tile * tm + r
            _row_copy(ys_ref, ya.at[slot], dest_ref[2 * t], r, sem.at[slot]).start()
            _row_copy(ys_ref, yb.at[slot], dest_ref[2 * t + 1], r, sem.at[slot]).start()
            return carry

        lax.fori_loop(0, n, issue, 0, unroll=8)

    def emit(n, o_ref):
        slot = i % 2
        for buf in (ya, yb):
            pltpu.make_async_copy(ys_ref.at[pl.ds(0, n)], buf.at[slot, pl.ds(0, n)], sem.at[slot]).wait()
        w = rw_ref[pl.ds(0, n), :]
        h = (h_ref[pl.ds(0, n), :] + w[:, 0:1] * ya[slot, pl.ds(0, n), :]
             + w[:, 1:2] * yb[slot, pl.ds(0, n), :])
        o_ref[...] = _rms(h, g_ref[...])

    @pl.when(i == 0)
    def _():
        gather(i, tm)

    @pl.when(i + 1 < n_prompt_tiles)
    def _():
        gather(i + 1, tm)

    @pl.when(i + 1 == n_prompt_tiles)
    def _():
        gather(i + 1, ts)

    @pl.when(i < n_prompt_tiles)
    def _():
        emit(tm, yp_ref)

    @pl.when(i == n_prompt_tiles)
    def _():
        emit(ts, yss_ref)


def _final(dest, h_all, rw, y_sorted, g, tp, ts):
    t, d = h_all.shape
    tm = 256
    assert tp % tm == 0 and ts <= tm and t == tp + ts
    npt = tp // tm
    kern = functools.partial(_final_kernel, tm=tm, n_prompt_tiles=npt, ts=ts)
    return pl.pallas_call(
        kern,
        grid_spec=pltpu.PrefetchScalarGridSpec(
            num_scalar_prefetch=1,
            grid=(npt + 1,),
            in_specs=[
                pl.BlockSpec((tm, d), lambda i, dref: (i, 0)),
                pl.BlockSpec((tm, LANES), lambda i, dref: (i, 0)),
                pl.BlockSpec(memory_space=pl.ANY),
                pl.BlockSpec((1, d), lambda i, dref: (0, 0)),
            ],
            out_specs=[
                pl.BlockSpec((tm, d), lambda i, dref: (jnp.minimum(i, npt - 1), 0)),
                pl.BlockSpec((ts, d), lambda i, dref: (0, 0)),
            ],
            scratch_shapes=[pltpu.VMEM((2, tm, d), F32), pltpu.VMEM((2, tm, d), F32),
                            pltpu.SemaphoreType.DMA((2,))],
        ),
        out_shape=[jax.ShapeDtypeStruct((tp, d), F32), jax.ShapeDtypeStruct((ts, d), F32)],
        compiler_params=_params(("arbitrary",)),
        name="final",
    )(dest, h_all, rw, y_sorted, g)


def _routing_tables(eid):
    i32 = jnp.int32
    onehot = (eid[:, None] == jnp.arange(N_EXPERTS, dtype=i32)[None, :]).astype(i32)
    csum = jnp.cumsum(onehot, axis=0)
    counts = csum[-1]
    nsub = (counts + EXPERT_SUB - 1) // EXPERT_SUB
    bend = jnp.cumsum(nsub)
    bstart = bend - nsub
    seg = bstart * EXPERT_SUB
    dest = jnp.sum(onehot * (csum - 1 + seg[None, :]), axis=1).astype(i32)
    pad0 = (seg + counts).astype(i32)
    padn = (nsub * EXPERT_SUB - counts).astype(i32)
    bstart_all = jnp.concatenate([bstart, bend[-1:]]).astype(i32)
    return (dest, pad0, padn, bend[-1:].astype(i32)), bstart_all


def _group_major(a, width):
    lead = a.shape[:-1]
    g = a.shape[-1] // width
    return jnp.moveaxis(a.reshape(lead + (g, width)), -2, 0)


def kernel(x_prompt, x_sample, cache_mem_k, cache_mem_v, state_conv_a, state_conv_m, state_ssm, mem_prompt,
           norm_mix, w_in, conv_a_w, w_a_out, conv_m_w, conv_m_b, dt_bias, a_log, d_skip, ssm_norm,
           w_m_out, w_o, norm_cross, norm_mem, w_q, w_k, w_v, w_co, norm_ffn, w_rg, b_rg, w_re, b_re,
           w_gate, w_up, w_down, norm_final):
    depth = w_in.shape[0]
    assert depth == 1, "single-layer step"
    nbp, seq_p, d = x_prompt.shape
    nbs, seq_s, _ = x_sample.shape
    tp, ts = nbp * seq_p, nbs * seq_s
    n_mem = mem_prompt.shape[1]
    d_conv = conv_a_w.shape[2]
    d_inner = w_m_out.shape[1]
    nheads = dt_bias.shape[1]
    bc_w = N_GROUPS * D_STATE
    l = 0

    xp = x_prompt.reshape(tp, d)
    xs = x_sample.reshape(ts, d)
    w_in_t = jnp.swapaxes(w_in[l], 0, 1)
    col_z = 3 * d_conv
    col_x = col_z + d_inner
    col_b = col_x + d_inner
    col_dt = col_b + 2 * bc_w
    col_g = col_dt + nheads
    w_dt = w_in_t[col_dt:col_g]

    mk4, mv4 = _memory_kv(mem_prompt.reshape(nbp * n_mem, d), norm_mem[l][None], w_k[l], w_v[l], nbp)

    xnp, xns, dt_p, dt_s = _norm_dt(xp, xs, norm_mix[l][None], w_dt)
    abv_p, abv_s, ca_p, ca_s = _proj_a(xnp, xns, w_in_t, conv_a_w[l], state_conv_a[l], nbp, seq_p, d_conv)
    z_p, z_s = _proj_raw(xnp, xns, w_in_t, col_z, d_inner, GROUP_W)
    cw, cbias, scm = conv_m_w[l], conv_m_b[l][None], state_conv_m[l]
    xc_p, xc_s, cmx_p, cmx_s = _proj_conv(xnp, xns, w_in_t, cw, cbias, scm, col_x, 0, d_inner, GROUP_W,
                                           nbp, seq_p)
    bc_p, bc_s, cmb_p, cmb_s = _proj_conv(xnp, xns, w_in_t, cw, cbias, scm, col_b, d_inner, 2 * bc_w, D_STATE,
                                           nbp, seq_p)
    g_p, g_s = _proj_raw(xnp, xns, w_in_t, col_g, 2 * d, 0)

    prm = {
        "dtb": dt_bias[l][None],
        "alog": a_log[l][None],
        "dsk": _group_major(jnp.repeat(d_skip[l], HEAD_DIM)[None], GROUP_W),
        "gn": _group_major(ssm_norm[l][None], GROUP_W),
    }
    q_p = _pick(seq_p, (128, 64, 32, 16, 8))
    y_p, h_p = _ssd(z_p, xc_p, bc_p, dt_p, prm, nbp, seq_p, q_p)
    y_s, h_s = _ssd(z_s, xc_s, bc_s, dt_s, prm, nbs, seq_s, seq_s,
                    h_prev=state_ssm[l].reshape(nbs, N_GROUPS, GROUP_W, D_STATE))

    merged_p, merged_s = _merge(abv_p, abv_s, y_p, y_s, g_p, g_s, w_a_out[l],
                                w_m_out[l])
    h1_p, h1_s, hn1_p, hn1_s = _res(merged_p, merged_s, xp, xs, w_o[l], norm_cross[l][None], False, BF16)

    att_p = _attn(hn1_p, mk4, mv4, w_q[l], nbp, seq_p)
    att_s = _attn(hn1_s, cache_mem_k[l], cache_mem_v[l], w_q[l], nbs, seq_s)
    npad = LANES - N_EXPERT_GROUPS - N_EXPERTS
    w_r = jnp.concatenate([w_rg[l], w_re[l], jnp.zeros((d, npad), F32)], axis=1)
    b_r = jnp.concatenate([b_rg[l], b_re[l], jnp.zeros((npad,), F32)])[None]
    h2_all, hn2_all, ri, rw = _res(att_p, att_s, h1_p, h1_s, w_co[l], norm_ffn[l][None], True, F32,
                                   router=(w_r, b_r))

    tables, bstart = _routing_tables(ri[:, :2].reshape(-1))
    n_blocks = -(-2 * (tp + ts) // EXPERT_SUB) + N_EXPERTS
    x_sorted = _dispatch(tables, hn2_all, n_blocks)
    y_sorted = _experts(bstart, x_sorted, w_gate[l], w_up[l], w_down[l], 256)
    y_prompt, y_sample = _final(tables[0], h2_all, rw, y_sorted, norm_final[None], tp, ts)

    xh = d // N_XHEADS
    return (
        y_prompt.reshape(nbp, seq_p, d),
        y_sample.reshape(nbs, seq_s, d),
        mk4[None],
        mv4[None],
        ca_p[None],
        jnp.concatenate([cmx_p, cmb_p], axis=-1)[None],
        h_p.reshape(1, nbp, nheads, HEAD_DIM, D_STATE),
        ca_s[None],
        jnp.concatenate([cmx_s, cmb_s], axis=-1)[None],
        h_s.reshape(1, nbs, nheads, HEAD_DIM, D_STATE),
    )
```

```python
import functools

import jax
import jax.numpy as jnp
from jax import lax
from jax.experimental import pallas as pl
from jax.experimental.pallas import tpu as pltpu

F32 = jnp.float32
BF16 = jnp.bfloat16
EPS = 1e-6
LOG2_E = 1.4426950408889634

V7X_VMEM_BYTES = 64 * 1024 * 1024
VMEM_LIMIT = V7X_VMEM_BYTES - 8 * 1024 * 1024
LANES = 128
SUBLANES = 8

N_GROUPS = 8
HEADS_PER_GROUP = 8
HEAD_DIM = 64
D_STATE = 128
GROUP_W = HEADS_PER_GROUP * HEAD_DIM
N_XHEADS = 4
N_EXPERTS = 32
N_EXPERT_GROUPS = 4
EXPERTS_PER_GROUP = 8
EXPERT_SUB = 256
CONV_A_K = 3
CONV_M_K = 4

NT_DIMS = (((1,), (1,)), ((), ()))
TN_DIMS = (((0,), (0,)), ((), ()))


def _params(sem):
    return pltpu.CompilerParams(dimension_semantics=sem, vmem_limit_bytes=VMEM_LIMIT)


def _pick(n, cands):
    for c in cands:
        if n % c == 0:
            return c
    raise ValueError(f"no tile for {n} in {cands}")


def _dot(a, b):
    return jnp.dot(a, b, preferred_element_type=F32)


def _dot_nt(a, b):
    return lax.dot_general(a, b, NT_DIMS, preferred_element_type=F32)


def _rms(x, g):
    return x * lax.rsqrt(jnp.mean(x * x, axis=-1, keepdims=True) + EPS) * g


def _split3(x):
    hi = x.astype(BF16)
    r = x - hi.astype(F32)
    mid = r.astype(BF16)
    lo = (r - mid.astype(F32)).astype(BF16)
    return hi, mid, lo


def _softplus(x):
    return jnp.maximum(x, 0.0) + jnp.log1p(jnp.exp(-jnp.abs(x)))


def _norm_dt_kernel(xp_ref, xs_ref, g_ref, wdt_ref, xnp_ref, xns_ref, dtp_ref, dts_ref):
    wdt = wdt_ref[...].astype(BF16)

    def one(x_ref, xn_ref, dt_ref):
        xn = _rms(x_ref[...], g_ref[...]).astype(BF16)
        xn_ref[...] = xn
        dt_ref[...] = _dot_nt(xn, wdt)

    one(xp_ref, xnp_ref, dtp_ref)

    @pl.when(pl.program_id(0) == 0)
    def _():
        one(xs_ref, xns_ref, dts_ref)


def _norm_dt(xp, xs, g, wdt):
    tp, d = xp.shape
    ts = xs.shape[0]
    nh = wdt.shape[0]
    tm = _pick(tp, (512, 256, 128))
    return pl.pallas_call(
        _norm_dt_kernel,
        grid=(tp // tm,),
        in_specs=[
            pl.BlockSpec((tm, d), lambda i: (i, 0)),
            pl.BlockSpec((ts, d), lambda i: (0, 0)),
            pl.BlockSpec((1, d), lambda i: (0, 0)),
            pl.BlockSpec((nh, d), lambda i: (0, 0)),
        ],
        out_specs=[
            pl.BlockSpec((tm, d), lambda i: (i, 0)),
            pl.BlockSpec((ts, d), lambda i: (0, 0)),
            pl.BlockSpec((tm, nh), lambda i: (i, 0)),
            pl.BlockSpec((ts, nh), lambda i: (0, 0)),
        ],
        out_shape=[
            jax.ShapeDtypeStruct((tp, d), BF16),
            jax.ShapeDtypeStruct((ts, d), BF16),
            jax.ShapeDtypeStruct((tp, nh), F32),
            jax.ShapeDtypeStruct((ts, nh), F32),
        ],
        compiler_params=_params(("arbitrary",)),
        name="norm_dt",
    )(xp, xs, g, wdt)


def _proj_a_kernel(xp_ref, xs_ref, wb_ref, wc_ref, wh_ref, cw_ref, st_ref,
                   op_ref, os_ref, cap_ref, cas_ref,
                   wbf, ubuf, sbuf, s1buf, s2buf, *, tiles_per_batch, nb_s, l_s):
    i = pl.program_id(1)
    tm = xp_ref.shape[0]
    ts = xs_ref.shape[0]
    cw = cw_ref[...]

    @pl.when(i == 0)
    def _():
        wbf[0] = wb_ref[...].astype(BF16)
        wbf[1] = wc_ref[...].astype(BF16)
        wbf[2] = wh_ref[...].astype(BF16)
        x = xs_ref[...]
        u = _dot_nt(x, wbf[1]) * _dot_nt(x, wbf[2])
        sbuf[pl.ds(0, SUBLANES), :] = jnp.zeros((SUBLANES, u.shape[1]), F32)
        sbuf[pl.ds(SUBLANES, ts), :] = u
        s1buf[...] = jnp.zeros_like(s1buf)
        s2buf[...] = jnp.zeros_like(s2buf)
        for b in range(nb_s):
            s1buf[pl.ds(b * l_s, 1), :] = st_ref[b, pl.ds(1, 1), :]
            s2buf[pl.ds(b * l_s, 1), :] = st_ref[b, pl.ds(0, 1), :]
            s2buf[pl.ds(b * l_s + 1, 1), :] = st_ref[b, pl.ds(1, 1), :]
        rmod = lax.broadcasted_iota(jnp.int32, (ts, 1), 0) % l_s
        prev1 = jnp.where(rmod == 0, s1buf[...], sbuf[pl.ds(SUBLANES - 1, ts), :])
        prev2 = jnp.where(rmod < 2, s2buf[...], sbuf[pl.ds(SUBLANES - 2, ts), :])
        v = prev2 * cw[0:1, :] + prev1 * cw[1:2, :] + u * cw[2:3, :]
        os_ref[...] = (_dot_nt(x, wbf[0]) * v).astype(BF16)
        for b in range(nb_s):
            cas_ref[b] = sbuf[pl.ds(SUBLANES + (b + 1) * l_s - 2, 2), :]

    @pl.when(i % tiles_per_batch == 0)
    def _():
        ubuf[pl.ds(0, SUBLANES), :] = jnp.zeros((SUBLANES, ubuf.shape[1]), F32)

    x = xp_ref[...]
    u = _dot_nt(x, wbf[1]) * _dot_nt(x, wbf[2])
    ubuf[pl.ds(SUBLANES, tm), :] = u
    v = (ubuf[pl.ds(SUBLANES - 2, tm), :] * cw[0:1, :]
         + ubuf[pl.ds(SUBLANES - 1, tm), :] * cw[1:2, :] + u * cw[2:3, :])
    op_ref[...] = (_dot_nt(x, wbf[0]) * v).astype(BF16)
    ubuf[pl.ds(0, SUBLANES), :] = ubuf[pl.ds(tm, SUBLANES), :]

    @pl.when(i % tiles_per_batch == tiles_per_batch - 1)
    def _():
        cap_ref[0] = ubuf[pl.ds(SUBLANES + tm - 2, 2), :]


def _proj_a(xnp, xns, w_in, conv_w, state_s, n_batch_p, seq_p, d_conv):
    tp, d = xnp.shape
    ts = xns.shape[0]
    nb_s = state_s.shape[0]
    l_s = ts // nb_s
    tn = 512
    tm = _pick(seq_p, (1024, 512, 256, 128))
    tpb = seq_p // tm
    ncol = d_conv // tn
    kern = functools.partial(_proj_a_kernel, tiles_per_batch=tpb, nb_s=nb_s, l_s=l_s)
    return pl.pallas_call(
        kern,
        grid=(ncol, tp // tm),
        in_specs=[
            pl.BlockSpec((tm, d), lambda j, i: (i, 0)),
            pl.BlockSpec((ts, d), lambda j, i: (0, 0)),
            pl.BlockSpec((tn, d), lambda j, i: (j, 0)),
            pl.BlockSpec((tn, d), lambda j, i: (j + ncol, 0)),
            pl.BlockSpec((tn, d), lambda j, i: (j + 2 * ncol, 0)),
            pl.BlockSpec((CONV_A_K, tn), lambda j, i: (0, j)),
            pl.BlockSpec((nb_s, CONV_A_K - 1, tn), lambda j, i: (0, 0, j)),
        ],
        out_specs=[
            pl.BlockSpec((tm, tn), lambda j, i: (i, j)),
            pl.BlockSpec((ts, tn), lambda j, i: (0, j)),
            pl.BlockSpec((1, CONV_A_K - 1, tn), lambda j, i: (i // tpb, 0, j)),
            pl.BlockSpec((nb_s, CONV_A_K - 1, tn), lambda j, i: (0, 0, j)),
        ],
        out_shape=[
            jax.ShapeDtypeStruct((tp, d_conv), BF16),
            jax.ShapeDtypeStruct((ts, d_conv), BF16),
            jax.ShapeDtypeStruct((n_batch_p, CONV_A_K - 1, d_conv), F32),
            jax.ShapeDtypeStruct((nb_s, CONV_A_K - 1, d_conv), F32),
        ],
        scratch_shapes=[
            pltpu.VMEM((3, tn, d), BF16),
            pltpu.VMEM((SUBLANES + tm, tn), F32),
            pltpu.VMEM((SUBLANES + ts, tn), F32),
            pltpu.VMEM((ts, tn), F32),
            pltpu.VMEM((ts, tn), F32),
        ],
        compiler_params=_params(("arbitrary", "arbitrary")),
        name="proj_a",
    )(xnp, xns, w_in, w_in, w_in, conv_w, state_s)


def _proj_raw_kernel(xp_ref, xs_ref, w_hbm, op_ref, os_ref, wbf, wstage, wsem, *, nsplit, width, row0):
    j = pl.program_id(0)
    tn = wbf.shape[0]

    def w_copy(jj, slot):
        rows = pl.ds(pl.multiple_of(row0 + jj * tn, SUBLANES), tn)
        return pltpu.make_async_copy(w_hbm.at[rows], wstage.at[slot], wsem.at[slot])

    def emit(x_ref, o_ref):
        acc = _dot_nt(x_ref[...], wbf[...])
        if nsplit == 0:
            o_ref[...] = acc
        else:
            for s in range(nsplit):
                o_ref[s] = acc[:, s * width:(s + 1) * width]

    @pl.when(pl.program_id(1) == 0)
    def _():
        @pl.when(j == 0)
        def _():
            w_copy(0, 0).start()

        w_copy(j, j % 2).wait()

        @pl.when(j + 1 < pl.num_programs(0))
        def _():
            w_copy(j + 1, (j + 1) % 2).start()

        wbf[...] = wstage[j % 2].astype(BF16)
        emit(xs_ref, os_ref)

    emit(xp_ref, op_ref)


def _proj_raw(xnp, xns, w, col0, ncols, width):
    tp, d = xnp.shape
    ts = xns.shape[0]
    tn = 1024
    tm = _pick(tp, (1024, 512, 256, 128))
    assert col0 % SUBLANES == 0 and ncols % tn == 0
    if width == 0:
        nsplit = 0
        out_specs = [pl.BlockSpec((tm, tn), lambda j, i: (i, j)),
                     pl.BlockSpec((ts, tn), lambda j, i: (0, j))]
        out_shape = [jax.ShapeDtypeStruct((tp, ncols), F32), jax.ShapeDtypeStruct((ts, ncols), F32)]
    else:
        nsplit = tn // width
        out_specs = [pl.BlockSpec((nsplit, tm, width), lambda j, i: (j, i, 0)),
                     pl.BlockSpec((nsplit, ts, width), lambda j, i: (j, 0, 0))]
        out_shape = [jax.ShapeDtypeStruct((ncols // width, tp, width), F32),
                     jax.ShapeDtypeStruct((ncols // width, ts, width), F32)]
    kern = functools.partial(_proj_raw_kernel, nsplit=nsplit, width=width, row0=col0)
    return pl.pallas_call(
        kern,
        grid=(ncols // tn, tp // tm),
        in_specs=[
            pl.BlockSpec((tm, d), lambda j, i: (i, 0)),
            pl.BlockSpec((ts, d), lambda j, i: (0, 0)),
            pl.BlockSpec(memory_space=pl.ANY),
        ],
        out_specs=out_specs,
        out_shape=out_shape,
        scratch_shapes=[pltpu.VMEM((tn, d), BF16), pltpu.VMEM((2, tn, d), F32), pltpu.SemaphoreType.DMA((2,))],
        compiler_params=_params(("arbitrary", "arbitrary")),
        name=f"proj_raw_{col0}",
    )(xnp, xns, w)


def _proj_conv_kernel(xp_ref, xs_ref, w_ref, cw_ref, cb_ref, st_ref, op_ref, os_ref, cmp_ref, cms_ref,
                      wbf, ubuf, sbuf, fix, *, tiles_per_batch, nb_s, l_s, nsplit, width):
    i = pl.program_id(1)
    tm = xp_ref.shape[0]
    ts = xs_ref.shape[0]
    tail = CONV_M_K - 1
    cw = cw_ref[...]
    bias = cb_ref[...]

    def store(o_ref, act):
        for s in range(nsplit):
            o_ref[s] = act[:, s * width:(s + 1) * width]

    @pl.when(i == 0)
    def _():
        wbf[...] = w_ref[...].astype(BF16)
        raw = _dot_nt(xs_ref[...], wbf[...])
        sbuf[pl.ds(0, SUBLANES), :] = jnp.zeros((SUBLANES, raw.shape[1]), F32)
        sbuf[pl.ds(SUBLANES, ts), :] = raw
        fix[...] = jnp.zeros_like(fix)
        for b in range(nb_s):
            for back in range(1, tail + 1):
                for m in range(back):
                    fix[back - 1, pl.ds(b * l_s + m, 1), :] = st_ref[b, pl.ds(tail + m - back, 1), :]
        rmod = lax.broadcasted_iota(jnp.int32, (ts, 1), 0) % l_s
        acc = raw * cw[tail:tail + 1, :]
        for back in range(1, tail + 1):
            tap = jnp.where(rmod < back, fix[back - 1], sbuf[pl.ds(SUBLANES - back, ts), :])
            acc = acc + tap * cw[tail - back:tail - back + 1, :]
        store(os_ref, jax.nn.silu(acc + bias))
        for b in range(nb_s):
            cms_ref[b] = sbuf[pl.ds(SUBLANES + (b + 1) * l_s - tail, tail), :]

    @pl.when(i % tiles_per_batch == 0)
    def _():
        ubuf[pl.ds(0, SUBLANES), :] = jnp.zeros((SUBLANES, ubuf.shape[1]), F32)

    raw = _dot_nt(xp_ref[...], wbf[...])
    ubuf[pl.ds(SUBLANES, tm), :] = raw
    acc = raw * cw[tail:tail + 1, :]
    for back in range(1, tail + 1):
        acc = acc + ubuf[pl.ds(SUBLANES - back, tm), :] * cw[tail - back:tail - back + 1, :]
    store(op_ref, jax.nn.silu(acc + bias))
    ubuf[pl.ds(0, SUBLANES), :] = ubuf[pl.ds(tm, SUBLANES), :]

    @pl.when(i % tiles_per_batch == tiles_per_batch - 1)
    def _():
        cmp_ref[0] = ubuf[pl.ds(SUBLANES + tm - tail, tail), :]


def _proj_conv(xnp, xns, w, conv_w, conv_b, state_s, col0, ch0, ncols, width, n_batch_p, seq_p):
    tp, d = xnp.shape
    ts = xns.shape[0]
    nb_s = state_s.shape[0]
    l_s = ts // nb_s
    tail = CONV_M_K - 1
    tn = 1024
    tm = _pick(seq_p, (1024, 512, 256, 128))
    tpb = seq_p // tm
    assert col0 % tn == 0 and ch0 % tn == 0 and ncols % tn == 0 and l_s >= tail
    jb, cb0 = col0 // tn, ch0 // tn
    nsplit = tn // width
    kern = functools.partial(_proj_conv_kernel, tiles_per_batch=tpb, nb_s=nb_s, l_s=l_s,
                             nsplit=nsplit, width=width)
    return pl.pallas_call(
        kern,
        grid=(ncols // tn, tp // tm),
        in_specs=[
            pl.BlockSpec((tm, d), lambda j, i: (i, 0)),
            pl.BlockSpec((ts, d), lambda j, i: (0, 0)),
            pl.BlockSpec((tn, d), lambda j, i: (j + jb, 0)),
            pl.BlockSpec((CONV_M_K, tn), lambda j, i: (0, j + cb0)),
            pl.BlockSpec((1, tn), lambda j, i: (0, j + cb0)),
            pl.BlockSpec((nb_s, tail, tn), lambda j, i: (0, 0, j + cb0)),
        ],
        out_specs=[
            pl.BlockSpec((nsplit, tm, width), lambda j, i: (j, i, 0)),
            pl.BlockSpec((nsplit, ts, width), lambda j, i: (j, 0, 0)),
            pl.BlockSpec((1, tail, tn), lambda j, i: (i // tpb, 0, j)),
            pl.BlockSpec((nb_s, tail, tn), lambda j, i: (0, 0, j)),
        ],
        out_shape=[
            jax.ShapeDtypeStruct((ncols // width, tp, width), F32),
            jax.ShapeDtypeStruct((ncols // width, ts, width), F32),
            jax.ShapeDtypeStruct((n_batch_p, tail, ncols), F32),
            jax.ShapeDtypeStruct((nb_s, tail, ncols), F32),
        ],
        scratch_shapes=[
            pltpu.VMEM((tn, d), BF16),
            pltpu.VMEM((SUBLANES + tm, tn), F32),
            pltpu.VMEM((SUBLANES + ts, tn), F32),
            pltpu.VMEM((tail, ts, tn), F32),
        ],
        compiler_params=_params(("arbitrary", "arbitrary")),
        name=f"proj_conv_{col0}",
    )(xnp, xns, w, conv_w, conv_b, state_s)


def _ssd_kernel(*refs, q, has_state, nchunks):
    (z_ref, xs_ref, b_ref, c_ref, dt_ref, dtb_ref, alog_ref, dsk_ref, gn_ref, *rest) = refs
    if has_state:
        hprev, *rest = rest
    (y_ref, oh, h_s, acg, rowt) = rest
    c = pl.program_id(1)
    nheads = N_GROUPS * HEADS_PER_GROUP

    @pl.when(c == 0)
    def _init():
        if has_state:
            h_s[...] = hprev[...]
        else:
            h_s[...] = jnp.zeros_like(h_s)

    dt = _softplus(dt_ref[...] + dtb_ref[...])
    da = dt * (-jnp.exp(alog_ref[...]))
    ri = lax.broadcasted_iota(jnp.int32, (q, q), 0)
    ci = lax.broadcasted_iota(jnp.int32, (q, q), 1)
    causal = ri >= ci
    tril = jnp.where(causal, 1.0, 0.0).astype(BF16)
    acum = sum(_dot(tril, p) for p in _split3(da))
    eye = jnp.where(lax.broadcasted_iota(jnp.int32, (nheads, nheads), 0)
                    == lax.broadcasted_iota(jnp.int32, (nheads, nheads), 1), 1.0, 0.0).astype(BF16)
    acum2 = acum * LOG2_E
    rowt[0] = sum(lax.dot_general(eye, p, NT_DIMS, preferred_element_type=F32) for p in _split3(acum2))
    rowt[1] = sum(lax.dot_general(eye, p, NT_DIMS, preferred_element_type=F32) for p in _split3(dt))
    wend = jnp.exp(acum[q - 1:q, :] - acum) * dt
    rowt[2] = sum(lax.dot_general(eye, p, NT_DIMS, preferred_element_type=F32) for p in _split3(wend))
    for g in range(N_GROUPS):
        acg[g] = acum2[:, g * HEADS_PER_GROUP:(g + 1) * HEADS_PER_GROUP]

    lane = lax.broadcasted_iota(jnp.int32, (1, LANES), 1)
    rowi = lax.broadcasted_iota(jnp.int32, (LANES, 1), 0)
    half_w = LANES // 2

    def group_body(g, carry):
        xs = xs_ref[g]
        bb = b_ref[g].astype(BF16)
        ccb16 = c_ref[g].astype(BF16)
        cb_ = lax.dot_general(ccb16, bb, NT_DIMS, preferred_element_type=F32)
        ac8 = acg[g]
        dsk = dsk_ref[g]
        lo_half = lane < half_w
        lo_rows = rowi < half_w
        ys = []
        for pair in range(HEADS_PER_GROUP // 2):
            sl = slice(pair * LANES, (pair + 1) * LANES)
            xp = xs[:, sl]
            hp = h_s[g, pl.ds(pair * LANES, LANES), :]
            ms_, acols, wrows = [], [], []
            for r in (2 * pair, 2 * pair + 1):
                head = g * HEADS_PER_GROUP + r
                acol = jnp.broadcast_to(ac8[:, r:r + 1], (q, LANES))
                arow = rowt[0, pl.ds(head, 1), :]
                drow = rowt[1, pl.ds(head, 1), :]
                decay = jnp.exp2(jnp.where(causal, acol[:, :q] - arow, -jnp.inf))
                ms_.append((cb_ * decay * drow).astype(BF16))
                acols.append(acol)
                wrows.append(jnp.broadcast_to(rowt[2, pl.ds(head, 1), :], (half_w, q)))
            xlo = jnp.where(lo_half, xp, 0.0).astype(BF16)
            xhi = jnp.where(lo_half, 0.0, xp).astype(BF16)
            if q % LANES == 0:
                ydiag = _dot(jnp.concatenate(ms_, axis=1), jnp.concatenate([xlo, xhi], axis=0))
            else:
                ydiag = _dot(ms_[0], xlo) + _dot(ms_[1], xhi)
            ea = jnp.where(lo_half, jnp.exp2(acols[0]), jnp.exp2(acols[1]))
            yoff = ea * lax.dot_general(ccb16, hp.astype(BF16), NT_DIMS, preferred_element_type=F32)
            ys.append(dsk[:, sl] * xp + ydiag + yoff)
            xwt = (xp.T * jnp.concatenate(wrows, axis=0)).astype(BF16)
            dlast = [jnp.broadcast_to(jnp.exp2(a[q - 1:q, :]), (LANES, LANES)) for a in acols]
            h_s[g, pl.ds(pair * LANES, LANES), :] = hp * jnp.where(lo_rows, dlast[0], dlast[1]) + _dot(xwt, bb)
        yg = jnp.concatenate(ys, axis=1)
        hh = yg * jax.nn.silu(z_ref[g])
        ms = jnp.mean(hh * hh, axis=-1, keepdims=True)
        y_ref[g] = (hh * lax.rsqrt(ms + EPS) * gn_ref[g]).astype(BF16)
        return carry

    lax.fori_loop(0, N_GROUPS, group_body, 0, unroll=4)

    @pl.when(c == nchunks - 1)
    def _():
        oh[...] = h_s[...]


def _ssd(z, xc, bc, dt_raw, prm, n_batch, seq, q, h_prev=None):
    nchunks = seq // q
    nheads = N_GROUPS * HEADS_PER_GROUP
    has_state = h_prev is not None
    g8 = N_GROUPS

    def tok(first):
        return lambda b, c: (first, b * nchunks + c, 0)

    def const3(b, c):
        return (0, 0, 0)

    in_specs = [
        pl.BlockSpec((g8, q, GROUP_W), tok(0)),
        pl.BlockSpec((g8, q, GROUP_W), tok(0)),
        pl.BlockSpec((g8, q, D_STATE), tok(0)),
        pl.BlockSpec((g8, q, D_STATE), tok(1)),
        pl.BlockSpec((q, nheads), lambda b, c: (b * nchunks + c, 0)),
        pl.BlockSpec((1, nheads), lambda b, c: (0, 0)),
        pl.BlockSpec((1, nheads), lambda b, c: (0, 0)),
        pl.BlockSpec((g8, 1, GROUP_W), const3),
        pl.BlockSpec((g8, 1, GROUP_W), const3),
    ]
    args = [z, xc, bc, bc, dt_raw, prm["dtb"], prm["alog"], prm["dsk"], prm["gn"]]

    def per_batch(shape):
        return pl.BlockSpec((None,) + shape, lambda b, c: (b,) + (0,) * len(shape))

    if has_state:
        in_specs.append(per_batch((g8, GROUP_W, D_STATE)))
        args.append(h_prev)
    t = n_batch * seq
    out_specs = [
        pl.BlockSpec((g8, q, GROUP_W), lambda b, c: (0, b * nchunks + c, 0)),
        per_batch((g8, GROUP_W, D_STATE)),
    ]
    out_shape = [
        jax.ShapeDtypeStruct((g8, t, GROUP_W), BF16),
        jax.ShapeDtypeStruct((n_batch, g8, GROUP_W, D_STATE), F32),
    ]
    scratch = [
        pltpu.VMEM((g8, GROUP_W, D_STATE), F32),
        pltpu.VMEM((g8, q, HEADS_PER_GROUP), F32),
        pltpu.VMEM((3, nheads, q), F32),
    ]
    kern = functools.partial(_ssd_kernel, q=q, has_state=has_state, nchunks=nchunks)
    return pl.pallas_call(
        kern,
        grid=(n_batch, nchunks),
        in_specs=in_specs,
        out_specs=out_specs,
        out_shape=out_shape,
        scratch_shapes=scratch,
        compiler_params=_params(("arbitrary", "arbitrary")),
        name="ssd_state" if has_state else "ssd",
    )(*args)


def _merge_kernel(ap_ref, as_ref, yp_ref, ys_ref, gap_ref, gmp_ref, gas_ref, gms_ref, wa_ref, wm_ref,
                  op_ref, os_ref, wab, wmb):
    def emit(a_ref, y_ref, ga_ref, gm_ref, o_ref):
        oa = _dot(a_ref[...], wab[...])
        y = jnp.concatenate([y_ref[g] for g in range(N_GROUPS)], axis=1)
        om = _dot(y, wmb[...])
        o_ref[...] = (jax.nn.sigmoid(ga_ref[...]) * oa + jax.nn.sigmoid(gm_ref[...]) * om).astype(BF16)

    @pl.when(pl.program_id(1) == 0)
    def _():
        wab[...] = wa_ref[...].astype(BF16)
        wmb[...] = wm_ref[...].astype(BF16)
        emit(as_ref, ys_ref, gas_ref, gms_ref, os_ref)

    emit(ap_ref, yp_ref, gap_ref, gmp_ref, op_ref)


def _merge(abv_p, abv_s, y_p, y_s, g_p, g_s, w_a_out, w_m_out):
    tp, dc = abv_p.shape
    ts = abv_s.shape[0]
    dm = w_a_out.shape[1]
    tn = 512
    tm = _pick(tp, (512, 256, 128))
    ncol = dm // tn
    return pl.pallas_call(
        _merge_kernel,
        grid=(ncol, tp // tm),
        in_specs=[
            pl.BlockSpec((tm, dc), lambda j, i: (i, 0)),
            pl.BlockSpec((ts, dc), lambda j, i: (0, 0)),
            pl.BlockSpec((N_GROUPS, tm, GROUP_W), lambda j, i: (0, i, 0)),
            pl.BlockSpec((N_GROUPS, ts, GROUP_W), lambda j, i: (0, 0, 0)),
            pl.BlockSpec((tm, tn), lambda j, i: (i, j)),
            pl.BlockSpec((tm, tn), lambda j, i: (i, j + ncol)),
            pl.BlockSpec((ts, tn), lambda j, i: (0, j)),
            pl.BlockSpec((ts, tn), lambda j, i: (0, j + ncol)),
            pl.BlockSpec((dc, tn), lambda j, i: (0, j)),
            pl.BlockSpec((N_GROUPS * GROUP_W, tn), lambda j, i: (0, j)),
        ],
        out_specs=[
            pl.BlockSpec((tm, tn), lambda j, i: (i, j)),
            pl.BlockSpec((ts, tn), lambda j, i: (0, j)),
        ],
        out_shape=[jax.ShapeDtypeStruct((tp, dm), BF16), jax.ShapeDtypeStruct((ts, dm), BF16)],
        scratch_shapes=[pltpu.VMEM((dc, tn), BF16), pltpu.VMEM((N_GROUPS * GROUP_W, tn), BF16)],
        compiler_params=_params(("arbitrary", "arbitrary")),
        name="merge",
    )(abv_p, abv_s, y_p, y_s, g_p, g_p, g_s, g_s, w_a_out, w_m_out)


def _res_kernel(ap_ref, as_ref, rp_ref, rs_ref, w_ref, g_ref, *outs, n_prompt_tiles, merged_out, ts):
    if merged_out:
        h_ref, hn_ref, wbf = outs
    else:
        hp_ref, hs_ref, hnp_ref, hns_ref, wbf = outs
    i = pl.program_id(0)

    @pl.when(i == 0)
    def _():
        wbf[...] = w_ref[...].astype(BF16)

    def emit(a_ref, r_ref, store_h, store_hn):
        h = r_ref[...] + _dot(a_ref[...], wbf[...])
        store_h(h)
        store_hn(_rms(h, g_ref[...]))

    if merged_out:
        @pl.when(i < n_prompt_tiles)
        def _():
            def sh(h):
                h_ref[...] = h

            def shn(hn):
                hn_ref[...] = hn

            emit(ap_ref, rp_ref, sh, shn)

        @pl.when(i == n_prompt_tiles)
        def _():
            def sh(h):
                h_ref[pl.ds(0, ts), :] = h

            def shn(hn):
                hn_ref[pl.ds(0, ts), :] = hn

            emit(as_ref, rs_ref, sh, shn)
    else:
        def shp(h):
            hp_ref[...] = h

        def shnp(hn):
            hnp_ref[...] = hn.astype(hnp_ref.dtype)

        emit(ap_ref, rp_ref, shp, shnp)

        @pl.when(i == 0)
        def _():
            def shs(h):
                hs_ref[...] = h

            def shns(hn):
                hns_ref[...] = hn.astype(hns_ref.dtype)

            emit(as_ref, rs_ref, shs, shns)


def _res(a_p, a_s, r_p, r_s, w, g, merged_out, hn_dtype):
    tp, d = a_p.shape
    ts = a_s.shape[0]
    tm = _pick(tp, (512, 256))
    assert tp % tm == 0 and ts <= tm
    npt = tp // tm
    last = npt - 1
    in_specs = [
        pl.BlockSpec((tm, d), lambda i: (jnp.minimum(i, last), 0)),
        pl.BlockSpec((ts, d), lambda i: (0, 0)),
        pl.BlockSpec((tm, d), lambda i: (jnp.minimum(i, last), 0)),
        pl.BlockSpec((ts, d), lambda i: (0, 0)),
        pl.BlockSpec((d, d), lambda i: (0, 0), pipeline_mode=pl.Buffered(1)),
        pl.BlockSpec((1, d), lambda i: (0, 0)),
    ]
    if merged_out:
        grid = (npt + 1,)
        out_specs = [pl.BlockSpec((tm, d), lambda i: (i, 0)), pl.BlockSpec((tm, d), lambda i: (i, 0))]
        out_shape = [jax.ShapeDtypeStruct((tp + ts, d), F32), jax.ShapeDtypeStruct((tp + ts, d), hn_dtype)]
    else:
        grid = (npt,)
        out_specs = [pl.BlockSpec((tm, d), lambda i: (i, 0)), pl.BlockSpec((ts, d), lambda i: (0, 0)),
                     pl.BlockSpec((tm, d), lambda i: (i, 0)), pl.BlockSpec((ts, d), lambda i: (0, 0))]
        out_shape = [jax.ShapeDtypeStruct((tp, d), F32), jax.ShapeDtypeStruct((ts, d), F32),
                     jax.ShapeDtypeStruct((tp, d), hn_dtype), jax.ShapeDtypeStruct((ts, d), hn_dtype)]
    kern = functools.partial(_res_kernel, n_prompt_tiles=npt, merged_out=merged_out, ts=ts)
    return pl.pallas_call(
        kern,
        grid=grid,
        in_specs=in_specs,
        out_specs=out_specs,
        out_shape=out_shape,
        scratch_shapes=[pltpu.VMEM((d, d), BF16)],
        compiler_params=_params(("arbitrary",)),
        name="res_merged" if merged_out else "res",
    )(a_p, a_s, r_p, r_s, w, g)


def _kv_kernel(m_ref, g_ref, wk_ref, wv_ref, k_hbm, v_hbm, mn, kvb, sem, *, n_batch):
    j = pl.program_id(0)

    @pl.when(j == 0)
    def _():
        mn[...] = _rms(m_ref[...], g_ref[...]).astype(BF16)

    kvb[0] = _dot(mn[...], wk_ref[...].astype(BF16))
    kvb[1] = _dot(mn[...], wv_ref[...].astype(BF16))
    nm = kvb.shape[1] // n_batch
    for h in range(N_XHEADS):
        @pl.when(j == h)
        def _(h=h):
            copies = [pltpu.make_async_copy(kvb.at[t, pl.ds(b * nm, nm)], dst.at[b, :, h, :], sem.at[t, b])
                      for t, dst in enumerate((k_hbm, v_hbm)) for b in range(n_batch)]
            for c in copies:
                c.start()
            for c in copies:
                c.wait()


def _memory_kv(mem2d, g, w_k, w_v, n_batch):
    m, d = mem2d.shape
    dh = d // N_XHEADS
    out_sds = jax.ShapeDtypeStruct((n_batch, m // n_batch, N_XHEADS, dh), F32)
    kern = functools.partial(_kv_kernel, n_batch=n_batch)
    return pl.pallas_call(
        kern,
        grid=(N_XHEADS,),
        in_specs=[
            pl.BlockSpec((m, d), lambda j: (0, 0)),
            pl.BlockSpec((1, d), lambda j: (0, 0)),
            pl.BlockSpec((d, dh), lambda j: (0, j)),
            pl.BlockSpec((d, dh), lambda j: (0, j)),
        ],
        out_specs=[pl.BlockSpec(memory_space=pl.ANY), pl.BlockSpec(memory_space=pl.ANY)],
        out_shape=[out_sds, out_sds],
        scratch_shapes=[pltpu.VMEM((m, d), BF16), pltpu.VMEM((2, m, dh), F32),
                        pltpu.SemaphoreType.DMA((2, n_batch))],
        compiler_params=_params(("arbitrary",)),
        name="memory_kv",
    )(mem2d, g, w_k, w_v)


def _attn_kernel(hn_ref, k_ref, v_ref, wq_ref, o_ref, wqb, kb, vb, kvf, sem, *q_all):
    b = pl.program_id(0)
    i = pl.program_id(1)
    tm = o_ref.shape[0]

    @pl.when((b == 0) & (i == 0))
    def _():
        wqb[...] = wq_ref[...].astype(BF16)
        if q_all:
            q_all[0][...] = _dot(hn_ref[...], wqb[...])

    d = wqb.shape[1]
    dh = d // N_XHEADS

    @pl.when(i == 0)
    def _():
        def copies(bb):
            slot = bb % 2
            return [pltpu.make_async_copy(src.at[bb, :, h, :], kvf.at[slot, t, h], sem.at[slot, t, h])
                    for t, src in enumerate((k_ref, v_ref)) for h in range(N_XHEADS)]

        @pl.when(b == 0)
        def _():
            for c in copies(b):
                c.start()

        for c in copies(b):
            c.wait()

        @pl.when(b + 1 < pl.num_programs(0))
        def _():
            for c in copies(b + 1):
                c.start()

        for h in range(N_XHEADS):
            kb[:, h * dh:(h + 1) * dh] = kvf[b % 2, 0, h].astype(BF16)
            vb[:, h * dh:(h + 1) * dh] = kvf[b % 2, 1, h].astype(BF16)

    if q_all:
        row0 = pl.multiple_of((b * pl.num_programs(1) + i) * tm, SUBLANES)
        q = q_all[0][pl.ds(row0, tm), :]
    else:
        q = _dot(hn_ref[...], wqb[...])
    outs = []
    for h in range(N_XHEADS):
        sl = slice(h * dh, (h + 1) * dh)
        s = lax.dot_general(q[:, sl].astype(BF16), kb[:, sl], NT_DIMS, preferred_element_type=F32)
        s = s * (dh ** -0.5)
        e = jnp.exp(s - jnp.max(s, axis=-1, keepdims=True))
        p = e / jnp.sum(e, axis=-1, keepdims=True)
        outs.append(_dot(p.astype(BF16), vb[:, sl]))
    o_ref[...] = jnp.concatenate(outs, axis=1).astype(BF16)


def _attn(hn, k, v, w_q, n_batch, seq):
    t, d = hn.shape
    nm, nh, dh = k.shape[1:]
    tm = _pick(seq, (512, 256, 128, 64, 32, 16))
    tpb = seq // tm
    kv_spec = pl.BlockSpec(memory_space=pl.ANY)
    scratch = [pltpu.VMEM((d, d), BF16), pltpu.VMEM((nm, d), BF16), pltpu.VMEM((nm, d), BF16),
               pltpu.VMEM((2, 2, nh, nm, dh), F32), pltpu.SemaphoreType.DMA((2, 2, nh))]
    if tm < 128 and t <= 512:
        hn_spec = pl.BlockSpec((t, d), lambda b, i: (0, 0))
        scratch.append(pltpu.VMEM((t, d), F32))
    else:
        hn_spec = pl.BlockSpec((tm, d), lambda b, i: (b * tpb + i, 0))
    return pl.pallas_call(
        _attn_kernel,
        grid=(n_batch, tpb),
        in_specs=[
            hn_spec,
            kv_spec,
            kv_spec,
            pl.BlockSpec((d, d), lambda b, i: (0, 0), pipeline_mode=pl.Buffered(1)),
        ],
        out_specs=pl.BlockSpec((tm, d), lambda b, i: (b * tpb + i, 0)),
        out_shape=jax.ShapeDtypeStruct((t, d), BF16),
        scratch_shapes=scratch,
        compiler_params=_params(("arbitrary", "arbitrary")),
        name=f"attn_{seq}",
    )(hn, k, v, w_q)


def _router_kernel(x_ref, w_ref, b_ref, ri_ref, rw_ref):
    logits = _dot(x_ref[...].astype(BF16), w_ref[...].astype(BF16)) + b_ref[...]
    lane_i = lax.broadcasted_iota(jnp.int32, logits.shape, 1)
    lane = lane_i.astype(F32)
    ninf = -jnp.inf
    big = float(LANES)
    is_g = lane < N_EXPERT_GROUPS
    gl = jnp.where(is_g, logits, ninf)
    gmax = jnp.max(gl, axis=-1, keepdims=True)
    gsel = jnp.min(jnp.where(gl == gmax, lane, big), axis=-1, keepdims=True)
    pg = 1.0 / jnp.sum(jnp.where(is_g, jnp.exp(gl - gmax), 0.0), axis=-1, keepdims=True)
    lo = N_EXPERT_GROUPS + EXPERTS_PER_GROUP * gsel
    el = jnp.where(lane >= lo, jnp.where(lane < lo + EXPERTS_PER_GROUP, logits, ninf), ninf)
    m1 = jnp.max(el, axis=-1, keepdims=True)
    i1 = jnp.min(jnp.where(el == m1, lane, big), axis=-1, keepdims=True)
    el2 = jnp.where(lane == i1, ninf, el)
    m2 = jnp.max(el2, axis=-1, keepdims=True)
    i2 = jnp.min(jnp.where(el2 == m2, lane, big), axis=-1, keepdims=True)
    e = jnp.exp(m2 - m1)
    w1 = pg / (1.0 + e)
    w2 = pg * e / (1.0 + e)
    ri_ref[...] = jnp.where(lane_i == 0, i1 - N_EXPERT_GROUPS,
                            jnp.where(lane_i == 1, i2 - N_EXPERT_GROUPS, 0.0)).astype(jnp.int32)
    rw_ref[...] = jnp.where(lane_i == 0, w1, jnp.where(lane_i == 1, w2, 0.0))


def _router(hn_all, w_r, b_r):
    t, d = hn_all.shape
    tm = _pick(t, (640, 512, 384, 256, 128))
    return pl.pallas_call(
        _router_kernel,
        grid=(t // tm,),
        in_specs=[
            pl.BlockSpec((tm, d), lambda i: (i, 0)),
            pl.BlockSpec((d, LANES), lambda i: (0, 0)),
            pl.BlockSpec((1, LANES), lambda i: (0, 0)),
        ],
        out_specs=[pl.BlockSpec((tm, LANES), lambda i: (i, 0)), pl.BlockSpec((tm, LANES), lambda i: (i, 0))],
        out_shape=[jax.ShapeDtypeStruct((t, LANES), jnp.int32), jax.ShapeDtypeStruct((t, LANES), F32)],
        compiler_params=_params(("arbitrary",)),
        name="router",
    )(hn_all, w_r, b_r)


def _row_copy(src, dst, s, d, sem):
    return pltpu.make_async_copy(src.at[pl.ds(s, 1)], dst.at[pl.ds(d, 1)], sem)


def _dispatch_kernel(dest_ref, pad0_ref, padn_ref, nsub_ref, x_ref, o_ref, zrow, zblk, sem, zsem, csem, bsem, *,
                     chunk, n_blocks):
    base = pl.program_id(0) * chunk

    @pl.when(pl.program_id(0) == 0)
    def _():
        zrow[...] = jnp.zeros_like(zrow)
        zblk[...] = jnp.zeros_like(zblk)

        def pad_plan(e):
            p0 = pad0_ref[e]
            head = jnp.minimum(padn_ref[e], (SUBLANES - p0 % SUBLANES) % SUBLANES)
            return p0, head, (padn_ref[e] - head) // SUBLANES

        def group_copy(row):
            return pltpu.make_async_copy(zrow, o_ref.at[pl.ds(pl.multiple_of(row, SUBLANES), SUBLANES)], csem)

        def pad_start(e, carry):
            p0, head, ngroups = pad_plan(e)

            def start_row(r, c):
                _row_copy(zrow, o_ref, 0, p0 + r, zsem).start()
                return c

            def start_group(j, c):
                group_copy(p0 + head + j * SUBLANES).start()
                return c

            lax.fori_loop(0, head, start_row, 0)
            return lax.fori_loop(0, ngroups, start_group, carry)

        def pad_wait(e, carry):
            _, head, ngroups = pad_plan(e)

            def wait_row(r, c):
                _row_copy(zrow, o_ref, 0, 0, zsem).wait()
                return c

            def wait_group(j, c):
                group_copy(0).wait()
                return c

            lax.fori_loop(0, head, wait_row, 0)
            return lax.fori_loop(0, ngroups, wait_group, carry)

        lax.fori_loop(0, N_EXPERTS, pad_start, 0)

        def blk_copy(b):
            return pltpu.make_async_copy(zblk, o_ref.at[pl.ds(pl.multiple_of(b * EXPERT_SUB, EXPERT_SUB),
                                                             EXPERT_SUB)], bsem)

        def tail_start(b, c):
            blk_copy(b).start()
            return c

        def tail_wait(b, c):
            blk_copy(b).wait()
            return c

        lax.fori_loop(nsub_ref[0], n_blocks, tail_start, 0)
        lax.fori_loop(0, N_EXPERTS, pad_wait, 0)
        lax.fori_loop(nsub_ref[0], n_blocks, tail_wait, 0)

    def issue(r, carry):
        t = base + r
        _row_copy(x_ref, o_ref, r, dest_ref[2 * t], sem).start()
        _row_copy(x_ref, o_ref, r, dest_ref[2 * t + 1], sem).start()
        return carry

    lax.fori_loop(0, chunk, issue, 0, unroll=8)

    for _ in range(2):
        pltpu.make_async_copy(x_ref, o_ref.at[pl.ds(0, chunk)], sem).wait()


def _dispatch(tables, hn_all, n_blocks):
    dest, pad0, padn, nsub = tables
    t, d = hn_all.shape
    chunk = _pick(t, (640, 512, 384, 256, 128))
    kern = functools.partial(_dispatch_kernel, chunk=chunk, n_blocks=n_blocks)
    return pl.pallas_call(
        kern,
        grid_spec=pltpu.PrefetchScalarGridSpec(
            num_scalar_prefetch=4,
            grid=(t // chunk,),
            in_specs=[pl.BlockSpec((chunk, d), lambda i, *_: (i, 0))],
            out_specs=pl.BlockSpec(memory_space=pl.ANY),
            scratch_shapes=[pltpu.VMEM((SUBLANES, d), F32), pltpu.VMEM((EXPERT_SUB, d), F32),
                            pltpu.SemaphoreType.DMA(()), pltpu.SemaphoreType.DMA(()),
                            pltpu.SemaphoreType.DMA(()), pltpu.SemaphoreType.DMA(())],
        ),
        out_shape=jax.ShapeDtypeStruct((n_blocks * EXPERT_SUB, d), F32),
        compiler_params=_params(("arbitrary",)),
        name="dispatch",
    )(dest, pad0, padn, nsub, hn_all)


def _expert_kernel(bstart_ref, wg_ref, wu_ref, wd_ref, x_hbm, y_hbm, gcache, ucache, dcache, xbuf, ybuf, xb,
                   xsem, ysem):
    e = pl.program_id(0)
    k = pl.program_id(1)
    nk = gcache.shape[1]
    total = bstart_ref[N_EXPERTS]

    def x_copy(b, slot):
        rows = pl.ds(pl.multiple_of(b * EXPERT_SUB, EXPERT_SUB), EXPERT_SUB)
        return pltpu.make_async_copy(x_hbm.at[rows], xbuf.at[slot], xsem.at[slot])

    def y_copy(b, slot):
        rows = pl.ds(pl.multiple_of(b * EXPERT_SUB, EXPERT_SUB), EXPERT_SUB)
        return pltpu.make_async_copy(ybuf.at[slot], y_hbm.at[rows], ysem.at[slot])

    @pl.when(e < N_EXPERTS)
    def _():
        slot = e % 2
        gcache[slot, k] = wg_ref[...].astype(BF16)
        ucache[slot, k] = wu_ref[...].astype(BF16)
        dcache[slot, k] = wd_ref[...].astype(BF16)

    @pl.when((e == 0) & (k == 0) & (total > 0))
    def _():
        x_copy(0, 0).start()

    @pl.when(e >= 1)
    def _():
        owner = e - 1
        wslot = owner % 2
        b0 = bstart_ref[owner]
        n = bstart_ref[owner + 1] - b0

        cpu = 2
        upb = nk // cpu

        def unit(u, carry):
            b = b0 + u // upb
            part = u % upb
            slot = b % 2

            def swiglu(x, p):
                kks = [p * cpu + c for c in range(cpu)]
                wg = jnp.concatenate([gcache[wslot, kk] for kk in kks], axis=1)
                wu = jnp.concatenate([ucache[wslot, kk] for kk in kks], axis=1)
                wd = jnp.concatenate([dcache[wslot, kk] for kk in kks], axis=0)
                hid = (jax.nn.silu(_dot(x, wg)) * _dot(x, wu)).astype(BF16)
                return _dot(hid, wd)

            @pl.when(part == 0)
            def _():
                x_copy(b, slot).wait()

                @pl.when(b + 1 < total)
                def _():
                    x_copy(b + 1, 1 - slot).start()

                x = xbuf[slot].astype(BF16)
                xb[...] = x

                @pl.when(b >= 2)
                def _():
                    y_copy(b - 2, slot).wait()

                ybuf[slot] = swiglu(x, 0)

            for p in range(1, upb):
                @pl.when(part == p)
                def _(p=p):
                    ybuf[slot] += swiglu(xb[...], p)
                    if p == upb - 1:
                        y_copy(b, slot).start()

            return carry

        lax.fori_loop((n * upb * k) // nk, (n * upb * (k + 1)) // nk, unit, 0)

    @pl.when((e == N_EXPERTS) & (k == nk - 1))
    def _():
        @pl.when(total >= 2)
        def _():
            y_copy(total - 2, total % 2).wait()

        @pl.when(total >= 1)
        def _():
            y_copy(total - 1, (total - 1) % 2).wait()


def _experts(bstart, x_sorted, w_gate, w_up, w_down, kchunk):
    nrows, d = x_sorted.shape
    de = w_gate.shape[2]
    nk = de // kchunk
    last = N_EXPERTS - 1

    def widx(e, k):
        return jnp.minimum(e, last), jnp.where(e <= last, k, nk - 1)

    def in_idx(e, k, b):
        ee, kk = widx(e, k)
        return (ee, 0, kk)

    def down_idx(e, k, b):
        ee, kk = widx(e, k)
        return (ee, kk, 0)

    return pl.pallas_call(
        _expert_kernel,
        grid_spec=pltpu.PrefetchScalarGridSpec(
            num_scalar_prefetch=1,
            grid=(N_EXPERTS + 1, nk),
            in_specs=[
                pl.BlockSpec((None, d, kchunk), in_idx),
                pl.BlockSpec((None, d, kchunk), in_idx),
                pl.BlockSpec((None, kchunk, d), down_idx),
                pl.BlockSpec(memory_space=pl.ANY),
            ],
            out_specs=pl.BlockSpec(memory_space=pl.ANY),
            scratch_shapes=[
                pltpu.VMEM((2, nk, d, kchunk), BF16),
                pltpu.VMEM((2, nk, d, kchunk), BF16),
                pltpu.VMEM((2, nk, kchunk, d), BF16),
                pltpu.VMEM((2, EXPERT_SUB, d), F32),
                pltpu.VMEM((2, EXPERT_SUB, d), F32),
                pltpu.VMEM((EXPERT_SUB, d), BF16),
                pltpu.SemaphoreType.DMA((2,)),
                pltpu.SemaphoreType.DMA((2,)),
            ],
        ),
        out_shape=jax.ShapeDtypeStruct((nrows, d), F32),
        input_output_aliases={4: 0},
        compiler_params=_params(("arbitrary", "arbitrary")),
        name="experts",
    )(bstart, w_gate, w_up, w_down, x_sorted)


def _final_kernel(dest_ref, h_ref, rw_ref, ys_ref, g_ref, yp_ref, yss_ref, ya, yb, sem, *,
                  tm, n_prompt_tiles, ts):
    i = pl.program_id(0)

    def gather(tile, n):
        slot = tile % 2

        def issue(r, carry):
            t = tile * tm + r
            _row_copy(ys_ref, ya.at[slot], dest_ref[2 * t], r, sem.at[slot]).start()
            _row_copy(ys_ref, yb.at[slot], dest_ref[2 * t + 1], r, sem.at[slot]).start()
            return carry

        lax.fori_loop(0, n, issue, 0, unroll=8)

    def emit(n, o_ref):
        slot = i % 2
        for buf in (ya, yb):
            pltpu.make_async_copy(ys_ref.at[pl.ds(0, n)], buf.at[slot, pl.ds(0, n)], sem.at[slot]).wait()
        w = rw_ref[pl.ds(0, n), :]
        h = (h_ref[pl.ds(0, n), :] + w[:, 0:1] * ya[slot, pl.ds(0, n), :]
             + w[:, 1:2] * yb[slot, pl.ds(0, n), :])
        o_ref[...] = _rms(h, g_ref[...])

    @pl.when(i == 0)
    def _():
        gather(i, tm)

    @pl.when(i + 1 < n_prompt_tiles)
    def _():
        gather(i + 1, tm)

    @pl.when(i + 1 == n_prompt_tiles)
    def _():
        gather(i + 1, ts)

    @pl.when(i < n_prompt_tiles)
    def _():
        emit(tm, yp_ref)

    @pl.when(i == n_prompt_tiles)
    def _():
        emit(ts, yss_ref)


def _final(dest, h_all, rw, y_sorted, g, tp, ts):
    t, d = h_all.shape
    tm = 256
    assert tp % tm == 0 and ts <= tm and t == tp + ts
    npt = tp // tm
    kern = functools.partial(_final_kernel, tm=tm, n_prompt_tiles=npt, ts=ts)
    return pl.pallas_call(
        kern,
        grid_spec=pltpu.PrefetchScalarGridSpec(
            num_scalar_prefetch=1,
            grid=(npt + 1,),
            in_specs=[
                pl.BlockSpec((tm, d), lambda i, dref: (i, 0)),
                pl.BlockSpec((tm, LANES), lambda i, dref: (i, 0)),
                pl.BlockSpec(memory_space=pl.ANY),
                pl.BlockSpec((1, d), lambda i, dref: (0, 0)),
            ],
            out_specs=[
                pl.BlockSpec((tm, d), lambda i, dref: (jnp.minimum(i, npt - 1), 0)),
                pl.BlockSpec((ts, d), lambda i, dref: (0, 0)),
            ],
            scratch_shapes=[pltpu.VMEM((2, tm, d), F32), pltpu.VMEM((2, tm, d), F32),
                            pltpu.SemaphoreType.DMA((2,))],
        ),
        out_shape=[jax.ShapeDtypeStruct((tp, d), F32), jax.ShapeDtypeStruct((ts, d), F32)],
        compiler_params=_params(("arbitrary",)),
        name="final",
    )(dest, h_all, rw, y_sorted, g)


def _routing_tables(eid):
    i32 = jnp.int32
    onehot = (eid[:, None] == jnp.arange(N_EXPERTS, dtype=i32)[None, :]).astype(i32)
    csum = jnp.cumsum(onehot, axis=0)
    counts = csum[-1]
    nsub = (counts + EXPERT_SUB - 1) // EXPERT_SUB
    bend = jnp.cumsum(nsub)
    bstart = bend - nsub
    seg = bstart * EXPERT_SUB
    dest = jnp.sum(onehot * (csum - 1 + seg[None, :]), axis=1).astype(i32)
    pad0 = (seg + counts).astype(i32)
    padn = (nsub * EXPERT_SUB - counts).astype(i32)
    bstart_all = jnp.concatenate([bstart, bend[-1:]]).astype(i32)
    return (dest, pad0, padn, bend[-1:].astype(i32)), bstart_all


def _group_major(a, width):
    lead = a.shape[:-1]
    g = a.shape[-1] // width
    return jnp.moveaxis(a.reshape(lead + (g, width)), -2, 0)


def kernel(x_prompt, x_sample, cache_mem_k, cache_mem_v, state_conv_a, state_conv_m, state_ssm, mem_prompt,
           norm_mix, w_in, conv_a_w, w_a_out, conv_m_w, conv_m_b, dt_bias, a_log, d_skip, ssm_norm,
           w_m_out, w_o, norm_cross, norm_mem, w_q, w_k, w_v, w_co, norm_ffn, w_rg, b_rg, w_re, b_re,
           w_gate, w_up, w_down, norm_final):
    depth = w_in.shape[0]
    assert depth == 1, "single-layer step"
    nbp, seq_p, d = x_prompt.shape
    nbs, seq_s, _ = x_sample.shape
    tp, ts = nbp * seq_p, nbs * seq_s
    n_mem = mem_prompt.shape[1]
    d_conv = conv_a_w.shape[2]
    d_inner = w_m_out.shape[1]
    nheads = dt_bias.shape[1]
    bc_w = N_GROUPS * D_STATE
    l = 0

    xp = x_prompt.reshape(tp, d)
    xs = x_sample.reshape(ts, d)
    w_in_t = jnp.swapaxes(w_in[l], 0, 1)
    col_z = 3 * d_conv
    col_x = col_z + d_inner
    col_b = col_x + d_inner
    col_dt = col_b + 2 * bc_w
    col_g = col_dt + nheads
    w_dt = w_in_t[col_dt:col_g]

    mk4, mv4 = _memory_kv(mem_prompt.reshape(nbp * n_mem, d), norm_mem[l][None], w_k[l], w_v[l], nbp)

    xnp, xns, dt_p, dt_s = _norm_dt(xp, xs, norm_mix[l][None], w_dt)
    abv_p, abv_s, ca_p, ca_s = _proj_a(xnp, xns, w_in_t, conv_a_w[l], state_conv_a[l], nbp, seq_p, d_conv)
    z_p, z_s = _proj_raw(xnp, xns, w_in_t, col_z, d_inner, GROUP_W)
    cw, cbias, scm = conv_m_w[l], conv_m_b[l][None], state_conv_m[l]
    xc_p, xc_s, cmx_p, cmx_s = _proj_conv(xnp, xns, w_in_t, cw, cbias, scm, col_x, 0, d_inner, GROUP_W,
                                           nbp, seq_p)
    bc_p, bc_s, cmb_p, cmb_s = _proj_conv(xnp, xns, w_in_t, cw, cbias, scm, col_b, d_inner, 2 * bc_w, D_STATE,
                                           nbp, seq_p)
    g_p, g_s = _proj_raw(xnp, xns, w_in_t, col_g, 2 * d, 0)

    prm = {
        "dtb": dt_bias[l][None],
        "alog": a_log[l][None],
        "dsk": _group_major(jnp.repeat(d_skip[l], HEAD_DIM)[None], GROUP_W),
        "gn": _group_major(ssm_norm[l][None], GROUP_W),
    }
    q_p = _pick(seq_p, (128, 64, 32, 16, 8))
    y_p, h_p = _ssd(z_p, xc_p, bc_p, dt_p, prm, nbp, seq_p, q_p)
    y_s, h_s = _ssd(z_s, xc_s, bc_s, dt_s, prm, nbs, seq_s, seq_s,
                    h_prev=state_ssm[l].reshape(nbs, N_GROUPS, GROUP_W, D_STATE))

    merged_p, merged_s = _merge(abv_p, abv_s, y_p, y_s, g_p, g_s, w_a_out[l],
                                w_m_out[l])
    h1_p, h1_s, hn1_p, hn1_s = _res(merged_p, merged_s, xp, xs, w_o[l], norm_cross[l][None], False, BF16)

    att_p = _attn(hn1_p, mk4, mv4, w_q[l], nbp, seq_p)
    att_s = _attn(hn1_s, cache_mem_k[l], cache_mem_v[l], w_q[l], nbs, seq_s)
    h2_all, hn2_all = _res(att_p, att_s, h1_p, h1_s, w_co[l], norm_ffn[l][None], True, F32)

    npad = LANES - N_EXPERT_GROUPS - N_EXPERTS
    w_r = jnp.concatenate([w_rg[l], w_re[l], jnp.zeros((d, npad), F32)], axis=1)
    b_r = jnp.concatenate([b_rg[l], b_re[l], jnp.zeros((npad,), F32)])[None]
    ri, rw = _router(hn2_all, w_r, b_r)
    tables, bstart = _routing_tables(ri[:, :2].reshape(-1))
    n_blocks = -(-2 * (tp + ts) // EXPERT_SUB) + N_EXPERTS
    x_sorted = _dispatch(tables, hn2_all, n_blocks)
    y_sorted = _experts(bstart, x_sorted, w_gate[l], w_up[l], w_down[l], 256)
    y_prompt, y_sample = _final(tables[0], h2_all, rw, y_sorted, norm_final[None], tp, ts)

    xh = d // N_XHEADS
    return (
        y_prompt.reshape(nbp, seq_p, d),
        y_sample.reshape(nbs, seq_s, d),
        mk4[None],
        mv4[None],
        ca_p[None],
        jnp.concatenate([cmx_p, cmb_p], axis=-1)[None],
        h_p.reshape(1, nbp, nheads, HEAD_DIM, D_STATE),
        ca_s[None],
        jnp.concatenate([cmx_s, cmb_s], axis=-1)[None],
        h_s.reshape(1, nbs, nheads, HEAD_DIM, D_STATE),
    )
```

```python
import functools

import jax
import jax.numpy as jnp
from jax import lax
from jax.experimental import pallas as pl
from jax.experimental.pallas import tpu as pltpu

F32 = jnp.float32
BF16 = jnp.bfloat16
EPS = 1e-6
LOG2_E = 1.4426950408889634

V7X_VMEM_BYTES = 64 * 1024 * 1024
VMEM_LIMIT = V7X_VMEM_BYTES - 8 * 1024 * 1024
LANES = 128
SUBLANES = 8

N_GROUPS = 8
HEADS_PER_GROUP = 8
HEAD_DIM = 64
D_STATE = 128
GROUP_W = HEADS_PER_GROUP * HEAD_DIM
N_XHEADS = 4
N_EXPERTS = 32
N_EXPERT_GROUPS = 4
EXPERTS_PER_GROUP = 8
EXPERT_SUB = 256
CONV_A_K = 3
CONV_M_K = 4

NT_DIMS = (((1,), (1,)), ((), ()))
TN_DIMS = (((0,), (0,)), ((), ()))


def _params(sem):
    return pltpu.CompilerParams(dimension_semantics=sem, vmem_limit_bytes=VMEM_LIMIT)


def _pick(n, cands):
    for c in cands:
        if n % c == 0:
            return c
    raise ValueError(f"no tile for {n} in {cands}")


def _dot(a, b):
    return jnp.dot(a, b, preferred_element_type=F32)


def _dot_nt(a, b):
    return lax.dot_general(a, b, NT_DIMS, preferred_element_type=F32)


def _rms(x, g):
    return x * lax.rsqrt(jnp.mean(x * x, axis=-1, keepdims=True) + EPS) * g


def _split3(x):
    hi = x.astype(BF16)
    r = x - hi.astype(F32)
    mid = r.astype(BF16)
    lo = (r - mid.astype(F32)).astype(BF16)
    return hi, mid, lo


def _softplus(x):
    return jnp.maximum(x, 0.0) + jnp.log1p(jnp.exp(-jnp.abs(x)))


def _norm_dt_kernel(xp_ref, xs_ref, g_ref, wdt_ref, xnp_ref, xns_ref, dtp_ref, dts_ref):
    wdt = wdt_ref[...].astype(BF16)

    def one(x_ref, xn_ref, dt_ref):
        xn = _rms(x_ref[...], g_ref[...]).astype(BF16)
        xn_ref[...] = xn
        dt_ref[...] = _dot_nt(xn, wdt)

    one(xp_ref, xnp_ref, dtp_ref)

    @pl.when(pl.program_id(0) == 0)
    def _():
        one(xs_ref, xns_ref, dts_ref)


def _norm_dt(xp, xs, g, wdt):
    tp, d = xp.shape
    ts = xs.shape[0]
    nh = wdt.shape[0]
    tm = _pick(tp, (512, 256, 128))
    return pl.pallas_call(
        _norm_dt_kernel,
        grid=(tp // tm,),
        in_specs=[
            pl.BlockSpec((tm, d), lambda i: (i, 0)),
            pl.BlockSpec((ts, d), lambda i: (0, 0)),
            pl.BlockSpec((1, d), lambda i: (0, 0)),
            pl.BlockSpec((nh, d), lambda i: (0, 0)),
        ],
        out_specs=[
            pl.BlockSpec((tm, d), lambda i: (i, 0)),
            pl.BlockSpec((ts, d), lambda i: (0, 0)),
            pl.BlockSpec((tm, nh), lambda i: (i, 0)),
            pl.BlockSpec((ts, nh), lambda i: (0, 0)),
        ],
        out_shape=[
            jax.ShapeDtypeStruct((tp, d), BF16),
            jax.ShapeDtypeStruct((ts, d), BF16),
            jax.ShapeDtypeStruct((tp, nh), F32),
            jax.ShapeDtypeStruct((ts, nh), F32),
        ],
        compiler_params=_params(("arbitrary",)),
        name="norm_dt",
    )(xp, xs, g, wdt)


def _proj_a_kernel(xp_ref, xs_ref, wb_ref, wc_ref, wh_ref, cw_ref, st_ref,
                   op_ref, os_ref, cap_ref, cas_ref,
                   wbf, ubuf, sbuf, s1buf, s2buf, *, tiles_per_batch, nb_s, l_s):
    i = pl.program_id(1)
    tm = xp_ref.shape[0]
    ts = xs_ref.shape[0]
    cw = cw_ref[...]

    @pl.when(i == 0)
    def _():
        wbf[0] = wb_ref[...].astype(BF16)
        wbf[1] = wc_ref[...].astype(BF16)
        wbf[2] = wh_ref[...].astype(BF16)
        x = xs_ref[...]
        u = _dot_nt(x, wbf[1]) * _dot_nt(x, wbf[2])
        sbuf[pl.ds(0, SUBLANES), :] = jnp.zeros((SUBLANES, u.shape[1]), F32)
        sbuf[pl.ds(SUBLANES, ts), :] = u
        s1buf[...] = jnp.zeros_like(s1buf)
        s2buf[...] = jnp.zeros_like(s2buf)
        for b in range(nb_s):
            s1buf[pl.ds(b * l_s, 1), :] = st_ref[b, pl.ds(1, 1), :]
            s2buf[pl.ds(b * l_s, 1), :] = st_ref[b, pl.ds(0, 1), :]
            s2buf[pl.ds(b * l_s + 1, 1), :] = st_ref[b, pl.ds(1, 1), :]
        rmod = lax.broadcasted_iota(jnp.int32, (ts, 1), 0) % l_s
        prev1 = jnp.where(rmod == 0, s1buf[...], sbuf[pl.ds(SUBLANES - 1, ts), :])
        prev2 = jnp.where(rmod < 2, s2buf[...], sbuf[pl.ds(SUBLANES - 2, ts), :])
        v = prev2 * cw[0:1, :] + prev1 * cw[1:2, :] + u * cw[2:3, :]
        os_ref[...] = (_dot_nt(x, wbf[0]) * v).astype(BF16)
        for b in range(nb_s):
            cas_ref[b] = sbuf[pl.ds(SUBLANES + (b + 1) * l_s - 2, 2), :]

    @pl.when(i % tiles_per_batch == 0)
    def _():
        ubuf[pl.ds(0, SUBLANES), :] = jnp.zeros((SUBLANES, ubuf.shape[1]), F32)

    x = xp_ref[...]
    u = _dot_nt(x, wbf[1]) * _dot_nt(x, wbf[2])
    ubuf[pl.ds(SUBLANES, tm), :] = u
    v = (ubuf[pl.ds(SUBLANES - 2, tm), :] * cw[0:1, :]
         + ubuf[pl.ds(SUBLANES - 1, tm), :] * cw[1:2, :] + u * cw[2:3, :])
    op_ref[...] = (_dot_nt(x, wbf[0]) * v).astype(BF16)
    ubuf[pl.ds(0, SUBLANES), :] = ubuf[pl.ds(tm, SUBLANES), :]

    @pl.when(i % tiles_per_batch == tiles_per_batch - 1)
    def _():
        cap_ref[0] = ubuf[pl.ds(SUBLANES + tm - 2, 2), :]


def _proj_a(xnp, xns, w_in, conv_w, state_s, n_batch_p, seq_p, d_conv):
    tp, d = xnp.shape
    ts = xns.shape[0]
    nb_s = state_s.shape[0]
    l_s = ts // nb_s
    tn = 512
    tm = _pick(seq_p, (1024, 512, 256, 128))
    tpb = seq_p // tm
    ncol = d_conv // tn
    kern = functools.partial(_proj_a_kernel, tiles_per_batch=tpb, nb_s=nb_s, l_s=l_s)
    return pl.pallas_call(
        kern,
        grid=(ncol, tp // tm),
        in_specs=[
            pl.BlockSpec((tm, d), lambda j, i: (i, 0)),
            pl.BlockSpec((ts, d), lambda j, i: (0, 0)),
            pl.BlockSpec((tn, d), lambda j, i: (j, 0)),
            pl.BlockSpec((tn, d), lambda j, i: (j + ncol, 0)),
            pl.BlockSpec((tn, d), lambda j, i: (j + 2 * ncol, 0)),
            pl.BlockSpec((CONV_A_K, tn), lambda j, i: (0, j)),
            pl.BlockSpec((nb_s, CONV_A_K - 1, tn), lambda j, i: (0, 0, j)),
        ],
        out_specs=[
            pl.BlockSpec((tm, tn), lambda j, i: (i, j)),
            pl.BlockSpec((ts, tn), lambda j, i: (0, j)),
            pl.BlockSpec((1, CONV_A_K - 1, tn), lambda j, i: (i // tpb, 0, j)),
            pl.BlockSpec((nb_s, CONV_A_K - 1, tn), lambda j, i: (0, 0, j)),
        ],
        out_shape=[
            jax.ShapeDtypeStruct((tp, d_conv), BF16),
            jax.ShapeDtypeStruct((ts, d_conv), BF16),
            jax.ShapeDtypeStruct((n_batch_p, CONV_A_K - 1, d_conv), F32),
            jax.ShapeDtypeStruct((nb_s, CONV_A_K - 1, d_conv), F32),
        ],
        scratch_shapes=[
            pltpu.VMEM((3, tn, d), BF16),
            pltpu.VMEM((SUBLANES + tm, tn), F32),
            pltpu.VMEM((SUBLANES + ts, tn), F32),
            pltpu.VMEM((ts, tn), F32),
            pltpu.VMEM((ts, tn), F32),
        ],
        compiler_params=_params(("arbitrary", "arbitrary")),
        name="proj_a",
    )(xnp, xns, w_in, w_in, w_in, conv_w, state_s)


def _proj_raw_kernel(xp_ref, xs_ref, w_hbm, op_ref, os_ref, wbf, wstage, wsem, *, nsplit, width, row0):
    j = pl.program_id(0)
    tn = wbf.shape[0]

    def w_copy(jj, slot):
        rows = pl.ds(pl.multiple_of(row0 + jj * tn, SUBLANES), tn)
        return pltpu.make_async_copy(w_hbm.at[rows], wstage.at[slot], wsem.at[slot])

    def emit(x_ref, o_ref):
        acc = _dot_nt(x_ref[...], wbf[...])
        if nsplit == 0:
            o_ref[...] = acc
        else:
            for s in range(nsplit):
                o_ref[s] = acc[:, s * width:(s + 1) * width]

    @pl.when(pl.program_id(1) == 0)
    def _():
        @pl.when(j == 0)
        def _():
            w_copy(0, 0).start()

        w_copy(j, j % 2).wait()

        @pl.when(j + 1 < pl.num_programs(0))
        def _():
            w_copy(j + 1, (j + 1) % 2).start()

        wbf[...] = wstage[j % 2].astype(BF16)
        emit(xs_ref, os_ref)

    emit(xp_ref, op_ref)


def _proj_raw(xnp, xns, w, col0, ncols, width):
    tp, d = xnp.shape
    ts = xns.shape[0]
    tn = 1024
    tm = _pick(tp, (1024, 512, 256, 128))
    assert col0 % SUBLANES == 0 and ncols % tn == 0
    if width == 0:
        nsplit = 0
        out_specs = [pl.BlockSpec((tm, tn), lambda j, i: (i, j)),
                     pl.BlockSpec((ts, tn), lambda j, i: (0, j))]
        out_shape = [jax.ShapeDtypeStruct((tp, ncols), F32), jax.ShapeDtypeStruct((ts, ncols), F32)]
    else:
        nsplit = tn // width
        out_specs = [pl.BlockSpec((nsplit, tm, width), lambda j, i: (j, i, 0)),
                     pl.BlockSpec((nsplit, ts, width), lambda j, i: (j, 0, 0))]
        out_shape = [jax.ShapeDtypeStruct((ncols // width, tp, width), F32),
                     jax.ShapeDtypeStruct((ncols // width, ts, width), F32)]
    kern = functools.partial(_proj_raw_kernel, nsplit=nsplit, width=width, row0=col0)
    return pl.pallas_call(
        kern,
        grid=(ncols // tn, tp // tm),
        in_specs=[
            pl.BlockSpec((tm, d), lambda j, i: (i, 0)),
            pl.BlockSpec((ts, d), lambda j, i: (0, 0)),
            pl.BlockSpec(memory_space=pl.ANY),
        ],
        out_specs=out_specs,
        out_shape=out_shape,
        scratch_shapes=[pltpu.VMEM((tn, d), BF16), pltpu.VMEM((2, tn, d), F32), pltpu.SemaphoreType.DMA((2,))],
        compiler_params=_params(("arbitrary", "arbitrary")),
        name=f"proj_raw_{col0}",
    )(xnp, xns, w)


def _proj_conv_kernel(xp_ref, xs_ref, w_ref, cw_ref, cb_ref, st_ref, op_ref, os_ref, cmp_ref, cms_ref,
                      wbf, ubuf, sbuf, fix, *, tiles_per_batch, nb_s, l_s, nsplit, width):
    i = pl.program_id(1)
    tm = xp_ref.shape[0]
    ts = xs_ref.shape[0]
    tail = CONV_M_K - 1
    cw = cw_ref[...]
    bias = cb_ref[...]

    def store(o_ref, act):
        for s in range(nsplit):
            o_ref[s] = act[:, s * width:(s + 1) * width]

    @pl.when(i == 0)
    def _():
        wbf[...] = w_ref[...].astype(BF16)
        raw = _dot_nt(xs_ref[...], wbf[...])
        sbuf[pl.ds(0, SUBLANES), :] = jnp.zeros((SUBLANES, raw.shape[1]), F32)
        sbuf[pl.ds(SUBLANES, ts), :] = raw
        fix[...] = jnp.zeros_like(fix)
        for b in range(nb_s):
            for back in range(1, tail + 1):
                for m in range(back):
                    fix[back - 1, pl.ds(b * l_s + m, 1), :] = st_ref[b, pl.ds(tail + m - back, 1), :]
        rmod = lax.broadcasted_iota(jnp.int32, (ts, 1), 0) % l_s
        acc = raw * cw[tail:tail + 1, :]
        for back in range(1, tail + 1):
            tap = jnp.where(rmod < back, fix[back - 1], sbuf[pl.ds(SUBLANES - back, ts), :])
            acc = acc + tap * cw[tail - back:tail - back + 1, :]
        store(os_ref, jax.nn.silu(acc + bias))
        for b in range(nb_s):
            cms_ref[b] = sbuf[pl.ds(SUBLANES + (b + 1) * l_s - tail, tail), :]

    @pl.when(i % tiles_per_batch == 0)
    def _():
        ubuf[pl.ds(0, SUBLANES), :] = jnp.zeros((SUBLANES, ubuf.shape[1]), F32)

    raw = _dot_nt(xp_ref[...], wbf[...])
    ubuf[pl.ds(SUBLANES, tm), :] = raw
    acc = raw * cw[tail:tail + 1, :]
    for back in range(1, tail + 1):
        acc = acc + ubuf[pl.ds(SUBLANES - back, tm), :] * cw[tail - back:tail - back + 1, :]
    store(op_ref, jax.nn.silu(acc + bias))
    ubuf[pl.ds(0, SUBLANES), :] = ubuf[pl.ds(tm, SUBLANES), :]

    @pl.when(i % tiles_per_batch == tiles_per_batch - 1)
    def _():
        cmp_ref[0] = ubuf[pl.ds(SUBLANES + tm - tail, tail), :]


def _proj_conv(xnp, xns, w, conv_w, conv_b, state_s, col0, ch0, ncols, width, n_batch_p, seq_p):
    tp, d = xnp.shape
    ts = xns.shape[0]
    nb_s = state_s.shape[0]
    l_s = ts // nb_s
    tail = CONV_M_K - 1
    tn = 1024
    tm = _pick(seq_p, (1024, 512, 256, 128))
    tpb = seq_p // tm
    assert col0 % tn == 0 and ch0 % tn == 0 and ncols % tn == 0 and l_s >= tail
    jb, cb0 = col0 // tn, ch0 // tn
    nsplit = tn // width
    kern = functools.partial(_proj_conv_kernel, tiles_per_batch=tpb, nb_s=nb_s, l_s=l_s,
                             nsplit=nsplit, width=width)
    return pl.pallas_call(
        kern,
        grid=(ncols // tn, tp // tm),
        in_specs=[
            pl.BlockSpec((tm, d), lambda j, i: (i, 0)),
            pl.BlockSpec((ts, d), lambda j, i: (0, 0)),
            pl.BlockSpec((tn, d), lambda j, i: (j + jb, 0)),
            pl.BlockSpec((CONV_M_K, tn), lambda j, i: (0, j + cb0)),
            pl.BlockSpec((1, tn), lambda j, i: (0, j + cb0)),
            pl.BlockSpec((nb_s, tail, tn), lambda j, i: (0, 0, j + cb0)),
        ],
        out_specs=[
            pl.BlockSpec((nsplit, tm, width), lambda j, i: (j, i, 0)),
            pl.BlockSpec((nsplit, ts, width), lambda j, i: (j, 0, 0)),
            pl.BlockSpec((1, tail, tn), lambda j, i: (i // tpb, 0, j)),
            pl.BlockSpec((nb_s, tail, tn), lambda j, i: (0, 0, j)),
        ],
        out_shape=[
            jax.ShapeDtypeStruct((ncols // width, tp, width), F32),
            jax.ShapeDtypeStruct((ncols // width, ts, width), F32),
            jax.ShapeDtypeStruct((n_batch_p, tail, ncols), F32),
            jax.ShapeDtypeStruct((nb_s, tail, ncols), F32),
        ],
        scratch_shapes=[
            pltpu.VMEM((tn, d), BF16),
            pltpu.VMEM((SUBLANES + tm, tn), F32),
            pltpu.VMEM((SUBLANES + ts, tn), F32),
            pltpu.VMEM((tail, ts, tn), F32),
        ],
        compiler_params=_params(("arbitrary", "arbitrary")),
        name=f"proj_conv_{col0}",
    )(xnp, xns, w, conv_w, conv_b, state_s)


def _ssd_kernel(*refs, q, has_state, nchunks):
    (z_ref, xs_ref, b_ref, c_ref, dt_ref, dtb_ref, alog_ref, dsk_ref, gn_ref, *rest) = refs
    if has_state:
        hprev, *rest = rest
    (y_ref, oh, h_s, acg, rowt) = rest
    c = pl.program_id(1)
    nheads = N_GROUPS * HEADS_PER_GROUP

    @pl.when(c == 0)
    def _init():
        if has_state:
            h_s[...] = hprev[...]
        else:
            h_s[...] = jnp.zeros_like(h_s)

    dt = _softplus(dt_ref[...] + dtb_ref[...])
    da = dt * (-jnp.exp(alog_ref[...]))
    ri = lax.broadcasted_iota(jnp.int32, (q, q), 0)
    ci = lax.broadcasted_iota(jnp.int32, (q, q), 1)
    causal = ri >= ci
    tril = jnp.where(causal, 1.0, 0.0).astype(BF16)
    acum = sum(_dot(tril, p) for p in _split3(da))
    eye = jnp.where(lax.broadcasted_iota(jnp.int32, (nheads, nheads), 0)
                    == lax.broadcasted_iota(jnp.int32, (nheads, nheads), 1), 1.0, 0.0).astype(BF16)
    acum2 = acum * LOG2_E
    rowt[0] = sum(lax.dot_general(eye, p, NT_DIMS, preferred_element_type=F32) for p in _split3(acum2))
    rowt[1] = sum(lax.dot_general(eye, p, NT_DIMS, preferred_element_type=F32) for p in _split3(dt))
    wend = jnp.exp(acum[q - 1:q, :] - acum) * dt
    rowt[2] = sum(lax.dot_general(eye, p, NT_DIMS, preferred_element_type=F32) for p in _split3(wend))
    for g in range(N_GROUPS):
        acg[g] = acum2[:, g * HEADS_PER_GROUP:(g + 1) * HEADS_PER_GROUP]

    lane = lax.broadcasted_iota(jnp.int32, (1, LANES), 1)
    rowi = lax.broadcasted_iota(jnp.int32, (LANES, 1), 0)
    half_w = LANES // 2

    def group_body(g, carry):
        xs = xs_ref[g]
        bb = b_ref[g].astype(BF16)
        ccb16 = c_ref[g].astype(BF16)
        cb_ = lax.dot_general(ccb16, bb, NT_DIMS, preferred_element_type=F32)
        ac8 = acg[g]
        dsk = dsk_ref[g]
        lo_half = lane < half_w
        lo_rows = rowi < half_w
        ys = []
        for pair in range(HEADS_PER_GROUP // 2):
            sl = slice(pair * LANES, (pair + 1) * LANES)
            xp = xs[:, sl]
            hp = h_s[g, pl.ds(pair * LANES, LANES), :]
            ms_, acols, wrows = [], [], []
            for r in (2 * pair, 2 * pair + 1):
                head = g * HEADS_PER_GROUP + r
                acol = jnp.broadcast_to(ac8[:, r:r + 1], (q, LANES))
                arow = rowt[0, pl.ds(head, 1), :]
                drow = rowt[1, pl.ds(head, 1), :]
                decay = jnp.exp2(jnp.where(causal, acol[:, :q] - arow, -jnp.inf))
                ms_.append((cb_ * decay * drow).astype(BF16))
                acols.append(acol)
                wrows.append(jnp.broadcast_to(rowt[2, pl.ds(head, 1), :], (half_w, q)))
            xlo = jnp.where(lo_half, xp, 0.0).astype(BF16)
            xhi = jnp.where(lo_half, 0.0, xp).astype(BF16)
            if q % LANES == 0:
                ydiag = _dot(jnp.concatenate(ms_, axis=1), jnp.concatenate([xlo, xhi], axis=0))
            else:
                ydiag = _dot(ms_[0], xlo) + _dot(ms_[1], xhi)
            ea = jnp.where(lo_half, jnp.exp2(acols[0]), jnp.exp2(acols[1]))
            yoff = ea * lax.dot_general(ccb16, hp.astype(BF16), NT_DIMS, preferred_element_type=F32)
            ys.append(dsk[:, sl] * xp + ydiag + yoff)
            xwt = (xp.T * jnp.concatenate(wrows, axis=0)).astype(BF16)
            dlast = [jnp.broadcast_to(jnp.exp2(a[q - 1:q, :]), (LANES, LANES)) for a in acols]
            h_s[g, pl.ds(pair * LANES, LANES), :] = hp * jnp.where(lo_rows, dlast[0], dlast[1]) + _dot(xwt, bb)
        yg = jnp.concatenate(ys, axis=1)
        hh = yg * jax.nn.silu(z_ref[g])
        ms = jnp.mean(hh * hh, axis=-1, keepdims=True)
        y_ref[g] = (hh * lax.rsqrt(ms + EPS) * gn_ref[g]).astype(BF16)
        return carry

    lax.fori_loop(0, N_GROUPS, group_body, 0, unroll=8)

    @pl.when(c == nchunks - 1)
    def _():
        oh[...] = h_s[...]


def _ssd(z, xc, bc, dt_raw, prm, n_batch, seq, q, h_prev=None):
    nchunks = seq // q
    nheads = N_GROUPS * HEADS_PER_GROUP
    has_state = h_prev is not None
    g8 = N_GROUPS

    def tok(first):
        return lambda b, c: (first, b * nchunks + c, 0)

    def const3(b, c):
        return (0, 0, 0)

    in_specs = [
        pl.BlockSpec((g8, q, GROUP_W), tok(0)),
        pl.BlockSpec((g8, q, GROUP_W), tok(0)),
        pl.BlockSpec((g8, q, D_STATE), tok(0)),
        pl.BlockSpec((g8, q, D_STATE), tok(1)),
        pl.BlockSpec((q, nheads), lambda b, c: (b * nchunks + c, 0)),
        pl.BlockSpec((1, nheads), lambda b, c: (0, 0)),
        pl.BlockSpec((1, nheads), lambda b, c: (0, 0)),
        pl.BlockSpec((g8, 1, GROUP_W), const3),
        pl.BlockSpec((g8, 1, GROUP_W), const3),
    ]
    args = [z, xc, bc, bc, dt_raw, prm["dtb"], prm["alog"], prm["dsk"], prm["gn"]]

    def per_batch(shape):
        return pl.BlockSpec((None,) + shape, lambda b, c: (b,) + (0,) * len(shape))

    if has_state:
        in_specs.append(per_batch((g8, GROUP_W, D_STATE)))
        args.append(h_prev)
    t = n_batch * seq
    out_specs = [
        pl.BlockSpec((g8, q, GROUP_W), lambda b, c: (0, b * nchunks + c, 0)),
        per_batch((g8, GROUP_W, D_STATE)),
    ]
    out_shape = [
        jax.ShapeDtypeStruct((g8, t, GROUP_W), BF16),
        jax.ShapeDtypeStruct((n_batch, g8, GROUP_W, D_STATE), F32),
    ]
    scratch = [
        pltpu.VMEM((g8, GROUP_W, D_STATE), F32),
        pltpu.VMEM((g8, q, HEADS_PER_GROUP), F32),
        pltpu.VMEM((3, nheads, q), F32),
    ]
    kern = functools.partial(_ssd_kernel, q=q, has_state=has_state, nchunks=nchunks)
    return pl.pallas_call(
        kern,
        grid=(n_batch, nchunks),
        in_specs=in_specs,
        out_specs=out_specs,
        out_shape=out_shape,
        scratch_shapes=scratch,
        compiler_params=_params(("arbitrary", "arbitrary")),
        name="ssd_state" if has_state else "ssd",
    )(*args)


def _merge_kernel(ap_ref, as_ref, yp_ref, ys_ref, gap_ref, gmp_ref, gas_ref, gms_ref, wa_ref, wm_ref,
                  op_ref, os_ref, wab, wmb):
    def emit(a_ref, y_ref, ga_ref, gm_ref, o_ref):
        oa = _dot(a_ref[...], wab[...])
        y = jnp.concatenate([y_ref[g] for g in range(N_GROUPS)], axis=1)
        om = _dot(y, wmb[...])
        o_ref[...] = (jax.nn.sigmoid(ga_ref[...]) * oa + jax.nn.sigmoid(gm_ref[...]) * om).astype(BF16)

    @pl.when(pl.program_id(1) == 0)
    def _():
        wab[...] = wa_ref[...].astype(BF16)
        wmb[...] = wm_ref[...].astype(BF16)
        emit(as_ref, ys_ref, gas_ref, gms_ref, os_ref)

    emit(ap_ref, yp_ref, gap_ref, gmp_ref, op_ref)


def _merge(abv_p, abv_s, y_p, y_s, g_p, g_s, w_a_out, w_m_out):
    tp, dc = abv_p.shape
    ts = abv_s.shape[0]
    dm = w_a_out.shape[1]
    tn = 512
    tm = _pick(tp, (512, 256, 128))
    ncol = dm // tn
    return pl.pallas_call(
        _merge_kernel,
        grid=(ncol, tp // tm),
        in_specs=[
            pl.BlockSpec((tm, dc), lambda j, i: (i, 0)),
            pl.BlockSpec((ts, dc), lambda j, i: (0, 0)),
            pl.BlockSpec((N_GROUPS, tm, GROUP_W), lambda j, i: (0, i, 0)),
            pl.BlockSpec((N_GROUPS, ts, GROUP_W), lambda j, i: (0, 0, 0)),
            pl.BlockSpec((tm, tn), lambda j, i: (i, j)),
            pl.BlockSpec((tm, tn), lambda j, i: (i, j + ncol)),
            pl.BlockSpec((ts, tn), lambda j, i: (0, j)),
            pl.BlockSpec((ts, tn), lambda j, i: (0, j + ncol)),
            pl.BlockSpec((dc, tn), lambda j, i: (0, j)),
            pl.BlockSpec((N_GROUPS * GROUP_W, tn), lambda j, i: (0, j)),
        ],
        out_specs=[
            pl.BlockSpec((tm, tn), lambda j, i: (i, j)),
            pl.BlockSpec((ts, tn), lambda j, i: (0, j)),
        ],
        out_shape=[jax.ShapeDtypeStruct((tp, dm), BF16), jax.ShapeDtypeStruct((ts, dm), BF16)],
        scratch_shapes=[pltpu.VMEM((dc, tn), BF16), pltpu.VMEM((N_GROUPS * GROUP_W, tn), BF16)],
        compiler_params=_params(("arbitrary", "arbitrary")),
        name="merge",
    )(abv_p, abv_s, y_p, y_s, g_p, g_p, g_s, g_s, w_a_out, w_m_out)


def _res_kernel(ap_ref, as_ref, rp_ref, rs_ref, w_ref, g_ref, *outs, n_prompt_tiles, merged_out, ts):
    if merged_out:
        h_ref, hn_ref, wbf = outs
    else:
        hp_ref, hs_ref, hnp_ref, hns_ref, wbf = outs
    i = pl.program_id(0)

    @pl.when(i == 0)
    def _():
        wbf[...] = w_ref[...].astype(BF16)

    def emit(a_ref, r_ref, store_h, store_hn):
        h = r_ref[...] + _dot(a_ref[...], wbf[...])
        store_h(h)
        store_hn(_rms(h, g_ref[...]))

    if merged_out:
        @pl.when(i < n_prompt_tiles)
        def _():
            def sh(h):
                h_ref[...] = h

            def shn(hn):
                hn_ref[...] = hn

            emit(ap_ref, rp_ref, sh, shn)

        @pl.when(i == n_prompt_tiles)
        def _():
            def sh(h):
                h_ref[pl.ds(0, ts), :] = h

            def shn(hn):
                hn_ref[pl.ds(0, ts), :] = hn

            emit(as_ref, rs_ref, sh, shn)
    else:
        def shp(h):
            hp_ref[...] = h

        def shnp(hn):
            hnp_ref[...] = hn.astype(hnp_ref.dtype)

        emit(ap_ref, rp_ref, shp, shnp)

        @pl.when(i == 0)
        def _():
            def shs(h):
                hs_ref[...] = h

            def shns(hn):
                hns_ref[...] = hn.astype(hns_ref.dtype)

            emit(as_ref, rs_ref, shs, shns)


def _res(a_p, a_s, r_p, r_s, w, g, merged_out, hn_dtype):
    tp, d = a_p.shape
    ts = a_s.shape[0]
    tm = _pick(tp, (512, 256))
    assert tp % tm == 0 and ts <= tm
    npt = tp // tm
    last = npt - 1
    in_specs = [
        pl.BlockSpec((tm, d), lambda i: (jnp.minimum(i, last), 0)),
        pl.BlockSpec((ts, d), lambda i: (0, 0)),
        pl.BlockSpec((tm, d), lambda i: (jnp.minimum(i, last), 0)),
        pl.BlockSpec((ts, d), lambda i: (0, 0)),
        pl.BlockSpec((d, d), lambda i: (0, 0), pipeline_mode=pl.Buffered(1)),
        pl.BlockSpec((1, d), lambda i: (0, 0)),
    ]
    if merged_out:
        grid = (npt + 1,)
        out_specs = [pl.BlockSpec((tm, d), lambda i: (i, 0)), pl.BlockSpec((tm, d), lambda i: (i, 0))]
        out_shape = [jax.ShapeDtypeStruct((tp + ts, d), F32), jax.ShapeDtypeStruct((tp + ts, d), hn_dtype)]
    else:
        grid = (npt,)
        out_specs = [pl.BlockSpec((tm, d), lambda i: (i, 0)), pl.BlockSpec((ts, d), lambda i: (0, 0)),
                     pl.BlockSpec((tm, d), lambda i: (i, 0)), pl.BlockSpec((ts, d), lambda i: (0, 0))]
        out_shape = [jax.ShapeDtypeStruct((tp, d), F32), jax.ShapeDtypeStruct((ts, d), F32),
                     jax.ShapeDtypeStruct((tp, d), hn_dtype), jax.ShapeDtypeStruct((ts, d), hn_dtype)]
    kern = functools.partial(_res_kernel, n_prompt_tiles=npt, merged_out=merged_out, ts=ts)
    return pl.pallas_call(
        kern,
        grid=grid,
        in_specs=in_specs,
        out_specs=out_specs,
        out_shape=out_shape,
        scratch_shapes=[pltpu.VMEM((d, d), BF16)],
        compiler_params=_params(("arbitrary",)),
        name="res_merged" if merged_out else "res",
    )(a_p, a_s, r_p, r_s, w, g)


def _kv_kernel(m_ref, g_ref, wk_ref, wv_ref, k_hbm, v_hbm, mn, kvb, sem, *, n_batch):
    j = pl.program_id(0)

    @pl.when(j == 0)
    def _():
        mn[...] = _rms(m_ref[...], g_ref[...]).astype(BF16)

    kvb[0] = _dot(mn[...], wk_ref[...].astype(BF16))
    kvb[1] = _dot(mn[...], wv_ref[...].astype(BF16))
    nm = kvb.shape[1] // n_batch
    for h in range(N_XHEADS):
        @pl.when(j == h)
        def _(h=h):
            copies = [pltpu.make_async_copy(kvb.at[t, pl.ds(b * nm, nm)], dst.at[b, :, h, :], sem.at[t, b])
                      for t, dst in enumerate((k_hbm, v_hbm)) for b in range(n_batch)]
            for c in copies:
                c.start()
            for c in copies:
                c.wait()


def _memory_kv(mem2d, g, w_k, w_v, n_batch):
    m, d = mem2d.shape
    dh = d // N_XHEADS
    out_sds = jax.ShapeDtypeStruct((n_batch, m // n_batch, N_XHEADS, dh), F32)
    kern = functools.partial(_kv_kernel, n_batch=n_batch)
    return pl.pallas_call(
        kern,
        grid=(N_XHEADS,),
        in_specs=[
            pl.BlockSpec((m, d), lambda j: (0, 0)),
            pl.BlockSpec((1, d), lambda j: (0, 0)),
            pl.BlockSpec((d, dh), lambda j: (0, j)),
            pl.BlockSpec((d, dh), lambda j: (0, j)),
        ],
        out_specs=[pl.BlockSpec(memory_space=pl.ANY), pl.BlockSpec(memory_space=pl.ANY)],
        out_shape=[out_sds, out_sds],
        scratch_shapes=[pltpu.VMEM((m, d), BF16), pltpu.VMEM((2, m, dh), F32),
                        pltpu.SemaphoreType.DMA((2, n_batch))],
        compiler_params=_params(("arbitrary",)),
        name="memory_kv",
    )(mem2d, g, w_k, w_v)


def _attn_kernel(hn_ref, k_ref, v_ref, wq_ref, o_ref, wqb, kb, vb, kvf, sem, *q_all):
    b = pl.program_id(0)
    i = pl.program_id(1)
    tm = o_ref.shape[0]

    @pl.when((b == 0) & (i == 0))
    def _():
        wqb[...] = wq_ref[...].astype(BF16)
        if q_all:
            q_all[0][...] = _dot(hn_ref[...], wqb[...])

    d = wqb.shape[1]
    dh = d // N_XHEADS

    @pl.when(i == 0)
    def _():
        def copies(bb):
            slot = bb % 2
            return [pltpu.make_async_copy(src.at[bb, :, h, :], kvf.at[slot, t, h], sem.at[slot, t, h])
                    for t, src in enumerate((k_ref, v_ref)) for h in range(N_XHEADS)]

        @pl.when(b == 0)
        def _():
            for c in copies(b):
                c.start()

        for c in copies(b):
            c.wait()

        @pl.when(b + 1 < pl.num_programs(0))
        def _():
            for c in copies(b + 1):
                c.start()

        for h in range(N_XHEADS):
            kb[:, h * dh:(h + 1) * dh] = kvf[b % 2, 0, h].astype(BF16)
            vb[:, h * dh:(h + 1) * dh] = kvf[b % 2, 1, h].astype(BF16)

    if q_all:
        row0 = pl.multiple_of((b * pl.num_programs(1) + i) * tm, SUBLANES)
        q = q_all[0][pl.ds(row0, tm), :]
    else:
        q = _dot(hn_ref[...], wqb[...])
    outs = []
    for h in range(N_XHEADS):
        sl = slice(h * dh, (h + 1) * dh)
        s = lax.dot_general(q[:, sl].astype(BF16), kb[:, sl], NT_DIMS, preferred_element_type=F32)
        s = s * (dh ** -0.5)
        e = jnp.exp(s - jnp.max(s, axis=-1, keepdims=True))
        p = e / jnp.sum(e, axis=-1, keepdims=True)
        outs.append(_dot(p.astype(BF16), vb[:, sl]))
    o_ref[...] = jnp.concatenate(outs, axis=1).astype(BF16)


def _attn(hn, k, v, w_q, n_batch, seq):
    t, d = hn.shape
    nm, nh, dh = k.shape[1:]
    tm = _pick(seq, (512, 256, 128, 64, 32, 16))
    tpb = seq // tm
    kv_spec = pl.BlockSpec(memory_space=pl.ANY)
    scratch = [pltpu.VMEM((d, d), BF16), pltpu.VMEM((nm, d), BF16), pltpu.VMEM((nm, d), BF16),
               pltpu.VMEM((2, 2, nh, nm, dh), F32), pltpu.SemaphoreType.DMA((2, 2, nh))]
    if tm < 128 and t <= 512:
        hn_spec = pl.BlockSpec((t, d), lambda b, i: (0, 0))
        scratch.append(pltpu.VMEM((t, d), F32))
    else:
        hn_spec = pl.BlockSpec((tm, d), lambda b, i: (b * tpb + i, 0))
    return pl.pallas_call(
        _attn_kernel,
        grid=(n_batch, tpb),
        in_specs=[
            hn_spec,
            kv_spec,
            kv_spec,
            pl.BlockSpec((d, d), lambda b, i: (0, 0), pipeline_mode=pl.Buffered(1)),
        ],
        out_specs=pl.BlockSpec((tm, d), lambda b, i: (b * tpb + i, 0)),
        out_shape=jax.ShapeDtypeStruct((t, d), BF16),
        scratch_shapes=scratch,
        compiler_params=_params(("arbitrary", "arbitrary")),
        name=f"attn_{seq}",
    )(hn, k, v, w_q)


def _router_kernel(x_ref, w_ref, b_ref, ri_ref, rw_ref):
    logits = _dot(x_ref[...].astype(BF16), w_ref[...].astype(BF16)) + b_ref[...]
    lane_i = lax.broadcasted_iota(jnp.int32, logits.shape, 1)
    lane = lane_i.astype(F32)
    ninf = -jnp.inf
    big = float(LANES)
    is_g = lane < N_EXPERT_GROUPS
    gl = jnp.where(is_g, logits, ninf)
    gmax = jnp.max(gl, axis=-1, keepdims=True)
    gsel = jnp.min(jnp.where(gl == gmax, lane, big), axis=-1, keepdims=True)
    pg = 1.0 / jnp.sum(jnp.where(is_g, jnp.exp(gl - gmax), 0.0), axis=-1, keepdims=True)
    lo = N_EXPERT_GROUPS + EXPERTS_PER_GROUP * gsel
    el = jnp.where(lane >= lo, jnp.where(lane < lo + EXPERTS_PER_GROUP, logits, ninf), ninf)
    m1 = jnp.max(el, axis=-1, keepdims=True)
    i1 = jnp.min(jnp.where(el == m1, lane, big), axis=-1, keepdims=True)
    el2 = jnp.where(lane == i1, ninf, el)
    m2 = jnp.max(el2, axis=-1, keepdims=True)
    i2 = jnp.min(jnp.where(el2 == m2, lane, big), axis=-1, keepdims=True)
    e = jnp.exp(m2 - m1)
    w1 = pg / (1.0 + e)
    w2 = pg * e / (1.0 + e)
    ri_ref[...] = jnp.where(lane_i == 0, i1 - N_EXPERT_GROUPS,
                            jnp.where(lane_i == 1, i2 - N_EXPERT_GROUPS, 0.0)).astype(jnp.int32)
    rw_ref[...] = jnp.where(lane_i == 0, w1, jnp.where(lane_i == 1, w2, 0.0))


def _router(hn_all, w_r, b_r):
    t, d = hn_all.shape
    tm = _pick(t, (640, 512, 384, 256, 128))
    return pl.pallas_call(
        _router_kernel,
        grid=(t // tm,),
        in_specs=[
            pl.BlockSpec((tm, d), lambda i: (i, 0)),
            pl.BlockSpec((d, LANES), lambda i: (0, 0)),
            pl.BlockSpec((1, LANES), lambda i: (0, 0)),
        ],
        out_specs=[pl.BlockSpec((tm, LANES), lambda i: (i, 0)), pl.BlockSpec((tm, LANES), lambda i: (i, 0))],
        out_shape=[jax.ShapeDtypeStruct((t, LANES), jnp.int32), jax.ShapeDtypeStruct((t, LANES), F32)],
        compiler_params=_params(("arbitrary",)),
        name="router",
    )(hn_all, w_r, b_r)


def _row_copy(src, dst, s, d, sem):
    return pltpu.make_async_copy(src.at[pl.ds(s, 1)], dst.at[pl.ds(d, 1)], sem)


def _dispatch_kernel(dest_ref, pad0_ref, padn_ref, nsub_ref, x_ref, o_ref, zrow, zblk, sem, zsem, csem, bsem, *,
                     chunk, n_blocks):
    base = pl.program_id(0) * chunk

    @pl.when(pl.program_id(0) == 0)
    def _():
        zrow[...] = jnp.zeros_like(zrow)
        zblk[...] = jnp.zeros_like(zblk)

        def pad_plan(e):
            p0 = pad0_ref[e]
            head = jnp.minimum(padn_ref[e], (SUBLANES - p0 % SUBLANES) % SUBLANES)
            return p0, head, (padn_ref[e] - head) // SUBLANES

        def group_copy(row):
            return pltpu.make_async_copy(zrow, o_ref.at[pl.ds(pl.multiple_of(row, SUBLANES), SUBLANES)], csem)

        def pad_start(e, carry):
            p0, head, ngroups = pad_plan(e)

            def start_row(r, c):
                _row_copy(zrow, o_ref, 0, p0 + r, zsem).start()
                return c

            def start_group(j, c):
                group_copy(p0 + head + j * SUBLANES).start()
                return c

            lax.fori_loop(0, head, start_row, 0)
            return lax.fori_loop(0, ngroups, start_group, carry)

        def pad_wait(e, carry):
            _, head, ngroups = pad_plan(e)

            def wait_row(r, c):
                _row_copy(zrow, o_ref, 0, 0, zsem).wait()
                return c

            def wait_group(j, c):
                group_copy(0).wait()
                return c

            lax.fori_loop(0, head, wait_row, 0)
            return lax.fori_loop(0, ngroups, wait_group, carry)

        lax.fori_loop(0, N_EXPERTS, pad_start, 0)

        def blk_copy(b):
            return pltpu.make_async_copy(zblk, o_ref.at[pl.ds(pl.multiple_of(b * EXPERT_SUB, EXPERT_SUB),
                                                             EXPERT_SUB)], bsem)

        def tail_start(b, c):
            blk_copy(b).start()
            return c

        def tail_wait(b, c):
            blk_copy(b).wait()
            return c

        lax.fori_loop(nsub_ref[0], n_blocks, tail_start, 0)
        lax.fori_loop(0, N_EXPERTS, pad_wait, 0)
        lax.fori_loop(nsub_ref[0], n_blocks, tail_wait, 0)

    def issue(r, carry):
        t = base + r
        _row_copy(x_ref, o_ref, r, dest_ref[2 * t], sem).start()
        _row_copy(x_ref, o_ref, r, dest_ref[2 * t + 1], sem).start()
        return carry

    lax.fori_loop(0, chunk, issue, 0, unroll=8)

    for _ in range(2):
        pltpu.make_async_copy(x_ref, o_ref.at[pl.ds(0, chunk)], sem).wait()


def _dispatch(tables, hn_all, n_blocks):
    dest, pad0, padn, nsub = tables
    t, d = hn_all.shape
    chunk = _pick(t, (640, 512, 384, 256, 128))
    kern = functools.partial(_dispatch_kernel, chunk=chunk, n_blocks=n_blocks)
    return pl.pallas_call(
        kern,
        grid_spec=pltpu.PrefetchScalarGridSpec(
            num_scalar_prefetch=4,
            grid=(t // chunk,),
            in_specs=[pl.BlockSpec((chunk, d), lambda i, *_: (i, 0))],
            out_specs=pl.BlockSpec(memory_space=pl.ANY),
            scratch_shapes=[pltpu.VMEM((SUBLANES, d), F32), pltpu.VMEM((EXPERT_SUB, d), F32),
                            pltpu.SemaphoreType.DMA(()), pltpu.SemaphoreType.DMA(()),
                            pltpu.SemaphoreType.DMA(()), pltpu.SemaphoreType.DMA(())],
        ),
        out_shape=jax.ShapeDtypeStruct((n_blocks * EXPERT_SUB, d), F32),
        compiler_params=_params(("arbitrary",)),
        name="dispatch",
    )(dest, pad0, padn, nsub, hn_all)


def _expert_kernel(bstart_ref, wg_ref, wu_ref, wd_ref, x_hbm, y_hbm, gcache, ucache, dcache, xbuf, ybuf, xb,
                   xsem, ysem):
    e = pl.program_id(0)
    k = pl.program_id(1)
    nk = gcache.shape[1]
    total = bstart_ref[N_EXPERTS]

    def x_copy(b, slot):
        rows = pl.ds(pl.multiple_of(b * EXPERT_SUB, EXPERT_SUB), EXPERT_SUB)
        return pltpu.make_async_copy(x_hbm.at[rows], xbuf.at[slot], xsem.at[slot])

    def y_copy(b, slot):
        rows = pl.ds(pl.multiple_of(b * EXPERT_SUB, EXPERT_SUB), EXPERT_SUB)
        return pltpu.make_async_copy(ybuf.at[slot], y_hbm.at[rows], ysem.at[slot])

    @pl.when(e < N_EXPERTS)
    def _():
        slot = e % 2
        gcache[slot, k] = wg_ref[...].astype(BF16)
        ucache[slot, k] = wu_ref[...].astype(BF16)
        dcache[slot, k] = wd_ref[...].astype(BF16)

    @pl.when((e == 0) & (k == 0) & (total > 0))
    def _():
        x_copy(0, 0).start()

    @pl.when(e >= 1)
    def _():
        owner = e - 1
        wslot = owner % 2
        b0 = bstart_ref[owner]
        n = bstart_ref[owner + 1] - b0

        cpu = 2
        upb = nk // cpu

        def unit(u, carry):
            b = b0 + u // upb
            part = u % upb
            slot = b % 2

            def swiglu(x, p):
                kks = [p * cpu + c for c in range(cpu)]
                wg = jnp.concatenate([gcache[wslot, kk] for kk in kks], axis=1)
                wu = jnp.concatenate([ucache[wslot, kk] for kk in kks], axis=1)
                wd = jnp.concatenate([dcache[wslot, kk] for kk in kks], axis=0)
                hid = (jax.nn.silu(_dot(x, wg)) * _dot(x, wu)).astype(BF16)
                return _dot(hid, wd)

            @pl.when(part == 0)
            def _():
                x_copy(b, slot).wait()

                @pl.when(b + 1 < total)
                def _():
                    x_copy(b + 1, 1 - slot).start()

                x = xbuf[slot].astype(BF16)
                xb[...] = x

                @pl.when(b >= 2)
                def _():
                    y_copy(b - 2, slot).wait()

                ybuf[slot] = swiglu(x, 0)

            for p in range(1, upb):
                @pl.when(part == p)
                def _(p=p):
                    ybuf[slot] += swiglu(xb[...], p)
                    if p == upb - 1:
                        y_copy(b, slot).start()

            return carry

        lax.fori_loop((n * upb * k) // nk, (n * upb * (k + 1)) // nk, unit, 0)

    @pl.when((e == N_EXPERTS) & (k == nk - 1))
    def _():
        @pl.when(total >= 2)
        def _():
            y_copy(total - 2, total % 2).wait()

        @pl.when(total >= 1)
        def _():
            y_copy(total - 1, (total - 1) % 2).wait()


def _experts(bstart, x_sorted, w_gate, w_up, w_down, kchunk):
    nrows, d = x_sorted.shape
    de = w_gate.shape[2]
    nk = de // kchunk
    last = N_EXPERTS - 1

    def widx(e, k):
        return jnp.minimum(e, last), jnp.where(e <= last, k, nk - 1)

    def in_idx(e, k, b):
        ee, kk = widx(e, k)
        return (ee, 0, kk)

    def down_idx(e, k, b):
        ee, kk = widx(e, k)
        return (ee, kk, 0)

    return pl.pallas_call(
        _expert_kernel,
        grid_spec=pltpu.PrefetchScalarGridSpec(
            num_scalar_prefetch=1,
            grid=(N_EXPERTS + 1, nk),
            in_specs=[
                pl.BlockSpec((None, d, kchunk), in_idx),
                pl.BlockSpec((None, d, kchunk), in_idx),
                pl.BlockSpec((None, kchunk, d), down_idx),
                pl.BlockSpec(memory_space=pl.ANY),
            ],
            out_specs=pl.BlockSpec(memory_space=pl.ANY),
            scratch_shapes=[
                pltpu.VMEM((2, nk, d, kchunk), BF16),
                pltpu.VMEM((2, nk, d, kchunk), BF16),
                pltpu.VMEM((2, nk, kchunk, d), BF16),
                pltpu.VMEM((2, EXPERT_SUB, d), F32),
                pltpu.VMEM((2, EXPERT_SUB, d), F32),
                pltpu.VMEM((EXPERT_SUB, d), BF16),
                pltpu.SemaphoreType.DMA((2,)),
                pltpu.SemaphoreType.DMA((2,)),
            ],
        ),
        out_shape=jax.ShapeDtypeStruct((nrows, d), F32),
        input_output_aliases={4: 0},
        compiler_params=_params(("arbitrary", "arbitrary")),
        name="experts",
    )(bstart, w_gate, w_up, w_down, x_sorted)


def _final_kernel(dest_ref, h_ref, rw_ref, ys_ref, g_ref, yp_ref, yss_ref, ya, yb, sem, *,
                  tm, n_prompt_tiles, ts):
    i = pl.program_id(0)

    def gather(tile, n):
        slot = tile % 2

        def issue(r, carry):
            t = tile * tm + r
            _row_copy(ys_ref, ya.at[slot], dest_ref[2 * t], r, sem.at[slot]).start()
            _row_copy(ys_ref, yb.at[slot], dest_ref[2 * t + 1], r, sem.at[slot]).start()
            return carry

        lax.fori_loop(0, n, issue, 0, unroll=8)

    def emit(n, o_ref):
        slot = i % 2
        for buf in (ya, yb):
            pltpu.make_async_copy(ys_ref.at[pl.ds(0, n)], buf.at[slot, pl.ds(0, n)], sem.at[slot]).wait()
        w = rw_ref[pl.ds(0, n), :]
        h = (h_ref[pl.ds(0, n), :] + w[:, 0:1] * ya[slot, pl.ds(0, n), :]
             + w[:, 1:2] * yb[slot, pl.ds(0, n), :])
        o_ref[...] = _rms(h, g_ref[...])

    @pl.when(i == 0)
    def _():
        gather(i, tm)

    @pl.when(i + 1 < n_prompt_tiles)
    def _():
        gather(i + 1, tm)

    @pl.when(i + 1 == n_prompt_tiles)
    def _():
        gather(i + 1, ts)

    @pl.when(i < n_prompt_tiles)
    def _():
        emit(tm, yp_ref)

    @pl.when(i == n_prompt_tiles)
    def _():
        emit(ts, yss_ref)


def _final(dest, h_all, rw, y_sorted, g, tp, ts):
    t, d = h_all.shape
    tm = 256
    assert tp % tm == 0 and ts <= tm and t == tp + ts
    npt = tp // tm
    kern = functools.partial(_final_kernel, tm=tm, n_prompt_tiles=npt, ts=ts)
    return pl.pallas_call(
        kern,
        grid_spec=pltpu.PrefetchScalarGridSpec(
            num_scalar_prefetch=1,
            grid=(npt + 1,),
            in_specs=[
                pl.BlockSpec((tm, d), lambda i, dref: (i, 0)),
                pl.BlockSpec((tm, LANES), lambda i, dref: (i, 0)),
                pl.BlockSpec(memory_space=pl.ANY),
                pl.BlockSpec((1, d), lambda i, dref: (0, 0)),
            ],
            out_specs=[
                pl.BlockSpec((tm, d), lambda i, dref: (jnp.minimum(i, npt - 1), 0)),
                pl.BlockSpec((ts, d), lambda i, dref: (0, 0)),
            ],
            scratch_shapes=[pltpu.VMEM((2, tm, d), F32), pltpu.VMEM((2, tm, d), F32),
                            pltpu.SemaphoreType.DMA((2,))],
        ),
        out_shape=[jax.ShapeDtypeStruct((tp, d), F32), jax.ShapeDtypeStruct((ts, d), F32)],
        compiler_params=_params(("arbitrary",)),
        name="final",
    )(dest, h_all, rw, y_sorted, g)


def _routing_tables(eid):
    i32 = jnp.int32
    onehot = (eid[:, None] == jnp.arange(N_EXPERTS, dtype=i32)[None, :]).astype(i32)
    csum = jnp.cumsum(onehot, axis=0)
    counts = csum[-1]
    nsub = (counts + EXPERT_SUB - 1) // EXPERT_SUB
    bend = jnp.cumsum(nsub)
    bstart = bend - nsub
    seg = bstart * EXPERT_SUB
    dest = jnp.sum(onehot * (csum - 1 + seg[None, :]), axis=1).astype(i32)
    pad0 = (seg + counts).astype(i32)
    padn = (nsub * EXPERT_SUB - counts).astype(i32)
    bstart_all = jnp.concatenate([bstart, bend[-1:]]).astype(i32)
    return (dest, pad0, padn, bend[-1:].astype(i32)), bstart_all


def _group_major(a, width):
    lead = a.shape[:-1]
    g = a.shape[-1] // width
    return jnp.moveaxis(a.reshape(lead + (g, width)), -2, 0)


def kernel(x_prompt, x_sample, cache_mem_k, cache_mem_v, state_conv_a, state_conv_m, state_ssm, mem_prompt,
           norm_mix, w_in, conv_a_w, w_a_out, conv_m_w, conv_m_b, dt_bias, a_log, d_skip, ssm_norm,
           w_m_out, w_o, norm_cross, norm_mem, w_q, w_k, w_v, w_co, norm_ffn, w_rg, b_rg, w_re, b_re,
           w_gate, w_up, w_down, norm_final):
    depth = w_in.shape[0]
    assert depth == 1, "single-layer step"
    nbp, seq_p, d = x_prompt.shape
    nbs, seq_s, _ = x_sample.shape
    tp, ts = nbp * seq_p, nbs * seq_s
    n_mem = mem_prompt.shape[1]
    d_conv = conv_a_w.shape[2]
    d_inner = w_m_out.shape[1]
    nheads = dt_bias.shape[1]
    bc_w = N_GROUPS * D_STATE
    l = 0

    xp = x_prompt.reshape(tp, d)
    xs = x_sample.reshape(ts, d)
    w_in_t = jnp.swapaxes(w_in[l], 0, 1)
    col_z = 3 * d_conv
    col_x = col_z + d_inner
    col_b = col_x + d_inner
    col_dt = col_b + 2 * bc_w
    col_g = col_dt + nheads
    w_dt = w_in_t[col_dt:col_g]

    mk4, mv4 = _memory_kv(mem_prompt.reshape(nbp * n_mem, d), norm_mem[l][None], w_k[l], w_v[l], nbp)

    xnp, xns, dt_p, dt_s = _norm_dt(xp, xs, norm_mix[l][None], w_dt)
    abv_p, abv_s, ca_p, ca_s = _proj_a(xnp, xns, w_in_t, conv_a_w[l], state_conv_a[l], nbp, seq_p, d_conv)
    z_p, z_s = _proj_raw(xnp, xns, w_in_t, col_z, d_inner, GROUP_W)
    cw, cbias, scm = conv_m_w[l], conv_m_b[l][None], state_conv_m[l]
    xc_p, xc_s, cmx_p, cmx_s = _proj_conv(xnp, xns, w_in_t, cw, cbias, scm, col_x, 0, d_inner, GROUP_W,
                                           nbp, seq_p)
    bc_p, bc_s, cmb_p, cmb_s = _proj_conv(xnp, xns, w_in_t, cw, cbias, scm, col_b, d_inner, 2 * bc_w, D_STATE,
                                           nbp, seq_p)
    g_p, g_s = _proj_raw(xnp, xns, w_in_t, col_g, 2 * d, 0)

    prm = {
        "dtb": dt_bias[l][None],
        "alog": a_log[l][None],
        "dsk": _group_major(jnp.repeat(d_skip[l], HEAD_DIM)[None], GROUP_W),
        "gn": _group_major(ssm_norm[l][None], GROUP_W),
    }
    q_p = _pick(seq_p, (128, 64, 32, 16, 8))
    y_p, h_p = _ssd(z_p, xc_p, bc_p, dt_p, prm, nbp, seq_p, q_p)
    y_s, h_s = _ssd(z_s, xc_s, bc_s, dt_s, prm, nbs, seq_s, seq_s,
                    h_prev=state_ssm[l].reshape(nbs, N_GROUPS, GROUP_W, D_STATE))

    merged_p, merged_s = _merge(abv_p, abv_s, y_p, y_s, g_p, g_s, w_a_out[l],
                                w_m_out[l])
    h1_p, h1_s, hn1_p, hn1_s = _res(merged_p, merged_s, xp, xs, w_o[l], norm_cross[l][None], False, BF16)

    att_p = _attn(hn1_p, mk4, mv4, w_q[l], nbp, seq_p)
    att_s = _attn(hn1_s, cache_mem_k[l], cache_mem_v[l], w_q[l], nbs, seq_s)
    h2_all, hn2_all = _res(att_p, att_s, h1_p, h1_s, w_co[l], norm_ffn[l][None], True, F32)

    npad = LANES - N_EXPERT_GROUPS - N_EXPERTS
    w_r = jnp.concatenate([w_rg[l], w_re[l], jnp.zeros((d, npad), F32)], axis=1)
    b_r = jnp.concatenate([b_rg[l], b_re[l], jnp.zeros((npad,), F32)])[None]
    ri, rw = _router(hn2_all, w_r, b_r)
    tables, bstart = _routing_tables(ri[:, :2].reshape(-1))
    n_blocks = -(-2 * (tp + ts) // EXPERT_SUB) + N_EXPERTS
    x_sorted = _dispatch(tables, hn2_all, n_blocks)
    y_sorted = _experts(bstart, x_sorted, w_gate[l], w_up[l], w_down[l], 256)
    y_prompt, y_sample = _final(tables[0], h2_all, rw, y_sorted, norm_final[None], tp, ts)

    xh = d // N_XHEADS
    return (
        y_prompt.reshape(nbp, seq_p, d),
        y_sample.reshape(nbs, seq_s, d),
        mk4[None],
        mv4[None],
        ca_p[None],
        jnp.concatenate([cmx_p, cmb_p], axis=-1)[None],
        h_p.reshape(1, nbp, nheads, HEAD_DIM, D_STATE),
        ca_s[None],
        jnp.concatenate([cmx_s, cmb_s], axis=-1)[None],
        h_s.reshape(1, nbs, nheads, HEAD_DIM, D_STATE),
    )
```

```python
import functools

import jax
import jax.numpy as jnp
from jax import lax
from jax.experimental import pallas as pl
from jax.experimental.pallas import tpu as pltpu

F32 = jnp.float32
BF16 = jnp.bfloat16
EPS = 1e-6
LOG2_E = 1.4426950408889634

V7X_VMEM_BYTES = 64 * 1024 * 1024
VMEM_LIMIT = V7X_VMEM_BYTES - 8 * 1024 * 1024
LANES = 128
SUBLANES = 8

N_GROUPS = 8
HEADS_PER_GROUP = 8
HEAD_DIM = 64
D_STATE = 128
GROUP_W = HEADS_PER_GROUP * HEAD_DIM
N_XHEADS = 4
N_EXPERTS = 32
N_EXPERT_GROUPS = 4
EXPERTS_PER_GROUP = 8
EXPERT_SUB = 256
CONV_A_K = 3
CONV_M_K = 4

NT_DIMS = (((1,), (1,)), ((), ()))
TN_DIMS = (((0,), (0,)), ((), ()))


def _params(sem):
    return pltpu.CompilerParams(dimension_semantics=sem, vmem_limit_bytes=VMEM_LIMIT)


def _pick(n, cands):
    for c in cands:
        if n % c == 0:
            return c
    raise ValueError(f"no tile for {n} in {cands}")


def _dot(a, b):
    return jnp.dot(a, b, preferred_element_type=F32)


def _dot_nt(a, b):
    return lax.dot_general(a, b, NT_DIMS, preferred_element_type=F32)


def _rms(x, g):
    return x * lax.rsqrt(jnp.mean(x * x, axis=-1, keepdims=True) + EPS) * g


def _split3(x):
    hi = x.astype(BF16)
    r = x - hi.astype(F32)
    mid = r.astype(BF16)
    lo = (r - mid.astype(F32)).astype(BF16)
    return hi, mid, lo


def _softplus(x):
    return jnp.maximum(x, 0.0) + jnp.log1p(jnp.exp(-jnp.abs(x)))


def _norm_dt_kernel(xp_ref, xs_ref, g_ref, wdt_ref, xnp_ref, xns_ref, dtp_ref, dts_ref):
    wdt = wdt_ref[...].astype(BF16)

    def one(x_ref, xn_ref, dt_ref):
        xn = _rms(x_ref[...], g_ref[...]).astype(BF16)
        xn_ref[...] = xn
        dt_ref[...] = _dot_nt(xn, wdt)

    one(xp_ref, xnp_ref, dtp_ref)

    @pl.when(pl.program_id(0) == 0)
    def _():
        one(xs_ref, xns_ref, dts_ref)


def _norm_dt(xp, xs, g, wdt):
    tp, d = xp.shape
    ts = xs.shape[0]
    nh = wdt.shape[0]
    tm = _pick(tp, (512, 256, 128))
    return pl.pallas_call(
        _norm_dt_kernel,
        grid=(tp // tm,),
        in_specs=[
            pl.BlockSpec((tm, d), lambda i: (i, 0)),
            pl.BlockSpec((ts, d), lambda i: (0, 0)),
            pl.BlockSpec((1, d), lambda i: (0, 0)),
            pl.BlockSpec((nh, d), lambda i: (0, 0)),
        ],
        out_specs=[
            pl.BlockSpec((tm, d), lambda i: (i, 0)),
            pl.BlockSpec((ts, d), lambda i: (0, 0)),
            pl.BlockSpec((tm, nh), lambda i: (i, 0)),
            pl.BlockSpec((ts, nh), lambda i: (0, 0)),
        ],
        out_shape=[
            jax.ShapeDtypeStruct((tp, d), BF16),
            jax.ShapeDtypeStruct((ts, d), BF16),
            jax.ShapeDtypeStruct((tp, nh), F32),
            jax.ShapeDtypeStruct((ts, nh), F32),
        ],
        compiler_params=_params(("arbitrary",)),
        name="norm_dt",
    )(xp, xs, g, wdt)


def _proj_a_kernel(xp_ref, xs_ref, wb_ref, wc_ref, wh_ref, cw_ref, st_ref,
                   op_ref, os_ref, cap_ref, cas_ref,
                   wbf, ubuf, sbuf, s1buf, s2buf, *, tiles_per_batch, nb_s, l_s):
    i = pl.program_id(1)
    tm = xp_ref.shape[0]
    ts = xs_ref.shape[0]
    cw = cw_ref[...]

    @pl.when(i == 0)
    def _():
        wbf[0] = wb_ref[...].astype(BF16)
        wbf[1] = wc_ref[...].astype(BF16)
        wbf[2] = wh_ref[...].astype(BF16)
        x = xs_ref[...]
        u = _dot_nt(x, wbf[1]) * _dot_nt(x, wbf[2])
        sbuf[pl.ds(0, SUBLANES), :] = jnp.zeros((SUBLANES, u.shape[1]), F32)
        sbuf[pl.ds(SUBLANES, ts), :] = u
        s1buf[...] = jnp.zeros_like(s1buf)
        s2buf[...] = jnp.zeros_like(s2buf)
        for b in range(nb_s):
            s1buf[pl.ds(b * l_s, 1), :] = st_ref[b, pl.ds(1, 1), :]
            s2buf[pl.ds(b * l_s, 1), :] = st_ref[b, pl.ds(0, 1), :]
            s2buf[pl.ds(b * l_s + 1, 1), :] = st_ref[b, pl.ds(1, 1), :]
        rmod = lax.broadcasted_iota(jnp.int32, (ts, 1), 0) % l_s
        prev1 = jnp.where(rmod == 0, s1buf[...], sbuf[pl.ds(SUBLANES - 1, ts), :])
        prev2 = jnp.where(rmod < 2, s2buf[...], sbuf[pl.ds(SUBLANES - 2, ts), :])
        v = prev2 * cw[0:1, :] + prev1 * cw[1:2, :] + u * cw[2:3, :]
        os_ref[...] = (_dot_nt(x, wbf[0]) * v).astype(BF16)
        for b in range(nb_s):
            cas_ref[b] = sbuf[pl.ds(SUBLANES + (b + 1) * l_s - 2, 2), :]

    @pl.when(i % tiles_per_batch == 0)
    def _():
        ubuf[pl.ds(0, SUBLANES), :] = jnp.zeros((SUBLANES, ubuf.shape[1]), F32)

    x = xp_ref[...]
    u = _dot_nt(x, wbf[1]) * _dot_nt(x, wbf[2])
    ubuf[pl.ds(SUBLANES, tm), :] = u
    v = (ubuf[pl.ds(SUBLANES - 2, tm), :] * cw[0:1, :]
         + ubuf[pl.ds(SUBLANES - 1, tm), :] * cw[1:2, :] + u * cw[2:3, :])
    op_ref[...] = (_dot_nt(x, wbf[0]) * v).astype(BF16)
    ubuf[pl.ds(0, SUBLANES), :] = ubuf[pl.ds(tm, SUBLANES), :]

    @pl.when(i % tiles_per_batch == tiles_per_batch - 1)
    def _():
        cap_ref[0] = ubuf[pl.ds(SUBLANES + tm - 2, 2), :]


def _proj_a(xnp, xns, w_in, conv_w, state_s, n_batch_p, seq_p, d_conv):
    tp, d = xnp.shape
    ts = xns.shape[0]
    nb_s = state_s.shape[0]
    l_s = ts // nb_s
    tn = 512
    tm = _pick(seq_p, (1024, 512, 256, 128))
    tpb = seq_p // tm
    ncol = d_conv // tn
    kern = functools.partial(_proj_a_kernel, tiles_per_batch=tpb, nb_s=nb_s, l_s=l_s)
    return pl.pallas_call(
        kern,
        grid=(ncol, tp // tm),
        in_specs=[
            pl.BlockSpec((tm, d), lambda j, i: (i, 0)),
            pl.BlockSpec((ts, d), lambda j, i: (0, 0)),
            pl.BlockSpec((tn, d), lambda j, i: (j, 0)),
            pl.BlockSpec((tn, d), lambda j, i: (j + ncol, 0)),
            pl.BlockSpec((tn, d), lambda j, i: (j + 2 * ncol, 0)),
            pl.BlockSpec((CONV_A_K, tn), lambda j, i: (0, j)),
            pl.BlockSpec((nb_s, CONV_A_K - 1, tn), lambda j, i: (0, 0, j)),
        ],
        out_specs=[
            pl.BlockSpec((tm, tn), lambda j, i: (i, j)),
            pl.BlockSpec((ts, tn), lambda j, i: (0, j)),
            pl.BlockSpec((1, CONV_A_K - 1, tn), lambda j, i: (i // tpb, 0, j)),
            pl.BlockSpec((nb_s, CONV_A_K - 1, tn), lambda j, i: (0, 0, j)),
        ],
        out_shape=[
            jax.ShapeDtypeStruct((tp, d_conv), BF16),
            jax.ShapeDtypeStruct((ts, d_conv), BF16),
            jax.ShapeDtypeStruct((n_batch_p, CONV_A_K - 1, d_conv), F32),
            jax.ShapeDtypeStruct((nb_s, CONV_A_K - 1, d_conv), F32),
        ],
        scratch_shapes=[
            pltpu.VMEM((3, tn, d), BF16),
            pltpu.VMEM((SUBLANES + tm, tn), F32),
            pltpu.VMEM((SUBLANES + ts, tn), F32),
            pltpu.VMEM((ts, tn), F32),
            pltpu.VMEM((ts, tn), F32),
        ],
        compiler_params=_params(("arbitrary", "arbitrary")),
        name="proj_a",
    )(xnp, xns, w_in, w_in, w_in, conv_w, state_s)


def _proj_raw_kernel(xp_ref, xs_ref, w_hbm, op_ref, os_ref, wbf, wstage, wsem, *, nsplit, width, row0):
    j = pl.program_id(0)
    tn = wbf.shape[0]

    def w_copy(jj, slot):
        rows = pl.ds(pl.multiple_of(row0 + jj * tn, SUBLANES), tn)
        return pltpu.make_async_copy(w_hbm.at[rows], wstage.at[slot], wsem.at[slot])

    def emit(x_ref, o_ref):
        acc = _dot_nt(x_ref[...], wbf[...])
        if nsplit == 0:
            o_ref[...] = acc
        else:
            for s in range(nsplit):
                o_ref[s] = acc[:, s * width:(s + 1) * width]

    @pl.when(pl.program_id(1) == 0)
    def _():
        @pl.when(j == 0)
        def _():
            w_copy(0, 0).start()

        w_copy(j, j % 2).wait()

        @pl.when(j + 1 < pl.num_programs(0))
        def _():
            w_copy(j + 1, (j + 1) % 2).start()

        wbf[...] = wstage[j % 2].astype(BF16)
        emit(xs_ref, os_ref)

    emit(xp_ref, op_ref)


def _proj_raw(xnp, xns, w, col0, ncols, width):
    tp, d = xnp.shape
    ts = xns.shape[0]
    tn = 1024
    tm = _pick(tp, (1024, 512, 256, 128))
    assert col0 % SUBLANES == 0 and ncols % tn == 0
    if width == 0:
        nsplit = 0
        out_specs = [pl.BlockSpec((tm, tn), lambda j, i: (i, j)),
                     pl.BlockSpec((ts, tn), lambda j, i: (0, j))]
        out_shape = [jax.ShapeDtypeStruct((tp, ncols), F32), jax.ShapeDtypeStruct((ts, ncols), F32)]
    else:
        nsplit = tn // width
        out_specs = [pl.BlockSpec((nsplit, tm, width), lambda j, i: (j, i, 0)),
                     pl.BlockSpec((nsplit, ts, width), lambda j, i: (j, 0, 0))]
        out_shape = [jax.ShapeDtypeStruct((ncols // width, tp, width), F32),
                     jax.ShapeDtypeStruct((ncols // width, ts, width), F32)]
    kern = functools.partial(_proj_raw_kernel, nsplit=nsplit, width=width, row0=col0)
    return pl.pallas_call(
        kern,
        grid=(ncols // tn, tp // tm),
        in_specs=[
            pl.BlockSpec((tm, d), lambda j, i: (i, 0)),
            pl.BlockSpec((ts, d), lambda j, i: (0, 0)),
            pl.BlockSpec(memory_space=pl.ANY),
        ],
        out_specs=out_specs,
        out_shape=out_shape,
        scratch_shapes=[pltpu.VMEM((tn, d), BF16), pltpu.VMEM((2, tn, d), F32), pltpu.SemaphoreType.DMA((2,))],
        compiler_params=_params(("arbitrary", "arbitrary")),
        name=f"proj_raw_{col0}",
    )(xnp, xns, w)


def _proj_conv_kernel(xp_ref, xs_ref, w_ref, cw_ref, cb_ref, st_ref, op_ref, os_ref, cmp_ref, cms_ref,
                      wbf, ubuf, sbuf, fix, *, tiles_per_batch, nb_s, l_s, nsplit, width):
    i = pl.program_id(1)
    tm = xp_ref.shape[0]
    ts = xs_ref.shape[0]
    tail = CONV_M_K - 1
    cw = cw_ref[...]
    bias = cb_ref[...]

    def store(o_ref, act):
        for s in range(nsplit):
            o_ref[s] = act[:, s * width:(s + 1) * width]

    @pl.when(i == 0)
    def _():
        wbf[...] = w_ref[...].astype(BF16)
        raw = _dot_nt(xs_ref[...], wbf[...])
        sbuf[pl.ds(0, SUBLANES), :] = jnp.zeros((SUBLANES, raw.shape[1]), F32)
        sbuf[pl.ds(SUBLANES, ts), :] = raw
        fix[...] = jnp.zeros_like(fix)
        for b in range(nb_s):
            for back in range(1, tail + 1):
                for m in range(back):
                    fix[back - 1, pl.ds(b * l_s + m, 1), :] = st_ref[b, pl.ds(tail + m - back, 1), :]
        rmod = lax.broadcasted_iota(jnp.int32, (ts, 1), 0) % l_s
        acc = raw * cw[tail:tail + 1, :]
        for back in range(1, tail + 1):
            tap = jnp.where(rmod < back, fix[back - 1], sbuf[pl.ds(SUBLANES - back, ts), :])
            acc = acc + tap * cw[tail - back:tail - back + 1, :]
        store(os_ref, jax.nn.silu(acc + bias))
        for b in range(nb_s):
            cms_ref[b] = sbuf[pl.ds(SUBLANES + (b + 1) * l_s - tail, tail), :]

    @pl.when(i % tiles_per_batch == 0)
    def _():
        ubuf[pl.ds(0, SUBLANES), :] = jnp.zeros((SUBLANES, ubuf.shape[1]), F32)

    raw = _dot_nt(xp_ref[...], wbf[...])
    ubuf[pl.ds(SUBLANES, tm), :] = raw
    acc = raw * cw[tail:tail + 1, :]
    for back in range(1, tail + 1):
        acc = acc + ubuf[pl.ds(SUBLANES - back, tm), :] * cw[tail - back:tail - back + 1, :]
    store(op_ref, jax.nn.silu(acc + bias))
    ubuf[pl.ds(0, SUBLANES), :] = ubuf[pl.ds(tm, SUBLANES), :]

    @pl.when(i % tiles_per_batch == tiles_per_batch - 1)
    def _():
        cmp_ref[0] = ubuf[pl.ds(SUBLANES + tm - tail, tail), :]


def _proj_conv(xnp, xns, w, conv_w, conv_b, state_s, col0, ch0, ncols, width, n_batch_p, seq_p):
    tp, d = xnp.shape
    ts = xns.shape[0]
    nb_s = state_s.shape[0]
    l_s = ts // nb_s
    tail = CONV_M_K - 1
    tn = 1024
    tm = _pick(seq_p, (1024, 512, 256, 128))
    tpb = seq_p // tm
    assert col0 % tn == 0 and ch0 % tn == 0 and ncols % tn == 0 and l_s >= tail
    jb, cb0 = col0 // tn, ch0 // tn
    nsplit = tn // width
    kern = functools.partial(_proj_conv_kernel, tiles_per_batch=tpb, nb_s=nb_s, l_s=l_s,
                             nsplit=nsplit, width=width)
    return pl.pallas_call(
        kern,
        grid=(ncols // tn, tp // tm),
        in_specs=[
            pl.BlockSpec((tm, d), lambda j, i: (i, 0)),
            pl.BlockSpec((ts, d), lambda j, i: (0, 0)),
            pl.BlockSpec((tn, d), lambda j, i: (j + jb, 0)),
            pl.BlockSpec((CONV_M_K, tn), lambda j, i: (0, j + cb0)),
            pl.BlockSpec((1, tn), lambda j, i: (0, j + cb0)),
            pl.BlockSpec((nb_s, tail, tn), lambda j, i: (0, 0, j + cb0)),
        ],
        out_specs=[
            pl.BlockSpec((nsplit, tm, width), lambda j, i: (j, i, 0)),
            pl.BlockSpec((nsplit, ts, width), lambda j, i: (j, 0, 0)),
            pl.BlockSpec((1, tail, tn), lambda j, i: (i // tpb, 0, j)),
            pl.BlockSpec((nb_s, tail, tn), lambda j, i: (0, 0, j)),
        ],
        out_shape=[
            jax.ShapeDtypeStruct((ncols // width, tp, width), F32),
            jax.ShapeDtypeStruct((ncols // width, ts, width), F32),
            jax.ShapeDtypeStruct((n_batch_p, tail, ncols), F32),
            jax.ShapeDtypeStruct((nb_s, tail, ncols), F32),
        ],
        scratch_shapes=[
            pltpu.VMEM((tn, d), BF16),
            pltpu.VMEM((SUBLANES + tm, tn), F32),
            pltpu.VMEM((SUBLANES + ts, tn), F32),
            pltpu.VMEM((tail, ts, tn), F32),
        ],
        compiler_params=_params(("arbitrary", "arbitrary")),
        name=f"proj_conv_{col0}",
    )(xnp, xns, w, conv_w, conv_b, state_s)


def _ssd_kernel(*refs, q, has_state, nchunks):
    (z_ref, xs_ref, b_ref, c_ref, dt_ref, dtb_ref, alog_ref, dsk_ref, gn_ref, *rest) = refs
    if has_state:
        hprev, *rest = rest
    (y_ref, oh, h_s, acg, rowt) = rest
    c = pl.program_id(1)
    nheads = N_GROUPS * HEADS_PER_GROUP

    @pl.when(c == 0)
    def _init():
        if has_state:
            h_s[...] = hprev[...]
        else:
            h_s[...] = jnp.zeros_like(h_s)

    dt = _softplus(dt_ref[...] + dtb_ref[...])
    da = dt * (-jnp.exp(alog_ref[...]))
    ri = lax.broadcasted_iota(jnp.int32, (q, q), 0)
    ci = lax.broadcasted_iota(jnp.int32, (q, q), 1)
    causal = ri >= ci
    tril = jnp.where(causal, 1.0, 0.0).astype(BF16)
    acum = sum(_dot(tril, p) for p in _split3(da))
    eye = jnp.where(lax.broadcasted_iota(jnp.int32, (nheads, nheads), 0)
                    == lax.broadcasted_iota(jnp.int32, (nheads, nheads), 1), 1.0, 0.0).astype(BF16)
    acum2 = acum * LOG2_E
    rowt[0] = sum(lax.dot_general(eye, p, NT_DIMS, preferred_element_type=F32) for p in _split3(acum2))
    rowt[1] = sum(lax.dot_general(eye, p, NT_DIMS, preferred_element_type=F32) for p in _split3(dt))
    wend = jnp.exp(acum[q - 1:q, :] - acum) * dt
    rowt[2] = sum(lax.dot_general(eye, p, NT_DIMS, preferred_element_type=F32) for p in _split3(wend))
    for g in range(N_GROUPS):
        acg[g] = acum2[:, g * HEADS_PER_GROUP:(g + 1) * HEADS_PER_GROUP]

    lane = lax.broadcasted_iota(jnp.int32, (1, LANES), 1)
    rowi = lax.broadcasted_iota(jnp.int32, (LANES, 1), 0)
    half_w = LANES // 2

    def group_body(g, carry):
        xs = xs_ref[g]
        bb = b_ref[g].astype(BF16)
        ccb16 = c_ref[g].astype(BF16)
        cb_ = lax.dot_general(ccb16, bb, NT_DIMS, preferred_element_type=F32)
        ac8 = acg[g]
        dsk = dsk_ref[g]
        lo_half = lane < half_w
        lo_rows = rowi < half_w
        ys = []
        for pair in range(HEADS_PER_GROUP // 2):
            sl = slice(pair * LANES, (pair + 1) * LANES)
            xp = xs[:, sl]
            hp = h_s[g, pl.ds(pair * LANES, LANES), :]
            ms_, acols, wrows = [], [], []
            for r in (2 * pair, 2 * pair + 1):
                head = g * HEADS_PER_GROUP + r
                acol = jnp.broadcast_to(ac8[:, r:r + 1], (q, LANES))
                arow = rowt[0, pl.ds(head, 1), :]
                drow = rowt[1, pl.ds(head, 1), :]
                decay = jnp.exp2(jnp.where(causal, acol[:, :q] - arow, -jnp.inf))
                ms_.append((cb_ * decay * drow).astype(BF16))
                acols.append(acol)
                wrows.append(jnp.broadcast_to(rowt[2, pl.ds(head, 1), :], (half_w, q)))
            xlo = jnp.where(lo_half, xp, 0.0).astype(BF16)
            xhi = jnp.where(lo_half, 0.0, xp).astype(BF16)
            if q % LANES == 0:
                ydiag = _dot(jnp.concatenate(ms_, axis=1), jnp.concatenate([xlo, xhi], axis=0))
            else:
                ydiag = _dot(ms_[0], xlo) + _dot(ms_[1], xhi)
            ea = jnp.where(lo_half, jnp.exp2(acols[0]), jnp.exp2(acols[1]))
            yoff = ea * lax.dot_general(ccb16, hp.astype(BF16), NT_DIMS, preferred_element_type=F32)
            ys.append(dsk[:, sl] * xp + ydiag + yoff)
            xwt = (xp.T * jnp.concatenate(wrows, axis=0)).astype(BF16)
            dlast = [jnp.broadcast_to(jnp.exp2(a[q - 1:q, :]), (LANES, LANES)) for a in acols]
            h_s[g, pl.ds(pair * LANES, LANES), :] = hp * jnp.where(lo_rows, dlast[0], dlast[1]) + _dot(xwt, bb)
        yg = jnp.concatenate(ys, axis=1)
        hh = yg * jax.nn.silu(z_ref[g])
        ms = jnp.mean(hh * hh, axis=-1, keepdims=True)
        y_ref[g] = (hh * lax.rsqrt(ms + EPS) * gn_ref[g]).astype(BF16)
        return carry

    lax.fori_loop(0, N_GROUPS, group_body, 0, unroll=8)

    @pl.when(c == nchunks - 1)
    def _():
        oh[...] = h_s[...]


def _ssd(z, xc, bc, dt_raw, prm, n_batch, seq, q, h_prev=None):
    nchunks = seq // q
    nheads = N_GROUPS * HEADS_PER_GROUP
    has_state = h_prev is not None
    g8 = N_GROUPS

    def tok(first):
        return lambda b, c: (first, b * nchunks + c, 0)

    def const3(b, c):
        return (0, 0, 0)

    in_specs = [
        pl.BlockSpec((g8, q, GROUP_W), tok(0)),
        pl.BlockSpec((g8, q, GROUP_W), tok(0)),
        pl.BlockSpec((g8, q, D_STATE), tok(0)),
        pl.BlockSpec((g8, q, D_STATE), tok(1)),
        pl.BlockSpec((q, nheads), lambda b, c: (b * nchunks + c, 0)),
        pl.BlockSpec((1, nheads), lambda b, c: (0, 0)),
        pl.BlockSpec((1, nheads), lambda b, c: (0, 0)),
        pl.BlockSpec((g8, 1, GROUP_W), const3),
        pl.BlockSpec((g8, 1, GROUP_W), const3),
    ]
    args = [z, xc, bc, bc, dt_raw, prm["dtb"], prm["alog"], prm["dsk"], prm["gn"]]

    def per_batch(shape):
        return pl.BlockSpec((None,) + shape, lambda b, c: (b,) + (0,) * len(shape))

    if has_state:
        in_specs.append(per_batch((g8, GROUP_W, D_STATE)))
        args.append(h_prev)
    t = n_batch * seq
    out_specs = [
        pl.BlockSpec((g8, q, GROUP_W), lambda b, c: (0, b * nchunks + c, 0)),
        per_batch((g8, GROUP_W, D_STATE)),
    ]
    out_shape = [
        jax.ShapeDtypeStruct((g8, t, GROUP_W), BF16),
        jax.ShapeDtypeStruct((n_batch, g8, GROUP_W, D_STATE), F32),
    ]
    scratch = [
        pltpu.VMEM((g8, GROUP_W, D_STATE), F32),
        pltpu.VMEM((g8, q, HEADS_PER_GROUP), F32),
        pltpu.VMEM((3, nheads, q), F32),
    ]
    kern = functools.partial(_ssd_kernel, q=q, has_state=has_state, nchunks=nchunks)
    return pl.pallas_call(
        kern,
        grid=(n_batch, nchunks),
        in_specs=in_specs,
        out_specs=out_specs,
        out_shape=out_shape,
        scratch_shapes=scratch,
        compiler_params=_params(("arbitrary", "arbitrary")),
        name="ssd_state" if has_state else "ssd",
    )(*args)


def _merge_kernel(ap_ref, as_ref, yp_ref, ys_ref, gap_ref, gmp_ref, gas_ref, gms_ref, wa_ref, wm_ref,
                  op_ref, os_ref, wab, wmb):
    def emit(a_ref, y_ref, ga_ref, gm_ref, o_ref):
        oa = _dot(a_ref[...], wab[...])
        y = jnp.concatenate([y_ref[g] for g in range(N_GROUPS)], axis=1)
        om = _dot(y, wmb[...])
        o_ref[...] = (jax.nn.sigmoid(ga_ref[...]) * oa + jax.nn.sigmoid(gm_ref[...]) * om).astype(BF16)

    @pl.when(pl.program_id(1) == 0)
    def _():
        wab[...] = wa_ref[...].astype(BF16)
        wmb[...] = wm_ref[...].astype(BF16)
        emit(as_ref, ys_ref, gas_ref, gms_ref, os_ref)

    emit(ap_ref, yp_ref, gap_ref, gmp_ref, op_ref)


def _merge(abv_p, abv_s, y_p, y_s, g_p, g_s, w_a_out, w_m_out):
    tp, dc = abv_p.shape
    ts = abv_s.shape[0]
    dm = w_a_out.shape[1]
    tn = 512
    tm = _pick(tp, (512, 256, 128))
    ncol = dm // tn
    return pl.pallas_call(
        _merge_kernel,
        grid=(ncol, tp // tm),
        in_specs=[
            pl.BlockSpec((tm, dc), lambda j, i: (i, 0)),
            pl.BlockSpec((ts, dc), lambda j, i: (0, 0)),
            pl.BlockSpec((N_GROUPS, tm, GROUP_W), lambda j, i: (0, i, 0)),
            pl.BlockSpec((N_GROUPS, ts, GROUP_W), lambda j, i: (0, 0, 0)),
            pl.BlockSpec((tm, tn), lambda j, i: (i, j)),
            pl.BlockSpec((tm, tn), lambda j, i: (i, j + ncol)),
            pl.BlockSpec((ts, tn), lambda j, i: (0, j)),
            pl.BlockSpec((ts, tn), lambda j, i: (0, j + ncol)),
            pl.BlockSpec((dc, tn), lambda j, i: (0, j)),
            pl.BlockSpec((N_GROUPS * GROUP_W, tn), lambda j, i: (0, j)),
        ],
        out_specs=[
            pl.BlockSpec((tm, tn), lambda j, i: (i, j)),
            pl.BlockSpec((ts, tn), lambda j, i: (0, j)),
        ],
        out_shape=[jax.ShapeDtypeStruct((tp, dm), BF16), jax.ShapeDtypeStruct((ts, dm), BF16)],
        scratch_shapes=[pltpu.VMEM((dc, tn), BF16), pltpu.VMEM((N_GROUPS * GROUP_W, tn), BF16)],
        compiler_params=_params(("arbitrary", "arbitrary")),
        name="merge",
    )(abv_p, abv_s, y_p, y_s, g_p, g_p, g_s, g_s, w_a_out, w_m_out)


def _res_kernel(ap_ref, as_ref, rp_ref, rs_ref, w_ref, g_ref, *outs, n_prompt_tiles, merged_out, ts):
    if merged_out:
        h_ref, hn_ref, wbf = outs
    else:
        hp_ref, hs_ref, hnp_ref, hns_ref, wbf = outs
    i = pl.program_id(0)

    @pl.when(i == 0)
    def _():
        wbf[...] = w_ref[...].astype(BF16)

    def emit(a_ref, r_ref, store_h, store_hn):
        h = r_ref[...] + _dot(a_ref[...], wbf[...])
        store_h(h)
        store_hn(_rms(h, g_ref[...]))

    if merged_out:
        @pl.when(i < n_prompt_tiles)
        def _():
            def sh(h):
                h_ref[...] = h

            def shn(hn):
                hn_ref[...] = hn

            emit(ap_ref, rp_ref, sh, shn)

        @pl.when(i == n_prompt_tiles)
        def _():
            def sh(h):
                h_ref[pl.ds(0, ts), :] = h

            def shn(hn):
                hn_ref[pl.ds(0, ts), :] = hn

            emit(as_ref, rs_ref, sh, shn)
    else:
        def shp(h):
            hp_ref[...] = h

        def shnp(hn):
            hnp_ref[...] = hn.astype(hnp_ref.dtype)

        emit(ap_ref, rp_ref, shp, shnp)

        @pl.when(i == 0)
        def _():
            def shs(h):
                hs_ref[...] = h

            def shns(hn):
                hns_ref[...] = hn.astype(hns_ref.dtype)

            emit(as_ref, rs_ref, shs, shns)


def _res(a_p, a_s, r_p, r_s, w, g, merged_out, hn_dtype):
    tp, d = a_p.shape
    ts = a_s.shape[0]
    tm = _pick(tp, (512, 256))
    assert tp % tm == 0 and ts <= tm
    npt = tp // tm
    last = npt - 1
    in_specs = [
        pl.BlockSpec((tm, d), lambda i: (jnp.minimum(i, last), 0)),
        pl.BlockSpec((ts, d), lambda i: (0, 0)),
        pl.BlockSpec((tm, d), lambda i: (jnp.minimum(i, last), 0)),
        pl.BlockSpec((ts, d), lambda i: (0, 0)),
        pl.BlockSpec((d, d), lambda i: (0, 0), pipeline_mode=pl.Buffered(1)),
        pl.BlockSpec((1, d), lambda i: (0, 0)),
    ]
    if merged_out:
        grid = (npt + 1,)
        out_specs = [pl.BlockSpec((tm, d), lambda i: (i, 0)), pl.BlockSpec((tm, d), lambda i: (i, 0))]
        out_shape = [jax.ShapeDtypeStruct((tp + ts, d), F32), jax.ShapeDtypeStruct((tp + ts, d), hn_dtype)]
    else:
        grid = (npt,)
        out_specs = [pl.BlockSpec((tm, d), lambda i: (i, 0)), pl.BlockSpec((ts, d), lambda i: (0, 0)),
                     pl.BlockSpec((tm, d), lambda i: (i, 0)), pl.BlockSpec((ts, d), lambda i: (0, 0))]
        out_shape = [jax.ShapeDtypeStruct((tp, d), F32), jax.ShapeDtypeStruct((ts, d), F32),
                     jax.ShapeDtypeStruct((tp, d), hn_dtype), jax.ShapeDtypeStruct((ts, d), hn_dtype)]
    kern = functools.partial(_res_kernel, n_prompt_tiles=npt, merged_out=merged_out, ts=ts)
    return pl.pallas_call(
        kern,
        grid=grid,
        in_specs=in_specs,
        out_specs=out_specs,
        out_shape=out_shape,
        scratch_shapes=[pltpu.VMEM((d, d), BF16)],
        compiler_params=_params(("arbitrary",)),
        name="res_merged" if merged_out else "res",
    )(a_p, a_s, r_p, r_s, w, g)


def _kv_kernel(m_ref, g_ref, wk_ref, wv_ref, k_hbm, v_hbm, mn, kvb, sem, *, n_batch):
    j = pl.program_id(0)

    @pl.when(j == 0)
    def _():
        mn[...] = _rms(m_ref[...], g_ref[...]).astype(BF16)

    kvb[0] = _dot(mn[...], wk_ref[...].astype(BF16))
    kvb[1] = _dot(mn[...], wv_ref[...].astype(BF16))
    nm = kvb.shape[1] // n_batch
    for h in range(N_XHEADS):
        @pl.when(j == h)
        def _(h=h):
            copies = [pltpu.make_async_copy(kvb.at[t, pl.ds(b * nm, nm)], dst.at[b, :, h, :], sem.at[t, b])
                      for t, dst in enumerate((k_hbm, v_hbm)) for b in range(n_batch)]
            for c in copies:
                c.start()
            for c in copies:
                c.wait()


def _memory_kv(mem2d, g, w_k, w_v, n_batch):
    m, d = mem2d.shape
    dh = d // N_XHEADS
    out_sds = jax.ShapeDtypeStruct((n_batch, m // n_batch, N_XHEADS, dh), F32)
    kern = functools.partial(_kv_kernel, n_batch=n_batch)
    return pl.pallas_call(
        kern,
        grid=(N_XHEADS,),
        in_specs=[
            pl.BlockSpec((m, d), lambda j: (0, 0)),
            pl.BlockSpec((1, d), lambda j: (0, 0)),
            pl.BlockSpec((d, dh), lambda j: (0, j)),
            pl.BlockSpec((d, dh), lambda j: (0, j)),
        ],
        out_specs=[pl.BlockSpec(memory_space=pl.ANY), pl.BlockSpec(memory_space=pl.ANY)],
        out_shape=[out_sds, out_sds],
        scratch_shapes=[pltpu.VMEM((m, d), BF16), pltpu.VMEM((2, m, dh), F32),
                        pltpu.SemaphoreType.DMA((2, n_batch))],
        compiler_params=_params(("arbitrary",)),
        name="memory_kv",
    )(mem2d, g, w_k, w_v)


def _attn_kernel(hn_ref, k_ref, v_ref, wq_ref, o_ref, wqb, kb, vb, kvf, sem, *q_all):
    b = pl.program_id(0)
    i = pl.program_id(1)
    tm = o_ref.shape[0]

    @pl.when((b == 0) & (i == 0))
    def _():
        wqb[...] = wq_ref[...].astype(BF16)
        if q_all:
            q_all[0][...] = _dot(hn_ref[...], wqb[...])

    d = wqb.shape[1]
    dh = d // N_XHEADS

    @pl.when(i == 0)
    def _():
        def copies(bb):
            slot = bb % 2
            return [pltpu.make_async_copy(src.at[bb, :, h, :], kvf.at[slot, t, h], sem.at[slot, t, h])
                    for t, src in enumerate((k_ref, v_ref)) for h in range(N_XHEADS)]

        @pl.when(b == 0)
        def _():
            for c in copies(b):
                c.start()

        for c in copies(b):
            c.wait()

        @pl.when(b + 1 < pl.num_programs(0))
        def _():
            for c in copies(b + 1):
                c.start()

        for h in range(N_XHEADS):
            kb[:, h * dh:(h + 1) * dh] = kvf[b % 2, 0, h].astype(BF16)
            vb[:, h * dh:(h + 1) * dh] = kvf[b % 2, 1, h].astype(BF16)

    if q_all:
        row0 = pl.multiple_of((b * pl.num_programs(1) + i) * tm, SUBLANES)
        q = q_all[0][pl.ds(row0, tm), :]
    else:
        q = _dot(hn_ref[...], wqb[...])
    outs = []
    for h in range(N_XHEADS):
        sl = slice(h * dh, (h + 1) * dh)
        s = lax.dot_general(q[:, sl].astype(BF16), kb[:, sl], NT_DIMS, preferred_element_type=F32)
        s = s * (dh ** -0.5)
        e = jnp.exp(s - jnp.max(s, axis=-1, keepdims=True))
        p = e / jnp.sum(e, axis=-1, keepdims=True)
        outs.append(_dot(p.astype(BF16), vb[:, sl]))
    o_ref[...] = jnp.concatenate(outs, axis=1).astype(BF16)


def _attn(hn, k, v, w_q, n_batch, seq):
    t, d = hn.shape
    nm, nh, dh = k.shape[1:]
    tm = _pick(seq, (512, 256, 128, 64, 32, 16))
    tpb = seq // tm
    kv_spec = pl.BlockSpec(memory_space=pl.ANY)
    scratch = [pltpu.VMEM((d, d), BF16), pltpu.VMEM((nm, d), BF16), pltpu.VMEM((nm, d), BF16),
               pltpu.VMEM((2, 2, nh, nm, dh), F32), pltpu.SemaphoreType.DMA((2, 2, nh))]
    if tm < 128 and t <= 512:
        hn_spec = pl.BlockSpec((t, d), lambda b, i: (0, 0))
        scratch.append(pltpu.VMEM((t, d), F32))
    else:
        hn_spec = pl.BlockSpec((tm, d), lambda b, i: (b * tpb + i, 0))
    return pl.pallas_call(
        _attn_kernel,
        grid=(n_batch, tpb),
        in_specs=[
            hn_spec,
            kv_spec,
            kv_spec,
            pl.BlockSpec((d, d), lambda b, i: (0, 0), pipeline_mode=pl.Buffered(1)),
        ],
        out_specs=pl.BlockSpec((tm, d), lambda b, i: (b * tpb + i, 0)),
        out_shape=jax.ShapeDtypeStruct((t, d), BF16),
        scratch_shapes=scratch,
        compiler_params=_params(("arbitrary", "arbitrary")),
        name=f"attn_{seq}",
    )(hn, k, v, w_q)


def _router_kernel(x_ref, w_ref, b_ref, ri_ref, rw_ref):
    logits = _dot(x_ref[...].astype(BF16), w_ref[...].astype(BF16)) + b_ref[...]
    lane_i = lax.broadcasted_iota(jnp.int32, logits.shape, 1)
    lane = lane_i.astype(F32)
    ninf = -jnp.inf
    big = float(LANES)
    is_g = lane < N_EXPERT_GROUPS
    gl = jnp.where(is_g, logits, ninf)
    gmax = jnp.max(gl, axis=-1, keepdims=True)
    gsel = jnp.min(jnp.where(gl == gmax, lane, big), axis=-1, keepdims=True)
    pg = 1.0 / jnp.sum(jnp.where(is_g, jnp.exp(gl - gmax), 0.0), axis=-1, keepdims=True)
    lo = N_EXPERT_GROUPS + EXPERTS_PER_GROUP * gsel
    el = jnp.where(lane >= lo, jnp.where(lane < lo + EXPERTS_PER_GROUP, logits, ninf), ninf)
    m1 = jnp.max(el, axis=-1, keepdims=True)
    i1 = jnp.min(jnp.where(el == m1, lane, big), axis=-1, keepdims=True)
    el2 = jnp.where(lane == i1, ninf, el)
    m2 = jnp.max(el2, axis=-1, keepdims=True)
    i2 = jnp.min(jnp.where(el2 == m2, lane, big), axis=-1, keepdims=True)
    e = jnp.exp(m2 - m1)
    w1 = pg / (1.0 + e)
    w2 = pg * e / (1.0 + e)
    ri_ref[...] = jnp.where(lane_i == 0, i1 - N_EXPERT_GROUPS,
                            jnp.where(lane_i == 1, i2 - N_EXPERT_GROUPS, 0.0)).astype(jnp.int32)
    rw_ref[...] = jnp.where(lane_i == 0, w1, jnp.where(lane_i == 1, w2, 0.0))


def _router(hn_all, w_r, b_r):
    t, d = hn_all.shape
    tm = _pick(t, (640, 512, 384, 256, 128))
    return pl.pallas_call(
        _router_kernel,
        grid=(t // tm,),
        in_specs=[
            pl.BlockSpec((tm, d), lambda i: (i, 0)),
            pl.BlockSpec((d, LANES), lambda i: (0, 0)),
            pl.BlockSpec((1, LANES), lambda i: (0, 0)),
        ],
        out_specs=[pl.BlockSpec((tm, LANES), lambda i: (i, 0)), pl.BlockSpec((tm, LANES), lambda i: (i, 0))],
        out_shape=[jax.ShapeDtypeStruct((t, LANES), jnp.int32), jax.ShapeDtypeStruct((t, LANES), F32)],
        compiler_params=_params(("arbitrary",)),
        name="router",
    )(hn_all, w_r, b_r)


def _row_copy(src, dst, s, d, sem):
    return pltpu.make_async_copy(src.at[pl.ds(s, 1)], dst.at[pl.ds(d, 1)], sem)


def _dispatch_kernel(dest_ref, pad0_ref, padn_ref, nsub_ref, x_ref, o_ref, zrow, zblk, sem, zsem, csem, bsem, *,
                     chunk, n_blocks):
    base = pl.program_id(0) * chunk

    @pl.when(pl.program_id(0) == 0)
    def _():
        zrow[...] = jnp.zeros_like(zrow)
        zblk[...] = jnp.zeros_like(zblk)

        def pad_plan(e):
            p0 = pad0_ref[e]
            head = jnp.minimum(padn_ref[e], (SUBLANES - p0 % SUBLANES) % SUBLANES)
            return p0, head, (padn_ref[e] - head) // SUBLANES

        def group_copy(row):
            return pltpu.make_async_copy(zrow, o_ref.at[pl.ds(pl.multiple_of(row, SUBLANES), SUBLANES)], csem)

        def pad_start(e, carry):
            p0, head, ngroups = pad_plan(e)

            def start_row(r, c):
                _row_copy(zrow, o_ref, 0, p0 + r, zsem).start()
                return c

            def start_group(j, c):
                group_copy(p0 + head + j * SUBLANES).start()
                return c

            lax.fori_loop(0, head, start_row, 0)
            return lax.fori_loop(0, ngroups, start_group, carry)

        def pad_wait(e, carry):
            _, head, ngroups = pad_plan(e)

            def wait_row(r, c):
                _row_copy(zrow, o_ref, 0, 0, zsem).wait()
                return c

            def wait_group(j, c):
                group_copy(0).wait()
                return c

            lax.fori_loop(0, head, wait_row, 0)
            return lax.fori_loop(0, ngroups, wait_group, carry)

        lax.fori_loop(0, N_EXPERTS, pad_start, 0)

        def blk_copy(b):
            return pltpu.make_async_copy(zblk, o_ref.at[pl.ds(pl.multiple_of(b * EXPERT_SUB, EXPERT_SUB),
                                                             EXPERT_SUB)], bsem)

        def tail_start(b, c):
            blk_copy(b).start()
            return c

        def tail_wait(b, c):
            blk_copy(b).wait()
            return c

        lax.fori_loop(nsub_ref[0], n_blocks, tail_start, 0)
        lax.fori_loop(0, N_EXPERTS, pad_wait, 0)
        lax.fori_loop(nsub_ref[0], n_blocks, tail_wait, 0)

    def issue(r, carry):
        t = base + r
        _row_copy(x_ref, o_ref, r, dest_ref[2 * t], sem).start(priority=0)
        _row_copy(x_ref, o_ref, r, dest_ref[2 * t + 1], sem).start(priority=1)
        return carry

    lax.fori_loop(0, chunk, issue, 0, unroll=8)

    for _ in range(2):
        pltpu.make_async_copy(x_ref, o_ref.at[pl.ds(0, chunk)], sem).wait()


def _dispatch(tables, hn_all, n_blocks):
    dest, pad0, padn, nsub = tables
    t, d = hn_all.shape
    chunk = _pick(t, (640, 512, 384, 256, 128))
    kern = functools.partial(_dispatch_kernel, chunk=chunk, n_blocks=n_blocks)
    return pl.pallas_call(
        kern,
        grid_spec=pltpu.PrefetchScalarGridSpec(
            num_scalar_prefetch=4,
            grid=(t // chunk,),
            in_specs=[pl.BlockSpec((chunk, d), lambda i, *_: (i, 0))],
            out_specs=pl.BlockSpec(memory_space=pl.ANY),
            scratch_shapes=[pltpu.VMEM((SUBLANES, d), F32), pltpu.VMEM((EXPERT_SUB, d), F32),
                            pltpu.SemaphoreType.DMA(()), pltpu.SemaphoreType.DMA(()),
                            pltpu.SemaphoreType.DMA(()), pltpu.SemaphoreType.DMA(())],
        ),
        out_shape=jax.ShapeDtypeStruct((n_blocks * EXPERT_SUB, d), F32),
        compiler_params=_params(("arbitrary",)),
        name="dispatch",
    )(dest, pad0, padn, nsub, hn_all)


def _expert_kernel(bstart_ref, wg_ref, wu_ref, wd_ref, x_hbm, y_hbm, gcache, ucache, dcache, xbuf, ybuf, xb,
                   xsem, ysem):
    e = pl.program_id(0)
    k = pl.program_id(1)
    nk = gcache.shape[1]
    total = bstart_ref[N_EXPERTS]

    def x_copy(b, slot):
        rows = pl.ds(pl.multiple_of(b * EXPERT_SUB, EXPERT_SUB), EXPERT_SUB)
        return pltpu.make_async_copy(x_hbm.at[rows], xbuf.at[slot], xsem.at[slot])

    def y_copy(b, slot):
        rows = pl.ds(pl.multiple_of(b * EXPERT_SUB, EXPERT_SUB), EXPERT_SUB)
        return pltpu.make_async_copy(ybuf.at[slot], y_hbm.at[rows], ysem.at[slot])

    @pl.when(e < N_EXPERTS)
    def _():
        slot = e % 2
        gcache[slot, k] = wg_ref[...].astype(BF16)
        ucache[slot, k] = wu_ref[...].astype(BF16)
        dcache[slot, k] = wd_ref[...].astype(BF16)

    @pl.when((e == 0) & (k == 0) & (total > 0))
    def _():
        x_copy(0, 0).start()

    @pl.when(e >= 1)
    def _():
        owner = e - 1
        wslot = owner % 2
        b0 = bstart_ref[owner]
        n = bstart_ref[owner + 1] - b0

        cpu = 2
        upb = nk // cpu

        def unit(u, carry):
            b = b0 + u // upb
            part = u % upb
            slot = b % 2

            def swiglu(x, p):
                kks = [p * cpu + c for c in range(cpu)]
                wg = jnp.concatenate([gcache[wslot, kk] for kk in kks], axis=1)
                wu = jnp.concatenate([ucache[wslot, kk] for kk in kks], axis=1)
                wd = jnp.concatenate([dcache[wslot, kk] for kk in kks], axis=0)
                hid = (jax.nn.silu(_dot(x, wg)) * _dot(x, wu)).astype(BF16)
                return _dot(hid, wd)

            @pl.when(part == 0)
            def _():
                x_copy(b, slot).wait()

                @pl.when(b + 1 < total)
                def _():
                    x_copy(b + 1, 1 - slot).start()

                x = xbuf[slot].astype(BF16)
                xb[...] = x

                @pl.when(b >= 2)
                def _():
                    y_copy(b - 2, slot).wait()

                ybuf[slot] = swiglu(x, 0)

            for p in range(1, upb):
                @pl.when(part == p)
                def _(p=p):
                    ybuf[slot] += swiglu(xb[...], p)
                    if p == upb - 1:
                        y_copy(b, slot).start()

            return carry

        lax.fori_loop((n * upb * k) // nk, (n * upb * (k + 1)) // nk, unit, 0)

    @pl.when((e == N_EXPERTS) & (k == nk - 1))
    def _():
        @pl.when(total >= 2)
        def _():
            y_copy(total - 2, total % 2).wait()

        @pl.when(total >= 1)
        def _():
            y_copy(total - 1, (total - 1) % 2).wait()


def _experts(bstart, x_sorted, w_gate, w_up, w_down, kchunk):
    nrows, d = x_sorted.shape
    de = w_gate.shape[2]
    nk = de // kchunk
    last = N_EXPERTS - 1

    def widx(e, k):
        return jnp.minimum(e, last), jnp.where(e <= last, k, nk - 1)

    def in_idx(e, k, b):
        ee, kk = widx(e, k)
        return (ee, 0, kk)

    def down_idx(e, k, b):
        ee, kk = widx(e, k)
        return (ee, kk, 0)

    return pl.pallas_call(
        _expert_kernel,
        grid_spec=pltpu.PrefetchScalarGridSpec(
            num_scalar_prefetch=1,
            grid=(N_EXPERTS + 1, nk),
            in_specs=[
                pl.BlockSpec((None, d, kchunk), in_idx),
                pl.BlockSpec((None, d, kchunk), in_idx),
                pl.BlockSpec((None, kchunk, d), down_idx),
                pl.BlockSpec(memory_space=pl.ANY),
            ],
            out_specs=pl.BlockSpec(memory_space=pl.ANY),
            scratch_shapes=[
                pltpu.VMEM((2, nk, d, kchunk), BF16),
                pltpu.VMEM((2, nk, d, kchunk), BF16),
                pltpu.VMEM((2, nk, kchunk, d), BF16),
                pltpu.VMEM((2, EXPERT_SUB, d), F32),
                pltpu.VMEM((2, EXPERT_SUB, d), F32),
                pltpu.VMEM((EXPERT_SUB, d), BF16),
                pltpu.SemaphoreType.DMA((2,)),
                pltpu.SemaphoreType.DMA((2,)),
            ],
        ),
        out_shape=jax.ShapeDtypeStruct((nrows, d), F32),
        input_output_aliases={4: 0},
        compiler_params=_params(("arbitrary", "arbitrary")),
        name="experts",
    )(bstart, w_gate, w_up, w_down, x_sorted)


def _final_kernel(dest_ref, h_ref, rw_ref, ys_ref, g_ref, yp_ref, yss_ref, ya, yb, sem, *,
                  tm, n_prompt_tiles, ts):
    i = pl.program_id(0)

    def gather(tile, n):
        slot = tile % 2

        def issue(r, carry):
            t = tile * tm + r
            _row_copy(ys_ref, ya.at[slot], dest_ref[2 * t], r, sem.at[slot]).start(priority=0)
            _row_copy(ys_ref, yb.at[slot], dest_ref[2 * t + 1], r, sem.at[slot]).start(priority=1)
            return carry

        lax.fori_loop(0, n, issue, 0, unroll=8)

    def emit(n, o_ref):
        slot = i % 2
        for buf in (ya, yb):
            pltpu.make_async_copy(ys_ref.at[pl.ds(0, n)], buf.at[slot, pl.ds(0, n)], sem.at[slot]).wait()
        w = rw_ref[pl.ds(0, n), :]
        h = (h_ref[pl.ds(0, n), :] + w[:, 0:1] * ya[slot, pl.ds(0, n), :]
             + w[:, 1:2] * yb[slot, pl.ds(0, n), :])
        o_ref[...] = _rms(h, g_ref[...])

    @pl.when(i == 0)
    def _():
        gather(i, tm)

    @pl.when(i + 1 < n_prompt_tiles)
    def _():
        gather(i + 1, tm)

    @pl.when(i + 1 == n_prompt_tiles)
    def _():
        gather(i + 1, ts)

    @pl.when(i < n_prompt_tiles)
    def _():
        emit(tm, yp_ref)

    @pl.when(i == n_prompt_tiles)
    def _():
        emit(ts, yss_ref)


def _final(dest, h_all, rw, y_sorted, g, tp, ts):
    t, d = h_all.shape
    tm = 256
    assert tp % tm == 0 and ts <= tm and t == tp + ts
    npt = tp // tm
    kern = functools.partial(_final_kernel, tm=tm, n_prompt_tiles=npt, ts=ts)
    return pl.pallas_call(
        kern,
        grid_spec=pltpu.PrefetchScalarGridSpec(
            num_scalar_prefetch=1,
            grid=(npt + 1,),
            in_specs=[
                pl.BlockSpec((tm, d), lambda i, dref: (i, 0)),
                pl.BlockSpec((tm, LANES), lambda i, dref: (i, 0)),
                pl.BlockSpec(memory_space=pl.ANY),
                pl.BlockSpec((1, d), lambda i, dref: (0, 0)),
            ],
            out_specs=[
                pl.BlockSpec((tm, d), lambda i, dref: (jnp.minimum(i, npt - 1), 0)),
                pl.BlockSpec((ts, d), lambda i, dref: (0, 0)),
            ],
            scratch_shapes=[pltpu.VMEM((2, tm, d), F32), pltpu.VMEM((2, tm, d), F32),
                            pltpu.SemaphoreType.DMA((2,))],
        ),
        out_shape=[jax.ShapeDtypeStruct((tp, d), F32), jax.ShapeDtypeStruct((ts, d), F32)],
        compiler_params=_params(("arbitrary",)),
        name="final",
    )(dest, h_all, rw, y_sorted, g)


def _routing_tables(eid):
    i32 = jnp.int32
    onehot = (eid[:, None] == jnp.arange(N_EXPERTS, dtype=i32)[None, :]).astype(i32)
    csum = jnp.cumsum(onehot, axis=0)
    counts = csum[-1]
    nsub = (counts + EXPERT_SUB - 1) // EXPERT_SUB
    bend = jnp.cumsum(nsub)
    bstart = bend - nsub
    seg = bstart * EXPERT_SUB
    dest = jnp.sum(onehot * (csum - 1 + seg[None, :]), axis=1).astype(i32)
    pad0 = (seg + counts).astype(i32)
    padn = (nsub * EXPERT_SUB - counts).astype(i32)
    bstart_all = jnp.concatenate([bstart, bend[-1:]]).astype(i32)
    return (dest, pad0, padn, bend[-1:].astype(i32)), bstart_all


def _group_major(a, width):
    lead = a.shape[:-1]
    g = a.shape[-1] // width
    return jnp.moveaxis(a.reshape(lead + (g, width)), -2, 0)


def kernel(x_prompt, x_sample, cache_mem_k, cache_mem_v, state_conv_a, state_conv_m, state_ssm, mem_prompt,
           norm_mix, w_in, conv_a_w, w_a_out, conv_m_w, conv_m_b, dt_bias, a_log, d_skip, ssm_norm,
           w_m_out, w_o, norm_cross, norm_mem, w_q, w_k, w_v, w_co, norm_ffn, w_rg, b_rg, w_re, b_re,
           w_gate, w_up, w_down, norm_final):
    depth = w_in.shape[0]
    assert depth == 1, "single-layer step"
    nbp, seq_p, d = x_prompt.shape
    nbs, seq_s, _ = x_sample.shape
    tp, ts = nbp * seq_p, nbs * seq_s
    n_mem = mem_prompt.shape[1]
    d_conv = conv_a_w.shape[2]
    d_inner = w_m_out.shape[1]
    nheads = dt_bias.shape[1]
    bc_w = N_GROUPS * D_STATE
    l = 0

    xp = x_prompt.reshape(tp, d)
    xs = x_sample.reshape(ts, d)
    w_in_t = jnp.swapaxes(w_in[l], 0, 1)
    col_z = 3 * d_conv
    col_x = col_z + d_inner
    col_b = col_x + d_inner
    col_dt = col_b + 2 * bc_w
    col_g = col_dt + nheads
    w_dt = w_in_t[col_dt:col_g]

    mk4, mv4 = _memory_kv(mem_prompt.reshape(nbp * n_mem, d), norm_mem[l][None], w_k[l], w_v[l], nbp)

    xnp, xns, dt_p, dt_s = _norm_dt(xp, xs, norm_mix[l][None], w_dt)
    abv_p, abv_s, ca_p, ca_s = _proj_a(xnp, xns, w_in_t, conv_a_w[l], state_conv_a[l], nbp, seq_p, d_conv)
    z_p, z_s = _proj_raw(xnp, xns, w_in_t, col_z, d_inner, GROUP_W)
    cw, cbias, scm = conv_m_w[l], conv_m_b[l][None], state_conv_m[l]
    xc_p, xc_s, cmx_p, cmx_s = _proj_conv(xnp, xns, w_in_t, cw, cbias, scm, col_x, 0, d_inner, GROUP_W,
                                           nbp, seq_p)
    bc_p, bc_s, cmb_p, cmb_s = _proj_conv(xnp, xns, w_in_t, cw, cbias, scm, col_b, d_inner, 2 * bc_w, D_STATE,
                                           nbp, seq_p)
    g_p, g_s = _proj_raw(xnp, xns, w_in_t, col_g, 2 * d, 0)

    prm = {
        "dtb": dt_bias[l][None],
        "alog": a_log[l][None],
        "dsk": _group_major(jnp.repeat(d_skip[l], HEAD_DIM)[None], GROUP_W),
        "gn": _group_major(ssm_norm[l][None], GROUP_W),
    }
    q_p = _pick(seq_p, (128, 64, 32, 16, 8))
    y_p, h_p = _ssd(z_p, xc_p, bc_p, dt_p, prm, nbp, seq_p, q_p)
    y_s, h_s = _ssd(z_s, xc_s, bc_s, dt_s, prm, nbs, seq_s, seq_s,
                    h_prev=state_ssm[l].reshape(nbs, N_GROUPS, GROUP_W, D_STATE))

    merged_p, merged_s = _merge(abv_p, abv_s, y_p, y_s, g_p, g_s, w_a_out[l],
                                w_m_out[l])
    h1_p, h1_s, hn1_p, hn1_s = _res(merged_p, merged_s, xp, xs, w_o[l], norm_cross[l][None], False, BF16)

    att_p = _attn(hn1_p, mk4, mv4, w_q[l], nbp, seq_p)
    att_s = _attn(hn1_s, cache_mem_k[l], cache_mem_v[l], w_q[l], nbs, seq_s)
    h2_all, hn2_all = _res(att_p, att_s, h1_p, h1_s, w_co[l], norm_ffn[l][None], True, F32)

    npad = LANES - N_EXPERT_GROUPS - N_EXPERTS
    w_r = jnp.concatenate([w_rg[l], w_re[l], jnp.zeros((d, npad), F32)], axis=1)
    b_r = jnp.concatenate([b_rg[l], b_re[l], jnp.zeros((npad,), F32)])[None]
    ri, rw = _router(hn2_all, w_r, b_r)
    tables, bstart = _routing_tables(ri[:, :2].reshape(-1))
    n_blocks = -(-2 * (tp + ts) // EXPERT_SUB) + N_EXPERTS
    x_sorted = _dispatch(tables, hn2_all, n_blocks)
    y_sorted = _experts(bstart, x_sorted, w_gate[l], w_up[l], w_down[l], 256)
    y_prompt, y_sample = _final(tables[0], h2_all, rw, y_sorted, norm_final[None], tp, ts)

    xh = d // N_XHEADS
    return (
        y_prompt.reshape(nbp, seq_p, d),
        y_sample.reshape(nbs, seq_s, d),
        mk4[None],
        mv4[None],
        ca_p[None],
        jnp.concatenate([cmx_p, cmb_p], axis=-1)[None],
        h_p.reshape(1, nbp, nheads, HEAD_DIM, D_STATE),
        ca_s[None],
        jnp.concatenate([cmx_s, cmb_s], axis=-1)[None],
        h_s.reshape(1, nbs, nheads, HEAD_DIM, D_STATE),
    )
```

```python
import functools

import jax
import jax.numpy as jnp
from jax import lax
from jax.experimental import pallas as pl
from jax.experimental.pallas import tpu as pltpu

F32 = jnp.float32
BF16 = jnp.bfloat16
EPS = 1e-6
LOG2_E = 1.4426950408889634

V7X_VMEM_BYTES = 64 * 1024 * 1024
VMEM_LIMIT = V7X_VMEM_BYTES - 8 * 1024 * 1024
LANES = 128
SUBLANES = 8

N_GROUPS = 8
HEADS_PER_GROUP = 8
HEAD_DIM = 64
D_STATE = 128
GROUP_W = HEADS_PER_GROUP * HEAD_DIM
N_XHEADS = 4
N_EXPERTS = 32
N_EXPERT_GROUPS = 4
EXPERTS_PER_GROUP = 8
EXPERT_SUB = 256
CONV_A_K = 3
CONV_M_K = 4

NT_DIMS = (((1,), (1,)), ((), ()))
TN_DIMS = (((0,), (0,)), ((), ()))


def _params(sem):
    return pltpu.CompilerParams(dimension_semantics=sem, vmem_limit_bytes=VMEM_LIMIT)


def _pick(n, cands):
    for c in cands:
        if n % c == 0:
            return c
    raise ValueError(f"no tile for {n} in {cands}")


def _dot(a, b):
    return jnp.dot(a, b, preferred_element_type=F32)


def _dot_nt(a, b):
    return lax.dot_general(a, b, NT_DIMS, preferred_element_type=F32)


def _rms(x, g):
    return x * lax.rsqrt(jnp.mean(x * x, axis=-1, keepdims=True) + EPS) * g


def _split3(x):
    hi = x.astype(BF16)
    r = x - hi.astype(F32)
    mid = r.astype(BF16)
    lo = (r - mid.astype(F32)).astype(BF16)
    return hi, mid, lo


def _softplus(x):
    return jnp.maximum(x, 0.0) + jnp.log1p(jnp.exp(-jnp.abs(x)))


def _norm_dt_kernel(xp_ref, xs_ref, g_ref, wdt_ref, xnp_ref, xns_ref, dtp_ref, dts_ref):
    wdt = wdt_ref[...].astype(BF16)

    def one(x_ref, xn_ref, dt_ref):
        xn = _rms(x_ref[...], g_ref[...]).astype(BF16)
        xn_ref[...] = xn
        dt_ref[...] = _dot_nt(xn, wdt)

    one(xp_ref, xnp_ref, dtp_ref)

    @pl.when(pl.program_id(0) == 0)
    def _():
        one(xs_ref, xns_ref, dts_ref)


def _norm_dt(xp, xs, g, wdt):
    tp, d = xp.shape
    ts = xs.shape[0]
    nh = wdt.shape[0]
    tm = _pick(tp, (512, 256, 128))
    return pl.pallas_call(
        _norm_dt_kernel,
        grid=(tp // tm,),
        in_specs=[
            pl.BlockSpec((tm, d), lambda i: (i, 0)),
            pl.BlockSpec((ts, d), lambda i: (0, 0)),
            pl.BlockSpec((1, d), lambda i: (0, 0)),
            pl.BlockSpec((nh, d), lambda i: (0, 0)),
        ],
        out_specs=[
            pl.BlockSpec((tm, d), lambda i: (i, 0)),
            pl.BlockSpec((ts, d), lambda i: (0, 0)),
            pl.BlockSpec((tm, nh), lambda i: (i, 0)),
            pl.BlockSpec((ts, nh), lambda i: (0, 0)),
        ],
        out_shape=[
            jax.ShapeDtypeStruct((tp, d), BF16),
            jax.ShapeDtypeStruct((ts, d), BF16),
            jax.ShapeDtypeStruct((tp, nh), F32),
            jax.ShapeDtypeStruct((ts, nh), F32),
        ],
        compiler_params=_params(("arbitrary",)),
        name="norm_dt",
    )(xp, xs, g, wdt)


def _proj_a_kernel(xp_ref, xs_ref, wb_ref, wc_ref, wh_ref, cw_ref, st_ref,
                   op_ref, os_ref, cap_ref, cas_ref,
                   wbf, ubuf, sbuf, s1buf, s2buf, *, tiles_per_batch, nb_s, l_s):
    i = pl.program_id(1)
    tm = xp_ref.shape[0]
    ts = xs_ref.shape[0]
    cw = cw_ref[...]

    @pl.when(i == 0)
    def _():
        wbf[0] = wb_ref[...].astype(BF16)
        wbf[1] = wc_ref[...].astype(BF16)
        wbf[2] = wh_ref[...].astype(BF16)
        x = xs_ref[...]
        u = _dot_nt(x, wbf[1]) * _dot_nt(x, wbf[2])
        sbuf[pl.ds(0, SUBLANES), :] = jnp.zeros((SUBLANES, u.shape[1]), F32)
        sbuf[pl.ds(SUBLANES, ts), :] = u
        s1buf[...] = jnp.zeros_like(s1buf)
        s2buf[...] = jnp.zeros_like(s2buf)
        for b in range(nb_s):
            s1buf[pl.ds(b * l_s, 1), :] = st_ref[b, pl.ds(1, 1), :]
            s2buf[pl.ds(b * l_s, 1), :] = st_ref[b, pl.ds(0, 1), :]
            s2buf[pl.ds(b * l_s + 1, 1), :] = st_ref[b, pl.ds(1, 1), :]
        rmod = lax.broadcasted_iota(jnp.int32, (ts, 1), 0) % l_s
        prev1 = jnp.where(rmod == 0, s1buf[...], sbuf[pl.ds(SUBLANES - 1, ts), :])
        prev2 = jnp.where(rmod < 2, s2buf[...], sbuf[pl.ds(SUBLANES - 2, ts), :])
        v = prev2 * cw[0:1, :] + prev1 * cw[1:2, :] + u * cw[2:3, :]
        os_ref[...] = (_dot_nt(x, wbf[0]) * v).astype(BF16)
        for b in range(nb_s):
            cas_ref[b] = sbuf[pl.ds(SUBLANES + (b + 1) * l_s - 2, 2), :]

    @pl.when(i % tiles_per_batch == 0)
    def _():
        ubuf[pl.ds(0, SUBLANES), :] = jnp.zeros((SUBLANES, ubuf.shape[1]), F32)

    x = xp_ref[...]
    u = _dot_nt(x, wbf[1]) * _dot_nt(x, wbf[2])
    ubuf[pl.ds(SUBLANES, tm), :] = u
    v = (ubuf[pl.ds(SUBLANES - 2, tm), :] * cw[0:1, :]
         + ubuf[pl.ds(SUBLANES - 1, tm), :] * cw[1:2, :] + u * cw[2:3, :])
    op_ref[...] = (_dot_nt(x, wbf[0]) * v).astype(BF16)
    ubuf[pl.ds(0, SUBLANES), :] = ubuf[pl.ds(tm, SUBLANES), :]

    @pl.when(i % tiles_per_batch == tiles_per_batch - 1)
    def _():
        cap_ref[0] = ubuf[pl.ds(SUBLANES + tm - 2, 2), :]


def _proj_a(xnp, xns, w_in, conv_w, state_s, n_batch_p, seq_p, d_conv):
    tp, d = xnp.shape
    ts = xns.shape[0]
    nb_s = state_s.shape[0]
    l_s = ts // nb_s
    tn = 512
    tm = _pick(seq_p, (1024, 512, 256, 128))
    tpb = seq_p // tm
    ncol = d_conv // tn
    kern = functools.partial(_proj_a_kernel, tiles_per_batch=tpb, nb_s=nb_s, l_s=l_s)
    return pl.pallas_call(
        kern,
        grid=(ncol, tp // tm),
        in_specs=[
            pl.BlockSpec((tm, d), lambda j, i: (i, 0)),
            pl.BlockSpec((ts, d), lambda j, i: (0, 0)),
            pl.BlockSpec((tn, d), lambda j, i: (j, 0)),
            pl.BlockSpec((tn, d), lambda j, i: (j + ncol, 0)),
            pl.BlockSpec((tn, d), lambda j, i: (j + 2 * ncol, 0)),
            pl.BlockSpec((CONV_A_K, tn), lambda j, i: (0, j)),
            pl.BlockSpec((nb_s, CONV_A_K - 1, tn), lambda j, i: (0, 0, j)),
        ],
        out_specs=[
            pl.BlockSpec((tm, tn), lambda j, i: (i, j)),
            pl.BlockSpec((ts, tn), lambda j, i: (0, j)),
            pl.BlockSpec((1, CONV_A_K - 1, tn), lambda j, i: (i // tpb, 0, j)),
            pl.BlockSpec((nb_s, CONV_A_K - 1, tn), lambda j, i: (0, 0, j)),
        ],
        out_shape=[
            jax.ShapeDtypeStruct((tp, d_conv), BF16),
            jax.ShapeDtypeStruct((ts, d_conv), BF16),
            jax.ShapeDtypeStruct((n_batch_p, CONV_A_K - 1, d_conv), F32),
            jax.ShapeDtypeStruct((nb_s, CONV_A_K - 1, d_conv), F32),
        ],
        scratch_shapes=[
            pltpu.VMEM((3, tn, d), BF16),
            pltpu.VMEM((SUBLANES + tm, tn), F32),
            pltpu.VMEM((SUBLANES + ts, tn), F32),
            pltpu.VMEM((ts, tn), F32),
            pltpu.VMEM((ts, tn), F32),
        ],
        compiler_params=_params(("arbitrary", "arbitrary")),
        name="proj_a",
    )(xnp, xns, w_in, w_in, w_in, conv_w, state_s)


def _proj_raw_kernel(xp_ref, xs_ref, w_hbm, op_ref, os_ref, wbf, wstage, wsem, *, nsplit, width, row0):
    j = pl.program_id(0)
    tn = wbf.shape[0]

    def w_copy(jj, slot):
        rows = pl.ds(pl.multiple_of(row0 + jj * tn, SUBLANES), tn)
        return pltpu.make_async_copy(w_hbm.at[rows], wstage.at[slot], wsem.at[slot])

    def emit(x_ref, o_ref):
        acc = _dot_nt(x_ref[...], wbf[...])
        if nsplit == 0:
            o_ref[...] = acc
        else:
            for s in range(nsplit):
                o_ref[s] = acc[:, s * width:(s + 1) * width]

    @pl.when(pl.program_id(1) == 0)
    def _():
        @pl.when(j == 0)
        def _():
            w_copy(0, 0).start()

        w_copy(j, j % 2).wait()

        @pl.when(j + 1 < pl.num_programs(0))
        def _():
            w_copy(j + 1, (j + 1) % 2).start()

        wbf[...] = wstage[j % 2].astype(BF16)
        emit(xs_ref, os_ref)

    emit(xp_ref, op_ref)


def _proj_raw(xnp, xns, w, col0, ncols, width):
    tp, d = xnp.shape
    ts = xns.shape[0]
    tn = 1024
    tm = _pick(tp, (1024, 512, 256, 128))
    assert col0 % SUBLANES == 0 and ncols % tn == 0
    if width == 0:
        nsplit = 0
        out_specs = [pl.BlockSpec((tm, tn), lambda j, i: (i, j)),
                     pl.BlockSpec((ts, tn), lambda j, i: (0, j))]
        out_shape = [jax.ShapeDtypeStruct((tp, ncols), F32), jax.ShapeDtypeStruct((ts, ncols), F32)]
    else:
        nsplit = tn // width
        out_specs = [pl.BlockSpec((nsplit, tm, width), lambda j, i: (j, i, 0)),
                     pl.BlockSpec((nsplit, ts, width), lambda j, i: (j, 0, 0))]
        out_shape = [jax.ShapeDtypeStruct((ncols // width, tp, width), F32),
                     jax.ShapeDtypeStruct((ncols // width, ts, width), F32)]
    kern = functools.partial(_proj_raw_kernel, nsplit=nsplit, width=width, row0=col0)
    return pl.pallas_call(
        kern,
        grid=(ncols // tn, tp // tm),
        in_specs=[
            pl.BlockSpec((tm, d), lambda j, i: (i, 0)),
            pl.BlockSpec((ts, d), lambda j, i: (0, 0)),
            pl.BlockSpec(memory_space=pl.ANY),
        ],
        out_specs=out_specs,
        out_shape=out_shape,
        scratch_shapes=[pltpu.VMEM((tn, d), BF16), pltpu.VMEM((2, tn, d), F32), pltpu.SemaphoreType.DMA((2,))],
        compiler_params=_params(("arbitrary", "arbitrary")),
        name=f"proj_raw_{col0}",
    )(xnp, xns, w)


def _proj_conv_kernel(xp_ref, xs_ref, w_ref, cw_ref, cb_ref, st_ref, op_ref, os_ref, cmp_ref, cms_ref,
                      wbf, ubuf, sbuf, fix, *, tiles_per_batch, nb_s, l_s, nsplit, width):
    i = pl.program_id(1)
    tm = xp_ref.shape[0]
    ts = xs_ref.shape[0]
    tail = CONV_M_K - 1
    cw = cw_ref[...]
    bias = cb_ref[...]

    def store(o_ref, act):
        for s in range(nsplit):
            o_ref[s] = act[:, s * width:(s + 1) * width]

    @pl.when(i == 0)
    def _():
        wbf[...] = w_ref[...].astype(BF16)
        raw = _dot_nt(xs_ref[...], wbf[...])
        sbuf[pl.ds(0, SUBLANES), :] = jnp.zeros((SUBLANES, raw.shape[1]), F32)
        sbuf[pl.ds(SUBLANES, ts), :] = raw
        fix[...] = jnp.zeros_like(fix)
        for b in range(nb_s):
            for back in range(1, tail + 1):
                for m in range(back):
                    fix[back - 1, pl.ds(b * l_s + m, 1), :] = st_ref[b, pl.ds(tail + m - back, 1), :]
        rmod = lax.broadcasted_iota(jnp.int32, (ts, 1), 0) % l_s
        acc = raw * cw[tail:tail + 1, :]
        for back in range(1, tail + 1):
            tap = jnp.where(rmod < back, fix[back - 1], sbuf[pl.ds(SUBLANES - back, ts), :])
            acc = acc + tap * cw[tail - back:tail - back + 1, :]
        store(os_ref, jax.nn.silu(acc + bias))
        for b in range(nb_s):
            cms_ref[b] = sbuf[pl.ds(SUBLANES + (b + 1) * l_s - tail, tail), :]

    @pl.when(i % tiles_per_batch == 0)
    def _():
        ubuf[pl.ds(0, SUBLANES), :] = jnp.zeros((SUBLANES, ubuf.shape[1]), F32)

    raw = _dot_nt(xp_ref[...], wbf[...])
    ubuf[pl.ds(SUBLANES, tm), :] = raw
    acc = raw * cw[tail:tail + 1, :]
    for back in range(1, tail + 1):
        acc = acc + ubuf[pl.ds(SUBLANES - back, tm), :] * cw[tail - back:tail - back + 1, :]
    store(op_ref, jax.nn.silu(acc + bias))
    ubuf[pl.ds(0, SUBLANES), :] = ubuf[pl.ds(tm, SUBLANES), :]

    @pl.when(i % tiles_per_batch == tiles_per_batch - 1)
    def _():
        cmp_ref[0] = ubuf[pl.ds(SUBLANES + tm - tail, tail), :]


def _proj_conv(xnp, xns, w, conv_w, conv_b, state_s, col0, ch0, ncols, width, n_batch_p, seq_p):
    tp, d = xnp.shape
    ts = xns.shape[0]
    nb_s = state_s.shape[0]
    l_s = ts // nb_s
    tail = CONV_M_K - 1
    tn = 1024
    tm = _pick(seq_p, (1024, 512, 256, 128))
    tpb = seq_p // tm
    assert col0 % tn == 0 and ch0 % tn == 0 and ncols % tn == 0 and l_s >= tail
    jb, cb0 = col0 // tn, ch0 // tn
    nsplit = tn // width
    kern = functools.partial(_proj_conv_kernel, tiles_per_batch=tpb, nb_s=nb_s, l_s=l_s,
                             nsplit=nsplit, width=width)
    return pl.pallas_call(
        kern,
        grid=(ncols // tn, tp // tm),
        in_specs=[
            pl.BlockSpec((tm, d), lambda j, i: (i, 0)),
            pl.BlockSpec((ts, d), lambda j, i: (0, 0)),
            pl.BlockSpec((tn, d), lambda j, i: (j + jb, 0)),
            pl.BlockSpec((CONV_M_K, tn), lambda j, i: (0, j + cb0)),
            pl.BlockSpec((1, tn), lambda j, i: (0, j + cb0)),
            pl.BlockSpec((nb_s, tail, tn), lambda j, i: (0, 0, j + cb0)),
        ],
        out_specs=[
            pl.BlockSpec((nsplit, tm, width), lambda j, i: (j, i, 0)),
            pl.BlockSpec((nsplit, ts, width), lambda j, i: (j, 0, 0)),
            pl.BlockSpec((1, tail, tn), lambda j, i: (i // tpb, 0, j)),
            pl.BlockSpec((nb_s, tail, tn), lambda j, i: (0, 0, j)),
        ],
        out_shape=[
            jax.ShapeDtypeStruct((ncols // width, tp, width), F32),
            jax.ShapeDtypeStruct((ncols // width, ts, width), F32),
            jax.ShapeDtypeStruct((n_batch_p, tail, ncols), F32),
            jax.ShapeDtypeStruct((nb_s, tail, ncols), F32),
        ],
        scratch_shapes=[
            pltpu.VMEM((tn, d), BF16),
            pltpu.VMEM((SUBLANES + tm, tn), F32),
            pltpu.VMEM((SUBLANES + ts, tn), F32),
            pltpu.VMEM((tail, ts, tn), F32),
        ],
        compiler_params=_params(("arbitrary", "arbitrary")),
        name=f"proj_conv_{col0}",
    )(xnp, xns, w, conv_w, conv_b, state_s)


def _ssd_kernel(*refs, q, has_state, nchunks):
    (z_ref, xs_ref, b_ref, c_ref, dt_ref, dtb_ref, alog_ref, dsk_ref, gn_ref, *rest) = refs
    if has_state:
        hprev, *rest = rest
    (y_ref, oh, h_s, acg, rowt) = rest
    c = pl.program_id(1)
    nheads = N_GROUPS * HEADS_PER_GROUP

    @pl.when(c == 0)
    def _init():
        if has_state:
            h_s[...] = hprev[...]
        else:
            h_s[...] = jnp.zeros_like(h_s)

    dt = _softplus(dt_ref[...] + dtb_ref[...])
    da = dt * (-jnp.exp(alog_ref[...]))
    ri = lax.broadcasted_iota(jnp.int32, (q, q), 0)
    ci = lax.broadcasted_iota(jnp.int32, (q, q), 1)
    causal = ri >= ci
    tril = jnp.where(causal, 1.0, 0.0).astype(BF16)
    acum = sum(_dot(tril, p) for p in _split3(da))
    eye = jnp.where(lax.broadcasted_iota(jnp.int32, (nheads, nheads), 0)
                    == lax.broadcasted_iota(jnp.int32, (nheads, nheads), 1), 1.0, 0.0).astype(BF16)
    acum2 = acum * LOG2_E
    rowt[0] = sum(lax.dot_general(eye, p, NT_DIMS, preferred_element_type=F32) for p in _split3(acum2))
    rowt[1] = sum(lax.dot_general(eye, p, NT_DIMS, preferred_element_type=F32) for p in _split3(dt))
    wend = jnp.exp(acum[q - 1:q, :] - acum) * dt
    rowt[2] = sum(lax.dot_general(eye, p, NT_DIMS, preferred_element_type=F32) for p in _split3(wend))
    for g in range(N_GROUPS):
        acg[g] = acum2[:, g * HEADS_PER_GROUP:(g + 1) * HEADS_PER_GROUP]

    lane = lax.broadcasted_iota(jnp.int32, (1, LANES), 1)
    rowi = lax.broadcasted_iota(jnp.int32, (LANES, 1), 0)
    half_w = LANES // 2

    def group_body(g, carry):
        xs = xs_ref[g]
        bb = b_ref[g].astype(BF16)
        ccb16 = c_ref[g].astype(BF16)
        cb_ = lax.dot_general(ccb16, bb, NT_DIMS, preferred_element_type=F32)
        ac8 = acg[g]
        dsk = dsk_ref[g]
        lo_half = lane < half_w
        lo_rows = rowi < half_w
        ys = []
        for pair in range(HEADS_PER_GROUP // 2):
            sl = slice(pair * LANES, (pair + 1) * LANES)
            xp = xs[:, sl]
            hp = h_s[g, pl.ds(pair * LANES, LANES), :]
            ms_, acols, wrows = [], [], []
            for r in (2 * pair, 2 * pair + 1):
                head = g * HEADS_PER_GROUP + r
                acol = jnp.broadcast_to(ac8[:, r:r + 1], (q, LANES))
                arow = rowt[0, pl.ds(head, 1), :]
                drow = rowt[1, pl.ds(head, 1), :]
                decay = jnp.exp2(jnp.where(causal, acol[:, :q] - arow, -jnp.inf))
                ms_.append((cb_ * decay * drow).astype(BF16))
                acols.append(acol)
                wrows.append(jnp.broadcast_to(rowt[2, pl.ds(head, 1), :], (half_w, q)))
            xlo = jnp.where(lo_half, xp, 0.0).astype(BF16)
            xhi = jnp.where(lo_half, 0.0, xp).astype(BF16)
            if q % LANES == 0:
                ydiag = _dot(jnp.concatenate(ms_, axis=1), jnp.concatenate([xlo, xhi], axis=0))
            else:
                ydiag = _dot(ms_[0], xlo) + _dot(ms_[1], xhi)
            ea = jnp.where(lo_half, jnp.exp2(acols[0]), jnp.exp2(acols[1]))
            yoff = ea * lax.dot_general(ccb16, hp.astype(BF16), NT_DIMS, preferred_element_type=F32)
            ys.append(dsk[:, sl] * xp + ydiag + yoff)
            xwt = (xp.T * jnp.concatenate(wrows, axis=0)).astype(BF16)
            dlast = [jnp.broadcast_to(jnp.exp2(a[q - 1:q, :]), (LANES, LANES)) for a in acols]
            h_s[g, pl.ds(pair * LANES, LANES), :] = hp * jnp.where(lo_rows, dlast[0], dlast[1]) + _dot(xwt, bb)
        yg = jnp.concatenate(ys, axis=1)
        hh = yg * jax.nn.silu(z_ref[g])
        ms = jnp.mean(hh * hh, axis=-1, keepdims=True)
        y_ref[g] = (hh * lax.rsqrt(ms + EPS) * gn_ref[g]).astype(BF16)
        return carry

    lax.fori_loop(0, N_GROUPS, group_body, 0, unroll=8)

    @pl.when(c == nchunks - 1)
    def _():
        oh[...] = h_s[...]


def _ssd(z, xc, bc, dt_raw, prm, n_batch, seq, q, h_prev=None):
    nchunks = seq // q
    nheads = N_GROUPS * HEADS_PER_GROUP
    has_state = h_prev is not None
    g8 = N_GROUPS

    def tok(first):
        return lambda b, c: (first, b * nchunks + c, 0)

    def const3(b, c):
        return (0, 0, 0)

    in_specs = [
        pl.BlockSpec((g8, q, GROUP_W), tok(0)),
        pl.BlockSpec((g8, q, GROUP_W), tok(0)),
        pl.BlockSpec((g8, q, D_STATE), tok(0)),
        pl.BlockSpec((g8, q, D_STATE), tok(1)),
        pl.BlockSpec((q, nheads), lambda b, c: (b * nchunks + c, 0)),
        pl.BlockSpec((1, nheads), lambda b, c: (0, 0)),
        pl.BlockSpec((1, nheads), lambda b, c: (0, 0)),
        pl.BlockSpec((g8, 1, GROUP_W), const3),
        pl.BlockSpec((g8, 1, GROUP_W), const3),
    ]
    args = [z, xc, bc, bc, dt_raw, prm["dtb"], prm["alog"], prm["dsk"], prm["gn"]]

    def per_batch(shape):
        return pl.BlockSpec((None,) + shape, lambda b, c: (b,) + (0,) * len(shape))

    if has_state:
        in_specs.append(per_batch((g8, GROUP_W, D_STATE)))
        args.append(h_prev)
    t = n_batch * seq
    out_specs = [
        pl.BlockSpec((g8, q, GROUP_W), lambda b, c: (0, b * nchunks + c, 0)),
        per_batch((g8, GROUP_W, D_STATE)),
    ]
    out_shape = [
        jax.ShapeDtypeStruct((g8, t, GROUP_W), BF16),
        jax.ShapeDtypeStruct((n_batch, g8, GROUP_W, D_STATE), F32),
    ]
    scratch = [
        pltpu.VMEM((g8, GROUP_W, D_STATE), F32),
        pltpu.VMEM((g8, q, HEADS_PER_GROUP), F32),
        pltpu.VMEM((3, nheads, q), F32),
    ]
    kern = functools.partial(_ssd_kernel, q=q, has_state=has_state, nchunks=nchunks)
    return pl.pallas_call(
        kern,
        grid=(n_batch, nchunks),
        in_specs=in_specs,
        out_specs=out_specs,
        out_shape=out_shape,
        scratch_shapes=scratch,
        compiler_params=_params(("arbitrary", "arbitrary")),
        name="ssd_state" if has_state else "ssd",
    )(*args)


def _merge_kernel(ap_ref, as_ref, yp_ref, ys_ref, gap_ref, gmp_ref, gas_ref, gms_ref, wa_ref, wm_ref,
                  op_ref, os_ref, wab, wmb):
    def emit(a_ref, y_ref, ga_ref, gm_ref, o_ref):
        oa = _dot(a_ref[...], wab[...])
        y = jnp.concatenate([y_ref[g] for g in range(N_GROUPS)], axis=1)
        om = _dot(y, wmb[...])
        o_ref[...] = (jax.nn.sigmoid(ga_ref[...]) * oa + jax.nn.sigmoid(gm_ref[...]) * om).astype(BF16)

    @pl.when(pl.program_id(1) == 0)
    def _():
        wab[...] = wa_ref[...].astype(BF16)
        wmb[...] = wm_ref[...].astype(BF16)
        emit(as_ref, ys_ref, gas_ref, gms_ref, os_ref)

    emit(ap_ref, yp_ref, gap_ref, gmp_ref, op_ref)


def _merge(abv_p, abv_s, y_p, y_s, g_p, g_s, w_a_out, w_m_out):
    tp, dc = abv_p.shape
    ts = abv_s.shape[0]
    dm = w_a_out.shape[1]
    tn = 512
    tm = _pick(tp, (512, 256, 128))
    ncol = dm // tn
    return pl.pallas_call(
        _merge_kernel,
        grid=(ncol, tp // tm),
        in_specs=[
            pl.BlockSpec((tm, dc), lambda j, i: (i, 0)),
            pl.BlockSpec((ts, dc), lambda j, i: (0, 0)),
            pl.BlockSpec((N_GROUPS, tm, GROUP_W), lambda j, i: (0, i, 0)),
            pl.BlockSpec((N_GROUPS, ts, GROUP_W), lambda j, i: (0, 0, 0)),
            pl.BlockSpec((tm, tn), lambda j, i: (i, j)),
            pl.BlockSpec((tm, tn), lambda j, i: (i, j + ncol)),
            pl.BlockSpec((ts, tn), lambda j, i: (0, j)),
            pl.BlockSpec((ts, tn), lambda j, i: (0, j + ncol)),
            pl.BlockSpec((dc, tn), lambda j, i: (0, j)),
            pl.BlockSpec((N_GROUPS * GROUP_W, tn), lambda j, i: (0, j)),
        ],
        out_specs=[
            pl.BlockSpec((tm, tn), lambda j, i: (i, j)),
            pl.BlockSpec((ts, tn), lambda j, i: (0, j)),
        ],
        out_shape=[jax.ShapeDtypeStruct((tp, dm), BF16), jax.ShapeDtypeStruct((ts, dm), BF16)],
        scratch_shapes=[pltpu.VMEM((dc, tn), BF16), pltpu.VMEM((N_GROUPS * GROUP_W, tn), BF16)],
        compiler_params=_params(("arbitrary", "arbitrary")),
        name="merge",
    )(abv_p, abv_s, y_p, y_s, g_p, g_p, g_s, g_s, w_a_out, w_m_out)


def _res_kernel(ap_ref, as_ref, rp_ref, rs_ref, w_ref, g_ref, *outs, n_prompt_tiles, merged_out, ts):
    if merged_out:
        h_ref, hn_ref, wbf = outs
    else:
        hp_ref, hs_ref, hnp_ref, hns_ref, wbf = outs
    i = pl.program_id(0)

    @pl.when(i == 0)
    def _():
        wbf[...] = w_ref[...].astype(BF16)

    def emit(a_ref, r_ref, store_h, store_hn):
        h = r_ref[...] + _dot(a_ref[...], wbf[...])
        store_h(h)
        store_hn(_rms(h, g_ref[...]))

    if merged_out:
        @pl.when(i < n_prompt_tiles)
        def _():
            def sh(h):
                h_ref[...] = h

            def shn(hn):
                hn_ref[...] = hn

            emit(ap_ref, rp_ref, sh, shn)

        @pl.when(i == n_prompt_tiles)
        def _():
            def sh(h):
                h_ref[pl.ds(0, ts), :] = h

            def shn(hn):
                hn_ref[pl.ds(0, ts), :] = hn

            emit(as_ref, rs_ref, sh, shn)
    else:
        def shp(h):
            hp_ref[...] = h

        def shnp(hn):
            hnp_ref[...] = hn.astype(hnp_ref.dtype)

        emit(ap_ref, rp_ref, shp, shnp)

        @pl.when(i == 0)
        def _():
            def shs(h):
                hs_ref[...] = h

            def shns(hn):
                hns_ref[...] = hn.astype(hns_ref.dtype)

            emit(as_ref, rs_ref, shs, shns)


def _res(a_p, a_s, r_p, r_s, w, g, merged_out, hn_dtype):
    tp, d = a_p.shape
    ts = a_s.shape[0]
    tm = _pick(tp, (512, 256))
    assert tp % tm == 0 and ts <= tm
    npt = tp // tm
    last = npt - 1
    in_specs = [
        pl.BlockSpec((tm, d), lambda i: (jnp.minimum(i, last), 0)),
        pl.BlockSpec((ts, d), lambda i: (0, 0)),
        pl.BlockSpec((tm, d), lambda i: (jnp.minimum(i, last), 0)),
        pl.BlockSpec((ts, d), lambda i: (0, 0)),
        pl.BlockSpec((d, d), lambda i: (0, 0), pipeline_mode=pl.Buffered(1)),
        pl.BlockSpec((1, d), lambda i: (0, 0)),
    ]
    if merged_out:
        grid = (npt + 1,)
        out_specs = [pl.BlockSpec((tm, d), lambda i: (i, 0)), pl.BlockSpec((tm, d), lambda i: (i, 0))]
        out_shape = [jax.ShapeDtypeStruct((tp + ts, d), F32), jax.ShapeDtypeStruct((tp + ts, d), hn_dtype)]
    else:
        grid = (npt,)
        out_specs = [pl.BlockSpec((tm, d), lambda i: (i, 0)), pl.BlockSpec((ts, d), lambda i: (0, 0)),
                     pl.BlockSpec((tm, d), lambda i: (i, 0)), pl.BlockSpec((ts, d), lambda i: (0, 0))]
        out_shape = [jax.ShapeDtypeStruct((tp, d), F32), jax.ShapeDtypeStruct((ts, d), F32),
                     jax.ShapeDtypeStruct((tp, d), hn_dtype), jax.ShapeDtypeStruct((ts, d), hn_dtype)]
    kern = functools.partial(_res_kernel, n_prompt_tiles=npt, merged_out=merged_out, ts=ts)
    return pl.pallas_call(
        kern,
        grid=grid,
        in_specs=in_specs,
        out_specs=out_specs,
        out_shape=out_shape,
        scratch_shapes=[pltpu.VMEM((d, d), BF16)],
        compiler_params=_params(("arbitrary",)),
        name="res_merged" if merged_out else "res",
    )(a_p, a_s, r_p, r_s, w, g)


def _kv_kernel(m_ref, g_ref, wk_ref, wv_ref, k_hbm, v_hbm, mn, kvb, sem, *, n_batch):
    j = pl.program_id(0)

    @pl.when(j == 0)
    def _():
        mn[...] = _rms(m_ref[...], g_ref[...]).astype(BF16)

    kvb[0] = _dot(mn[...], wk_ref[...].astype(BF16))
    kvb[1] = _dot(mn[...], wv_ref[...].astype(BF16))
    nm = kvb.shape[1] // n_batch
    for h in range(N_XHEADS):
        @pl.when(j == h)
        def _(h=h):
            copies = [pltpu.make_async_copy(kvb.at[t, pl.ds(b * nm, nm)], dst.at[b, :, h, :], sem.at[t, b])
                      for t, dst in enumerate((k_hbm, v_hbm)) for b in range(n_batch)]
            for c in copies:
                c.start()
            for c in copies:
                c.wait()


def _memory_kv(mem2d, g, w_k, w_v, n_batch):
    m, d = mem2d.shape
    dh = d // N_XHEADS
    out_sds = jax.ShapeDtypeStruct((n_batch, m // n_batch, N_XHEADS, dh), F32)
    kern = functools.partial(_kv_kernel, n_batch=n_batch)
    return pl.pallas_call(
        kern,
        grid=(N_XHEADS,),
        in_specs=[
            pl.BlockSpec((m, d), lambda j: (0, 0)),
            pl.BlockSpec((1, d), lambda j: (0, 0)),
            pl.BlockSpec((d, dh), lambda j: (0, j)),
            pl.BlockSpec((d, dh), lambda j: (0, j)),
        ],
        out_specs=[pl.BlockSpec(memory_space=pl.ANY), pl.BlockSpec(memory_space=pl.ANY)],
        out_shape=[out_sds, out_sds],
        scratch_shapes=[pltpu.VMEM((m, d), BF16), pltpu.VMEM((2, m, dh), F32),
                        pltpu.SemaphoreType.DMA((2, n_batch))],
        compiler_params=_params(("arbitrary",)),
        name="memory_kv",
    )(mem2d, g, w_k, w_v)


def _attn_kernel(hn_ref, k_ref, v_ref, wq_ref, o_ref, wqb, kb, vb, kvf, sem, *q_all):
    b = pl.program_id(0)
    i = pl.program_id(1)
    tm = o_ref.shape[0]

    @pl.when((b == 0) & (i == 0))
    def _():
        wqb[...] = wq_ref[...].astype(BF16)
        if q_all:
            q_all[0][...] = _dot(hn_ref[...], wqb[...])

    d = wqb.shape[1]
    dh = d // N_XHEADS

    @pl.when(i == 0)
    def _():
        def copies(bb):
            slot = bb % 2
            return [pltpu.make_async_copy(src.at[bb, :, h, :], kvf.at[slot, t, h], sem.at[slot, t, h])
                    for t, src in enumerate((k_ref, v_ref)) for h in range(N_XHEADS)]

        @pl.when(b == 0)
        def _():
            for c in copies(b):
                c.start()

        for c in copies(b):
            c.wait()

        @pl.when(b + 1 < pl.num_programs(0))
        def _():
            for c in copies(b + 1):
                c.start()

        for h in range(N_XHEADS):
            kb[:, h * dh:(h + 1) * dh] = kvf[b % 2, 0, h].astype(BF16)
            vb[:, h * dh:(h + 1) * dh] = kvf[b % 2, 1, h].astype(BF16)

    if q_all:
        row0 = pl.multiple_of((b * pl.num_programs(1) + i) * tm, SUBLANES)
        q = q_all[0][pl.ds(row0, tm), :]
    else:
        q = _dot(hn_ref[...], wqb[...])
    outs = []
    for h in range(N_XHEADS):
        sl = slice(h * dh, (h + 1) * dh)
        s = lax.dot_general(q[:, sl].astype(BF16), kb[:, sl], NT_DIMS, preferred_element_type=F32)
        s = s * (dh ** -0.5)
        e = jnp.exp(s - jnp.max(s, axis=-1, keepdims=True))
        p = e / jnp.sum(e, axis=-1, keepdims=True)
        outs.append(_dot(p.astype(BF16), vb[:, sl]))
    o_ref[...] = jnp.concatenate(outs, axis=1).astype(BF16)


def _attn(hn, k, v, w_q, n_batch, seq):
    t, d = hn.shape
    nm, nh, dh = k.shape[1:]
    tm = _pick(seq, (512, 256, 128, 64, 32, 16))
    tpb = seq // tm
    kv_spec = pl.BlockSpec(memory_space=pl.ANY)
    scratch = [pltpu.VMEM((d, d), BF16), pltpu.VMEM((nm, d), BF16), pltpu.VMEM((nm, d), BF16),
               pltpu.VMEM((2, 2, nh, nm, dh), F32), pltpu.SemaphoreType.DMA((2, 2, nh))]
    if tm < 128 and t <= 512:
        hn_spec = pl.BlockSpec((t, d), lambda b, i: (0, 0))
        scratch.append(pltpu.VMEM((t, d), F32))
    else:
        hn_spec = pl.BlockSpec((tm, d), lambda b, i: (b * tpb + i, 0))
    return pl.pallas_call(
        _attn_kernel,
        grid=(n_batch, tpb),
        in_specs=[
            hn_spec,
            kv_spec,
            kv_spec,
            pl.BlockSpec((d, d), lambda b, i: (0, 0), pipeline_mode=pl.Buffered(1)),
        ],
        out_specs=pl.BlockSpec((tm, d), lambda b, i: (b * tpb + i, 0)),
        out_shape=jax.ShapeDtypeStruct((t, d), BF16),
        scratch_shapes=scratch,
        compiler_params=_params(("arbitrary", "arbitrary")),
        name=f"attn_{seq}",
    )(hn, k, v, w_q)


def _router_kernel(x_ref, w_ref, b_ref, ri_ref, rw_ref):
    logits = _dot(x_ref[...].astype(BF16), w_ref[...].astype(BF16)) + b_ref[...]
    lane_i = lax.broadcasted_iota(jnp.int32, logits.shape, 1)
    lane = lane_i.astype(F32)
    ninf = -jnp.inf
    big = float(LANES)
    is_g = lane < N_EXPERT_GROUPS
    gl = jnp.where(is_g, logits, ninf)
    gmax = jnp.max(gl, axis=-1, keepdims=True)
    gsel = jnp.min(jnp.where(gl == gmax, lane, big), axis=-1, keepdims=True)
    pg = 1.0 / jnp.sum(jnp.where(is_g, jnp.exp(gl - gmax), 0.0), axis=-1, keepdims=True)
    lo = N_EXPERT_GROUPS + EXPERTS_PER_GROUP * gsel
    el = jnp.where(lane >= lo, jnp.where(lane < lo + EXPERTS_PER_GROUP, logits, ninf), ninf)
    m1 = jnp.max(el, axis=-1, keepdims=True)
    i1 = jnp.min(jnp.where(el == m1, lane, big), axis=-1, keepdims=True)
    el2 = jnp.where(lane == i1, ninf, el)
    m2 = jnp.max(el2, axis=-1, keepdims=True)
    i2 = jnp.min(jnp.where(el2 == m2, lane, big), axis=-1, keepdims=True)
    e = jnp.exp(m2 - m1)
    w1 = pg / (1.0 + e)
    w2 = pg * e / (1.0 + e)
    ri_ref[...] = jnp.where(lane_i == 0, i1 - N_EXPERT_GROUPS,
                            jnp.where(lane_i == 1, i2 - N_EXPERT_GROUPS, 0.0)).astype(jnp.int32)
    rw_ref[...] = jnp.where(lane_i == 0, w1, jnp.where(lane_i == 1, w2, 0.0))


def _router(hn_all, w_r, b_r):
    t, d = hn_all.shape
    tm = _pick(t, (640, 512, 384, 256, 128))
    return pl.pallas_call(
        _router_kernel,
        grid=(t // tm,),
        in_specs=[
            pl.BlockSpec((tm, d), lambda i: (i, 0)),
            pl.BlockSpec((d, LANES), lambda i: (0, 0)),
            pl.BlockSpec((1, LANES), lambda i: (0, 0)),
        ],
        out_specs=[pl.BlockSpec((tm, LANES), lambda i: (i, 0)), pl.BlockSpec((tm, LANES), lambda i: (i, 0))],
        out_shape=[jax.ShapeDtypeStruct((t, LANES), jnp.int32), jax.ShapeDtypeStruct((t, LANES), F32)],
        compiler_params=_params(("arbitrary",)),
        name="router",
    )(hn_all, w_r, b_r)


def _row_copy(src, dst, s, d, sem):
    return pltpu.make_async_copy(src.at[pl.ds(s, 1)], dst.at[pl.ds(d, 1)], sem)


def _dispatch_kernel(dest_ref, pad0_ref, padn_ref, nsub_ref, x_ref, o_ref, zrow, zblk, sem, zsem, csem, bsem, *,
                     chunk, n_blocks):
    base = pl.program_id(0) * chunk

    @pl.when(pl.program_id(0) == 0)
    def _():
        zrow[...] = jnp.zeros_like(zrow)
        zblk[...] = jnp.zeros_like(zblk)

        def pad_plan(e):
            p0 = pad0_ref[e]
            head = jnp.minimum(padn_ref[e], (SUBLANES - p0 % SUBLANES) % SUBLANES)
            return p0, head, (padn_ref[e] - head) // SUBLANES

        def group_copy(row):
            return pltpu.make_async_copy(zrow, o_ref.at[pl.ds(pl.multiple_of(row, SUBLANES), SUBLANES)], csem)

        def pad_start(e, carry):
            p0, head, ngroups = pad_plan(e)

            def start_row(r, c):
                _row_copy(zrow, o_ref, 0, p0 + r, zsem).start()
                return c

            def start_group(j, c):
                group_copy(p0 + head + j * SUBLANES).start()
                return c

            lax.fori_loop(0, head, start_row, 0)
            return lax.fori_loop(0, ngroups, start_group, carry)

        def pad_wait(e, carry):
            _, head, ngroups = pad_plan(e)

            def wait_row(r, c):
                _row_copy(zrow, o_ref, 0, 0, zsem).wait()
                return c

            def wait_group(j, c):
                group_copy(0).wait()
                return c

            lax.fori_loop(0, head, wait_row, 0)
            return lax.fori_loop(0, ngroups, wait_group, carry)

        lax.fori_loop(0, N_EXPERTS, pad_start, 0)

        def blk_copy(b):
            return pltpu.make_async_copy(zblk, o_ref.at[pl.ds(pl.multiple_of(b * EXPERT_SUB, EXPERT_SUB),
                                                             EXPERT_SUB)], bsem)

        def tail_start(b, c):
            blk_copy(b).start()
            return c

        def tail_wait(b, c):
            blk_copy(b).wait()
            return c

        lax.fori_loop(nsub_ref[0], n_blocks, tail_start, 0)
        lax.fori_loop(0, N_EXPERTS, pad_wait, 0)
        lax.fori_loop(nsub_ref[0], n_blocks, tail_wait, 0)

    def issue(r, carry):
        t = base + r
        _row_copy(x_ref, o_ref, r, dest_ref[2 * t], sem).start()
        _row_copy(x_ref, o_ref, r, dest_ref[2 * t + 1], sem).start()
        return carry

    lax.fori_loop(0, chunk, issue, 0, unroll=8)

    for _ in range(2):
        pltpu.make_async_copy(x_ref, o_ref.at[pl.ds(0, chunk)], sem).wait()


def _dispatch(tables, hn_all, n_blocks):
    dest, pad0, padn, nsub = tables
    t, d = hn_all.shape
    chunk = _pick(t, (640, 512, 384, 256, 128))
    kern = functools.partial(_dispatch_kernel, chunk=chunk, n_blocks=n_blocks)
    return pl.pallas_call(
        kern,
        grid_spec=pltpu.PrefetchScalarGridSpec(
            num_scalar_prefetch=4,
            grid=(t // chunk,),
            in_specs=[pl.BlockSpec((chunk, d), lambda i, *_: (i, 0))],
            out_specs=pl.BlockSpec(memory_space=pl.ANY),
            scratch_shapes=[pltpu.VMEM((SUBLANES, d), F32), pltpu.VMEM((EXPERT_SUB, d), F32),
                            pltpu.SemaphoreType.DMA(()), pltpu.SemaphoreType.DMA(()),
                            pltpu.SemaphoreType.DMA(()), pltpu.SemaphoreType.DMA(())],
        ),
        out_shape=jax.ShapeDtypeStruct((n_blocks * EXPERT_SUB, d), F32),
        compiler_params=_params(("arbitrary",)),
        name="dispatch",
    )(dest, pad0, padn, nsub, hn_all)


def _expert_kernel(bstart_ref, wg_ref, wu_ref, wd_ref, x_hbm, y_hbm, gcache, ucache, dcache, xbuf, ybuf, xb,
                   xsem, ysem):
    e = pl.program_id(0)
    k = pl.program_id(1)
    nk = gcache.shape[1]
    total = bstart_ref[N_EXPERTS]

    def x_copy(b, slot):
        rows = pl.ds(pl.multiple_of(b * EXPERT_SUB, EXPERT_SUB), EXPERT_SUB)
        return pltpu.make_async_copy(x_hbm.at[rows], xbuf.at[slot], xsem.at[slot])

    def y_copy(b, slot):
        rows = pl.ds(pl.multiple_of(b * EXPERT_SUB, EXPERT_SUB), EXPERT_SUB)
        return pltpu.make_async_copy(ybuf.at[slot], y_hbm.at[rows], ysem.at[slot])

    @pl.when(e < N_EXPERTS)
    def _():
        slot = e % 2
        gcache[slot, k] = wg_ref[...].astype(BF16)
        ucache[slot, k] = wu_ref[...].astype(BF16)
        dcache[slot, k] = wd_ref[...].astype(BF16)

    @pl.when((e == 0) & (k == 0) & (total > 0))
    def _():
        x_copy(0, 0).start()

    @pl.when(e >= 1)
    def _():
        owner = e - 1
        wslot = owner % 2
        b0 = bstart_ref[owner]
        n = bstart_ref[owner + 1] - b0

        cpu = 2
        upb = nk // cpu

        def unit(u, carry):
            b = b0 + u // upb
            part = u % upb
            slot = b % 2

            def swiglu(x, p):
                kks = [p * cpu + c for c in range(cpu)]
                wg = jnp.concatenate([gcache[wslot, kk] for kk in kks], axis=1)
                wu = jnp.concatenate([ucache[wslot, kk] for kk in kks], axis=1)
                wd = jnp.concatenate([dcache[wslot, kk] for kk in kks], axis=0)
                hid = (jax.nn.silu(_dot(x, wg)) * _dot(x, wu)).astype(BF16)
                return _dot(hid, wd)

            @pl.when(part == 0)
            def _():
                x_copy(b, slot).wait()

                @pl.when(b + 1 < total)
                def _():
                    x_copy(b + 1, 1 - slot).start()

                x = xbuf[slot].astype(BF16)
                xb[...] = x

                @pl.when(b >= 2)
                def _():
                    y_copy(b - 2, slot).wait()

                ybuf[slot] = swiglu(x, 0)

            for p in range(1, upb):
                @pl.when(part == p)
                def _(p=p):
                    ybuf[slot] += swiglu(xb[...], p)
                    if p == upb - 1:
                        y_copy(b, slot).start()

            return carry

        lax.fori_loop((n * upb * k) // nk, (n * upb * (k + 1)) // nk, unit, 0)

    @pl.when((e == N_EXPERTS) & (k == nk - 1))
    def _():
        @pl.when(total >= 2)
        def _():
            y_copy(total - 2, total % 2).wait()

        @pl.when(total >= 1)
        def _():
            y_copy(total - 1, (total - 1) % 2).wait()


def _experts(bstart, x_sorted, w_gate, w_up, w_down, kchunk):
    nrows, d = x_sorted.shape
    de = w_gate.shape[2]
    nk = de // kchunk
    last = N_EXPERTS - 1

    def widx(e, k):
        return jnp.minimum(e, last), jnp.where(e <= last, k, nk - 1)

    def in_idx(e, k, b):
        ee, kk = widx(e, k)
        return (ee, 0, kk)

    def down_idx(e, k, b):
        ee, kk = widx(e, k)
        return (ee, kk, 0)

    return pl.pallas_call(
        _expert_kernel,
        grid_spec=pltpu.PrefetchScalarGridSpec(
            num_scalar_prefetch=1,
            grid=(N_EXPERTS + 1, nk),
            in_specs=[
                pl.BlockSpec((None, d, kchunk), in_idx),
                pl.BlockSpec((None, d, kchunk), in_idx),
                pl.BlockSpec((None, kchunk, d), down_idx),
                pl.BlockSpec(memory_space=pl.ANY),
            ],
            out_specs=pl.BlockSpec(memory_space=pl.ANY),
            scratch_shapes=[
                pltpu.VMEM((2, nk, d, kchunk), BF16),
                pltpu.VMEM((2, nk, d, kchunk), BF16),
                pltpu.VMEM((2, nk, kchunk, d), BF16),
                pltpu.VMEM((2, EXPERT_SUB, d), F32),
                pltpu.VMEM((2, EXPERT_SUB, d), F32),
                pltpu.VMEM((EXPERT_SUB, d), BF16),
                pltpu.SemaphoreType.DMA((2,)),
                pltpu.SemaphoreType.DMA((2,)),
            ],
        ),
        out_shape=jax.ShapeDtypeStruct((nrows, d), F32),
        input_output_aliases={4: 0},
        compiler_params=_params(("arbitrary", "arbitrary")),
        name="experts",
    )(bstart, w_gate, w_up, w_down, x_sorted)


def _final_kernel(dest_ref, h_ref, rw_ref, ys_ref, g_ref, yp_ref, yss_ref, ya, yb, sem, *,
                  tm, n_prompt_tiles, ts):
    i = pl.program_id(0)

    def gather(tile, n):
        slot = tile % 2

        def issue(r, carry):
            t = tile * tm + r
            _row_copy(ys_ref, ya.at[slot], dest_ref[2 * t], r, sem.at[slot]).start()
            _row_copy(ys_ref, yb.at[slot], dest_ref[2 * t + 1], r, sem.at[slot]).start()
            return carry

        lax.fori_loop(0, n, issue, 0, unroll=8)

    def emit(n, o_ref):
        slot = i % 2
        for buf in (ya, yb):
            pltpu.make_async_copy(ys_ref.at[pl.ds(0, n)], buf.at[slot, pl.ds(0, n)], sem.at[slot]).wait()
        w = rw_ref[pl.ds(0, n), :]
        h = (h_ref[pl.ds(0, n), :] + w[:, 0:1] * ya[slot, pl.ds(0, n), :]
             + w[:, 1:2] * yb[slot, pl.ds(0, n), :])
        o_ref[...] = _rms(h, g_ref[...])

    @pl.when(i == 0)
    def _():
        gather(i, tm)

    @pl.when(i + 1 < n_prompt_tiles)
    def _():
        gather(i + 1, tm)

    @pl.when(i + 1 == n_prompt_tiles)
    def _():
        gather(i + 1, ts)

    @pl.when(i < n_prompt_tiles)
    def _():
        emit(tm, yp_ref)

    @pl.when(i == n_prompt_tiles)
    def _():
        emit(ts, yss_ref)


def _final(dest, h_all, rw, y_sorted, g, tp, ts):
    t, d = h_all.shape
    tm = _pick(tp, (512, 256))
    assert tp % tm == 0 and ts <= tm and t == tp + ts
    npt = tp // tm
    kern = functools.partial(_final_kernel, tm=tm, n_prompt_tiles=npt, ts=ts)
    return pl.pallas_call(
        kern,
        grid_spec=pltpu.PrefetchScalarGridSpec(
            num_scalar_prefetch=1,
            grid=(npt + 1,),
            in_specs=[
                pl.BlockSpec((tm, d), lambda i, dref: (i, 0)),
                pl.BlockSpec((tm, LANES), lambda i, dref: (i, 0)),
                pl.BlockSpec(memory_space=pl.ANY),
                pl.BlockSpec((1, d), lambda i, dref: (0, 0)),
            ],
            out_specs=[
                pl.BlockSpec((tm, d), lambda i, dref: (jnp.minimum(i, npt - 1), 0)),
                pl.BlockSpec((ts, d), lambda i, dref: (0, 0)),
            ],
            scratch_shapes=[pltpu.VMEM((2, tm, d), F32), pltpu.VMEM((2, tm, d), F32),
                            pltpu.SemaphoreType.DMA((2,))],
        ),
        out_shape=[jax.ShapeDtypeStruct((tp, d), F32), jax.ShapeDtypeStruct((ts, d), F32)],
        compiler_params=_params(("arbitrary",)),
        name="final",
    )(dest, h_all, rw, y_sorted, g)


def _routing_tables(eid):
    i32 = jnp.int32
    onehot = (eid[:, None] == jnp.arange(N_EXPERTS, dtype=i32)[None, :]).astype(i32)
    csum = jnp.cumsum(onehot, axis=0)
    counts = csum[-1]
    nsub = (counts + EXPERT_SUB - 1) // EXPERT_SUB
    bend = jnp.cumsum(nsub)
    bstart = bend - nsub
    seg = bstart * EXPERT_SUB
    dest = jnp.sum(onehot * (csum - 1 + seg[None, :]), axis=1).astype(i32)
    pad0 = (seg + counts).astype(i32)
    padn = (nsub * EXPERT_SUB - counts).astype(i32)
    bstart_all = jnp.concatenate([bstart, bend[-1:]]).astype(i32)
    return (dest, pad0, padn, bend[-1:].astype(i32)), bstart_all


def _group_major(a, width):
    lead = a.shape[:-1]
    g = a.shape[-1] // width
    return jnp.moveaxis(a.reshape(lead + (g, width)), -2, 0)


def kernel(x_prompt, x_sample, cache_mem_k, cache_mem_v, state_conv_a, state_conv_m, state_ssm, mem_prompt,
           norm_mix, w_in, conv_a_w, w_a_out, conv_m_w, conv_m_b, dt_bias, a_log, d_skip, ssm_norm,
           w_m_out, w_o, norm_cross, norm_mem, w_q, w_k, w_v, w_co, norm_ffn, w_rg, b_rg, w_re, b_re,
           w_gate, w_up, w_down, norm_final):
    depth = w_in.shape[0]
    assert depth == 1, "single-layer step"
    nbp, seq_p, d = x_prompt.shape
    nbs, seq_s, _ = x_sample.shape
    tp, ts = nbp * seq_p, nbs * seq_s
    n_mem = mem_prompt.shape[1]
    d_conv = conv_a_w.shape[2]
    d_inner = w_m_out.shape[1]
    nheads = dt_bias.shape[1]
    bc_w = N_GROUPS * D_STATE
    l = 0

    xp = x_prompt.reshape(tp, d)
    xs = x_sample.reshape(ts, d)
    w_in_t = jnp.swapaxes(w_in[l], 0, 1)
    col_z = 3 * d_conv
    col_x = col_z + d_inner
    col_b = col_x + d_inner
    col_dt = col_b + 2 * bc_w
    col_g = col_dt + nheads
    w_dt = w_in_t[col_dt:col_g]

    mk4, mv4 = _memory_kv(mem_prompt.reshape(nbp * n_mem, d), norm_mem[l][None], w_k[l], w_v[l], nbp)

    xnp, xns, dt_p, dt_s = _norm_dt(xp, xs, norm_mix[l][None], w_dt)
    abv_p, abv_s, ca_p, ca_s = _proj_a(xnp, xns, w_in_t, conv_a_w[l], state_conv_a[l], nbp, seq_p, d_conv)
    z_p, z_s = _proj_raw(xnp, xns, w_in_t, col_z, d_inner, GROUP_W)
    cw, cbias, scm = conv_m_w[l], conv_m_b[l][None], state_conv_m[l]
    xc_p, xc_s, cmx_p, cmx_s = _proj_conv(xnp, xns, w_in_t, cw, cbias, scm, col_x, 0, d_inner, GROUP_W,
                                           nbp, seq_p)
    bc_p, bc_s, cmb_p, cmb_s = _proj_conv(xnp, xns, w_in_t, cw, cbias, scm, col_b, d_inner, 2 * bc_w, D_STATE,
                                           nbp, seq_p)
    g_p, g_s = _proj_raw(xnp, xns, w_in_t, col_g, 2 * d, 0)

    prm = {
        "dtb": dt_bias[l][None],
        "alog": a_log[l][None],
        "dsk": _group_major(jnp.repeat(d_skip[l], HEAD_DIM)[None], GROUP_W),
        "gn": _group_major(ssm_norm[l][None], GROUP_W),
    }
    q_p = _pick(seq_p, (128, 64, 32, 16, 8))
    y_p, h_p = _ssd(z_p, xc_p, bc_p, dt_p, prm, nbp, seq_p, q_p)
    y_s, h_s = _ssd(z_s, xc_s, bc_s, dt_s, prm, nbs, seq_s, seq_s,
                    h_prev=state_ssm[l].reshape(nbs, N_GROUPS, GROUP_W, D_STATE))

    merged_p, merged_s = _merge(abv_p, abv_s, y_p, y_s, g_p, g_s, w_a_out[l],
                                w_m_out[l])
    h1_p, h1_s, hn1_p, hn1_s = _res(merged_p, merged_s, xp, xs, w_o[l], norm_cross[l][None], False, BF16)

    att_p = _attn(hn1_p, mk4, mv4, w_q[l], nbp, seq_p)
    att_s = _attn(hn1_s, cache_mem_k[l], cache_mem_v[l], w_q[l], nbs, seq_s)
    h2_all, hn2_all = _res(att_p, att_s, h1_p, h1_s, w_co[l], norm_ffn[l][None], True, F32)

    npad = LANES - N_EXPERT_GROUPS - N_EXPERTS
    w_r = jnp.concatenate([w_rg[l], w_re[l], jnp.zeros((d, npad), F32)], axis=1)
    b_r = jnp.concatenate([b_rg[l], b_re[l], jnp.zeros((npad,), F32)])[None]
    ri, rw = _router(hn2_all, w_r, b_r)
    tables, bstart = _routing_tables(ri[:, :2].reshape(-1))
    n_blocks = -(-2 * (tp + ts) // EXPERT_SUB) + N_EXPERTS
    x_sorted = _dispatch(tables, hn2_all, n_blocks)
    y_sorted = _experts(bstart, x_sorted, w_gate[l], w_up[l], w_down[l], 256)
    y_prompt, y_sample = _final(tables[0], h2_all, rw, y_sorted, norm_final[None], tp, ts)

    xh = d // N_XHEADS
    return (
        y_prompt.reshape(nbp, seq_p, d),
        y_sample.reshape(nbs, seq_s, d),
        mk4[None],
        mv4[None],
        ca_p[None],
        jnp.concatenate([cmx_p, cmb_p], axis=-1)[None],
        h_p.reshape(1, nbp, nheads, HEAD_DIM, D_STATE),
        ca_s[None],
        jnp.concatenate([cmx_s, cmb_s], axis=-1)[None],
        h_s.reshape(1, nbs, nheads, HEAD_DIM, D_STATE),
    )
```
